```python
import jax, jax.numpy as jnp
from jax import lax
import numpy as np

D_MODEL = 1024
BATCH = 32
SEQ = 2048
DEPTH = 2

CHUNK = 64
EPS = 1e-6
NEG_INF = -1e30

CONV_WIDTH = D_MODEL // 2
CONV_KERNEL = 31
HEAD_DIM = 64
ATTN_HEADS = D_MODEL // 128
ATTN_WIDTH = ATTN_HEADS * HEAD_DIM
LEFT_CHUNKS = 8
BAND = (LEFT_CHUNKS + 1) * CHUNK
KEY_PAD = BAND - CHUNK
MAX_REL = 256
POOL_WINDOWS = (2, 4, 8, 16)
POOL_GROUPS = len(POOL_WINDOWS)
POOL_WIDTH = D_MODEL // 2
POOL_GROUP_DIM = POOL_WIDTH // POOL_GROUPS
N_BRANCH = 3

IN_SPLITS = (CONV_WIDTH, CONV_WIDTH, CONV_WIDTH,
             ATTN_WIDTH, ATTN_WIDTH, ATTN_WIDTH, ATTN_WIDTH,
             POOL_WIDTH, POOL_WIDTH,
             N_BRANCH * D_MODEL)
IN_COLS = sum(IN_SPLITS)

kernel_name = "hybrid_conv_chunkattn_pool_gated_block"


def rms_norm(x, g):
    xf = x.astype(jnp.float32)
    y = xf * lax.rsqrt(jnp.mean(xf * xf, axis=-1, keepdims=True) + EPS)
    return (y * g.astype(jnp.float32)).astype(x.dtype)


def layer_norm(x, g, b):
    xf = x.astype(jnp.float32)
    mu = jnp.mean(xf, axis=-1, keepdims=True)
    xc = xf - mu
    y = xc * lax.rsqrt(jnp.mean(xc * xc, axis=-1, keepdims=True) + EPS)
    return (y * g.astype(jnp.float32) + b.astype(jnp.float32)).astype(x.dtype)


def conv_branch(a, b, gate, dw, dw_b, ln_g, ln_b, w_o):
    u = a * jax.nn.sigmoid(b)
    u = lax.conv_general_dilated(
        u, dw[:, None, :], window_strides=(1,),
        padding=[(CONV_KERNEL - 1, 0)],
        dimension_numbers=('NWC', 'WIO', 'NWC'),
        feature_group_count=CONV_WIDTH) + dw_b
    u = jax.nn.silu(layer_norm(u, ln_g, ln_b))
    return (u * jax.nn.silu(gate)) @ w_o


def chunk_attention_branch(q, k, v, gate, rel_table, w_o):
    B, S, _ = q.shape
    n_chunks = S // CHUNK
    qc = (q * (HEAD_DIM ** -0.5)).reshape(B, n_chunks, CHUNK, ATTN_HEADS, HEAD_DIM)
    qc = qc.transpose(1, 0, 3, 2, 4)
    kh = k.reshape(B, S, ATTN_HEADS, HEAD_DIM).transpose(0, 2, 1, 3)
    vh = v.reshape(B, S, ATTN_HEADS, HEAD_DIM).transpose(0, 2, 1, 3)
    kh = jnp.pad(kh, ((0, 0), (0, 0), (KEY_PAD, 0), (0, 0)))
    vh = jnp.pad(vh, ((0, 0), (0, 0), (KEY_PAD, 0), (0, 0)))
    rel = jnp.arange(CHUNK)[:, None] + KEY_PAD - jnp.arange(BAND)[None, :]
    bias = rel_table[:, jnp.clip(rel, -MAX_REL, MAX_REL) + MAX_REL].astype(jnp.float32)
    key_offsets = jnp.arange(BAND) - KEY_PAD

    def one_chunk(args):
        q_blk, c = args
        start = c * CHUNK
        kb = lax.dynamic_slice_in_dim(kh, start, BAND, axis=2)
        vb = lax.dynamic_slice_in_dim(vh, start, BAND, axis=2)
        s = jnp.einsum('bhqd,bhkd->bhqk', q_blk, kb).astype(jnp.float32) + bias
        valid = (start + key_offsets) >= 0
        s = jnp.where(valid, s, NEG_INF)
        p = jax.nn.softmax(s, axis=-1).astype(vb.dtype)
        return jnp.einsum('bhqk,bhkd->bhqd', p, vb)

    o = lax.map(one_chunk, (qc, jnp.arange(n_chunks)))
    o = o.transpose(1, 0, 3, 2, 4).reshape(B, S, ATTN_WIDTH)
    return (o * jax.nn.silu(gate)) @ w_o


def pool_branch(u, gate, w_grp, b_grp, scale, w_o):
    B, S, _ = u.shape
    uf = u.astype(jnp.float32)
    cs = jnp.pad(jnp.cumsum(uf, axis=1), ((0, 0), (1, 0), (0, 0)))
    t = jnp.arange(S)
    outs = []
    for g, w in enumerate(POOL_WINDOWS):
        sl = slice(g * POOL_GROUP_DIM, (g + 1) * POOL_GROUP_DIM)
        csg = cs[..., sl]
        lower = jnp.concatenate(
            [jnp.zeros((B, w - 1, POOL_GROUP_DIM), jnp.float32), csg[:, :S + 1 - w]], axis=1)
        cnt = jnp.minimum(t + 1, w).astype(jnp.float32)[None, :, None]
        outs.append((csg[:, 1:] - lower) / cnt - uf[..., sl])
    pooled = jnp.stack(outs, axis=2).astype(u.dtype)
    mixed = jnp.einsum('bsgc,gcd->bsgd', pooled, w_grp) + b_grp
    mixed = mixed.reshape(B, S, POOL_WIDTH) * scale
    return (mixed * jax.nn.silu(gate)) @ w_o


def hybrid_layer(x, pre_g, post_g, w_in, conv_dw, conv_dw_b, conv_ln_g, conv_ln_b,
                 w_conv_out, rel_bias, w_attn_out, pool_w, pool_b, pool_scale,
                 w_pool_out, w_out):
    B, S, D = x.shape
    h = rms_norm(x, pre_g)
    z = h @ w_in
    (c_a, c_b, c_gate, q, k, v, a_gate, p_in, p_gate, g_merge) = jnp.split(
        z, np.cumsum(IN_SPLITS)[:-1].tolist(), axis=-1)
    y_conv = conv_branch(c_a, c_b, c_gate, conv_dw, conv_dw_b, conv_ln_g, conv_ln_b, w_conv_out)
    y_attn = chunk_attention_branch(q, k, v, a_gate, rel_bias, w_attn_out)
    y_pool = pool_branch(p_in, p_gate, pool_w, pool_b, pool_scale, w_pool_out)
    gates = jax.nn.sigmoid(g_merge).reshape(B, S, N_BRANCH, D)
    merged = gates[:, :, 0] * y_conv + gates[:, :, 1] * y_attn + gates[:, :, 2] * y_pool
    y = merged @ w_out
    return x + rms_norm(y, post_g)


def _fwd_setup_inputs(seed: int = 0) -> dict:
    key = jax.random.key(seed)
    ks = jax.random.split(key, 20)
    n = lambda k, shape, s: jax.random.normal(k, shape, jnp.float32) * s
    L, D = DEPTH, D_MODEL
    return {
        "x": n(ks[0], (BATCH, SEQ, D), 1.0),
        "pre_norm_g": 1.0 + n(ks[1], (L, D), 0.05),
        "post_norm_g": 1.0 + n(ks[2], (L, D), 0.05),
        "w_in": n(ks[3], (L, D, IN_COLS), D ** -0.5),
        "conv_dw": n(ks[4], (L, CONV_KERNEL, CONV_WIDTH), CONV_KERNEL ** -0.5),
        "conv_dw_b": n(ks[5], (L, CONV_WIDTH), 0.02),
        "conv_ln_g": 1.0 + n(ks[6], (L, CONV_WIDTH), 0.05),
        "conv_ln_b": n(ks[7], (L, CONV_WIDTH), 0.02),
        "w_conv_out": n(ks[8], (L, CONV_WIDTH, D), CONV_WIDTH ** -0.5),
        "rel_bias": n(ks[9], (L, ATTN_HEADS, 2 * MAX_REL + 1), 0.1),
        "w_attn_out": n(ks[10], (L, ATTN_WIDTH, D), ATTN_WIDTH ** -0.5),
        "pool_w": n(ks[11], (L, POOL_GROUPS, POOL_GROUP_DIM, POOL_GROUP_DIM), POOL_GROUP_DIM ** -0.5),
        "pool_b": n(ks[12], (L, POOL_GROUPS, POOL_GROUP_DIM), 0.02),
        "pool_scale": 1.0 + n(ks[13], (L, POOL_WIDTH), 0.1),
        "w_pool_out": n(ks[14], (L, POOL_WIDTH, D), POOL_WIDTH ** -0.5),
        "w_out": n(ks[15], (L, D, D), D ** -0.5),
    }


def _fwd_reference(x, pre_norm_g, post_norm_g, w_in, conv_dw, conv_dw_b, conv_ln_g, conv_ln_b,
              w_conv_out, rel_bias, w_attn_out, pool_w, pool_b, pool_scale, w_pool_out, w_out):
    for l in range(DEPTH):
        x = hybrid_layer(x, pre_norm_g[l], post_norm_g[l], w_in[l], conv_dw[l], conv_dw_b[l],
                         conv_ln_g[l], conv_ln_b[l], w_conv_out[l], rel_bias[l], w_attn_out[l],
                         pool_w[l], pool_b[l], pool_scale[l], w_pool_out[l], w_out[l])
    return x


import jax as _jax
import jax.numpy as _jnp

TWIN_FORMAT = 'train_step'
FWD_PARAMS = ['x', 'pre_norm_g', 'post_norm_g', 'w_in', 'conv_dw', 'conv_dw_b', 'conv_ln_g', 'conv_ln_b', 'w_conv_out', 'rel_bias', 'w_attn_out', 'pool_w', 'pool_b', 'pool_scale', 'w_pool_out', 'w_out']
TWIN_WEIGHTS = ['pre_norm_g', 'post_norm_g', 'w_in', 'conv_dw', 'conv_dw_b', 'conv_ln_g', 'conv_ln_b', 'w_conv_out', 'rel_bias', 'w_attn_out', 'pool_w', 'pool_b', 'pool_scale', 'w_pool_out', 'w_out']
TWIN_DIFF_INPUT = 'x'
TWIN_INPUTS = ['x', 'pre_norm_g', 'post_norm_g', 'w_in', 'conv_dw', 'conv_dw_b', 'conv_ln_g', 'conv_ln_b', 'w_conv_out', 'rel_bias', 'w_attn_out', 'pool_w', 'pool_b', 'pool_scale', 'w_pool_out', 'w_out', 'loss_target', 'm_pre_norm_g', 'm_post_norm_g', 'm_w_in', 'm_conv_dw', 'm_conv_dw_b', 'm_conv_ln_g', 'm_conv_ln_b', 'm_w_conv_out', 'm_rel_bias', 'm_w_attn_out', 'm_pool_w', 'm_pool_b', 'm_pool_scale', 'm_w_pool_out', 'm_w_out', 'v_pre_norm_g', 'v_post_norm_g', 'v_w_in', 'v_conv_dw', 'v_conv_dw_b', 'v_conv_ln_g', 'v_conv_ln_b', 'v_w_conv_out', 'v_rel_bias', 'v_w_attn_out', 'v_pool_w', 'v_pool_b', 'v_pool_scale', 'v_w_pool_out', 'v_w_out']
TWIN_OUTPUTS = ['loss', 'grad_x', 'grad_pre_norm_g', 'grad_post_norm_g', 'grad_w_in', 'grad_conv_dw', 'grad_conv_dw_b', 'grad_conv_ln_g', 'grad_conv_ln_b', 'grad_w_conv_out', 'grad_rel_bias', 'grad_w_attn_out', 'grad_pool_w', 'grad_pool_b', 'grad_pool_scale', 'grad_w_pool_out', 'grad_w_out', 'delta_pre_norm_g', 'delta_post_norm_g', 'delta_w_in', 'delta_conv_dw', 'delta_conv_dw_b', 'delta_conv_ln_g', 'delta_conv_ln_b', 'delta_w_conv_out', 'delta_rel_bias', 'delta_w_attn_out', 'delta_pool_w', 'delta_pool_b', 'delta_pool_scale', 'delta_w_pool_out', 'delta_w_out', 'new_m_pre_norm_g', 'new_m_post_norm_g', 'new_m_w_in', 'new_m_conv_dw', 'new_m_conv_dw_b', 'new_m_conv_ln_g', 'new_m_conv_ln_b', 'new_m_w_conv_out', 'new_m_rel_bias', 'new_m_w_attn_out', 'new_m_pool_w', 'new_m_pool_b', 'new_m_pool_scale', 'new_m_w_pool_out', 'new_m_w_out', 'new_v_pre_norm_g', 'new_v_post_norm_g', 'new_v_w_in', 'new_v_conv_dw', 'new_v_conv_dw_b', 'new_v_conv_ln_g', 'new_v_conv_ln_b', 'new_v_w_conv_out', 'new_v_rel_bias', 'new_v_w_attn_out', 'new_v_pool_w', 'new_v_pool_b', 'new_v_pool_scale', 'new_v_w_pool_out', 'new_v_w_out']
TWIN_LEAF_KINDS = {'loss': 'loss', 'grad_x': 'grad_x', 'grad_pre_norm_g': 'grad_w', 'grad_post_norm_g': 'grad_w', 'grad_w_in': 'grad_w', 'grad_conv_dw': 'grad_w', 'grad_conv_dw_b': 'grad_w', 'grad_conv_ln_g': 'grad_w', 'grad_conv_ln_b': 'grad_w', 'grad_w_conv_out': 'grad_w', 'grad_rel_bias': 'grad_w', 'grad_w_attn_out': 'grad_w', 'grad_pool_w': 'grad_w', 'grad_pool_b': 'grad_w', 'grad_pool_scale': 'grad_w', 'grad_w_pool_out': 'grad_w', 'grad_w_out': 'grad_w', 'delta_pre_norm_g': 'delta_w', 'delta_post_norm_g': 'delta_w', 'delta_w_in': 'delta_w', 'delta_conv_dw': 'delta_w', 'delta_conv_dw_b': 'delta_w', 'delta_conv_ln_g': 'delta_w', 'delta_conv_ln_b': 'delta_w', 'delta_w_conv_out': 'delta_w', 'delta_rel_bias': 'delta_w', 'delta_w_attn_out': 'delta_w', 'delta_pool_w': 'delta_w', 'delta_pool_b': 'delta_w', 'delta_pool_scale': 'delta_w', 'delta_w_pool_out': 'delta_w', 'delta_w_out': 'delta_w', 'new_m_pre_norm_g': 'new_m', 'new_m_post_norm_g': 'new_m', 'new_m_w_in': 'new_m', 'new_m_conv_dw': 'new_m', 'new_m_conv_dw_b': 'new_m', 'new_m_conv_ln_g': 'new_m', 'new_m_conv_ln_b': 'new_m', 'new_m_w_conv_out': 'new_m', 'new_m_rel_bias': 'new_m', 'new_m_w_attn_out': 'new_m', 'new_m_pool_w': 'new_m', 'new_m_pool_b': 'new_m', 'new_m_pool_scale': 'new_m', 'new_m_w_pool_out': 'new_m', 'new_m_w_out': 'new_m', 'new_v_pre_norm_g': 'new_v', 'new_v_post_norm_g': 'new_v', 'new_v_w_in': 'new_v', 'new_v_conv_dw': 'new_v', 'new_v_conv_dw_b': 'new_v', 'new_v_conv_ln_g': 'new_v', 'new_v_conv_ln_b': 'new_v', 'new_v_w_conv_out': 'new_v', 'new_v_rel_bias': 'new_v', 'new_v_w_attn_out': 'new_v', 'new_v_pool_w': 'new_v', 'new_v_pool_b': 'new_v', 'new_v_pool_scale': 'new_v', 'new_v_w_pool_out': 'new_v', 'new_v_w_out': 'new_v'}


def _forward(args):
    return _fwd_reference(*[args[k] for k in FWD_PARAMS])


def _output_shape():
    out = _jax.eval_shape(lambda: _forward(_fwd_setup_inputs(0)))
    return out.shape, out.dtype

N_MICROBATCH = 1
ADAM_LR = 0.001
ADAM_B1 = 0.9
ADAM_B2 = 0.999
ADAM_EPS = 1e-08
ADAM_WD = 0.01
ADAM_STEP = 10
PER_EXAMPLE_BATCH_AXIS = {'x': 0, 'loss_target': 0}
SHARED_INPUTS = []
_WEIGHT_DTYPES = {'pre_norm_g': _jnp.float32, 'post_norm_g': _jnp.float32, 'w_in': _jnp.float32, 'conv_dw': _jnp.float32, 'conv_dw_b': _jnp.float32, 'conv_ln_g': _jnp.float32, 'conv_ln_b': _jnp.float32, 'w_conv_out': _jnp.float32, 'rel_bias': _jnp.float32, 'w_attn_out': _jnp.float32, 'pool_w': _jnp.float32, 'pool_b': _jnp.float32, 'pool_scale': _jnp.float32, 'w_pool_out': _jnp.float32, 'w_out': _jnp.float32}
MOMENT_SCALE = {'pre_norm_g': 9.112725e-01, 'post_norm_g': 6.397581e+01, 'w_in': 3.260620e-01, 'conv_dw': 4.504938e-01, 'conv_dw_b': 1.653812e+00, 'conv_ln_g': 8.230135e-01, 'conv_ln_b': 1.229437e+00, 'w_conv_out': 3.967464e-01, 'rel_bias': 3.671953e-02, 'w_attn_out': 8.136952e-02, 'pool_w': 7.619684e-01, 'pool_b': 2.259443e+00, 'pool_scale': 8.095866e-01, 'w_pool_out': 5.633484e-01, 'w_out': 7.152948e-01}


def _to_microbatches(a, axis):
    t = _jnp.moveaxis(a, axis, 0)
    t = t.reshape((N_MICROBATCH, t.shape[0] // N_MICROBATCH) + t.shape[1:])
    return _jnp.moveaxis(t, 1, axis + 1)


def setup_inputs(seed: int = 0) -> dict:
    inp = _fwd_setup_inputs(seed)
    key = _jax.random.fold_in(_jax.random.key(seed), 7919)
    shape, _ = _output_shape()
    out = dict(inp)
    out["loss_target"] = _jax.random.normal(_jax.random.fold_in(key, 0), shape, _jnp.float32)
    for i, name in enumerate(TWIN_WEIGHTS):
        w = inp[name].astype(_jnp.float32)
        if MOMENT_SCALE is None:
            s = _jnp.sqrt(_jnp.mean(_jnp.square(w)) + 1e-30)
        else:
            s = MOMENT_SCALE[name]
        km, kv = _jax.random.split(_jax.random.fold_in(key, i + 1))
        out[name] = w
        out["m_" + name] = s * _jax.random.normal(km, w.shape, _jnp.float32)
        out["v_" + name] = (s * s) * _jax.random.uniform(kv, w.shape, _jnp.float32, 0.5, 1.5)
    if N_MICROBATCH > 1:
        for name, axis in PER_EXAMPLE_BATCH_AXIS.items():
            out[name] = _to_microbatches(out[name], axis)
    return {'x': out['x'], 'pre_norm_g': out['pre_norm_g'], 'post_norm_g': out['post_norm_g'], 'w_in': out['w_in'], 'conv_dw': out['conv_dw'], 'conv_dw_b': out['conv_dw_b'], 'conv_ln_g': out['conv_ln_g'], 'conv_ln_b': out['conv_ln_b'], 'w_conv_out': out['w_conv_out'], 'rel_bias': out['rel_bias'], 'w_attn_out': out['w_attn_out'], 'pool_w': out['pool_w'], 'pool_b': out['pool_b'], 'pool_scale': out['pool_scale'], 'w_pool_out': out['w_pool_out'], 'w_out': out['w_out'], 'loss_target': out['loss_target'], 'm_pre_norm_g': out['m_pre_norm_g'], 'm_post_norm_g': out['m_post_norm_g'], 'm_w_in': out['m_w_in'], 'm_conv_dw': out['m_conv_dw'], 'm_conv_dw_b': out['m_conv_dw_b'], 'm_conv_ln_g': out['m_conv_ln_g'], 'm_conv_ln_b': out['m_conv_ln_b'], 'm_w_conv_out': out['m_w_conv_out'], 'm_rel_bias': out['m_rel_bias'], 'm_w_attn_out': out['m_w_attn_out'], 'm_pool_w': out['m_pool_w'], 'm_pool_b': out['m_pool_b'], 'm_pool_scale': out['m_pool_scale'], 'm_w_pool_out': out['m_w_pool_out'], 'm_w_out': out['m_w_out'], 'v_pre_norm_g': out['v_pre_norm_g'], 'v_post_norm_g': out['v_post_norm_g'], 'v_w_in': out['v_w_in'], 'v_conv_dw': out['v_conv_dw'], 'v_conv_dw_b': out['v_conv_dw_b'], 'v_conv_ln_g': out['v_conv_ln_g'], 'v_conv_ln_b': out['v_conv_ln_b'], 'v_w_conv_out': out['v_w_conv_out'], 'v_rel_bias': out['v_rel_bias'], 'v_w_attn_out': out['v_w_attn_out'], 'v_pool_w': out['v_pool_w'], 'v_pool_b': out['v_pool_b'], 'v_pool_scale': out['v_pool_scale'], 'v_w_pool_out': out['v_w_pool_out'], 'v_w_out': out['v_w_out']}


def _loss(weights, diff, rest, loss_target):
    with _jax.named_scope("forward"):
        args = {**rest, TWIN_DIFF_INPUT: diff, **{k: w.astype(_WEIGHT_DTYPES[k]) for k, w in weights.items()}}
        y = _forward(args)
    with _jax.named_scope("loss_head"):
        err = _jnp.square(y.astype(_jnp.float32) - loss_target)
        return 0.5 * _jnp.sum(_jnp.mean(err, axis=-1)) if err.ndim else 0.5 * err


def _adamw(w, g, m, v):
    m = ADAM_B1 * m + (1.0 - ADAM_B1) * g
    v = ADAM_B2 * v + (1.0 - ADAM_B2) * _jnp.square(g)
    m_hat = m / (1.0 - ADAM_B1 ** ADAM_STEP)
    v_hat = v / (1.0 - ADAM_B2 ** ADAM_STEP)
    delta = -ADAM_LR * (m_hat / (_jnp.sqrt(v_hat) + ADAM_EPS) + ADAM_WD * w)
    return delta, m, v


def reference(x, pre_norm_g, post_norm_g, w_in, conv_dw, conv_dw_b, conv_ln_g, conv_ln_b, w_conv_out, rel_bias, w_attn_out, pool_w, pool_b, pool_scale, w_pool_out, w_out, loss_target, m_pre_norm_g, m_post_norm_g, m_w_in, m_conv_dw, m_conv_dw_b, m_conv_ln_g, m_conv_ln_b, m_w_conv_out, m_rel_bias, m_w_attn_out, m_pool_w, m_pool_b, m_pool_scale, m_w_pool_out, m_w_out, v_pre_norm_g, v_post_norm_g, v_w_in, v_conv_dw, v_conv_dw_b, v_conv_ln_g, v_conv_ln_b, v_w_conv_out, v_rel_bias, v_w_attn_out, v_pool_w, v_pool_b, v_pool_scale, v_w_pool_out, v_w_out):
    given = dict(x=x, pre_norm_g=pre_norm_g, post_norm_g=post_norm_g, w_in=w_in, conv_dw=conv_dw, conv_dw_b=conv_dw_b, conv_ln_g=conv_ln_g, conv_ln_b=conv_ln_b, w_conv_out=w_conv_out, rel_bias=rel_bias, w_attn_out=w_attn_out, pool_w=pool_w, pool_b=pool_b, pool_scale=pool_scale, w_pool_out=w_pool_out, w_out=w_out, loss_target=loss_target, m_pre_norm_g=m_pre_norm_g, m_post_norm_g=m_post_norm_g, m_w_in=m_w_in, m_conv_dw=m_conv_dw, m_conv_dw_b=m_conv_dw_b, m_conv_ln_g=m_conv_ln_g, m_conv_ln_b=m_conv_ln_b, m_w_conv_out=m_w_conv_out, m_rel_bias=m_rel_bias, m_w_attn_out=m_w_attn_out, m_pool_w=m_pool_w, m_pool_b=m_pool_b, m_pool_scale=m_pool_scale, m_w_pool_out=m_w_pool_out, m_w_out=m_w_out, v_pre_norm_g=v_pre_norm_g, v_post_norm_g=v_post_norm_g, v_w_in=v_w_in, v_conv_dw=v_conv_dw, v_conv_dw_b=v_conv_dw_b, v_conv_ln_g=v_conv_ln_g, v_conv_ln_b=v_conv_ln_b, v_w_conv_out=v_w_conv_out, v_rel_bias=v_rel_bias, v_w_attn_out=v_w_attn_out, v_pool_w=v_pool_w, v_pool_b=v_pool_b, v_pool_scale=v_pool_scale, v_w_pool_out=v_w_pool_out, v_w_out=v_w_out)
    weights = {n: given[n] for n in TWIN_WEIGHTS}
    shared = {n: given[n] for n in SHARED_INPUTS}
    per_example = {n: given[n] for n in ['x']}
    grad_fn = _jax.value_and_grad(_loss, argnums=(0, 1))

    def one_microbatch(ex, loss_target):
        ex = dict(ex)
        diff = ex.pop(TWIN_DIFF_INPUT)
        return grad_fn(weights, diff, {**shared, **ex}, loss_target)

    if N_MICROBATCH == 1:
        loss, (grad_w, grad_x) = one_microbatch(per_example, given["loss_target"])
    else:
        def body(carry, xs):
            loss_sum, grad_sum = carry
            l_k, (gw_k, gx_k) = one_microbatch(xs[0], xs[1])
            with _jax.named_scope("update"):
                return (loss_sum + l_k, _jax.tree.map(_jnp.add, grad_sum, gw_k)), gx_k

        init = (_jnp.zeros((), _jnp.float32), _jax.tree.map(_jnp.zeros_like, weights))
        (loss, grad_w), grad_x = _jax.lax.scan(body, init, (per_example, given["loss_target"]))
    with _jax.named_scope("update"):
        delta_w, new_m, new_v = {}, {}, {}
        for n in TWIN_WEIGHTS:
            delta_w[n], new_m[n], new_v[n] = _adamw(weights[n], grad_w[n], given["m_" + n], given["v_" + n])
    return (loss, grad_x, *[grad_w[n] for n in TWIN_WEIGHTS], *[delta_w[n] for n in TWIN_WEIGHTS],
            *[new_m[n] for n in TWIN_WEIGHTS], *[new_v[n] for n in TWIN_WEIGHTS])
```

```python
import numpy as np
import jax
import jax.numpy as jnp
from jax import lax
from jax.experimental import pallas as pl
from jax.experimental.pallas import tpu as pltpu

f32 = jnp.float32
bf16 = jnp.bfloat16

D = 1024
DEPTH = 2
WC = 512
HEAD_DIM = 64
CHUNK = 64
LEFT_CHUNKS = 8
KEY_PAD = LEFT_CHUNKS * CHUNK
MAX_REL = 256
CONV_K = 31
POOL_WINDOWS = (2, 4, 8, 16)
GD = 128
NCOL = 7680
EPS = 1e-6
NEG_INF = -1e30
COL_A, COL_B, COL_CG, COL_Q, COL_K, COL_V, COL_AG, COL_PI, COL_PG, COL_GM = (
    0, 512, 1024, 1536, 2048, 2560, 3072, 3584, 4096, 4608)

ADAM_LR = 0.001
ADAM_B1 = 0.9
ADAM_B2 = 0.999
ADAM_EPS = 1e-08
ADAM_WD = 0.01
ADAM_STEP = 10

QG = 256
KW = KEY_PAD + QG
CT = 128
HALO = 32
PHALO = 16
N_CHIPS = 4
N_DEV = 8
VMEM_LIMIT = 56 * 1024 * 1024
MESH = pl.DeviceIdType.MESH
ANY = pl.BlockSpec(memory_space=pl.ANY)

DZ_BLOCKS = 18
DZ_CONV, DZ_ATTN, DZ_POOL, DZ_GM = 0, 4, 8, 12


def _dz_block(c):
    return c + (c >= 3).astype(jnp.int32) + 2 * (c >= 9).astype(jnp.int32)


def _params(sem=None):
    return pltpu.CompilerParams(dimension_semantics=sem, vmem_limit_bytes=VMEM_LIMIT)


def _sig(x):
    return 1.0 / (1.0 + jnp.exp(-x))


def _dsilu(x, s):
    return s * (1.0 + x * (1.0 - s))


def _colsum(x):
    return jnp.sum(x, axis=0, keepdims=True)


def _rms_pre(x2, g):
    T = x2.shape[0]
    tm = 512

    def body(x_ref, g_ref, h_ref):
        x = x_ref[...]
        r = lax.rsqrt(jnp.mean(x * x, axis=-1, keepdims=True) + EPS)
        h_ref[...] = ((x * r) * g_ref[...]).astype(bf16)

    row = pl.BlockSpec((tm, D), lambda i: (i, 0))
    vec = pl.BlockSpec((1, D), lambda i: (0, 0))
    return pl.pallas_call(
        body, grid=(T // tm,), in_specs=[row, vec], out_specs=row,
        out_shape=jax.ShapeDtypeStruct((T, D), bf16), name="rms_pre",
        compiler_params=_params(("parallel",)))(x2, g)


def _post_fwd(y, x2, g):
    T = x2.shape[0]
    tm = 512

    def body(y_ref, x_ref, g_ref, o_ref):
        y = y_ref[...]
        r = lax.rsqrt(jnp.mean(y * y, axis=-1, keepdims=True) + EPS)
        o_ref[...] = x_ref[...] + (y * r) * g_ref[...]

    row = pl.BlockSpec((tm, D), lambda i: (i, 0))
    vec = pl.BlockSpec((1, D), lambda i: (0, 0))
    return pl.pallas_call(
        body, grid=(T // tm,), in_specs=[row, row, vec], out_specs=row,
        out_shape=jax.ShapeDtypeStruct((T, D), f32), name="post_fwd",
        compiler_params=_params(("parallel",)))(y, x2, g)


def _loss_head(out, tgt):
    T = out.shape[0]
    tm = 512

    def body(o_ref, t_ref, d_ref, l_ref):
        e = o_ref[...] - t_ref[...]
        d_ref[...] = e / float(D)

        @pl.when(pl.program_id(0) == 0)
        def _():
            l_ref[...] = jnp.zeros_like(l_ref)

        l_ref[...] += _colsum(e * e)

    row = pl.BlockSpec((tm, D), lambda i: (i, 0))
    vec = pl.BlockSpec((1, D), lambda i: (0, 0))
    return pl.pallas_call(
        body, grid=(T // tm,), in_specs=[row, row], out_specs=[row, vec],
        out_shape=[jax.ShapeDtypeStruct((T, D), f32), jax.ShapeDtypeStruct((1, D), f32)],
        name="loss_head", compiler_params=_params(("arbitrary",)))(out, tgt)


def _post_bwd(dout, y, g):
    T = y.shape[0]
    tm = 512

    def body(d_ref, y_ref, g_ref, dy_ref, dg_ref):
        y = y_ref[...]
        d = d_ref[...]
        r = lax.rsqrt(jnp.mean(y * y, axis=-1, keepdims=True) + EPS)
        yn = y * r
        dyn = d * g_ref[...]
        dy = r * (dyn - yn * jnp.mean(dyn * yn, axis=-1, keepdims=True))
        dy_ref[...] = dy.astype(bf16)

        @pl.when(pl.program_id(0) == 0)
        def _():
            dg_ref[...] = jnp.zeros_like(dg_ref)

        dg_ref[...] += _colsum(d * yn)

    row = pl.BlockSpec((tm, D), lambda i: (i, 0))
    vec = pl.BlockSpec((1, D), lambda i: (0, 0))
    return pl.pallas_call(
        body, grid=(T // tm,), in_specs=[row, row, vec], out_specs=[row, vec],
        out_shape=[jax.ShapeDtypeStruct((T, D), bf16), jax.ShapeDtypeStruct((1, D), f32)],
        name="post_bwd", compiler_params=_params(("arbitrary",)))(dout, y, g)


def _pre_bwd(dh, x2, g, dout):
    T = x2.shape[0]
    tm = 512

    def body(dh_ref, x_ref, g_ref, d_ref, dx_ref, dg_ref):
        x = x_ref[...]
        dh_ = dh_ref[...]
        r = lax.rsqrt(jnp.mean(x * x, axis=-1, keepdims=True) + EPS)
        xn = x * r
        dxn = dh_ * g_ref[...]
        dx_ref[...] = r * (dxn - xn * jnp.mean(dxn * xn, axis=-1, keepdims=True)) + d_ref[...]

        @pl.when(pl.program_id(0) == 0)
        def _():
            dg_ref[...] = jnp.zeros_like(dg_ref)

        dg_ref[...] += _colsum(dh_ * xn)

    row = pl.BlockSpec((tm, D), lambda i: (i, 0))
    vec = pl.BlockSpec((1, D), lambda i: (0, 0))
    return pl.pallas_call(
        body, grid=(T // tm,), in_specs=[row, row, vec, row], out_specs=[row, vec],
        out_shape=[jax.ShapeDtypeStruct((T, D), f32), jax.ShapeDtypeStruct((1, D), f32)],
        name="pre_bwd", compiler_params=_params(("arbitrary",)))(dh, x2, g, dout)


def _mat_spec(arr, tr, tc, rc, layer, is_dz):
    if is_dz:
        def dz_index(i, j, k):
            r, c = rc(i, j, k)
            return (_dz_block(c), r, 0)
        return pl.BlockSpec((None, tr, tc), dz_index)
    if arr.ndim == 2:
        return pl.BlockSpec((tr, tc), rc)

    def index(i, j, k):
        r, c = rc(i, j, k)
        return (layer, r, c)

    return pl.BlockSpec((None, tr, tc), index)


def _mm(name, a, b, mode, m, n, k, tm, tn, tk, out_dtype, layer=None, a_dz=False, b_dz=False):
    nk = k // tk
    assert m % tm == 0 and n % tn == 0 and k % tk == 0
    if mode == "nn":
        a_spec = _mat_spec(a, tm, tk, lambda i, j, kk: (i, kk), layer, a_dz)
        b_spec = _mat_spec(b, tk, tn, lambda i, j, kk: (kk, j), layer, b_dz)
        dn = (((1,), (0,)), ((), ()))
    elif mode == "nt":
        a_spec = _mat_spec(a, tm, tk, lambda i, j, kk: (i, kk), layer, a_dz)
        b_spec = _mat_spec(b, tn, tk, lambda i, j, kk: (j, kk), layer, b_dz)
        dn = (((1,), (1,)), ((), ()))
    else:
        a_spec = _mat_spec(a, tk, tm, lambda i, j, kk: (kk, i), layer, a_dz)
        b_spec = _mat_spec(b, tk, tn, lambda i, j, kk: (kk, j), layer, b_dz)
        dn = (((0,), (0,)), ((), ()))

    def body(a_ref, b_ref, o_ref, acc_ref):
        p = lax.dot_general(a_ref[...].astype(bf16), b_ref[...].astype(bf16), dn, preferred_element_type=f32)
        if nk == 1:
            o_ref[...] = p.astype(o_ref.dtype)
        else:
            kk = pl.program_id(2)

            @pl.when(kk == 0)
            def _():
                acc_ref[...] = p

            @pl.when(kk > 0)
            def _():
                acc_ref[...] += p

            @pl.when(kk == nk - 1)
            def _():
                o_ref[...] = acc_ref[...].astype(o_ref.dtype)

    acc_shape = (tm, tn) if nk > 1 else (8, 128)
    return pl.pallas_call(
        body, grid=(m // tm, n // tn, nk), in_specs=[a_spec, b_spec],
        out_specs=pl.BlockSpec((tm, tn), lambda i, j, kk: (i, j)),
        out_shape=jax.ShapeDtypeStruct((m, n), out_dtype),
        scratch_shapes=[pltpu.VMEM(acc_shape, f32)], name=name,
        compiler_params=_params(("parallel", "parallel", "arbitrary")))(a, b)


def _conv_delays():
    return [(8 * a + b, a, b) for b in range(8) for a in range(4) if 8 * a + b < CONV_K]


def _conv_taps(win, dw_ref):
    rolled = {}
    acc = None
    for d, a, b in _conv_delays():
        if b not in rolled:
            rolled[b] = win if b == 0 else pltpu.roll(win, b, axis=0)
        term = rolled[b][HALO - 8 * a:HALO - 8 * a + CT, :] * dw_ref[pl.ds(CONV_K - 1 - d, 1), :]
        acc = term if acc is None else acc + term
    return acc, rolled


def _conv_fwd(z, dw32, cvec, BL, SEQ):
    T = BL * SEQ
    nct = SEQ // CT

    def body(a_ref, b_ref, cg_ref, dw_ref, vec_ref, o_ref, p_ref):
        p_ref[pl.ds(0, HALO), :] = jnp.zeros((HALO, WC), f32)

        def glu(c, carry):
            r0 = pl.multiple_of(c * CT, CT)
            p_ref[pl.ds(r0 + HALO, CT), :] = a_ref[pl.ds(r0, CT), :] * _sig(b_ref[pl.ds(r0, CT), :])
            return carry

        lax.fori_loop(0, nct, glu, 0)

        def step(c, carry):
            r0 = pl.multiple_of(c * CT, CT)
            u1, _ = _conv_taps(p_ref[pl.ds(r0, CT + HALO), :], dw_ref)
            u1 = u1 + vec_ref[0:1, :]
            xc = u1 - jnp.mean(u1, axis=-1, keepdims=True)
            rs = lax.rsqrt(jnp.mean(xc * xc, axis=-1, keepdims=True) + EPS)
            u2 = (xc * rs) * vec_ref[1:2, :] + vec_ref[2:3, :]
            cg = cg_ref[pl.ds(r0, CT), :]
            o_ref[pl.ds(r0, CT), :] = ((u2 * _sig(u2)) * (cg * _sig(cg))).astype(bf16)
            return carry

        lax.fori_loop(0, nct, step, 0)

    def zs(col):
        return pl.BlockSpec((SEQ, WC), lambda b: (b, col // WC))

    return pl.pallas_call(
        body, grid=(BL,),
        in_specs=[zs(COL_A), zs(COL_B), zs(COL_CG), pl.BlockSpec((32, WC), lambda b: (0, 0)),
                  pl.BlockSpec((8, WC), lambda b: (0, 0))],
        out_specs=pl.BlockSpec((SEQ, WC), lambda b: (b, 0)),
        out_shape=jax.ShapeDtypeStruct((T, WC), bf16),
        scratch_shapes=[pltpu.VMEM((SEQ + HALO, WC), f32)], name="conv_fwd",
        compiler_params=_params(("parallel",)))(z, z, z, dw32, cvec)


def _conv_bwd(z, dcv, dz, dw32, cvec, BL, SEQ):
    nct = SEQ // CT

    def body(a_ref, b_ref, cg_ref, dcv_ref, dzin_ref, dw_ref, vec_ref, dz_ref, ddw_ref, dvec_ref, p_ref, q_ref):
        @pl.when(pl.program_id(0) == 0)
        def _():
            ddw_ref[...] = jnp.zeros_like(ddw_ref)
            dvec_ref[...] = jnp.zeros_like(dvec_ref)

        p_ref[pl.ds(0, HALO), :] = jnp.zeros((HALO, WC), f32)
        q_ref[pl.ds(SEQ, HALO), :] = jnp.zeros((HALO, WC), f32)

        def glu(c, carry):
            r0 = pl.multiple_of(c * CT, CT)
            p_ref[pl.ds(r0 + HALO, CT), :] = a_ref[pl.ds(r0, CT), :] * _sig(b_ref[pl.ds(r0, CT), :])
            return carry

        lax.fori_loop(0, nct, glu, 0)

        def step(c, carry):
            r0 = pl.multiple_of(c * CT, CT)
            u1, rolled = _conv_taps(p_ref[pl.ds(r0, CT + HALO), :], dw_ref)
            u1 = u1 + vec_ref[0:1, :]
            xc = u1 - jnp.mean(u1, axis=-1, keepdims=True)
            rs = lax.rsqrt(jnp.mean(xc * xc, axis=-1, keepdims=True) + EPS)
            nrm = xc * rs
            u2 = nrm * vec_ref[1:2, :] + vec_ref[2:3, :]
            s2 = _sig(u2)
            u3 = u2 * s2
            cg = cg_ref[pl.ds(r0, CT), :]
            scg = _sig(cg)
            dcv_ = dcv_ref[pl.ds(r0, CT), :]
            dz_ref[2, pl.ds(r0, CT), :] = (dcv_ * u3 * _dsilu(cg, scg)).astype(bf16)
            du2 = dcv_ * (cg * scg) * _dsilu(u2, s2)
            dvec_ref[1:2, :] += _colsum(du2 * nrm)
            dvec_ref[2:3, :] += _colsum(du2)
            dn = du2 * vec_ref[1:2, :]
            du1 = rs * (dn - jnp.mean(dn, axis=-1, keepdims=True)
                        - nrm * jnp.mean(dn * nrm, axis=-1, keepdims=True))
            dvec_ref[0:1, :] += _colsum(du1)
            q_ref[pl.ds(r0, CT), :] = du1
            for d, a, b in _conv_delays():
                row = CONV_K - 1 - d
                ddw_ref[pl.ds(row, 1), :] += _colsum(du1 * rolled[b][HALO - 8 * a:HALO - 8 * a + CT, :])
            return carry

        lax.fori_loop(0, nct, step, 0)

        def back(c, carry):
            r0 = pl.multiple_of(c * CT, CT)
            wq = q_ref[pl.ds(r0, CT + HALO), :]
            up = {}
            acc = None
            for d, a, b in _conv_delays():
                if b not in up:
                    up[b] = wq if b == 0 else pltpu.roll(wq, CT + HALO - b, axis=0)
                term = up[b][8 * a:8 * a + CT, :] * dw_ref[pl.ds(CONV_K - 1 - d, 1), :]
                acc = term if acc is None else acc + term
            a_ = a_ref[pl.ds(r0, CT), :]
            sb = _sig(b_ref[pl.ds(r0, CT), :])
            dz_ref[0, pl.ds(r0, CT), :] = (acc * sb).astype(bf16)
            dz_ref[1, pl.ds(r0, CT), :] = (acc * a_ * sb * (1.0 - sb)).astype(bf16)
            return carry

        lax.fori_loop(0, nct, back, 0)

    def zs(col):
        return pl.BlockSpec((SEQ, WC), lambda b: (b, col // WC), pipeline_mode=pl.Buffered(1))

    def const(r):
        return pl.BlockSpec((r, WC), lambda b: (0, 0))

    return pl.pallas_call(
        body, grid=(BL,),
        in_specs=[zs(COL_A), zs(COL_B), zs(COL_CG),
                  pl.BlockSpec((SEQ, WC), lambda b: (b, 0), pipeline_mode=pl.Buffered(1)), ANY, const(32), const(8)],
        out_specs=[pl.BlockSpec((3, SEQ, WC), lambda b: (DZ_CONV // 3, b, 0)), const(32), const(8)],
        out_shape=[jax.ShapeDtypeStruct(dz.shape, bf16), jax.ShapeDtypeStruct((32, WC), f32),
                   jax.ShapeDtypeStruct((8, WC), f32)],
        scratch_shapes=[pltpu.VMEM((SEQ + HALO, WC), f32), pltpu.VMEM((SEQ + HALO, WC), f32)],
        input_output_aliases={4: 0}, name="conv_bwd",
        compiler_params=_params(("arbitrary",)))(z, z, z, dcv, dz, dw32, cvec)


def _pool_counts(r0):
    t1 = r0 + 1 + lax.broadcasted_iota(jnp.int32, (CT, 1), 0)
    return [jnp.minimum(t1, w).astype(f32) for w in POOL_WINDOWS]


def _pool_sums(win, forward):
    n = CT + PHALO

    def sh(x, s):
        return pltpu.roll(x, (n - s) if forward else s, axis=0)

    s2 = win + sh(win, 1)
    s4 = s2[:, GD:] + sh(s2[:, GD:], 2)
    s8 = s4[:, GD:] + sh(s4[:, GD:], 4)
    s16 = s8[:, GD:] + sh(s8[:, GD:], 8)
    lo = 0 if forward else PHALO
    return [s[lo:lo + CT, :GD] for s in (s2, s4, s8, s16)]


def _pool_fwd(z, pw, pvec, BL, SEQ):
    T = BL * SEQ
    nct = SEQ // CT

    def body(pi_ref, pg_ref, pw_ref, vec_ref, o_ref, p_ref):
        p_ref[pl.ds(0, PHALO), :] = jnp.zeros((PHALO, WC), f32)

        def fill(c, carry):
            r0 = pl.multiple_of(c * CT, CT)
            p_ref[pl.ds(r0 + PHALO, CT), :] = pi_ref[pl.ds(r0, CT), :]
            return carry

        lax.fori_loop(0, nct, fill, 0)

        def step(c, carry):
            r0 = pl.multiple_of(c * CT, CT)
            sums = _pool_sums(p_ref[pl.ds(r0, CT + PHALO), :], False)
            cnt = _pool_counts(r0)
            pin = pi_ref[pl.ds(r0, CT), :]
            mixed = []
            for g in range(4):
                pooled = sums[g] / cnt[g] - pin[:, g * GD:(g + 1) * GD]
                mixed.append(jnp.dot(pooled.astype(bf16), pw_ref[g], preferred_element_type=f32))
            m0 = jnp.concatenate(mixed, axis=1) + vec_ref[0:1, :]
            pg = pg_ref[pl.ds(r0, CT), :]
            o_ref[pl.ds(r0, CT), :] = ((m0 * vec_ref[1:2, :]) * (pg * _sig(pg))).astype(bf16)
            return carry

        lax.fori_loop(0, nct, step, 0)

    def zs(col):
        return pl.BlockSpec((SEQ, WC), lambda b: (b, col // WC))

    return pl.pallas_call(
        body, grid=(BL,),
        in_specs=[zs(COL_PI), zs(COL_PG), pl.BlockSpec((4, GD, GD), lambda b: (0, 0, 0)),
                  pl.BlockSpec((8, WC), lambda b: (0, 0))],
        out_specs=pl.BlockSpec((SEQ, WC), lambda b: (b, 0)),
        out_shape=jax.ShapeDtypeStruct((T, WC), bf16),
        scratch_shapes=[pltpu.VMEM((SEQ + PHALO, WC), f32)], name="pool_fwd",
        compiler_params=_params(("parallel",)))(z, z, pw, pvec)


def _pool_bwd(z, dpl, dz, pw, pvec, BL, SEQ):
    nct = SEQ // CT

    def body(pi_ref, pg_ref, dpl_ref, dzin_ref, pw_ref, vec_ref, dz_ref, dpw_ref, dvec_ref, p_ref, e_ref, dp_ref):
        @pl.when(pl.program_id(0) == 0)
        def _():
            dpw_ref[...] = jnp.zeros_like(dpw_ref)
            dvec_ref[...] = jnp.zeros_like(dvec_ref)

        p_ref[pl.ds(0, PHALO), :] = jnp.zeros((PHALO, WC), f32)
        e_ref[pl.ds(SEQ, PHALO), :] = jnp.zeros((PHALO, WC), f32)

        def fill(c, carry):
            r0 = pl.multiple_of(c * CT, CT)
            p_ref[pl.ds(r0 + PHALO, CT), :] = pi_ref[pl.ds(r0, CT), :]
            return carry

        lax.fori_loop(0, nct, fill, 0)

        def step(c, carry):
            r0 = pl.multiple_of(c * CT, CT)
            sums = _pool_sums(p_ref[pl.ds(r0, CT + PHALO), :], False)
            cnt = _pool_counts(r0)
            pin = pi_ref[pl.ds(r0, CT), :]
            pooled = [(sums[g] / cnt[g] - pin[:, g * GD:(g + 1) * GD]).astype(bf16) for g in range(4)]
            m0 = jnp.concatenate(
                [jnp.dot(pooled[g], pw_ref[g], preferred_element_type=f32) for g in range(4)], axis=1) + vec_ref[0:1, :]
            scale = vec_ref[1:2, :]
            pg = pg_ref[pl.ds(r0, CT), :]
            spg = _sig(pg)
            dpl_ = dpl_ref[pl.ds(r0, CT), :]
            dmixed = dpl_ * (pg * spg)
            dz_ref[1, pl.ds(r0, CT), :] = (dpl_ * (m0 * scale) * _dsilu(pg, spg)).astype(bf16)
            dvec_ref[1:2, :] += _colsum(dmixed * m0)
            dm0 = dmixed * scale
            dvec_ref[0:1, :] += _colsum(dm0)
            dps, es = [], []
            for g in range(4):
                dm0g = dm0[:, g * GD:(g + 1) * GD].astype(bf16)
                dpw_ref[g] += lax.dot_general(pooled[g], dm0g, (((0,), (0,)), ((), ())), preferred_element_type=f32)
                dpg = lax.dot_general(dm0g, pw_ref[g], (((1,), (1,)), ((), ())), preferred_element_type=f32)
                dps.append(dpg)
                es.append(dpg / cnt[g])
            dp_ref[pl.ds(r0, CT), :] = jnp.concatenate(dps, axis=1)
            e_ref[pl.ds(r0, CT), :] = jnp.concatenate(es, axis=1)
            return carry

        lax.fori_loop(0, nct, step, 0)

        def back(c, carry):
            r0 = pl.multiple_of(c * CT, CT)
            fs = _pool_sums(e_ref[pl.ds(r0, CT + PHALO), :], True)
            dz_ref[0, pl.ds(r0, CT), :] = (jnp.concatenate(fs, axis=1) - dp_ref[pl.ds(r0, CT), :]).astype(bf16)
            return carry

        lax.fori_loop(0, nct, back, 0)

    def zs(col):
        return pl.BlockSpec((SEQ, WC), lambda b: (b, col // WC), pipeline_mode=pl.Buffered(1))

    return pl.pallas_call(
        body, grid=(BL,),
        in_specs=[zs(COL_PI), zs(COL_PG),
                  pl.BlockSpec((SEQ, WC), lambda b: (b, 0), pipeline_mode=pl.Buffered(1)), ANY,
                  pl.BlockSpec((4, GD, GD), lambda b: (0, 0, 0)), pl.BlockSpec((8, WC), lambda b: (0, 0))],
        out_specs=[pl.BlockSpec((2, SEQ, WC), lambda b: (DZ_POOL // 2, b, 0)),
                   pl.BlockSpec((4, GD, GD), lambda b: (0, 0, 0)), pl.BlockSpec((8, WC), lambda b: (0, 0))],
        out_shape=[jax.ShapeDtypeStruct(dz.shape, bf16), jax.ShapeDtypeStruct((4, GD, GD), f32),
                   jax.ShapeDtypeStruct((8, WC), f32)],
        scratch_shapes=[pltpu.VMEM((SEQ + PHALO, WC), f32), pltpu.VMEM((SEQ + PHALO, WC), f32),
                        pltpu.VMEM((SEQ, WC), f32)],
        input_output_aliases={3: 0}, name="pool_bwd",
        compiler_params=_params(("arbitrary",)))(z, z, dpl, dz, pw, pvec)


def _attn_prologue(q_ref, k_ref, v_ref, qs0, qs1, kp, vp, SEQ):
    head0 = lax.broadcasted_iota(jnp.int32, (1, 2 * HEAD_DIM), 1) < HEAD_DIM
    kp[pl.ds(0, KEY_PAD), :] = jnp.zeros((KEY_PAD, 2 * HEAD_DIM), bf16)
    vp[pl.ds(0, KEY_PAD), :] = jnp.zeros((KEY_PAD, 2 * HEAD_DIM), bf16)

    def fill(g, carry):
        r0 = pl.multiple_of(g * QG, QG)
        q = q_ref[pl.ds(r0, QG), :] * (HEAD_DIM ** -0.5)
        qs0[pl.ds(r0, QG), :] = jnp.where(head0, q, 0.0).astype(bf16)
        qs1[pl.ds(r0, QG), :] = jnp.where(head0, 0.0, q).astype(bf16)
        kp[pl.ds(r0 + KEY_PAD, QG), :] = k_ref[pl.ds(r0, QG), :].astype(bf16)
        vp[pl.ds(r0 + KEY_PAD, QG), :] = v_ref[pl.ds(r0, QG), :].astype(bf16)
        return carry

    lax.fori_loop(0, SEQ // QG, fill, 0)
    return head0


def _attn_probs(qh, kw, bias, r0):
    s = lax.dot_general(qh, kw, (((1,), (1,)), ((), ())), preferred_element_type=f32) + bias
    key_pos = r0 - KEY_PAD + lax.broadcasted_iota(jnp.int32, (1, KW), 1)
    s = jnp.where(key_pos >= 0, s, NEG_INF)
    e = jnp.exp(s - jnp.max(s, axis=-1, keepdims=True))
    return e * (1.0 / jnp.sum(e, axis=-1, keepdims=True))


def _attn_fwd(z, bm, BL, SEQ):
    T = BL * SEQ
    W2 = 2 * HEAD_DIM

    def body(q_ref, k_ref, v_ref, ag_ref, bm_ref, o_ref, qs0, qs1, kp, vp):
        head0 = _attn_prologue(q_ref, k_ref, v_ref, qs0, qs1, kp, vp, SEQ)

        def group(g, carry):
            r0 = pl.multiple_of(g * QG, QG)
            kw = kp[pl.ds(r0, KW), :]
            vw = vp[pl.ds(r0, KW), :]
            outs = []
            for hh, qs in enumerate((qs0, qs1)):
                p = _attn_probs(qs[pl.ds(r0, QG), :], kw, bm_ref[hh], r0)
                outs.append(jnp.dot(p.astype(bf16), vw, preferred_element_type=f32))
            o = jnp.where(head0, outs[0], outs[1])
            ag = ag_ref[pl.ds(r0, QG), :]
            o_ref[pl.ds(r0, QG), :] = (o * (ag * _sig(ag))).astype(bf16)
            return carry

        lax.fori_loop(0, SEQ // QG, group, 0)

    def zs(col):
        return pl.BlockSpec((SEQ, W2), lambda b, hp: (b, col // W2 + hp))

    return pl.pallas_call(
        body, grid=(BL, WC // W2),
        in_specs=[zs(COL_Q), zs(COL_K), zs(COL_V), zs(COL_AG), pl.BlockSpec((2, QG, KW), lambda b, hp: (hp, 0, 0))],
        out_specs=pl.BlockSpec((SEQ, W2), lambda b, hp: (b, hp)),
        out_shape=jax.ShapeDtypeStruct((T, WC), bf16),
        scratch_shapes=[pltpu.VMEM((SEQ, W2), bf16), pltpu.VMEM((SEQ, W2), bf16),
                        pltpu.VMEM((SEQ + KEY_PAD, W2), bf16), pltpu.VMEM((SEQ + KEY_PAD, W2), bf16)],
        name="attn_fwd", compiler_params=_params(("parallel", "parallel")))(z, z, z, z, bm)


def _attn_bwd(z, dat, dz, bm, BL, SEQ):
    W2 = 2 * HEAD_DIM

    def body(q_ref, k_ref, v_ref, ag_ref, dat_ref, dzin_ref, bm_ref, dz_ref, dbm_ref,
             qs0, qs1, kp, vp, do0, do1, dka, dva):
        @pl.when(pl.program_id(1) == 0)
        def _():
            dbm_ref[...] = jnp.zeros_like(dbm_ref)

        head0 = _attn_prologue(q_ref, k_ref, v_ref, qs0, qs1, kp, vp, SEQ)
        dka[...] = jnp.zeros_like(dka)
        dva[...] = jnp.zeros_like(dva)

        def fill(g, carry):
            r0 = pl.multiple_of(g * QG, QG)
            ag = ag_ref[pl.ds(r0, QG), :]
            do = dat_ref[pl.ds(r0, QG), :] * (ag * _sig(ag))
            do0[pl.ds(r0, QG), :] = jnp.where(head0, do, 0.0).astype(bf16)
            do1[pl.ds(r0, QG), :] = jnp.where(head0, 0.0, do).astype(bf16)
            return carry

        lax.fori_loop(0, SEQ // QG, fill, 0)

        def group(g, carry):
            r0 = pl.multiple_of(g * QG, QG)
            kw = kp[pl.ds(r0, KW), :]
            vw = vp[pl.ds(r0, KW), :]
            outs, dqs = [], []
            for hh, (qs, dos) in enumerate(((qs0, do0), (qs1, do1))):
                qh = qs[pl.ds(r0, QG), :]
                doh = dos[pl.ds(r0, QG), :]
                p = _attn_probs(qh, kw, bm_ref[hh], r0)
                pb = p.astype(bf16)
                outs.append(jnp.dot(pb, vw, preferred_element_type=f32))
                dp = lax.dot_general(doh, vw, (((1,), (1,)), ((), ())), preferred_element_type=f32)
                ds_ = p * (dp - jnp.sum(p * dp, axis=-1, keepdims=True))
                dbm_ref[hh] += ds_
                dsb = ds_.astype(bf16)
                dqs.append(jnp.dot(dsb, kw, preferred_element_type=f32))
                dka[pl.ds(r0, KW), :] += lax.dot_general(dsb, qh, (((0,), (0,)), ((), ())), preferred_element_type=f32)
                dva[pl.ds(r0, KW), :] += lax.dot_general(pb, doh, (((0,), (0,)), ((), ())), preferred_element_type=f32)
            o = jnp.where(head0, outs[0], outs[1])
            dq = jnp.where(head0, dqs[0], dqs[1]) * (HEAD_DIM ** -0.5)
            ag = ag_ref[pl.ds(r0, QG), :]
            dz_ref[0, pl.ds(r0, QG), :] = dq.astype(bf16)
            dz_ref[3, pl.ds(r0, QG), :] = (dat_ref[pl.ds(r0, QG), :] * o * _dsilu(ag, _sig(ag))).astype(bf16)
            return carry

        lax.fori_loop(0, SEQ // QG, group, 0)

        def flush(g, carry):
            r0 = pl.multiple_of(g * QG, QG)
            dz_ref[1, pl.ds(r0, QG), :] = dka[pl.ds(r0 + KEY_PAD, QG), :].astype(bf16)
            dz_ref[2, pl.ds(r0, QG), :] = dva[pl.ds(r0 + KEY_PAD, QG), :].astype(bf16)
            return carry

        lax.fori_loop(0, SEQ // QG, flush, 0)

    def zs(col):
        return pl.BlockSpec((SEQ, W2), lambda hp, b: (b, col // W2 + hp))

    return pl.pallas_call(
        body, grid=(WC // W2, BL),
        in_specs=[zs(COL_Q), zs(COL_K), zs(COL_V), zs(COL_AG), pl.BlockSpec((SEQ, W2), lambda hp, b: (b, hp)), ANY,
                  pl.BlockSpec((2, QG, KW), lambda hp, b: (hp, 0, 0))],
        out_specs=[pl.BlockSpec((4, SEQ, W2), lambda hp, b: (DZ_ATTN // 4, b, hp)),
                   pl.BlockSpec((2, QG, KW), lambda hp, b: (hp, 0, 0))],
        out_shape=[jax.ShapeDtypeStruct(dz.shape, bf16), jax.ShapeDtypeStruct((8, QG, KW), f32)],
        scratch_shapes=[pltpu.VMEM((SEQ, W2), bf16), pltpu.VMEM((SEQ, W2), bf16),
                        pltpu.VMEM((SEQ + KEY_PAD, W2), bf16), pltpu.VMEM((SEQ + KEY_PAD, W2), bf16),
                        pltpu.VMEM((SEQ, W2), bf16), pltpu.VMEM((SEQ, W2), bf16),
                        pltpu.VMEM((SEQ + KEY_PAD, W2), f32), pltpu.VMEM((SEQ + KEY_PAD, W2), f32)],
        input_output_aliases={5: 0}, name="attn_bwd",
        compiler_params=_params(("parallel", "arbitrary")))(z, z, z, z, dat, dz, bm)


def _bias_matrix(table):
    width = QG + KW - 1
    row = jnp.concatenate([table[:, 1:2 * MAX_REL], jnp.broadcast_to(table[:, 2 * MAX_REL:], (8, width - 2 * MAX_REL + 1))],
                          axis=1)
    flat = jnp.broadcast_to(row[:, None, :], (8, QG, width)).reshape(8, QG * width)
    skew = jnp.pad(flat, ((0, 0), (0, QG))).reshape(8, QG, width + 1)[:, :, :KW]
    vals = skew[:, :, ::-1]
    r = np.arange(QG)[:, None] // CHUNK
    j = np.arange(KW)[None, :] // CHUNK
    band = (j >= r) & (j <= r + LEFT_CHUNKS)
    return jnp.where(jnp.asarray(band)[None], vals, NEG_INF)


def _bias_fold(dbm):
    width = QG + KW - 1
    flipped = dbm[:, :, ::-1]
    padded = jnp.pad(flipped, ((0, 0), (0, 0), (0, width + 1 - KW))).reshape(8, QG * (width + 1))
    return jnp.pad(padded[:, :QG * width].reshape(8, QG, width), ((0, 0), (0, 0), (0, 1)))


def _bias_colsum(folded):
    width = folded.shape[2]

    def body(x_ref, o_ref):
        for h in range(8):
            o_ref[pl.ds(h, 1), :] = _colsum(x_ref[h])

    return pl.pallas_call(body, out_shape=jax.ShapeDtypeStruct((8, width), f32), name="bias_colsum",
                          compiler_params=_params())(folded)


def _bias_table_grad(colsum):
    tail = jnp.sum(colsum[:, 2 * MAX_REL - 1:], axis=1, keepdims=True)
    return jnp.concatenate([jnp.zeros((8, 1), f32), colsum[:, :2 * MAX_REL - 1], tail], axis=1)


def _merge_fwd(z, ys):
    T = z.shape[0]
    tm = 512

    def body(g0, g1, g2, y0, y1, y2, o_ref):
        acc = _sig(g0[...]) * y0[...] + _sig(g1[...]) * y1[...] + _sig(g2[...]) * y2[...]
        o_ref[...] = acc.astype(bf16)

    def gs(br):
        return pl.BlockSpec((tm, WC), lambda i, j: (i, (COL_GM + br * D) // WC + j))

    ysp = pl.BlockSpec((tm, WC), lambda i, j: (i, j))
    return pl.pallas_call(
        body, grid=(T // tm, D // WC), in_specs=[gs(0), gs(1), gs(2), ysp, ysp, ysp], out_specs=ysp,
        out_shape=jax.ShapeDtypeStruct((T, D), bf16), name="merge_fwd",
        compiler_params=_params(("parallel", "parallel")))(z, z, z, *ys)


def _merge_bwd(z, dmerged, ys):
    T = z.shape[0]
    tm = 256

    def body(*refs):
        g = refs[0:6]
        dm_ref = refs[6]
        y = refs[7:10]
        dy = refs[10:13]
        dz_ref = refs[13]
        for br in range(3):
            for jh in range(2):
                cols = slice(jh * WC, (jh + 1) * WC)
                s = _sig(g[2 * br + jh][...])
                dm = dm_ref[:, cols]
                dy[br][:, cols] = (dm * s).astype(bf16)
                dz_ref[2 * br + jh] = (dm * y[br][:, cols] * s * (1.0 - s)).astype(bf16)

    def gs(blk):
        return pl.BlockSpec((tm, WC), lambda i: (i, COL_GM // WC + blk))

    row = pl.BlockSpec((tm, D), lambda i: (i, 0))
    return pl.pallas_call(
        body, grid=(T // tm,), in_specs=[gs(b) for b in range(6)] + [row] * 4,
        out_specs=[row, row, row, pl.BlockSpec((6, tm, WC), lambda i: (DZ_GM // 6, i, 0))],
        out_shape=[jax.ShapeDtypeStruct((T, D), bf16)] * 3 + [jax.ShapeDtypeStruct((DZ_BLOCKS, T, WC), bf16)],
        name="merge_bwd", compiler_params=_params(("parallel",)))(*([z] * 6), dmerged, *ys)


def _adamw(name, g, w, m, v):
    R, C = w.shape
    tr = R
    for cand in (512, 256, 248, 128, 64, 32, 16, 8):
        if R % cand == 0 and cand * C * 4 <= 2 * 1024 * 1024:
            tr = cand
            break
    c1 = 1.0 - ADAM_B1
    c2 = 1.0 - ADAM_B2
    bc1 = 1.0 - ADAM_B1 ** ADAM_STEP
    bc2 = 1.0 - ADAM_B2 ** ADAM_STEP

    def body(g_ref, w_ref, m_ref, v_ref, d_ref, nm_ref, nv_ref):
        g_ = g_ref[...]
        nm = ADAM_B1 * m_ref[...] + c1 * g_
        nv = ADAM_B2 * v_ref[...] + c2 * (g_ * g_)
        nm_ref[...] = nm
        nv_ref[...] = nv
        d_ref[...] = -ADAM_LR * ((nm / bc1) / (jnp.sqrt(nv / bc2) + ADAM_EPS) + ADAM_WD * w_ref[...])

    spec = pl.BlockSpec((tr, C), lambda i: (i, 0))
    return pl.pallas_call(
        body, grid=(R // tr,), in_specs=[spec] * 4, out_specs=[spec] * 3,
        out_shape=[jax.ShapeDtypeStruct((R, C), f32)] * 3, name=name,
        compiler_params=_params(("parallel",)))(g, w, m, v)


def _sum_slots(name, parts):
    _, R, C = parts.shape
    tr = R
    for cand in (256, 128, 64, 32, 16, 8):
        if R % cand == 0 and cand * C * 4 * N_DEV <= 8 * 1024 * 1024:
            tr = cand
            break

    def body(p_ref, o_ref):
        acc = p_ref[0].astype(f32)
        for s in range(1, N_DEV):
            acc = acc + p_ref[s].astype(f32)
        o_ref[...] = acc

    return pl.pallas_call(
        body, grid=(R // tr,), in_specs=[pl.BlockSpec((N_DEV, tr, C), lambda i: (0, i, 0))],
        out_specs=pl.BlockSpec((tr, C), lambda i: (i, 0)), out_shape=jax.ShapeDtypeStruct((R, C), f32),
        name=name, compiler_params=_params(("parallel",)))(parts)


def _place():
    x, y, c = lax.axis_index("x"), lax.axis_index("y"), lax.axis_index("c")
    return x, y, c


def _flip(v, bit):
    return 1 - v if bit else v


def _gather_weights(shards):
    n = len(shards)

    def body(*refs):
        src = refs[:n]
        out = refs[n:2 * n]
        send_sems, recv_sems, local_sems = refs[2 * n:]
        x, y, c = _place()
        chip = 2 * x + y
        sibling = (x, y, 1 - c)
        flips = [(1, 0), (0, 1), (1, 1)]

        def remote(a, k, half, owner, to, from_src):
            dst = out[a].at[half, owner]
            return pltpu.make_async_remote_copy(
                src_ref=src[a].at[half] if from_src else dst, dst_ref=dst,
                send_sem=send_sems.at[a, k], recv_sem=recv_sems.at[a, k], device_id=to, device_id_type=MESH)

        started = []
        locals_ = []
        for a in range(n):
            for half in range(2):
                cp = pltpu.make_async_copy(src[a].at[half], out[a].at[half, chip], local_sems.at[a, half])
                cp.start()
                locals_.append(cp)
            for k, (fx, fy) in enumerate(flips):
                cp = remote(a, k, c, chip, (_flip(x, fx), _flip(y, fy), c), True)
                cp.start()
                started.append(cp)
        for a in range(n):
            for k, (fx, fy) in enumerate(flips):
                owner = 2 * _flip(x, fx) + _flip(y, fy)
                remote(a, k, c, owner, sibling, False).wait_recv()
                cp = remote(a, 3 + k, c, owner, sibling, False)
                cp.start()
                started.append(cp)
        for a in range(n):
            for k, (fx, fy) in enumerate(flips):
                owner = 2 * _flip(x, fx) + _flip(y, fy)
                remote(a, 3 + k, 1 - c, owner, sibling, False).wait_recv()
        for cp in started:
            cp.wait_send()
        for cp in locals_:
            cp.wait()

    out_shape = [jax.ShapeDtypeStruct((2, N_CHIPS) + s.shape[1:], s.dtype) for s in shards]
    return pl.pallas_call(
        body, in_specs=[ANY] * n, out_specs=[ANY] * n, out_shape=out_shape,
        scratch_shapes=[pltpu.SemaphoreType.DMA((n, 6)), pltpu.SemaphoreType.DMA((n, 6)),
                        pltpu.SemaphoreType.DMA((n, 2))],
        name="gather_weights")(*shards)


def _scatter_grads(grads):
    n = len(grads)

    def body(*refs):
        src = refs[:n]
        out = refs[n:2 * n]
        send_sems, recv_sems, local_sems = refs[2 * n:]
        x, y, c = _place()

        def piece(a, px, py, pc):
            r2 = src[a].shape[0] // 2
            cw = src[a].shape[1] // N_CHIPS
            return src[a].at[pl.ds(pc * r2, r2), pl.ds((2 * px + py) * cw, cw)]

        started = []
        for a in range(n):
            cp = pltpu.make_async_copy(piece(a, x, y, c), out[a].at[0], local_sems.at[a])
            cp.start()
            started.append(cp)
        remotes = []
        for a in range(n):
            for k in range(1, N_DEV):
                px, py, pc = _flip(x, (k >> 2) & 1), _flip(y, (k >> 1) & 1), _flip(c, k & 1)
                cp = pltpu.make_async_remote_copy(
                    src_ref=piece(a, px, py, pc), dst_ref=out[a].at[k], send_sem=send_sems.at[a, k - 1],
                    recv_sem=recv_sems.at[a, k - 1], device_id=(px, py, pc), device_id_type=MESH)
                cp.start()
                remotes.append(cp)
        for cp in remotes:
            cp.wait_recv()
        for cp in remotes:
            cp.wait_send()
        for cp in started:
            cp.wait()

    out_shape = [jax.ShapeDtypeStruct((N_DEV, g.shape[0] // 2, g.shape[1] // N_CHIPS), g.dtype) for g in grads]
    return pl.pallas_call(
        body, in_specs=[ANY] * n, out_specs=[ANY] * n, out_shape=out_shape,
        scratch_shapes=[pltpu.SemaphoreType.DMA((n, N_DEV - 1)), pltpu.SemaphoreType.DMA((n, N_DEV - 1)),
                        pltpu.SemaphoreType.DMA((n,))],
        name="scatter_grads")(*grads)


def _share_halves(halves):
    n = len(halves)

    def body(*refs):
        src = refs[:n * DEPTH]
        out = refs[n * DEPTH:n * DEPTH + n]
        send_sems, recv_sems, local_sems = refs[n * DEPTH + n:]
        x, y, c = _place()
        sibling = (x, y, 1 - c)
        copies = []
        for a in range(n):
            for l in range(DEPTH):
                s = src[a * DEPTH + l]
                loc = pltpu.make_async_copy(s, out[a].at[l, c], local_sems.at[a, l])
                loc.start()
                rem = pltpu.make_async_remote_copy(
                    src_ref=s, dst_ref=out[a].at[l, c], send_sem=send_sems.at[a, l], recv_sem=recv_sems.at[a, l],
                    device_id=sibling, device_id_type=MESH)
                rem.start()
                copies.append((loc, rem, a, l))
        for loc, rem, a, l in copies:
            pltpu.make_async_remote_copy(
                src_ref=src[a * DEPTH + l], dst_ref=out[a].at[l, 1 - c], send_sem=send_sems.at[a, l],
                recv_sem=recv_sems.at[a, l], device_id=sibling, device_id_type=MESH).wait_recv()
            rem.wait_send()
            loc.wait()

    flat = [h for per_weight in halves for h in per_weight]
    out_shape = [jax.ShapeDtypeStruct((DEPTH, 2) + per_weight[0].shape, f32) for per_weight in halves]
    return pl.pallas_call(
        body, in_specs=[ANY] * (n * DEPTH), out_specs=[ANY] * n, out_shape=out_shape,
        scratch_shapes=[pltpu.SemaphoreType.DMA((n, DEPTH)), pltpu.SemaphoreType.DMA((n, DEPTH)),
                        pltpu.SemaphoreType.DMA((n, DEPTH))],
        name="share_halves")(*flat)


def _gather_small(packed):
    def body(src, out, send_sems, recv_sems, local_sem):
        x, y, c = _place()
        me = 4 * x + 2 * y + c
        loc = pltpu.make_async_copy(src, out.at[me], local_sem)
        loc.start()
        remotes = []
        for k in range(1, N_DEV):
            px, py, pc = _flip(x, (k >> 2) & 1), _flip(y, (k >> 1) & 1), _flip(c, k & 1)
            cp = pltpu.make_async_remote_copy(
                src_ref=src, dst_ref=out.at[me], send_sem=send_sems.at[k - 1], recv_sem=recv_sems.at[k - 1],
                device_id=(px, py, pc), device_id_type=MESH)
            cp.start()
            remotes.append((cp, 4 * px + 2 * py + pc))
        for k, (cp, peer) in enumerate(remotes):
            pltpu.make_async_remote_copy(
                src_ref=src, dst_ref=out.at[peer], send_sem=send_sems.at[k], recv_sem=recv_sems.at[k],
                device_id=(x, y, c), device_id_type=MESH).wait_recv()
        for cp, _ in remotes:
            cp.wait_send()
        loc.wait()

    return pl.pallas_call(
        body, in_specs=[ANY], out_specs=ANY, out_shape=jax.ShapeDtypeStruct((N_DEV,) + packed.shape, f32),
        scratch_shapes=[pltpu.SemaphoreType.DMA((N_DEV - 1,)), pltpu.SemaphoreType.DMA((N_DEV - 1,)),
                        pltpu.SemaphoreType.DMA],
        name="gather_small")(packed)


def _rows8(v):
    return jnp.pad(v[None, :], ((0, 7), (0, 0)))


def _vec_rows(vs):
    return jnp.pad(jnp.stack(vs), ((0, 8 - len(vs)), (0, 0)))


SMALL_ROWS = 224


def _pack_small(conv_vec, conv_dw, pool_vec, pool_w, pre_g, post_g, rel):
    return jnp.concatenate([
        conv_vec, conv_dw, pool_vec, pool_w.reshape(GD, WC),
        _rows8(pre_g).reshape(16, WC), _rows8(post_g).reshape(16, WC),
        jnp.pad(rel, ((0, 0), (0, D - rel.shape[1]))).reshape(16, WC)], axis=0)


def _unpack_small(p):
    conv_vec, pool_vec = p[0:8], p[40:48]
    return dict(
        conv_dw_b=conv_vec[0], conv_ln_g=conv_vec[1], conv_ln_b=conv_vec[2], conv_dw=p[8:8 + CONV_K],
        pool_b=pool_vec[0].reshape(4, GD), pool_scale=pool_vec[1], pool_w=p[48:176].reshape(4, GD, GD),
        pre_norm_g=p[176:192].reshape(8, D)[0], post_norm_g=p[192:208].reshape(8, D)[0],
        rel_bias=p[208:224].reshape(8, D)[:, :2 * MAX_REL + 1])


def _layer_fwd(x2, lw, l, BL, SEQ):
    T = BL * SEQ
    h = _rms_pre(x2, lw["pre_g"][l])
    z = _mm("mm_in", h, lw["w_in"], "nn", T, NCOL, D, 1024, 1536, D, f32, layer=l)
    cv = _conv_fwd(z, lw["dw32"][l], lw["cvec"][l], BL, SEQ)
    at = _attn_fwd(z, lw["bm"][l], BL, SEQ)
    pv = _pool_fwd(z, lw["pw"][l], lw["pvec"][l], BL, SEQ)
    ys = [_mm("mm_branch_out", act, lw[w], "nn", T, D, WC, 1024, D, WC, f32, layer=l)
          for act, w in ((cv, "w_conv"), (at, "w_attn"), (pv, "w_pool"))]
    merged = _merge_fwd(z, ys)
    y = _mm("mm_out", merged, lw["w_out"], "nn", T, D, D, 1024, D, D, f32, layer=l)
    out = _post_fwd(y, x2, lw["post_g"][l])
    return out, dict(x=x2, h=h, z=z, acts=(cv, at, pv), ys=ys, merged=merged, y=y)


def _layer_bwd(dout, sv, lw, l, BL, SEQ):
    T = BL * SEQ
    dy, dpost = _post_bwd(dout, sv["y"], lw["post_g"][l])
    dmerged = _mm("mm_dmerged", dy, lw["w_out"], "nt", T, D, D, 1024, D, D, f32, layer=l)
    dw_out_t = _mm("mm_dw_out", dy, sv["merged"], "tn", D, D, T, D, D, 1024, bf16)
    dys_and_dz = _merge_bwd(sv["z"], dmerged, sv["ys"])
    dys, dz = dys_and_dz[:3], dys_and_dz[3]
    dacts = [_mm("mm_dact", dyb, lw[w], "nt", T, WC, D, 1024, WC, D, f32, layer=l)
             for dyb, w in zip(dys, ("w_conv", "w_attn", "w_pool"))]
    dws = [_mm("mm_dw_branch", act, dyb, "tn", WC, D, T, WC, D, 1024, bf16) for act, dyb in zip(sv["acts"], dys)]
    dz, ddw, dcvec = _conv_bwd(sv["z"], dacts[0], dz, lw["dw32"][l], lw["cvec"][l], BL, SEQ)
    dz, dbm = _attn_bwd(sv["z"], dacts[1], dz, lw["bm"][l], BL, SEQ)
    dz, dpw, dpvec = _pool_bwd(sv["z"], dacts[2], dz, lw["pw"][l], lw["pvec"][l], BL, SEQ)
    dh = _mm("mm_dh", dz, lw["w_in"], "nt", T, D, NCOL, 1024, D, WC, f32, layer=l, a_dz=True)
    dw_in = _mm("mm_dw_in", sv["h"], dz, "tn", D, NCOL, T, D, WC, 1024, bf16, b_dz=True)
    dx, dpre = _pre_bwd(dh, sv["x"], lw["pre_g"][l], dout)
    drel = _bias_table_grad(_bias_colsum(_bias_fold(dbm)))
    small = _pack_small(dcvec, ddw, dpvec, dpw, dpre[0], dpost[0], drel)
    return dx, dict(w_in=dw_in, w_conv_out=dws[0], w_attn_out=dws[1], w_pool_out=dws[2], w_out=dw_out_t), small


def _local_step(x2, tgt2, lw, BL, SEQ):
    saved = []
    cur = x2
    for l in range(DEPTH):
        cur, sv = _layer_fwd(cur, lw, l, BL, SEQ)
        saved.append(sv)
    dout, sq = _loss_head(cur, tgt2)
    big, small = [None] * DEPTH, [None] * DEPTH
    for l in reversed(range(DEPTH)):
        dout, big[l], small[l] = _layer_bwd(dout, saved[l], lw, l, BL, SEQ)
    return sq, dout, big, small


def _full_weights(gathered, pre_norm_g, post_norm_g, conv_dw_full, conv_dw_b, conv_ln_g, conv_ln_b, rel_bias,
                  pool_w, pool_b, pool_scale):
    def cols(g):
        return jnp.transpose(g, (0, 2, 1, 3)).reshape(g.shape[0], g.shape[2], N_CHIPS * g.shape[3])

    g_out = gathered["w_out"]
    return dict(
        w_in=cols(gathered["w_in"]), w_conv=cols(gathered["w_conv_out"]), w_attn=cols(gathered["w_attn_out"]),
        w_pool=cols(gathered["w_pool_out"]), w_out=g_out.reshape(DEPTH, D, D),
        pre_g=pre_norm_g[:, None, :], post_g=post_norm_g[:, None, :],
        dw32=jnp.pad(conv_dw_full, ((0, 0), (0, 32 - CONV_K), (0, 0))),
        cvec=jnp.stack([_vec_rows([conv_dw_b[l], conv_ln_g[l], conv_ln_b[l]]) for l in range(DEPTH)]),
        bm=jnp.stack([_bias_matrix(rel_bias[l]) for l in range(DEPTH)]),
        pw=pool_w.astype(bf16),
        pvec=jnp.stack([_vec_rows([pool_b[l].reshape(WC), pool_scale[l]]) for l in range(DEPTH)]))


BIG = ("w_in", "w_conv_out", "w_attn_out", "w_pool_out", "w_out")
SMALL = ("pre_norm_g", "post_norm_g", "conv_dw_b", "conv_ln_g", "conv_ln_b", "rel_bias", "pool_w", "pool_b", "pool_scale")
ORDER = ("pre_norm_g", "post_norm_g", "w_in", "conv_dw", "conv_dw_b", "conv_ln_g", "conv_ln_b", "w_conv_out",
         "rel_bias", "w_attn_out", "pool_w", "pool_b", "pool_scale", "w_pool_out", "w_out")


def _pack_small_params(p):
    return jnp.concatenate([
        _pack_small(_vec_rows([p["conv_dw_b"][l], p["conv_ln_g"][l], p["conv_ln_b"][l]]), jnp.zeros((32, WC), f32),
                    _vec_rows([p["pool_b"][l].reshape(WC), p["pool_scale"][l]]), p["pool_w"][l],
                    p["pre_norm_g"][l], p["post_norm_g"][l], p["rel_bias"][l])
        for l in range(DEPTH)], axis=0)


def _unpack_small_params(packed):
    layers = [_unpack_small(packed[l * SMALL_ROWS:(l + 1) * SMALL_ROWS]) for l in range(DEPTH)]
    return {k: jnp.stack([layers[l][k] for l in range(DEPTH)]) for k in layers[0]}


def kernel(x, pre_norm_g, post_norm_g, w_in, conv_dw, conv_dw_b, conv_ln_g, conv_ln_b, w_conv_out, rel_bias, w_attn_out, pool_w, pool_b, pool_scale, w_pool_out, w_out, loss_target, m_pre_norm_g, m_post_norm_g, m_w_in, m_conv_dw, m_conv_dw_b, m_conv_ln_g, m_conv_ln_b, m_w_conv_out, m_rel_bias, m_w_attn_out, m_pool_w, m_pool_b, m_pool_scale, m_w_pool_out, m_w_out, v_pre_norm_g, v_post_norm_g, v_w_in, v_conv_dw, v_conv_dw_b, v_conv_ln_g, v_conv_ln_b, v_w_conv_out, v_rel_bias, v_w_attn_out, v_pool_w, v_pool_b, v_pool_scale, v_w_pool_out, v_w_out):
    BL, SEQ, _ = x.shape
    T = BL * SEQ
    w = dict(pre_norm_g=pre_norm_g, post_norm_g=post_norm_g, w_in=w_in, conv_dw=conv_dw, conv_dw_b=conv_dw_b,
             conv_ln_g=conv_ln_g, conv_ln_b=conv_ln_b, w_conv_out=w_conv_out, rel_bias=rel_bias, w_attn_out=w_attn_out,
             pool_w=pool_w, pool_b=pool_b, pool_scale=pool_scale, w_pool_out=w_pool_out, w_out=w_out)
    m = dict(pre_norm_g=m_pre_norm_g, post_norm_g=m_post_norm_g, w_in=m_w_in, conv_dw=m_conv_dw, conv_dw_b=m_conv_dw_b,
             conv_ln_g=m_conv_ln_g, conv_ln_b=m_conv_ln_b, w_conv_out=m_w_conv_out, rel_bias=m_rel_bias,
             w_attn_out=m_w_attn_out, pool_w=m_pool_w, pool_b=m_pool_b, pool_scale=m_pool_scale,
             w_pool_out=m_w_pool_out, w_out=m_w_out)
    v = dict(pre_norm_g=v_pre_norm_g, post_norm_g=v_post_norm_g, w_in=v_w_in, conv_dw=v_conv_dw, conv_dw_b=v_conv_dw_b,
             conv_ln_g=v_conv_ln_g, conv_ln_b=v_conv_ln_b, w_conv_out=v_w_conv_out, rel_bias=v_rel_bias,
             w_attn_out=v_w_attn_out, pool_w=v_pool_w, pool_b=v_pool_b, pool_scale=v_pool_scale,
             w_pool_out=v_w_pool_out, w_out=v_w_out)

    names = BIG + ("conv_dw",)
    gathered = dict(zip(names, _gather_weights([w[k].astype(bf16) for k in BIG] + [conv_dw])))
    conv_dw_full = jnp.transpose(gathered["conv_dw"], (0, 2, 1, 3)).reshape(DEPTH, CONV_K, WC)
    lw = _full_weights(gathered, pre_norm_g, post_norm_g, conv_dw_full, conv_dw_b, conv_ln_g, conv_ln_b, rel_bias,
                       pool_w, pool_b, pool_scale)

    sq, grad_x, big, small = _local_step(x.reshape(T, D), loss_target.reshape(T, D), lw, BL, SEQ)
    loss = lax.psum(0.5 * jnp.sum(sq) / float(D), ("x", "y", "c"))

    flat = [big[l][k] for k in BIG for l in range(DEPTH)]
    slots = _scatter_grads(flat)
    halves = [_sum_slots("sum_grads", s) for s in slots]
    shared = _share_halves([halves[i * DEPTH:(i + 1) * DEPTH] for i in range(len(BIG))])
    grads, deltas, new_m, new_v = {}, {}, {}, {}
    for k, g4 in zip(BIG, shared):
        g = g4.reshape(DEPTH, g4.shape[2] * 2, g4.shape[3])
        if k == "w_out":
            g = jnp.transpose(g, (0, 2, 1))
        grads[k] = g
        shape = w[k].shape
        flat2 = lambda a: a.reshape(shape[0] * shape[1], shape[2])
        d_, nm_, nv_ = _adamw("adamw_big", flat2(g), flat2(w[k]), flat2(m[k]), flat2(v[k]))
        deltas[k], new_m[k], new_v[k] = d_.reshape(shape), nm_.reshape(shape), nv_.reshape(shape)

    gsmall = _sum_slots("sum_small", _gather_small(jnp.concatenate(small, axis=0)))
    d_, nm_, nv_ = _adamw("adamw_small", gsmall, _pack_small_params(w), _pack_small_params(m), _pack_small_params(v))
    gs, ds, ms, vs = (_unpack_small_params(a) for a in (gsmall, d_, nm_, nv_))
    for k in SMALL:
        grads[k], deltas[k], new_m[k], new_v[k] = gs[k], ds[k], ms[k], vs[k]
    chip = 2 * lax.axis_index("x") + lax.axis_index("y")
    g_dw = lax.dynamic_slice_in_dim(gs["conv_dw"], chip * GD, GD, axis=2)
    flat2 = lambda a: a.reshape(DEPTH * CONV_K, GD)
    d_, nm_, nv_ = _adamw("adamw_conv_dw", flat2(g_dw), flat2(conv_dw), flat2(m["conv_dw"]), flat2(v["conv_dw"]))
    grads["conv_dw"] = g_dw
    deltas["conv_dw"], new_m["conv_dw"], new_v["conv_dw"] = (a.reshape(conv_dw.shape) for a in (d_, nm_, nv_))

    return (loss, grad_x.reshape(x.shape), *[grads[k] for k in ORDER], *[deltas[k] for k in ORDER],
            *[new_m[k] for k in ORDER], *[new_v[k] for k in ORDER])
```

```python
import numpy as np
import jax
import jax.numpy as jnp
from jax import lax
from jax.experimental import pallas as pl
from jax.experimental.pallas import tpu as pltpu

f32 = jnp.float32
bf16 = jnp.bfloat16

D = 1024
DEPTH = 2
WC = 512
HEAD_DIM = 64
CHUNK = 64
LEFT_CHUNKS = 8
KEY_PAD = LEFT_CHUNKS * CHUNK
MAX_REL = 256
CONV_K = 31
POOL_WINDOWS = (2, 4, 8, 16)
GD = 128
NCOL = 7680
EPS = 1e-6
NEG_INF = -1e30
COL_A, COL_B, COL_CG, COL_Q, COL_K, COL_V, COL_AG, COL_PI, COL_PG, COL_GM = (
    0, 512, 1024, 1536, 2048, 2560, 3072, 3584, 4096, 4608)

ADAM_LR = 0.001
ADAM_B1 = 0.9
ADAM_B2 = 0.999
ADAM_EPS = 1e-08
ADAM_WD = 0.01
ADAM_STEP = 10

QG = 256
KW = KEY_PAD + QG
CT = 128
HALO = 32
PHALO = 16
N_CHIPS = 4
N_DEV = 8
VMEM_LIMIT = 56 * 1024 * 1024
MESH = pl.DeviceIdType.MESH
ANY = pl.BlockSpec(memory_space=pl.ANY)

DZ_BLOCKS = 18
DZ_CONV, DZ_ATTN, DZ_POOL, DZ_GM = 0, 4, 8, 12


def _dz_block(c):
    return c + (c >= 3).astype(jnp.int32) + 2 * (c >= 9).astype(jnp.int32)


def _params(sem=None):
    return pltpu.CompilerParams(dimension_semantics=sem, vmem_limit_bytes=VMEM_LIMIT)


def _sig(x):
    return 1.0 / (1.0 + jnp.exp(-x))


def _dsilu(x, s):
    return s * (1.0 + x * (1.0 - s))


def _colsum(x):
    return jnp.sum(x, axis=0, keepdims=True)


def _rms_pre(x2, g):
    T = x2.shape[0]
    tm = 512

    def body(x_ref, g_ref, h_ref, ht_ref):
        x = x_ref[...]
        r = lax.rsqrt(jnp.mean(x * x, axis=-1, keepdims=True) + EPS)
        h = (x * r) * g_ref[...]
        h_ref[...] = h.astype(bf16)
        ht_ref[...] = h.T.astype(bf16)

    row = pl.BlockSpec((tm, D), lambda i: (i, 0))
    vec = pl.BlockSpec((1, D), lambda i: (0, 0))
    return pl.pallas_call(
        body, grid=(T // tm,), in_specs=[row, vec], out_specs=[row, pl.BlockSpec((D, tm), lambda i: (0, i))],
        out_shape=[jax.ShapeDtypeStruct((T, D), bf16), jax.ShapeDtypeStruct((D, T), bf16)], name="rms_pre",
        compiler_params=_params(("parallel",)))(x2, g)


def _post_fwd(y, x2, g):
    T = x2.shape[0]
    tm = 512

    def body(y_ref, x_ref, g_ref, o_ref):
        y = y_ref[...]
        r = lax.rsqrt(jnp.mean(y * y, axis=-1, keepdims=True) + EPS)
        o_ref[...] = x_ref[...] + (y * r) * g_ref[...]

    row = pl.BlockSpec((tm, D), lambda i: (i, 0))
    vec = pl.BlockSpec((1, D), lambda i: (0, 0))
    return pl.pallas_call(
        body, grid=(T // tm,), in_specs=[row, row, vec], out_specs=row,
        out_shape=jax.ShapeDtypeStruct((T, D), f32), name="post_fwd",
        compiler_params=_params(("parallel",)))(y, x2, g)


def _loss_head(out, tgt):
    T = out.shape[0]
    tm = 512

    def body(o_ref, t_ref, d_ref, l_ref):
        e = o_ref[...] - t_ref[...]
        d_ref[...] = e / float(D)

        @pl.when(pl.program_id(0) == 0)
        def _():
            l_ref[...] = jnp.zeros_like(l_ref)

        l_ref[...] += _colsum(e * e)

    row = pl.BlockSpec((tm, D), lambda i: (i, 0))
    vec = pl.BlockSpec((1, D), lambda i: (0, 0))
    return pl.pallas_call(
        body, grid=(T // tm,), in_specs=[row, row], out_specs=[row, vec],
        out_shape=[jax.ShapeDtypeStruct((T, D), f32), jax.ShapeDtypeStruct((1, D), f32)],
        name="loss_head", compiler_params=_params(("arbitrary",)))(out, tgt)


def _post_bwd(dout, y, g):
    T = y.shape[0]
    tm = 512

    def body(d_ref, y_ref, g_ref, dy_ref, dg_ref):
        y = y_ref[...]
        d = d_ref[...]
        r = lax.rsqrt(jnp.mean(y * y, axis=-1, keepdims=True) + EPS)
        yn = y * r
        dyn = d * g_ref[...]
        dy = r * (dyn - yn * jnp.mean(dyn * yn, axis=-1, keepdims=True))
        dy_ref[...] = dy.astype(bf16)

        @pl.when(pl.program_id(0) == 0)
        def _():
            dg_ref[...] = jnp.zeros_like(dg_ref)

        dg_ref[...] += _colsum(d * yn)

    row = pl.BlockSpec((tm, D), lambda i: (i, 0))
    vec = pl.BlockSpec((1, D), lambda i: (0, 0))
    return pl.pallas_call(
        body, grid=(T // tm,), in_specs=[row, row, vec], out_specs=[row, vec],
        out_shape=[jax.ShapeDtypeStruct((T, D), bf16), jax.ShapeDtypeStruct((1, D), f32)],
        name="post_bwd", compiler_params=_params(("arbitrary",)))(dout, y, g)


def _pre_bwd(dh, x2, g, dout):
    T = x2.shape[0]
    tm = 512

    def body(dh_ref, x_ref, g_ref, d_ref, dx_ref, dg_ref):
        x = x_ref[...]
        dh_ = dh_ref[...]
        r = lax.rsqrt(jnp.mean(x * x, axis=-1, keepdims=True) + EPS)
        xn = x * r
        dxn = dh_ * g_ref[...]
        dx_ref[...] = r * (dxn - xn * jnp.mean(dxn * xn, axis=-1, keepdims=True)) + d_ref[...]

        @pl.when(pl.program_id(0) == 0)
        def _():
            dg_ref[...] = jnp.zeros_like(dg_ref)

        dg_ref[...] += _colsum(dh_ * xn)

    row = pl.BlockSpec((tm, D), lambda i: (i, 0))
    vec = pl.BlockSpec((1, D), lambda i: (0, 0))
    return pl.pallas_call(
        body, grid=(T // tm,), in_specs=[row, row, vec, row], out_specs=[row, vec],
        out_shape=[jax.ShapeDtypeStruct((T, D), f32), jax.ShapeDtypeStruct((1, D), f32)],
        name="pre_bwd", compiler_params=_params(("arbitrary",)))(dh, x2, g, dout)


def _mat_spec(arr, tr, tc, rc, layer):
    if arr.ndim == 2:
        return pl.BlockSpec((tr, tc), rc)

    def index(i, j, k):
        r, c = rc(i, j, k)
        return (layer, r, c)

    return pl.BlockSpec((None, tr, tc), index)


def _mm(name, a, b, mode, m, n, k, tm, tn, tk, out_dtype, layer=None):
    nk = k // tk
    assert m % tm == 0 and n % tn == 0 and k % tk == 0
    if mode == "nn":
        a_spec = _mat_spec(a, tm, tk, lambda i, j, kk: (i, kk), layer)
        b_spec = _mat_spec(b, tk, tn, lambda i, j, kk: (kk, j), layer)
        dn = (((1,), (0,)), ((), ()))
    elif mode == "nt":
        a_spec = _mat_spec(a, tm, tk, lambda i, j, kk: (i, kk), layer)
        b_spec = _mat_spec(b, tn, tk, lambda i, j, kk: (j, kk), layer)
        dn = (((1,), (1,)), ((), ()))
    else:
        a_spec = _mat_spec(a, tk, tm, lambda i, j, kk: (kk, i), layer)
        b_spec = _mat_spec(b, tk, tn, lambda i, j, kk: (kk, j), layer)
        dn = (((0,), (0,)), ((), ()))

    def body(a_ref, b_ref, o_ref, acc_ref):
        p = lax.dot_general(a_ref[...].astype(bf16), b_ref[...].astype(bf16), dn, preferred_element_type=f32)
        if nk == 1:
            o_ref[...] = p.astype(o_ref.dtype)
        else:
            kk = pl.program_id(2)

            @pl.when(kk == 0)
            def _():
                acc_ref[...] = p

            @pl.when(kk > 0)
            def _():
                acc_ref[...] += p

            @pl.when(kk == nk - 1)
            def _():
                o_ref[...] = acc_ref[...].astype(o_ref.dtype)

    acc_shape = (tm, tn) if nk > 1 else (8, 128)
    return pl.pallas_call(
        body, grid=(m // tm, n // tn, nk), in_specs=[a_spec, b_spec],
        out_specs=pl.BlockSpec((tm, tn), lambda i, j, kk: (i, j)),
        out_shape=jax.ShapeDtypeStruct((m, n), out_dtype),
        scratch_shapes=[pltpu.VMEM(acc_shape, f32)], name=name,
        compiler_params=_params(("parallel", "parallel", "arbitrary")))(a, b)


DZ_SPANS = ((DZ_CONV, 3), (DZ_ATTN, 4), (DZ_POOL, 2), (DZ_GM, 6))


def _mm_dh(dz, w_in_t, layer):
    T = dz.shape[1]
    tm = 512

    def body(conv_ref, attn_ref, pool_ref, gm_ref, w_ref, o_ref):
        acc = None
        col = 0
        for ref, (_, blocks) in zip((conv_ref, attn_ref, pool_ref, gm_ref), DZ_SPANS):
            for b in range(blocks):
                p = jnp.dot(ref[b], w_ref[col * WC:(col + 1) * WC, :], preferred_element_type=f32)
                acc = p if acc is None else acc + p
                col += 1
        o_ref[...] = acc

    spans = [pl.BlockSpec((blocks, tm, WC), lambda i, first=first, blocks=blocks: (first // blocks, i, 0))
             for first, blocks in DZ_SPANS]
    return pl.pallas_call(
        body, grid=(T // tm,),
        in_specs=spans + [pl.BlockSpec((None, NCOL, D), lambda i: (layer, 0, 0), pipeline_mode=pl.Buffered(1))],
        out_specs=pl.BlockSpec((tm, D), lambda i: (i, 0)), out_shape=jax.ShapeDtypeStruct((T, D), f32),
        name="mm_dh", compiler_params=_params(("parallel",)))(dz, dz, dz, dz, w_in_t)


def _mm_dw_in(ht, dz):
    T = dz.shape[1]

    def body(ht_ref, dz_ref, o_ref):
        o_ref[...] = jnp.dot(ht_ref[...], dz_ref[...], preferred_element_type=f32).astype(bf16)

    return pl.pallas_call(
        body, grid=(NCOL // WC,),
        in_specs=[pl.BlockSpec((D, T), lambda j: (0, 0), pipeline_mode=pl.Buffered(1)),
                  pl.BlockSpec((None, T, WC), lambda j: (_dz_block(j), 0, 0))],
        out_specs=pl.BlockSpec((D, WC), lambda j: (0, j)), out_shape=jax.ShapeDtypeStruct((D, NCOL), bf16),
        name="mm_dw_in", compiler_params=_params(("parallel",)))(ht, dz)


def _conv_delays():
    return [(8 * a + b, a, b) for b in range(8) for a in range(4) if 8 * a + b < CONV_K]


def _conv_taps(win, dw_ref):
    rolled = {}
    acc = None
    for d, a, b in _conv_delays():
        if b not in rolled:
            rolled[b] = win if b == 0 else pltpu.roll(win, b, axis=0)
        term = rolled[b][HALO - 8 * a:HALO - 8 * a + CT, :] * dw_ref[pl.ds(CONV_K - 1 - d, 1), :]
        acc = term if acc is None else acc + term
    return acc, rolled


def _conv_fwd(z, dw32, cvec, BL, SEQ):
    T = BL * SEQ
    nct = SEQ // CT

    def body(a_ref, b_ref, cg_ref, dw_ref, vec_ref, o_ref, p_ref):
        p_ref[pl.ds(0, HALO), :] = jnp.zeros((HALO, WC), f32)

        def glu(c, carry):
            r0 = pl.multiple_of(c * CT, CT)
            p_ref[pl.ds(r0 + HALO, CT), :] = a_ref[pl.ds(r0, CT), :] * _sig(b_ref[pl.ds(r0, CT), :])
            return carry

        lax.fori_loop(0, nct, glu, 0)

        def step(c, carry):
            r0 = pl.multiple_of(c * CT, CT)
            u1, _ = _conv_taps(p_ref[pl.ds(r0, CT + HALO), :], dw_ref)
            u1 = u1 + vec_ref[0:1, :]
            xc = u1 - jnp.mean(u1, axis=-1, keepdims=True)
            rs = lax.rsqrt(jnp.mean(xc * xc, axis=-1, keepdims=True) + EPS)
            u2 = (xc * rs) * vec_ref[1:2, :] + vec_ref[2:3, :]
            cg = cg_ref[pl.ds(r0, CT), :]
            o_ref[pl.ds(r0, CT), :] = ((u2 * _sig(u2)) * (cg * _sig(cg))).astype(bf16)
            return carry

        lax.fori_loop(0, nct, step, 0)

    def zs(col):
        return pl.BlockSpec((SEQ, WC), lambda b: (b, col // WC))

    return pl.pallas_call(
        body, grid=(BL,),
        in_specs=[zs(COL_A), zs(COL_B), zs(COL_CG), pl.BlockSpec((32, WC), lambda b: (0, 0)),
                  pl.BlockSpec((8, WC), lambda b: (0, 0))],
        out_specs=pl.BlockSpec((SEQ, WC), lambda b: (b, 0)),
        out_shape=jax.ShapeDtypeStruct((T, WC), bf16),
        scratch_shapes=[pltpu.VMEM((SEQ + HALO, WC), f32)], name="conv_fwd",
        compiler_params=_params(("parallel",)))(z, z, z, dw32, cvec)


def _conv_bwd(z, dcv, dz, dw32, cvec, BL, SEQ):
    nct = SEQ // CT

    def body(a_ref, b_ref, cg_ref, dcv_ref, dzin_ref, dw_ref, vec_ref, dz_ref, ddw_ref, dvec_ref, p_ref, q_ref):
        @pl.when(pl.program_id(0) == 0)
        def _():
            ddw_ref[...] = jnp.zeros_like(ddw_ref)
            dvec_ref[...] = jnp.zeros_like(dvec_ref)

        p_ref[pl.ds(0, HALO), :] = jnp.zeros((HALO, WC), f32)
        q_ref[pl.ds(SEQ, HALO), :] = jnp.zeros((HALO, WC), f32)

        def glu(c, carry):
            r0 = pl.multiple_of(c * CT, CT)
            p_ref[pl.ds(r0 + HALO, CT), :] = a_ref[pl.ds(r0, CT), :] * _sig(b_ref[pl.ds(r0, CT), :])
            return carry

        lax.fori_loop(0, nct, glu, 0)

        def step(c, carry):
            r0 = pl.multiple_of(c * CT, CT)
            u1, rolled = _conv_taps(p_ref[pl.ds(r0, CT + HALO), :], dw_ref)
            u1 = u1 + vec_ref[0:1, :]
            xc = u1 - jnp.mean(u1, axis=-1, keepdims=True)
            rs = lax.rsqrt(jnp.mean(xc * xc, axis=-1, keepdims=True) + EPS)
            nrm = xc * rs
            u2 = nrm * vec_ref[1:2, :] + vec_ref[2:3, :]
            s2 = _sig(u2)
            u3 = u2 * s2
            cg = cg_ref[pl.ds(r0, CT), :]
            scg = _sig(cg)
            dcv_ = dcv_ref[pl.ds(r0, CT), :]
            dz_ref[2, pl.ds(r0, CT), :] = (dcv_ * u3 * _dsilu(cg, scg)).astype(bf16)
            du2 = dcv_ * (cg * scg) * _dsilu(u2, s2)
            dvec_ref[1:2, :] += _colsum(du2 * nrm)
            dvec_ref[2:3, :] += _colsum(du2)
            dn = du2 * vec_ref[1:2, :]
            du1 = rs * (dn - jnp.mean(dn, axis=-1, keepdims=True)
                        - nrm * jnp.mean(dn * nrm, axis=-1, keepdims=True))
            dvec_ref[0:1, :] += _colsum(du1)
            q_ref[pl.ds(r0, CT), :] = du1
            for d, a, b in _conv_delays():
                row = CONV_K - 1 - d
                ddw_ref[pl.ds(row, 1), :] += _colsum(du1 * rolled[b][HALO - 8 * a:HALO - 8 * a + CT, :])
            return carry

        lax.fori_loop(0, nct, step, 0)

        def back(c, carry):
            r0 = pl.multiple_of(c * CT, CT)
            wq = q_ref[pl.ds(r0, CT + HALO), :]
            up = {}
            acc = None
            for d, a, b in _conv_delays():
                if b not in up:
                    up[b] = wq if b == 0 else pltpu.roll(wq, CT + HALO - b, axis=0)
                term = up[b][8 * a:8 * a + CT, :] * dw_ref[pl.ds(CONV_K - 1 - d, 1), :]
                acc = term if acc is None else acc + term
            a_ = a_ref[pl.ds(r0, CT), :]
            sb = _sig(b_ref[pl.ds(r0, CT), :])
            dz_ref[0, pl.ds(r0, CT), :] = (acc * sb).astype(bf16)
            dz_ref[1, pl.ds(r0, CT), :] = (acc * a_ * sb * (1.0 - sb)).astype(bf16)
            return carry

        lax.fori_loop(0, nct, back, 0)

    def zs(col):
        return pl.BlockSpec((SEQ, WC), lambda b: (b, col // WC), pipeline_mode=pl.Buffered(1))

    def const(r):
        return pl.BlockSpec((r, WC), lambda b: (0, 0))

    return pl.pallas_call(
        body, grid=(BL,),
        in_specs=[zs(COL_A), zs(COL_B), zs(COL_CG),
                  pl.BlockSpec((SEQ, WC), lambda b: (b, 0), pipeline_mode=pl.Buffered(1)), ANY, const(32), const(8)],
        out_specs=[pl.BlockSpec((3, SEQ, WC), lambda b: (DZ_CONV // 3, b, 0)), const(32), const(8)],
        out_shape=[jax.ShapeDtypeStruct(dz.shape, bf16), jax.ShapeDtypeStruct((32, WC), f32),
                   jax.ShapeDtypeStruct((8, WC), f32)],
        scratch_shapes=[pltpu.VMEM((SEQ + HALO, WC), f32), pltpu.VMEM((SEQ + HALO, WC), f32)],
        input_output_aliases={4: 0}, name="conv_bwd",
        compiler_params=_params(("arbitrary",)))(z, z, z, dcv, dz, dw32, cvec)


def _pool_counts(r0):
    t1 = r0 + 1 + lax.broadcasted_iota(jnp.int32, (CT, 1), 0)
    return [jnp.minimum(t1, w).astype(f32) for w in POOL_WINDOWS]


def _pool_sums(win, forward):
    n = CT + PHALO

    def sh(x, s):
        return pltpu.roll(x, (n - s) if forward else s, axis=0)

    s2 = win + sh(win, 1)
    s4 = s2[:, GD:] + sh(s2[:, GD:], 2)
    s8 = s4[:, GD:] + sh(s4[:, GD:], 4)
    s16 = s8[:, GD:] + sh(s8[:, GD:], 8)
    lo = 0 if forward else PHALO
    return [s[lo:lo + CT, :GD] for s in (s2, s4, s8, s16)]


def _pool_fwd(z, pw, pvec, BL, SEQ):
    T = BL * SEQ
    nct = SEQ // CT

    def body(pi_ref, pg_ref, pw_ref, vec_ref, o_ref, p_ref):
        p_ref[pl.ds(0, PHALO), :] = jnp.zeros((PHALO, WC), f32)

        def fill(c, carry):
            r0 = pl.multiple_of(c * CT, CT)
            p_ref[pl.ds(r0 + PHALO, CT), :] = pi_ref[pl.ds(r0, CT), :]
            return carry

        lax.fori_loop(0, nct, fill, 0)

        def step(c, carry):
            r0 = pl.multiple_of(c * CT, CT)
            sums = _pool_sums(p_ref[pl.ds(r0, CT + PHALO), :], False)
            cnt = _pool_counts(r0)
            pin = pi_ref[pl.ds(r0, CT), :]
            mixed = []
            for g in range(4):
                pooled = sums[g] / cnt[g] - pin[:, g * GD:(g + 1) * GD]
                mixed.append(jnp.dot(pooled.astype(bf16), pw_ref[g], preferred_element_type=f32))
            m0 = jnp.concatenate(mixed, axis=1) + vec_ref[0:1, :]
            pg = pg_ref[pl.ds(r0, CT), :]
            o_ref[pl.ds(r0, CT), :] = ((m0 * vec_ref[1:2, :]) * (pg * _sig(pg))).astype(bf16)
            return carry

        lax.fori_loop(0, nct, step, 0)

    def zs(col):
        return pl.BlockSpec((SEQ, WC), lambda b: (b, col // WC))

    return pl.pallas_call(
        body, grid=(BL,),
        in_specs=[zs(COL_PI), zs(COL_PG), pl.BlockSpec((4, GD, GD), lambda b: (0, 0, 0)),
                  pl.BlockSpec((8, WC), lambda b: (0, 0))],
        out_specs=pl.BlockSpec((SEQ, WC), lambda b: (b, 0)),
        out_shape=jax.ShapeDtypeStruct((T, WC), bf16),
        scratch_shapes=[pltpu.VMEM((SEQ + PHALO, WC), f32)], name="pool_fwd",
        compiler_params=_params(("parallel",)))(z, z, pw, pvec)


def _pool_bwd(z, dpl, dz, pw, pvec, BL, SEQ):
    nct = SEQ // CT

    def body(pi_ref, pg_ref, dpl_ref, dzin_ref, pw_ref, vec_ref, dz_ref, dpw_ref, dvec_ref, p_ref, e_ref, dp_ref):
        @pl.when(pl.program_id(0) == 0)
        def _():
            dpw_ref[...] = jnp.zeros_like(dpw_ref)
            dvec_ref[...] = jnp.zeros_like(dvec_ref)

        p_ref[pl.ds(0, PHALO), :] = jnp.zeros((PHALO, WC), f32)
        e_ref[pl.ds(SEQ, PHALO), :] = jnp.zeros((PHALO, WC), f32)

        def fill(c, carry):
            r0 = pl.multiple_of(c * CT, CT)
            p_ref[pl.ds(r0 + PHALO, CT), :] = pi_ref[pl.ds(r0, CT), :]
            return carry

        lax.fori_loop(0, nct, fill, 0)

        def step(c, carry):
            r0 = pl.multiple_of(c * CT, CT)
            sums = _pool_sums(p_ref[pl.ds(r0, CT + PHALO), :], False)
            cnt = _pool_counts(r0)
            pin = pi_ref[pl.ds(r0, CT), :]
            pooled = [(sums[g] / cnt[g] - pin[:, g * GD:(g + 1) * GD]).astype(bf16) for g in range(4)]
            m0 = jnp.concatenate(
                [jnp.dot(pooled[g], pw_ref[g], preferred_element_type=f32) for g in range(4)], axis=1) + vec_ref[0:1, :]
            scale = vec_ref[1:2, :]
            pg = pg_ref[pl.ds(r0, CT), :]
            spg = _sig(pg)
            dpl_ = dpl_ref[pl.ds(r0, CT), :]
            dmixed = dpl_ * (pg * spg)
            dz_ref[1, pl.ds(r0, CT), :] = (dpl_ * (m0 * scale) * _dsilu(pg, spg)).astype(bf16)
            dvec_ref[1:2, :] += _colsum(dmixed * m0)
            dm0 = dmixed * scale
            dvec_ref[0:1, :] += _colsum(dm0)
            dps, es = [], []
            for g in range(4):
                dm0g = dm0[:, g * GD:(g + 1) * GD].astype(bf16)
                dpw_ref[g] += lax.dot_general(pooled[g], dm0g, (((0,), (0,)), ((), ())), preferred_element_type=f32)
                dpg = lax.dot_general(dm0g, pw_ref[g], (((1,), (1,)), ((), ())), preferred_element_type=f32)
                dps.append(dpg)
                es.append(dpg / cnt[g])
            dp_ref[pl.ds(r0, CT), :] = jnp.concatenate(dps, axis=1)
            e_ref[pl.ds(r0, CT), :] = jnp.concatenate(es, axis=1)
            return carry

        lax.fori_loop(0, nct, step, 0)

        def back(c, carry):
            r0 = pl.multiple_of(c * CT, CT)
            fs = _pool_sums(e_ref[pl.ds(r0, CT + PHALO), :], True)
            dz_ref[0, pl.ds(r0, CT), :] = (jnp.concatenate(fs, axis=1) - dp_ref[pl.ds(r0, CT), :]).astype(bf16)
            return carry

        lax.fori_loop(0, nct, back, 0)

    def zs(col):
        return pl.BlockSpec((SEQ, WC), lambda b: (b, col // WC), pipeline_mode=pl.Buffered(1))

    return pl.pallas_call(
        body, grid=(BL,),
        in_specs=[zs(COL_PI), zs(COL_PG),
                  pl.BlockSpec((SEQ, WC), lambda b: (b, 0), pipeline_mode=pl.Buffered(1)), ANY,
                  pl.BlockSpec((4, GD, GD), lambda b: (0, 0, 0)), pl.BlockSpec((8, WC), lambda b: (0, 0))],
        out_specs=[pl.BlockSpec((2, SEQ, WC), lambda b: (DZ_POOL // 2, b, 0)),
                   pl.BlockSpec((4, GD, GD), lambda b: (0, 0, 0)), pl.BlockSpec((8, WC), lambda b: (0, 0))],
        out_shape=[jax.ShapeDtypeStruct(dz.shape, bf16), jax.ShapeDtypeStruct((4, GD, GD), f32),
                   jax.ShapeDtypeStruct((8, WC), f32)],
        scratch_shapes=[pltpu.VMEM((SEQ + PHALO, WC), f32), pltpu.VMEM((SEQ + PHALO, WC), f32),
                        pltpu.VMEM((SEQ, WC), f32)],
        input_output_aliases={3: 0}, name="pool_bwd",
        compiler_params=_params(("arbitrary",)))(z, z, dpl, dz, pw, pvec)


def _attn_prologue(q_ref, k_ref, v_ref, qs0, qs1, kp, vp, SEQ):
    head0 = lax.broadcasted_iota(jnp.int32, (1, 2 * HEAD_DIM), 1) < HEAD_DIM
    kp[pl.ds(0, KEY_PAD), :] = jnp.zeros((KEY_PAD, 2 * HEAD_DIM), bf16)
    vp[pl.ds(0, KEY_PAD), :] = jnp.zeros((KEY_PAD, 2 * HEAD_DIM), bf16)

    def fill(g, carry):
        r0 = pl.multiple_of(g * QG, QG)
        q = q_ref[pl.ds(r0, QG), :] * (HEAD_DIM ** -0.5)
        qs0[pl.ds(r0, QG), :] = jnp.where(head0, q, 0.0).astype(bf16)
        qs1[pl.ds(r0, QG), :] = jnp.where(head0, 0.0, q).astype(bf16)
        kp[pl.ds(r0 + KEY_PAD, QG), :] = k_ref[pl.ds(r0, QG), :].astype(bf16)
        vp[pl.ds(r0 + KEY_PAD, QG), :] = v_ref[pl.ds(r0, QG), :].astype(bf16)
        return carry

    lax.fori_loop(0, SEQ // QG, fill, 0)
    return head0


def _attn_probs(qh, kw, bias, r0):
    s = lax.dot_general(qh, kw, (((1,), (1,)), ((), ())), preferred_element_type=f32) + bias
    key_pos = r0 - KEY_PAD + lax.broadcasted_iota(jnp.int32, (1, KW), 1)
    s = jnp.where(key_pos >= 0, s, NEG_INF)
    e = jnp.exp(s - jnp.max(s, axis=-1, keepdims=True))
    return e * (1.0 / jnp.sum(e, axis=-1, keepdims=True))


def _attn_fwd(z, bm, BL, SEQ):
    T = BL * SEQ
    W2 = 2 * HEAD_DIM

    def body(q_ref, k_ref, v_ref, ag_ref, bm_ref, o_ref, qs0, qs1, kp, vp):
        head0 = _attn_prologue(q_ref, k_ref, v_ref, qs0, qs1, kp, vp, SEQ)

        def group(g, carry):
            r0 = pl.multiple_of(g * QG, QG)
            kw = kp[pl.ds(r0, KW), :]
            vw = vp[pl.ds(r0, KW), :]
            outs = []
            for hh, qs in enumerate((qs0, qs1)):
                p = _attn_probs(qs[pl.ds(r0, QG), :], kw, bm_ref[hh], r0)
                outs.append(jnp.dot(p.astype(bf16), vw, preferred_element_type=f32))
            o = jnp.where(head0, outs[0], outs[1])
            ag = ag_ref[pl.ds(r0, QG), :]
            o_ref[pl.ds(r0, QG), :] = (o * (ag * _sig(ag))).astype(bf16)
            return carry

        lax.fori_loop(0, SEQ // QG, group, 0)

    def zs(col):
        return pl.BlockSpec((SEQ, W2), lambda b, hp: (b, col // W2 + hp))

    return pl.pallas_call(
        body, grid=(BL, WC // W2),
        in_specs=[zs(COL_Q), zs(COL_K), zs(COL_V), zs(COL_AG), pl.BlockSpec((2, QG, KW), lambda b, hp: (hp, 0, 0))],
        out_specs=pl.BlockSpec((SEQ, W2), lambda b, hp: (b, hp)),
        out_shape=jax.ShapeDtypeStruct((T, WC), bf16),
        scratch_shapes=[pltpu.VMEM((SEQ, W2), bf16), pltpu.VMEM((SEQ, W2), bf16),
                        pltpu.VMEM((SEQ + KEY_PAD, W2), bf16), pltpu.VMEM((SEQ + KEY_PAD, W2), bf16)],
        name="attn_fwd", compiler_params=_params(("parallel", "parallel")))(z, z, z, z, bm)


def _attn_bwd(z, dat, dz, bm, BL, SEQ):
    W2 = 2 * HEAD_DIM

    def body(q_ref, k_ref, v_ref, ag_ref, dat_ref, dzin_ref, bm_ref, dz_ref, dbm_ref,
             qs0, qs1, kp, vp, do0, do1, dka, dva):
        @pl.when(pl.program_id(1) == 0)
        def _():
            dbm_ref[...] = jnp.zeros_like(dbm_ref)

        head0 = _attn_prologue(q_ref, k_ref, v_ref, qs0, qs1, kp, vp, SEQ)
        dka[...] = jnp.zeros_like(dka)
        dva[...] = jnp.zeros_like(dva)

        def fill(g, carry):
            r0 = pl.multiple_of(g * QG, QG)
            ag = ag_ref[pl.ds(r0, QG), :]
            do = dat_ref[pl.ds(r0, QG), :] * (ag * _sig(ag))
            do0[pl.ds(r0, QG), :] = jnp.where(head0, do, 0.0).astype(bf16)
            do1[pl.ds(r0, QG), :] = jnp.where(head0, 0.0, do).astype(bf16)
            return carry

        lax.fori_loop(0, SEQ // QG, fill, 0)

        def group(g, carry):
            r0 = pl.multiple_of(g * QG, QG)
            kw = kp[pl.ds(r0, KW), :]
            vw = vp[pl.ds(r0, KW), :]
            outs, dqs = [], []
            for hh, (qs, dos) in enumerate(((qs0, do0), (qs1, do1))):
                qh = qs[pl.ds(r0, QG), :]
                doh = dos[pl.ds(r0, QG), :]
                p = _attn_probs(qh, kw, bm_ref[hh], r0)
                pb = p.astype(bf16)
                outs.append(jnp.dot(pb, vw, preferred_element_type=f32))
                dp = lax.dot_general(doh, vw, (((1,), (1,)), ((), ())), preferred_element_type=f32)
                ds_ = p * (dp - jnp.sum(p * dp, axis=-1, keepdims=True))
                dbm_ref[hh] += ds_
                dsb = ds_.astype(bf16)
                dqs.append(jnp.dot(dsb, kw, preferred_element_type=f32))
                dka[pl.ds(r0, KW), :] += lax.dot_general(dsb, qh, (((0,), (0,)), ((), ())), preferred_element_type=f32)
                dva[pl.ds(r0, KW), :] += lax.dot_general(pb, doh, (((0,), (0,)), ((), ())), preferred_element_type=f32)
            o = jnp.where(head0, outs[0], outs[1])
            dq = jnp.where(head0, dqs[0], dqs[1]) * (HEAD_DIM ** -0.5)
            ag = ag_ref[pl.ds(r0, QG), :]
            dz_ref[0, pl.ds(r0, QG), :] = dq.astype(bf16)
            dz_ref[3, pl.ds(r0, QG), :] = (dat_ref[pl.ds(r0, QG), :] * o * _dsilu(ag, _sig(ag))).astype(bf16)
            return carry

        lax.fori_loop(0, SEQ // QG, group, 0)

        def flush(g, carry):
            r0 = pl.multiple_of(g * QG, QG)
            dz_ref[1, pl.ds(r0, QG), :] = dka[pl.ds(r0 + KEY_PAD, QG), :].astype(bf16)
            dz_ref[2, pl.ds(r0, QG), :] = dva[pl.ds(r0 + KEY_PAD, QG), :].astype(bf16)
            return carry

        lax.fori_loop(0, SEQ // QG, flush, 0)

    def zs(col):
        return pl.BlockSpec((SEQ, W2), lambda hp, b: (b, col // W2 + hp))

    return pl.pallas_call(
        body, grid=(WC // W2, BL),
        in_specs=[zs(COL_Q), zs(COL_K), zs(COL_V), zs(COL_AG), pl.BlockSpec((SEQ, W2), lambda hp, b: (b, hp)), ANY,
                  pl.BlockSpec((2, QG, KW), lambda hp, b: (hp, 0, 0))],
        out_specs=[pl.BlockSpec((4, SEQ, W2), lambda hp, b: (DZ_ATTN // 4, b, hp)),
                   pl.BlockSpec((2, QG, KW), lambda hp, b: (hp, 0, 0))],
        out_shape=[jax.ShapeDtypeStruct(dz.shape, bf16), jax.ShapeDtypeStruct((8, QG, KW), f32)],
        scratch_shapes=[pltpu.VMEM((SEQ, W2), bf16), pltpu.VMEM((SEQ, W2), bf16),
                        pltpu.VMEM((SEQ + KEY_PAD, W2), bf16), pltpu.VMEM((SEQ + KEY_PAD, W2), bf16),
                        pltpu.VMEM((SEQ, W2), bf16), pltpu.VMEM((SEQ, W2), bf16),
                        pltpu.VMEM((SEQ + KEY_PAD, W2), f32), pltpu.VMEM((SEQ + KEY_PAD, W2), f32)],
        input_output_aliases={5: 0}, name="attn_bwd",
        compiler_params=_params(("parallel", "arbitrary")))(z, z, z, z, dat, dz, bm)


def _bias_matrix(table):
    n = 2 * MAX_REL
    wd = QG + KW
    e = jnp.concatenate([jnp.broadcast_to(table[:, n:], (8, wd - n)), table[:, n - 1:0:-1], jnp.zeros((8, 1), f32)], axis=1)
    flat = jnp.broadcast_to(e[:, None, :], (8, QG, wd)).reshape(8, QG * wd)
    skew = flat[:, :QG * (wd - 1)].reshape(8, QG, wd - 1)
    vals = skew[:, :, QG - 1:QG - 1 + KW]
    r = np.arange(QG)[:, None] // CHUNK
    j = np.arange(KW)[None, :] // CHUNK
    band = (j >= r) & (j <= r + LEFT_CHUNKS)
    return jnp.where(jnp.asarray(band)[None], vals, NEG_INF)


def _bias_fold(dbm):
    wd = QG + KW
    placed = jnp.pad(dbm, ((0, 0), (0, 0), (QG - 1, 0))).reshape(8, QG * (wd - 1))
    return jnp.pad(placed, ((0, 0), (0, QG))).reshape(8, QG, wd)


def _bias_colsum(folded):
    width = folded.shape[2]

    def body(x_ref, o_ref):
        for h in range(8):
            o_ref[pl.ds(h, 1), :] = _colsum(x_ref[h])

    return pl.pallas_call(body, out_shape=jax.ShapeDtypeStruct((8, width), f32), name="bias_colsum",
                          compiler_params=_params())(folded)


def _bias_table_grad(colsum):
    n = 2 * MAX_REL
    wd = QG + KW
    clipped = jnp.sum(colsum[:, :wd - n], axis=1, keepdims=True)
    return jnp.concatenate([jnp.zeros((8, 1), f32), colsum[:, wd - 2:wd - n - 1:-1], clipped], axis=1)


def _merge_fwd(z, ys):
    T = z.shape[0]
    tm = 512

    def body(g0, g1, g2, y0, y1, y2, o_ref):
        acc = _sig(g0[...]) * y0[...] + _sig(g1[...]) * y1[...] + _sig(g2[...]) * y2[...]
        o_ref[...] = acc.astype(bf16)

    def gs(br):
        return pl.BlockSpec((tm, WC), lambda i, j: (i, (COL_GM + br * D) // WC + j))

    ysp = pl.BlockSpec((tm, WC), lambda i, j: (i, j))
    return pl.pallas_call(
        body, grid=(T // tm, D // WC), in_specs=[gs(0), gs(1), gs(2), ysp, ysp, ysp], out_specs=ysp,
        out_shape=jax.ShapeDtypeStruct((T, D), bf16), name="merge_fwd",
        compiler_params=_params(("parallel", "parallel")))(z, z, z, *ys)


def _merge_bwd(z, dmerged, ys):
    T = z.shape[0]
    tm = 256

    def body(*refs):
        g = refs[0:6]
        dm_ref = refs[6]
        y = refs[7:10]
        dy = refs[10:13]
        dz_ref = refs[13]
        for br in range(3):
            for jh in range(2):
                cols = slice(jh * WC, (jh + 1) * WC)
                s = _sig(g[2 * br + jh][...])
                dm = dm_ref[:, cols]
                dy[br][:, cols] = (dm * s).astype(bf16)
                dz_ref[2 * br + jh] = (dm * y[br][:, cols] * s * (1.0 - s)).astype(bf16)

    def gs(blk):
        return pl.BlockSpec((tm, WC), lambda i: (i, COL_GM // WC + blk))

    row = pl.BlockSpec((tm, D), lambda i: (i, 0))
    return pl.pallas_call(
        body, grid=(T // tm,), in_specs=[gs(b) for b in range(6)] + [row] * 4,
        out_specs=[row, row, row, pl.BlockSpec((6, tm, WC), lambda i: (DZ_GM // 6, i, 0))],
        out_shape=[jax.ShapeDtypeStruct((T, D), bf16)] * 3 + [jax.ShapeDtypeStruct((DZ_BLOCKS, T, WC), bf16)],
        name="merge_bwd", compiler_params=_params(("parallel",)))(*([z] * 6), dmerged, *ys)


def _adamw(name, g, w, m, v):
    R, C = w.shape
    tr = R
    for cand in (512, 256, 248, 128, 64, 32, 16, 8):
        if R % cand == 0 and cand * C * 4 <= 2 * 1024 * 1024:
            tr = cand
            break
    c1 = 1.0 - ADAM_B1
    c2 = 1.0 - ADAM_B2
    bc1 = 1.0 - ADAM_B1 ** ADAM_STEP
    bc2 = 1.0 - ADAM_B2 ** ADAM_STEP

    def body(g_ref, w_ref, m_ref, v_ref, d_ref, nm_ref, nv_ref):
        g_ = g_ref[...]
        nm = ADAM_B1 * m_ref[...] + c1 * g_
        nv = ADAM_B2 * v_ref[...] + c2 * (g_ * g_)
        nm_ref[...] = nm
        nv_ref[...] = nv
        d_ref[...] = -ADAM_LR * ((nm / bc1) / (jnp.sqrt(nv / bc2) + ADAM_EPS) + ADAM_WD * w_ref[...])

    spec = pl.BlockSpec((tr, C), lambda i: (i, 0))
    return pl.pallas_call(
        body, grid=(R // tr,), in_specs=[spec] * 4, out_specs=[spec] * 3,
        out_shape=[jax.ShapeDtypeStruct((R, C), f32)] * 3, name=name,
        compiler_params=_params(("parallel",)))(g, w, m, v)


def _sum_slots(name, parts):
    _, R, C = parts.shape
    tr = R
    for cand in (256, 128, 64, 32, 16, 8):
        if R % cand == 0 and cand * C * 4 * N_DEV <= 8 * 1024 * 1024:
            tr = cand
            break

    def body(p_ref, o_ref):
        acc = p_ref[0].astype(f32)
        for s in range(1, N_DEV):
            acc = acc + p_ref[s].astype(f32)
        o_ref[...] = acc

    return pl.pallas_call(
        body, grid=(R // tr,), in_specs=[pl.BlockSpec((N_DEV, tr, C), lambda i: (0, i, 0))],
        out_specs=pl.BlockSpec((tr, C), lambda i: (i, 0)), out_shape=jax.ShapeDtypeStruct((R, C), f32),
        name=name, compiler_params=_params(("parallel",)))(parts)


def _place():
    x, y, c = lax.axis_index("x"), lax.axis_index("y"), lax.axis_index("c")
    return x, y, c


def _flip(v, bit):
    return 1 - v if bit else v


CHIP_FLIPS = ((1, 0), (0, 1), (1, 1))


class _Sems:
    def __init__(self, send, recv, local):
        self.send, self.recv, self.loc = send, recv, local
        self.pairs = 0
        self.locals = 0

    def pair(self):
        k = self.pairs
        self.pairs += 1
        return self.send.at[k], self.recv.at[k]

    def local(self):
        k = self.locals
        self.locals += 1
        return self.loc.at[k]


def _remote(src, dst, lands, sems, to):
    s, r = sems.pair()
    copy = pltpu.make_async_remote_copy(src_ref=src, dst_ref=dst, send_sem=s, recv_sem=r, device_id=to, device_id_type=MESH)
    wait = pltpu.make_async_remote_copy(src_ref=lands, dst_ref=lands, send_sem=s, recv_sem=r, device_id=to, device_id_type=MESH)
    return copy, wait


def _exchange(name, build, ins, through, fresh, n_remote, n_local):
    n_in, n_thr = len(ins), len(through)

    def body(*refs):
        in_refs = refs[:n_in]
        out_refs = refs[n_in + n_thr:n_in + 2 * n_thr + len(fresh)]
        send, recv, loc = refs[n_in + 2 * n_thr + len(fresh):]
        locals_, remotes, recvs = build(in_refs, out_refs, _Sems(send, recv, loc))
        for cp in locals_ + remotes:
            cp.start()
        for rv in recvs:
            rv.wait_recv()
        for cp in remotes:
            cp.wait_send()
        for cp in locals_:
            cp.wait()

    out_shape = [jax.ShapeDtypeStruct(t.shape, t.dtype) for t in through] + list(fresh)
    return pl.pallas_call(
        body, in_specs=[ANY] * (n_in + n_thr), out_specs=[ANY] * len(out_shape), out_shape=out_shape,
        scratch_shapes=[pltpu.SemaphoreType.DMA((max(n_remote, 1),)), pltpu.SemaphoreType.DMA((max(n_remote, 1),)),
                        pltpu.SemaphoreType.DMA((max(n_local, 1),))],
        input_output_aliases={n_in + i: i for i in range(n_thr)}, name=name)(*ins, *through)


def _gather_weights(shards):
    n = len(shards)

    def over_ici(src, out, sems):
        x, y, c = _place()
        chip = 2 * x + y
        locals_, remotes, recvs = [], [], []
        for a in range(n):
            for half in range(2):
                locals_.append(pltpu.make_async_copy(src[a].at[half], out[a].at[half, chip], sems.local()))
            for fx, fy in CHIP_FLIPS:
                px, py = _flip(x, fx), _flip(y, fy)
                cp, rv = _remote(src[a].at[c], out[a].at[c, chip], out[a].at[c, 2 * px + py], sems, (px, py, c))
                remotes.append(cp)
                recvs.append(rv)
        return locals_, remotes, recvs

    def over_d2d(src, out, sems):
        x, y, c = _place()
        remotes, recvs = [], []
        for a in range(n):
            for fx, fy in CHIP_FLIPS:
                owner = 2 * _flip(x, fx) + _flip(y, fy)
                cp, rv = _remote(out[a].at[c, owner], out[a].at[c, owner], out[a].at[1 - c, owner], sems, (x, y, 1 - c))
                remotes.append(cp)
                recvs.append(rv)
        return [], remotes, recvs

    fresh = [jax.ShapeDtypeStruct((2, N_CHIPS) + s.shape[1:], s.dtype) for s in shards]
    partial = _exchange("gather_weights_ici", over_ici, shards, [], fresh, 3 * n, 2 * n)
    return _exchange("gather_weights_d2d", over_d2d, [], partial, [], 3 * n, 0)


def _scatter_grads(grads):
    n = len(grads)

    def build(src, out, sems):
        x, y, c = _place()

        def piece(a, px, py, pc):
            r2 = src[a].shape[0] // 2
            cw = src[a].shape[1] // N_CHIPS
            return src[a].at[pl.ds(pc * r2, r2), pl.ds((2 * px + py) * cw, cw)]

        locals_, remotes, recvs = [], [], []
        for a in range(n):
            locals_.append(pltpu.make_async_copy(piece(a, x, y, c), out[a].at[0], sems.local()))
            for k in range(1, N_DEV):
                px, py, pc = _flip(x, (k >> 2) & 1), _flip(y, (k >> 1) & 1), _flip(c, k & 1)
                cp, rv = _remote(piece(a, px, py, pc), out[a].at[k], out[a].at[k], sems, (px, py, pc))
                remotes.append(cp)
                recvs.append(rv)
        return locals_, remotes, recvs

    fresh = [jax.ShapeDtypeStruct((N_DEV, g.shape[0] // 2, g.shape[1] // N_CHIPS), g.dtype) for g in grads]
    return _exchange("scatter_grads", build, grads, [], fresh, (N_DEV - 1) * n, n)


def _share_halves(halves):
    n = len(halves)

    def to_sibling(src, out, sems):
        x, y, c = _place()
        remotes, recvs = [], []
        for a in range(n):
            for l in range(DEPTH):
                cp, rv = _remote(src[a * DEPTH + l], out[a].at[l, c], out[a].at[l, 1 - c], sems, (x, y, 1 - c))
                remotes.append(cp)
                recvs.append(rv)
        return [], remotes, recvs

    def own(src, out, sems):
        _, _, c = _place()
        return [pltpu.make_async_copy(src[a * DEPTH + l], out[a].at[l, c], sems.local())
                for a in range(n) for l in range(DEPTH)], [], []

    flat = [h for per_weight in halves for h in per_weight]
    fresh = [jax.ShapeDtypeStruct((DEPTH, 2) + per_weight[0].shape, f32) for per_weight in halves]
    partial = _exchange("share_halves_d2d", to_sibling, flat, [], fresh, n * DEPTH, 0)
    return _exchange("share_halves_own", own, flat, partial, [], 0, n * DEPTH)


def _gather_small(packed):
    def build(src, out, sems):
        x, y, c = _place()
        me = 4 * x + 2 * y + c
        locals_ = [pltpu.make_async_copy(src[0], out[0].at[me], sems.local())]
        remotes, recvs = [], []
        for k in range(1, N_DEV):
            px, py, pc = _flip(x, (k >> 2) & 1), _flip(y, (k >> 1) & 1), _flip(c, k & 1)
            cp, rv = _remote(src[0], out[0].at[me], out[0].at[4 * px + 2 * py + pc], sems, (px, py, pc))
            remotes.append(cp)
            recvs.append(rv)
        return locals_, remotes, recvs

    fresh = [jax.ShapeDtypeStruct((N_DEV,) + packed.shape, f32)]
    return _exchange("gather_small", build, [packed], [], fresh, N_DEV - 1, 1)[0]


def _rows8(v):
    return jnp.pad(v[None, :], ((0, 7), (0, 0)))


def _vec_rows(vs):
    return jnp.pad(jnp.stack(vs), ((0, 8 - len(vs)), (0, 0)))


SMALL_ROWS = 224


def _pack_small(conv_vec, conv_dw, pool_vec, pool_w, pre_g, post_g, rel):
    return jnp.concatenate([
        conv_vec, conv_dw, pool_vec, pool_w.reshape(GD, WC),
        _rows8(pre_g).reshape(16, WC), _rows8(post_g).reshape(16, WC),
        jnp.pad(rel, ((0, 0), (0, D - rel.shape[1]))).reshape(16, WC)], axis=0)


def _unpack_small(p):
    conv_vec, pool_vec = p[0:8], p[40:48]
    return dict(
        conv_dw_b=conv_vec[0], conv_ln_g=conv_vec[1], conv_ln_b=conv_vec[2], conv_dw=p[8:8 + CONV_K],
        pool_b=pool_vec[0].reshape(4, GD), pool_scale=pool_vec[1], pool_w=p[48:176].reshape(4, GD, GD),
        pre_norm_g=p[176:192].reshape(8, D)[0], post_norm_g=p[192:208].reshape(8, D)[0],
        rel_bias=p[208:224].reshape(8, D)[:, :2 * MAX_REL + 1])


def _layer_fwd(x2, lw, l, BL, SEQ):
    T = BL * SEQ
    h, ht = _rms_pre(x2, lw["pre_g"][l])
    z = _mm("mm_in", h, lw["w_in"], "nn", T, NCOL, D, 1024, 1536, D, f32, layer=l)
    cv = _conv_fwd(z, lw["dw32"][l], lw["cvec"][l], BL, SEQ)
    at = _attn_fwd(z, lw["bm"][l], BL, SEQ)
    pv = _pool_fwd(z, lw["pw"][l], lw["pvec"][l], BL, SEQ)
    ys = [_mm("mm_branch_out", act, lw[w], "nn", T, D, WC, 1024, D, WC, f32, layer=l)
          for act, w in ((cv, "w_conv"), (at, "w_attn"), (pv, "w_pool"))]
    merged = _merge_fwd(z, ys)
    y = _mm("mm_out", merged, lw["w_out"], "nn", T, D, D, 1024, D, D, f32, layer=l)
    out = _post_fwd(y, x2, lw["post_g"][l])
    return out, dict(x=x2, ht=ht, z=z, acts=(cv, at, pv), ys=ys, merged=merged, y=y)


def _layer_bwd(dout, sv, lw, l, BL, SEQ):
    T = BL * SEQ
    dy, dpost = _post_bwd(dout, sv["y"], lw["post_g"][l])
    dmerged = _mm("mm_dmerged", dy, lw["w_out"], "nt", T, D, D, 1024, D, D, f32, layer=l)
    dw_out_t = _mm("mm_dw_out", dy, sv["merged"], "tn", D, D, T, D, D, 1024, bf16)
    dys_and_dz = _merge_bwd(sv["z"], dmerged, sv["ys"])
    dys, dz = dys_and_dz[:3], dys_and_dz[3]
    dacts = [_mm("mm_dact", dyb, lw[w], "nt", T, WC, D, 1024, WC, D, f32, layer=l)
             for dyb, w in zip(dys, ("w_conv", "w_attn", "w_pool"))]
    dws = [_mm("mm_dw_branch", act, dyb, "tn", WC, D, T, WC, D, 1024, bf16) for act, dyb in zip(sv["acts"], dys)]
    dz, ddw, dcvec = _conv_bwd(sv["z"], dacts[0], dz, lw["dw32"][l], lw["cvec"][l], BL, SEQ)
    dz, dbm = _attn_bwd(sv["z"], dacts[1], dz, lw["bm"][l], BL, SEQ)
    dz, dpw, dpvec = _pool_bwd(sv["z"], dacts[2], dz, lw["pw"][l], lw["pvec"][l], BL, SEQ)
    dh = _mm_dh(dz, lw["w_in_t"], l)
    dw_in = _mm_dw_in(sv["ht"], dz)
    dx, dpre = _pre_bwd(dh, sv["x"], lw["pre_g"][l], dout)
    drel = _bias_table_grad(_bias_colsum(_bias_fold(dbm)))
    small = _pack_small(dcvec, ddw, dpvec, dpw, dpre[0], dpost[0], drel)
    return dx, dict(w_in=dw_in, w_conv_out=dws[0], w_attn_out=dws[1], w_pool_out=dws[2], w_out=dw_out_t), small


def _local_step(x2, tgt2, lw, BL, SEQ):
    saved = []
    cur = x2
    for l in range(DEPTH):
        cur, sv = _layer_fwd(cur, lw, l, BL, SEQ)
        saved.append(sv)
    dout, sq = _loss_head(cur, tgt2)
    big, small = [None] * DEPTH, [None] * DEPTH
    for l in reversed(range(DEPTH)):
        dout, big[l], small[l] = _layer_bwd(dout, saved[l], lw, l, BL, SEQ)
    return sq, dout, big, small


def _full_weights(gathered, pre_norm_g, post_norm_g, conv_dw_full, conv_dw_b, conv_ln_g, conv_ln_b, rel_bias,
                  pool_w, pool_b, pool_scale):
    def cols(g):
        return jnp.transpose(g, (0, 2, 1, 3)).reshape(g.shape[0], g.shape[2], N_CHIPS * g.shape[3])

    g_out = gathered["w_out"]
    w_in = cols(gathered["w_in"])
    return dict(
        w_in=w_in, w_in_t=jnp.transpose(w_in, (0, 2, 1)),
        w_conv=cols(gathered["w_conv_out"]), w_attn=cols(gathered["w_attn_out"]),
        w_pool=cols(gathered["w_pool_out"]), w_out=g_out.reshape(DEPTH, D, D),
        pre_g=pre_norm_g[:, None, :], post_g=post_norm_g[:, None, :],
        dw32=jnp.pad(conv_dw_full, ((0, 0), (0, 32 - CONV_K), (0, 0))),
        cvec=jnp.stack([_vec_rows([conv_dw_b[l], conv_ln_g[l], conv_ln_b[l]]) for l in range(DEPTH)]),
        bm=jnp.stack([_bias_matrix(rel_bias[l]) for l in range(DEPTH)]),
        pw=pool_w.astype(bf16),
        pvec=jnp.stack([_vec_rows([pool_b[l].reshape(WC), pool_scale[l]]) for l in range(DEPTH)]))


BIG = ("w_in", "w_conv_out", "w_attn_out", "w_pool_out", "w_out")
SMALL = ("pre_norm_g", "post_norm_g", "conv_dw_b", "conv_ln_g", "conv_ln_b", "rel_bias", "pool_w", "pool_b", "pool_scale")
ORDER = ("pre_norm_g", "post_norm_g", "w_in", "conv_dw", "conv_dw_b", "conv_ln_g", "conv_ln_b", "w_conv_out",
         "rel_bias", "w_attn_out", "pool_w", "pool_b", "pool_scale", "w_pool_out", "w_out")


def _pack_small_params(p):
    return jnp.concatenate([
        _pack_small(_vec_rows([p["conv_dw_b"][l], p["conv_ln_g"][l], p["conv_ln_b"][l]]), jnp.zeros((32, WC), f32),
                    _vec_rows([p["pool_b"][l].reshape(WC), p["pool_scale"][l]]), p["pool_w"][l],
                    p["pre_norm_g"][l], p["post_norm_g"][l], p["rel_bias"][l])
        for l in range(DEPTH)], axis=0)


def _unpack_small_params(packed):
    layers = [_unpack_small(packed[l * SMALL_ROWS:(l + 1) * SMALL_ROWS]) for l in range(DEPTH)]
    return {k: jnp.stack([layers[l][k] for l in range(DEPTH)]) for k in layers[0]}


def kernel(x, pre_norm_g, post_norm_g, w_in, conv_dw, conv_dw_b, conv_ln_g, conv_ln_b, w_conv_out, rel_bias, w_attn_out, pool_w, pool_b, pool_scale, w_pool_out, w_out, loss_target, m_pre_norm_g, m_post_norm_g, m_w_in, m_conv_dw, m_conv_dw_b, m_conv_ln_g, m_conv_ln_b, m_w_conv_out, m_rel_bias, m_w_attn_out, m_pool_w, m_pool_b, m_pool_scale, m_w_pool_out, m_w_out, v_pre_norm_g, v_post_norm_g, v_w_in, v_conv_dw, v_conv_dw_b, v_conv_ln_g, v_conv_ln_b, v_w_conv_out, v_rel_bias, v_w_attn_out, v_pool_w, v_pool_b, v_pool_scale, v_w_pool_out, v_w_out):
    BL, SEQ, _ = x.shape
    T = BL * SEQ
    w = dict(pre_norm_g=pre_norm_g, post_norm_g=post_norm_g, w_in=w_in, conv_dw=conv_dw, conv_dw_b=conv_dw_b,
             conv_ln_g=conv_ln_g, conv_ln_b=conv_ln_b, w_conv_out=w_conv_out, rel_bias=rel_bias, w_attn_out=w_attn_out,
             pool_w=pool_w, pool_b=pool_b, pool_scale=pool_scale, w_pool_out=w_pool_out, w_out=w_out)
    m = dict(pre_norm_g=m_pre_norm_g, post_norm_g=m_post_norm_g, w_in=m_w_in, conv_dw=m_conv_dw, conv_dw_b=m_conv_dw_b,
             conv_ln_g=m_conv_ln_g, conv_ln_b=m_conv_ln_b, w_conv_out=m_w_conv_out, rel_bias=m_rel_bias,
             w_attn_out=m_w_attn_out, pool_w=m_pool_w, pool_b=m_pool_b, pool_scale=m_pool_scale,
             w_pool_out=m_w_pool_out, w_out=m_w_out)
    v = dict(pre_norm_g=v_pre_norm_g, post_norm_g=v_post_norm_g, w_in=v_w_in, conv_dw=v_conv_dw, conv_dw_b=v_conv_dw_b,
             conv_ln_g=v_conv_ln_g, conv_ln_b=v_conv_ln_b, w_conv_out=v_w_conv_out, rel_bias=v_rel_bias,
             w_attn_out=v_w_attn_out, pool_w=v_pool_w, pool_b=v_pool_b, pool_scale=v_pool_scale,
             w_pool_out=v_w_pool_out, w_out=v_w_out)

    names = BIG + ("conv_dw",)
    gathered = dict(zip(names, _gather_weights([w[k].astype(bf16) for k in BIG] + [conv_dw])))
    conv_dw_full = jnp.transpose(gathered["conv_dw"], (0, 2, 1, 3)).reshape(DEPTH, CONV_K, WC)
    lw = _full_weights(gathered, pre_norm_g, post_norm_g, conv_dw_full, conv_dw_b, conv_ln_g, conv_ln_b, rel_bias,
                       pool_w, pool_b, pool_scale)

    sq, grad_x, big, small = _local_step(x.reshape(T, D), loss_target.reshape(T, D), lw, BL, SEQ)
    loss = lax.psum(0.5 * jnp.sum(sq) / float(D), ("x", "y", "c"))

    flat = [big[l][k] for k in BIG for l in range(DEPTH)]
    slots = _scatter_grads(flat)
    halves = [_sum_slots("sum_grads", s) for s in slots]
    shared = _share_halves([halves[i * DEPTH:(i + 1) * DEPTH] for i in range(len(BIG))])
    grads, deltas, new_m, new_v = {}, {}, {}, {}
    for k, g4 in zip(BIG, shared):
        g = g4.reshape(DEPTH, g4.shape[2] * 2, g4.shape[3])
        if k == "w_out":
            g = jnp.transpose(g, (0, 2, 1))
        grads[k] = g
        shape = w[k].shape
        flat2 = lambda a: a.reshape(shape[0] * shape[1], shape[2])
        d_, nm_, nv_ = _adamw("adamw_big", flat2(g), flat2(w[k]), flat2(m[k]), flat2(v[k]))
        deltas[k], new_m[k], new_v[k] = d_.reshape(shape), nm_.reshape(shape), nv_.reshape(shape)

    gsmall = _sum_slots("sum_small", _gather_small(jnp.concatenate(small, axis=0)))
    d_, nm_, nv_ = _adamw("adamw_small", gsmall, _pack_small_params(w), _pack_small_params(m), _pack_small_params(v))
    gs, ds, ms, vs = (_unpack_small_params(a) for a in (gsmall, d_, nm_, nv_))
    for k in SMALL:
        grads[k], deltas[k], new_m[k], new_v[k] = gs[k], ds[k], ms[k], vs[k]
    chip = 2 * lax.axis_index("x") + lax.axis_index("y")
    g_dw = lax.dynamic_slice_in_dim(gs["conv_dw"], chip * GD, GD, axis=2)
    flat2 = lambda a: a.reshape(DEPTH * CONV_K, GD)
    d_, nm_, nv_ = _adamw("adamw_conv_dw", flat2(g_dw), flat2(conv_dw), flat2(m["conv_dw"]), flat2(v["conv_dw"]))
    grads["conv_dw"] = g_dw
    deltas["conv_dw"], new_m["conv_dw"], new_v["conv_dw"] = (a.reshape(conv_dw.shape) for a in (d_, nm_, nv_))

    return (loss, grad_x.reshape(x.shape), *[grads[k] for k in ORDER], *[deltas[k] for k in ORDER],
            *[new_m[k] for k in ORDER], *[new_v[k] for k in ORDER])
```

```python
import numpy as np
import jax
import jax.numpy as jnp
from jax import lax
from jax.experimental import pallas as pl
from jax.experimental.pallas import tpu as pltpu

f32 = jnp.float32
bf16 = jnp.bfloat16

D = 1024
DEPTH = 2
WC = 512
HEAD_DIM = 64
CHUNK = 64
LEFT_CHUNKS = 8
KEY_PAD = LEFT_CHUNKS * CHUNK
MAX_REL = 256
CONV_K = 31
POOL_WINDOWS = (2, 4, 8, 16)
GD = 128
NCOL = 7680
EPS = 1e-6
NEG_INF = -1e30
COL_A, COL_B, COL_CG, COL_Q, COL_K, COL_V, COL_AG, COL_PI, COL_PG, COL_GM = (
    0, 512, 1024, 1536, 2048, 2560, 3072, 3584, 4096, 4608)

ADAM_LR = 0.001
ADAM_B1 = 0.9
ADAM_B2 = 0.999
ADAM_EPS = 1e-08
ADAM_WD = 0.01
ADAM_STEP = 10

QG = 256
KW = KEY_PAD + QG
CT = 128
HALO = 32
PHALO = 16
N_CHIPS = 4
N_DEV = 8
VMEM_LIMIT = 56 * 1024 * 1024
MESH = pl.DeviceIdType.MESH
ANY = pl.BlockSpec(memory_space=pl.ANY)

DZ_BLOCKS = 18
DZ_CONV, DZ_ATTN, DZ_POOL, DZ_GM = 0, 4, 8, 12


def _dz_block(c):
    return c + (c >= 3).astype(jnp.int32) + 2 * (c >= 9).astype(jnp.int32)


def _params(sem=None):
    return pltpu.CompilerParams(dimension_semantics=sem, vmem_limit_bytes=VMEM_LIMIT)


def _sig(x):
    return 1.0 / (1.0 + jnp.exp(-x))


def _dsilu(x, s):
    return s * (1.0 + x * (1.0 - s))


def _colsum(x):
    return jnp.sum(x, axis=0, keepdims=True)


def _rms_pre(x2, g):
    T = x2.shape[0]
    tm = 512

    def body(x_ref, g_ref, h_ref, ht_ref):
        x = x_ref[...]
        r = lax.rsqrt(jnp.mean(x * x, axis=-1, keepdims=True) + EPS)
        h = (x * r) * g_ref[...]
        h_ref[...] = h.astype(bf16)
        ht_ref[...] = h.T.astype(bf16)

    row = pl.BlockSpec((tm, D), lambda i: (i, 0))
    vec = pl.BlockSpec((1, D), lambda i: (0, 0))
    return pl.pallas_call(
        body, grid=(T // tm,), in_specs=[row, vec], out_specs=[row, pl.BlockSpec((D, tm), lambda i: (0, i))],
        out_shape=[jax.ShapeDtypeStruct((T, D), bf16), jax.ShapeDtypeStruct((D, T), bf16)], name="rms_pre",
        compiler_params=_params(("parallel",)))(x2, g)


def _post_fwd(y, x2, g):
    T = x2.shape[0]
    tm = 512

    def body(y_ref, x_ref, g_ref, o_ref):
        y = y_ref[...]
        r = lax.rsqrt(jnp.mean(y * y, axis=-1, keepdims=True) + EPS)
        o_ref[...] = x_ref[...] + (y * r) * g_ref[...]

    row = pl.BlockSpec((tm, D), lambda i: (i, 0))
    vec = pl.BlockSpec((1, D), lambda i: (0, 0))
    return pl.pallas_call(
        body, grid=(T // tm,), in_specs=[row, row, vec], out_specs=row,
        out_shape=jax.ShapeDtypeStruct((T, D), f32), name="post_fwd",
        compiler_params=_params(("parallel",)))(y, x2, g)


def _loss_head(out, tgt):
    T = out.shape[0]
    tm = 512

    def body(o_ref, t_ref, d_ref, l_ref):
        e = o_ref[...] - t_ref[...]
        d_ref[...] = e / float(D)

        @pl.when(pl.program_id(0) == 0)
        def _():
            l_ref[...] = jnp.zeros_like(l_ref)

        l_ref[...] += _colsum(e * e)

    row = pl.BlockSpec((tm, D), lambda i: (i, 0))
    vec = pl.BlockSpec((1, D), lambda i: (0, 0))
    return pl.pallas_call(
        body, grid=(T // tm,), in_specs=[row, row], out_specs=[row, vec],
        out_shape=[jax.ShapeDtypeStruct((T, D), f32), jax.ShapeDtypeStruct((1, D), f32)],
        name="loss_head", compiler_params=_params(("arbitrary",)))(out, tgt)


def _post_bwd(dout, y, g):
    T = y.shape[0]
    tm = 512

    def body(d_ref, y_ref, g_ref, dy_ref, dg_ref):
        y = y_ref[...]
        d = d_ref[...]
        r = lax.rsqrt(jnp.mean(y * y, axis=-1, keepdims=True) + EPS)
        yn = y * r
        dyn = d * g_ref[...]
        dy = r * (dyn - yn * jnp.mean(dyn * yn, axis=-1, keepdims=True))
        dy_ref[...] = dy.astype(bf16)

        @pl.when(pl.program_id(0) == 0)
        def _():
            dg_ref[...] = jnp.zeros_like(dg_ref)

        dg_ref[...] += _colsum(d * yn)

    row = pl.BlockSpec((tm, D), lambda i: (i, 0))
    vec = pl.BlockSpec((1, D), lambda i: (0, 0))
    return pl.pallas_call(
        body, grid=(T // tm,), in_specs=[row, row, vec], out_specs=[row, vec],
        out_shape=[jax.ShapeDtypeStruct((T, D), bf16), jax.ShapeDtypeStruct((1, D), f32)],
        name="post_bwd", compiler_params=_params(("arbitrary",)))(dout, y, g)


def _pre_bwd(dh, x2, g, dout):
    T = x2.shape[0]
    tm = 512

    def body(dh_ref, x_ref, g_ref, d_ref, dx_ref, dg_ref):
        x = x_ref[...]
        dh_ = dh_ref[...]
        r = lax.rsqrt(jnp.mean(x * x, axis=-1, keepdims=True) + EPS)
        xn = x * r
        dxn = dh_ * g_ref[...]
        dx_ref[...] = r * (dxn - xn * jnp.mean(dxn * xn, axis=-1, keepdims=True)) + d_ref[...]

        @pl.when(pl.program_id(0) == 0)
        def _():
            dg_ref[...] = jnp.zeros_like(dg_ref)

        dg_ref[...] += _colsum(dh_ * xn)

    row = pl.BlockSpec((tm, D), lambda i: (i, 0))
    vec = pl.BlockSpec((1, D), lambda i: (0, 0))
    return pl.pallas_call(
        body, grid=(T // tm,), in_specs=[row, row, vec, row], out_specs=[row, vec],
        out_shape=[jax.ShapeDtypeStruct((T, D), f32), jax.ShapeDtypeStruct((1, D), f32)],
        name="pre_bwd", compiler_params=_params(("arbitrary",)))(dh, x2, g, dout)


def _mat_spec(arr, tr, tc, rc, layer):
    if arr.ndim == 2:
        return pl.BlockSpec((tr, tc), rc)

    def index(i, j, k):
        r, c = rc(i, j, k)
        return (layer, r, c)

    return pl.BlockSpec((None, tr, tc), index)


def _mm(name, a, b, mode, m, n, k, tm, tn, tk, out_dtype, layer=None):
    nk = k // tk
    assert m % tm == 0 and n % tn == 0 and k % tk == 0
    if mode == "nn":
        a_spec = _mat_spec(a, tm, tk, lambda i, j, kk: (i, kk), layer)
        b_spec = _mat_spec(b, tk, tn, lambda i, j, kk: (kk, j), layer)
        dn = (((1,), (0,)), ((), ()))
    elif mode == "nt":
        a_spec = _mat_spec(a, tm, tk, lambda i, j, kk: (i, kk), layer)
        b_spec = _mat_spec(b, tn, tk, lambda i, j, kk: (j, kk), layer)
        dn = (((1,), (1,)), ((), ()))
    else:
        a_spec = _mat_spec(a, tk, tm, lambda i, j, kk: (kk, i), layer)
        b_spec = _mat_spec(b, tk, tn, lambda i, j, kk: (kk, j), layer)
        dn = (((0,), (0,)), ((), ()))

    def body(a_ref, b_ref, o_ref, acc_ref):
        p = lax.dot_general(a_ref[...].astype(bf16), b_ref[...].astype(bf16), dn, preferred_element_type=f32)
        if nk == 1:
            o_ref[...] = p.astype(o_ref.dtype)
        else:
            kk = pl.program_id(2)

            @pl.when(kk == 0)
            def _():
                acc_ref[...] = p

            @pl.when(kk > 0)
            def _():
                acc_ref[...] += p

            @pl.when(kk == nk - 1)
            def _():
                o_ref[...] = acc_ref[...].astype(o_ref.dtype)

    acc_shape = (tm, tn) if nk > 1 else (8, 128)
    return pl.pallas_call(
        body, grid=(m // tm, n // tn, nk), in_specs=[a_spec, b_spec],
        out_specs=pl.BlockSpec((tm, tn), lambda i, j, kk: (i, j)),
        out_shape=jax.ShapeDtypeStruct((m, n), out_dtype),
        scratch_shapes=[pltpu.VMEM(acc_shape, f32)], name=name,
        compiler_params=_params(("parallel", "parallel", "arbitrary")))(a, b)


DZ_SPANS = ((DZ_CONV, 3), (DZ_ATTN, 4), (DZ_POOL, 2), (DZ_GM, 6))


def _mm_dh(dz, w_in_t, layer):
    T = dz.shape[1]
    tm = 512

    def body(conv_ref, attn_ref, pool_ref, gm_ref, w_ref, o_ref):
        acc = None
        col = 0
        for ref, (_, blocks) in zip((conv_ref, attn_ref, pool_ref, gm_ref), DZ_SPANS):
            for b in range(blocks):
                p = jnp.dot(ref[b], w_ref[col * WC:(col + 1) * WC, :], preferred_element_type=f32)
                acc = p if acc is None else acc + p
                col += 1
        o_ref[...] = acc

    spans = [pl.BlockSpec((blocks, tm, WC), lambda i, first=first, blocks=blocks: (first // blocks, i, 0))
             for first, blocks in DZ_SPANS]
    return pl.pallas_call(
        body, grid=(T // tm,),
        in_specs=spans + [pl.BlockSpec((None, NCOL, D), lambda i: (layer, 0, 0), pipeline_mode=pl.Buffered(1))],
        out_specs=pl.BlockSpec((tm, D), lambda i: (i, 0)), out_shape=jax.ShapeDtypeStruct((T, D), f32),
        name="mm_dh", compiler_params=_params(("parallel",)))(dz, dz, dz, dz, w_in_t)


def _mm_dw_in(ht, dz):
    T = dz.shape[1]

    def body(ht_ref, dz_ref, o_ref):
        o_ref[...] = jnp.dot(ht_ref[...], dz_ref[...], preferred_element_type=f32).astype(bf16)

    return pl.pallas_call(
        body, grid=(NCOL // WC,),
        in_specs=[pl.BlockSpec((D, T), lambda j: (0, 0), pipeline_mode=pl.Buffered(1)),
                  pl.BlockSpec((None, T, WC), lambda j: (_dz_block(j), 0, 0))],
        out_specs=pl.BlockSpec((D, WC), lambda j: (0, j)), out_shape=jax.ShapeDtypeStruct((D, NCOL), bf16),
        name="mm_dw_in", compiler_params=_params(("parallel",)))(ht, dz)


def _conv_delays():
    return [(8 * a + b, a, b) for b in range(8) for a in range(4) if 8 * a + b < CONV_K]


def _conv_taps(win, dw_ref):
    rolled = {}
    acc = None
    for d, a, b in _conv_delays():
        if b not in rolled:
            rolled[b] = win if b == 0 else pltpu.roll(win, b, axis=0)
        term = rolled[b][HALO - 8 * a:HALO - 8 * a + CT, :] * dw_ref[pl.ds(CONV_K - 1 - d, 1), :]
        acc = term if acc is None else acc + term
    return acc, rolled


def _conv_fwd(z, dw32, cvec, BL, SEQ):
    T = BL * SEQ
    nct = SEQ // CT

    def body(a_ref, b_ref, cg_ref, dw_ref, vec_ref, o_ref, p_ref):
        p_ref[pl.ds(0, HALO), :] = jnp.zeros((HALO, WC), f32)

        def glu(c, carry):
            r0 = pl.multiple_of(c * CT, CT)
            p_ref[pl.ds(r0 + HALO, CT), :] = a_ref[pl.ds(r0, CT), :] * _sig(b_ref[pl.ds(r0, CT), :])
            return carry

        lax.fori_loop(0, nct, glu, 0)

        def step(c, carry):
            r0 = pl.multiple_of(c * CT, CT)
            u1, _ = _conv_taps(p_ref[pl.ds(r0, CT + HALO), :], dw_ref)
            u1 = u1 + vec_ref[0:1, :]
            xc = u1 - jnp.mean(u1, axis=-1, keepdims=True)
            rs = lax.rsqrt(jnp.mean(xc * xc, axis=-1, keepdims=True) + EPS)
            u2 = (xc * rs) * vec_ref[1:2, :] + vec_ref[2:3, :]
            cg = cg_ref[pl.ds(r0, CT), :]
            o_ref[pl.ds(r0, CT), :] = ((u2 * _sig(u2)) * (cg * _sig(cg))).astype(bf16)
            return carry

        lax.fori_loop(0, nct, step, 0)

    def zs(col):
        return pl.BlockSpec((SEQ, WC), lambda b: (b, col // WC))

    return pl.pallas_call(
        body, grid=(BL,),
        in_specs=[zs(COL_A), zs(COL_B), zs(COL_CG), pl.BlockSpec((32, WC), lambda b: (0, 0)),
                  pl.BlockSpec((8, WC), lambda b: (0, 0))],
        out_specs=pl.BlockSpec((SEQ, WC), lambda b: (b, 0)),
        out_shape=jax.ShapeDtypeStruct((T, WC), bf16),
        scratch_shapes=[pltpu.VMEM((SEQ + HALO, WC), f32)], name="conv_fwd",
        compiler_params=_params(("parallel",)))(z, z, z, dw32, cvec)


def _conv_bwd(z, dcv, dz, dw32, cvec, BL, SEQ):
    nct = SEQ // CT

    def body(a_ref, b_ref, cg_ref, dcv_ref, dzin_ref, dw_ref, vec_ref, dz_ref, ddw_ref, dvec_ref, p_ref, q_ref):
        @pl.when(pl.program_id(0) == 0)
        def _():
            ddw_ref[...] = jnp.zeros_like(ddw_ref)
            dvec_ref[...] = jnp.zeros_like(dvec_ref)

        p_ref[pl.ds(0, HALO), :] = jnp.zeros((HALO, WC), f32)
        q_ref[pl.ds(SEQ, HALO), :] = jnp.zeros((HALO, WC), f32)

        def glu(c, carry):
            r0 = pl.multiple_of(c * CT, CT)
            p_ref[pl.ds(r0 + HALO, CT), :] = a_ref[pl.ds(r0, CT), :] * _sig(b_ref[pl.ds(r0, CT), :])
            return carry

        lax.fori_loop(0, nct, glu, 0)

        def step(c, carry):
            r0 = pl.multiple_of(c * CT, CT)
            u1, rolled = _conv_taps(p_ref[pl.ds(r0, CT + HALO), :], dw_ref)
            u1 = u1 + vec_ref[0:1, :]
            xc = u1 - jnp.mean(u1, axis=-1, keepdims=True)
            rs = lax.rsqrt(jnp.mean(xc * xc, axis=-1, keepdims=True) + EPS)
            nrm = xc * rs
            u2 = nrm * vec_ref[1:2, :] + vec_ref[2:3, :]
            s2 = _sig(u2)
            u3 = u2 * s2
            cg = cg_ref[pl.ds(r0, CT), :]
            scg = _sig(cg)
            dcv_ = dcv_ref[pl.ds(r0, CT), :]
            dz_ref[2, pl.ds(r0, CT), :] = (dcv_ * u3 * _dsilu(cg, scg)).astype(bf16)
            du2 = dcv_ * (cg * scg) * _dsilu(u2, s2)
            dvec_ref[1:2, :] += _colsum(du2 * nrm)
            dvec_ref[2:3, :] += _colsum(du2)
            dn = du2 * vec_ref[1:2, :]
            du1 = rs * (dn - jnp.mean(dn, axis=-1, keepdims=True)
                        - nrm * jnp.mean(dn * nrm, axis=-1, keepdims=True))
            dvec_ref[0:1, :] += _colsum(du1)
            q_ref[pl.ds(r0, CT), :] = du1
            for d, a, b in _conv_delays():
                row = CONV_K - 1 - d
                ddw_ref[pl.ds(row, 1), :] += _colsum(du1 * rolled[b][HALO - 8 * a:HALO - 8 * a + CT, :])
            return carry

        lax.fori_loop(0, nct, step, 0)

        def back(c, carry):
            r0 = pl.multiple_of(c * CT, CT)
            wq = q_ref[pl.ds(r0, CT + HALO), :]
            up = {}
            acc = None
            for d, a, b in _conv_delays():
                if b not in up:
                    up[b] = wq if b == 0 else pltpu.roll(wq, CT + HALO - b, axis=0)
                term = up[b][8 * a:8 * a + CT, :] * dw_ref[pl.ds(CONV_K - 1 - d, 1), :]
                acc = term if acc is None else acc + term
            a_ = a_ref[pl.ds(r0, CT), :]
            sb = _sig(b_ref[pl.ds(r0, CT), :])
            dz_ref[0, pl.ds(r0, CT), :] = (acc * sb).astype(bf16)
            dz_ref[1, pl.ds(r0, CT), :] = (acc * a_ * sb * (1.0 - sb)).astype(bf16)
            return carry

        lax.fori_loop(0, nct, back, 0)

    def zs(col):
        return pl.BlockSpec((SEQ, WC), lambda b: (b, col // WC), pipeline_mode=pl.Buffered(1))

    def const(r):
        return pl.BlockSpec((r, WC), lambda b: (0, 0))

    return pl.pallas_call(
        body, grid=(BL,),
        in_specs=[zs(COL_A), zs(COL_B), zs(COL_CG),
                  pl.BlockSpec((SEQ, WC), lambda b: (b, 0), pipeline_mode=pl.Buffered(1)), ANY, const(32), const(8)],
        out_specs=[pl.BlockSpec((3, SEQ, WC), lambda b: (DZ_CONV // 3, b, 0)), const(32), const(8)],
        out_shape=[jax.ShapeDtypeStruct(dz.shape, bf16), jax.ShapeDtypeStruct((32, WC), f32),
                   jax.ShapeDtypeStruct((8, WC), f32)],
        scratch_shapes=[pltpu.VMEM((SEQ + HALO, WC), f32), pltpu.VMEM((SEQ + HALO, WC), f32)],
        input_output_aliases={4: 0}, name="conv_bwd",
        compiler_params=_params(("arbitrary",)))(z, z, z, dcv, dz, dw32, cvec)


def _pool_counts(r0):
    t1 = r0 + 1 + lax.broadcasted_iota(jnp.int32, (CT, 1), 0)
    return [jnp.minimum(t1, w).astype(f32) for w in POOL_WINDOWS]


def _pool_sums(win, forward):
    n = CT + PHALO

    def sh(x, s):
        return pltpu.roll(x, (n - s) if forward else s, axis=0)

    s2 = win + sh(win, 1)
    s4 = s2[:, GD:] + sh(s2[:, GD:], 2)
    s8 = s4[:, GD:] + sh(s4[:, GD:], 4)
    s16 = s8[:, GD:] + sh(s8[:, GD:], 8)
    lo = 0 if forward else PHALO
    return [s[lo:lo + CT, :GD] for s in (s2, s4, s8, s16)]


def _pool_fwd(z, pw, pvec, BL, SEQ):
    T = BL * SEQ
    nct = SEQ // CT

    def body(pi_ref, pg_ref, pw_ref, vec_ref, o_ref, p_ref):
        p_ref[pl.ds(0, PHALO), :] = jnp.zeros((PHALO, WC), f32)

        def fill(c, carry):
            r0 = pl.multiple_of(c * CT, CT)
            p_ref[pl.ds(r0 + PHALO, CT), :] = pi_ref[pl.ds(r0, CT), :]
            return carry

        lax.fori_loop(0, nct, fill, 0)

        def step(c, carry):
            r0 = pl.multiple_of(c * CT, CT)
            sums = _pool_sums(p_ref[pl.ds(r0, CT + PHALO), :], False)
            cnt = _pool_counts(r0)
            pin = pi_ref[pl.ds(r0, CT), :]
            mixed = []
            for g in range(4):
                pooled = sums[g] / cnt[g] - pin[:, g * GD:(g + 1) * GD]
                mixed.append(jnp.dot(pooled.astype(bf16), pw_ref[g], preferred_element_type=f32))
            m0 = jnp.concatenate(mixed, axis=1) + vec_ref[0:1, :]
            pg = pg_ref[pl.ds(r0, CT), :]
            o_ref[pl.ds(r0, CT), :] = ((m0 * vec_ref[1:2, :]) * (pg * _sig(pg))).astype(bf16)
            return carry

        lax.fori_loop(0, nct, step, 0)

    def zs(col):
        return pl.BlockSpec((SEQ, WC), lambda b: (b, col // WC))

    return pl.pallas_call(
        body, grid=(BL,),
        in_specs=[zs(COL_PI), zs(COL_PG), pl.BlockSpec((4, GD, GD), lambda b: (0, 0, 0)),
                  pl.BlockSpec((8, WC), lambda b: (0, 0))],
        out_specs=pl.BlockSpec((SEQ, WC), lambda b: (b, 0)),
        out_shape=jax.ShapeDtypeStruct((T, WC), bf16),
        scratch_shapes=[pltpu.VMEM((SEQ + PHALO, WC), f32)], name="pool_fwd",
        compiler_params=_params(("parallel",)))(z, z, pw, pvec)


def _pool_bwd(z, dpl, dz, pw, pvec, BL, SEQ):
    nct = SEQ // CT

    def body(pi_ref, pg_ref, dpl_ref, dzin_ref, pw_ref, vec_ref, dz_ref, dpw_ref, dvec_ref, p_ref, e_ref, dp_ref):
        @pl.when(pl.program_id(0) == 0)
        def _():
            dpw_ref[...] = jnp.zeros_like(dpw_ref)
            dvec_ref[...] = jnp.zeros_like(dvec_ref)

        p_ref[pl.ds(0, PHALO), :] = jnp.zeros((PHALO, WC), f32)
        e_ref[pl.ds(SEQ, PHALO), :] = jnp.zeros((PHALO, WC), f32)

        def fill(c, carry):
            r0 = pl.multiple_of(c * CT, CT)
            p_ref[pl.ds(r0 + PHALO, CT), :] = pi_ref[pl.ds(r0, CT), :]
            return carry

        lax.fori_loop(0, nct, fill, 0)

        def step(c, carry):
            r0 = pl.multiple_of(c * CT, CT)
            sums = _pool_sums(p_ref[pl.ds(r0, CT + PHALO), :], False)
            cnt = _pool_counts(r0)
            pin = pi_ref[pl.ds(r0, CT), :]
            pooled = [(sums[g] / cnt[g] - pin[:, g * GD:(g + 1) * GD]).astype(bf16) for g in range(4)]
            m0 = jnp.concatenate(
                [jnp.dot(pooled[g], pw_ref[g], preferred_element_type=f32) for g in range(4)], axis=1) + vec_ref[0:1, :]
            scale = vec_ref[1:2, :]
            pg = pg_ref[pl.ds(r0, CT), :]
            spg = _sig(pg)
            dpl_ = dpl_ref[pl.ds(r0, CT), :]
            dmixed = dpl_ * (pg * spg)
            dz_ref[1, pl.ds(r0, CT), :] = (dpl_ * (m0 * scale) * _dsilu(pg, spg)).astype(bf16)
            dvec_ref[1:2, :] += _colsum(dmixed * m0)
            dm0 = dmixed * scale
            dvec_ref[0:1, :] += _colsum(dm0)
            dps, es = [], []
            for g in range(4):
                dm0g = dm0[:, g * GD:(g + 1) * GD].astype(bf16)
                dpw_ref[g] += lax.dot_general(pooled[g], dm0g, (((0,), (0,)), ((), ())), preferred_element_type=f32)
                dpg = lax.dot_general(dm0g, pw_ref[g], (((1,), (1,)), ((), ())), preferred_element_type=f32)
                dps.append(dpg)
                es.append(dpg / cnt[g])
            dp_ref[pl.ds(r0, CT), :] = jnp.concatenate(dps, axis=1)
            e_ref[pl.ds(r0, CT), :] = jnp.concatenate(es, axis=1)
            return carry

        lax.fori_loop(0, nct, step, 0)

        def back(c, carry):
            r0 = pl.multiple_of(c * CT, CT)
            fs = _pool_sums(e_ref[pl.ds(r0, CT + PHALO), :], True)
            dz_ref[0, pl.ds(r0, CT), :] = (jnp.concatenate(fs, axis=1) - dp_ref[pl.ds(r0, CT), :]).astype(bf16)
            return carry

        lax.fori_loop(0, nct, back, 0)

    def zs(col):
        return pl.BlockSpec((SEQ, WC), lambda b: (b, col // WC), pipeline_mode=pl.Buffered(1))

    return pl.pallas_call(
        body, grid=(BL,),
        in_specs=[zs(COL_PI), zs(COL_PG),
                  pl.BlockSpec((SEQ, WC), lambda b: (b, 0), pipeline_mode=pl.Buffered(1)), ANY,
                  pl.BlockSpec((4, GD, GD), lambda b: (0, 0, 0)), pl.BlockSpec((8, WC), lambda b: (0, 0))],
        out_specs=[pl.BlockSpec((2, SEQ, WC), lambda b: (DZ_POOL // 2, b, 0)),
                   pl.BlockSpec((4, GD, GD), lambda b: (0, 0, 0)), pl.BlockSpec((8, WC), lambda b: (0, 0))],
        out_shape=[jax.ShapeDtypeStruct(dz.shape, bf16), jax.ShapeDtypeStruct((4, GD, GD), f32),
                   jax.ShapeDtypeStruct((8, WC), f32)],
        scratch_shapes=[pltpu.VMEM((SEQ + PHALO, WC), f32), pltpu.VMEM((SEQ + PHALO, WC), f32),
                        pltpu.VMEM((SEQ, WC), f32)],
        input_output_aliases={3: 0}, name="pool_bwd",
        compiler_params=_params(("arbitrary",)))(z, z, dpl, dz, pw, pvec)


def _attn_prologue(q_ref, k_ref, v_ref, qs0, qs1, kp, vp, SEQ):
    head0 = lax.broadcasted_iota(jnp.int32, (1, 2 * HEAD_DIM), 1) < HEAD_DIM
    kp[pl.ds(0, KEY_PAD), :] = jnp.zeros((KEY_PAD, 2 * HEAD_DIM), bf16)
    vp[pl.ds(0, KEY_PAD), :] = jnp.zeros((KEY_PAD, 2 * HEAD_DIM), bf16)

    def fill(g, carry):
        r0 = pl.multiple_of(g * QG, QG)
        q = q_ref[pl.ds(r0, QG), :] * (HEAD_DIM ** -0.5)
        qs0[pl.ds(r0, QG), :] = jnp.where(head0, q, 0.0).astype(bf16)
        qs1[pl.ds(r0, QG), :] = jnp.where(head0, 0.0, q).astype(bf16)
        kp[pl.ds(r0 + KEY_PAD, QG), :] = k_ref[pl.ds(r0, QG), :].astype(bf16)
        vp[pl.ds(r0 + KEY_PAD, QG), :] = v_ref[pl.ds(r0, QG), :].astype(bf16)
        return carry

    lax.fori_loop(0, SEQ // QG, fill, 0)
    return head0


def _attn_probs(qh, kw, bias, r0):
    s = lax.dot_general(qh, kw, (((1,), (1,)), ((), ())), preferred_element_type=f32) + bias
    key_pos = r0 - KEY_PAD + lax.broadcasted_iota(jnp.int32, (1, KW), 1)
    s = jnp.where(key_pos >= 0, s, NEG_INF)
    e = jnp.exp(s - jnp.max(s, axis=-1, keepdims=True))
    return e * (1.0 / jnp.sum(e, axis=-1, keepdims=True))


def _attn_fwd(z, bm, BL, SEQ):
    T = BL * SEQ
    W2 = 2 * HEAD_DIM

    def body(q_ref, k_ref, v_ref, ag_ref, bm_ref, o_ref, qs0, qs1, kp, vp):
        head0 = _attn_prologue(q_ref, k_ref, v_ref, qs0, qs1, kp, vp, SEQ)

        def group(g, carry):
            r0 = pl.multiple_of(g * QG, QG)
            kw = kp[pl.ds(r0, KW), :]
            vw = vp[pl.ds(r0, KW), :]
            outs = []
            for hh, qs in enumerate((qs0, qs1)):
                p = _attn_probs(qs[pl.ds(r0, QG), :], kw, bm_ref[hh], r0)
                outs.append(jnp.dot(p.astype(bf16), vw, preferred_element_type=f32))
            o = jnp.where(head0, outs[0], outs[1])
            ag = ag_ref[pl.ds(r0, QG), :]
            o_ref[pl.ds(r0, QG), :] = (o * (ag * _sig(ag))).astype(bf16)
            return carry

        lax.fori_loop(0, SEQ // QG, group, 0)

    def zs(col):
        return pl.BlockSpec((SEQ, W2), lambda b, hp: (b, col // W2 + hp))

    return pl.pallas_call(
        body, grid=(BL, WC // W2),
        in_specs=[zs(COL_Q), zs(COL_K), zs(COL_V), zs(COL_AG), pl.BlockSpec((2, QG, KW), lambda b, hp: (hp, 0, 0))],
        out_specs=pl.BlockSpec((SEQ, W2), lambda b, hp: (b, hp)),
        out_shape=jax.ShapeDtypeStruct((T, WC), bf16),
        scratch_shapes=[pltpu.VMEM((SEQ, W2), bf16), pltpu.VMEM((SEQ, W2), bf16),
                        pltpu.VMEM((SEQ + KEY_PAD, W2), bf16), pltpu.VMEM((SEQ + KEY_PAD, W2), bf16)],
        name="attn_fwd", compiler_params=_params(("parallel", "parallel")))(z, z, z, z, bm)


def _attn_bwd(z, dat, dz, bm, BL, SEQ):
    W2 = 2 * HEAD_DIM

    def body(q_ref, k_ref, v_ref, ag_ref, dat_ref, dzin_ref, bm_ref, dz_ref, dbm_ref,
             qs0, qs1, kp, vp, do0, do1, dka, dva):
        @pl.when(pl.program_id(1) == 0)
        def _():
            dbm_ref[...] = jnp.zeros_like(dbm_ref)

        head0 = _attn_prologue(q_ref, k_ref, v_ref, qs0, qs1, kp, vp, SEQ)
        dka[...] = jnp.zeros_like(dka)
        dva[...] = jnp.zeros_like(dva)

        def fill(g, carry):
            r0 = pl.multiple_of(g * QG, QG)
            ag = ag_ref[pl.ds(r0, QG), :]
            do = dat_ref[pl.ds(r0, QG), :] * (ag * _sig(ag))
            do0[pl.ds(r0, QG), :] = jnp.where(head0, do, 0.0).astype(bf16)
            do1[pl.ds(r0, QG), :] = jnp.where(head0, 0.0, do).astype(bf16)
            return carry

        lax.fori_loop(0, SEQ // QG, fill, 0)

        def group(g, carry):
            r0 = pl.multiple_of(g * QG, QG)
            kw = kp[pl.ds(r0, KW), :]
            vw = vp[pl.ds(r0, KW), :]
            outs, dqs = [], []
            for hh, (qs, dos) in enumerate(((qs0, do0), (qs1, do1))):
                qh = qs[pl.ds(r0, QG), :]
                doh = dos[pl.ds(r0, QG), :]
                p = _attn_probs(qh, kw, bm_ref[hh], r0)
                pb = p.astype(bf16)
                outs.append(jnp.dot(pb, vw, preferred_element_type=f32))
                dp = lax.dot_general(doh, vw, (((1,), (1,)), ((), ())), preferred_element_type=f32)
                ds_ = p * (dp - jnp.sum(p * dp, axis=-1, keepdims=True))
                dbm_ref[hh] += ds_
                dsb = ds_.astype(bf16)
                dqs.append(jnp.dot(dsb, kw, preferred_element_type=f32))
                dka[pl.ds(r0, KW), :] += lax.dot_general(dsb, qh, (((0,), (0,)), ((), ())), preferred_element_type=f32)
                dva[pl.ds(r0, KW), :] += lax.dot_general(pb, doh, (((0,), (0,)), ((), ())), preferred_element_type=f32)
            o = jnp.where(head0, outs[0], outs[1])
            dq = jnp.where(head0, dqs[0], dqs[1]) * (HEAD_DIM ** -0.5)
            ag = ag_ref[pl.ds(r0, QG), :]
            dz_ref[0, pl.ds(r0, QG), :] = dq.astype(bf16)
            dz_ref[3, pl.ds(r0, QG), :] = (dat_ref[pl.ds(r0, QG), :] * o * _dsilu(ag, _sig(ag))).astype(bf16)
            return carry

        lax.fori_loop(0, SEQ // QG, group, 0)

        def flush(g, carry):
            r0 = pl.multiple_of(g * QG, QG)
            dz_ref[1, pl.ds(r0, QG), :] = dka[pl.ds(r0 + KEY_PAD, QG), :].astype(bf16)
            dz_ref[2, pl.ds(r0, QG), :] = dva[pl.ds(r0 + KEY_PAD, QG), :].astype(bf16)
            return carry

        lax.fori_loop(0, SEQ // QG, flush, 0)

    def zs(col):
        return pl.BlockSpec((SEQ, W2), lambda hp, b: (b, col // W2 + hp))

    return pl.pallas_call(
        body, grid=(WC // W2, BL),
        in_specs=[zs(COL_Q), zs(COL_K), zs(COL_V), zs(COL_AG), pl.BlockSpec((SEQ, W2), lambda hp, b: (b, hp)), ANY,
                  pl.BlockSpec((2, QG, KW), lambda hp, b: (hp, 0, 0))],
        out_specs=[pl.BlockSpec((4, SEQ, W2), lambda hp, b: (DZ_ATTN // 4, b, hp)),
                   pl.BlockSpec((2, QG, KW), lambda hp, b: (hp, 0, 0))],
        out_shape=[jax.ShapeDtypeStruct(dz.shape, bf16), jax.ShapeDtypeStruct((8, QG, KW), f32)],
        scratch_shapes=[pltpu.VMEM((SEQ, W2), bf16), pltpu.VMEM((SEQ, W2), bf16),
                        pltpu.VMEM((SEQ + KEY_PAD, W2), bf16), pltpu.VMEM((SEQ + KEY_PAD, W2), bf16),
                        pltpu.VMEM((SEQ, W2), bf16), pltpu.VMEM((SEQ, W2), bf16),
                        pltpu.VMEM((SEQ + KEY_PAD, W2), f32), pltpu.VMEM((SEQ + KEY_PAD, W2), f32)],
        input_output_aliases={5: 0}, name="attn_bwd",
        compiler_params=_params(("parallel", "arbitrary")))(z, z, z, z, dat, dz, bm)


def _bias_matrix(table):
    n = 2 * MAX_REL
    wd = QG + KW
    e = jnp.concatenate([jnp.broadcast_to(table[:, n:], (8, wd - n)), table[:, n - 1:0:-1], jnp.zeros((8, 1), f32)], axis=1)
    flat = jnp.broadcast_to(e[:, None, :], (8, QG, wd)).reshape(8, QG * wd)
    skew = flat[:, :QG * (wd - 1)].reshape(8, QG, wd - 1)
    vals = skew[:, :, QG - 1:QG - 1 + KW]
    r = np.arange(QG)[:, None] // CHUNK
    j = np.arange(KW)[None, :] // CHUNK
    band = (j >= r) & (j <= r + LEFT_CHUNKS)
    return jnp.where(jnp.asarray(band)[None], vals, NEG_INF)


def _bias_fold(dbm):
    wd = QG + KW
    placed = jnp.pad(dbm, ((0, 0), (0, 0), (QG - 1, 0))).reshape(8, QG * (wd - 1))
    return jnp.pad(placed, ((0, 0), (0, QG))).reshape(8, QG, wd)


def _bias_colsum(folded):
    width = folded.shape[2]

    def body(x_ref, o_ref):
        for h in range(8):
            o_ref[pl.ds(h, 1), :] = _colsum(x_ref[h])

    return pl.pallas_call(body, out_shape=jax.ShapeDtypeStruct((8, width), f32), name="bias_colsum",
                          compiler_params=_params())(folded)


def _bias_table_grad(colsum):
    n = 2 * MAX_REL
    wd = QG + KW
    clipped = jnp.sum(colsum[:, :wd - n], axis=1, keepdims=True)
    return jnp.concatenate([jnp.zeros((8, 1), f32), colsum[:, wd - 2:wd - n - 1:-1], clipped], axis=1)


def _merge_fwd(z, ys):
    T = z.shape[0]
    tm = 512

    def body(g0, g1, g2, y0, y1, y2, o_ref):
        acc = _sig(g0[...]) * y0[...] + _sig(g1[...]) * y1[...] + _sig(g2[...]) * y2[...]
        o_ref[...] = acc.astype(bf16)

    def gs(br):
        return pl.BlockSpec((tm, WC), lambda i, j: (i, (COL_GM + br * D) // WC + j))

    ysp = pl.BlockSpec((tm, WC), lambda i, j: (i, j))
    return pl.pallas_call(
        body, grid=(T // tm, D // WC), in_specs=[gs(0), gs(1), gs(2), ysp, ysp, ysp], out_specs=ysp,
        out_shape=jax.ShapeDtypeStruct((T, D), bf16), name="merge_fwd",
        compiler_params=_params(("parallel", "parallel")))(z, z, z, *ys)


def _merge_bwd(z, dmerged, ys):
    T = z.shape[0]
    tm = 256

    def body(*refs):
        g = refs[0:6]
        dm_ref = refs[6]
        y = refs[7:10]
        dy = refs[10:13]
        dz_ref = refs[13]
        for br in range(3):
            for jh in range(2):
                cols = slice(jh * WC, (jh + 1) * WC)
                s = _sig(g[2 * br + jh][...])
                dm = dm_ref[:, cols]
                dy[br][:, cols] = (dm * s).astype(bf16)
                dz_ref[2 * br + jh] = (dm * y[br][:, cols] * s * (1.0 - s)).astype(bf16)

    def gs(blk):
        return pl.BlockSpec((tm, WC), lambda i: (i, COL_GM // WC + blk))

    row = pl.BlockSpec((tm, D), lambda i: (i, 0))
    return pl.pallas_call(
        body, grid=(T // tm,), in_specs=[gs(b) for b in range(6)] + [row] * 4,
        out_specs=[row, row, row, pl.BlockSpec((6, tm, WC), lambda i: (DZ_GM // 6, i, 0))],
        out_shape=[jax.ShapeDtypeStruct((T, D), bf16)] * 3 + [jax.ShapeDtypeStruct((DZ_BLOCKS, T, WC), bf16)],
        name="merge_bwd", compiler_params=_params(("parallel",)))(*([z] * 6), dmerged, *ys)


def _adamw(name, g, w, m, v):
    R, C = w.shape
    tr = R
    for cand in (512, 256, 248, 128, 64, 32, 16, 8):
        if R % cand == 0 and cand * C * 4 <= 2 * 1024 * 1024:
            tr = cand
            break
    c1 = 1.0 - ADAM_B1
    c2 = 1.0 - ADAM_B2
    bc1 = 1.0 - ADAM_B1 ** ADAM_STEP
    bc2 = 1.0 - ADAM_B2 ** ADAM_STEP

    def body(g_ref, w_ref, m_ref, v_ref, d_ref, nm_ref, nv_ref):
        g_ = g_ref[...]
        nm = ADAM_B1 * m_ref[...] + c1 * g_
        nv = ADAM_B2 * v_ref[...] + c2 * (g_ * g_)
        nm_ref[...] = nm
        nv_ref[...] = nv
        d_ref[...] = -ADAM_LR * ((nm / bc1) / (jnp.sqrt(nv / bc2) + ADAM_EPS) + ADAM_WD * w_ref[...])

    spec = pl.BlockSpec((tr, C), lambda i: (i, 0))
    return pl.pallas_call(
        body, grid=(R // tr,), in_specs=[spec] * 4, out_specs=[spec] * 3,
        out_shape=[jax.ShapeDtypeStruct((R, C), f32)] * 3, name=name,
        compiler_params=_params(("parallel",)))(g, w, m, v)


def _sum_slots(name, parts):
    _, R, C = parts.shape
    tr = R
    for cand in (256, 128, 64, 32, 16, 8):
        if R % cand == 0 and cand * C * 4 * N_DEV <= 8 * 1024 * 1024:
            tr = cand
            break

    def body(p_ref, o_ref):
        acc = p_ref[0].astype(f32)
        for s in range(1, N_DEV):
            acc = acc + p_ref[s].astype(f32)
        o_ref[...] = acc

    return pl.pallas_call(
        body, grid=(R // tr,), in_specs=[pl.BlockSpec((N_DEV, tr, C), lambda i: (0, i, 0))],
        out_specs=pl.BlockSpec((tr, C), lambda i: (i, 0)), out_shape=jax.ShapeDtypeStruct((R, C), f32),
        name=name, compiler_params=_params(("parallel",)))(parts)


def _row_tile(rows, row_bytes, budget):
    for cand in (512, 256, 128, 64, 32, 16):
        if rows % cand == 0 and cand * row_bytes <= budget:
            return cand
    return rows


def _pair_sum(core, g, theirs):
    R2, C4 = theirs.shape
    tr = _row_tile(R2, C4 * 2, 2 * 1024 * 1024)
    nb = R2 // tr

    def body(core_ref, g_ref, t_ref, o_ref):
        o_ref[...] = (g_ref[...].astype(f32) + t_ref[...].astype(f32)).astype(bf16)

    return pl.pallas_call(
        body,
        grid_spec=pltpu.PrefetchScalarGridSpec(
            num_scalar_prefetch=1, grid=(nb,),
            in_specs=[pl.BlockSpec((tr, C4), lambda i, core_ref: (core_ref[0] * nb + i, 0)),
                      pl.BlockSpec((tr, C4), lambda i, core_ref: (i, 0))],
            out_specs=pl.BlockSpec((tr, C4), lambda i, core_ref: (i, 0))),
        out_shape=jax.ShapeDtypeStruct((R2, C4), bf16), name="pair_sum",
        compiler_params=_params(("parallel",)))(core, g, theirs)


def _chip_sum(chip, mine, others):
    _, R2, C = others.shape
    tr = _row_tile(R2, C * 4, 1024 * 1024)

    def body(chip_ref, m_ref, o_ref, out_ref):
        acc = m_ref[...].astype(f32)
        for s in range(N_CHIPS - 1):
            acc = acc + o_ref[s].astype(f32)
        out_ref[...] = acc

    return pl.pallas_call(
        body,
        grid_spec=pltpu.PrefetchScalarGridSpec(
            num_scalar_prefetch=1, grid=(R2 // tr,),
            in_specs=[pl.BlockSpec((tr, C), lambda i, chip_ref: (i, chip_ref[0])),
                      pl.BlockSpec((N_CHIPS - 1, tr, C), lambda i, chip_ref: (0, i, 0))],
            out_specs=pl.BlockSpec((tr, C), lambda i, chip_ref: (i, 0))),
        out_shape=jax.ShapeDtypeStruct((R2, C), f32), name="chip_sum",
        compiler_params=_params(("parallel",)))(chip, mine, others)


def _place():
    x, y, c = lax.axis_index("x"), lax.axis_index("y"), lax.axis_index("c")
    return x, y, c


def _flip(v, bit):
    return 1 - v if bit else v


CHIP_FLIPS = ((1, 0), (0, 1), (1, 1))


class _Sems:
    def __init__(self, send, recv, local):
        self.send, self.recv, self.loc = send, recv, local
        self.pairs = 0
        self.locals = 0

    def pair(self):
        k = self.pairs
        self.pairs += 1
        return self.send.at[k], self.recv.at[k]

    def local(self):
        k = self.locals
        self.locals += 1
        return self.loc.at[k]


def _remote(src, dst, lands, sems, to):
    s, r = sems.pair()
    copy = pltpu.make_async_remote_copy(src_ref=src, dst_ref=dst, send_sem=s, recv_sem=r, device_id=to, device_id_type=MESH)
    wait = pltpu.make_async_remote_copy(src_ref=lands, dst_ref=lands, send_sem=s, recv_sem=r, device_id=to, device_id_type=MESH)
    return copy, wait


def _exchange(name, build, ins, through, fresh, n_remote, n_local):
    n_in, n_thr = len(ins), len(through)

    def body(*refs):
        in_refs = refs[:n_in]
        out_refs = refs[n_in + n_thr:n_in + 2 * n_thr + len(fresh)]
        send, recv, loc = refs[n_in + 2 * n_thr + len(fresh):]
        locals_, remotes, recvs = build(in_refs, out_refs, _Sems(send, recv, loc))
        for cp in locals_ + remotes:
            cp.start()
        for rv in recvs:
            rv.wait_recv()
        for cp in remotes:
            cp.wait_send()
        for cp in locals_:
            cp.wait()

    out_shape = [jax.ShapeDtypeStruct(t.shape, t.dtype) for t in through] + list(fresh)
    return pl.pallas_call(
        body, in_specs=[ANY] * (n_in + n_thr), out_specs=[ANY] * len(out_shape), out_shape=out_shape,
        scratch_shapes=[pltpu.SemaphoreType.DMA((max(n_remote, 1),)), pltpu.SemaphoreType.DMA((max(n_remote, 1),)),
                        pltpu.SemaphoreType.DMA((max(n_local, 1),))],
        input_output_aliases={n_in + i: i for i in range(n_thr)}, name=name)(*ins, *through)


def _gather_weights(shards):
    n = len(shards)

    def over_ici(src, out, sems):
        x, y, c = _place()
        chip = 2 * x + y
        remotes, recvs = [], []
        for a in range(n):
            for fx, fy in CHIP_FLIPS:
                px, py = _flip(x, fx), _flip(y, fy)
                cp, rv = _remote(src[a].at[c], out[a].at[c, chip], out[a].at[c, 2 * px + py], sems, (px, py, c))
                remotes.append(cp)
                recvs.append(rv)
        return [], remotes, recvs

    def over_d2d(src, out, sems):
        x, y, c = _place()
        remotes, recvs = [], []
        for a in range(n):
            for fx, fy in CHIP_FLIPS:
                owner = 2 * _flip(x, fx) + _flip(y, fy)
                cp, rv = _remote(out[a].at[c, owner], out[a].at[c, owner], out[a].at[1 - c, owner], sems, (x, y, 1 - c))
                remotes.append(cp)
                recvs.append(rv)
        return [], remotes, recvs

    fresh = [jax.ShapeDtypeStruct((2, N_CHIPS) + s.shape[1:], s.dtype) for s in shards]
    partial = _exchange("gather_weights_ici", over_ici, shards, [], fresh, 3 * n, 0)
    others = _exchange("gather_weights_d2d", over_d2d, [], partial, [], 3 * n, 0)
    chip = 2 * lax.axis_index("x") + lax.axis_index("y")
    return [lax.dynamic_update_slice_in_dim(g, s[:, None], chip, axis=1) for g, s in zip(others, shards)]


def _reduce_grads(grads):
    n = len(grads)
    x, y, c = _place()
    core = c.reshape(1).astype(jnp.int32)
    chip = (2 * x + y).reshape(1).astype(jnp.int32)

    def to_sibling(src, out, sems):
        x, y, c = _place()
        remotes, recvs = [], []
        for a in range(n):
            r2 = src[a].shape[0] // 2
            cp, rv = _remote(src[a].at[pl.ds((1 - c) * r2, r2), :], out[a], out[a], sems, (x, y, 1 - c))
            remotes.append(cp)
            recvs.append(rv)
        return [], remotes, recvs

    def across_chips(src, out, sems):
        x, y, c = _place()
        remotes, recvs = [], []
        for a in range(n):
            cw = src[a].shape[1] // N_CHIPS
            for k, (fx, fy) in enumerate(CHIP_FLIPS):
                px, py = _flip(x, fx), _flip(y, fy)
                cp, rv = _remote(src[a].at[:, pl.ds((2 * px + py) * cw, cw)], out[a].at[k], out[a].at[k], sems, (px, py, c))
                remotes.append(cp)
                recvs.append(rv)
        return [], remotes, recvs

    def share(src, out, sems):
        x, y, c = _place()
        remotes, recvs = [], []
        for a in range(n):
            cp, rv = _remote(src[a], out[a], out[a], sems, (x, y, 1 - c))
            remotes.append(cp)
            recvs.append(rv)
        return [], remotes, recvs

    half = [jax.ShapeDtypeStruct((g.shape[0] // 2, g.shape[1]), bf16) for g in grads]
    theirs = _exchange("reduce_grads_pair", to_sibling, grads, [], half, n, 0)
    pair = [_pair_sum(core, g, t) for g, t in zip(grads, theirs)]
    blocks = [jax.ShapeDtypeStruct((N_CHIPS - 1, g.shape[0] // 2, g.shape[1] // N_CHIPS), bf16) for g in grads]
    others = _exchange("reduce_grads_chips", across_chips, pair, [], blocks, 3 * n, 0)
    mine = [_chip_sum(chip, p, o) for p, o in zip(pair, others)]
    sibs = _exchange("reduce_grads_share", share, mine, [], [jax.ShapeDtypeStruct(h.shape, f32) for h in mine], n, 0)
    return [jnp.where(c == 0, jnp.concatenate([h, s], axis=0), jnp.concatenate([s, h], axis=0))
            for h, s in zip(mine, sibs)]


def _gather_small(packed):
    def build(src, out, sems):
        x, y, c = _place()
        me = 4 * x + 2 * y + c
        remotes, recvs = [], []
        for k in range(1, N_DEV):
            px, py, pc = _flip(x, (k >> 2) & 1), _flip(y, (k >> 1) & 1), _flip(c, k & 1)
            cp, rv = _remote(src[0], out[0].at[me], out[0].at[4 * px + 2 * py + pc], sems, (px, py, pc))
            remotes.append(cp)
            recvs.append(rv)
        return [], remotes, recvs

    fresh = [jax.ShapeDtypeStruct((N_DEV,) + packed.shape, f32)]
    others = _exchange("gather_small", build, [packed], [], fresh, N_DEV - 1, 0)[0]
    x, y, c = _place()
    return lax.dynamic_update_slice_in_dim(others, packed[None], 4 * x + 2 * y + c, axis=0)


def _rows8(v):
    return jnp.pad(v[None, :], ((0, 7), (0, 0)))


def _vec_rows(vs):
    return jnp.pad(jnp.stack(vs), ((0, 8 - len(vs)), (0, 0)))


SMALL_ROWS = 224


def _pack_small(conv_vec, conv_dw, pool_vec, pool_w, pre_g, post_g, rel):
    return jnp.concatenate([
        conv_vec, conv_dw, pool_vec, pool_w.reshape(GD, WC),
        _rows8(pre_g).reshape(16, WC), _rows8(post_g).reshape(16, WC),
        jnp.pad(rel, ((0, 0), (0, D - rel.shape[1]))).reshape(16, WC)], axis=0)


def _unpack_small(p):
    conv_vec, pool_vec = p[0:8], p[40:48]
    return dict(
        conv_dw_b=conv_vec[0], conv_ln_g=conv_vec[1], conv_ln_b=conv_vec[2], conv_dw=p[8:8 + CONV_K],
        pool_b=pool_vec[0].reshape(4, GD), pool_scale=pool_vec[1], pool_w=p[48:176].reshape(4, GD, GD),
        pre_norm_g=p[176:192].reshape(8, D)[0], post_norm_g=p[192:208].reshape(8, D)[0],
        rel_bias=p[208:224].reshape(8, D)[:, :2 * MAX_REL + 1])


def _layer_fwd(x2, lw, l, BL, SEQ):
    T = BL * SEQ
    h, ht = _rms_pre(x2, lw["pre_g"][l])
    z = _mm("mm_in", h, lw["w_in"], "nn", T, NCOL, D, 1024, 1536, D, f32, layer=l)
    cv = _conv_fwd(z, lw["dw32"][l], lw["cvec"][l], BL, SEQ)
    at = _attn_fwd(z, lw["bm"][l], BL, SEQ)
    pv = _pool_fwd(z, lw["pw"][l], lw["pvec"][l], BL, SEQ)
    ys = [_mm("mm_branch_out", act, lw[w], "nn", T, D, WC, 1024, D, WC, f32, layer=l)
          for act, w in ((cv, "w_conv"), (at, "w_attn"), (pv, "w_pool"))]
    merged = _merge_fwd(z, ys)
    y = _mm("mm_out", merged, lw["w_out"], "nn", T, D, D, 1024, D, D, f32, layer=l)
    out = _post_fwd(y, x2, lw["post_g"][l])
    return out, dict(x=x2, ht=ht, z=z, acts=(cv, at, pv), ys=ys, merged=merged, y=y)


def _layer_bwd(dout, sv, lw, l, BL, SEQ):
    T = BL * SEQ
    dy, dpost = _post_bwd(dout, sv["y"], lw["post_g"][l])
    dmerged = _mm("mm_dmerged", dy, lw["w_out"], "nt", T, D, D, 1024, D, D, f32, layer=l)
    dw_out_t = _mm("mm_dw_out", dy, sv["merged"], "tn", D, D, T, D, D, 1024, bf16)
    dys_and_dz = _merge_bwd(sv["z"], dmerged, sv["ys"])
    dys, dz = dys_and_dz[:3], dys_and_dz[3]
    dacts = [_mm("mm_dact", dyb, lw[w], "nt", T, WC, D, 1024, WC, D, f32, layer=l)
             for dyb, w in zip(dys, ("w_conv", "w_attn", "w_pool"))]
    dws = [_mm("mm_dw_branch", act, dyb, "tn", WC, D, T, WC, D, 1024, bf16) for act, dyb in zip(sv["acts"], dys)]
    dz, ddw, dcvec = _conv_bwd(sv["z"], dacts[0], dz, lw["dw32"][l], lw["cvec"][l], BL, SEQ)
    dz, dbm = _attn_bwd(sv["z"], dacts[1], dz, lw["bm"][l], BL, SEQ)
    dz, dpw, dpvec = _pool_bwd(sv["z"], dacts[2], dz, lw["pw"][l], lw["pvec"][l], BL, SEQ)
    dh = _mm_dh(dz, lw["w_in_t"], l)
    dw_in = _mm_dw_in(sv["ht"], dz)
    dx, dpre = _pre_bwd(dh, sv["x"], lw["pre_g"][l], dout)
    drel = _bias_table_grad(_bias_colsum(_bias_fold(dbm)))
    small = _pack_small(dcvec, ddw, dpvec, dpw, dpre[0], dpost[0], drel)
    return dx, dict(w_in=dw_in, w_conv_out=dws[0], w_attn_out=dws[1], w_pool_out=dws[2], w_out=dw_out_t), small


def _local_step(x2, tgt2, lw, BL, SEQ):
    saved = []
    cur = x2
    for l in range(DEPTH):
        cur, sv = _layer_fwd(cur, lw, l, BL, SEQ)
        saved.append(sv)
    dout, sq = _loss_head(cur, tgt2)
    big, small = [None] * DEPTH, [None] * DEPTH
    for l in reversed(range(DEPTH)):
        dout, big[l], small[l] = _layer_bwd(dout, saved[l], lw, l, BL, SEQ)
    return sq, dout, big, small


def _full_weights(gathered, pre_norm_g, post_norm_g, conv_dw_full, conv_dw_b, conv_ln_g, conv_ln_b, rel_bias,
                  pool_w, pool_b, pool_scale):
    def cols(g):
        return jnp.transpose(g, (0, 2, 1, 3)).reshape(g.shape[0], g.shape[2], N_CHIPS * g.shape[3])

    g_out = gathered["w_out"]
    w_in = cols(gathered["w_in"])
    return dict(
        w_in=w_in, w_in_t=jnp.transpose(w_in, (0, 2, 1)),
        w_conv=cols(gathered["w_conv_out"]), w_attn=cols(gathered["w_attn_out"]),
        w_pool=cols(gathered["w_pool_out"]), w_out=g_out.reshape(DEPTH, D, D),
        pre_g=pre_norm_g[:, None, :], post_g=post_norm_g[:, None, :],
        dw32=jnp.pad(conv_dw_full, ((0, 0), (0, 32 - CONV_K), (0, 0))),
        cvec=jnp.stack([_vec_rows([conv_dw_b[l], conv_ln_g[l], conv_ln_b[l]]) for l in range(DEPTH)]),
        bm=jnp.stack([_bias_matrix(rel_bias[l]) for l in range(DEPTH)]),
        pw=pool_w.astype(bf16),
        pvec=jnp.stack([_vec_rows([pool_b[l].reshape(WC), pool_scale[l]]) for l in range(DEPTH)]))


BIG = ("w_in", "w_conv_out", "w_attn_out", "w_pool_out", "w_out")
SMALL = ("pre_norm_g", "post_norm_g", "conv_dw_b", "conv_ln_g", "conv_ln_b", "rel_bias", "pool_w", "pool_b", "pool_scale")
ORDER = ("pre_norm_g", "post_norm_g", "w_in", "conv_dw", "conv_dw_b", "conv_ln_g", "conv_ln_b", "w_conv_out",
         "rel_bias", "w_attn_out", "pool_w", "pool_b", "pool_scale", "w_pool_out", "w_out")


def _pack_small_params(p):
    return jnp.concatenate([
        _pack_small(_vec_rows([p["conv_dw_b"][l], p["conv_ln_g"][l], p["conv_ln_b"][l]]), jnp.zeros((32, WC), f32),
                    _vec_rows([p["pool_b"][l].reshape(WC), p["pool_scale"][l]]), p["pool_w"][l],
                    p["pre_norm_g"][l], p["post_norm_g"][l], p["rel_bias"][l])
        for l in range(DEPTH)], axis=0)


def _unpack_small_params(packed):
    layers = [_unpack_small(packed[l * SMALL_ROWS:(l + 1) * SMALL_ROWS]) for l in range(DEPTH)]
    return {k: jnp.stack([layers[l][k] for l in range(DEPTH)]) for k in layers[0]}


def kernel(x, pre_norm_g, post_norm_g, w_in, conv_dw, conv_dw_b, conv_ln_g, conv_ln_b, w_conv_out, rel_bias, w_attn_out, pool_w, pool_b, pool_scale, w_pool_out, w_out, loss_target, m_pre_norm_g, m_post_norm_g, m_w_in, m_conv_dw, m_conv_dw_b, m_conv_ln_g, m_conv_ln_b, m_w_conv_out, m_rel_bias, m_w_attn_out, m_pool_w, m_pool_b, m_pool_scale, m_w_pool_out, m_w_out, v_pre_norm_g, v_post_norm_g, v_w_in, v_conv_dw, v_conv_dw_b, v_conv_ln_g, v_conv_ln_b, v_w_conv_out, v_rel_bias, v_w_attn_out, v_pool_w, v_pool_b, v_pool_scale, v_w_pool_out, v_w_out):
    BL, SEQ, _ = x.shape
    T = BL * SEQ
    w = dict(pre_norm_g=pre_norm_g, post_norm_g=post_norm_g, w_in=w_in, conv_dw=conv_dw, conv_dw_b=conv_dw_b,
             conv_ln_g=conv_ln_g, conv_ln_b=conv_ln_b, w_conv_out=w_conv_out, rel_bias=rel_bias, w_attn_out=w_attn_out,
             pool_w=pool_w, pool_b=pool_b, pool_scale=pool_scale, w_pool_out=w_pool_out, w_out=w_out)
    m = dict(pre_norm_g=m_pre_norm_g, post_norm_g=m_post_norm_g, w_in=m_w_in, conv_dw=m_conv_dw, conv_dw_b=m_conv_dw_b,
             conv_ln_g=m_conv_ln_g, conv_ln_b=m_conv_ln_b, w_conv_out=m_w_conv_out, rel_bias=m_rel_bias,
             w_attn_out=m_w_attn_out, pool_w=m_pool_w, pool_b=m_pool_b, pool_scale=m_pool_scale,
             w_pool_out=m_w_pool_out, w_out=m_w_out)
    v = dict(pre_norm_g=v_pre_norm_g, post_norm_g=v_post_norm_g, w_in=v_w_in, conv_dw=v_conv_dw, conv_dw_b=v_conv_dw_b,
             conv_ln_g=v_conv_ln_g, conv_ln_b=v_conv_ln_b, w_conv_out=v_w_conv_out, rel_bias=v_rel_bias,
             w_attn_out=v_w_attn_out, pool_w=v_pool_w, pool_b=v_pool_b, pool_scale=v_pool_scale,
             w_pool_out=v_w_pool_out, w_out=v_w_out)

    names = BIG + ("conv_dw",)
    gathered = dict(zip(names, _gather_weights([w[k].astype(bf16) for k in BIG] + [conv_dw])))
    conv_dw_full = jnp.transpose(gathered["conv_dw"], (0, 2, 1, 3)).reshape(DEPTH, CONV_K, WC)
    lw = _full_weights(gathered, pre_norm_g, post_norm_g, conv_dw_full, conv_dw_b, conv_ln_g, conv_ln_b, rel_bias,
                       pool_w, pool_b, pool_scale)

    sq, grad_x, big, small = _local_step(x.reshape(T, D), loss_target.reshape(T, D), lw, BL, SEQ)
    loss = lax.psum(0.5 * jnp.sum(sq) / float(D), ("x", "y", "c"))

    summed = _reduce_grads([big[l][k] for k in BIG for l in range(DEPTH)])
    grads, deltas, new_m, new_v = {}, {}, {}, {}
    for i, k in enumerate(BIG):
        g = jnp.stack(summed[i * DEPTH:(i + 1) * DEPTH])
        if k == "w_out":
            g = jnp.transpose(g, (0, 2, 1))
        grads[k] = g
        shape = w[k].shape
        flat2 = lambda a: a.reshape(shape[0] * shape[1], shape[2])
        d_, nm_, nv_ = _adamw("adamw_big", flat2(g), flat2(w[k]), flat2(m[k]), flat2(v[k]))
        deltas[k], new_m[k], new_v[k] = d_.reshape(shape), nm_.reshape(shape), nv_.reshape(shape)

    gsmall = _sum_slots("sum_small", _gather_small(jnp.concatenate(small, axis=0)))
    d_, nm_, nv_ = _adamw("adamw_small", gsmall, _pack_small_params(w), _pack_small_params(m), _pack_small_params(v))
    gs, ds, ms, vs = (_unpack_small_params(a) for a in (gsmall, d_, nm_, nv_))
    for k in SMALL:
        grads[k], deltas[k], new_m[k], new_v[k] = gs[k], ds[k], ms[k], vs[k]
    chip = 2 * lax.axis_index("x") + lax.axis_index("y")
    g_dw = lax.dynamic_slice_in_dim(gs["conv_dw"], chip * GD, GD, axis=2)
    flat2 = lambda a: a.reshape(DEPTH * CONV_K, GD)
    d_, nm_, nv_ = _adamw("adamw_conv_dw", flat2(g_dw), flat2(conv_dw), flat2(m["conv_dw"]), flat2(v["conv_dw"]))
    grads["conv_dw"] = g_dw
    deltas["conv_dw"], new_m["conv_dw"], new_v["conv_dw"] = (a.reshape(conv_dw.shape) for a in (d_, nm_, nv_))

    return (loss, grad_x.reshape(x.shape), *[grads[k] for k in ORDER], *[deltas[k] for k in ORDER],
            *[new_m[k] for k in ORDER], *[new_v[k] for k in ORDER])
```

```python
import numpy as np
import jax
import jax.numpy as jnp
from jax import lax
from jax.experimental import pallas as pl
from jax.experimental.pallas import tpu as pltpu

f32 = jnp.float32
bf16 = jnp.bfloat16

D = 1024
DEPTH = 2
WC = 512
HEAD_DIM = 64
CHUNK = 64
LEFT_CHUNKS = 8
KEY_PAD = LEFT_CHUNKS * CHUNK
MAX_REL = 256
CONV_K = 31
POOL_WINDOWS = (2, 4, 8, 16)
GD = 128
NCOL = 7680
EPS = 1e-6
NEG_INF = -1e30
COL_A, COL_B, COL_CG, COL_Q, COL_K, COL_V, COL_AG, COL_PI, COL_PG, COL_GM = (
    0, 512, 1024, 1536, 2048, 2560, 3072, 3584, 4096, 4608)

ADAM_LR = 0.001
ADAM_B1 = 0.9
ADAM_B2 = 0.999
ADAM_EPS = 1e-08
ADAM_WD = 0.01
ADAM_STEP = 10

QG = 256
KW = KEY_PAD + QG
CT = 128
HALO = 32
PHALO = 16
N_CHIPS = 4
N_DEV = 8
VMEM_LIMIT = 56 * 1024 * 1024
MESH = pl.DeviceIdType.MESH
ANY = pl.BlockSpec(memory_space=pl.ANY)

DZ_BLOCKS = 18
DZ_CONV, DZ_ATTN, DZ_POOL, DZ_GM = 0, 4, 8, 12


def _dz_block(c):
    return c + (c >= 3).astype(jnp.int32) + 2 * (c >= 9).astype(jnp.int32)


def _params(sem=None):
    return pltpu.CompilerParams(dimension_semantics=sem, vmem_limit_bytes=VMEM_LIMIT)


def _sig(x):
    return 1.0 / (1.0 + jnp.exp(-x))


def _dsilu(x, s):
    return s * (1.0 + x * (1.0 - s))


def _colsum(x):
    return jnp.sum(x, axis=0, keepdims=True)


def _rms_pre(x2, g):
    T = x2.shape[0]
    tm = 512

    def body(x_ref, g_ref, h_ref, ht_ref):
        x = x_ref[...]
        r = lax.rsqrt(jnp.mean(x * x, axis=-1, keepdims=True) + EPS)
        h = (x * r) * g_ref[...]
        h_ref[...] = h.astype(bf16)
        ht_ref[...] = h.T.astype(bf16)

    row = pl.BlockSpec((tm, D), lambda i: (i, 0))
    vec = pl.BlockSpec((1, D), lambda i: (0, 0))
    return pl.pallas_call(
        body, grid=(T // tm,), in_specs=[row, vec], out_specs=[row, pl.BlockSpec((D, tm), lambda i: (0, i))],
        out_shape=[jax.ShapeDtypeStruct((T, D), bf16), jax.ShapeDtypeStruct((D, T), bf16)], name="rms_pre",
        compiler_params=_params(("parallel",)))(x2, g)


def _post_fwd(y, x2, g):
    T = x2.shape[0]
    tm = 512

    def body(y_ref, x_ref, g_ref, o_ref):
        y = y_ref[...]
        r = lax.rsqrt(jnp.mean(y * y, axis=-1, keepdims=True) + EPS)
        o_ref[...] = x_ref[...] + (y * r) * g_ref[...]

    row = pl.BlockSpec((tm, D), lambda i: (i, 0))
    vec = pl.BlockSpec((1, D), lambda i: (0, 0))
    return pl.pallas_call(
        body, grid=(T // tm,), in_specs=[row, row, vec], out_specs=row,
        out_shape=jax.ShapeDtypeStruct((T, D), f32), name="post_fwd",
        compiler_params=_params(("parallel",)))(y, x2, g)


def _loss_head(out, tgt):
    T = out.shape[0]
    tm = 512

    def body(o_ref, t_ref, d_ref, l_ref):
        e = o_ref[...] - t_ref[...]
        d_ref[...] = e / float(D)

        @pl.when(pl.program_id(0) == 0)
        def _():
            l_ref[...] = jnp.zeros_like(l_ref)

        l_ref[...] += _colsum(e * e)

    row = pl.BlockSpec((tm, D), lambda i: (i, 0))
    vec = pl.BlockSpec((1, D), lambda i: (0, 0))
    return pl.pallas_call(
        body, grid=(T // tm,), in_specs=[row, row], out_specs=[row, vec],
        out_shape=[jax.ShapeDtypeStruct((T, D), f32), jax.ShapeDtypeStruct((1, D), f32)],
        name="loss_head", compiler_params=_params(("arbitrary",)))(out, tgt)


def _post_bwd(dout, y, g):
    T = y.shape[0]
    tm = 512

    def body(d_ref, y_ref, g_ref, dy_ref, dg_ref):
        y = y_ref[...]
        d = d_ref[...]
        r = lax.rsqrt(jnp.mean(y * y, axis=-1, keepdims=True) + EPS)
        yn = y * r
        dyn = d * g_ref[...]
        dy = r * (dyn - yn * jnp.mean(dyn * yn, axis=-1, keepdims=True))
        dy_ref[...] = dy.astype(bf16)

        @pl.when(pl.program_id(0) == 0)
        def _():
            dg_ref[...] = jnp.zeros_like(dg_ref)

        dg_ref[...] += _colsum(d * yn)

    row = pl.BlockSpec((tm, D), lambda i: (i, 0))
    vec = pl.BlockSpec((1, D), lambda i: (0, 0))
    return pl.pallas_call(
        body, grid=(T // tm,), in_specs=[row, row, vec], out_specs=[row, vec],
        out_shape=[jax.ShapeDtypeStruct((T, D), bf16), jax.ShapeDtypeStruct((1, D), f32)],
        name="post_bwd", compiler_params=_params(("arbitrary",)))(dout, y, g)


def _pre_bwd(dh, x2, g, dout):
    T = x2.shape[0]
    tm = 512

    def body(dh_ref, x_ref, g_ref, d_ref, dx_ref, dg_ref):
        x = x_ref[...]
        dh_ = dh_ref[...]
        r = lax.rsqrt(jnp.mean(x * x, axis=-1, keepdims=True) + EPS)
        xn = x * r
        dxn = dh_ * g_ref[...]
        dx_ref[...] = r * (dxn - xn * jnp.mean(dxn * xn, axis=-1, keepdims=True)) + d_ref[...]

        @pl.when(pl.program_id(0) == 0)
        def _():
            dg_ref[...] = jnp.zeros_like(dg_ref)

        dg_ref[...] += _colsum(dh_ * xn)

    row = pl.BlockSpec((tm, D), lambda i: (i, 0))
    vec = pl.BlockSpec((1, D), lambda i: (0, 0))
    return pl.pallas_call(
        body, grid=(T // tm,), in_specs=[row, row, vec, row], out_specs=[row, vec],
        out_shape=[jax.ShapeDtypeStruct((T, D), f32), jax.ShapeDtypeStruct((1, D), f32)],
        name="pre_bwd", compiler_params=_params(("arbitrary",)))(dh, x2, g, dout)


def _mat_spec(arr, tr, tc, rc, layer):
    if arr.ndim == 2:
        return pl.BlockSpec((tr, tc), rc)

    def index(i, j, k):
        r, c = rc(i, j, k)
        return (layer, r, c)

    return pl.BlockSpec((None, tr, tc), index)


def _mm(name, a, b, mode, m, n, k, tm, tn, tk, out_dtype, layer=None):
    nk = k // tk
    assert m % tm == 0 and n % tn == 0 and k % tk == 0
    if mode == "nn":
        a_spec = _mat_spec(a, tm, tk, lambda i, j, kk: (i, kk), layer)
        b_spec = _mat_spec(b, tk, tn, lambda i, j, kk: (kk, j), layer)
        dn = (((1,), (0,)), ((), ()))
    elif mode == "nt":
        a_spec = _mat_spec(a, tm, tk, lambda i, j, kk: (i, kk), layer)
        b_spec = _mat_spec(b, tn, tk, lambda i, j, kk: (j, kk), layer)
        dn = (((1,), (1,)), ((), ()))
    else:
        a_spec = _mat_spec(a, tk, tm, lambda i, j, kk: (kk, i), layer)
        b_spec = _mat_spec(b, tk, tn, lambda i, j, kk: (kk, j), layer)
        dn = (((0,), (0,)), ((), ()))

    def body(a_ref, b_ref, o_ref, acc_ref):
        p = lax.dot_general(a_ref[...].astype(bf16), b_ref[...].astype(bf16), dn, preferred_element_type=f32)
        if nk == 1:
            o_ref[...] = p.astype(o_ref.dtype)
        else:
            kk = pl.program_id(2)

            @pl.when(kk == 0)
            def _():
                acc_ref[...] = p

            @pl.when(kk > 0)
            def _():
                acc_ref[...] += p

            @pl.when(kk == nk - 1)
            def _():
                o_ref[...] = acc_ref[...].astype(o_ref.dtype)

    acc_shape = (tm, tn) if nk > 1 else (8, 128)
    return pl.pallas_call(
        body, grid=(m // tm, n // tn, nk), in_specs=[a_spec, b_spec],
        out_specs=pl.BlockSpec((tm, tn), lambda i, j, kk: (i, j)),
        out_shape=jax.ShapeDtypeStruct((m, n), out_dtype),
        scratch_shapes=[pltpu.VMEM(acc_shape, f32)], name=name,
        compiler_params=_params(("parallel", "parallel", "arbitrary")))(a, b)


DZ_SPANS = ((DZ_CONV, 3), (DZ_ATTN, 4), (DZ_POOL, 2), (DZ_GM, 6))


def _mm_dh(dz, w_in_t, after):
    T = dz.shape[1]
    tm = 512

    def body(conv_ref, attn_ref, pool_ref, gm_ref, w_ref, after_ref, o_ref):
        acc = None
        col = 0
        for ref, (_, blocks) in zip((conv_ref, attn_ref, pool_ref, gm_ref), DZ_SPANS):
            for b in range(blocks):
                p = jnp.dot(ref[b], w_ref[col * WC:(col + 1) * WC, :], preferred_element_type=f32)
                acc = p if acc is None else acc + p
                col += 1
        o_ref[...] = acc

    spans = [pl.BlockSpec((blocks, tm, WC), lambda i, first=first, blocks=blocks: (first // blocks, i, 0))
             for first, blocks in DZ_SPANS]
    return pl.pallas_call(
        body, grid=(T // tm,),
        in_specs=spans + [pl.BlockSpec((NCOL, D), lambda i: (0, 0), pipeline_mode=pl.Buffered(1)), ANY],
        out_specs=pl.BlockSpec((tm, D), lambda i: (i, 0)), out_shape=jax.ShapeDtypeStruct((T, D), f32),
        name="mm_dh", compiler_params=_params(("parallel",)))(dz, dz, dz, dz, w_in_t, after)


def _mm_dw_in(ht, dz):
    T = dz.shape[1]

    def body(ht_ref, dz_ref, o_ref):
        o_ref[...] = jnp.dot(ht_ref[...], dz_ref[...], preferred_element_type=f32).astype(bf16)

    return pl.pallas_call(
        body, grid=(NCOL // WC,),
        in_specs=[pl.BlockSpec((D, T), lambda j: (0, 0), pipeline_mode=pl.Buffered(1)),
                  pl.BlockSpec((None, T, WC), lambda j: (_dz_block(j), 0, 0))],
        out_specs=pl.BlockSpec((D, WC), lambda j: (0, j)), out_shape=jax.ShapeDtypeStruct((D, NCOL), bf16),
        name="mm_dw_in", compiler_params=_params(("parallel",)))(ht, dz)


def _conv_delays():
    return [(8 * a + b, a, b) for b in range(8) for a in range(4) if 8 * a + b < CONV_K]


def _conv_taps(win, dw_ref):
    rolled = {}
    acc = None
    for d, a, b in _conv_delays():
        if b not in rolled:
            rolled[b] = win if b == 0 else pltpu.roll(win, b, axis=0)
        term = rolled[b][HALO - 8 * a:HALO - 8 * a + CT, :] * dw_ref[pl.ds(CONV_K - 1 - d, 1), :]
        acc = term if acc is None else acc + term
    return acc, rolled


def _conv_fwd(z, dw32, cvec, BL, SEQ):
    T = BL * SEQ
    nct = SEQ // CT

    def body(a_ref, b_ref, cg_ref, dw_ref, vec_ref, o_ref, p_ref):
        p_ref[pl.ds(0, HALO), :] = jnp.zeros((HALO, WC), f32)

        def glu(c, carry):
            r0 = pl.multiple_of(c * CT, CT)
            p_ref[pl.ds(r0 + HALO, CT), :] = a_ref[pl.ds(r0, CT), :] * _sig(b_ref[pl.ds(r0, CT), :])
            return carry

        lax.fori_loop(0, nct, glu, 0)

        def step(c, carry):
            r0 = pl.multiple_of(c * CT, CT)
            u1, _ = _conv_taps(p_ref[pl.ds(r0, CT + HALO), :], dw_ref)
            u1 = u1 + vec_ref[0:1, :]
            xc = u1 - jnp.mean(u1, axis=-1, keepdims=True)
            rs = lax.rsqrt(jnp.mean(xc * xc, axis=-1, keepdims=True) + EPS)
            u2 = (xc * rs) * vec_ref[1:2, :] + vec_ref[2:3, :]
            cg = cg_ref[pl.ds(r0, CT), :]
            o_ref[pl.ds(r0, CT), :] = ((u2 * _sig(u2)) * (cg * _sig(cg))).astype(bf16)
            return carry

        lax.fori_loop(0, nct, step, 0)

    def zs(col):
        return pl.BlockSpec((SEQ, WC), lambda b: (b, col // WC))

    return pl.pallas_call(
        body, grid=(BL,),
        in_specs=[zs(COL_A), zs(COL_B), zs(COL_CG), pl.BlockSpec((32, WC), lambda b: (0, 0)),
                  pl.BlockSpec((8, WC), lambda b: (0, 0))],
        out_specs=pl.BlockSpec((SEQ, WC), lambda b: (b, 0)),
        out_shape=jax.ShapeDtypeStruct((T, WC), bf16),
        scratch_shapes=[pltpu.VMEM((SEQ + HALO, WC), f32)], name="conv_fwd",
        compiler_params=_params(("parallel",)))(z, z, z, dw32, cvec)


def _conv_bwd(z, dcv, dz, dw32, cvec, BL, SEQ):
    nct = SEQ // CT

    def body(a_ref, b_ref, cg_ref, dcv_ref, dzin_ref, dw_ref, vec_ref, dz_ref, ddw_ref, dvec_ref, p_ref, q_ref):
        @pl.when(pl.program_id(0) == 0)
        def _():
            ddw_ref[...] = jnp.zeros_like(ddw_ref)
            dvec_ref[...] = jnp.zeros_like(dvec_ref)

        p_ref[pl.ds(0, HALO), :] = jnp.zeros((HALO, WC), f32)
        q_ref[pl.ds(SEQ, HALO), :] = jnp.zeros((HALO, WC), f32)

        def glu(c, carry):
            r0 = pl.multiple_of(c * CT, CT)
            p_ref[pl.ds(r0 + HALO, CT), :] = a_ref[pl.ds(r0, CT), :] * _sig(b_ref[pl.ds(r0, CT), :])
            return carry

        lax.fori_loop(0, nct, glu, 0)

        def step(c, carry):
            r0 = pl.multiple_of(c * CT, CT)
            u1, rolled = _conv_taps(p_ref[pl.ds(r0, CT + HALO), :], dw_ref)
            u1 = u1 + vec_ref[0:1, :]
            xc = u1 - jnp.mean(u1, axis=-1, keepdims=True)
            rs = lax.rsqrt(jnp.mean(xc * xc, axis=-1, keepdims=True) + EPS)
            nrm = xc * rs
            u2 = nrm * vec_ref[1:2, :] + vec_ref[2:3, :]
            s2 = _sig(u2)
            u3 = u2 * s2
            cg = cg_ref[pl.ds(r0, CT), :]
            scg = _sig(cg)
            dcv_ = dcv_ref[pl.ds(r0, CT), :]
            dz_ref[2, pl.ds(r0, CT), :] = (dcv_ * u3 * _dsilu(cg, scg)).astype(bf16)
            du2 = dcv_ * (cg * scg) * _dsilu(u2, s2)
            dvec_ref[1:2, :] += _colsum(du2 * nrm)
            dvec_ref[2:3, :] += _colsum(du2)
            dn = du2 * vec_ref[1:2, :]
            du1 = rs * (dn - jnp.mean(dn, axis=-1, keepdims=True)
                        - nrm * jnp.mean(dn * nrm, axis=-1, keepdims=True))
            dvec_ref[0:1, :] += _colsum(du1)
            q_ref[pl.ds(r0, CT), :] = du1
            for d, a, b in _conv_delays():
                row = CONV_K - 1 - d
                ddw_ref[pl.ds(row, 1), :] += _colsum(du1 * rolled[b][HALO - 8 * a:HALO - 8 * a + CT, :])
            return carry

        lax.fori_loop(0, nct, step, 0)

        def back(c, carry):
            r0 = pl.multiple_of(c * CT, CT)
            wq = q_ref[pl.ds(r0, CT + HALO), :]
            up = {}
            acc = None
            for d, a, b in _conv_delays():
                if b not in up:
                    up[b] = wq if b == 0 else pltpu.roll(wq, CT + HALO - b, axis=0)
                term = up[b][8 * a:8 * a + CT, :] * dw_ref[pl.ds(CONV_K - 1 - d, 1), :]
                acc = term if acc is None else acc + term
            a_ = a_ref[pl.ds(r0, CT), :]
            sb = _sig(b_ref[pl.ds(r0, CT), :])
            dz_ref[0, pl.ds(r0, CT), :] = (acc * sb).astype(bf16)
            dz_ref[1, pl.ds(r0, CT), :] = (acc * a_ * sb * (1.0 - sb)).astype(bf16)
            return carry

        lax.fori_loop(0, nct, back, 0)

    def zs(col):
        return pl.BlockSpec((SEQ, WC), lambda b: (b, col // WC), pipeline_mode=pl.Buffered(1))

    def const(r):
        return pl.BlockSpec((r, WC), lambda b: (0, 0))

    return pl.pallas_call(
        body, grid=(BL,),
        in_specs=[zs(COL_A), zs(COL_B), zs(COL_CG),
                  pl.BlockSpec((SEQ, WC), lambda b: (b, 0), pipeline_mode=pl.Buffered(1)), ANY, const(32), const(8)],
        out_specs=[pl.BlockSpec((3, SEQ, WC), lambda b: (DZ_CONV // 3, b, 0)), const(32), const(8)],
        out_shape=[jax.ShapeDtypeStruct(dz.shape, bf16), jax.ShapeDtypeStruct((32, WC), f32),
                   jax.ShapeDtypeStruct((8, WC), f32)],
        scratch_shapes=[pltpu.VMEM((SEQ + HALO, WC), f32), pltpu.VMEM((SEQ + HALO, WC), f32)],
        input_output_aliases={4: 0}, name="conv_bwd",
        compiler_params=_params(("arbitrary",)))(z, z, z, dcv, dz, dw32, cvec)


def _pool_counts(r0):
    t1 = r0 + 1 + lax.broadcasted_iota(jnp.int32, (CT, 1), 0)
    return [jnp.minimum(t1, w).astype(f32) for w in POOL_WINDOWS]


def _pool_sums(win, forward):
    n = CT + PHALO

    def sh(x, s):
        return pltpu.roll(x, (n - s) if forward else s, axis=0)

    s2 = win + sh(win, 1)
    s4 = s2[:, GD:] + sh(s2[:, GD:], 2)
    s8 = s4[:, GD:] + sh(s4[:, GD:], 4)
    s16 = s8[:, GD:] + sh(s8[:, GD:], 8)
    lo = 0 if forward else PHALO
    return [s[lo:lo + CT, :GD] for s in (s2, s4, s8, s16)]


def _pool_fwd(z, pw, pvec, BL, SEQ):
    T = BL * SEQ
    nct = SEQ // CT

    def body(pi_ref, pg_ref, pw_ref, vec_ref, o_ref, p_ref):
        p_ref[pl.ds(0, PHALO), :] = jnp.zeros((PHALO, WC), f32)

        def fill(c, carry):
            r0 = pl.multiple_of(c * CT, CT)
            p_ref[pl.ds(r0 + PHALO, CT), :] = pi_ref[pl.ds(r0, CT), :]
            return carry

        lax.fori_loop(0, nct, fill, 0)

        def step(c, carry):
            r0 = pl.multiple_of(c * CT, CT)
            sums = _pool_sums(p_ref[pl.ds(r0, CT + PHALO), :], False)
            cnt = _pool_counts(r0)
            pin = pi_ref[pl.ds(r0, CT), :]
            mixed = []
            for g in range(4):
                pooled = sums[g] / cnt[g] - pin[:, g * GD:(g + 1) * GD]
                mixed.append(jnp.dot(pooled.astype(bf16), pw_ref[g], preferred_element_type=f32))
            m0 = jnp.concatenate(mixed, axis=1) + vec_ref[0:1, :]
            pg = pg_ref[pl.ds(r0, CT), :]
            o_ref[pl.ds(r0, CT), :] = ((m0 * vec_ref[1:2, :]) * (pg * _sig(pg))).astype(bf16)
            return carry

        lax.fori_loop(0, nct, step, 0)

    def zs(col):
        return pl.BlockSpec((SEQ, WC), lambda b: (b, col // WC))

    return pl.pallas_call(
        body, grid=(BL,),
        in_specs=[zs(COL_PI), zs(COL_PG), pl.BlockSpec((4, GD, GD), lambda b: (0, 0, 0)),
                  pl.BlockSpec((8, WC), lambda b: (0, 0))],
        out_specs=pl.BlockSpec((SEQ, WC), lambda b: (b, 0)),
        out_shape=jax.ShapeDtypeStruct((T, WC), bf16),
        scratch_shapes=[pltpu.VMEM((SEQ + PHALO, WC), f32)], name="pool_fwd",
        compiler_params=_params(("parallel",)))(z, z, pw, pvec)


def _pool_bwd(z, dpl, dz, pw, pvec, BL, SEQ):
    nct = SEQ // CT

    def body(pi_ref, pg_ref, dpl_ref, dzin_ref, pw_ref, vec_ref, dz_ref, dpw_ref, dvec_ref, p_ref, e_ref, dp_ref):
        @pl.when(pl.program_id(0) == 0)
        def _():
            dpw_ref[...] = jnp.zeros_like(dpw_ref)
            dvec_ref[...] = jnp.zeros_like(dvec_ref)

        p_ref[pl.ds(0, PHALO), :] = jnp.zeros((PHALO, WC), f32)
        e_ref[pl.ds(SEQ, PHALO), :] = jnp.zeros((PHALO, WC), f32)

        def fill(c, carry):
            r0 = pl.multiple_of(c * CT, CT)
            p_ref[pl.ds(r0 + PHALO, CT), :] = pi_ref[pl.ds(r0, CT), :]
            return carry

        lax.fori_loop(0, nct, fill, 0)

        def step(c, carry):
            r0 = pl.multiple_of(c * CT, CT)
            sums = _pool_sums(p_ref[pl.ds(r0, CT + PHALO), :], False)
            cnt = _pool_counts(r0)
            pin = pi_ref[pl.ds(r0, CT), :]
            pooled = [(sums[g] / cnt[g] - pin[:, g * GD:(g + 1) * GD]).astype(bf16) for g in range(4)]
            m0 = jnp.concatenate(
                [jnp.dot(pooled[g], pw_ref[g], preferred_element_type=f32) for g in range(4)], axis=1) + vec_ref[0:1, :]
            scale = vec_ref[1:2, :]
            pg = pg_ref[pl.ds(r0, CT), :]
            spg = _sig(pg)
            dpl_ = dpl_ref[pl.ds(r0, CT), :]
            dmixed = dpl_ * (pg * spg)
            dz_ref[1, pl.ds(r0, CT), :] = (dpl_ * (m0 * scale) * _dsilu(pg, spg)).astype(bf16)
            dvec_ref[1:2, :] += _colsum(dmixed * m0)
            dm0 = dmixed * scale
            dvec_ref[0:1, :] += _colsum(dm0)
            dps, es = [], []
            for g in range(4):
                dm0g = dm0[:, g * GD:(g + 1) * GD].astype(bf16)
                dpw_ref[g] += lax.dot_general(pooled[g], dm0g, (((0,), (0,)), ((), ())), preferred_element_type=f32)
                dpg = lax.dot_general(dm0g, pw_ref[g], (((1,), (1,)), ((), ())), preferred_element_type=f32)
                dps.append(dpg)
                es.append(dpg / cnt[g])
            dp_ref[pl.ds(r0, CT), :] = jnp.concatenate(dps, axis=1)
            e_ref[pl.ds(r0, CT), :] = jnp.concatenate(es, axis=1)
            return carry

        lax.fori_loop(0, nct, step, 0)

        def back(c, carry):
            r0 = pl.multiple_of(c * CT, CT)
            fs = _pool_sums(e_ref[pl.ds(r0, CT + PHALO), :], True)
            dz_ref[0, pl.ds(r0, CT), :] = (jnp.concatenate(fs, axis=1) - dp_ref[pl.ds(r0, CT), :]).astype(bf16)
            return carry

        lax.fori_loop(0, nct, back, 0)

    def zs(col):
        return pl.BlockSpec((SEQ, WC), lambda b: (b, col // WC), pipeline_mode=pl.Buffered(1))

    return pl.pallas_call(
        body, grid=(BL,),
        in_specs=[zs(COL_PI), zs(COL_PG),
                  pl.BlockSpec((SEQ, WC), lambda b: (b, 0), pipeline_mode=pl.Buffered(1)), ANY,
                  pl.BlockSpec((4, GD, GD), lambda b: (0, 0, 0)), pl.BlockSpec((8, WC), lambda b: (0, 0))],
        out_specs=[pl.BlockSpec((2, SEQ, WC), lambda b: (DZ_POOL // 2, b, 0)),
                   pl.BlockSpec((4, GD, GD), lambda b: (0, 0, 0)), pl.BlockSpec((8, WC), lambda b: (0, 0))],
        out_shape=[jax.ShapeDtypeStruct(dz.shape, bf16), jax.ShapeDtypeStruct((4, GD, GD), f32),
                   jax.ShapeDtypeStruct((8, WC), f32)],
        scratch_shapes=[pltpu.VMEM((SEQ + PHALO, WC), f32), pltpu.VMEM((SEQ + PHALO, WC), f32),
                        pltpu.VMEM((SEQ, WC), f32)],
        input_output_aliases={3: 0}, name="pool_bwd",
        compiler_params=_params(("arbitrary",)))(z, z, dpl, dz, pw, pvec)


def _attn_prologue(q_ref, k_ref, v_ref, qs0, qs1, kp, vp, SEQ):
    head0 = lax.broadcasted_iota(jnp.int32, (1, 2 * HEAD_DIM), 1) < HEAD_DIM
    kp[pl.ds(0, KEY_PAD), :] = jnp.zeros((KEY_PAD, 2 * HEAD_DIM), bf16)
    vp[pl.ds(0, KEY_PAD), :] = jnp.zeros((KEY_PAD, 2 * HEAD_DIM), bf16)

    def fill(g, carry):
        r0 = pl.multiple_of(g * QG, QG)
        q = q_ref[pl.ds(r0, QG), :] * (HEAD_DIM ** -0.5)
        qs0[pl.ds(r0, QG), :] = jnp.where(head0, q, 0.0).astype(bf16)
        qs1[pl.ds(r0, QG), :] = jnp.where(head0, 0.0, q).astype(bf16)
        kp[pl.ds(r0 + KEY_PAD, QG), :] = k_ref[pl.ds(r0, QG), :].astype(bf16)
        vp[pl.ds(r0 + KEY_PAD, QG), :] = v_ref[pl.ds(r0, QG), :].astype(bf16)
        return carry

    lax.fori_loop(0, SEQ // QG, fill, 0)
    return head0


def _attn_probs(qh, kw, bias, r0):
    s = lax.dot_general(qh, kw, (((1,), (1,)), ((), ())), preferred_element_type=f32) + bias
    key_pos = r0 - KEY_PAD + lax.broadcasted_iota(jnp.int32, (1, KW), 1)
    s = jnp.where(key_pos >= 0, s, NEG_INF)
    e = jnp.exp(s - jnp.max(s, axis=-1, keepdims=True))
    return e * (1.0 / jnp.sum(e, axis=-1, keepdims=True))


def _attn_fwd(z, bm, BL, SEQ):
    T = BL * SEQ
    W2 = 2 * HEAD_DIM

    def body(q_ref, k_ref, v_ref, ag_ref, bm_ref, o_ref, qs0, qs1, kp, vp):
        head0 = _attn_prologue(q_ref, k_ref, v_ref, qs0, qs1, kp, vp, SEQ)

        def group(g, carry):
            r0 = pl.multiple_of(g * QG, QG)
            kw = kp[pl.ds(r0, KW), :]
            vw = vp[pl.ds(r0, KW), :]
            outs = []
            for hh, qs in enumerate((qs0, qs1)):
                p = _attn_probs(qs[pl.ds(r0, QG), :], kw, bm_ref[hh], r0)
                outs.append(jnp.dot(p.astype(bf16), vw, preferred_element_type=f32))
            o = jnp.where(head0, outs[0], outs[1])
            ag = ag_ref[pl.ds(r0, QG), :]
            o_ref[pl.ds(r0, QG), :] = (o * (ag * _sig(ag))).astype(bf16)
            return carry

        lax.fori_loop(0, SEQ // QG, group, 0)

    def zs(col):
        return pl.BlockSpec((SEQ, W2), lambda b, hp: (b, col // W2 + hp))

    return pl.pallas_call(
        body, grid=(BL, WC // W2),
        in_specs=[zs(COL_Q), zs(COL_K), zs(COL_V), zs(COL_AG), pl.BlockSpec((2, QG, KW), lambda b, hp: (hp, 0, 0))],
        out_specs=pl.BlockSpec((SEQ, W2), lambda b, hp: (b, hp)),
        out_shape=jax.ShapeDtypeStruct((T, WC), bf16),
        scratch_shapes=[pltpu.VMEM((SEQ, W2), bf16), pltpu.VMEM((SEQ, W2), bf16),
                        pltpu.VMEM((SEQ + KEY_PAD, W2), bf16), pltpu.VMEM((SEQ + KEY_PAD, W2), bf16)],
        name="attn_fwd", compiler_params=_params(("parallel", "parallel")))(z, z, z, z, bm)


def _attn_bwd(z, dat, dz, bm, BL, SEQ):
    W2 = 2 * HEAD_DIM

    def body(q_ref, k_ref, v_ref, ag_ref, dat_ref, dzin_ref, bm_ref, dz_ref, dbm_ref,
             qs0, qs1, kp, vp, do0, do1, dka, dva):
        @pl.when(pl.program_id(1) == 0)
        def _():
            dbm_ref[...] = jnp.zeros_like(dbm_ref)

        head0 = _attn_prologue(q_ref, k_ref, v_ref, qs0, qs1, kp, vp, SEQ)
        dka[...] = jnp.zeros_like(dka)
        dva[...] = jnp.zeros_like(dva)

        def fill(g, carry):
            r0 = pl.multiple_of(g * QG, QG)
            ag = ag_ref[pl.ds(r0, QG), :]
            do = dat_ref[pl.ds(r0, QG), :] * (ag * _sig(ag))
            do0[pl.ds(r0, QG), :] = jnp.where(head0, do, 0.0).astype(bf16)
            do1[pl.ds(r0, QG), :] = jnp.where(head0, 0.0, do).astype(bf16)
            return carry

        lax.fori_loop(0, SEQ // QG, fill, 0)

        def group(g, carry):
            r0 = pl.multiple_of(g * QG, QG)
            kw = kp[pl.ds(r0, KW), :]
            vw = vp[pl.ds(r0, KW), :]
            outs, dqs = [], []
            for hh, (qs, dos) in enumerate(((qs0, do0), (qs1, do1))):
                qh = qs[pl.ds(r0, QG), :]
                doh = dos[pl.ds(r0, QG), :]
                p = _attn_probs(qh, kw, bm_ref[hh], r0)
                pb = p.astype(bf16)
                outs.append(jnp.dot(pb, vw, preferred_element_type=f32))
                dp = lax.dot_general(doh, vw, (((1,), (1,)), ((), ())), preferred_element_type=f32)
                ds_ = p * (dp - jnp.sum(p * dp, axis=-1, keepdims=True))
                dbm_ref[hh] += ds_
                dsb = ds_.astype(bf16)
                dqs.append(jnp.dot(dsb, kw, preferred_element_type=f32))
                dka[pl.ds(r0, KW), :] += lax.dot_general(dsb, qh, (((0,), (0,)), ((), ())), preferred_element_type=f32)
                dva[pl.ds(r0, KW), :] += lax.dot_general(pb, doh, (((0,), (0,)), ((), ())), preferred_element_type=f32)
            o = jnp.where(head0, outs[0], outs[1])
            dq = jnp.where(head0, dqs[0], dqs[1]) * (HEAD_DIM ** -0.5)
            ag = ag_ref[pl.ds(r0, QG), :]
            dz_ref[0, pl.ds(r0, QG), :] = dq.astype(bf16)
            dz_ref[3, pl.ds(r0, QG), :] = (dat_ref[pl.ds(r0, QG), :] * o * _dsilu(ag, _sig(ag))).astype(bf16)
            return carry

        lax.fori_loop(0, SEQ // QG, group, 0)

        def flush(g, carry):
            r0 = pl.multiple_of(g * QG, QG)
            dz_ref[1, pl.ds(r0, QG), :] = dka[pl.ds(r0 + KEY_PAD, QG), :].astype(bf16)
            dz_ref[2, pl.ds(r0, QG), :] = dva[pl.ds(r0 + KEY_PAD, QG), :].astype(bf16)
            return carry

        lax.fori_loop(0, SEQ // QG, flush, 0)

    def zs(col):
        return pl.BlockSpec((SEQ, W2), lambda hp, b: (b, col // W2 + hp))

    return pl.pallas_call(
        body, grid=(WC // W2, BL),
        in_specs=[zs(COL_Q), zs(COL_K), zs(COL_V), zs(COL_AG), pl.BlockSpec((SEQ, W2), lambda hp, b: (b, hp)), ANY,
                  pl.BlockSpec((2, QG, KW), lambda hp, b: (hp, 0, 0))],
        out_specs=[pl.BlockSpec((4, SEQ, W2), lambda hp, b: (DZ_ATTN // 4, b, hp)),
                   pl.BlockSpec((2, QG, KW), lambda hp, b: (hp, 0, 0))],
        out_shape=[jax.ShapeDtypeStruct(dz.shape, bf16), jax.ShapeDtypeStruct((8, QG, KW), f32)],
        scratch_shapes=[pltpu.VMEM((SEQ, W2), bf16), pltpu.VMEM((SEQ, W2), bf16),
                        pltpu.VMEM((SEQ + KEY_PAD, W2), bf16), pltpu.VMEM((SEQ + KEY_PAD, W2), bf16),
                        pltpu.VMEM((SEQ, W2), bf16), pltpu.VMEM((SEQ, W2), bf16),
                        pltpu.VMEM((SEQ + KEY_PAD, W2), f32), pltpu.VMEM((SEQ + KEY_PAD, W2), f32)],
        input_output_aliases={5: 0}, name="attn_bwd",
        compiler_params=_params(("parallel", "arbitrary")))(z, z, z, z, dat, dz, bm)


def _bias_matrix(table):
    n = 2 * MAX_REL
    wd = QG + KW
    e = jnp.concatenate([jnp.broadcast_to(table[:, n:], (8, wd - n)), table[:, n - 1:0:-1], jnp.zeros((8, 1), f32)], axis=1)
    flat = jnp.broadcast_to(e[:, None, :], (8, QG, wd)).reshape(8, QG * wd)
    skew = flat[:, :QG * (wd - 1)].reshape(8, QG, wd - 1)
    vals = skew[:, :, QG - 1:QG - 1 + KW]
    r = np.arange(QG)[:, None] // CHUNK
    j = np.arange(KW)[None, :] // CHUNK
    band = (j >= r) & (j <= r + LEFT_CHUNKS)
    return jnp.where(jnp.asarray(band)[None], vals, NEG_INF)


def _bias_fold(dbm):
    wd = QG + KW
    placed = jnp.pad(dbm, ((0, 0), (0, 0), (QG - 1, 0))).reshape(8, QG * (wd - 1))
    return jnp.pad(placed, ((0, 0), (0, QG))).reshape(8, QG, wd)


def _bias_colsum(folded):
    width = folded.shape[2]

    def body(x_ref, o_ref):
        for h in range(8):
            o_ref[pl.ds(h, 1), :] = _colsum(x_ref[h])

    return pl.pallas_call(body, out_shape=jax.ShapeDtypeStruct((8, width), f32), name="bias_colsum",
                          compiler_params=_params())(folded)


def _bias_table_grad(colsum):
    n = 2 * MAX_REL
    wd = QG + KW
    clipped = jnp.sum(colsum[:, :wd - n], axis=1, keepdims=True)
    return jnp.concatenate([jnp.zeros((8, 1), f32), colsum[:, wd - 2:wd - n - 1:-1], clipped], axis=1)


def _merge_fwd(z, ys):
    T = z.shape[0]
    tm = 512

    def body(g0, g1, g2, y0, y1, y2, o_ref):
        acc = _sig(g0[...]) * y0[...] + _sig(g1[...]) * y1[...] + _sig(g2[...]) * y2[...]
        o_ref[...] = acc.astype(bf16)

    def gs(br):
        return pl.BlockSpec((tm, WC), lambda i, j: (i, (COL_GM + br * D) // WC + j))

    ysp = pl.BlockSpec((tm, WC), lambda i, j: (i, j))
    return pl.pallas_call(
        body, grid=(T // tm, D // WC), in_specs=[gs(0), gs(1), gs(2), ysp, ysp, ysp], out_specs=ysp,
        out_shape=jax.ShapeDtypeStruct((T, D), bf16), name="merge_fwd",
        compiler_params=_params(("parallel", "parallel")))(z, z, z, *ys)


def _merge_bwd(z, dmerged, ys):
    T = z.shape[0]
    tm = 256

    def body(*refs):
        g = refs[0:6]
        dm_ref = refs[6]
        y = refs[7:10]
        dy = refs[10:13]
        dz_ref = refs[13]
        for br in range(3):
            for jh in range(2):
                cols = slice(jh * WC, (jh + 1) * WC)
                s = _sig(g[2 * br + jh][...])
                dm = dm_ref[:, cols]
                dy[br][:, cols] = (dm * s).astype(bf16)
                dz_ref[2 * br + jh] = (dm * y[br][:, cols] * s * (1.0 - s)).astype(bf16)

    def gs(blk):
        return pl.BlockSpec((tm, WC), lambda i: (i, COL_GM // WC + blk))

    row = pl.BlockSpec((tm, D), lambda i: (i, 0))
    return pl.pallas_call(
        body, grid=(T // tm,), in_specs=[gs(b) for b in range(6)] + [row] * 4,
        out_specs=[row, row, row, pl.BlockSpec((6, tm, WC), lambda i: (DZ_GM // 6, i, 0))],
        out_shape=[jax.ShapeDtypeStruct((T, D), bf16)] * 3 + [jax.ShapeDtypeStruct((DZ_BLOCKS, T, WC), bf16)],
        name="merge_bwd", compiler_params=_params(("parallel",)))(*([z] * 6), dmerged, *ys)


def _adamw(name, g, w, m, v):
    R, C = w.shape
    tr = R
    for cand in (512, 256, 248, 128, 64, 32, 16, 8):
        if R % cand == 0 and cand * C * 4 <= 2 * 1024 * 1024:
            tr = cand
            break
    c1 = 1.0 - ADAM_B1
    c2 = 1.0 - ADAM_B2
    bc1 = 1.0 - ADAM_B1 ** ADAM_STEP
    bc2 = 1.0 - ADAM_B2 ** ADAM_STEP

    def body(g_ref, w_ref, m_ref, v_ref, d_ref, nm_ref, nv_ref):
        g_ = g_ref[...]
        nm = ADAM_B1 * m_ref[...] + c1 * g_
        nv = ADAM_B2 * v_ref[...] + c2 * (g_ * g_)
        nm_ref[...] = nm
        nv_ref[...] = nv
        d_ref[...] = -ADAM_LR * ((nm / bc1) / (jnp.sqrt(nv / bc2) + ADAM_EPS) + ADAM_WD * w_ref[...])

    spec = pl.BlockSpec((tr, C), lambda i: (i, 0))
    return pl.pallas_call(
        body, grid=(R // tr,), in_specs=[spec] * 4, out_specs=[spec] * 3,
        out_shape=[jax.ShapeDtypeStruct((R, C), f32)] * 3, name=name,
        compiler_params=_params(("parallel",)))(g, w, m, v)


def _sum_slots(name, parts):
    _, R, C = parts.shape
    tr = R
    for cand in (256, 128, 64, 32, 16, 8):
        if R % cand == 0 and cand * C * 4 * N_DEV <= 8 * 1024 * 1024:
            tr = cand
            break

    def body(p_ref, o_ref):
        acc = p_ref[0].astype(f32)
        for s in range(1, N_DEV):
            acc = acc + p_ref[s].astype(f32)
        o_ref[...] = acc

    return pl.pallas_call(
        body, grid=(R // tr,), in_specs=[pl.BlockSpec((N_DEV, tr, C), lambda i: (0, i, 0))],
        out_specs=pl.BlockSpec((tr, C), lambda i: (i, 0)), out_shape=jax.ShapeDtypeStruct((R, C), f32),
        name=name, compiler_params=_params(("parallel",)))(parts)


def _row_tile(rows, row_bytes, budget):
    for cand in (512, 256, 128, 64, 32, 16):
        if rows % cand == 0 and cand * row_bytes <= budget:
            return cand
    return rows


def _pair_sum(core, g, theirs):
    R2, C4 = theirs.shape
    tr = _row_tile(R2, C4 * 2, 2 * 1024 * 1024)
    nb = R2 // tr

    def body(core_ref, g_ref, t_ref, o_ref):
        o_ref[...] = (g_ref[...].astype(f32) + t_ref[...].astype(f32)).astype(bf16)

    return pl.pallas_call(
        body,
        grid_spec=pltpu.PrefetchScalarGridSpec(
            num_scalar_prefetch=1, grid=(nb,),
            in_specs=[pl.BlockSpec((tr, C4), lambda i, core_ref: (core_ref[0] * nb + i, 0)),
                      pl.BlockSpec((tr, C4), lambda i, core_ref: (i, 0))],
            out_specs=pl.BlockSpec((tr, C4), lambda i, core_ref: (i, 0))),
        out_shape=jax.ShapeDtypeStruct((R2, C4), bf16), name="pair_sum",
        compiler_params=_params(("parallel",)))(core, g, theirs)


def _chip_sum(chip, mine, others):
    _, R2, C = others.shape
    tr = _row_tile(R2, C * 4, 1024 * 1024)

    def body(chip_ref, m_ref, o_ref, out_ref):
        acc = m_ref[...].astype(f32)
        for s in range(N_CHIPS - 1):
            acc = acc + o_ref[s].astype(f32)
        out_ref[...] = acc

    return pl.pallas_call(
        body,
        grid_spec=pltpu.PrefetchScalarGridSpec(
            num_scalar_prefetch=1, grid=(R2 // tr,),
            in_specs=[pl.BlockSpec((tr, C), lambda i, chip_ref: (i, chip_ref[0])),
                      pl.BlockSpec((N_CHIPS - 1, tr, C), lambda i, chip_ref: (0, i, 0))],
            out_specs=pl.BlockSpec((tr, C), lambda i, chip_ref: (i, 0))),
        out_shape=jax.ShapeDtypeStruct((R2, C), f32), name="chip_sum",
        compiler_params=_params(("parallel",)))(chip, mine, others)


def _place():
    x, y, c = lax.axis_index("x"), lax.axis_index("y"), lax.axis_index("c")
    return x, y, c


def _flip(v, bit):
    return 1 - v if bit else v


CHIP_FLIPS = ((1, 0), (0, 1), (1, 1))


class _Sems:
    def __init__(self, send, recv):
        self.send, self.recv = send, recv
        self.pairs = 0

    def pair(self):
        k = self.pairs
        self.pairs += 1
        return self.send.at[k], self.recv.at[k]


def _remote(src, dst, lands, sems, to):
    s, r = sems.pair()
    copy = pltpu.make_async_remote_copy(src_ref=src, dst_ref=dst, send_sem=s, recv_sem=r, device_id=to, device_id_type=MESH)
    wait = pltpu.make_async_remote_copy(src_ref=lands, dst_ref=lands, send_sem=s, recv_sem=r, device_id=to, device_id_type=MESH)
    return copy, wait


def _exchange(name, build, srcs, lands, n_remote):
    n_s, n_l = len(srcs), len(lands)

    def body(*refs):
        send, recv = refs[n_s + 2 * n_l:]
        remotes, recvs = build(refs[:n_s], refs[n_s + n_l:n_s + 2 * n_l], _Sems(send, recv))
        for cp in remotes:
            cp.start()
        for rv in recvs:
            rv.wait_recv()
        for cp in remotes:
            cp.wait_send()

    return pl.pallas_call(
        body, in_specs=[ANY] * (n_s + n_l), out_specs=[ANY] * n_l,
        out_shape=[jax.ShapeDtypeStruct(t.shape, t.dtype) for t in lands],
        scratch_shapes=[pltpu.SemaphoreType.DMA((n_remote,)), pltpu.SemaphoreType.DMA((n_remote,))],
        input_output_aliases={n_s + i: i for i in range(n_l)}, name=name)(*srcs, *lands)


HBM = pl.BlockSpec(memory_space=pltpu.HBM)
SEMS = pl.BlockSpec(memory_space=pltpu.SEMAPHORE)
DATAFLOW = pltpu.SideEffectType.DATAFLOW_SIDE_EFFECTING


def _start(name, build, srcs, lands, n_remote):
    n_s, n_l = len(srcs), len(lands)

    def body(*refs):
        send, recv = refs[n_s + n_l], refs[n_s + n_l + 1]
        remotes, _ = build(refs[:n_s], refs[n_s:n_s + n_l], _Sems(send, recv))
        for cp in remotes:
            cp.start()
        refs[-1][...] = jnp.zeros((8, 128), f32)

    arrays = [pltpu.with_memory_space_constraint(a, pltpu.HBM) for a in (*srcs, *lands)]
    out = pl.pallas_call(
        body, name=name, in_specs=[HBM] * (n_s + n_l),
        out_specs=(SEMS, SEMS, *[HBM] * (n_s + n_l), pl.BlockSpec(memory_space=pltpu.VMEM)),
        out_shape=(pltpu.SemaphoreType.DMA((n_remote,)), pltpu.SemaphoreType.DMA((n_remote,)),
                   *[pltpu.HBM(a.shape, a.dtype) for a in arrays], jax.ShapeDtypeStruct((8, 128), f32)),
        input_output_aliases={i: 2 + i for i in range(n_s + n_l)},
        compiler_params=pltpu.CompilerParams(has_side_effects=DATAFLOW))(*arrays)
    return dict(name=name, build=build, sems=out[:2], srcs=out[2:2 + n_s], lands=out[2 + n_s:2 + n_s + n_l], token=out[-1])


def _wait(started, after):
    srcs, lands, build = started["srcs"], started["lands"], started["build"]
    n_s, n_l = len(srcs), len(lands)

    def body(*refs):
        send, recv = refs[n_s + n_l], refs[n_s + n_l + 1]
        remotes, recvs = build(refs[:n_s], refs[n_s:n_s + n_l], _Sems(send, recv))
        for rv in recvs:
            rv.wait_recv()
        for cp in remotes:
            cp.wait_send()

    out = pl.pallas_call(
        body, name=started["name"] + "_wait", in_specs=[HBM] * (n_s + n_l) + [SEMS, SEMS, ANY],
        out_specs=[HBM] * (n_s + n_l), out_shape=[pltpu.HBM(a.shape, a.dtype) for a in (*srcs, *lands)],
        input_output_aliases={i: i for i in range(n_s + n_l)},
        compiler_params=pltpu.CompilerParams(has_side_effects=DATAFLOW))(*srcs, *lands, *started["sems"], after)
    return out[:n_s], out[n_s:]


def _gather_plans(n_split, n_all):
    def over_ici(src, land, sems):
        x, y, c = _place()
        chip = 2 * x + y
        remotes, recvs = [], []
        for a in range(n_all):
            for fx, fy in CHIP_FLIPS:
                px, py = _flip(x, fx), _flip(y, fy)
                if a < n_split:
                    r2 = src[a].shape[0] // 2
                    rows = pl.ds(c * r2, r2)
                    cp, rv = _remote(src[a].at[rows], land[a].at[chip, rows], land[a].at[2 * px + py, rows], sems, (px, py, c))
                else:
                    cp, rv = _remote(src[a], land[a].at[chip], land[a].at[2 * px + py], sems, (px, py, c))
                remotes.append(cp)
                recvs.append(rv)
        return remotes, recvs

    def over_d2d(src, land, sems):
        x, y, c = _place()
        remotes, recvs = [], []
        for a in range(n_split):
            r2 = land[a].shape[1] // 2
            for fx, fy in CHIP_FLIPS:
                owner = 2 * _flip(x, fx) + _flip(y, fy)
                mine = land[a].at[owner, pl.ds(c * r2, r2)]
                cp, rv = _remote(mine, mine, land[a].at[owner, pl.ds((1 - c) * r2, r2)], sems, (x, y, 1 - c))
                remotes.append(cp)
                recvs.append(rv)
        return remotes, recvs

    return over_ici, over_d2d


def _gather_begin(shards, n_split):
    over_ici, _ = _gather_plans(n_split, len(shards))
    lands = [lax.empty((N_CHIPS,) + s.shape, s.dtype) for s in shards]
    return _start("gather_ici", over_ici, shards, lands, 3 * len(shards))


def _gather_end(started, shards, n_split, after):
    _, over_d2d = _gather_plans(n_split, len(shards))
    lands = _exchange("gather_d2d", over_d2d, [], _wait(started, after)[1], 3 * n_split)
    chip = 2 * lax.axis_index("x") + lax.axis_index("y")
    return [lax.dynamic_update_slice_in_dim(g, s[None], chip, axis=0) for g, s in zip(lands, shards)]


def _reduce_plans(n):
    def to_sibling(src, land, sems):
        x, y, c = _place()
        remotes, recvs = [], []
        for a in range(n):
            r2 = src[a].shape[0] // 2
            cp, rv = _remote(src[a].at[pl.ds((1 - c) * r2, r2), :], land[a], land[a], sems, (x, y, 1 - c))
            remotes.append(cp)
            recvs.append(rv)
        return remotes, recvs

    def across_chips(src, land, sems):
        x, y, c = _place()
        remotes, recvs = [], []
        for a in range(n):
            cw = src[a].shape[1] // N_CHIPS
            for k, (fx, fy) in enumerate(CHIP_FLIPS):
                px, py = _flip(x, fx), _flip(y, fy)
                cp, rv = _remote(src[a].at[:, pl.ds((2 * px + py) * cw, cw)], land[a].at[k], land[a].at[k], sems, (px, py, c))
                remotes.append(cp)
                recvs.append(rv)
        return remotes, recvs

    def share(src, land, sems):
        x, y, c = _place()
        remotes, recvs = [], []
        for a in range(n):
            cp, rv = _remote(src[a], land[a], land[a], sems, (x, y, 1 - c))
            remotes.append(cp)
            recvs.append(rv)
        return remotes, recvs

    return to_sibling, across_chips, share


def _reduce_begin(grads):
    n = len(grads)
    to_sibling, across_chips, _ = _reduce_plans(n)
    core = lax.axis_index("c").reshape(1).astype(jnp.int32)
    theirs = _exchange("reduce_pair", to_sibling, grads,
                       [lax.empty((g.shape[0] // 2, g.shape[1]), bf16) for g in grads], n)
    pair = [_pair_sum(core, g, t) for g, t in zip(grads, theirs)]
    lands = [lax.empty((N_CHIPS - 1, g.shape[0] // 2, g.shape[1] // N_CHIPS), bf16) for g in grads]
    return _start("reduce_chips", across_chips, pair, lands, 3 * n)


def _reduce_end(started, after):
    x, y, c = _place()
    chip = (2 * x + y).reshape(1).astype(jnp.int32)
    pair, others = _wait(started, after)
    _, _, share = _reduce_plans(len(pair))
    mine = [_chip_sum(chip, p, o) for p, o in zip(pair, others)]
    sibs = _exchange("reduce_share", share, mine, [lax.empty(h.shape, f32) for h in mine], len(mine))
    return [jnp.where(c == 0, jnp.concatenate([h, s], axis=0), jnp.concatenate([s, h], axis=0))
            for h, s in zip(mine, sibs)]


def _gather_small(packed):
    def build(src, out, sems):
        x, y, c = _place()
        me = 4 * x + 2 * y + c
        remotes, recvs = [], []
        for k in range(1, N_DEV):
            px, py, pc = _flip(x, (k >> 2) & 1), _flip(y, (k >> 1) & 1), _flip(c, k & 1)
            cp, rv = _remote(src[0], out[0].at[me], out[0].at[4 * px + 2 * py + pc], sems, (px, py, pc))
            remotes.append(cp)
            recvs.append(rv)
        return remotes, recvs

    others = _exchange("gather_small", build, [packed], [lax.empty((N_DEV,) + packed.shape, f32)], N_DEV - 1)[0]
    x, y, c = _place()
    return lax.dynamic_update_slice_in_dim(others, packed[None], 4 * x + 2 * y + c, axis=0)


def _rows8(v):
    return jnp.pad(v[None, :], ((0, 7), (0, 0)))


def _vec_rows(vs):
    return jnp.pad(jnp.stack(vs), ((0, 8 - len(vs)), (0, 0)))


SMALL_ROWS = 224


def _pack_small(conv_vec, conv_dw, pool_vec, pool_w, pre_g, post_g, rel):
    return jnp.concatenate([
        conv_vec, conv_dw, pool_vec, pool_w.reshape(GD, WC),
        _rows8(pre_g).reshape(16, WC), _rows8(post_g).reshape(16, WC),
        jnp.pad(rel, ((0, 0), (0, D - rel.shape[1]))).reshape(16, WC)], axis=0)


def _unpack_small(p):
    conv_vec, pool_vec = p[0:8], p[40:48]
    return dict(
        conv_dw_b=conv_vec[0], conv_ln_g=conv_vec[1], conv_ln_b=conv_vec[2], conv_dw=p[8:8 + CONV_K],
        pool_b=pool_vec[0].reshape(4, GD), pool_scale=pool_vec[1], pool_w=p[48:176].reshape(4, GD, GD),
        pre_norm_g=p[176:192].reshape(8, D)[0], post_norm_g=p[192:208].reshape(8, D)[0],
        rel_bias=p[208:224].reshape(8, D)[:, :2 * MAX_REL + 1])


def _layer_fwd(x2, lw, BL, SEQ):
    T = BL * SEQ
    h, ht = _rms_pre(x2, lw["pre_g"])
    z = _mm("mm_in", h, lw["w_in"], "nn", T, NCOL, D, 1024, 1536, D, f32)
    cv = _conv_fwd(z, lw["dw32"], lw["cvec"], BL, SEQ)
    at = _attn_fwd(z, lw["bm"], BL, SEQ)
    pv = _pool_fwd(z, lw["pw"], lw["pvec"], BL, SEQ)
    ys = [_mm("mm_branch_out", act, lw[w], "nn", T, D, WC, 1024, D, WC, f32)
          for act, w in ((cv, "w_conv_out"), (at, "w_attn_out"), (pv, "w_pool_out"))]
    merged = _merge_fwd(z, ys)
    y = _mm("mm_out", merged, lw["w_out"], "nn", T, D, D, 1024, D, D, f32)
    out = _post_fwd(y, x2, lw["post_g"])
    return out, dict(x=x2, ht=ht, z=z, acts=(cv, at, pv), ys=ys, merged=merged, y=y)


def _layer_bwd(dout, sv, lw, BL, SEQ, meanwhile=None):
    T = BL * SEQ
    dy, dpost = _post_bwd(dout, sv["y"], lw["post_g"])
    dmerged = _mm("mm_dmerged", dy, lw["w_out"], "nt", T, D, D, 1024, D, D, f32)
    dw_out_t = _mm("mm_dw_out", dy, sv["merged"], "tn", D, D, T, D, D, 1024, bf16)
    dys_and_dz = _merge_bwd(sv["z"], dmerged, sv["ys"])
    dys, dz = dys_and_dz[:3], dys_and_dz[3]
    dacts = [_mm("mm_dact", dyb, lw[w], "nt", T, WC, D, 1024, WC, D, f32)
             for dyb, w in zip(dys, ("w_conv_out", "w_attn_out", "w_pool_out"))]
    dws = [_mm("mm_dw_branch", act, dyb, "tn", WC, D, T, WC, D, 1024, bf16) for act, dyb in zip(sv["acts"], dys)]
    if meanwhile is not None:
        meanwhile(dws[2])
    dz, ddw, dcvec = _conv_bwd(sv["z"], dacts[0], dz, lw["dw32"], lw["cvec"], BL, SEQ)
    dz, dbm = _attn_bwd(sv["z"], dacts[1], dz, lw["bm"], BL, SEQ)
    dz, dpw, dpvec = _pool_bwd(sv["z"], dacts[2], dz, lw["pw"], lw["pvec"], BL, SEQ)
    dw_in = _mm_dw_in(sv["ht"], dz)
    reduction = _reduce_begin([dw_in, dws[0], dws[1], dws[2], dw_out_t])
    dh = _mm_dh(dz, lw["w_in_t"], reduction["token"])
    dx, dpre = _pre_bwd(dh, sv["x"], lw["pre_g"], dout)
    drel = _bias_table_grad(_bias_colsum(_bias_fold(dbm)))
    small = _pack_small(dcvec, ddw, dpvec, dpw, dpre[0], dpost[0], drel)
    return dx, reduction, small


BIG = ("w_in", "w_conv_out", "w_attn_out", "w_pool_out", "w_out")


def _layer_shards(w, l):
    return [w[k][l].astype(bf16) for k in BIG] + [w["conv_dw"][l]]


def _layer_weights(gathered, w, l):
    def cols(g):
        return jnp.transpose(g, (1, 0, 2)).reshape(g.shape[1], N_CHIPS * g.shape[2])

    lw = {k: cols(g) for k, g in zip(BIG[:4], gathered[:4])}
    lw["w_out"] = gathered[4].reshape(D, D)
    lw["w_in_t"] = lw["w_in"].T
    lw["pre_g"] = w["pre_norm_g"][l][None]
    lw["post_g"] = w["post_norm_g"][l][None]
    lw["dw32"] = jnp.pad(cols(gathered[5]), ((0, 32 - CONV_K), (0, 0)))
    lw["cvec"] = _vec_rows([w["conv_dw_b"][l], w["conv_ln_g"][l], w["conv_ln_b"][l]])
    lw["bm"] = _bias_matrix(w["rel_bias"][l])
    lw["pw"] = w["pool_w"][l].astype(bf16)
    lw["pvec"] = _vec_rows([w["pool_b"][l].reshape(WC), w["pool_scale"][l]])
    return lw
SMALL = ("pre_norm_g", "post_norm_g", "conv_dw_b", "conv_ln_g", "conv_ln_b", "rel_bias", "pool_w", "pool_b", "pool_scale")
ORDER = ("pre_norm_g", "post_norm_g", "w_in", "conv_dw", "conv_dw_b", "conv_ln_g", "conv_ln_b", "w_conv_out",
         "rel_bias", "w_attn_out", "pool_w", "pool_b", "pool_scale", "w_pool_out", "w_out")


def _pack_small_params(p):
    return jnp.concatenate([
        _pack_small(_vec_rows([p["conv_dw_b"][l], p["conv_ln_g"][l], p["conv_ln_b"][l]]), jnp.zeros((32, WC), f32),
                    _vec_rows([p["pool_b"][l].reshape(WC), p["pool_scale"][l]]), p["pool_w"][l],
                    p["pre_norm_g"][l], p["post_norm_g"][l], p["rel_bias"][l])
        for l in range(DEPTH)], axis=0)


def _unpack_small_params(packed):
    layers = [_unpack_small(packed[l * SMALL_ROWS:(l + 1) * SMALL_ROWS]) for l in range(DEPTH)]
    return {k: jnp.stack([layers[l][k] for l in range(DEPTH)]) for k in layers[0]}


def kernel(x, pre_norm_g, post_norm_g, w_in, conv_dw, conv_dw_b, conv_ln_g, conv_ln_b, w_conv_out, rel_bias, w_attn_out, pool_w, pool_b, pool_scale, w_pool_out, w_out, loss_target, m_pre_norm_g, m_post_norm_g, m_w_in, m_conv_dw, m_conv_dw_b, m_conv_ln_g, m_conv_ln_b, m_w_conv_out, m_rel_bias, m_w_attn_out, m_pool_w, m_pool_b, m_pool_scale, m_w_pool_out, m_w_out, v_pre_norm_g, v_post_norm_g, v_w_in, v_conv_dw, v_conv_dw_b, v_conv_ln_g, v_conv_ln_b, v_w_conv_out, v_rel_bias, v_w_attn_out, v_pool_w, v_pool_b, v_pool_scale, v_w_pool_out, v_w_out):
    BL, SEQ, _ = x.shape
    T = BL * SEQ
    w = dict(pre_norm_g=pre_norm_g, post_norm_g=post_norm_g, w_in=w_in, conv_dw=conv_dw, conv_dw_b=conv_dw_b,
             conv_ln_g=conv_ln_g, conv_ln_b=conv_ln_b, w_conv_out=w_conv_out, rel_bias=rel_bias, w_attn_out=w_attn_out,
             pool_w=pool_w, pool_b=pool_b, pool_scale=pool_scale, w_pool_out=w_pool_out, w_out=w_out)
    m = dict(pre_norm_g=m_pre_norm_g, post_norm_g=m_post_norm_g, w_in=m_w_in, conv_dw=m_conv_dw, conv_dw_b=m_conv_dw_b,
             conv_ln_g=m_conv_ln_g, conv_ln_b=m_conv_ln_b, w_conv_out=m_w_conv_out, rel_bias=m_rel_bias,
             w_attn_out=m_w_attn_out, pool_w=m_pool_w, pool_b=m_pool_b, pool_scale=m_pool_scale,
             w_pool_out=m_w_pool_out, w_out=m_w_out)
    v = dict(pre_norm_g=v_pre_norm_g, post_norm_g=v_post_norm_g, w_in=v_w_in, conv_dw=v_conv_dw, conv_dw_b=v_conv_dw_b,
             conv_ln_g=v_conv_ln_g, conv_ln_b=v_conv_ln_b, w_conv_out=v_w_conv_out, rel_bias=v_rel_bias,
             w_attn_out=v_w_attn_out, pool_w=v_pool_w, pool_b=v_pool_b, pool_scale=v_pool_scale,
             w_pool_out=v_w_pool_out, w_out=v_w_out)

    n_split = len(BIG)
    shards = [_layer_shards(w, l) for l in range(DEPTH)]
    gather0 = _gather_begin(shards[0], n_split)
    lw0 = _layer_weights(_gather_end(gather0, shards[0], n_split, gather0["token"]), w, 0)
    gather1 = _gather_begin(shards[1], n_split)
    lw0["pre_g"] = lw0["pre_g"] + gather1["token"][:1, :1]
    out0, saved0 = _layer_fwd(x.reshape(T, D), lw0, BL, SEQ)
    lw1 = _layer_weights(_gather_end(gather1, shards[1], n_split, out0), w, 1)
    out1, saved1 = _layer_fwd(out0, lw1, BL, SEQ)
    dout, sq = _loss_head(out1, loss_target.reshape(T, D))
    loss = lax.psum(0.5 * jnp.sum(sq) / float(D), ("x", "y", "c"))

    summed = [None] * DEPTH
    dx1, reduction1, small1 = _layer_bwd(dout, saved1, lw1, BL, SEQ)

    def finish_layer1(after):
        summed[1] = _reduce_end(reduction1, after)

    grad_x, reduction0, small0 = _layer_bwd(dx1, saved0, lw0, BL, SEQ, meanwhile=finish_layer1)
    summed[0] = _reduce_end(reduction0, grad_x)
    small = [small0, small1]

    grads, deltas, new_m, new_v = {}, {}, {}, {}
    for i, k in enumerate(BIG):
        g = jnp.stack([summed[l][i] for l in range(DEPTH)])
        if k == "w_out":
            g = jnp.transpose(g, (0, 2, 1))
        grads[k] = g
        shape = w[k].shape
        flat2 = lambda a: a.reshape(shape[0] * shape[1], shape[2])
        d_, nm_, nv_ = _adamw("adamw_big", flat2(g), flat2(w[k]), flat2(m[k]), flat2(v[k]))
        deltas[k], new_m[k], new_v[k] = d_.reshape(shape), nm_.reshape(shape), nv_.reshape(shape)

    gsmall = _sum_slots("sum_small", _gather_small(jnp.concatenate(small, axis=0)))
    d_, nm_, nv_ = _adamw("adamw_small", gsmall, _pack_small_params(w), _pack_small_params(m), _pack_small_params(v))
    gs, ds, ms, vs = (_unpack_small_params(a) for a in (gsmall, d_, nm_, nv_))
    for k in SMALL:
        grads[k], deltas[k], new_m[k], new_v[k] = gs[k], ds[k], ms[k], vs[k]
    chip = 2 * lax.axis_index("x") + lax.axis_index("y")
    g_dw = lax.dynamic_slice_in_dim(gs["conv_dw"], chip * GD, GD, axis=2)
    flat2 = lambda a: a.reshape(DEPTH * CONV_K, GD)
    d_, nm_, nv_ = _adamw("adamw_conv_dw", flat2(g_dw), flat2(conv_dw), flat2(m["conv_dw"]), flat2(v["conv_dw"]))
    grads["conv_dw"] = g_dw
    deltas["conv_dw"], new_m["conv_dw"], new_v["conv_dw"] = (a.reshape(conv_dw.shape) for a in (d_, nm_, nv_))

    return (loss, grad_x.reshape(x.shape), *[grads[k] for k in ORDER], *[deltas[k] for k in ORDER],
            *[new_m[k] for k in ORDER], *[new_v[k] for k in ORDER])
```

```python
import numpy as np
import jax
import jax.numpy as jnp
from jax import lax
from jax.experimental import pallas as pl
from jax.experimental.pallas import tpu as pltpu

f32 = jnp.float32
bf16 = jnp.bfloat16

D = 1024
DEPTH = 2
WC = 512
HEAD_DIM = 64
CHUNK = 64
LEFT_CHUNKS = 8
KEY_PAD = LEFT_CHUNKS * CHUNK
MAX_REL = 256
CONV_K = 31
POOL_WINDOWS = (2, 4, 8, 16)
GD = 128
NCOL = 7680
EPS = 1e-6
NEG_INF = -1e30
COL_A, COL_B, COL_CG, COL_Q, COL_K, COL_V, COL_AG, COL_PI, COL_PG, COL_GM = (
    0, 512, 1024, 1536, 2048, 2560, 3072, 3584, 4096, 4608)

ADAM_LR = 0.001
ADAM_B1 = 0.9
ADAM_B2 = 0.999
ADAM_EPS = 1e-08
ADAM_WD = 0.01
ADAM_STEP = 10

QG = 128
KW = KEY_PAD + QG
BIAS_VARIANTS = KEY_PAD // QG + 1
CT = 128
HALO = 32
PHALO = 16
N_CHIPS = 4
N_DEV = 8
VMEM_LIMIT = 56 * 1024 * 1024
MESH = pl.DeviceIdType.MESH
ANY = pl.BlockSpec(memory_space=pl.ANY)

DZ_BLOCKS = 18
DZ_CONV, DZ_ATTN, DZ_POOL, DZ_GM = 0, 4, 8, 12


def _dz_block(c):
    return c + (c >= 3).astype(jnp.int32) + 2 * (c >= 9).astype(jnp.int32)


def _params(sem=None):
    return pltpu.CompilerParams(dimension_semantics=sem, vmem_limit_bytes=VMEM_LIMIT)


def _sig(x):
    return 1.0 / (1.0 + jnp.exp(-x))


def _dsilu(x, s):
    return s * (1.0 + x * (1.0 - s))


def _colsum(x):
    return jnp.sum(x, axis=0, keepdims=True)


def _rms_pre(x2, g):
    T = x2.shape[0]
    tm = 512

    def body(x_ref, g_ref, h_ref, ht_ref):
        x = x_ref[...]
        r = lax.rsqrt(jnp.mean(x * x, axis=-1, keepdims=True) + EPS)
        h = (x * r) * g_ref[...]
        h_ref[...] = h.astype(bf16)
        ht_ref[...] = h.T.astype(bf16)

    row = pl.BlockSpec((tm, D), lambda i: (i, 0))
    vec = pl.BlockSpec((1, D), lambda i: (0, 0))
    return pl.pallas_call(
        body, grid=(T // tm,), in_specs=[row, vec], out_specs=[row, pl.BlockSpec((D, tm), lambda i: (0, i))],
        out_shape=[jax.ShapeDtypeStruct((T, D), bf16), jax.ShapeDtypeStruct((D, T), bf16)], name="rms_pre",
        compiler_params=_params(("parallel",)))(x2, g)


def _post_fwd(y, x2, g):
    T = x2.shape[0]
    tm = 512

    def body(y_ref, x_ref, g_ref, o_ref):
        y = y_ref[...]
        r = lax.rsqrt(jnp.mean(y * y, axis=-1, keepdims=True) + EPS)
        o_ref[...] = x_ref[...] + (y * r) * g_ref[...]

    row = pl.BlockSpec((tm, D), lambda i: (i, 0))
    vec = pl.BlockSpec((1, D), lambda i: (0, 0))
    return pl.pallas_call(
        body, grid=(T // tm,), in_specs=[row, row, vec], out_specs=row,
        out_shape=jax.ShapeDtypeStruct((T, D), f32), name="post_fwd",
        compiler_params=_params(("parallel",)))(y, x2, g)


def _loss_head(out, tgt):
    T = out.shape[0]
    tm = 512

    def body(o_ref, t_ref, d_ref, l_ref):
        e = o_ref[...] - t_ref[...]
        d_ref[...] = e / float(D)

        @pl.when(pl.program_id(0) == 0)
        def _():
            l_ref[...] = jnp.zeros_like(l_ref)

        l_ref[...] += _colsum(e * e)

    row = pl.BlockSpec((tm, D), lambda i: (i, 0))
    vec = pl.BlockSpec((1, D), lambda i: (0, 0))
    return pl.pallas_call(
        body, grid=(T // tm,), in_specs=[row, row], out_specs=[row, vec],
        out_shape=[jax.ShapeDtypeStruct((T, D), f32), jax.ShapeDtypeStruct((1, D), f32)],
        name="loss_head", compiler_params=_params(("arbitrary",)))(out, tgt)


def _post_bwd(dout, y, g):
    T = y.shape[0]
    tm = 512

    def body(d_ref, y_ref, g_ref, dy_ref, dg_ref):
        y = y_ref[...]
        d = d_ref[...]
        r = lax.rsqrt(jnp.mean(y * y, axis=-1, keepdims=True) + EPS)
        yn = y * r
        dyn = d * g_ref[...]
        dy = r * (dyn - yn * jnp.mean(dyn * yn, axis=-1, keepdims=True))
        dy_ref[...] = dy.astype(bf16)

        @pl.when(pl.program_id(0) == 0)
        def _():
            dg_ref[...] = jnp.zeros_like(dg_ref)

        dg_ref[...] += _colsum(d * yn)

    row = pl.BlockSpec((tm, D), lambda i: (i, 0))
    vec = pl.BlockSpec((1, D), lambda i: (0, 0))
    return pl.pallas_call(
        body, grid=(T // tm,), in_specs=[row, row, vec], out_specs=[row, vec],
        out_shape=[jax.ShapeDtypeStruct((T, D), bf16), jax.ShapeDtypeStruct((1, D), f32)],
        name="post_bwd", compiler_params=_params(("arbitrary",)))(dout, y, g)


def _pre_bwd(dh, x2, g, dout):
    T = x2.shape[0]
    tm = 512

    def body(dh_ref, x_ref, g_ref, d_ref, dx_ref, dg_ref):
        x = x_ref[...]
        dh_ = dh_ref[...]
        r = lax.rsqrt(jnp.mean(x * x, axis=-1, keepdims=True) + EPS)
        xn = x * r
        dxn = dh_ * g_ref[...]
        dx_ref[...] = r * (dxn - xn * jnp.mean(dxn * xn, axis=-1, keepdims=True)) + d_ref[...]

        @pl.when(pl.program_id(0) == 0)
        def _():
            dg_ref[...] = jnp.zeros_like(dg_ref)

        dg_ref[...] += _colsum(dh_ * xn)

    row = pl.BlockSpec((tm, D), lambda i: (i, 0))
    vec = pl.BlockSpec((1, D), lambda i: (0, 0))
    return pl.pallas_call(
        body, grid=(T // tm,), in_specs=[row, row, vec, row], out_specs=[row, vec],
        out_shape=[jax.ShapeDtypeStruct((T, D), f32), jax.ShapeDtypeStruct((1, D), f32)],
        name="pre_bwd", compiler_params=_params(("arbitrary",)))(dh, x2, g, dout)


def _mm(name, a, b, mode, m, n, k, tm, tn, tk, out_dtype, after=None):
    nk = k // tk
    assert m % tm == 0 and n % tn == 0 and k % tk == 0
    if mode == "nn":
        a_spec = pl.BlockSpec((tm, tk), lambda i, j, kk: (i, kk))
        b_spec = pl.BlockSpec((tk, tn), lambda i, j, kk: (kk, j))
        dn = (((1,), (0,)), ((), ()))
    elif mode == "nt":
        a_spec = pl.BlockSpec((tm, tk), lambda i, j, kk: (i, kk))
        b_spec = pl.BlockSpec((tn, tk), lambda i, j, kk: (j, kk))
        dn = (((1,), (1,)), ((), ()))
    else:
        a_spec = pl.BlockSpec((tk, tm), lambda i, j, kk: (kk, i))
        b_spec = pl.BlockSpec((tk, tn), lambda i, j, kk: (kk, j))
        dn = (((0,), (0,)), ((), ()))
    extra = [] if after is None else [after]

    def body(a_ref, b_ref, *rest):
        o_ref, acc_ref = rest[len(extra):]
        p = lax.dot_general(a_ref[...].astype(bf16), b_ref[...].astype(bf16), dn, preferred_element_type=f32)
        if nk == 1:
            o_ref[...] = p.astype(o_ref.dtype)
        else:
            kk = pl.program_id(2)

            @pl.when(kk == 0)
            def _():
                acc_ref[...] = p

            @pl.when(kk > 0)
            def _():
                acc_ref[...] += p

            @pl.when(kk == nk - 1)
            def _():
                o_ref[...] = acc_ref[...].astype(o_ref.dtype)

    acc_shape = (tm, tn) if nk > 1 else (8, 128)
    return pl.pallas_call(
        body, grid=(m // tm, n // tn, nk), in_specs=[a_spec, b_spec] + [ANY] * len(extra),
        out_specs=pl.BlockSpec((tm, tn), lambda i, j, kk: (i, j)),
        out_shape=jax.ShapeDtypeStruct((m, n), out_dtype),
        scratch_shapes=[pltpu.VMEM(acc_shape, f32)], name=name,
        compiler_params=_params(("parallel", "parallel", "arbitrary")))(a, b, *extra)


DZ_SPANS = ((DZ_CONV, 3), (DZ_ATTN, 4), (DZ_POOL, 2), (DZ_GM, 6))


def _mm_dh(dz, w_in_t, after):
    T = dz.shape[1]
    tm = 512

    def body(conv_ref, attn_ref, pool_ref, gm_ref, w_ref, after_ref, o_ref):
        acc = None
        col = 0
        for ref, (_, blocks) in zip((conv_ref, attn_ref, pool_ref, gm_ref), DZ_SPANS):
            for b in range(blocks):
                p = jnp.dot(ref[b], w_ref[col * WC:(col + 1) * WC, :], preferred_element_type=f32)
                acc = p if acc is None else acc + p
                col += 1
        o_ref[...] = acc

    spans = [pl.BlockSpec((blocks, tm, WC), lambda i, first=first, blocks=blocks: (first // blocks, i, 0))
             for first, blocks in DZ_SPANS]
    return pl.pallas_call(
        body, grid=(T // tm,),
        in_specs=spans + [pl.BlockSpec((NCOL, D), lambda i: (0, 0), pipeline_mode=pl.Buffered(1)), ANY],
        out_specs=pl.BlockSpec((tm, D), lambda i: (i, 0)), out_shape=jax.ShapeDtypeStruct((T, D), f32),
        name="mm_dh", compiler_params=_params(("parallel",)))(dz, dz, dz, dz, w_in_t, after)


def _mm_dw_in(ht, dz):
    T = dz.shape[1]

    def body(ht_ref, dz_ref, o_ref):
        o_ref[...] = jnp.dot(ht_ref[...], dz_ref[...], preferred_element_type=f32).astype(bf16)

    return pl.pallas_call(
        body, grid=(NCOL // WC,),
        in_specs=[pl.BlockSpec((D, T), lambda j: (0, 0), pipeline_mode=pl.Buffered(1)),
                  pl.BlockSpec((None, T, WC), lambda j: (_dz_block(j), 0, 0))],
        out_specs=pl.BlockSpec((D, WC), lambda j: (0, j)), out_shape=jax.ShapeDtypeStruct((D, NCOL), bf16),
        name="mm_dw_in", compiler_params=_params(("parallel",)))(ht, dz)


def _conv_delays():
    return [(8 * a + b, a, b) for b in range(8) for a in range(4) if 8 * a + b < CONV_K]


def _conv_rolls(win):
    return [win if b == 0 else pltpu.roll(win, b, axis=0) for b in range(8)]


def _conv_taps(rolled, dw_ref):
    acc = None
    for d, a, b in _conv_delays():
        term = rolled[b][HALO - 8 * a:HALO - 8 * a + CT, :] * dw_ref[pl.ds(CONV_K - 1 - d, 1), :]
        acc = term if acc is None else acc + term
    return acc


def _conv_fwd(z, dw32, cvec, BL, SEQ):
    T = BL * SEQ
    nct = SEQ // CT

    def body(a_ref, b_ref, cg_ref, dw_ref, vec_ref, o_ref, u1_ref, p_ref):
        p_ref[pl.ds(0, HALO), :] = jnp.zeros((HALO, WC), f32)

        def glu(c, carry):
            r0 = pl.multiple_of(c * CT, CT)
            p_ref[pl.ds(r0 + HALO, CT), :] = a_ref[pl.ds(r0, CT), :] * _sig(b_ref[pl.ds(r0, CT), :])
            return carry

        lax.fori_loop(0, nct, glu, 0)

        def step(c, carry):
            r0 = pl.multiple_of(c * CT, CT)
            u1 = _conv_taps(_conv_rolls(p_ref[pl.ds(r0, CT + HALO), :]), dw_ref) + vec_ref[0:1, :]
            u1_ref[pl.ds(r0, CT), :] = u1
            xc = u1 - jnp.mean(u1, axis=-1, keepdims=True)
            rs = lax.rsqrt(jnp.mean(xc * xc, axis=-1, keepdims=True) + EPS)
            u2 = (xc * rs) * vec_ref[1:2, :] + vec_ref[2:3, :]
            cg = cg_ref[pl.ds(r0, CT), :]
            o_ref[pl.ds(r0, CT), :] = ((u2 * _sig(u2)) * (cg * _sig(cg))).astype(bf16)
            return carry

        lax.fori_loop(0, nct, step, 0)

    def zs(col):
        return pl.BlockSpec((SEQ, WC), lambda b: (b, col // WC))

    seq = pl.BlockSpec((SEQ, WC), lambda b: (b, 0))
    return pl.pallas_call(
        body, grid=(BL,),
        in_specs=[zs(COL_A), zs(COL_B), zs(COL_CG), pl.BlockSpec((32, WC), lambda b: (0, 0)),
                  pl.BlockSpec((8, WC), lambda b: (0, 0))],
        out_specs=[seq, seq],
        out_shape=[jax.ShapeDtypeStruct((T, WC), bf16), jax.ShapeDtypeStruct((T, WC), f32)],
        scratch_shapes=[pltpu.VMEM((SEQ + HALO, WC), f32)], name="conv_fwd",
        compiler_params=_params(("parallel",)))(z, z, z, dw32, cvec)


def _conv_bwd(z, u1, dcv, dz, dw32, cvec, BL, SEQ):
    nct = SEQ // CT

    def body(a_ref, b_ref, cg_ref, u1_ref, dcv_ref, dzin_ref, dw_ref, vec_ref, dz_ref, ddw_ref, dvec_ref,
             p_ref, q_ref, taps_ref):
        @pl.when(pl.program_id(0) == 0)
        def _():
            ddw_ref[...] = jnp.zeros_like(ddw_ref)
            dvec_ref[...] = jnp.zeros_like(dvec_ref)

        p_ref[pl.ds(0, HALO), :] = jnp.zeros((HALO, WC), f32)
        q_ref[pl.ds(SEQ, HALO), :] = jnp.zeros((HALO, WC), f32)

        def glu(c, carry):
            r0 = pl.multiple_of(c * CT, CT)
            p_ref[pl.ds(r0 + HALO, CT), :] = a_ref[pl.ds(r0, CT), :] * _sig(b_ref[pl.ds(r0, CT), :])
            return carry

        lax.fori_loop(0, nct, glu, 0)

        def step(c, carry):
            r0 = pl.multiple_of(c * CT, CT)
            rolled = _conv_rolls(p_ref[pl.ds(r0, CT + HALO), :])
            u1 = u1_ref[pl.ds(r0, CT), :]
            xc = u1 - jnp.mean(u1, axis=-1, keepdims=True)
            rs = lax.rsqrt(jnp.mean(xc * xc, axis=-1, keepdims=True) + EPS)
            nrm = xc * rs
            u2 = nrm * vec_ref[1:2, :] + vec_ref[2:3, :]
            s2 = _sig(u2)
            u3 = u2 * s2
            cg = cg_ref[pl.ds(r0, CT), :]
            scg = _sig(cg)
            dcv_ = dcv_ref[pl.ds(r0, CT), :]
            dz_ref[2, pl.ds(r0, CT), :] = (dcv_ * u3 * _dsilu(cg, scg)).astype(bf16)
            du2 = dcv_ * (cg * scg) * _dsilu(u2, s2)
            dvec_ref[1:2, :] += _colsum(du2 * nrm)
            dvec_ref[2:3, :] += _colsum(du2)
            dn = du2 * vec_ref[1:2, :]
            du1 = rs * (dn - jnp.mean(dn, axis=-1, keepdims=True)
                        - nrm * jnp.mean(dn * nrm, axis=-1, keepdims=True))
            dvec_ref[0:1, :] += _colsum(du1)
            q_ref[pl.ds(r0, CT), :] = du1
            for d, a, b in _conv_delays():
                prod = du1 * rolled[b][HALO - 8 * a:HALO - 8 * a + CT, :]
                taps_ref[CONV_K - 1 - d] += jnp.sum(prod.reshape(CT // 8, 8, WC), axis=0)
            return carry

        taps_ref[...] = jnp.zeros_like(taps_ref)
        lax.fori_loop(0, nct, step, 0)
        for row in range(CONV_K):
            ddw_ref[pl.ds(row, 1), :] += _colsum(taps_ref[row])

        def back(c, carry):
            r0 = pl.multiple_of(c * CT, CT)
            wq = q_ref[pl.ds(r0, CT + HALO), :]
            up = {}
            acc = None
            for d, a, b in _conv_delays():
                if b not in up:
                    up[b] = wq if b == 0 else pltpu.roll(wq, CT + HALO - b, axis=0)
                term = up[b][8 * a:8 * a + CT, :] * dw_ref[pl.ds(CONV_K - 1 - d, 1), :]
                acc = term if acc is None else acc + term
            a_ = a_ref[pl.ds(r0, CT), :]
            sb = _sig(b_ref[pl.ds(r0, CT), :])
            dz_ref[0, pl.ds(r0, CT), :] = (acc * sb).astype(bf16)
            dz_ref[1, pl.ds(r0, CT), :] = (acc * a_ * sb * (1.0 - sb)).astype(bf16)
            return carry

        lax.fori_loop(0, nct, back, 0)

    def zs(col):
        return pl.BlockSpec((SEQ, WC), lambda b: (b, col // WC), pipeline_mode=pl.Buffered(1))

    def const(r):
        return pl.BlockSpec((r, WC), lambda b: (0, 0))

    seq = pl.BlockSpec((SEQ, WC), lambda b: (b, 0), pipeline_mode=pl.Buffered(1))
    return pl.pallas_call(
        body, grid=(BL,),
        in_specs=[zs(COL_A), zs(COL_B), zs(COL_CG), seq, seq, ANY, const(32), const(8)],
        out_specs=[pl.BlockSpec((3, SEQ, WC), lambda b: (DZ_CONV // 3, b, 0)), const(32), const(8)],
        out_shape=[jax.ShapeDtypeStruct(dz.shape, bf16), jax.ShapeDtypeStruct((32, WC), f32),
                   jax.ShapeDtypeStruct((8, WC), f32)],
        scratch_shapes=[pltpu.VMEM((SEQ + HALO, WC), f32), pltpu.VMEM((SEQ + HALO, WC), f32),
                        pltpu.VMEM((CONV_K, 8, WC), f32)],
        input_output_aliases={5: 0}, name="conv_bwd",
        compiler_params=_params(("arbitrary",)))(z, z, z, u1, dcv, dz, dw32, cvec)


def _pool_counts(r0):
    t1 = r0 + 1 + lax.broadcasted_iota(jnp.int32, (CT, 1), 0)
    return [jnp.minimum(t1, w).astype(f32) for w in POOL_WINDOWS]


def _pool_sums(win, forward):
    n = CT + PHALO

    def sh(x, s):
        return pltpu.roll(x, (n - s) if forward else s, axis=0)

    s2 = win + sh(win, 1)
    s4 = s2[:, GD:] + sh(s2[:, GD:], 2)
    s8 = s4[:, GD:] + sh(s4[:, GD:], 4)
    s16 = s8[:, GD:] + sh(s8[:, GD:], 8)
    lo = 0 if forward else PHALO
    return [s[lo:lo + CT, :GD] for s in (s2, s4, s8, s16)]


def _pool_fwd(z, pw, pvec, BL, SEQ):
    T = BL * SEQ
    nct = SEQ // CT

    def body(pi_ref, pg_ref, pw_ref, vec_ref, o_ref, p_ref):
        p_ref[pl.ds(0, PHALO), :] = jnp.zeros((PHALO, WC), f32)

        def fill(c, carry):
            r0 = pl.multiple_of(c * CT, CT)
            p_ref[pl.ds(r0 + PHALO, CT), :] = pi_ref[pl.ds(r0, CT), :]
            return carry

        lax.fori_loop(0, nct, fill, 0)

        def step(c, carry):
            r0 = pl.multiple_of(c * CT, CT)
            sums = _pool_sums(p_ref[pl.ds(r0, CT + PHALO), :], False)
            cnt = _pool_counts(r0)
            pin = pi_ref[pl.ds(r0, CT), :]
            mixed = []
            for g in range(4):
                pooled = sums[g] / cnt[g] - pin[:, g * GD:(g + 1) * GD]
                mixed.append(jnp.dot(pooled.astype(bf16), pw_ref[g], preferred_element_type=f32))
            m0 = jnp.concatenate(mixed, axis=1) + vec_ref[0:1, :]
            pg = pg_ref[pl.ds(r0, CT), :]
            o_ref[pl.ds(r0, CT), :] = ((m0 * vec_ref[1:2, :]) * (pg * _sig(pg))).astype(bf16)
            return carry

        lax.fori_loop(0, nct, step, 0)

    def zs(col):
        return pl.BlockSpec((SEQ, WC), lambda b: (b, col // WC))

    return pl.pallas_call(
        body, grid=(BL,),
        in_specs=[zs(COL_PI), zs(COL_PG), pl.BlockSpec((4, GD, GD), lambda b: (0, 0, 0)),
                  pl.BlockSpec((8, WC), lambda b: (0, 0))],
        out_specs=pl.BlockSpec((SEQ, WC), lambda b: (b, 0)),
        out_shape=jax.ShapeDtypeStruct((T, WC), bf16),
        scratch_shapes=[pltpu.VMEM((SEQ + PHALO, WC), f32)], name="pool_fwd",
        compiler_params=_params(("parallel",)))(z, z, pw, pvec)


def _pool_bwd(z, dpl, dz, pw, pvec, BL, SEQ):
    nct = SEQ // CT

    def body(pi_ref, pg_ref, dpl_ref, dzin_ref, pw_ref, vec_ref, dz_ref, dpw_ref, dvec_ref, p_ref, e_ref, dp_ref):
        @pl.when(pl.program_id(0) == 0)
        def _():
            dpw_ref[...] = jnp.zeros_like(dpw_ref)
            dvec_ref[...] = jnp.zeros_like(dvec_ref)

        p_ref[pl.ds(0, PHALO), :] = jnp.zeros((PHALO, WC), f32)
        e_ref[pl.ds(SEQ, PHALO), :] = jnp.zeros((PHALO, WC), f32)

        def fill(c, carry):
            r0 = pl.multiple_of(c * CT, CT)
            p_ref[pl.ds(r0 + PHALO, CT), :] = pi_ref[pl.ds(r0, CT), :]
            return carry

        lax.fori_loop(0, nct, fill, 0)

        def step(c, carry):
            r0 = pl.multiple_of(c * CT, CT)
            sums = _pool_sums(p_ref[pl.ds(r0, CT + PHALO), :], False)
            cnt = _pool_counts(r0)
            pin = pi_ref[pl.ds(r0, CT), :]
            pooled = [(sums[g] / cnt[g] - pin[:, g * GD:(g + 1) * GD]).astype(bf16) for g in range(4)]
            m0 = jnp.concatenate(
                [jnp.dot(pooled[g], pw_ref[g], preferred_element_type=f32) for g in range(4)], axis=1) + vec_ref[0:1, :]
            scale = vec_ref[1:2, :]
            pg = pg_ref[pl.ds(r0, CT), :]
            spg = _sig(pg)
            dpl_ = dpl_ref[pl.ds(r0, CT), :]
            dmixed = dpl_ * (pg * spg)
            dz_ref[1, pl.ds(r0, CT), :] = (dpl_ * (m0 * scale) * _dsilu(pg, spg)).astype(bf16)
            dvec_ref[1:2, :] += _colsum(dmixed * m0)
            dm0 = dmixed * scale
            dvec_ref[0:1, :] += _colsum(dm0)
            dps, es = [], []
            for g in range(4):
                dm0g = dm0[:, g * GD:(g + 1) * GD].astype(bf16)
                dpw_ref[g] += lax.dot_general(pooled[g], dm0g, (((0,), (0,)), ((), ())), preferred_element_type=f32)
                dpg = lax.dot_general(dm0g, pw_ref[g], (((1,), (1,)), ((), ())), preferred_element_type=f32)
                dps.append(dpg)
                es.append(dpg / cnt[g])
            dp_ref[pl.ds(r0, CT), :] = jnp.concatenate(dps, axis=1)
            e_ref[pl.ds(r0, CT), :] = jnp.concatenate(es, axis=1)
            return carry

        lax.fori_loop(0, nct, step, 0)

        def back(c, carry):
            r0 = pl.multiple_of(c * CT, CT)
            fs = _pool_sums(e_ref[pl.ds(r0, CT + PHALO), :], True)
            dz_ref[0, pl.ds(r0, CT), :] = (jnp.concatenate(fs, axis=1) - dp_ref[pl.ds(r0, CT), :]).astype(bf16)
            return carry

        lax.fori_loop(0, nct, back, 0)

    def zs(col):
        return pl.BlockSpec((SEQ, WC), lambda b: (b, col // WC), pipeline_mode=pl.Buffered(1))

    return pl.pallas_call(
        body, grid=(BL,),
        in_specs=[zs(COL_PI), zs(COL_PG),
                  pl.BlockSpec((SEQ, WC), lambda b: (b, 0), pipeline_mode=pl.Buffered(1)), ANY,
                  pl.BlockSpec((4, GD, GD), lambda b: (0, 0, 0)), pl.BlockSpec((8, WC), lambda b: (0, 0))],
        out_specs=[pl.BlockSpec((2, SEQ, WC), lambda b: (DZ_POOL // 2, b, 0)),
                   pl.BlockSpec((4, GD, GD), lambda b: (0, 0, 0)), pl.BlockSpec((8, WC), lambda b: (0, 0))],
        out_shape=[jax.ShapeDtypeStruct(dz.shape, bf16), jax.ShapeDtypeStruct((4, GD, GD), f32),
                   jax.ShapeDtypeStruct((8, WC), f32)],
        scratch_shapes=[pltpu.VMEM((SEQ + PHALO, WC), f32), pltpu.VMEM((SEQ + PHALO, WC), f32),
                        pltpu.VMEM((SEQ, WC), f32)],
        input_output_aliases={3: 0}, name="pool_bwd",
        compiler_params=_params(("arbitrary",)))(z, z, dpl, dz, pw, pvec)


def _attn_prologue(q_ref, k_ref, v_ref, qs0, qs1, kp, vp, SEQ):
    head0 = lax.broadcasted_iota(jnp.int32, (1, 2 * HEAD_DIM), 1) < HEAD_DIM
    kp[pl.ds(0, KEY_PAD), :] = jnp.zeros((KEY_PAD, 2 * HEAD_DIM), bf16)
    vp[pl.ds(0, KEY_PAD), :] = jnp.zeros((KEY_PAD, 2 * HEAD_DIM), bf16)

    def fill(g, carry):
        r0 = pl.multiple_of(g * QG, QG)
        q = q_ref[pl.ds(r0, QG), :] * (HEAD_DIM ** -0.5)
        qs0[pl.ds(r0, QG), :] = jnp.where(head0, q, 0.0).astype(bf16)
        qs1[pl.ds(r0, QG), :] = jnp.where(head0, 0.0, q).astype(bf16)
        kp[pl.ds(r0 + KEY_PAD, QG), :] = k_ref[pl.ds(r0, QG), :].astype(bf16)
        vp[pl.ds(r0 + KEY_PAD, QG), :] = v_ref[pl.ds(r0, QG), :].astype(bf16)
        return carry

    lax.fori_loop(0, SEQ // QG, fill, 0)
    return head0


def _attn_weights(qh, kw, bias):
    s = lax.dot_general(qh, kw, (((1,), (1,)), ((), ())), preferred_element_type=f32) + bias
    e = jnp.exp(s - jnp.max(s, axis=-1, keepdims=True))
    return e, 1.0 / jnp.sum(e, axis=-1, keepdims=True)


def _attn_fwd(z, bm, BL, SEQ):
    T = BL * SEQ
    W2 = 2 * HEAD_DIM

    def body(q_ref, k_ref, v_ref, ag_ref, bm_ref, o_ref, qs0, qs1, kp, vp):
        head0 = _attn_prologue(q_ref, k_ref, v_ref, qs0, qs1, kp, vp, SEQ)

        def group(g, carry):
            r0 = pl.multiple_of(g * QG, QG)
            kw = kp[pl.ds(r0, KW), :]
            vw = vp[pl.ds(r0, KW), :]
            variant = jnp.minimum(g, BIAS_VARIANTS - 1)
            outs = []
            for hh, qs in enumerate((qs0, qs1)):
                e, inv = _attn_weights(qs[pl.ds(r0, QG), :], kw, bm_ref[variant, hh])
                outs.append(jnp.dot(e.astype(bf16), vw, preferred_element_type=f32) * inv)
            o = jnp.where(head0, outs[0], outs[1])
            ag = ag_ref[pl.ds(r0, QG), :]
            o_ref[pl.ds(r0, QG), :] = (o * (ag * _sig(ag))).astype(bf16)
            return carry

        lax.fori_loop(0, SEQ // QG, group, 0)

    def zs(col):
        return pl.BlockSpec((SEQ, W2), lambda b, hp: (b, col // W2 + hp))

    return pl.pallas_call(
        body, grid=(BL, WC // W2),
        in_specs=[zs(COL_Q), zs(COL_K), zs(COL_V), zs(COL_AG),
                  pl.BlockSpec((BIAS_VARIANTS, 2, QG, KW), lambda b, hp: (0, hp, 0, 0))],
        out_specs=pl.BlockSpec((SEQ, W2), lambda b, hp: (b, hp)),
        out_shape=jax.ShapeDtypeStruct((T, WC), bf16),
        scratch_shapes=[pltpu.VMEM((SEQ, W2), bf16), pltpu.VMEM((SEQ, W2), bf16),
                        pltpu.VMEM((SEQ + KEY_PAD, W2), bf16), pltpu.VMEM((SEQ + KEY_PAD, W2), bf16)],
        name="attn_fwd", compiler_params=_params(("parallel", "parallel")))(z, z, z, z, bm)


def _attn_bwd(z, dat, dz, bm, BL, SEQ):
    W2 = 2 * HEAD_DIM

    def body(q_ref, k_ref, v_ref, ag_ref, dat_ref, dzin_ref, bm_ref, dz_ref, dbm_ref, qs0, qs1, kp, vp, dka, dva):
        @pl.when(pl.program_id(1) == 0)
        def _():
            dbm_ref[...] = jnp.zeros_like(dbm_ref)

        head0 = _attn_prologue(q_ref, k_ref, v_ref, qs0, qs1, kp, vp, SEQ)
        dka[...] = jnp.zeros_like(dka)
        dva[...] = jnp.zeros_like(dva)

        def group(g, carry):
            r0 = pl.multiple_of(g * QG, QG)
            kw = kp[pl.ds(r0, KW), :]
            vw = vp[pl.ds(r0, KW), :]
            variant = jnp.minimum(g, BIAS_VARIANTS - 1)
            ag = ag_ref[pl.ds(r0, QG), :]
            do = dat_ref[pl.ds(r0, QG), :] * (ag * _sig(ag))
            outs, dqs = [], []
            for hh, qs in enumerate((qs0, qs1)):
                qh = qs[pl.ds(r0, QG), :]
                e, inv = _attn_weights(qh, kw, bm_ref[variant, hh])
                eb = e.astype(bf16)
                outs.append(jnp.dot(eb, vw, preferred_element_type=f32) * inv)
                doh = (jnp.where(head0, do, 0.0) if hh == 0 else jnp.where(head0, 0.0, do)) * inv
                doh = doh.astype(bf16)
                dp = lax.dot_general(doh, vw, (((1,), (1,)), ((), ())), preferred_element_type=f32)
                ds_ = e * (dp - jnp.sum(e * dp, axis=-1, keepdims=True) * inv)
                dbm_ref[hh] += ds_
                dsb = ds_.astype(bf16)
                dqs.append(jnp.dot(dsb, kw, preferred_element_type=f32))
                dka[pl.ds(r0, KW), :] += lax.dot_general(dsb, qh, (((0,), (0,)), ((), ())), preferred_element_type=f32)
                dva[pl.ds(r0, KW), :] += lax.dot_general(eb, doh, (((0,), (0,)), ((), ())), preferred_element_type=f32)
            o = jnp.where(head0, outs[0], outs[1])
            dq = jnp.where(head0, dqs[0], dqs[1]) * (HEAD_DIM ** -0.5)
            dz_ref[0, pl.ds(r0, QG), :] = dq.astype(bf16)
            dz_ref[3, pl.ds(r0, QG), :] = (dat_ref[pl.ds(r0, QG), :] * o * _dsilu(ag, _sig(ag))).astype(bf16)
            return carry

        lax.fori_loop(0, SEQ // QG, group, 0)

        def flush(g, carry):
            r0 = pl.multiple_of(g * QG, QG)
            dz_ref[1, pl.ds(r0, QG), :] = dka[pl.ds(r0 + KEY_PAD, QG), :].astype(bf16)
            dz_ref[2, pl.ds(r0, QG), :] = dva[pl.ds(r0 + KEY_PAD, QG), :].astype(bf16)
            return carry

        lax.fori_loop(0, SEQ // QG, flush, 0)

    def zs(col):
        return pl.BlockSpec((SEQ, W2), lambda hp, b: (b, col // W2 + hp))

    return pl.pallas_call(
        body, grid=(WC // W2, BL),
        in_specs=[zs(COL_Q), zs(COL_K), zs(COL_V), zs(COL_AG), pl.BlockSpec((SEQ, W2), lambda hp, b: (b, hp)), ANY,
                  pl.BlockSpec((BIAS_VARIANTS, 2, QG, KW), lambda hp, b: (0, hp, 0, 0))],
        out_specs=[pl.BlockSpec((4, SEQ, W2), lambda hp, b: (DZ_ATTN // 4, b, hp)),
                   pl.BlockSpec((2, QG, KW), lambda hp, b: (hp, 0, 0))],
        out_shape=[jax.ShapeDtypeStruct(dz.shape, bf16), jax.ShapeDtypeStruct((8, QG, KW), f32)],
        scratch_shapes=[pltpu.VMEM((SEQ, W2), bf16), pltpu.VMEM((SEQ, W2), bf16),
                        pltpu.VMEM((SEQ + KEY_PAD, W2), bf16), pltpu.VMEM((SEQ + KEY_PAD, W2), bf16),
                        pltpu.VMEM((SEQ + KEY_PAD, W2), f32), pltpu.VMEM((SEQ + KEY_PAD, W2), f32)],
        input_output_aliases={5: 0}, name="attn_bwd",
        compiler_params=_params(("parallel", "arbitrary")))(z, z, z, z, dat, dz, bm)


BIAS_TOP = KEY_PAD + MAX_REL + QG - 1


def _bias_matrix(table):
    n = 2 * MAX_REL
    wd = QG + KW
    e = jnp.concatenate([jnp.broadcast_to(table[:, n:], (8, BIAS_TOP - n + 1)), table[:, n - 1:BIAS_TOP - wd + 1:-1],
                         jnp.zeros((8, 1), f32)], axis=1)
    flat = jnp.broadcast_to(e[:, None, :], (8, QG, wd)).reshape(8, QG * wd)
    skew = flat[:, :QG * (wd - 1)].reshape(8, QG, wd - 1)
    vals = skew[:, :, QG - 1:QG - 1 + KW]
    r = np.arange(QG)[:, None] // CHUNK
    j = np.arange(KW)[None, :]
    band = (j // CHUNK >= r) & (j // CHUNK <= r + LEFT_CHUNKS)
    keep = np.stack([band & (j >= KEY_PAD - v * QG) for v in range(BIAS_VARIANTS)])
    return jnp.where(jnp.asarray(keep)[:, None], vals[None], NEG_INF)


def _bias_fold(dbm):
    wd = QG + KW
    placed = jnp.pad(dbm, ((0, 0), (0, 0), (QG - 1, 0))).reshape(8, QG * (wd - 1))
    return jnp.pad(placed, ((0, 0), (0, QG))).reshape(8, QG, wd)


def _bias_colsum(folded):
    width = folded.shape[2]

    def body(x_ref, o_ref):
        for h in range(8):
            o_ref[pl.ds(h, 1), :] = _colsum(x_ref[h])

    return pl.pallas_call(body, out_shape=jax.ShapeDtypeStruct((8, width), f32), name="bias_colsum",
                          compiler_params=_params())(folded)


def _bias_table_grad(colsum):
    n = 2 * MAX_REL
    wd = QG + KW
    clipped = jnp.sum(colsum[:, :BIAS_TOP - n + 1], axis=1, keepdims=True)
    return jnp.concatenate([jnp.zeros((8, BIAS_TOP - wd + 2), f32), colsum[:, wd - 2:BIAS_TOP - n:-1], clipped], axis=1)


def _merge_fwd(z, ys):
    T = z.shape[0]
    tm = 512

    def body(g0, g1, g2, y0, y1, y2, o_ref):
        acc = _sig(g0[...]) * y0[...] + _sig(g1[...]) * y1[...] + _sig(g2[...]) * y2[...]
        o_ref[...] = acc.astype(bf16)

    def gs(br):
        return pl.BlockSpec((tm, WC), lambda i, j: (i, (COL_GM + br * D) // WC + j))

    ysp = pl.BlockSpec((tm, WC), lambda i, j: (i, j))
    return pl.pallas_call(
        body, grid=(T // tm, D // WC), in_specs=[gs(0), gs(1), gs(2), ysp, ysp, ysp], out_specs=ysp,
        out_shape=jax.ShapeDtypeStruct((T, D), bf16), name="merge_fwd",
        compiler_params=_params(("parallel", "parallel")))(z, z, z, *ys)


def _merge_bwd(z, dmerged, ys):
    T = z.shape[0]
    tm = 256

    def body(*refs):
        g = refs[0:6]
        dm_ref = refs[6]
        y = refs[7:10]
        dy = refs[10:13]
        dz_ref = refs[13]
        for br in range(3):
            for jh in range(2):
                cols = slice(jh * WC, (jh + 1) * WC)
                s = _sig(g[2 * br + jh][...])
                dm = dm_ref[:, cols]
                dy[br][:, cols] = (dm * s).astype(bf16)
                dz_ref[2 * br + jh] = (dm * y[br][:, cols] * s * (1.0 - s)).astype(bf16)

    def gs(blk):
        return pl.BlockSpec((tm, WC), lambda i: (i, COL_GM // WC + blk))

    row = pl.BlockSpec((tm, D), lambda i: (i, 0))
    return pl.pallas_call(
        body, grid=(T // tm,), in_specs=[gs(b) for b in range(6)] + [row] * 4,
        out_specs=[row, row, row, pl.BlockSpec((6, tm, WC), lambda i: (DZ_GM // 6, i, 0))],
        out_shape=[jax.ShapeDtypeStruct((T, D), bf16)] * 3 + [jax.ShapeDtypeStruct((DZ_BLOCKS, T, WC), bf16)],
        name="merge_bwd", compiler_params=_params(("parallel",)))(*([z] * 6), dmerged, *ys)


def _adamw(name, g, w, m, v):
    R, C = w.shape
    tr = R
    for cand in (512, 256, 248, 128, 64, 32, 16, 8):
        if R % cand == 0 and cand * C * 4 <= 2 * 1024 * 1024:
            tr = cand
            break
    c1 = 1.0 - ADAM_B1
    c2 = 1.0 - ADAM_B2
    bc1 = 1.0 - ADAM_B1 ** ADAM_STEP
    bc2 = 1.0 - ADAM_B2 ** ADAM_STEP

    def body(g_ref, w_ref, m_ref, v_ref, d_ref, nm_ref, nv_ref):
        g_ = g_ref[...]
        nm = ADAM_B1 * m_ref[...] + c1 * g_
        nv = ADAM_B2 * v_ref[...] + c2 * (g_ * g_)
        nm_ref[...] = nm
        nv_ref[...] = nv
        d_ref[...] = -ADAM_LR * ((nm / bc1) / (jnp.sqrt(nv / bc2) + ADAM_EPS) + ADAM_WD * w_ref[...])

    spec = pl.BlockSpec((tr, C), lambda i: (i, 0))
    return pl.pallas_call(
        body, grid=(R // tr,), in_specs=[spec] * 4, out_specs=[spec] * 3,
        out_shape=[jax.ShapeDtypeStruct((R, C), f32)] * 3, name=name,
        compiler_params=_params(("parallel",)))(g, w, m, v)


def _sum_slots(name, parts):
    _, R, C = parts.shape
    tr = R
    for cand in (256, 128, 64, 32, 16, 8):
        if R % cand == 0 and cand * C * 4 * N_DEV <= 8 * 1024 * 1024:
            tr = cand
            break

    def body(p_ref, o_ref):
        acc = p_ref[0].astype(f32)
        for s in range(1, N_DEV):
            acc = acc + p_ref[s].astype(f32)
        o_ref[...] = acc

    return pl.pallas_call(
        body, grid=(R // tr,), in_specs=[pl.BlockSpec((N_DEV, tr, C), lambda i: (0, i, 0))],
        out_specs=pl.BlockSpec((tr, C), lambda i: (i, 0)), out_shape=jax.ShapeDtypeStruct((R, C), f32),
        name=name, compiler_params=_params(("parallel",)))(parts)


def _row_tile(rows, row_bytes, budget):
    for cand in (512, 256, 128, 64, 32, 16):
        if rows % cand == 0 and cand * row_bytes <= budget:
            return cand
    return rows


def _pair_sum(core, g, theirs):
    R2, C4 = theirs.shape
    tr = _row_tile(R2, C4 * 2, 2 * 1024 * 1024)
    nb = R2 // tr

    def body(core_ref, g_ref, t_ref, o_ref):
        o_ref[...] = (g_ref[...].astype(f32) + t_ref[...].astype(f32)).astype(bf16)

    return pl.pallas_call(
        body,
        grid_spec=pltpu.PrefetchScalarGridSpec(
            num_scalar_prefetch=1, grid=(nb,),
            in_specs=[pl.BlockSpec((tr, C4), lambda i, core_ref: (core_ref[0] * nb + i, 0)),
                      pl.BlockSpec((tr, C4), lambda i, core_ref: (i, 0))],
            out_specs=pl.BlockSpec((tr, C4), lambda i, core_ref: (i, 0))),
        out_shape=jax.ShapeDtypeStruct((R2, C4), bf16), name="pair_sum",
        compiler_params=_params(("parallel",)))(core, g, theirs)


def _chip_sum(chip, mine, others):
    _, R2, C = others.shape
    tr = _row_tile(R2, C * 4, 1024 * 1024)

    def body(chip_ref, m_ref, o_ref, out_ref):
        acc = m_ref[...].astype(f32)
        for s in range(N_CHIPS - 1):
            acc = acc + o_ref[s].astype(f32)
        out_ref[...] = acc

    return pl.pallas_call(
        body,
        grid_spec=pltpu.PrefetchScalarGridSpec(
            num_scalar_prefetch=1, grid=(R2 // tr,),
            in_specs=[pl.BlockSpec((tr, C), lambda i, chip_ref: (i, chip_ref[0])),
                      pl.BlockSpec((N_CHIPS - 1, tr, C), lambda i, chip_ref: (0, i, 0))],
            out_specs=pl.BlockSpec((tr, C), lambda i, chip_ref: (i, 0))),
        out_shape=jax.ShapeDtypeStruct((R2, C), f32), name="chip_sum",
        compiler_params=_params(("parallel",)))(chip, mine, others)


def _place():
    x, y, c = lax.axis_index("x"), lax.axis_index("y"), lax.axis_index("c")
    return x, y, c


def _flip(v, bit):
    return 1 - v if bit else v


CHIP_FLIPS = ((1, 0), (0, 1), (1, 1))


class _Sems:
    def __init__(self, send, recv):
        self.send, self.recv = send, recv
        self.pairs = 0

    def pair(self):
        k = self.pairs
        self.pairs += 1
        return self.send.at[k], self.recv.at[k]


def _remote(src, dst, lands, sems, to):
    s, r = sems.pair()
    copy = pltpu.make_async_remote_copy(src_ref=src, dst_ref=dst, send_sem=s, recv_sem=r, device_id=to, device_id_type=MESH)
    wait = pltpu.make_async_remote_copy(src_ref=lands, dst_ref=lands, send_sem=s, recv_sem=r, device_id=to, device_id_type=MESH)
    return copy, wait


def _exchange(name, build, srcs, lands, n_remote):
    n_s, n_l = len(srcs), len(lands)

    def body(*refs):
        send, recv = refs[n_s + 2 * n_l:]
        remotes, recvs = build(refs[:n_s], refs[n_s + n_l:n_s + 2 * n_l], _Sems(send, recv))
        for cp in remotes:
            cp.start()
        for rv in recvs:
            rv.wait_recv()
        for cp in remotes:
            cp.wait_send()

    return pl.pallas_call(
        body, in_specs=[ANY] * (n_s + n_l), out_specs=[ANY] * n_l,
        out_shape=[jax.ShapeDtypeStruct(t.shape, t.dtype) for t in lands],
        scratch_shapes=[pltpu.SemaphoreType.DMA((n_remote,)), pltpu.SemaphoreType.DMA((n_remote,))],
        input_output_aliases={n_s + i: i for i in range(n_l)}, name=name)(*srcs, *lands)


HBM = pl.BlockSpec(memory_space=pltpu.HBM)
SEMS = pl.BlockSpec(memory_space=pltpu.SEMAPHORE)
DATAFLOW = pltpu.SideEffectType.DATAFLOW_SIDE_EFFECTING


def _start(name, build, srcs, lands, n_remote, after):
    n_s, n_l = len(srcs), len(lands)

    def body(*refs):
        send, recv = refs[n_s + n_l + 1], refs[n_s + n_l + 2]
        remotes, _ = build(refs[:n_s], refs[n_s:n_s + n_l], _Sems(send, recv))
        for cp in remotes:
            cp.start()
        refs[-1][...] = jnp.zeros((8, 128), f32)

    arrays = [pltpu.with_memory_space_constraint(a, pltpu.HBM) for a in (*srcs, *lands)]
    out = pl.pallas_call(
        body, name=name, in_specs=[HBM] * (n_s + n_l) + [ANY],
        out_specs=(SEMS, SEMS, *[HBM] * (n_s + n_l), pl.BlockSpec(memory_space=pltpu.VMEM)),
        out_shape=(pltpu.SemaphoreType.DMA((n_remote,)), pltpu.SemaphoreType.DMA((n_remote,)),
                   *[pltpu.HBM(a.shape, a.dtype) for a in arrays], jax.ShapeDtypeStruct((8, 128), f32)),
        input_output_aliases={i: 2 + i for i in range(n_s + n_l)},
        compiler_params=pltpu.CompilerParams(has_side_effects=DATAFLOW))(*arrays, after)
    return dict(name=name, build=build, sems=out[:2], srcs=out[2:2 + n_s], lands=out[2 + n_s:2 + n_s + n_l], token=out[-1])


def _wait(started, after):
    srcs, lands, build = started["srcs"], started["lands"], started["build"]
    n_s, n_l = len(srcs), len(lands)

    def body(*refs):
        send, recv = refs[n_s + n_l], refs[n_s + n_l + 1]
        remotes, recvs = build(refs[:n_s], refs[n_s:n_s + n_l], _Sems(send, recv))
        for rv in recvs:
            rv.wait_recv()
        for cp in remotes:
            cp.wait_send()

    out = pl.pallas_call(
        body, name=started["name"] + "_wait", in_specs=[HBM] * (n_s + n_l) + [SEMS, SEMS, ANY],
        out_specs=[HBM] * (n_s + n_l), out_shape=[pltpu.HBM(a.shape, a.dtype) for a in (*srcs, *lands)],
        input_output_aliases={i: i for i in range(n_s + n_l)},
        compiler_params=pltpu.CompilerParams(has_side_effects=DATAFLOW))(*srcs, *lands, *started["sems"], after)
    return out[:n_s], out[n_s:]


def _gather_plans(n_split, n_all):
    def over_ici(src, land, sems):
        x, y, c = _place()
        chip = 2 * x + y
        remotes, recvs = [], []
        for a in range(n_all):
            for fx, fy in CHIP_FLIPS:
                px, py = _flip(x, fx), _flip(y, fy)
                if a < n_split:
                    r2 = src[a].shape[0] // 2
                    rows = pl.ds(c * r2, r2)
                    cp, rv = _remote(src[a].at[rows], land[a].at[chip, rows], land[a].at[2 * px + py, rows], sems, (px, py, c))
                else:
                    cp, rv = _remote(src[a], land[a].at[chip], land[a].at[2 * px + py], sems, (px, py, c))
                remotes.append(cp)
                recvs.append(rv)
        return remotes, recvs

    def over_d2d(src, land, sems):
        x, y, c = _place()
        remotes, recvs = [], []
        for a in range(n_split):
            r2 = land[a].shape[1] // 2
            for fx, fy in CHIP_FLIPS:
                owner = 2 * _flip(x, fx) + _flip(y, fy)
                mine = land[a].at[owner, pl.ds(c * r2, r2)]
                cp, rv = _remote(mine, mine, land[a].at[owner, pl.ds((1 - c) * r2, r2)], sems, (x, y, 1 - c))
                remotes.append(cp)
                recvs.append(rv)
        return remotes, recvs

    return over_ici, over_d2d


def _gather_begin(shards, n_split, after):
    over_ici, _ = _gather_plans(n_split, len(shards))
    lands = [lax.empty((N_CHIPS,) + s.shape, s.dtype) for s in shards]
    return _start("gather_ici", over_ici, shards, lands, 3 * len(shards), after)


def _gather_end(started, shards, n_split, after):
    _, over_d2d = _gather_plans(n_split, len(shards))
    lands = _exchange("gather_d2d", over_d2d, [], _wait(started, after)[1], 3 * n_split)
    chip = 2 * lax.axis_index("x") + lax.axis_index("y")
    return [lax.dynamic_update_slice_in_dim(g, s[None], chip, axis=0) for g, s in zip(lands, shards)]


def _reduce_plans(n):
    def to_sibling(src, land, sems):
        x, y, c = _place()
        remotes, recvs = [], []
        for a in range(n):
            r2 = src[a].shape[0] // 2
            cp, rv = _remote(src[a].at[pl.ds((1 - c) * r2, r2), :], land[a], land[a], sems, (x, y, 1 - c))
            remotes.append(cp)
            recvs.append(rv)
        return remotes, recvs

    def across_chips(src, land, sems):
        x, y, c = _place()
        remotes, recvs = [], []
        for a in range(n):
            cw = src[a].shape[1] // N_CHIPS
            for k, (fx, fy) in enumerate(CHIP_FLIPS):
                px, py = _flip(x, fx), _flip(y, fy)
                cp, rv = _remote(src[a].at[:, pl.ds((2 * px + py) * cw, cw)], land[a].at[k], land[a].at[k], sems, (px, py, c))
                remotes.append(cp)
                recvs.append(rv)
        return remotes, recvs

    def share(src, land, sems):
        x, y, c = _place()
        remotes, recvs = [], []
        for a in range(n):
            cp, rv = _remote(src[a], land[a], land[a], sems, (x, y, 1 - c))
            remotes.append(cp)
            recvs.append(rv)
        return remotes, recvs

    return to_sibling, across_chips, share


def _reduce_begin(grads):
    n = len(grads)
    to_sibling, across_chips, _ = _reduce_plans(n)
    core = lax.axis_index("c").reshape(1).astype(jnp.int32)
    theirs = _exchange("reduce_pair", to_sibling, grads,
                       [lax.empty((g.shape[0] // 2, g.shape[1]), bf16) for g in grads], n)
    pair = [_pair_sum(core, g, t) for g, t in zip(grads, theirs)]
    lands = [lax.empty((N_CHIPS - 1, g.shape[0] // 2, g.shape[1] // N_CHIPS), bf16) for g in grads]
    return _start("reduce_chips", across_chips, pair, lands, 3 * n, pair[0])


def _reduce_end(started, after):
    x, y, c = _place()
    chip = (2 * x + y).reshape(1).astype(jnp.int32)
    pair, others = _wait(started, after)
    _, _, share = _reduce_plans(len(pair))
    mine = [_chip_sum(chip, p, o) for p, o in zip(pair, others)]
    sibs = _exchange("reduce_share", share, mine, [lax.empty(h.shape, f32) for h in mine], len(mine))
    return [jnp.where(c == 0, jnp.concatenate([h, s], axis=0), jnp.concatenate([s, h], axis=0))
            for h, s in zip(mine, sibs)]


def _to_all(src, land, sems):
    x, y, c = _place()
    me = 4 * x + 2 * y + c
    remotes, recvs = [], []
    for k in range(1, N_DEV):
        px, py, pc = _flip(x, (k >> 2) & 1), _flip(y, (k >> 1) & 1), _flip(c, k & 1)
        cp, rv = _remote(src[0], land[0].at[me], land[0].at[4 * px + 2 * py + pc], sems, (px, py, pc))
        remotes.append(cp)
        recvs.append(rv)
    return remotes, recvs


def _gather_small_begin(packed):
    return _start("gather_small", _to_all, [packed], [lax.empty((N_DEV,) + packed.shape, f32)], N_DEV - 1, packed)


def _gather_small_end(started, after):
    (packed,), (others,) = _wait(started, after)
    x, y, c = _place()
    return lax.dynamic_update_slice_in_dim(others, packed[None], 4 * x + 2 * y + c, axis=0)


def _rows8(v):
    return jnp.pad(v[None, :], ((0, 7), (0, 0)))


def _vec_rows(vs):
    return jnp.pad(jnp.stack(vs), ((0, 8 - len(vs)), (0, 0)))


SMALL_ROWS = 224


def _pack_small(conv_vec, conv_dw, pool_vec, pool_w, pre_g, post_g, rel):
    return jnp.concatenate([
        conv_vec, conv_dw, pool_vec, pool_w.reshape(GD, WC),
        _rows8(pre_g).reshape(16, WC), _rows8(post_g).reshape(16, WC),
        jnp.pad(rel, ((0, 0), (0, D - rel.shape[1]))).reshape(16, WC)], axis=0)


def _unpack_small(p):
    conv_vec, pool_vec = p[0:8], p[40:48]
    return dict(
        conv_dw_b=conv_vec[0], conv_ln_g=conv_vec[1], conv_ln_b=conv_vec[2], conv_dw=p[8:8 + CONV_K],
        pool_b=pool_vec[0].reshape(4, GD), pool_scale=pool_vec[1], pool_w=p[48:176].reshape(4, GD, GD),
        pre_norm_g=p[176:192].reshape(8, D)[0], post_norm_g=p[192:208].reshape(8, D)[0],
        rel_bias=p[208:224].reshape(8, D)[:, :2 * MAX_REL + 1])


def _layer_fwd(x2, h, ht, lw, BL, SEQ, after=None):
    T = BL * SEQ
    z = _mm("mm_in", h, lw["w_in"], "nn", T, NCOL, D, 1024, 1536, D, f32, after=after)
    cv, u1 = _conv_fwd(z, lw["dw32"], lw["cvec"], BL, SEQ)
    at = _attn_fwd(z, lw["bm"], BL, SEQ)
    pv = _pool_fwd(z, lw["pw"], lw["pvec"], BL, SEQ)
    ys = [_mm("mm_branch_out", act, lw[w], "nn", T, D, WC, 1024, D, WC, f32)
          for act, w in ((cv, "w_conv_out"), (at, "w_attn_out"), (pv, "w_pool_out"))]
    merged = _merge_fwd(z, ys)
    y = _mm("mm_out", merged, lw["w_out"], "nn", T, D, D, 1024, D, D, f32)
    out = _post_fwd(y, x2, lw["post_g"])
    return out, dict(x=x2, ht=ht, z=z, u1=u1, acts=(cv, at, pv), ys=ys, merged=merged, y=y)


def _layer_bwd(dout, sv, lw, BL, SEQ, meanwhile=None):
    T = BL * SEQ
    dy, dpost = _post_bwd(dout, sv["y"], lw["post_g"])
    dmerged = _mm("mm_dmerged", dy, lw["w_out"], "nt", T, D, D, 1024, D, D, f32)
    dw_out_t = _mm("mm_dw_out", dy, sv["merged"], "tn", D, D, T, D, D, 1024, bf16)
    dys_and_dz = _merge_bwd(sv["z"], dmerged, sv["ys"])
    dys, dz = dys_and_dz[:3], dys_and_dz[3]
    dacts = [_mm("mm_dact", dyb, lw[w], "nt", T, WC, D, 1024, WC, D, f32)
             for dyb, w in zip(dys, ("w_conv_out", "w_attn_out", "w_pool_out"))]
    dws = [_mm("mm_dw_branch", act, dyb, "tn", WC, D, T, WC, D, 1024, bf16) for act, dyb in zip(sv["acts"], dys)]
    if meanwhile is not None:
        meanwhile(dws[2])
    dz, ddw, dcvec = _conv_bwd(sv["z"], sv["u1"], dacts[0], dz, lw["dw32"], lw["cvec"], BL, SEQ)
    dz, dbm = _attn_bwd(sv["z"], dacts[1], dz, lw["bm"], BL, SEQ)
    dz, dpw, dpvec = _pool_bwd(sv["z"], dacts[2], dz, lw["pw"], lw["pvec"], BL, SEQ)
    dw_in = _mm_dw_in(sv["ht"], dz)
    reduction = _reduce_begin([dw_in, dws[0], dws[1], dws[2], dw_out_t])
    dh = _mm_dh(dz, lw["w_in_t"], reduction["token"])
    dx, dpre = _pre_bwd(dh, sv["x"], lw["pre_g"], dout)
    drel = _bias_table_grad(_bias_colsum(_bias_fold(dbm)))
    small = _pack_small(dcvec, ddw, dpvec, dpw, dpre[0], dpost[0], drel)
    return dx, reduction, small


BIG = ("w_in", "w_conv_out", "w_attn_out", "w_pool_out", "w_out")


def _layer_shards(w, l):
    return [w[k][l].astype(bf16) for k in BIG] + [w["conv_dw"][l]]


def _layer_weights(gathered, w, l):
    def cols(g):
        return jnp.transpose(g, (1, 0, 2)).reshape(g.shape[1], N_CHIPS * g.shape[2])

    lw = {k: cols(g) for k, g in zip(BIG[:4], gathered[:4])}
    lw["w_out"] = gathered[4].reshape(D, D)
    lw["w_in_t"] = lw["w_in"].T
    lw["pre_g"] = w["pre_norm_g"][l][None]
    lw["post_g"] = w["post_norm_g"][l][None]
    lw["dw32"] = jnp.pad(cols(gathered[5]), ((0, 32 - CONV_K), (0, 0)))
    lw["cvec"] = _vec_rows([w["conv_dw_b"][l], w["conv_ln_g"][l], w["conv_ln_b"][l]])
    lw["bm"] = _bias_matrix(w["rel_bias"][l])
    lw["pw"] = w["pool_w"][l].astype(bf16)
    lw["pvec"] = _vec_rows([w["pool_b"][l].reshape(WC), w["pool_scale"][l]])
    return lw
SMALL = ("pre_norm_g", "post_norm_g", "conv_dw_b", "conv_ln_g", "conv_ln_b", "rel_bias", "pool_w", "pool_b", "pool_scale")
ORDER = ("pre_norm_g", "post_norm_g", "w_in", "conv_dw", "conv_dw_b", "conv_ln_g", "conv_ln_b", "w_conv_out",
         "rel_bias", "w_attn_out", "pool_w", "pool_b", "pool_scale", "w_pool_out", "w_out")


def _pack_small_params(p):
    return jnp.concatenate([
        _pack_small(_vec_rows([p["conv_dw_b"][l], p["conv_ln_g"][l], p["conv_ln_b"][l]]), jnp.zeros((32, WC), f32),
                    _vec_rows([p["pool_b"][l].reshape(WC), p["pool_scale"][l]]), p["pool_w"][l],
                    p["pre_norm_g"][l], p["post_norm_g"][l], p["rel_bias"][l])
        for l in range(DEPTH)], axis=0)


def _unpack_small_params(packed):
    layers = [_unpack_small(packed[l * SMALL_ROWS:(l + 1) * SMALL_ROWS]) for l in range(DEPTH)]
    return {k: jnp.stack([layers[l][k] for l in range(DEPTH)]) for k in layers[0]}


def kernel(x, pre_norm_g, post_norm_g, w_in, conv_dw, conv_dw_b, conv_ln_g, conv_ln_b, w_conv_out, rel_bias, w_attn_out, pool_w, pool_b, pool_scale, w_pool_out, w_out, loss_target, m_pre_norm_g, m_post_norm_g, m_w_in, m_conv_dw, m_conv_dw_b, m_conv_ln_g, m_conv_ln_b, m_w_conv_out, m_rel_bias, m_w_attn_out, m_pool_w, m_pool_b, m_pool_scale, m_w_pool_out, m_w_out, v_pre_norm_g, v_post_norm_g, v_w_in, v_conv_dw, v_conv_dw_b, v_conv_ln_g, v_conv_ln_b, v_w_conv_out, v_rel_bias, v_w_attn_out, v_pool_w, v_pool_b, v_pool_scale, v_w_pool_out, v_w_out):
    BL, SEQ, _ = x.shape
    T = BL * SEQ
    w = dict(pre_norm_g=pre_norm_g, post_norm_g=post_norm_g, w_in=w_in, conv_dw=conv_dw, conv_dw_b=conv_dw_b,
             conv_ln_g=conv_ln_g, conv_ln_b=conv_ln_b, w_conv_out=w_conv_out, rel_bias=rel_bias, w_attn_out=w_attn_out,
             pool_w=pool_w, pool_b=pool_b, pool_scale=pool_scale, w_pool_out=w_pool_out, w_out=w_out)
    m = dict(pre_norm_g=m_pre_norm_g, post_norm_g=m_post_norm_g, w_in=m_w_in, conv_dw=m_conv_dw, conv_dw_b=m_conv_dw_b,
             conv_ln_g=m_conv_ln_g, conv_ln_b=m_conv_ln_b, w_conv_out=m_w_conv_out, rel_bias=m_rel_bias,
             w_attn_out=m_w_attn_out, pool_w=m_pool_w, pool_b=m_pool_b, pool_scale=m_pool_scale,
             w_pool_out=m_w_pool_out, w_out=m_w_out)
    v = dict(pre_norm_g=v_pre_norm_g, post_norm_g=v_post_norm_g, w_in=v_w_in, conv_dw=v_conv_dw, conv_dw_b=v_conv_dw_b,
             conv_ln_g=v_conv_ln_g, conv_ln_b=v_conv_ln_b, w_conv_out=v_w_conv_out, rel_bias=v_rel_bias,
             w_attn_out=v_w_attn_out, pool_w=v_pool_w, pool_b=v_pool_b, pool_scale=v_pool_scale,
             w_pool_out=v_w_pool_out, w_out=v_w_out)

    n_split = len(BIG)
    shards = [_layer_shards(w, l) for l in range(DEPTH)]
    x2 = x.reshape(T, D)
    h0, ht0 = _rms_pre(x2, pre_norm_g[0][None])
    gather0 = _gather_begin(shards[0], n_split, x2)
    lw0 = _layer_weights(_gather_end(gather0, shards[0], n_split, ht0), w, 0)
    gather1 = _gather_begin(shards[1], n_split, lw0["w_in"])
    out0, saved0 = _layer_fwd(x2, h0, ht0, lw0, BL, SEQ, after=gather1["token"])
    lw1 = _layer_weights(_gather_end(gather1, shards[1], n_split, out0), w, 1)
    h1, ht1 = _rms_pre(out0, lw1["pre_g"])
    out1, saved1 = _layer_fwd(out0, h1, ht1, lw1, BL, SEQ)
    dout, sq = _loss_head(out1, loss_target.reshape(T, D))
    loss = lax.psum(0.5 * jnp.sum(sq) / float(D), ("x", "y", "c"))

    summed = [None] * DEPTH
    dx1, reduction1, small1 = _layer_bwd(dout, saved1, lw1, BL, SEQ)
    small_gather1 = _gather_small_begin(small1)

    def finish_layer1(after):
        summed[1] = _reduce_end(reduction1, after)

    grad_x, reduction0, small0 = _layer_bwd(dx1, saved0, lw0, BL, SEQ, meanwhile=finish_layer1)
    small_gather0 = _gather_small_begin(small0)
    summed[0] = _reduce_end(reduction0, grad_x)
    gsmall = jnp.concatenate([_sum_slots("sum_small", _gather_small_end(s, summed[0][0]))
                              for s in (small_gather0, small_gather1)], axis=0)

    grads, deltas, new_m, new_v = {}, {}, {}, {}
    for i, k in enumerate(BIG):
        g = jnp.stack([summed[l][i] for l in range(DEPTH)])
        if k == "w_out":
            g = jnp.transpose(g, (0, 2, 1))
        grads[k] = g
        shape = w[k].shape
        flat2 = lambda a: a.reshape(shape[0] * shape[1], shape[2])
        d_, nm_, nv_ = _adamw("adamw_big", flat2(g), flat2(w[k]), flat2(m[k]), flat2(v[k]))
        deltas[k], new_m[k], new_v[k] = d_.reshape(shape), nm_.reshape(shape), nv_.reshape(shape)

    d_, nm_, nv_ = _adamw("adamw_small", gsmall, _pack_small_params(w), _pack_small_params(m), _pack_small_params(v))
    gs, ds, ms, vs = (_unpack_small_params(a) for a in (gsmall, d_, nm_, nv_))
    for k in SMALL:
        grads[k], deltas[k], new_m[k], new_v[k] = gs[k], ds[k], ms[k], vs[k]
    chip = 2 * lax.axis_index("x") + lax.axis_index("y")
    g_dw = lax.dynamic_slice_in_dim(gs["conv_dw"], chip * GD, GD, axis=2)
    flat2 = lambda a: a.reshape(DEPTH * CONV_K, GD)
    d_, nm_, nv_ = _adamw("adamw_conv_dw", flat2(g_dw), flat2(conv_dw), flat2(m["conv_dw"]), flat2(v["conv_dw"]))
    grads["conv_dw"] = g_dw
    deltas["conv_dw"], new_m["conv_dw"], new_v["conv_dw"] = (a.reshape(conv_dw.shape) for a in (d_, nm_, nv_))

    return (loss, grad_x.reshape(x.shape), *[grads[k] for k in ORDER], *[deltas[k] for k in ORDER],
            *[new_m[k] for k in ORDER], *[new_v[k] for k in ORDER])
```

```python
import numpy as np
import jax
import jax.numpy as jnp
from jax import lax
from jax.experimental import pallas as pl
from jax.experimental.pallas import tpu as pltpu

f32 = jnp.float32
bf16 = jnp.bfloat16

D = 1024
DEPTH = 2
WC = 512
HEAD_DIM = 64
CHUNK = 64
LEFT_CHUNKS = 8
KEY_PAD = LEFT_CHUNKS * CHUNK
MAX_REL = 256
CONV_K = 31
POOL_WINDOWS = (2, 4, 8, 16)
GD = 128
NCOL = 7680
EPS = 1e-6
NEG_INF = -1e30
COL_A, COL_B, COL_CG, COL_Q, COL_K, COL_V, COL_AG, COL_PI, COL_PG, COL_GM = (
    0, 512, 1024, 1536, 2048, 2560, 3072, 3584, 4096, 4608)

ADAM_LR = 0.001
ADAM_B1 = 0.9
ADAM_B2 = 0.999
ADAM_EPS = 1e-08
ADAM_WD = 0.01
ADAM_STEP = 10

QG = 256
KW = KEY_PAD + QG
BIAS_VARIANTS = KEY_PAD // QG + 1
CT = 128
HALO = 32
PHALO = 16
N_CHIPS = 4
N_DEV = 8
VMEM_LIMIT = 56 * 1024 * 1024
MESH = pl.DeviceIdType.MESH
ANY = pl.BlockSpec(memory_space=pl.ANY)

DZ_BLOCKS = 18
DZ_CONV, DZ_ATTN, DZ_POOL, DZ_GM = 0, 4, 8, 12


def _dz_block(c):
    return c + (c >= 3).astype(jnp.int32) + 2 * (c >= 9).astype(jnp.int32)


def _params(sem=None):
    return pltpu.CompilerParams(dimension_semantics=sem, vmem_limit_bytes=VMEM_LIMIT)


def _sig(x):
    return 1.0 / (1.0 + jnp.exp(-x))


def _dsilu(x, s):
    return s * (1.0 + x * (1.0 - s))


def _colsum(x):
    return jnp.sum(x, axis=0, keepdims=True)


def _rms_pre(x2, g):
    T = x2.shape[0]
    tm = 512

    def body(x_ref, g_ref, h_ref, ht_ref):
        x = x_ref[...]
        r = lax.rsqrt(jnp.mean(x * x, axis=-1, keepdims=True) + EPS)
        h = (x * r) * g_ref[...]
        h_ref[...] = h.astype(bf16)
        ht_ref[...] = h.T.astype(bf16)

    row = pl.BlockSpec((tm, D), lambda i: (i, 0))
    vec = pl.BlockSpec((1, D), lambda i: (0, 0))
    return pl.pallas_call(
        body, grid=(T // tm,), in_specs=[row, vec], out_specs=[row, pl.BlockSpec((D, tm), lambda i: (0, i))],
        out_shape=[jax.ShapeDtypeStruct((T, D), bf16), jax.ShapeDtypeStruct((D, T), bf16)], name="rms_pre",
        compiler_params=_params(("parallel",)))(x2, g)


def _loss_head(out, tgt):
    T = out.shape[0]
    tm = 512

    def body(o_ref, t_ref, d_ref, l_ref):
        e = o_ref[...] - t_ref[...]
        d_ref[...] = e / float(D)

        @pl.when(pl.program_id(0) == 0)
        def _():
            l_ref[...] = jnp.zeros_like(l_ref)

        l_ref[...] += _colsum(e * e)

    row = pl.BlockSpec((tm, D), lambda i: (i, 0))
    vec = pl.BlockSpec((1, D), lambda i: (0, 0))
    return pl.pallas_call(
        body, grid=(T // tm,), in_specs=[row, row], out_specs=[row, vec],
        out_shape=[jax.ShapeDtypeStruct((T, D), f32), jax.ShapeDtypeStruct((1, D), f32)],
        name="loss_head", compiler_params=_params(("arbitrary",)))(out, tgt)


def _pre_bwd(dh, x2, g, dout):
    T = x2.shape[0]
    tm = 512

    def body(dh_ref, x_ref, g_ref, d_ref, dx_ref, dg_ref):
        x = x_ref[...]
        dh_ = dh_ref[...]
        r = lax.rsqrt(jnp.mean(x * x, axis=-1, keepdims=True) + EPS)
        xn = x * r
        dxn = dh_ * g_ref[...]
        dx_ref[...] = r * (dxn - xn * jnp.mean(dxn * xn, axis=-1, keepdims=True)) + d_ref[...]

        @pl.when(pl.program_id(0) == 0)
        def _():
            dg_ref[...] = jnp.zeros_like(dg_ref)

        dg_ref[...] += _colsum(dh_ * xn)

    row = pl.BlockSpec((tm, D), lambda i: (i, 0))
    vec = pl.BlockSpec((1, D), lambda i: (0, 0))
    return pl.pallas_call(
        body, grid=(T // tm,), in_specs=[row, row, vec, row], out_specs=[row, vec],
        out_shape=[jax.ShapeDtypeStruct((T, D), f32), jax.ShapeDtypeStruct((1, D), f32)],
        name="pre_bwd", compiler_params=_params(("arbitrary",)))(dh, x2, g, dout)


def _mm(name, a, b, mode, m, n, k, tm, tn, tk, out_dtype, after=None):
    nk = k // tk
    assert m % tm == 0 and n % tn == 0 and k % tk == 0
    if mode == "nn":
        a_spec = pl.BlockSpec((tm, tk), lambda i, j, kk: (i, kk))
        b_spec = pl.BlockSpec((tk, tn), lambda i, j, kk: (kk, j))
        dn = (((1,), (0,)), ((), ()))
    elif mode == "nt":
        a_spec = pl.BlockSpec((tm, tk), lambda i, j, kk: (i, kk))
        b_spec = pl.BlockSpec((tn, tk), lambda i, j, kk: (j, kk))
        dn = (((1,), (1,)), ((), ()))
    else:
        a_spec = pl.BlockSpec((tk, tm), lambda i, j, kk: (kk, i))
        b_spec = pl.BlockSpec((tk, tn), lambda i, j, kk: (kk, j))
        dn = (((0,), (0,)), ((), ()))
    extra = [] if after is None else [after]

    def body(a_ref, b_ref, *rest):
        o_ref, acc_ref = rest[len(extra):]
        p = lax.dot_general(a_ref[...].astype(bf16), b_ref[...].astype(bf16), dn, preferred_element_type=f32)
        if nk == 1:
            o_ref[...] = p.astype(o_ref.dtype)
        else:
            kk = pl.program_id(2)

            @pl.when(kk == 0)
            def _():
                acc_ref[...] = p

            @pl.when(kk > 0)
            def _():
                acc_ref[...] += p

            @pl.when(kk == nk - 1)
            def _():
                o_ref[...] = acc_ref[...].astype(o_ref.dtype)

    acc_shape = (tm, tn) if nk > 1 else (8, 128)
    return pl.pallas_call(
        body, grid=(m // tm, n // tn, nk), in_specs=[a_spec, b_spec] + [ANY] * len(extra),
        out_specs=pl.BlockSpec((tm, tn), lambda i, j, kk: (i, j)),
        out_shape=jax.ShapeDtypeStruct((m, n), out_dtype),
        scratch_shapes=[pltpu.VMEM(acc_shape, f32)], name=name,
        compiler_params=_params(("parallel", "parallel", "arbitrary")))(a, b, *extra)


DZ_SPANS = ((DZ_CONV, 3), (DZ_ATTN, 4), (DZ_POOL, 2), (DZ_GM, 6))


def _mm_dh(dz, w_in_t, after):
    T = dz.shape[1]
    tm = 512

    def body(conv_ref, attn_ref, pool_ref, gm_ref, w_ref, after_ref, o_ref):
        acc = None
        col = 0
        for ref, (_, blocks) in zip((conv_ref, attn_ref, pool_ref, gm_ref), DZ_SPANS):
            for b in range(blocks):
                p = jnp.dot(ref[b], w_ref[col * WC:(col + 1) * WC, :], preferred_element_type=f32)
                acc = p if acc is None else acc + p
                col += 1
        o_ref[...] = acc

    spans = [pl.BlockSpec((blocks, tm, WC), lambda i, first=first, blocks=blocks: (first // blocks, i, 0))
             for first, blocks in DZ_SPANS]
    return pl.pallas_call(
        body, grid=(T // tm,),
        in_specs=spans + [pl.BlockSpec((NCOL, D), lambda i: (0, 0), pipeline_mode=pl.Buffered(1)), ANY],
        out_specs=pl.BlockSpec((tm, D), lambda i: (i, 0)), out_shape=jax.ShapeDtypeStruct((T, D), f32),
        name="mm_dh", compiler_params=_params(("parallel",)))(dz, dz, dz, dz, w_in_t, after)


def _mm_dw_in(ht, dz):
    T = dz.shape[1]

    def body(ht_ref, dz_ref, o_ref):
        o_ref[...] = jnp.dot(ht_ref[...], dz_ref[...], preferred_element_type=f32).astype(bf16)

    return pl.pallas_call(
        body, grid=(NCOL // WC,),
        in_specs=[pl.BlockSpec((D, T), lambda j: (0, 0), pipeline_mode=pl.Buffered(1)),
                  pl.BlockSpec((None, T, WC), lambda j: (_dz_block(j), 0, 0))],
        out_specs=pl.BlockSpec((D, WC), lambda j: (0, j)), out_shape=jax.ShapeDtypeStruct((D, NCOL), bf16),
        name="mm_dw_in", compiler_params=_params(("parallel",)))(ht, dz)


def _conv_delays():
    return [(8 * a + b, a, b) for b in range(8) for a in range(4) if 8 * a + b < CONV_K]


def _conv_rolls(win):
    return [win if b == 0 else pltpu.roll(win, b, axis=0) for b in range(8)]


def _conv_taps(rolled, dw_ref):
    acc = None
    for d, a, b in _conv_delays():
        term = rolled[b][HALO - 8 * a:HALO - 8 * a + CT, :] * dw_ref[pl.ds(CONV_K - 1 - d, 1), :]
        acc = term if acc is None else acc + term
    return acc


def _conv_fwd(z, dw32, cvec, BL, SEQ):
    T = BL * SEQ
    nct = SEQ // CT

    def body(a_ref, b_ref, cg_ref, dw_ref, vec_ref, o_ref, u1_ref, p_ref):
        p_ref[pl.ds(0, HALO), :] = jnp.zeros((HALO, WC), f32)

        def glu(c, carry):
            r0 = pl.multiple_of(c * CT, CT)
            p_ref[pl.ds(r0 + HALO, CT), :] = a_ref[pl.ds(r0, CT), :] * _sig(b_ref[pl.ds(r0, CT), :])
            return carry

        lax.fori_loop(0, nct, glu, 0)

        def step(c, carry):
            r0 = pl.multiple_of(c * CT, CT)
            u1 = _conv_taps(_conv_rolls(p_ref[pl.ds(r0, CT + HALO), :]), dw_ref) + vec_ref[0:1, :]
            u1_ref[pl.ds(r0, CT), :] = u1
            xc = u1 - jnp.mean(u1, axis=-1, keepdims=True)
            rs = lax.rsqrt(jnp.mean(xc * xc, axis=-1, keepdims=True) + EPS)
            u2 = (xc * rs) * vec_ref[1:2, :] + vec_ref[2:3, :]
            cg = cg_ref[pl.ds(r0, CT), :]
            o_ref[pl.ds(r0, CT), :] = ((u2 * _sig(u2)) * (cg * _sig(cg))).astype(bf16)
            return carry

        lax.fori_loop(0, nct, step, 0)

    def zs(col):
        return pl.BlockSpec((SEQ, WC), lambda b: (b, col // WC))

    seq = pl.BlockSpec((SEQ, WC), lambda b: (b, 0))
    return pl.pallas_call(
        body, grid=(BL,),
        in_specs=[zs(COL_A), zs(COL_B), zs(COL_CG), pl.BlockSpec((32, WC), lambda b: (0, 0)),
                  pl.BlockSpec((8, WC), lambda b: (0, 0))],
        out_specs=[seq, seq],
        out_shape=[jax.ShapeDtypeStruct((T, WC), bf16), jax.ShapeDtypeStruct((T, WC), f32)],
        scratch_shapes=[pltpu.VMEM((SEQ + HALO, WC), f32)], name="conv_fwd",
        compiler_params=_params(("parallel",)))(z, z, z, dw32, cvec)


def _conv_bwd(z, u1, dcv, dz, dw32, cvec, BL, SEQ):
    nct = SEQ // CT

    def body(a_ref, b_ref, cg_ref, u1_ref, dcv_ref, dzin_ref, dw_ref, vec_ref, dz_ref, ddw_ref, dvec_ref,
             p_ref, q_ref, taps_ref):
        @pl.when(pl.program_id(0) == 0)
        def _():
            ddw_ref[...] = jnp.zeros_like(ddw_ref)
            dvec_ref[...] = jnp.zeros_like(dvec_ref)

        p_ref[pl.ds(0, HALO), :] = jnp.zeros((HALO, WC), f32)
        q_ref[pl.ds(SEQ, HALO), :] = jnp.zeros((HALO, WC), f32)

        def glu(c, carry):
            r0 = pl.multiple_of(c * CT, CT)
            p_ref[pl.ds(r0 + HALO, CT), :] = a_ref[pl.ds(r0, CT), :] * _sig(b_ref[pl.ds(r0, CT), :])
            return carry

        lax.fori_loop(0, nct, glu, 0)

        def step(c, carry):
            r0 = pl.multiple_of(c * CT, CT)
            rolled = _conv_rolls(p_ref[pl.ds(r0, CT + HALO), :])
            u1 = u1_ref[pl.ds(r0, CT), :]
            xc = u1 - jnp.mean(u1, axis=-1, keepdims=True)
            rs = lax.rsqrt(jnp.mean(xc * xc, axis=-1, keepdims=True) + EPS)
            nrm = xc * rs
            u2 = nrm * vec_ref[1:2, :] + vec_ref[2:3, :]
            s2 = _sig(u2)
            u3 = u2 * s2
            cg = cg_ref[pl.ds(r0, CT), :]
            scg = _sig(cg)
            dcv_ = dcv_ref[pl.ds(r0, CT), :]
            dz_ref[2, pl.ds(r0, CT), :] = (dcv_ * u3 * _dsilu(cg, scg)).astype(bf16)
            du2 = dcv_ * (cg * scg) * _dsilu(u2, s2)
            dvec_ref[1:2, :] += _colsum(du2 * nrm)
            dvec_ref[2:3, :] += _colsum(du2)
            dn = du2 * vec_ref[1:2, :]
            du1 = rs * (dn - jnp.mean(dn, axis=-1, keepdims=True)
                        - nrm * jnp.mean(dn * nrm, axis=-1, keepdims=True))
            dvec_ref[0:1, :] += _colsum(du1)
            q_ref[pl.ds(r0, CT), :] = du1
            for d, a, b in _conv_delays():
                prod = du1 * rolled[b][HALO - 8 * a:HALO - 8 * a + CT, :]
                taps_ref[CONV_K - 1 - d] += jnp.sum(prod.reshape(CT // 8, 8, WC), axis=0)
            return carry

        taps_ref[...] = jnp.zeros_like(taps_ref)
        lax.fori_loop(0, nct, step, 0)
        for row in range(CONV_K):
            ddw_ref[pl.ds(row, 1), :] += _colsum(taps_ref[row])

        def back(c, carry):
            r0 = pl.multiple_of(c * CT, CT)
            wq = q_ref[pl.ds(r0, CT + HALO), :]
            up = {}
            acc = None
            for d, a, b in _conv_delays():
                if b not in up:
                    up[b] = wq if b == 0 else pltpu.roll(wq, CT + HALO - b, axis=0)
                term = up[b][8 * a:8 * a + CT, :] * dw_ref[pl.ds(CONV_K - 1 - d, 1), :]
                acc = term if acc is None else acc + term
            a_ = a_ref[pl.ds(r0, CT), :]
            sb = _sig(b_ref[pl.ds(r0, CT), :])
            dz_ref[0, pl.ds(r0, CT), :] = (acc * sb).astype(bf16)
            dz_ref[1, pl.ds(r0, CT), :] = (acc * a_ * sb * (1.0 - sb)).astype(bf16)
            return carry

        lax.fori_loop(0, nct, back, 0)

    def zs(col):
        return pl.BlockSpec((SEQ, WC), lambda b: (b, col // WC), pipeline_mode=pl.Buffered(1))

    def const(r):
        return pl.BlockSpec((r, WC), lambda b: (0, 0))

    seq = pl.BlockSpec((SEQ, WC), lambda b: (b, 0), pipeline_mode=pl.Buffered(1))
    return pl.pallas_call(
        body, grid=(BL,),
        in_specs=[zs(COL_A), zs(COL_B), zs(COL_CG), seq, seq, ANY, const(32), const(8)],
        out_specs=[pl.BlockSpec((3, SEQ, WC), lambda b: (DZ_CONV // 3, b, 0)), const(32), const(8)],
        out_shape=[jax.ShapeDtypeStruct(dz.shape, bf16), jax.ShapeDtypeStruct((32, WC), f32),
                   jax.ShapeDtypeStruct((8, WC), f32)],
        scratch_shapes=[pltpu.VMEM((SEQ + HALO, WC), f32), pltpu.VMEM((SEQ + HALO, WC), f32),
                        pltpu.VMEM((CONV_K, 8, WC), f32)],
        input_output_aliases={5: 0}, name="conv_bwd",
        compiler_params=_params(("arbitrary",)))(z, z, z, u1, dcv, dz, dw32, cvec)


def _pool_counts(r0):
    t1 = r0 + 1 + lax.broadcasted_iota(jnp.int32, (CT, 1), 0)
    return [jnp.minimum(t1, w).astype(f32) for w in POOL_WINDOWS]


def _pool_sums(win, forward):
    n = CT + PHALO

    def sh(x, s):
        return pltpu.roll(x, (n - s) if forward else s, axis=0)

    s2 = win + sh(win, 1)
    s4 = s2[:, GD:] + sh(s2[:, GD:], 2)
    s8 = s4[:, GD:] + sh(s4[:, GD:], 4)
    s16 = s8[:, GD:] + sh(s8[:, GD:], 8)
    lo = 0 if forward else PHALO
    return [s[lo:lo + CT, :GD] for s in (s2, s4, s8, s16)]


def _pool_fwd(z, pw, pvec, BL, SEQ):
    T = BL * SEQ
    nct = SEQ // CT

    def body(pi_ref, pg_ref, pw_ref, vec_ref, o_ref, p_ref):
        p_ref[pl.ds(0, PHALO), :] = jnp.zeros((PHALO, WC), f32)

        def fill(c, carry):
            r0 = pl.multiple_of(c * CT, CT)
            p_ref[pl.ds(r0 + PHALO, CT), :] = pi_ref[pl.ds(r0, CT), :]
            return carry

        lax.fori_loop(0, nct, fill, 0)

        def step(c, carry):
            r0 = pl.multiple_of(c * CT, CT)
            sums = _pool_sums(p_ref[pl.ds(r0, CT + PHALO), :], False)
            cnt = _pool_counts(r0)
            pin = pi_ref[pl.ds(r0, CT), :]
            mixed = []
            for g in range(4):
                pooled = sums[g] / cnt[g] - pin[:, g * GD:(g + 1) * GD]
                mixed.append(jnp.dot(pooled.astype(bf16), pw_ref[g], preferred_element_type=f32))
            m0 = jnp.concatenate(mixed, axis=1) + vec_ref[0:1, :]
            pg = pg_ref[pl.ds(r0, CT), :]
            o_ref[pl.ds(r0, CT), :] = ((m0 * vec_ref[1:2, :]) * (pg * _sig(pg))).astype(bf16)
            return carry

        lax.fori_loop(0, nct, step, 0)

    def zs(col):
        return pl.BlockSpec((SEQ, WC), lambda b: (b, col // WC))

    return pl.pallas_call(
        body, grid=(BL,),
        in_specs=[zs(COL_PI), zs(COL_PG), pl.BlockSpec((4, GD, GD), lambda b: (0, 0, 0)),
                  pl.BlockSpec((8, WC), lambda b: (0, 0))],
        out_specs=pl.BlockSpec((SEQ, WC), lambda b: (b, 0)),
        out_shape=jax.ShapeDtypeStruct((T, WC), bf16),
        scratch_shapes=[pltpu.VMEM((SEQ + PHALO, WC), f32)], name="pool_fwd",
        compiler_params=_params(("parallel",)))(z, z, pw, pvec)


def _pool_bwd(z, dpl, dz, pw, pvec, BL, SEQ):
    nct = SEQ // CT

    def body(pi_ref, pg_ref, dpl_ref, dzin_ref, pw_ref, vec_ref, dz_ref, dpw_ref, dvec_ref, p_ref, e_ref, dp_ref):
        @pl.when(pl.program_id(0) == 0)
        def _():
            dpw_ref[...] = jnp.zeros_like(dpw_ref)
            dvec_ref[...] = jnp.zeros_like(dvec_ref)

        p_ref[pl.ds(0, PHALO), :] = jnp.zeros((PHALO, WC), f32)
        e_ref[pl.ds(SEQ, PHALO), :] = jnp.zeros((PHALO, WC), f32)

        def fill(c, carry):
            r0 = pl.multiple_of(c * CT, CT)
            p_ref[pl.ds(r0 + PHALO, CT), :] = pi_ref[pl.ds(r0, CT), :]
            return carry

        lax.fori_loop(0, nct, fill, 0)

        def step(c, carry):
            r0 = pl.multiple_of(c * CT, CT)
            sums = _pool_sums(p_ref[pl.ds(r0, CT + PHALO), :], False)
            cnt = _pool_counts(r0)
            pin = pi_ref[pl.ds(r0, CT), :]
            pooled = [(sums[g] / cnt[g] - pin[:, g * GD:(g + 1) * GD]).astype(bf16) for g in range(4)]
            m0 = jnp.concatenate(
                [jnp.dot(pooled[g], pw_ref[g], preferred_element_type=f32) for g in range(4)], axis=1) + vec_ref[0:1, :]
            scale = vec_ref[1:2, :]
            pg = pg_ref[pl.ds(r0, CT), :]
            spg = _sig(pg)
            dpl_ = dpl_ref[pl.ds(r0, CT), :]
            dmixed = dpl_ * (pg * spg)
            dz_ref[1, pl.ds(r0, CT), :] = (dpl_ * (m0 * scale) * _dsilu(pg, spg)).astype(bf16)
            dvec_ref[1:2, :] += _colsum(dmixed * m0)
            dm0 = dmixed * scale
            dvec_ref[0:1, :] += _colsum(dm0)
            dps, es = [], []
            for g in range(4):
                dm0g = dm0[:, g * GD:(g + 1) * GD].astype(bf16)
                dpw_ref[g] += lax.dot_general(pooled[g], dm0g, (((0,), (0,)), ((), ())), preferred_element_type=f32)
                dpg = lax.dot_general(dm0g, pw_ref[g], (((1,), (1,)), ((), ())), preferred_element_type=f32)
                dps.append(dpg)
                es.append(dpg / cnt[g])
            dp_ref[pl.ds(r0, CT), :] = jnp.concatenate(dps, axis=1)
            e_ref[pl.ds(r0, CT), :] = jnp.concatenate(es, axis=1)
            return carry

        lax.fori_loop(0, nct, step, 0)

        def back(c, carry):
            r0 = pl.multiple_of(c * CT, CT)
            fs = _pool_sums(e_ref[pl.ds(r0, CT + PHALO), :], True)
            dz_ref[0, pl.ds(r0, CT), :] = (jnp.concatenate(fs, axis=1) - dp_ref[pl.ds(r0, CT), :]).astype(bf16)
            return carry

        lax.fori_loop(0, nct, back, 0)

    def zs(col):
        return pl.BlockSpec((SEQ, WC), lambda b: (b, col // WC), pipeline_mode=pl.Buffered(1))

    return pl.pallas_call(
        body, grid=(BL,),
        in_specs=[zs(COL_PI), zs(COL_PG),
                  pl.BlockSpec((SEQ, WC), lambda b: (b, 0), pipeline_mode=pl.Buffered(1)), ANY,
                  pl.BlockSpec((4, GD, GD), lambda b: (0, 0, 0)), pl.BlockSpec((8, WC), lambda b: (0, 0))],
        out_specs=[pl.BlockSpec((2, SEQ, WC), lambda b: (DZ_POOL // 2, b, 0)),
                   pl.BlockSpec((4, GD, GD), lambda b: (0, 0, 0)), pl.BlockSpec((8, WC), lambda b: (0, 0))],
        out_shape=[jax.ShapeDtypeStruct(dz.shape, bf16), jax.ShapeDtypeStruct((4, GD, GD), f32),
                   jax.ShapeDtypeStruct((8, WC), f32)],
        scratch_shapes=[pltpu.VMEM((SEQ + PHALO, WC), f32), pltpu.VMEM((SEQ + PHALO, WC), f32),
                        pltpu.VMEM((SEQ, WC), f32)],
        input_output_aliases={3: 0}, name="pool_bwd",
        compiler_params=_params(("arbitrary",)))(z, z, dpl, dz, pw, pvec)


def _attn_prologue(q_ref, k_ref, v_ref, qs0, qs1, kp, vp, SEQ):
    head0 = lax.broadcasted_iota(jnp.int32, (1, 2 * HEAD_DIM), 1) < HEAD_DIM
    kp[pl.ds(0, KEY_PAD), :] = jnp.zeros((KEY_PAD, 2 * HEAD_DIM), bf16)
    vp[pl.ds(0, KEY_PAD), :] = jnp.zeros((KEY_PAD, 2 * HEAD_DIM), bf16)

    def fill(g, carry):
        r0 = pl.multiple_of(g * QG, QG)
        q = q_ref[pl.ds(r0, QG), :] * (HEAD_DIM ** -0.5)
        qs0[pl.ds(r0, QG), :] = jnp.where(head0, q, 0.0).astype(bf16)
        qs1[pl.ds(r0, QG), :] = jnp.where(head0, 0.0, q).astype(bf16)
        kp[pl.ds(r0 + KEY_PAD, QG), :] = k_ref[pl.ds(r0, QG), :].astype(bf16)
        vp[pl.ds(r0 + KEY_PAD, QG), :] = v_ref[pl.ds(r0, QG), :].astype(bf16)
        return carry

    lax.fori_loop(0, SEQ // QG, fill, 0)
    return head0


def _attn_weights(qh, kw, bias):
    s = lax.dot_general(qh, kw, (((1,), (1,)), ((), ())), preferred_element_type=f32) + bias
    e = jnp.exp(s - jnp.max(s, axis=-1, keepdims=True))
    return e, 1.0 / jnp.sum(e, axis=-1, keepdims=True)


def _attn_fwd(z, bm, BL, SEQ):
    T = BL * SEQ
    W2 = 2 * HEAD_DIM

    def body(q_ref, k_ref, v_ref, ag_ref, bm_ref, o_ref, qs0, qs1, kp, vp):
        head0 = _attn_prologue(q_ref, k_ref, v_ref, qs0, qs1, kp, vp, SEQ)

        def group(g, carry):
            r0 = pl.multiple_of(g * QG, QG)
            kw = kp[pl.ds(r0, KW), :]
            vw = vp[pl.ds(r0, KW), :]
            variant = jnp.minimum(g, BIAS_VARIANTS - 1)
            outs = []
            for hh, qs in enumerate((qs0, qs1)):
                e, inv = _attn_weights(qs[pl.ds(r0, QG), :], kw, bm_ref[variant, hh])
                outs.append(jnp.dot(e.astype(bf16), vw, preferred_element_type=f32) * inv)
            o = jnp.where(head0, outs[0], outs[1])
            ag = ag_ref[pl.ds(r0, QG), :]
            o_ref[pl.ds(r0, QG), :] = (o * (ag * _sig(ag))).astype(bf16)
            return carry

        lax.fori_loop(0, SEQ // QG, group, 0, unroll=4)

    def zs(col):
        return pl.BlockSpec((SEQ, W2), lambda b, hp: (b, col // W2 + hp))

    return pl.pallas_call(
        body, grid=(BL, WC // W2),
        in_specs=[zs(COL_Q), zs(COL_K), zs(COL_V), zs(COL_AG),
                  pl.BlockSpec((BIAS_VARIANTS, 2, QG, KW), lambda b, hp: (0, hp, 0, 0))],
        out_specs=pl.BlockSpec((SEQ, W2), lambda b, hp: (b, hp)),
        out_shape=jax.ShapeDtypeStruct((T, WC), bf16),
        scratch_shapes=[pltpu.VMEM((SEQ, W2), bf16), pltpu.VMEM((SEQ, W2), bf16),
                        pltpu.VMEM((SEQ + KEY_PAD, W2), bf16), pltpu.VMEM((SEQ + KEY_PAD, W2), bf16)],
        name="attn_fwd", compiler_params=_params(("parallel", "parallel")))(z, z, z, z, bm)


def _attn_bwd(z, dat, dz, bm, BL, SEQ):
    W2 = 2 * HEAD_DIM

    def body(q_ref, k_ref, v_ref, ag_ref, dat_ref, dzin_ref, bm_ref, dz_ref, dbm_ref, qs0, qs1, kp, vp, dka, dva):
        @pl.when(pl.program_id(1) == 0)
        def _():
            dbm_ref[...] = jnp.zeros_like(dbm_ref)

        head0 = _attn_prologue(q_ref, k_ref, v_ref, qs0, qs1, kp, vp, SEQ)
        dka[...] = jnp.zeros_like(dka)
        dva[...] = jnp.zeros_like(dva)

        def group(g, carry):
            r0 = pl.multiple_of(g * QG, QG)
            kw = kp[pl.ds(r0, KW), :]
            vw = vp[pl.ds(r0, KW), :]
            variant = jnp.minimum(g, BIAS_VARIANTS - 1)
            ag = ag_ref[pl.ds(r0, QG), :]
            do = dat_ref[pl.ds(r0, QG), :] * (ag * _sig(ag))
            outs, dqs = [], []
            for hh, qs in enumerate((qs0, qs1)):
                qh = qs[pl.ds(r0, QG), :]
                e, inv = _attn_weights(qh, kw, bm_ref[variant, hh])
                eb = e.astype(bf16)
                outs.append(jnp.dot(eb, vw, preferred_element_type=f32) * inv)
                doh = (jnp.where(head0, do, 0.0) if hh == 0 else jnp.where(head0, 0.0, do)) * inv
                doh = doh.astype(bf16)
                dp = lax.dot_general(doh, vw, (((1,), (1,)), ((), ())), preferred_element_type=f32)
                ds_ = e * (dp - jnp.sum(e * dp, axis=-1, keepdims=True) * inv)
                dbm_ref[hh] += ds_
                dsb = ds_.astype(bf16)
                dqs.append(jnp.dot(dsb, kw, preferred_element_type=f32))
                dka[pl.ds(r0, KW), :] += lax.dot_general(dsb, qh, (((0,), (0,)), ((), ())), preferred_element_type=f32)
                dva[pl.ds(r0, KW), :] += lax.dot_general(eb, doh, (((0,), (0,)), ((), ())), preferred_element_type=f32)
            o = jnp.where(head0, outs[0], outs[1])
            dq = jnp.where(head0, dqs[0], dqs[1]) * (HEAD_DIM ** -0.5)
            dz_ref[0, pl.ds(r0, QG), :] = dq.astype(bf16)
            dz_ref[3, pl.ds(r0, QG), :] = (dat_ref[pl.ds(r0, QG), :] * o * _dsilu(ag, _sig(ag))).astype(bf16)
            return carry

        lax.fori_loop(0, SEQ // QG, group, 0, unroll=4)

        def flush(g, carry):
            r0 = pl.multiple_of(g * QG, QG)
            dz_ref[1, pl.ds(r0, QG), :] = dka[pl.ds(r0 + KEY_PAD, QG), :].astype(bf16)
            dz_ref[2, pl.ds(r0, QG), :] = dva[pl.ds(r0 + KEY_PAD, QG), :].astype(bf16)
            return carry

        lax.fori_loop(0, SEQ // QG, flush, 0)

    def zs(col):
        return pl.BlockSpec((SEQ, W2), lambda hp, b: (b, col // W2 + hp))

    return pl.pallas_call(
        body, grid=(WC // W2, BL),
        in_specs=[zs(COL_Q), zs(COL_K), zs(COL_V), zs(COL_AG), pl.BlockSpec((SEQ, W2), lambda hp, b: (b, hp)), ANY,
                  pl.BlockSpec((BIAS_VARIANTS, 2, QG, KW), lambda hp, b: (0, hp, 0, 0))],
        out_specs=[pl.BlockSpec((4, SEQ, W2), lambda hp, b: (DZ_ATTN // 4, b, hp)),
                   pl.BlockSpec((2, QG, KW), lambda hp, b: (hp, 0, 0))],
        out_shape=[jax.ShapeDtypeStruct(dz.shape, bf16), jax.ShapeDtypeStruct((8, QG, KW), f32)],
        scratch_shapes=[pltpu.VMEM((SEQ, W2), bf16), pltpu.VMEM((SEQ, W2), bf16),
                        pltpu.VMEM((SEQ + KEY_PAD, W2), bf16), pltpu.VMEM((SEQ + KEY_PAD, W2), bf16),
                        pltpu.VMEM((SEQ + KEY_PAD, W2), f32), pltpu.VMEM((SEQ + KEY_PAD, W2), f32)],
        input_output_aliases={5: 0}, name="attn_bwd",
        compiler_params=_params(("parallel", "arbitrary")))(z, z, z, z, dat, dz, bm)


BIAS_TOP = KEY_PAD + MAX_REL + QG - 1


def _bias_matrix(table):
    n = 2 * MAX_REL
    wd = QG + KW
    e = jnp.concatenate([jnp.broadcast_to(table[:, n:], (8, BIAS_TOP - n + 1)), table[:, n - 1:BIAS_TOP - wd + 1:-1],
                         jnp.zeros((8, 1), f32)], axis=1)
    flat = jnp.broadcast_to(e[:, None, :], (8, QG, wd)).reshape(8, QG * wd)
    skew = flat[:, :QG * (wd - 1)].reshape(8, QG, wd - 1)
    vals = skew[:, :, QG - 1:QG - 1 + KW]
    r = np.arange(QG)[:, None] // CHUNK
    j = np.arange(KW)[None, :]
    band = (j // CHUNK >= r) & (j // CHUNK <= r + LEFT_CHUNKS)
    keep = np.stack([band & (j >= KEY_PAD - v * QG) for v in range(BIAS_VARIANTS)])
    return jnp.where(jnp.asarray(keep)[:, None], vals[None], NEG_INF)


def _bias_fold(dbm):
    wd = QG + KW
    placed = jnp.pad(dbm, ((0, 0), (0, 0), (QG - 1, 0))).reshape(8, QG * (wd - 1))
    return jnp.pad(placed, ((0, 0), (0, QG))).reshape(8, QG, wd)


def _bias_colsum(folded):
    width = folded.shape[2]

    def body(x_ref, o_ref):
        for h in range(8):
            o_ref[pl.ds(h, 1), :] = _colsum(x_ref[h])

    return pl.pallas_call(body, out_shape=jax.ShapeDtypeStruct((8, width), f32), name="bias_colsum",
                          compiler_params=_params())(folded)


def _bias_table_grad(colsum):
    n = 2 * MAX_REL
    wd = QG + KW
    clipped = jnp.sum(colsum[:, :BIAS_TOP - n + 1], axis=1, keepdims=True)
    return jnp.concatenate([jnp.zeros((8, BIAS_TOP - wd + 2), f32), colsum[:, wd - 2:BIAS_TOP - n:-1], clipped], axis=1)


GATE_SPAN = 3 * WC


def _gate_specs(tm):
    return [pl.BlockSpec((tm, GATE_SPAN), lambda i: (i, COL_GM // GATE_SPAN)),
            pl.BlockSpec((tm, GATE_SPAN), lambda i: (i, COL_GM // GATE_SPAN + 1))]


def _gate_block(ga_ref, gb_ref, branch, half):
    k = 2 * branch + half
    ref, k = (ga_ref, k) if k < 3 else (gb_ref, k - 3)
    return _sig(ref[:, k * WC:(k + 1) * WC])


def _resident(shape):
    return pl.BlockSpec(shape, lambda i: (0,) * len(shape), pipeline_mode=pl.Buffered(1))


def _tail_fwd(z, acts, x2, lw):
    T = z.shape[0]
    tm = 256

    def body(cv_ref, at_ref, pv_ref, ga_ref, gb_ref, x_ref, wc_ref, wa_ref, wp_ref, wo_ref, g_ref,
             out_ref, merged_ref, y_ref):
        ys = [jnp.dot(a[...], w[...], preferred_element_type=f32)
              for a, w in ((cv_ref, wc_ref), (at_ref, wa_ref), (pv_ref, wp_ref))]
        halves = []
        for half in range(2):
            cols = slice(half * WC, (half + 1) * WC)
            halves.append(sum(_gate_block(ga_ref, gb_ref, br, half) * ys[br][:, cols] for br in range(3)))
        merged = jnp.concatenate(halves, axis=1).astype(bf16)
        merged_ref[...] = merged
        y = jnp.dot(merged, wo_ref[...], preferred_element_type=f32)
        y_ref[...] = y
        r = lax.rsqrt(jnp.mean(y * y, axis=-1, keepdims=True) + EPS)
        out_ref[...] = x_ref[...] + (y * r) * g_ref[...]

    act = pl.BlockSpec((tm, WC), lambda i: (i, 0))
    row = pl.BlockSpec((tm, D), lambda i: (i, 0))
    return pl.pallas_call(
        body, grid=(T // tm,),
        in_specs=[act, act, act] + _gate_specs(tm) + [row, _resident((WC, D)), _resident((WC, D)), _resident((WC, D)),
                                                      _resident((D, D)), _resident((1, D))],
        out_specs=[row, row, row],
        out_shape=[jax.ShapeDtypeStruct((T, D), f32), jax.ShapeDtypeStruct((T, D), bf16), jax.ShapeDtypeStruct((T, D), f32)],
        name="tail_fwd", compiler_params=_params(("parallel",)))(
            *acts, z, z, x2, lw["w_conv_out"], lw["w_attn_out"], lw["w_pool_out"], lw["w_out"], lw["post_g"])


def _tail_bwd(z, dout, y, acts, lw):
    T = z.shape[0]
    tm = 256
    nt = (((1,), (1,)), ((), ()))

    def body(d_ref, y_ref, cv_ref, at_ref, pv_ref, ga_ref, gb_ref, wc_ref, wa_ref, wp_ref, wo_ref, g_ref,
             dy_ref, dyc_ref, dya_ref, dyp_ref, dz_ref, dcv_ref, dat_ref, dpv_ref, dg_ref):
        y = y_ref[...]
        d = d_ref[...]
        r = lax.rsqrt(jnp.mean(y * y, axis=-1, keepdims=True) + EPS)
        yn = y * r
        dyn = d * g_ref[...]
        dy = (r * (dyn - yn * jnp.mean(dyn * yn, axis=-1, keepdims=True))).astype(bf16)
        dy_ref[...] = dy

        @pl.when(pl.program_id(0) == 0)
        def _():
            dg_ref[...] = jnp.zeros_like(dg_ref)

        dg_ref[...] += _colsum(d * yn)
        dmerged = lax.dot_general(dy, wo_ref[...], nt, preferred_element_type=f32)
        for br, (a_ref, w_ref, dyb_ref, da_ref) in enumerate(((cv_ref, wc_ref, dyc_ref, dcv_ref),
                                                               (at_ref, wa_ref, dya_ref, dat_ref),
                                                               (pv_ref, wp_ref, dyp_ref, dpv_ref))):
            yb = jnp.dot(a_ref[...], w_ref[...], preferred_element_type=f32)
            halves = []
            for half in range(2):
                cols = slice(half * WC, (half + 1) * WC)
                s = _gate_block(ga_ref, gb_ref, br, half)
                dm = dmerged[:, cols]
                halves.append((dm * s).astype(bf16))
                dz_ref[2 * br + half] = (dm * yb[:, cols] * s * (1.0 - s)).astype(bf16)
            dyb = jnp.concatenate(halves, axis=1)
            dyb_ref[...] = dyb
            da_ref[...] = lax.dot_general(dyb, w_ref[...], nt, preferred_element_type=f32)

    act = pl.BlockSpec((tm, WC), lambda i: (i, 0))
    row = pl.BlockSpec((tm, D), lambda i: (i, 0))
    vec = pl.BlockSpec((1, D), lambda i: (0, 0))
    return pl.pallas_call(
        body, grid=(T // tm,),
        in_specs=[row, row, act, act, act] + _gate_specs(tm) + [_resident((WC, D)), _resident((WC, D)), _resident((WC, D)),
                                                                _resident((D, D)), _resident((1, D))],
        out_specs=[row, row, row, row, pl.BlockSpec((6, tm, WC), lambda i: (DZ_GM // 6, i, 0)), act, act, act, vec],
        out_shape=[jax.ShapeDtypeStruct((T, D), bf16)] * 4 + [jax.ShapeDtypeStruct((DZ_BLOCKS, T, WC), bf16)]
        + [jax.ShapeDtypeStruct((T, WC), f32)] * 3 + [jax.ShapeDtypeStruct((1, D), f32)],
        name="tail_bwd", compiler_params=_params(("arbitrary",)))(
            dout, y, *acts, z, z, lw["w_conv_out"], lw["w_attn_out"], lw["w_pool_out"], lw["w_out"], lw["post_g"])


def _adamw(name, g, w, m, v):
    R, C = w.shape
    tr = R
    for cand in (512, 256, 248, 128, 64, 32, 16, 8):
        if R % cand == 0 and cand * C * 4 <= 2 * 1024 * 1024:
            tr = cand
            break
    c1 = 1.0 - ADAM_B1
    c2 = 1.0 - ADAM_B2
    bc1 = 1.0 - ADAM_B1 ** ADAM_STEP
    bc2 = 1.0 - ADAM_B2 ** ADAM_STEP

    def body(g_ref, w_ref, m_ref, v_ref, d_ref, nm_ref, nv_ref):
        g_ = g_ref[...]
        nm = ADAM_B1 * m_ref[...] + c1 * g_
        nv = ADAM_B2 * v_ref[...] + c2 * (g_ * g_)
        nm_ref[...] = nm
        nv_ref[...] = nv
        d_ref[...] = -ADAM_LR * ((nm / bc1) / (jnp.sqrt(nv / bc2) + ADAM_EPS) + ADAM_WD * w_ref[...])

    spec = pl.BlockSpec((tr, C), lambda i: (i, 0))
    return pl.pallas_call(
        body, grid=(R // tr,), in_specs=[spec] * 4, out_specs=[spec] * 3,
        out_shape=[jax.ShapeDtypeStruct((R, C), f32)] * 3, name=name,
        compiler_params=_params(("parallel",)))(g, w, m, v)


def _sum_slots(name, parts):
    _, R, C = parts.shape
    tr = R
    for cand in (256, 128, 64, 32, 16, 8):
        if R % cand == 0 and cand * C * 4 * N_DEV <= 8 * 1024 * 1024:
            tr = cand
            break

    def body(p_ref, o_ref):
        acc = p_ref[0].astype(f32)
        for s in range(1, N_DEV):
            acc = acc + p_ref[s].astype(f32)
        o_ref[...] = acc

    return pl.pallas_call(
        body, grid=(R // tr,), in_specs=[pl.BlockSpec((N_DEV, tr, C), lambda i: (0, i, 0))],
        out_specs=pl.BlockSpec((tr, C), lambda i: (i, 0)), out_shape=jax.ShapeDtypeStruct((R, C), f32),
        name=name, compiler_params=_params(("parallel",)))(parts)


def _row_tile(rows, row_bytes, budget):
    for cand in (512, 256, 128, 64, 32, 16):
        if rows % cand == 0 and cand * row_bytes <= budget:
            return cand
    return rows


def _pair_sum(core, g, theirs):
    R2, C4 = theirs.shape
    tr = _row_tile(R2, C4 * 2, 2 * 1024 * 1024)
    nb = R2 // tr

    def body(core_ref, g_ref, t_ref, o_ref):
        o_ref[...] = (g_ref[...].astype(f32) + t_ref[...].astype(f32)).astype(bf16)

    return pl.pallas_call(
        body,
        grid_spec=pltpu.PrefetchScalarGridSpec(
            num_scalar_prefetch=1, grid=(nb,),
            in_specs=[pl.BlockSpec((tr, C4), lambda i, core_ref: (core_ref[0] * nb + i, 0)),
                      pl.BlockSpec((tr, C4), lambda i, core_ref: (i, 0))],
            out_specs=pl.BlockSpec((tr, C4), lambda i, core_ref: (i, 0))),
        out_shape=jax.ShapeDtypeStruct((R2, C4), bf16), name="pair_sum",
        compiler_params=_params(("parallel",)))(core, g, theirs)


def _chip_sum(chip, mine, others):
    _, R2, C = others.shape
    tr = _row_tile(R2, C * 4, 1024 * 1024)

    def body(chip_ref, m_ref, o_ref, out_ref):
        acc = m_ref[...].astype(f32)
        for s in range(N_CHIPS - 1):
            acc = acc + o_ref[s].astype(f32)
        out_ref[...] = acc

    return pl.pallas_call(
        body,
        grid_spec=pltpu.PrefetchScalarGridSpec(
            num_scalar_prefetch=1, grid=(R2 // tr,),
            in_specs=[pl.BlockSpec((tr, C), lambda i, chip_ref: (i, chip_ref[0])),
                      pl.BlockSpec((N_CHIPS - 1, tr, C), lambda i, chip_ref: (0, i, 0))],
            out_specs=pl.BlockSpec((tr, C), lambda i, chip_ref: (i, 0))),
        out_shape=jax.ShapeDtypeStruct((R2, C), f32), name="chip_sum",
        compiler_params=_params(("parallel",)))(chip, mine, others)


def _place():
    x, y, c = lax.axis_index("x"), lax.axis_index("y"), lax.axis_index("c")
    return x, y, c


def _flip(v, bit):
    return 1 - v if bit else v


CHIP_FLIPS = ((1, 0), (0, 1), (1, 1))


class _Sems:
    def __init__(self, send, recv):
        self.send, self.recv = send, recv
        self.pairs = 0

    def pair(self):
        k = self.pairs
        self.pairs += 1
        return self.send.at[k], self.recv.at[k]


def _remote(src, dst, lands, sems, to):
    s, r = sems.pair()
    copy = pltpu.make_async_remote_copy(src_ref=src, dst_ref=dst, send_sem=s, recv_sem=r, device_id=to, device_id_type=MESH)
    wait = pltpu.make_async_remote_copy(src_ref=lands, dst_ref=lands, send_sem=s, recv_sem=r, device_id=to, device_id_type=MESH)
    return copy, wait


def _exchange(name, build, srcs, lands, n_remote):
    n_s, n_l = len(srcs), len(lands)

    def body(*refs):
        send, recv = refs[n_s + 2 * n_l:]
        remotes, recvs = build(refs[:n_s], refs[n_s + n_l:n_s + 2 * n_l], _Sems(send, recv))
        for cp in remotes:
            cp.start()
        for rv in recvs:
            rv.wait_recv()
        for cp in remotes:
            cp.wait_send()

    return pl.pallas_call(
        body, in_specs=[ANY] * (n_s + n_l), out_specs=[ANY] * n_l,
        out_shape=[jax.ShapeDtypeStruct(t.shape, t.dtype) for t in lands],
        scratch_shapes=[pltpu.SemaphoreType.DMA((n_remote,)), pltpu.SemaphoreType.DMA((n_remote,))],
        input_output_aliases={n_s + i: i for i in range(n_l)}, name=name)(*srcs, *lands)


HBM = pl.BlockSpec(memory_space=pltpu.HBM)
SEMS = pl.BlockSpec(memory_space=pltpu.SEMAPHORE)
DATAFLOW = pltpu.SideEffectType.DATAFLOW_SIDE_EFFECTING


def _start(name, build, srcs, lands, n_remote, after):
    n_s, n_l = len(srcs), len(lands)

    def body(*refs):
        send, recv = refs[n_s + n_l + 1], refs[n_s + n_l + 2]
        remotes, _ = build(refs[:n_s], refs[n_s:n_s + n_l], _Sems(send, recv))
        for cp in remotes:
            cp.start()
        refs[-1][...] = jnp.zeros((8, 128), f32)

    arrays = [pltpu.with_memory_space_constraint(a, pltpu.HBM) for a in (*srcs, *lands)]
    out = pl.pallas_call(
        body, name=name, in_specs=[HBM] * (n_s + n_l) + [ANY],
        out_specs=(SEMS, SEMS, *[HBM] * (n_s + n_l), pl.BlockSpec(memory_space=pltpu.VMEM)),
        out_shape=(pltpu.SemaphoreType.DMA((n_remote,)), pltpu.SemaphoreType.DMA((n_remote,)),
                   *[pltpu.HBM(a.shape, a.dtype) for a in arrays], jax.ShapeDtypeStruct((8, 128), f32)),
        input_output_aliases={i: 2 + i for i in range(n_s + n_l)},
        compiler_params=pltpu.CompilerParams(has_side_effects=DATAFLOW))(*arrays, after)
    return dict(name=name, build=build, sems=out[:2], srcs=out[2:2 + n_s], lands=out[2 + n_s:2 + n_s + n_l], token=out[-1])


def _wait(started, after):
    srcs, lands, build = started["srcs"], started["lands"], started["build"]
    n_s, n_l = len(srcs), len(lands)

    def body(*refs):
        send, recv = refs[n_s + n_l], refs[n_s + n_l + 1]
        remotes, recvs = build(refs[:n_s], refs[n_s:n_s + n_l], _Sems(send, recv))
        for rv in recvs:
            rv.wait_recv()
        for cp in remotes:
            cp.wait_send()

    out = pl.pallas_call(
        body, name=started["name"] + "_wait", in_specs=[HBM] * (n_s + n_l) + [SEMS, SEMS, ANY],
        out_specs=[HBM] * (n_s + n_l), out_shape=[pltpu.HBM(a.shape, a.dtype) for a in (*srcs, *lands)],
        input_output_aliases={i: i for i in range(n_s + n_l)},
        compiler_params=pltpu.CompilerParams(has_side_effects=DATAFLOW))(*srcs, *lands, *started["sems"], after)
    return out[:n_s], out[n_s:]


def _gather_plans(n_split, n_all):
    def over_ici(src, land, sems):
        x, y, c = _place()
        chip = 2 * x + y
        remotes, recvs = [], []
        for a in range(n_all):
            for fx, fy in CHIP_FLIPS:
                px, py = _flip(x, fx), _flip(y, fy)
                if a < n_split:
                    r2 = src[a].shape[0] // 2
                    rows = pl.ds(c * r2, r2)
                    cp, rv = _remote(src[a].at[rows], land[a].at[chip, rows], land[a].at[2 * px + py, rows], sems, (px, py, c))
                else:
                    cp, rv = _remote(src[a], land[a].at[chip], land[a].at[2 * px + py], sems, (px, py, c))
                remotes.append(cp)
                recvs.append(rv)
        return remotes, recvs

    def over_d2d(src, land, sems):
        x, y, c = _place()
        remotes, recvs = [], []
        for a in range(n_split):
            r2 = land[a].shape[1] // 2
            for fx, fy in CHIP_FLIPS:
                owner = 2 * _flip(x, fx) + _flip(y, fy)
                mine = land[a].at[owner, pl.ds(c * r2, r2)]
                cp, rv = _remote(mine, mine, land[a].at[owner, pl.ds((1 - c) * r2, r2)], sems, (x, y, 1 - c))
                remotes.append(cp)
                recvs.append(rv)
        return remotes, recvs

    return over_ici, over_d2d


def _gather_begin(shards, n_split, after):
    over_ici, _ = _gather_plans(n_split, len(shards))
    lands = [lax.empty((N_CHIPS,) + s.shape, s.dtype) for s in shards]
    return _start("gather_ici", over_ici, shards, lands, 3 * len(shards), after)


def _gather_end(started, shards, n_split, after):
    _, over_d2d = _gather_plans(n_split, len(shards))
    lands = _exchange("gather_d2d", over_d2d, [], _wait(started, after)[1], 3 * n_split)
    chip = 2 * lax.axis_index("x") + lax.axis_index("y")
    return [lax.dynamic_update_slice_in_dim(g, s[None], chip, axis=0) for g, s in zip(lands, shards)]


def _reduce_plans(n):
    def to_sibling(src, land, sems):
        x, y, c = _place()
        remotes, recvs = [], []
        for a in range(n):
            r2 = src[a].shape[0] // 2
            cp, rv = _remote(src[a].at[pl.ds((1 - c) * r2, r2), :], land[a], land[a], sems, (x, y, 1 - c))
            remotes.append(cp)
            recvs.append(rv)
        return remotes, recvs

    def across_chips(src, land, sems):
        x, y, c = _place()
        remotes, recvs = [], []
        for a in range(n):
            cw = src[a].shape[1] // N_CHIPS
            for k, (fx, fy) in enumerate(CHIP_FLIPS):
                px, py = _flip(x, fx), _flip(y, fy)
                cp, rv = _remote(src[a].at[:, pl.ds((2 * px + py) * cw, cw)], land[a].at[k], land[a].at[k], sems, (px, py, c))
                remotes.append(cp)
                recvs.append(rv)
        return remotes, recvs

    def share(src, land, sems):
        x, y, c = _place()
        remotes, recvs = [], []
        for a in range(n):
            cp, rv = _remote(src[a], land[a], land[a], sems, (x, y, 1 - c))
            remotes.append(cp)
            recvs.append(rv)
        return remotes, recvs

    return to_sibling, across_chips, share


def _reduce_begin(grads):
    n = len(grads)
    to_sibling, across_chips, _ = _reduce_plans(n)
    core = lax.axis_index("c").reshape(1).astype(jnp.int32)
    theirs = _exchange("reduce_pair", to_sibling, grads,
                       [lax.empty((g.shape[0] // 2, g.shape[1]), bf16) for g in grads], n)
    pair = [_pair_sum(core, g, t) for g, t in zip(grads, theirs)]
    lands = [lax.empty((N_CHIPS - 1, g.shape[0] // 2, g.shape[1] // N_CHIPS), bf16) for g in grads]
    return _start("reduce_chips", across_chips, pair, lands, 3 * n, pair[0])


def _reduce_end(started, after):
    x, y, c = _place()
    chip = (2 * x + y).reshape(1).astype(jnp.int32)
    pair, others = _wait(started, after)
    _, _, share = _reduce_plans(len(pair))
    mine = [_chip_sum(chip, p, o) for p, o in zip(pair, others)]
    sibs = _exchange("reduce_share", share, mine, [lax.empty(h.shape, f32) for h in mine], len(mine))
    return [jnp.where(c == 0, jnp.concatenate([h, s], axis=0), jnp.concatenate([s, h], axis=0))
            for h, s in zip(mine, sibs)]


def _to_all(src, land, sems):
    x, y, c = _place()
    me = 4 * x + 2 * y + c
    remotes, recvs = [], []
    for k in range(1, N_DEV):
        px, py, pc = _flip(x, (k >> 2) & 1), _flip(y, (k >> 1) & 1), _flip(c, k & 1)
        cp, rv = _remote(src[0], land[0].at[me], land[0].at[4 * px + 2 * py + pc], sems, (px, py, pc))
        remotes.append(cp)
        recvs.append(rv)
    return remotes, recvs


def _gather_small_begin(packed):
    return _start("gather_small", _to_all, [packed], [lax.empty((N_DEV,) + packed.shape, f32)], N_DEV - 1, packed)


def _gather_small_end(started, after):
    (packed,), (others,) = _wait(started, after)
    x, y, c = _place()
    return lax.dynamic_update_slice_in_dim(others, packed[None], 4 * x + 2 * y + c, axis=0)


def _rows8(v):
    return jnp.pad(v[None, :], ((0, 7), (0, 0)))


def _vec_rows(vs):
    return jnp.pad(jnp.stack(vs), ((0, 8 - len(vs)), (0, 0)))


SMALL_ROWS = 224


def _pack_small(conv_vec, conv_dw, pool_vec, pool_w, pre_g, post_g, rel):
    return jnp.concatenate([
        conv_vec, conv_dw, pool_vec, pool_w.reshape(GD, WC),
        _rows8(pre_g).reshape(16, WC), _rows8(post_g).reshape(16, WC),
        jnp.pad(rel, ((0, 0), (0, D - rel.shape[1]))).reshape(16, WC)], axis=0)


def _unpack_small(p):
    conv_vec, pool_vec = p[0:8], p[40:48]
    return dict(
        conv_dw_b=conv_vec[0], conv_ln_g=conv_vec[1], conv_ln_b=conv_vec[2], conv_dw=p[8:8 + CONV_K],
        pool_b=pool_vec[0].reshape(4, GD), pool_scale=pool_vec[1], pool_w=p[48:176].reshape(4, GD, GD),
        pre_norm_g=p[176:192].reshape(8, D)[0], post_norm_g=p[192:208].reshape(8, D)[0],
        rel_bias=p[208:224].reshape(8, D)[:, :2 * MAX_REL + 1])


def _layer_fwd(x2, h, ht, lw, BL, SEQ, after=None):
    T = BL * SEQ
    z = _mm("mm_in", h, lw["w_in"], "nn", T, NCOL, D, 1024, 1536, D, f32, after=after)
    cv, u1 = _conv_fwd(z, lw["dw32"], lw["cvec"], BL, SEQ)
    at = _attn_fwd(z, lw["bm"], BL, SEQ)
    pv = _pool_fwd(z, lw["pw"], lw["pvec"], BL, SEQ)
    out, merged, y = _tail_fwd(z, (cv, at, pv), x2, lw)
    return out, dict(x=x2, ht=ht, z=z, u1=u1, acts=(cv, at, pv), merged=merged, y=y)


def _layer_bwd(dout, sv, lw, BL, SEQ, meanwhile=None):
    T = BL * SEQ
    tail = _tail_bwd(sv["z"], dout, sv["y"], sv["acts"], lw)
    dy, dys, dz, dacts, dpost = tail[0], tail[1:4], tail[4], tail[5:8], tail[8]
    dw_out_t = _mm("mm_dw_out", dy, sv["merged"], "tn", D, D, T, D, D, 1024, bf16)
    dws = [_mm("mm_dw_branch", act, dyb, "tn", WC, D, T, WC, D, 1024, bf16) for act, dyb in zip(sv["acts"], dys)]
    if meanwhile is not None:
        meanwhile(dws[2])
    dz, ddw, dcvec = _conv_bwd(sv["z"], sv["u1"], dacts[0], dz, lw["dw32"], lw["cvec"], BL, SEQ)
    dz, dbm = _attn_bwd(sv["z"], dacts[1], dz, lw["bm"], BL, SEQ)
    dz, dpw, dpvec = _pool_bwd(sv["z"], dacts[2], dz, lw["pw"], lw["pvec"], BL, SEQ)
    dw_in = _mm_dw_in(sv["ht"], dz)
    reduction = _reduce_begin([dw_in, dws[0], dws[1], dws[2], dw_out_t])
    dh = _mm_dh(dz, lw["w_in_t"], reduction["token"])
    dx, dpre = _pre_bwd(dh, sv["x"], lw["pre_g"], dout)
    drel = _bias_table_grad(_bias_colsum(_bias_fold(dbm)))
    small = _pack_small(dcvec, ddw, dpvec, dpw, dpre[0], dpost[0], drel)
    return dx, reduction, small


BIG = ("w_in", "w_conv_out", "w_attn_out", "w_pool_out", "w_out")


def _layer_shards(w, l):
    return [w[k][l].astype(bf16) for k in BIG] + [w["conv_dw"][l]]


def _layer_weights(gathered, w, l):
    def cols(g):
        return jnp.transpose(g, (1, 0, 2)).reshape(g.shape[1], N_CHIPS * g.shape[2])

    lw = {k: cols(g) for k, g in zip(BIG[:4], gathered[:4])}
    lw["w_out"] = gathered[4].reshape(D, D)
    lw["w_in_t"] = lw["w_in"].T
    lw["pre_g"] = w["pre_norm_g"][l][None]
    lw["post_g"] = w["post_norm_g"][l][None]
    lw["dw32"] = jnp.pad(cols(gathered[5]), ((0, 32 - CONV_K), (0, 0)))
    lw["cvec"] = _vec_rows([w["conv_dw_b"][l], w["conv_ln_g"][l], w["conv_ln_b"][l]])
    lw["bm"] = _bias_matrix(w["rel_bias"][l])
    lw["pw"] = w["pool_w"][l].astype(bf16)
    lw["pvec"] = _vec_rows([w["pool_b"][l].reshape(WC), w["pool_scale"][l]])
    return lw
SMALL = ("pre_norm_g", "post_norm_g", "conv_dw_b", "conv_ln_g", "conv_ln_b", "rel_bias", "pool_w", "pool_b", "pool_scale")
ORDER = ("pre_norm_g", "post_norm_g", "w_in", "conv_dw", "conv_dw_b", "conv_ln_g", "conv_ln_b", "w_conv_out",
         "rel_bias", "w_attn_out", "pool_w", "pool_b", "pool_scale", "w_pool_out", "w_out")


def _pack_small_params(p):
    return jnp.concatenate([
        _pack_small(_vec_rows([p["conv_dw_b"][l], p["conv_ln_g"][l], p["conv_ln_b"][l]]), jnp.zeros((32, WC), f32),
                    _vec_rows([p["pool_b"][l].reshape(WC), p["pool_scale"][l]]), p["pool_w"][l],
                    p["pre_norm_g"][l], p["post_norm_g"][l], p["rel_bias"][l])
        for l in range(DEPTH)], axis=0)


def _unpack_small_params(packed):
    layers = [_unpack_small(packed[l * SMALL_ROWS:(l + 1) * SMALL_ROWS]) for l in range(DEPTH)]
    return {k: jnp.stack([layers[l][k] for l in range(DEPTH)]) for k in layers[0]}


def kernel(x, pre_norm_g, post_norm_g, w_in, conv_dw, conv_dw_b, conv_ln_g, conv_ln_b, w_conv_out, rel_bias, w_attn_out, pool_w, pool_b, pool_scale, w_pool_out, w_out, loss_target, m_pre_norm_g, m_post_norm_g, m_w_in, m_conv_dw, m_conv_dw_b, m_conv_ln_g, m_conv_ln_b, m_w_conv_out, m_rel_bias, m_w_attn_out, m_pool_w, m_pool_b, m_pool_scale, m_w_pool_out, m_w_out, v_pre_norm_g, v_post_norm_g, v_w_in, v_conv_dw, v_conv_dw_b, v_conv_ln_g, v_conv_ln_b, v_w_conv_out, v_rel_bias, v_w_attn_out, v_pool_w, v_pool_b, v_pool_scale, v_w_pool_out, v_w_out):
    BL, SEQ, _ = x.shape
    T = BL * SEQ
    w = dict(pre_norm_g=pre_norm_g, post_norm_g=post_norm_g, w_in=w_in, conv_dw=conv_dw, conv_dw_b=conv_dw_b,
             conv_ln_g=conv_ln_g, conv_ln_b=conv_ln_b, w_conv_out=w_conv_out, rel_bias=rel_bias, w_attn_out=w_attn_out,
             pool_w=pool_w, pool_b=pool_b, pool_scale=pool_scale, w_pool_out=w_pool_out, w_out=w_out)
    m = dict(pre_norm_g=m_pre_norm_g, post_norm_g=m_post_norm_g, w_in=m_w_in, conv_dw=m_conv_dw, conv_dw_b=m_conv_dw_b,
             conv_ln_g=m_conv_ln_g, conv_ln_b=m_conv_ln_b, w_conv_out=m_w_conv_out, rel_bias=m_rel_bias,
             w_attn_out=m_w_attn_out, pool_w=m_pool_w, pool_b=m_pool_b, pool_scale=m_pool_scale,
             w_pool_out=m_w_pool_out, w_out=m_w_out)
    v = dict(pre_norm_g=v_pre_norm_g, post_norm_g=v_post_norm_g, w_in=v_w_in, conv_dw=v_conv_dw, conv_dw_b=v_conv_dw_b,
             conv_ln_g=v_conv_ln_g, conv_ln_b=v_conv_ln_b, w_conv_out=v_w_conv_out, rel_bias=v_rel_bias,
             w_attn_out=v_w_attn_out, pool_w=v_pool_w, pool_b=v_pool_b, pool_scale=v_pool_scale,
             w_pool_out=v_w_pool_out, w_out=v_w_out)

    n_split = len(BIG)
    shards = [_layer_shards(w, l) for l in range(DEPTH)]
    x2 = x.reshape(T, D)
    h0, ht0 = _rms_pre(x2, pre_norm_g[0][None])
    gather0 = _gather_begin(shards[0], n_split, x2)
    lw0 = _layer_weights(_gather_end(gather0, shards[0], n_split, ht0), w, 0)
    gather1 = _gather_begin(shards[1], n_split, lw0["w_in"])
    out0, saved0 = _layer_fwd(x2, h0, ht0, lw0, BL, SEQ, after=gather1["token"])
    lw1 = _layer_weights(_gather_end(gather1, shards[1], n_split, out0), w, 1)
    h1, ht1 = _rms_pre(out0, lw1["pre_g"])
    out1, saved1 = _layer_fwd(out0, h1, ht1, lw1, BL, SEQ)
    dout, sq = _loss_head(out1, loss_target.reshape(T, D))
    loss = lax.psum(0.5 * jnp.sum(sq) / float(D), ("x", "y", "c"))

    summed = [None] * DEPTH
    dx1, reduction1, small1 = _layer_bwd(dout, saved1, lw1, BL, SEQ)
    small_gather1 = _gather_small_begin(small1)

    def finish_layer1(after):
        summed[1] = _reduce_end(reduction1, after)

    grad_x, reduction0, small0 = _layer_bwd(dx1, saved0, lw0, BL, SEQ, meanwhile=finish_layer1)
    small_gather0 = _gather_small_begin(small0)
    summed[0] = _reduce_end(reduction0, grad_x)
    gsmall = jnp.concatenate([_sum_slots("sum_small", _gather_small_end(s, summed[0][0]))
                              for s in (small_gather0, small_gather1)], axis=0)

    grads, deltas, new_m, new_v = {}, {}, {}, {}
    for i, k in enumerate(BIG):
        g = jnp.stack([summed[l][i] for l in range(DEPTH)])
        if k == "w_out":
            g = jnp.transpose(g, (0, 2, 1))
        grads[k] = g
        shape = w[k].shape
        flat2 = lambda a: a.reshape(shape[0] * shape[1], shape[2])
        d_, nm_, nv_ = _adamw("adamw_big", flat2(g), flat2(w[k]), flat2(m[k]), flat2(v[k]))
        deltas[k], new_m[k], new_v[k] = d_.reshape(shape), nm_.reshape(shape), nv_.reshape(shape)

    d_, nm_, nv_ = _adamw("adamw_small", gsmall, _pack_small_params(w), _pack_small_params(m), _pack_small_params(v))
    gs, ds, ms, vs = (_unpack_small_params(a) for a in (gsmall, d_, nm_, nv_))
    for k in SMALL:
        grads[k], deltas[k], new_m[k], new_v[k] = gs[k], ds[k], ms[k], vs[k]
    chip = 2 * lax.axis_index("x") + lax.axis_index("y")
    g_dw = lax.dynamic_slice_in_dim(gs["conv_dw"], chip * GD, GD, axis=2)
    flat2 = lambda a: a.reshape(DEPTH * CONV_K, GD)
    d_, nm_, nv_ = _adamw("adamw_conv_dw", flat2(g_dw), flat2(conv_dw), flat2(m["conv_dw"]), flat2(v["conv_dw"]))
    grads["conv_dw"] = g_dw
    deltas["conv_dw"], new_m["conv_dw"], new_v["conv_dw"] = (a.reshape(conv_dw.shape) for a in (d_, nm_, nv_))

    return (loss, grad_x.reshape(x.shape), *[grads[k] for k in ORDER], *[deltas[k] for k in ORDER],
            *[new_m[k] for k in ORDER], *[new_v[k] for k in ORDER])
```

```python
import numpy as np
import jax
import jax.numpy as jnp
from jax import lax
from jax.experimental import pallas as pl
from jax.experimental.pallas import tpu as pltpu

f32 = jnp.float32
bf16 = jnp.bfloat16

D = 1024
DEPTH = 2
WC = 512
HEAD_DIM = 64
CHUNK = 64
LEFT_CHUNKS = 8
KEY_PAD = LEFT_CHUNKS * CHUNK
MAX_REL = 256
CONV_K = 31
POOL_WINDOWS = (2, 4, 8, 16)
GD = 128
NCOL = 7680
EPS = 1e-6
NEG_INF = -1e30
COL_A, COL_B, COL_CG, COL_Q, COL_K, COL_V, COL_AG, COL_PI, COL_PG, COL_GM = (
    0, 512, 1024, 1536, 2048, 2560, 3072, 3584, 4096, 4608)

ADAM_LR = 0.001
ADAM_B1 = 0.9
ADAM_B2 = 0.999
ADAM_EPS = 1e-08
ADAM_WD = 0.01
ADAM_STEP = 10

QG = 256
KW = KEY_PAD + QG
BIAS_VARIANTS = KEY_PAD // QG + 1
CT = 128
HALO = 32
PHALO = 16
N_CHIPS = 4
N_DEV = 8
VMEM_LIMIT = 56 * 1024 * 1024
MESH = pl.DeviceIdType.MESH
ANY = pl.BlockSpec(memory_space=pl.ANY)

DZ_BLOCKS = 18
DZ_CONV, DZ_ATTN, DZ_POOL, DZ_GM = 0, 4, 8, 12


def _dz_block(c):
    return c + (c >= 3).astype(jnp.int32) + 2 * (c >= 9).astype(jnp.int32)


def _params(sem=None):
    return pltpu.CompilerParams(dimension_semantics=sem, vmem_limit_bytes=VMEM_LIMIT)


def _sig(x):
    return 1.0 / (1.0 + jnp.exp(-x))


def _dsilu(x, s):
    return s * (1.0 + x * (1.0 - s))


def _colsum(x):
    return jnp.sum(x, axis=0, keepdims=True)


def _rms_pre(x2, g):
    T = x2.shape[0]
    tm = 512

    def body(x_ref, g_ref, h_ref, ht_ref):
        x = x_ref[...]
        r = lax.rsqrt(jnp.mean(x * x, axis=-1, keepdims=True) + EPS)
        h = (x * r) * g_ref[...]
        h_ref[...] = h.astype(bf16)
        ht_ref[...] = h.T.astype(bf16)

    row = pl.BlockSpec((tm, D), lambda i: (i, 0))
    vec = pl.BlockSpec((1, D), lambda i: (0, 0))
    return pl.pallas_call(
        body, grid=(T // tm,), in_specs=[row, vec], out_specs=[row, pl.BlockSpec((D, tm), lambda i: (0, i))],
        out_shape=[jax.ShapeDtypeStruct((T, D), bf16), jax.ShapeDtypeStruct((D, T), bf16)], name="rms_pre",
        compiler_params=_params(("parallel",)))(x2, g)


def _loss_head(out, tgt):
    T = out.shape[0]
    tm = 512

    def body(o_ref, t_ref, d_ref, l_ref):
        e = o_ref[...] - t_ref[...]
        d_ref[...] = e / float(D)

        @pl.when(pl.program_id(0) == 0)
        def _():
            l_ref[...] = jnp.zeros_like(l_ref)

        l_ref[...] += _colsum(e * e)

    row = pl.BlockSpec((tm, D), lambda i: (i, 0))
    vec = pl.BlockSpec((1, D), lambda i: (0, 0))
    return pl.pallas_call(
        body, grid=(T // tm,), in_specs=[row, row], out_specs=[row, vec],
        out_shape=[jax.ShapeDtypeStruct((T, D), f32), jax.ShapeDtypeStruct((1, D), f32)],
        name="loss_head", compiler_params=_params(("arbitrary",)))(out, tgt)


def _pre_bwd(dh, x2, g, dout):
    T = x2.shape[0]
    tm = 512

    def body(dh_ref, x_ref, g_ref, d_ref, dx_ref, dg_ref):
        x = x_ref[...]
        dh_ = dh_ref[...]
        r = lax.rsqrt(jnp.mean(x * x, axis=-1, keepdims=True) + EPS)
        xn = x * r
        dxn = dh_ * g_ref[...]
        dx_ref[...] = r * (dxn - xn * jnp.mean(dxn * xn, axis=-1, keepdims=True)) + d_ref[...]

        @pl.when(pl.program_id(0) == 0)
        def _():
            dg_ref[...] = jnp.zeros_like(dg_ref)

        dg_ref[...] += _colsum(dh_ * xn)

    row = pl.BlockSpec((tm, D), lambda i: (i, 0))
    vec = pl.BlockSpec((1, D), lambda i: (0, 0))
    return pl.pallas_call(
        body, grid=(T // tm,), in_specs=[row, row, vec, row], out_specs=[row, vec],
        out_shape=[jax.ShapeDtypeStruct((T, D), f32), jax.ShapeDtypeStruct((1, D), f32)],
        name="pre_bwd", compiler_params=_params(("arbitrary",)))(dh, x2, g, dout)


def _mm(name, a, b, mode, m, n, k, tm, tn, tk, out_dtype, after=None):
    nk = k // tk
    assert m % tm == 0 and n % tn == 0 and k % tk == 0
    if mode == "nn":
        a_spec = pl.BlockSpec((tm, tk), lambda i, j, kk: (i, kk))
        b_spec = pl.BlockSpec((tk, tn), lambda i, j, kk: (kk, j))
        dn = (((1,), (0,)), ((), ()))
    elif mode == "nt":
        a_spec = pl.BlockSpec((tm, tk), lambda i, j, kk: (i, kk))
        b_spec = pl.BlockSpec((tn, tk), lambda i, j, kk: (j, kk))
        dn = (((1,), (1,)), ((), ()))
    else:
        a_spec = pl.BlockSpec((tk, tm), lambda i, j, kk: (kk, i))
        b_spec = pl.BlockSpec((tk, tn), lambda i, j, kk: (kk, j))
        dn = (((0,), (0,)), ((), ()))
    extra = [] if after is None else [after]

    def body(a_ref, b_ref, *rest):
        o_ref, acc_ref = rest[len(extra):]
        p = lax.dot_general(a_ref[...].astype(bf16), b_ref[...].astype(bf16), dn, preferred_element_type=f32)
        if nk == 1:
            o_ref[...] = p.astype(o_ref.dtype)
        else:
            kk = pl.program_id(2)

            @pl.when(kk == 0)
            def _():
                acc_ref[...] = p

            @pl.when(kk > 0)
            def _():
                acc_ref[...] += p

            @pl.when(kk == nk - 1)
            def _():
                o_ref[...] = acc_ref[...].astype(o_ref.dtype)

    acc_shape = (tm, tn) if nk > 1 else (8, 128)
    return pl.pallas_call(
        body, grid=(m // tm, n // tn, nk), in_specs=[a_spec, b_spec] + [ANY] * len(extra),
        out_specs=pl.BlockSpec((tm, tn), lambda i, j, kk: (i, j)),
        out_shape=jax.ShapeDtypeStruct((m, n), out_dtype),
        scratch_shapes=[pltpu.VMEM(acc_shape, f32)], name=name,
        compiler_params=_params(("parallel", "parallel", "arbitrary")))(a, b, *extra)


DZ_SPANS = ((DZ_CONV, 3), (DZ_ATTN, 4), (DZ_POOL, 2), (DZ_GM, 6))


def _mm_dh(dz, w_in, after):
    T = dz.shape[1]
    tm = 512

    def body(conv_ref, attn_ref, pool_ref, gm_ref, w_ref, after_ref, o_ref):
        acc = None
        col = 0
        for ref, (_, blocks) in zip((conv_ref, attn_ref, pool_ref, gm_ref), DZ_SPANS):
            for b in range(blocks):
                p = lax.dot_general(ref[b], w_ref[:, col * WC:(col + 1) * WC], (((1,), (1,)), ((), ())),
                                    preferred_element_type=f32)
                acc = p if acc is None else acc + p
                col += 1
        o_ref[...] = acc

    spans = [pl.BlockSpec((blocks, tm, WC), lambda i, first=first, blocks=blocks: (first // blocks, i, 0))
             for first, blocks in DZ_SPANS]
    return pl.pallas_call(
        body, grid=(T // tm,),
        in_specs=spans + [pl.BlockSpec((D, NCOL), lambda i: (0, 0), pipeline_mode=pl.Buffered(1)), ANY],
        out_specs=pl.BlockSpec((tm, D), lambda i: (i, 0)), out_shape=jax.ShapeDtypeStruct((T, D), f32),
        name="mm_dh", compiler_params=_params(("parallel",)))(dz, dz, dz, dz, w_in, after)


def _mm_dw_in(ht, dz, after):
    T = dz.shape[1]

    def body(ht_ref, dz_ref, after_ref, o_ref):
        o_ref[...] = jnp.dot(ht_ref[...], dz_ref[...], preferred_element_type=f32).astype(bf16)

    return pl.pallas_call(
        body, grid=(NCOL // WC,),
        in_specs=[pl.BlockSpec((D, T), lambda j: (0, 0), pipeline_mode=pl.Buffered(1)),
                  pl.BlockSpec((None, T, WC), lambda j: (_dz_block(j), 0, 0)), ANY],
        out_specs=pl.BlockSpec((D, WC), lambda j: (0, j)), out_shape=jax.ShapeDtypeStruct((D, NCOL), bf16),
        name="mm_dw_in", compiler_params=_params(("parallel",)))(ht, dz, after)


def _conv_delays():
    return [(8 * a + b, a, b) for b in range(8) for a in range(4) if 8 * a + b < CONV_K]


def _conv_rolls(win):
    return [win if b == 0 else pltpu.roll(win, b, axis=0) for b in range(8)]


def _conv_taps(rolled, dw_ref):
    acc = None
    for d, a, b in _conv_delays():
        term = rolled[b][HALO - 8 * a:HALO - 8 * a + CT, :] * dw_ref[pl.ds(CONV_K - 1 - d, 1), :]
        acc = term if acc is None else acc + term
    return acc


def _conv_fwd(z, dw32, cvec, BL, SEQ):
    T = BL * SEQ
    nct = SEQ // CT

    def body(a_ref, b_ref, cg_ref, dw_ref, vec_ref, o_ref, u1_ref, p_ref):
        p_ref[pl.ds(0, HALO), :] = jnp.zeros((HALO, WC), f32)

        def glu(c, carry):
            r0 = pl.multiple_of(c * CT, CT)
            p_ref[pl.ds(r0 + HALO, CT), :] = a_ref[pl.ds(r0, CT), :] * _sig(b_ref[pl.ds(r0, CT), :])
            return carry

        lax.fori_loop(0, nct, glu, 0)

        def step(c, carry):
            r0 = pl.multiple_of(c * CT, CT)
            u1 = _conv_taps(_conv_rolls(p_ref[pl.ds(r0, CT + HALO), :]), dw_ref) + vec_ref[0:1, :]
            u1_ref[pl.ds(r0, CT), :] = u1
            xc = u1 - jnp.mean(u1, axis=-1, keepdims=True)
            rs = lax.rsqrt(jnp.mean(xc * xc, axis=-1, keepdims=True) + EPS)
            u2 = (xc * rs) * vec_ref[1:2, :] + vec_ref[2:3, :]
            cg = cg_ref[pl.ds(r0, CT), :]
            o_ref[pl.ds(r0, CT), :] = ((u2 * _sig(u2)) * (cg * _sig(cg))).astype(bf16)
            return carry

        lax.fori_loop(0, nct, step, 0)

    def zs(col):
        return pl.BlockSpec((SEQ, WC), lambda b: (b, col // WC))

    seq = pl.BlockSpec((SEQ, WC), lambda b: (b, 0))
    return pl.pallas_call(
        body, grid=(BL,),
        in_specs=[zs(COL_A), zs(COL_B), zs(COL_CG), pl.BlockSpec((32, WC), lambda b: (0, 0)),
                  pl.BlockSpec((8, WC), lambda b: (0, 0))],
        out_specs=[seq, seq],
        out_shape=[jax.ShapeDtypeStruct((T, WC), bf16), jax.ShapeDtypeStruct((T, WC), f32)],
        scratch_shapes=[pltpu.VMEM((SEQ + HALO, WC), f32)], name="conv_fwd",
        compiler_params=_params(("parallel",)))(z, z, z, dw32, cvec)


def _conv_bwd(z, u1, dcv, dz, dw32, cvec, BL, SEQ):
    nct = SEQ // CT

    def body(a_ref, b_ref, cg_ref, u1_ref, dcv_ref, dzin_ref, dw_ref, vec_ref, dz_ref, ddw_ref, dvec_ref,
             p_ref, q_ref, taps_ref):
        @pl.when(pl.program_id(0) == 0)
        def _():
            ddw_ref[...] = jnp.zeros_like(ddw_ref)
            dvec_ref[...] = jnp.zeros_like(dvec_ref)

        p_ref[pl.ds(0, HALO), :] = jnp.zeros((HALO, WC), f32)
        q_ref[pl.ds(SEQ, HALO), :] = jnp.zeros((HALO, WC), f32)

        def glu(c, carry):
            r0 = pl.multiple_of(c * CT, CT)
            p_ref[pl.ds(r0 + HALO, CT), :] = a_ref[pl.ds(r0, CT), :] * _sig(b_ref[pl.ds(r0, CT), :])
            return carry

        lax.fori_loop(0, nct, glu, 0)

        def step(c, carry):
            r0 = pl.multiple_of(c * CT, CT)
            rolled = _conv_rolls(p_ref[pl.ds(r0, CT + HALO), :])
            u1 = u1_ref[pl.ds(r0, CT), :]
            xc = u1 - jnp.mean(u1, axis=-1, keepdims=True)
            rs = lax.rsqrt(jnp.mean(xc * xc, axis=-1, keepdims=True) + EPS)
            nrm = xc * rs
            u2 = nrm * vec_ref[1:2, :] + vec_ref[2:3, :]
            s2 = _sig(u2)
            u3 = u2 * s2
            cg = cg_ref[pl.ds(r0, CT), :]
            scg = _sig(cg)
            dcv_ = dcv_ref[pl.ds(r0, CT), :]
            dz_ref[2, pl.ds(r0, CT), :] = (dcv_ * u3 * _dsilu(cg, scg)).astype(bf16)
            du2 = dcv_ * (cg * scg) * _dsilu(u2, s2)
            dvec_ref[1:2, :] += _colsum(du2 * nrm)
            dvec_ref[2:3, :] += _colsum(du2)
            dn = du2 * vec_ref[1:2, :]
            du1 = rs * (dn - jnp.mean(dn, axis=-1, keepdims=True)
                        - nrm * jnp.mean(dn * nrm, axis=-1, keepdims=True))
            dvec_ref[0:1, :] += _colsum(du1)
            q_ref[pl.ds(r0, CT), :] = du1
            for d, a, b in _conv_delays():
                prod = du1 * rolled[b][HALO - 8 * a:HALO - 8 * a + CT, :]
                taps_ref[CONV_K - 1 - d] += jnp.sum(prod.reshape(CT // 8, 8, WC), axis=0)
            return carry

        taps_ref[...] = jnp.zeros_like(taps_ref)
        lax.fori_loop(0, nct, step, 0)
        for row in range(CONV_K):
            ddw_ref[pl.ds(row, 1), :] += _colsum(taps_ref[row])

        def back(c, carry):
            r0 = pl.multiple_of(c * CT, CT)
            wq = q_ref[pl.ds(r0, CT + HALO), :]
            up = {}
            acc = None
            for d, a, b in _conv_delays():
                if b not in up:
                    up[b] = wq if b == 0 else pltpu.roll(wq, CT + HALO - b, axis=0)
                term = up[b][8 * a:8 * a + CT, :] * dw_ref[pl.ds(CONV_K - 1 - d, 1), :]
                acc = term if acc is None else acc + term
            a_ = a_ref[pl.ds(r0, CT), :]
            sb = _sig(b_ref[pl.ds(r0, CT), :])
            dz_ref[0, pl.ds(r0, CT), :] = (acc * sb).astype(bf16)
            dz_ref[1, pl.ds(r0, CT), :] = (acc * a_ * sb * (1.0 - sb)).astype(bf16)
            return carry

        lax.fori_loop(0, nct, back, 0)

    def zs(col):
        return pl.BlockSpec((SEQ, WC), lambda b: (b, col // WC), pipeline_mode=pl.Buffered(1))

    def const(r):
        return pl.BlockSpec((r, WC), lambda b: (0, 0))

    seq = pl.BlockSpec((SEQ, WC), lambda b: (b, 0), pipeline_mode=pl.Buffered(1))
    return pl.pallas_call(
        body, grid=(BL,),
        in_specs=[zs(COL_A), zs(COL_B), zs(COL_CG), seq, seq, ANY, const(32), const(8)],
        out_specs=[pl.BlockSpec((3, SEQ, WC), lambda b: (DZ_CONV // 3, b, 0)), const(32), const(8)],
        out_shape=[jax.ShapeDtypeStruct(dz.shape, bf16), jax.ShapeDtypeStruct((32, WC), f32),
                   jax.ShapeDtypeStruct((8, WC), f32)],
        scratch_shapes=[pltpu.VMEM((SEQ + HALO, WC), f32), pltpu.VMEM((SEQ + HALO, WC), f32),
                        pltpu.VMEM((CONV_K, 8, WC), f32)],
        input_output_aliases={5: 0}, name="conv_bwd",
        compiler_params=_params(("arbitrary",)))(z, z, z, u1, dcv, dz, dw32, cvec)


def _pool_counts(r0):
    t1 = r0 + 1 + lax.broadcasted_iota(jnp.int32, (CT, 1), 0)
    return [jnp.minimum(t1, w).astype(f32) for w in POOL_WINDOWS]


def _pool_sums(win, forward):
    n = CT + PHALO

    def sh(x, s):
        return pltpu.roll(x, (n - s) if forward else s, axis=0)

    s2 = win + sh(win, 1)
    s4 = s2[:, GD:] + sh(s2[:, GD:], 2)
    s8 = s4[:, GD:] + sh(s4[:, GD:], 4)
    s16 = s8[:, GD:] + sh(s8[:, GD:], 8)
    lo = 0 if forward else PHALO
    return [s[lo:lo + CT, :GD] for s in (s2, s4, s8, s16)]


def _pool_fwd(z, pw, pvec, BL, SEQ):
    T = BL * SEQ
    nct = SEQ // CT

    def body(pi_ref, pg_ref, pw_ref, vec_ref, o_ref, p_ref):
        p_ref[pl.ds(0, PHALO), :] = jnp.zeros((PHALO, WC), f32)

        def fill(c, carry):
            r0 = pl.multiple_of(c * CT, CT)
            p_ref[pl.ds(r0 + PHALO, CT), :] = pi_ref[pl.ds(r0, CT), :]
            return carry

        lax.fori_loop(0, nct, fill, 0)

        def step(c, carry):
            r0 = pl.multiple_of(c * CT, CT)
            sums = _pool_sums(p_ref[pl.ds(r0, CT + PHALO), :], False)
            cnt = _pool_counts(r0)
            pin = pi_ref[pl.ds(r0, CT), :]
            mixed = []
            for g in range(4):
                pooled = sums[g] / cnt[g] - pin[:, g * GD:(g + 1) * GD]
                mixed.append(jnp.dot(pooled.astype(bf16), pw_ref[g], preferred_element_type=f32))
            m0 = jnp.concatenate(mixed, axis=1) + vec_ref[0:1, :]
            pg = pg_ref[pl.ds(r0, CT), :]
            o_ref[pl.ds(r0, CT), :] = ((m0 * vec_ref[1:2, :]) * (pg * _sig(pg))).astype(bf16)
            return carry

        lax.fori_loop(0, nct, step, 0)

    def zs(col):
        return pl.BlockSpec((SEQ, WC), lambda b: (b, col // WC))

    return pl.pallas_call(
        body, grid=(BL,),
        in_specs=[zs(COL_PI), zs(COL_PG), pl.BlockSpec((4, GD, GD), lambda b: (0, 0, 0)),
                  pl.BlockSpec((8, WC), lambda b: (0, 0))],
        out_specs=pl.BlockSpec((SEQ, WC), lambda b: (b, 0)),
        out_shape=jax.ShapeDtypeStruct((T, WC), bf16),
        scratch_shapes=[pltpu.VMEM((SEQ + PHALO, WC), f32)], name="pool_fwd",
        compiler_params=_params(("parallel",)))(z, z, pw, pvec)


def _pool_bwd(z, dpl, dz, pw, pvec, BL, SEQ):
    nct = SEQ // CT

    def body(pi_ref, pg_ref, dpl_ref, dzin_ref, pw_ref, vec_ref, dz_ref, dpw_ref, dvec_ref, p_ref, e_ref, dp_ref):
        @pl.when(pl.program_id(0) == 0)
        def _():
            dpw_ref[...] = jnp.zeros_like(dpw_ref)
            dvec_ref[...] = jnp.zeros_like(dvec_ref)

        p_ref[pl.ds(0, PHALO), :] = jnp.zeros((PHALO, WC), f32)
        e_ref[pl.ds(SEQ, PHALO), :] = jnp.zeros((PHALO, WC), f32)

        def fill(c, carry):
            r0 = pl.multiple_of(c * CT, CT)
            p_ref[pl.ds(r0 + PHALO, CT), :] = pi_ref[pl.ds(r0, CT), :]
            return carry

        lax.fori_loop(0, nct, fill, 0)

        def step(c, carry):
            r0 = pl.multiple_of(c * CT, CT)
            sums = _pool_sums(p_ref[pl.ds(r0, CT + PHALO), :], False)
            cnt = _pool_counts(r0)
            pin = pi_ref[pl.ds(r0, CT), :]
            pooled = [(sums[g] / cnt[g] - pin[:, g * GD:(g + 1) * GD]).astype(bf16) for g in range(4)]
            m0 = jnp.concatenate(
                [jnp.dot(pooled[g], pw_ref[g], preferred_element_type=f32) for g in range(4)], axis=1) + vec_ref[0:1, :]
            scale = vec_ref[1:2, :]
            pg = pg_ref[pl.ds(r0, CT), :]
            spg = _sig(pg)
            dpl_ = dpl_ref[pl.ds(r0, CT), :]
            dmixed = dpl_ * (pg * spg)
            dz_ref[1, pl.ds(r0, CT), :] = (dpl_ * (m0 * scale) * _dsilu(pg, spg)).astype(bf16)
            dvec_ref[1:2, :] += _colsum(dmixed * m0)
            dm0 = dmixed * scale
            dvec_ref[0:1, :] += _colsum(dm0)
            dps, es = [], []
            for g in range(4):
                dm0g = dm0[:, g * GD:(g + 1) * GD].astype(bf16)
                dpw_ref[g] += lax.dot_general(pooled[g], dm0g, (((0,), (0,)), ((), ())), preferred_element_type=f32)
                dpg = lax.dot_general(dm0g, pw_ref[g], (((1,), (1,)), ((), ())), preferred_element_type=f32)
                dps.append(dpg)
                es.append(dpg / cnt[g])
            dp_ref[pl.ds(r0, CT), :] = jnp.concatenate(dps, axis=1)
            e_ref[pl.ds(r0, CT), :] = jnp.concatenate(es, axis=1)
            return carry

        lax.fori_loop(0, nct, step, 0)

        def back(c, carry):
            r0 = pl.multiple_of(c * CT, CT)
            fs = _pool_sums(e_ref[pl.ds(r0, CT + PHALO), :], True)
            dz_ref[0, pl.ds(r0, CT), :] = (jnp.concatenate(fs, axis=1) - dp_ref[pl.ds(r0, CT), :]).astype(bf16)
            return carry

        lax.fori_loop(0, nct, back, 0)

    def zs(col):
        return pl.BlockSpec((SEQ, WC), lambda b: (b, col // WC), pipeline_mode=pl.Buffered(1))

    return pl.pallas_call(
        body, grid=(BL,),
        in_specs=[zs(COL_PI), zs(COL_PG),
                  pl.BlockSpec((SEQ, WC), lambda b: (b, 0), pipeline_mode=pl.Buffered(1)), ANY,
                  pl.BlockSpec((4, GD, GD), lambda b: (0, 0, 0)), pl.BlockSpec((8, WC), lambda b: (0, 0))],
        out_specs=[pl.BlockSpec((2, SEQ, WC), lambda b: (DZ_POOL // 2, b, 0)),
                   pl.BlockSpec((4, GD, GD), lambda b: (0, 0, 0)), pl.BlockSpec((8, WC), lambda b: (0, 0))],
        out_shape=[jax.ShapeDtypeStruct(dz.shape, bf16), jax.ShapeDtypeStruct((4, GD, GD), f32),
                   jax.ShapeDtypeStruct((8, WC), f32)],
        scratch_shapes=[pltpu.VMEM((SEQ + PHALO, WC), f32), pltpu.VMEM((SEQ + PHALO, WC), f32),
                        pltpu.VMEM((SEQ, WC), f32)],
        input_output_aliases={3: 0}, name="pool_bwd",
        compiler_params=_params(("arbitrary",)))(z, z, dpl, dz, pw, pvec)


def _attn_prologue(q_ref, k_ref, v_ref, qs0, qs1, kp, vp, SEQ):
    head0 = lax.broadcasted_iota(jnp.int32, (1, 2 * HEAD_DIM), 1) < HEAD_DIM
    kp[pl.ds(0, KEY_PAD), :] = jnp.zeros((KEY_PAD, 2 * HEAD_DIM), bf16)
    vp[pl.ds(0, KEY_PAD), :] = jnp.zeros((KEY_PAD, 2 * HEAD_DIM), bf16)

    def fill(g, carry):
        r0 = pl.multiple_of(g * QG, QG)
        q = q_ref[pl.ds(r0, QG), :] * (HEAD_DIM ** -0.5)
        qs0[pl.ds(r0, QG), :] = jnp.where(head0, q, 0.0).astype(bf16)
        qs1[pl.ds(r0, QG), :] = jnp.where(head0, 0.0, q).astype(bf16)
        kp[pl.ds(r0 + KEY_PAD, QG), :] = k_ref[pl.ds(r0, QG), :].astype(bf16)
        vp[pl.ds(r0 + KEY_PAD, QG), :] = v_ref[pl.ds(r0, QG), :].astype(bf16)
        return carry

    lax.fori_loop(0, SEQ // QG, fill, 0)
    return head0


def _attn_weights(qh, kw, bias):
    s = lax.dot_general(qh, kw, (((1,), (1,)), ((), ())), preferred_element_type=f32) + bias
    e = jnp.exp(s - jnp.max(s, axis=-1, keepdims=True))
    return e, 1.0 / jnp.sum(e, axis=-1, keepdims=True)


def _attn_fwd(z, bm, BL, SEQ):
    T = BL * SEQ
    W2 = 2 * HEAD_DIM

    def body(q_ref, k_ref, v_ref, ag_ref, bm_ref, o_ref, qs0, qs1, kp, vp):
        head0 = _attn_prologue(q_ref, k_ref, v_ref, qs0, qs1, kp, vp, SEQ)

        def group(g, carry):
            r0 = pl.multiple_of(g * QG, QG)
            kw = kp[pl.ds(r0, KW), :]
            vw = vp[pl.ds(r0, KW), :]
            variant = jnp.minimum(g, BIAS_VARIANTS - 1)
            outs = []
            for hh, qs in enumerate((qs0, qs1)):
                e, inv = _attn_weights(qs[pl.ds(r0, QG), :], kw, bm_ref[variant, hh])
                outs.append(jnp.dot(e.astype(bf16), vw, preferred_element_type=f32) * inv)
            o = jnp.where(head0, outs[0], outs[1])
            ag = ag_ref[pl.ds(r0, QG), :]
            o_ref[pl.ds(r0, QG), :] = (o * (ag * _sig(ag))).astype(bf16)
            return carry

        lax.fori_loop(0, SEQ // QG, group, 0, unroll=4)

    def zs(col):
        return pl.BlockSpec((SEQ, W2), lambda b, hp: (b, col // W2 + hp))

    return pl.pallas_call(
        body, grid=(BL, WC // W2),
        in_specs=[zs(COL_Q), zs(COL_K), zs(COL_V), zs(COL_AG),
                  pl.BlockSpec((BIAS_VARIANTS, 2, QG, KW), lambda b, hp: (0, hp, 0, 0))],
        out_specs=pl.BlockSpec((SEQ, W2), lambda b, hp: (b, hp)),
        out_shape=jax.ShapeDtypeStruct((T, WC), bf16),
        scratch_shapes=[pltpu.VMEM((SEQ, W2), bf16), pltpu.VMEM((SEQ, W2), bf16),
                        pltpu.VMEM((SEQ + KEY_PAD, W2), bf16), pltpu.VMEM((SEQ + KEY_PAD, W2), bf16)],
        name="attn_fwd", compiler_params=_params(("parallel", "parallel")))(z, z, z, z, bm)


def _attn_bwd(z, dat, dz, bm, BL, SEQ):
    W2 = 2 * HEAD_DIM

    def body(q_ref, k_ref, v_ref, ag_ref, dat_ref, dzin_ref, bm_ref, dz_ref, dbm_ref, qs0, qs1, kp, vp, dka, dva):
        @pl.when(pl.program_id(1) == 0)
        def _():
            dbm_ref[...] = jnp.zeros_like(dbm_ref)

        head0 = _attn_prologue(q_ref, k_ref, v_ref, qs0, qs1, kp, vp, SEQ)
        dka[...] = jnp.zeros_like(dka)
        dva[...] = jnp.zeros_like(dva)

        def group(g, carry):
            r0 = pl.multiple_of(g * QG, QG)
            kw = kp[pl.ds(r0, KW), :]
            vw = vp[pl.ds(r0, KW), :]
            variant = jnp.minimum(g, BIAS_VARIANTS - 1)
            ag = ag_ref[pl.ds(r0, QG), :]
            do = dat_ref[pl.ds(r0, QG), :] * (ag * _sig(ag))
            outs, dqs = [], []
            for hh, qs in enumerate((qs0, qs1)):
                qh = qs[pl.ds(r0, QG), :]
                e, inv = _attn_weights(qh, kw, bm_ref[variant, hh])
                eb = e.astype(bf16)
                outs.append(jnp.dot(eb, vw, preferred_element_type=f32) * inv)
                doh = (jnp.where(head0, do, 0.0) if hh == 0 else jnp.where(head0, 0.0, do)) * inv
                doh = doh.astype(bf16)
                dp = lax.dot_general(doh, vw, (((1,), (1,)), ((), ())), preferred_element_type=f32)
                ds_ = e * (dp - jnp.sum(e * dp, axis=-1, keepdims=True) * inv)
                dbm_ref[hh] += ds_
                dsb = ds_.astype(bf16)
                dqs.append(jnp.dot(dsb, kw, preferred_element_type=f32))
                dka[pl.ds(r0, KW), :] += lax.dot_general(dsb, qh, (((0,), (0,)), ((), ())), preferred_element_type=f32)
                dva[pl.ds(r0, KW), :] += lax.dot_general(eb, doh, (((0,), (0,)), ((), ())), preferred_element_type=f32)
            o = jnp.where(head0, outs[0], outs[1])
            dq = jnp.where(head0, dqs[0], dqs[1]) * (HEAD_DIM ** -0.5)
            dz_ref[0, pl.ds(r0, QG), :] = dq.astype(bf16)
            dz_ref[3, pl.ds(r0, QG), :] = (dat_ref[pl.ds(r0, QG), :] * o * _dsilu(ag, _sig(ag))).astype(bf16)
            return carry

        lax.fori_loop(0, SEQ // QG, group, 0, unroll=4)

        def flush(g, carry):
            r0 = pl.multiple_of(g * QG, QG)
            dz_ref[1, pl.ds(r0, QG), :] = dka[pl.ds(r0 + KEY_PAD, QG), :].astype(bf16)
            dz_ref[2, pl.ds(r0, QG), :] = dva[pl.ds(r0 + KEY_PAD, QG), :].astype(bf16)
            return carry

        lax.fori_loop(0, SEQ // QG, flush, 0)

    def zs(col):
        return pl.BlockSpec((SEQ, W2), lambda hp, b: (b, col // W2 + hp))

    return pl.pallas_call(
        body, grid=(WC // W2, BL),
        in_specs=[zs(COL_Q), zs(COL_K), zs(COL_V), zs(COL_AG), pl.BlockSpec((SEQ, W2), lambda hp, b: (b, hp)), ANY,
                  pl.BlockSpec((BIAS_VARIANTS, 2, QG, KW), lambda hp, b: (0, hp, 0, 0))],
        out_specs=[pl.BlockSpec((4, SEQ, W2), lambda hp, b: (DZ_ATTN // 4, b, hp)),
                   pl.BlockSpec((2, QG, KW), lambda hp, b: (hp, 0, 0))],
        out_shape=[jax.ShapeDtypeStruct(dz.shape, bf16), jax.ShapeDtypeStruct((8, QG, KW), f32)],
        scratch_shapes=[pltpu.VMEM((SEQ, W2), bf16), pltpu.VMEM((SEQ, W2), bf16),
                        pltpu.VMEM((SEQ + KEY_PAD, W2), bf16), pltpu.VMEM((SEQ + KEY_PAD, W2), bf16),
                        pltpu.VMEM((SEQ + KEY_PAD, W2), f32), pltpu.VMEM((SEQ + KEY_PAD, W2), f32)],
        input_output_aliases={5: 0}, name="attn_bwd",
        compiler_params=_params(("parallel", "arbitrary")))(z, z, z, z, dat, dz, bm)


BIAS_TOP = KEY_PAD + MAX_REL + QG - 1


def _bias_matrix(table):
    n = 2 * MAX_REL
    wd = QG + KW
    e = jnp.concatenate([jnp.broadcast_to(table[:, n:], (8, BIAS_TOP - n + 1)), table[:, n - 1:BIAS_TOP - wd + 1:-1],
                         jnp.zeros((8, 1), f32)], axis=1)
    flat = jnp.broadcast_to(e[:, None, :], (8, QG, wd)).reshape(8, QG * wd)
    skew = flat[:, :QG * (wd - 1)].reshape(8, QG, wd - 1)
    vals = skew[:, :, QG - 1:QG - 1 + KW]
    r = np.arange(QG)[:, None] // CHUNK
    j = np.arange(KW)[None, :]
    band = (j // CHUNK >= r) & (j // CHUNK <= r + LEFT_CHUNKS)
    keep = np.stack([band & (j >= KEY_PAD - v * QG) for v in range(BIAS_VARIANTS)])
    return jnp.where(jnp.asarray(keep)[:, None], vals[None], NEG_INF)


def _bias_fold(dbm):
    wd = QG + KW
    placed = jnp.pad(dbm, ((0, 0), (0, 0), (QG - 1, 0))).reshape(8, QG * (wd - 1))
    return jnp.pad(placed, ((0, 0), (0, QG))).reshape(8, QG, wd)


def _bias_colsum(folded):
    width = folded.shape[2]

    def body(x_ref, o_ref):
        for h in range(8):
            o_ref[pl.ds(h, 1), :] = _colsum(x_ref[h])

    return pl.pallas_call(body, out_shape=jax.ShapeDtypeStruct((8, width), f32), name="bias_colsum",
                          compiler_params=_params())(folded)


def _bias_table_grad(colsum):
    n = 2 * MAX_REL
    wd = QG + KW
    clipped = jnp.sum(colsum[:, :BIAS_TOP - n + 1], axis=1, keepdims=True)
    return jnp.concatenate([jnp.zeros((8, BIAS_TOP - wd + 2), f32), colsum[:, wd - 2:BIAS_TOP - n:-1], clipped], axis=1)


GATE_SPAN = 3 * WC


def _gate_specs(tm):
    return [pl.BlockSpec((tm, GATE_SPAN), lambda i: (i, COL_GM // GATE_SPAN)),
            pl.BlockSpec((tm, GATE_SPAN), lambda i: (i, COL_GM // GATE_SPAN + 1))]


def _gate_block(ga_ref, gb_ref, branch, half):
    k = 2 * branch + half
    ref, k = (ga_ref, k) if k < 3 else (gb_ref, k - 3)
    return _sig(ref[:, k * WC:(k + 1) * WC])


def _resident(shape):
    return pl.BlockSpec(shape, lambda i: (0,) * len(shape), pipeline_mode=pl.Buffered(1))


def _tail_fwd(z, acts, x2, lw):
    T = z.shape[0]
    tm = 256

    def body(cv_ref, at_ref, pv_ref, ga_ref, gb_ref, x_ref, wc_ref, wa_ref, wp_ref, wo_ref, g_ref,
             out_ref, merged_ref, y_ref):
        ys = [jnp.dot(a[...], w[...], preferred_element_type=f32)
              for a, w in ((cv_ref, wc_ref), (at_ref, wa_ref), (pv_ref, wp_ref))]
        halves = []
        for half in range(2):
            cols = slice(half * WC, (half + 1) * WC)
            halves.append(sum(_gate_block(ga_ref, gb_ref, br, half) * ys[br][:, cols] for br in range(3)))
        merged = jnp.concatenate(halves, axis=1).astype(bf16)
        merged_ref[...] = merged
        y = jnp.dot(merged, wo_ref[...], preferred_element_type=f32)
        y_ref[...] = y
        r = lax.rsqrt(jnp.mean(y * y, axis=-1, keepdims=True) + EPS)
        out_ref[...] = x_ref[...] + (y * r) * g_ref[...]

    act = pl.BlockSpec((tm, WC), lambda i: (i, 0))
    row = pl.BlockSpec((tm, D), lambda i: (i, 0))
    return pl.pallas_call(
        body, grid=(T // tm,),
        in_specs=[act, act, act] + _gate_specs(tm) + [row, _resident((WC, D)), _resident((WC, D)), _resident((WC, D)),
                                                      _resident((D, D)), _resident((1, D))],
        out_specs=[row, row, row],
        out_shape=[jax.ShapeDtypeStruct((T, D), f32), jax.ShapeDtypeStruct((T, D), bf16), jax.ShapeDtypeStruct((T, D), f32)],
        name="tail_fwd", compiler_params=_params(("parallel",)))(
            *acts, z, z, x2, lw["w_conv_out"], lw["w_attn_out"], lw["w_pool_out"], lw["w_out"], lw["post_g"])


def _tail_bwd(z, dout, y, acts, lw):
    T = z.shape[0]
    tm = 256
    nt = (((1,), (1,)), ((), ()))

    def body(d_ref, y_ref, cv_ref, at_ref, pv_ref, ga_ref, gb_ref, wc_ref, wa_ref, wp_ref, wo_ref, g_ref,
             dy_ref, dyc_ref, dya_ref, dyp_ref, dz_ref, dcv_ref, dat_ref, dpv_ref, dg_ref):
        y = y_ref[...]
        d = d_ref[...]
        r = lax.rsqrt(jnp.mean(y * y, axis=-1, keepdims=True) + EPS)
        yn = y * r
        dyn = d * g_ref[...]
        dy = (r * (dyn - yn * jnp.mean(dyn * yn, axis=-1, keepdims=True))).astype(bf16)
        dy_ref[...] = dy

        @pl.when(pl.program_id(0) == 0)
        def _():
            dg_ref[...] = jnp.zeros_like(dg_ref)

        dg_ref[...] += _colsum(d * yn)
        dmerged = lax.dot_general(dy, wo_ref[...], nt, preferred_element_type=f32)
        for br, (a_ref, w_ref, dyb_ref, da_ref) in enumerate(((cv_ref, wc_ref, dyc_ref, dcv_ref),
                                                               (at_ref, wa_ref, dya_ref, dat_ref),
                                                               (pv_ref, wp_ref, dyp_ref, dpv_ref))):
            yb = jnp.dot(a_ref[...], w_ref[...], preferred_element_type=f32)
            halves = []
            for half in range(2):
                cols = slice(half * WC, (half + 1) * WC)
                s = _gate_block(ga_ref, gb_ref, br, half)
                dm = dmerged[:, cols]
                halves.append((dm * s).astype(bf16))
                dz_ref[2 * br + half] = (dm * yb[:, cols] * s * (1.0 - s)).astype(bf16)
            dyb = jnp.concatenate(halves, axis=1)
            dyb_ref[...] = dyb
            da_ref[...] = lax.dot_general(dyb, w_ref[...], nt, preferred_element_type=f32)

    act = pl.BlockSpec((tm, WC), lambda i: (i, 0))
    row = pl.BlockSpec((tm, D), lambda i: (i, 0))
    vec = pl.BlockSpec((1, D), lambda i: (0, 0))
    return pl.pallas_call(
        body, grid=(T // tm,),
        in_specs=[row, row, act, act, act] + _gate_specs(tm) + [_resident((WC, D)), _resident((WC, D)), _resident((WC, D)),
                                                                _resident((D, D)), _resident((1, D))],
        out_specs=[row, row, row, row, pl.BlockSpec((6, tm, WC), lambda i: (DZ_GM // 6, i, 0)), act, act, act, vec],
        out_shape=[jax.ShapeDtypeStruct((T, D), bf16)] * 4 + [jax.ShapeDtypeStruct((DZ_BLOCKS, T, WC), bf16)]
        + [jax.ShapeDtypeStruct((T, WC), f32)] * 3 + [jax.ShapeDtypeStruct((1, D), f32)],
        name="tail_bwd", compiler_params=_params(("arbitrary",)))(
            dout, y, *acts, z, z, lw["w_conv_out"], lw["w_attn_out"], lw["w_pool_out"], lw["w_out"], lw["post_g"])


def _adamw(name, g, w, m, v):
    R, C = w.shape
    tr = R
    for cand in (512, 256, 248, 128, 64, 32, 16, 8):
        if R % cand == 0 and cand * C * 4 <= 2 * 1024 * 1024:
            tr = cand
            break
    c1 = 1.0 - ADAM_B1
    c2 = 1.0 - ADAM_B2
    bc1 = 1.0 - ADAM_B1 ** ADAM_STEP
    bc2 = 1.0 - ADAM_B2 ** ADAM_STEP

    def body(g_ref, w_ref, m_ref, v_ref, d_ref, nm_ref, nv_ref):
        g_ = g_ref[...]
        nm = ADAM_B1 * m_ref[...] + c1 * g_
        nv = ADAM_B2 * v_ref[...] + c2 * (g_ * g_)
        nm_ref[...] = nm
        nv_ref[...] = nv
        d_ref[...] = -ADAM_LR * ((nm / bc1) / (jnp.sqrt(nv / bc2) + ADAM_EPS) + ADAM_WD * w_ref[...])

    spec = pl.BlockSpec((tr, C), lambda i: (i, 0))
    return pl.pallas_call(
        body, grid=(R // tr,), in_specs=[spec] * 4, out_specs=[spec] * 3,
        out_shape=[jax.ShapeDtypeStruct((R, C), f32)] * 3, name=name,
        compiler_params=_params(("parallel",)))(g, w, m, v)


def _sum_slots(name, parts):
    _, R, C = parts.shape
    tr = R
    for cand in (256, 128, 64, 32, 16, 8):
        if R % cand == 0 and cand * C * 4 * N_DEV <= 8 * 1024 * 1024:
            tr = cand
            break

    def body(p_ref, o_ref):
        acc = p_ref[0].astype(f32)
        for s in range(1, N_DEV):
            acc = acc + p_ref[s].astype(f32)
        o_ref[...] = acc

    return pl.pallas_call(
        body, grid=(R // tr,), in_specs=[pl.BlockSpec((N_DEV, tr, C), lambda i: (0, i, 0))],
        out_specs=pl.BlockSpec((tr, C), lambda i: (i, 0)), out_shape=jax.ShapeDtypeStruct((R, C), f32),
        name=name, compiler_params=_params(("parallel",)))(parts)


def _row_tile(rows, row_bytes, budget):
    for cand in (512, 256, 128, 64, 32, 16):
        if rows % cand == 0 and cand * row_bytes <= budget:
            return cand
    return rows


def _pair_sum(core, g, theirs):
    R2, C4 = theirs.shape
    tr = _row_tile(R2, C4 * 2, 2 * 1024 * 1024)
    nb = R2 // tr

    def body(core_ref, g_ref, t_ref, o_ref):
        o_ref[...] = (g_ref[...].astype(f32) + t_ref[...].astype(f32)).astype(bf16)

    return pl.pallas_call(
        body,
        grid_spec=pltpu.PrefetchScalarGridSpec(
            num_scalar_prefetch=1, grid=(nb,),
            in_specs=[pl.BlockSpec((tr, C4), lambda i, core_ref: (core_ref[0] * nb + i, 0)),
                      pl.BlockSpec((tr, C4), lambda i, core_ref: (i, 0))],
            out_specs=pl.BlockSpec((tr, C4), lambda i, core_ref: (i, 0))),
        out_shape=jax.ShapeDtypeStruct((R2, C4), bf16), name="pair_sum",
        compiler_params=_params(("parallel",)))(core, g, theirs)


def _chip_sum(chip, mine, others):
    _, R2, C = others.shape
    tr = _row_tile(R2, C * 4, 1024 * 1024)

    def body(chip_ref, m_ref, o_ref, out_ref):
        acc = m_ref[...].astype(f32)
        for s in range(N_CHIPS - 1):
            acc = acc + o_ref[s].astype(f32)
        out_ref[...] = acc

    return pl.pallas_call(
        body,
        grid_spec=pltpu.PrefetchScalarGridSpec(
            num_scalar_prefetch=1, grid=(R2 // tr,),
            in_specs=[pl.BlockSpec((tr, C), lambda i, chip_ref: (i, chip_ref[0])),
                      pl.BlockSpec((N_CHIPS - 1, tr, C), lambda i, chip_ref: (0, i, 0))],
            out_specs=pl.BlockSpec((tr, C), lambda i, chip_ref: (i, 0))),
        out_shape=jax.ShapeDtypeStruct((R2, C), f32), name="chip_sum",
        compiler_params=_params(("parallel",)))(chip, mine, others)


def _place():
    x, y, c = lax.axis_index("x"), lax.axis_index("y"), lax.axis_index("c")
    return x, y, c


def _flip(v, bit):
    return 1 - v if bit else v


CHIP_FLIPS = ((1, 0), (0, 1), (1, 1))


class _Sems:
    def __init__(self, send, recv):
        self.send, self.recv = send, recv
        self.pairs = 0

    def pair(self):
        k = self.pairs
        self.pairs += 1
        return self.send.at[k], self.recv.at[k]


def _remote(src, dst, lands, sems, to):
    s, r = sems.pair()
    copy = pltpu.make_async_remote_copy(src_ref=src, dst_ref=dst, send_sem=s, recv_sem=r, device_id=to, device_id_type=MESH)
    wait = pltpu.make_async_remote_copy(src_ref=lands, dst_ref=lands, send_sem=s, recv_sem=r, device_id=to, device_id_type=MESH)
    return copy, wait


def _exchange(name, build, srcs, lands, n_remote):
    n_s, n_l = len(srcs), len(lands)

    def body(*refs):
        send, recv = refs[n_s + 2 * n_l:]
        remotes, recvs = build(refs[:n_s], refs[n_s + n_l:n_s + 2 * n_l], _Sems(send, recv))
        for cp in remotes:
            cp.start()
        for rv in recvs:
            rv.wait_recv()
        for cp in remotes:
            cp.wait_send()

    return pl.pallas_call(
        body, in_specs=[ANY] * (n_s + n_l), out_specs=[ANY] * n_l,
        out_shape=[jax.ShapeDtypeStruct(t.shape, t.dtype) for t in lands],
        scratch_shapes=[pltpu.SemaphoreType.DMA((n_remote,)), pltpu.SemaphoreType.DMA((n_remote,))],
        input_output_aliases={n_s + i: i for i in range(n_l)}, name=name)(*srcs, *lands)


HBM = pl.BlockSpec(memory_space=pltpu.HBM)
SEMS = pl.BlockSpec(memory_space=pltpu.SEMAPHORE)
DATAFLOW = pltpu.SideEffectType.DATAFLOW_SIDE_EFFECTING


def _start(name, build, srcs, lands, n_remote, after):
    n_s, n_l = len(srcs), len(lands)

    def body(*refs):
        send, recv = refs[n_s + n_l + 1], refs[n_s + n_l + 2]
        remotes, _ = build(refs[:n_s], refs[n_s:n_s + n_l], _Sems(send, recv))
        for cp in remotes:
            cp.start()
        refs[-1][...] = jnp.zeros((8, 128), f32)

    arrays = [pltpu.with_memory_space_constraint(a, pltpu.HBM) for a in (*srcs, *lands)]
    out = pl.pallas_call(
        body, name=name, in_specs=[HBM] * (n_s + n_l) + [ANY],
        out_specs=(SEMS, SEMS, *[HBM] * (n_s + n_l), pl.BlockSpec(memory_space=pltpu.VMEM)),
        out_shape=(pltpu.SemaphoreType.DMA((n_remote,)), pltpu.SemaphoreType.DMA((n_remote,)),
                   *[pltpu.HBM(a.shape, a.dtype) for a in arrays], jax.ShapeDtypeStruct((8, 128), f32)),
        input_output_aliases={i: 2 + i for i in range(n_s + n_l)},
        compiler_params=pltpu.CompilerParams(has_side_effects=DATAFLOW))(*arrays, after)
    return dict(name=name, build=build, sems=out[:2], srcs=out[2:2 + n_s], lands=out[2 + n_s:2 + n_s + n_l], token=out[-1])


def _wait(started, after):
    srcs, lands, build = started["srcs"], started["lands"], started["build"]
    n_s, n_l = len(srcs), len(lands)

    def body(*refs):
        send, recv = refs[n_s + n_l], refs[n_s + n_l + 1]
        remotes, recvs = build(refs[:n_s], refs[n_s:n_s + n_l], _Sems(send, recv))
        for rv in recvs:
            rv.wait_recv()
        for cp in remotes:
            cp.wait_send()

    out = pl.pallas_call(
        body, name=started["name"] + "_wait", in_specs=[HBM] * (n_s + n_l) + [SEMS, SEMS, ANY],
        out_specs=[HBM] * (n_s + n_l), out_shape=[pltpu.HBM(a.shape, a.dtype) for a in (*srcs, *lands)],
        input_output_aliases={i: i for i in range(n_s + n_l)},
        compiler_params=pltpu.CompilerParams(has_side_effects=DATAFLOW))(*srcs, *lands, *started["sems"], after)
    return out[:n_s], out[n_s:]


def _gather_plans(n_split, n_all):
    def over_ici(src, land, sems):
        x, y, c = _place()
        chip = 2 * x + y
        remotes, recvs = [], []
        for a in range(n_all):
            for fx, fy in CHIP_FLIPS:
                px, py = _flip(x, fx), _flip(y, fy)
                if a < n_split:
                    r2 = src[a].shape[0] // 2
                    rows = pl.ds(c * r2, r2)
                    cp, rv = _remote(src[a].at[rows], land[a].at[chip, rows], land[a].at[2 * px + py, rows], sems, (px, py, c))
                else:
                    cp, rv = _remote(src[a], land[a].at[chip], land[a].at[2 * px + py], sems, (px, py, c))
                remotes.append(cp)
                recvs.append(rv)
        return remotes, recvs

    def over_d2d(src, land, sems):
        x, y, c = _place()
        remotes, recvs = [], []
        for a in range(n_split):
            r2 = land[a].shape[1] // 2
            for fx, fy in CHIP_FLIPS:
                owner = 2 * _flip(x, fx) + _flip(y, fy)
                mine = land[a].at[owner, pl.ds(c * r2, r2)]
                cp, rv = _remote(mine, mine, land[a].at[owner, pl.ds((1 - c) * r2, r2)], sems, (x, y, 1 - c))
                remotes.append(cp)
                recvs.append(rv)
        return remotes, recvs

    return over_ici, over_d2d


def _gather_begin(tag, shards, n_split, after):
    over_ici, _ = _gather_plans(n_split, len(shards))
    lands = [lax.empty((N_CHIPS,) + s.shape, s.dtype) for s in shards]
    return _start("gather_ici_" + tag, over_ici, shards, lands, 3 * len(shards), after)


def _gather_end(started, shards, n_split, after):
    _, over_d2d = _gather_plans(n_split, len(shards))
    lands = _exchange("gather_d2d", over_d2d, [], _wait(started, after)[1], 3 * n_split)
    chip = 2 * lax.axis_index("x") + lax.axis_index("y")
    return [lax.dynamic_update_slice_in_dim(g, s[None], chip, axis=0) for g, s in zip(lands, shards)]


def _reduce_plans(n):
    def to_sibling(src, land, sems):
        x, y, c = _place()
        remotes, recvs = [], []
        for a in range(n):
            r2 = src[a].shape[0] // 2
            cp, rv = _remote(src[a].at[pl.ds((1 - c) * r2, r2), :], land[a], land[a], sems, (x, y, 1 - c))
            remotes.append(cp)
            recvs.append(rv)
        return remotes, recvs

    def across_chips(src, land, sems):
        x, y, c = _place()
        remotes, recvs = [], []
        for a in range(n):
            cw = src[a].shape[1] // N_CHIPS
            for k, (fx, fy) in enumerate(CHIP_FLIPS):
                px, py = _flip(x, fx), _flip(y, fy)
                cp, rv = _remote(src[a].at[:, pl.ds((2 * px + py) * cw, cw)], land[a].at[k], land[a].at[k], sems, (px, py, c))
                remotes.append(cp)
                recvs.append(rv)
        return remotes, recvs

    def share(src, land, sems):
        x, y, c = _place()
        remotes, recvs = [], []
        for a in range(n):
            cp, rv = _remote(src[a], land[a], land[a], sems, (x, y, 1 - c))
            remotes.append(cp)
            recvs.append(rv)
        return remotes, recvs

    return to_sibling, across_chips, share


def _reduce_begin(grads):
    n = len(grads)
    to_sibling, across_chips, _ = _reduce_plans(n)
    core = lax.axis_index("c").reshape(1).astype(jnp.int32)
    theirs = _exchange("reduce_pair", to_sibling, grads,
                       [lax.empty((g.shape[0] // 2, g.shape[1]), bf16) for g in grads], n)
    pair = [_pair_sum(core, g, t) for g, t in zip(grads, theirs)]
    lands = [lax.empty((N_CHIPS - 1, g.shape[0] // 2, g.shape[1] // N_CHIPS), bf16) for g in grads]
    return _start("reduce_chips", across_chips, pair, lands, 3 * n, pair[0])


def _reduce_end(started, after):
    x, y, c = _place()
    chip = (2 * x + y).reshape(1).astype(jnp.int32)
    pair, others = _wait(started, after)
    _, _, share = _reduce_plans(len(pair))
    mine = [_chip_sum(chip, p, o) for p, o in zip(pair, others)]
    sibs = _exchange("reduce_share", share, mine, [lax.empty(h.shape, f32) for h in mine], len(mine))
    return [jnp.where(c == 0, jnp.concatenate([h, s], axis=0), jnp.concatenate([s, h], axis=0))
            for h, s in zip(mine, sibs)]


def _to_all(src, land, sems):
    x, y, c = _place()
    me = 4 * x + 2 * y + c
    remotes, recvs = [], []
    for k in range(1, N_DEV):
        px, py, pc = _flip(x, (k >> 2) & 1), _flip(y, (k >> 1) & 1), _flip(c, k & 1)
        cp, rv = _remote(src[0], land[0].at[me], land[0].at[4 * px + 2 * py + pc], sems, (px, py, pc))
        remotes.append(cp)
        recvs.append(rv)
    return remotes, recvs


def _gather_small_begin(packed):
    return _start("gather_small", _to_all, [packed], [lax.empty((N_DEV,) + packed.shape, f32)], N_DEV - 1, packed)


def _gather_small_end(started, after):
    (packed,), (others,) = _wait(started, after)
    x, y, c = _place()
    return lax.dynamic_update_slice_in_dim(others, packed[None], 4 * x + 2 * y + c, axis=0)


def _gather_all(packed):
    others = _exchange("gather_all", _to_all, [packed], [lax.empty((N_DEV,) + packed.shape, f32)], N_DEV - 1)[0]
    x, y, c = _place()
    return lax.dynamic_update_slice_in_dim(others, packed[None], 4 * x + 2 * y + c, axis=0)


def _rows8(v):
    return jnp.pad(v[None, :], ((0, 7), (0, 0)))


def _vec_rows(vs):
    return jnp.pad(jnp.stack(vs), ((0, 8 - len(vs)), (0, 0)))


SMALL_ROWS = 224


def _pack_small(conv_vec, conv_dw, pool_vec, pool_w, pre_g, post_g, rel):
    return jnp.concatenate([
        conv_vec, conv_dw, pool_vec, pool_w.reshape(GD, WC),
        _rows8(pre_g).reshape(16, WC), _rows8(post_g).reshape(16, WC),
        jnp.pad(rel, ((0, 0), (0, D - rel.shape[1]))).reshape(16, WC)], axis=0)


def _unpack_small(p):
    conv_vec, pool_vec = p[0:8], p[40:48]
    return dict(
        conv_dw_b=conv_vec[0], conv_ln_g=conv_vec[1], conv_ln_b=conv_vec[2], conv_dw=p[8:8 + CONV_K],
        pool_b=pool_vec[0].reshape(4, GD), pool_scale=pool_vec[1], pool_w=p[48:176].reshape(4, GD, GD),
        pre_norm_g=p[176:192].reshape(8, D)[0], post_norm_g=p[192:208].reshape(8, D)[0],
        rel_bias=p[208:224].reshape(8, D)[:, :2 * MAX_REL + 1])


def _in_proj(h, w_in, after=None):
    return _mm("mm_in", h, w_in, "nn", h.shape[0], NCOL, D, 1024, 1536, D, f32, after=after)


def _layer_fwd(x2, ht, z, lw, BL, SEQ):
    cv, u1 = _conv_fwd(z, lw["dw32"], lw["cvec"], BL, SEQ)
    at = _attn_fwd(z, lw["bm"], BL, SEQ)
    pv = _pool_fwd(z, lw["pw"], lw["pvec"], BL, SEQ)
    out, merged, y = _tail_fwd(z, (cv, at, pv), x2, lw)
    return out, dict(x=x2, ht=ht, z=z, u1=u1, acts=(cv, at, pv), merged=merged, y=y)


def _layer_bwd(dout, sv, lw, BL, SEQ, meanwhile=None):
    T = BL * SEQ
    tail = _tail_bwd(sv["z"], dout, sv["y"], sv["acts"], lw)
    dy, dys, dz, dacts, dpost = tail[0], tail[1:4], tail[4], tail[5:8], tail[8]
    dw_out_t = _mm("mm_dw_out", dy, sv["merged"], "tn", D, D, T, D, D, 1024, bf16)
    dws = [_mm("mm_dw_branch", act, dyb, "tn", WC, D, T, WC, D, 1024, bf16) for act, dyb in zip(sv["acts"], dys)]
    if meanwhile is not None:
        meanwhile(dws[2])
    dz, ddw, dcvec = _conv_bwd(sv["z"], sv["u1"], dacts[0], dz, lw["dw32"], lw["cvec"], BL, SEQ)
    dz, dbm = _attn_bwd(sv["z"], dacts[1], dz, lw["bm"], BL, SEQ)
    dz, dpw, dpvec = _pool_bwd(sv["z"], dacts[2], dz, lw["pw"], lw["pvec"], BL, SEQ)
    drel = _bias_table_grad(_bias_colsum(_bias_fold(dbm)))
    small_gather = _gather_small_begin(_pack_small(dcvec, ddw, dpvec, dpw, jnp.zeros((D,), f32), dpost[0], drel))
    dw_in = _mm_dw_in(sv["ht"], dz, small_gather["token"])
    reduction = _reduce_begin([dw_in, dws[0], dws[1], dws[2], dw_out_t])
    dh = _mm_dh(dz, lw["w_in"], reduction["token"])
    dx, dpre = _pre_bwd(dh, sv["x"], lw["pre_g"], dout)
    return dx, reduction, small_gather, dpre


BIG = ("w_in", "w_conv_out", "w_attn_out", "w_pool_out", "w_out")
PRE_ROWS = slice(176, 192)


def _layer_shards(w, l):
    return [w[k][l].astype(bf16) for k in BIG] + [w["conv_dw"][l]]


def _side_by_side(g):
    return jnp.transpose(g, (1, 0, 2)).reshape(g.shape[1], N_CHIPS * g.shape[2])


def _layer_weights(w_in, gathered, w, l):
    lw = {k: _side_by_side(g) for k, g in zip(BIG[1:4], gathered[:3])}
    lw["w_in"] = w_in
    lw["w_out"] = gathered[3].reshape(D, D)
    lw["pre_g"] = w["pre_norm_g"][l][None]
    lw["post_g"] = w["post_norm_g"][l][None]
    lw["dw32"] = jnp.pad(_side_by_side(gathered[4]), ((0, 32 - CONV_K), (0, 0)))
    lw["cvec"] = _vec_rows([w["conv_dw_b"][l], w["conv_ln_g"][l], w["conv_ln_b"][l]])
    lw["bm"] = _bias_matrix(w["rel_bias"][l])
    lw["pw"] = w["pool_w"][l].astype(bf16)
    lw["pvec"] = _vec_rows([w["pool_b"][l].reshape(WC), w["pool_scale"][l]])
    return lw


SMALL = ("pre_norm_g", "post_norm_g", "conv_dw_b", "conv_ln_g", "conv_ln_b", "rel_bias", "pool_w", "pool_b", "pool_scale")
ORDER = ("pre_norm_g", "post_norm_g", "w_in", "conv_dw", "conv_dw_b", "conv_ln_g", "conv_ln_b", "w_conv_out",
         "rel_bias", "w_attn_out", "pool_w", "pool_b", "pool_scale", "w_pool_out", "w_out")


def _pack_small_params(p):
    return jnp.concatenate([
        _pack_small(_vec_rows([p["conv_dw_b"][l], p["conv_ln_g"][l], p["conv_ln_b"][l]]), jnp.zeros((32, WC), f32),
                    _vec_rows([p["pool_b"][l].reshape(WC), p["pool_scale"][l]]), p["pool_w"][l],
                    p["pre_norm_g"][l], p["post_norm_g"][l], p["rel_bias"][l])
        for l in range(DEPTH)], axis=0)


def _unpack_small_params(packed):
    layers = [_unpack_small(packed[l * SMALL_ROWS:(l + 1) * SMALL_ROWS]) for l in range(DEPTH)]
    return {k: jnp.stack([layers[l][k] for l in range(DEPTH)]) for k in layers[0]}


def kernel(x, pre_norm_g, post_norm_g, w_in, conv_dw, conv_dw_b, conv_ln_g, conv_ln_b, w_conv_out, rel_bias, w_attn_out, pool_w, pool_b, pool_scale, w_pool_out, w_out, loss_target, m_pre_norm_g, m_post_norm_g, m_w_in, m_conv_dw, m_conv_dw_b, m_conv_ln_g, m_conv_ln_b, m_w_conv_out, m_rel_bias, m_w_attn_out, m_pool_w, m_pool_b, m_pool_scale, m_w_pool_out, m_w_out, v_pre_norm_g, v_post_norm_g, v_w_in, v_conv_dw, v_conv_dw_b, v_conv_ln_g, v_conv_ln_b, v_w_conv_out, v_rel_bias, v_w_attn_out, v_pool_w, v_pool_b, v_pool_scale, v_w_pool_out, v_w_out):
    BL, SEQ, _ = x.shape
    T = BL * SEQ
    w = dict(pre_norm_g=pre_norm_g, post_norm_g=post_norm_g, w_in=w_in, conv_dw=conv_dw, conv_dw_b=conv_dw_b,
             conv_ln_g=conv_ln_g, conv_ln_b=conv_ln_b, w_conv_out=w_conv_out, rel_bias=rel_bias, w_attn_out=w_attn_out,
             pool_w=pool_w, pool_b=pool_b, pool_scale=pool_scale, w_pool_out=w_pool_out, w_out=w_out)
    m = dict(pre_norm_g=m_pre_norm_g, post_norm_g=m_post_norm_g, w_in=m_w_in, conv_dw=m_conv_dw, conv_dw_b=m_conv_dw_b,
             conv_ln_g=m_conv_ln_g, conv_ln_b=m_conv_ln_b, w_conv_out=m_w_conv_out, rel_bias=m_rel_bias,
             w_attn_out=m_w_attn_out, pool_w=m_pool_w, pool_b=m_pool_b, pool_scale=m_pool_scale,
             w_pool_out=m_w_pool_out, w_out=m_w_out)
    v = dict(pre_norm_g=v_pre_norm_g, post_norm_g=v_post_norm_g, w_in=v_w_in, conv_dw=v_conv_dw, conv_dw_b=v_conv_dw_b,
             conv_ln_g=v_conv_ln_g, conv_ln_b=v_conv_ln_b, w_conv_out=v_w_conv_out, rel_bias=v_rel_bias,
             w_attn_out=v_w_attn_out, pool_w=v_pool_w, pool_b=v_pool_b, pool_scale=v_pool_scale,
             w_pool_out=v_w_pool_out, w_out=v_w_out)

    shards = [_layer_shards(w, l) for l in range(DEPTH)]
    x2 = x.reshape(T, D)
    h0, ht0 = _rms_pre(x2, pre_norm_g[0][None])
    first = _gather_begin("w_in0", shards[0][:1], 1, x2)
    w_in0 = _side_by_side(_gather_end(first, shards[0][:1], 1, ht0)[0])
    rest0 = _gather_begin("rest0", shards[0][1:], 4, w_in0)
    all1 = _gather_begin("layer1", shards[1], 5, rest0["token"])
    z0 = _in_proj(h0, w_in0, after=all1["token"])
    lw0 = _layer_weights(w_in0, _gather_end(rest0, shards[0][1:], 4, z0), w, 0)
    out0, saved0 = _layer_fwd(x2, ht0, z0, lw0, BL, SEQ)
    gathered1 = _gather_end(all1, shards[1], 5, out0)
    lw1 = _layer_weights(_side_by_side(gathered1[0]), gathered1[1:], w, 1)
    h1, ht1 = _rms_pre(out0, lw1["pre_g"])
    out1, saved1 = _layer_fwd(out0, ht1, _in_proj(h1, lw1["w_in"]), lw1, BL, SEQ)
    dout, sq = _loss_head(out1, loss_target.reshape(T, D))
    loss = lax.psum(0.5 * jnp.sum(sq) / float(D), ("x", "y", "c"))

    summed = [None] * DEPTH
    dx1, reduction1, small_gather1, dpre1 = _layer_bwd(dout, saved1, lw1, BL, SEQ)

    def finish_layer1(after):
        summed[1] = _reduce_end(reduction1, after)

    grad_x, reduction0, small_gather0, dpre0 = _layer_bwd(dx1, saved0, lw0, BL, SEQ, meanwhile=finish_layer1)
    summed[0] = _reduce_end(reduction0, grad_x)
    dpre = _sum_slots("sum_small", _gather_all(jnp.concatenate([_rows8(dpre0[0]), _rows8(dpre1[0])], axis=0)))
    gsmall = []
    for l, started in enumerate((small_gather0, small_gather1)):
        g = _sum_slots("sum_small", _gather_small_end(started, dpre))
        gsmall += [g[:PRE_ROWS.start], dpre[8 * l:8 * l + 8].reshape(16, WC), g[PRE_ROWS.stop:]]
    gsmall = jnp.concatenate(gsmall, axis=0)

    grads, deltas, new_m, new_v = {}, {}, {}, {}
    for i, k in enumerate(BIG):
        g = jnp.stack([summed[l][i] for l in range(DEPTH)])
        if k == "w_out":
            g = jnp.transpose(g, (0, 2, 1))
        grads[k] = g
        shape = w[k].shape
        flat2 = lambda a: a.reshape(shape[0] * shape[1], shape[2])
        d_, nm_, nv_ = _adamw("adamw_big", flat2(g), flat2(w[k]), flat2(m[k]), flat2(v[k]))
        deltas[k], new_m[k], new_v[k] = d_.reshape(shape), nm_.reshape(shape), nv_.reshape(shape)

    d_, nm_, nv_ = _adamw("adamw_small", gsmall, _pack_small_params(w), _pack_small_params(m), _pack_small_params(v))
    gs, ds, ms, vs = (_unpack_small_params(a) for a in (gsmall, d_, nm_, nv_))
    for k in SMALL:
        grads[k], deltas[k], new_m[k], new_v[k] = gs[k], ds[k], ms[k], vs[k]
    chip = 2 * lax.axis_index("x") + lax.axis_index("y")
    g_dw = lax.dynamic_slice_in_dim(gs["conv_dw"], chip * GD, GD, axis=2)
    flat2 = lambda a: a.reshape(DEPTH * CONV_K, GD)
    d_, nm_, nv_ = _adamw("adamw_conv_dw", flat2(g_dw), flat2(conv_dw), flat2(m["conv_dw"]), flat2(v["conv_dw"]))
    grads["conv_dw"] = g_dw
    deltas["conv_dw"], new_m["conv_dw"], new_v["conv_dw"] = (a.reshape(conv_dw.shape) for a in (d_, nm_, nv_))

    return (loss, grad_x.reshape(x.shape), *[grads[k] for k in ORDER], *[deltas[k] for k in ORDER],
            *[new_m[k] for k in ORDER], *[new_v[k] for k in ORDER])
```

```python
import numpy as np
import jax
import jax.numpy as jnp
from jax import lax
from jax.experimental import pallas as pl
from jax.experimental.pallas import tpu as pltpu

f32 = jnp.float32
bf16 = jnp.bfloat16

D = 1024
DEPTH = 2
WC = 512
HEAD_DIM = 64
CHUNK = 64
LEFT_CHUNKS = 8
KEY_PAD = LEFT_CHUNKS * CHUNK
MAX_REL = 256
CONV_K = 31
POOL_WINDOWS = (2, 4, 8, 16)
GD = 128
NCOL = 7680
EPS = 1e-6
NEG_INF = -1e30
COL_A, COL_B, COL_CG, COL_Q, COL_K, COL_V, COL_AG, COL_PI, COL_PG, COL_GM = (
    0, 512, 1024, 1536, 2048, 2560, 3072, 3584, 4096, 4608)

ADAM_LR = 0.001
ADAM_B1 = 0.9
ADAM_B2 = 0.999
ADAM_EPS = 1e-08
ADAM_WD = 0.01
ADAM_STEP = 10

QG = 256
KW = KEY_PAD + QG
BIAS_VARIANTS = KEY_PAD // QG + 1
CT = 128
HALO = 32
PHALO = 16
N_CHIPS = 4
N_DEV = 8
VMEM_LIMIT = 56 * 1024 * 1024
MESH = pl.DeviceIdType.MESH
ANY = pl.BlockSpec(memory_space=pl.ANY)

DZ_BLOCKS = 18
DZ_CONV, DZ_ATTN, DZ_POOL, DZ_GM = 0, 4, 8, 12


def _dz_block(c):
    return c + (c >= 3).astype(jnp.int32) + 2 * (c >= 9).astype(jnp.int32)


def _params(sem=None):
    return pltpu.CompilerParams(dimension_semantics=sem, vmem_limit_bytes=VMEM_LIMIT)


def _sig(x):
    return 1.0 / (1.0 + jnp.exp(-x))


def _dsilu(x, s):
    return s * (1.0 + x * (1.0 - s))


def _colsum(x):
    return jnp.sum(x, axis=0, keepdims=True)


def _rms_pre(x2, g):
    T = x2.shape[0]
    tm = 512

    def body(x_ref, g_ref, h_ref, ht_ref):
        x = x_ref[...]
        r = lax.rsqrt(jnp.mean(x * x, axis=-1, keepdims=True) + EPS)
        h = (x * r) * g_ref[...]
        h_ref[...] = h.astype(bf16)
        ht_ref[...] = h.T.astype(bf16)

    row = pl.BlockSpec((tm, D), lambda i: (i, 0))
    vec = pl.BlockSpec((1, D), lambda i: (0, 0))
    return pl.pallas_call(
        body, grid=(T // tm,), in_specs=[row, vec], out_specs=[row, pl.BlockSpec((D, tm), lambda i: (0, i))],
        out_shape=[jax.ShapeDtypeStruct((T, D), bf16), jax.ShapeDtypeStruct((D, T), bf16)], name="rms_pre",
        compiler_params=_params(("parallel",)))(x2, g)


def _pre_bwd(dh, x2, g, dout):
    T = x2.shape[0]
    tm = 512

    def body(dh_ref, x_ref, g_ref, d_ref, dx_ref, dg_ref):
        x = x_ref[...]
        dh_ = dh_ref[...]
        r = lax.rsqrt(jnp.mean(x * x, axis=-1, keepdims=True) + EPS)
        xn = x * r
        dxn = dh_ * g_ref[...]
        dx_ref[...] = r * (dxn - xn * jnp.mean(dxn * xn, axis=-1, keepdims=True)) + d_ref[...]

        @pl.when(pl.program_id(0) == 0)
        def _():
            dg_ref[...] = jnp.zeros_like(dg_ref)

        dg_ref[...] += _colsum(dh_ * xn)

    row = pl.BlockSpec((tm, D), lambda i: (i, 0))
    vec = pl.BlockSpec((1, D), lambda i: (0, 0))
    return pl.pallas_call(
        body, grid=(T // tm,), in_specs=[row, row, vec, row], out_specs=[row, vec],
        out_shape=[jax.ShapeDtypeStruct((T, D), f32), jax.ShapeDtypeStruct((1, D), f32)],
        name="pre_bwd", compiler_params=_params(("arbitrary",)))(dh, x2, g, dout)


def _mm(name, a, b, mode, m, n, k, tm, tn, tk, out_dtype, after=None):
    nk = k // tk
    assert m % tm == 0 and n % tn == 0 and k % tk == 0
    if mode == "nn":
        a_spec = pl.BlockSpec((tm, tk), lambda i, j, kk: (i, kk))
        b_spec = pl.BlockSpec((tk, tn), lambda i, j, kk: (kk, j))
        dn = (((1,), (0,)), ((), ()))
    elif mode == "nt":
        a_spec = pl.BlockSpec((tm, tk), lambda i, j, kk: (i, kk))
        b_spec = pl.BlockSpec((tn, tk), lambda i, j, kk: (j, kk))
        dn = (((1,), (1,)), ((), ()))
    else:
        a_spec = pl.BlockSpec((tk, tm), lambda i, j, kk: (kk, i))
        b_spec = pl.BlockSpec((tk, tn), lambda i, j, kk: (kk, j))
        dn = (((0,), (0,)), ((), ()))
    extra = [] if after is None else [after]

    def body(a_ref, b_ref, *rest):
        o_ref, acc_ref = rest[len(extra):]
        p = lax.dot_general(a_ref[...].astype(bf16), b_ref[...].astype(bf16), dn, preferred_element_type=f32)
        if nk == 1:
            o_ref[...] = p.astype(o_ref.dtype)
        else:
            kk = pl.program_id(2)

            @pl.when(kk == 0)
            def _():
                acc_ref[...] = p

            @pl.when(kk > 0)
            def _():
                acc_ref[...] += p

            @pl.when(kk == nk - 1)
            def _():
                o_ref[...] = acc_ref[...].astype(o_ref.dtype)

    acc_shape = (tm, tn) if nk > 1 else (8, 128)
    return pl.pallas_call(
        body, grid=(m // tm, n // tn, nk), in_specs=[a_spec, b_spec] + [ANY] * len(extra),
        out_specs=pl.BlockSpec((tm, tn), lambda i, j, kk: (i, j)),
        out_shape=jax.ShapeDtypeStruct((m, n), out_dtype),
        scratch_shapes=[pltpu.VMEM(acc_shape, f32)], name=name,
        compiler_params=_params(("parallel", "parallel", "arbitrary")))(a, b, *extra)


DZ_SPANS = ((DZ_CONV, 3), (DZ_ATTN, 4), (DZ_POOL, 2), (DZ_GM, 6))


def _mm_dh(dz, w_in, after):
    T = dz.shape[1]
    tm = 512

    def body(conv_ref, attn_ref, pool_ref, gm_ref, w_ref, after_ref, o_ref):
        acc = None
        col = 0
        for ref, (_, blocks) in zip((conv_ref, attn_ref, pool_ref, gm_ref), DZ_SPANS):
            for b in range(blocks):
                p = lax.dot_general(ref[b], w_ref[:, col * WC:(col + 1) * WC], (((1,), (1,)), ((), ())),
                                    preferred_element_type=f32)
                acc = p if acc is None else acc + p
                col += 1
        o_ref[...] = acc

    spans = [pl.BlockSpec((blocks, tm, WC), lambda i, first=first, blocks=blocks: (first // blocks, i, 0))
             for first, blocks in DZ_SPANS]
    return pl.pallas_call(
        body, grid=(T // tm,),
        in_specs=spans + [pl.BlockSpec((D, NCOL), lambda i: (0, 0), pipeline_mode=pl.Buffered(1)), ANY],
        out_specs=pl.BlockSpec((tm, D), lambda i: (i, 0)), out_shape=jax.ShapeDtypeStruct((T, D), f32),
        name="mm_dh", compiler_params=_params(("parallel",)))(dz, dz, dz, dz, w_in, after)


def _mm_dw_in(ht, dz, after):
    T = dz.shape[1]

    def body(ht_ref, dz_ref, after_ref, o_ref):
        o_ref[...] = jnp.dot(ht_ref[...], dz_ref[...], preferred_element_type=f32).astype(bf16)

    return pl.pallas_call(
        body, grid=(NCOL // WC,),
        in_specs=[pl.BlockSpec((D, T), lambda j: (0, 0), pipeline_mode=pl.Buffered(1)),
                  pl.BlockSpec((None, T, WC), lambda j: (_dz_block(j), 0, 0)), ANY],
        out_specs=pl.BlockSpec((D, WC), lambda j: (0, j)), out_shape=jax.ShapeDtypeStruct((D, NCOL), bf16),
        name="mm_dw_in", compiler_params=_params(("parallel",)))(ht, dz, after)


def _conv_delays():
    return [(8 * a + b, a, b) for b in range(8) for a in range(4) if 8 * a + b < CONV_K]


def _conv_rolls(win):
    return [win if b == 0 else pltpu.roll(win, b, axis=0) for b in range(8)]


def _conv_taps(rolled, dw_ref):
    acc = None
    for d, a, b in _conv_delays():
        term = rolled[b][HALO - 8 * a:HALO - 8 * a + CT, :] * dw_ref[pl.ds(CONV_K - 1 - d, 1), :]
        acc = term if acc is None else acc + term
    return acc


def _conv_fwd(z, dw32, cvec, BL, SEQ):
    T = BL * SEQ
    nct = SEQ // CT

    def body(a_ref, b_ref, cg_ref, dw_ref, vec_ref, o_ref, u1_ref, p_ref):
        p_ref[pl.ds(0, HALO), :] = jnp.zeros((HALO, WC), f32)

        def glu(c, carry):
            r0 = pl.multiple_of(c * CT, CT)
            p_ref[pl.ds(r0 + HALO, CT), :] = a_ref[pl.ds(r0, CT), :] * _sig(b_ref[pl.ds(r0, CT), :])
            return carry

        lax.fori_loop(0, nct, glu, 0)

        def step(c, carry):
            r0 = pl.multiple_of(c * CT, CT)
            u1 = _conv_taps(_conv_rolls(p_ref[pl.ds(r0, CT + HALO), :]), dw_ref) + vec_ref[0:1, :]
            u1_ref[pl.ds(r0, CT), :] = u1
            xc = u1 - jnp.mean(u1, axis=-1, keepdims=True)
            rs = lax.rsqrt(jnp.mean(xc * xc, axis=-1, keepdims=True) + EPS)
            u2 = (xc * rs) * vec_ref[1:2, :] + vec_ref[2:3, :]
            cg = cg_ref[pl.ds(r0, CT), :]
            o_ref[pl.ds(r0, CT), :] = ((u2 * _sig(u2)) * (cg * _sig(cg))).astype(bf16)
            return carry

        lax.fori_loop(0, nct, step, 0)

    def zs(col):
        return pl.BlockSpec((SEQ, WC), lambda b: (b, col // WC))

    seq = pl.BlockSpec((SEQ, WC), lambda b: (b, 0))
    return pl.pallas_call(
        body, grid=(BL,),
        in_specs=[zs(COL_A), zs(COL_B), zs(COL_CG), pl.BlockSpec((32, WC), lambda b: (0, 0)),
                  pl.BlockSpec((8, WC), lambda b: (0, 0))],
        out_specs=[seq, seq],
        out_shape=[jax.ShapeDtypeStruct((T, WC), bf16), jax.ShapeDtypeStruct((T, WC), f32)],
        scratch_shapes=[pltpu.VMEM((SEQ + HALO, WC), f32)], name="conv_fwd",
        compiler_params=_params(("parallel",)))(z, z, z, dw32, cvec)


def _conv_bwd(z, u1, dcv, dz, dw32, cvec, BL, SEQ):
    nct = SEQ // CT

    def body(a_ref, b_ref, cg_ref, u1_ref, dcv_ref, dzin_ref, dw_ref, vec_ref, dz_ref, ddw_ref, dvec_ref,
             p_ref, q_ref, taps_ref):
        @pl.when(pl.program_id(0) == 0)
        def _():
            ddw_ref[...] = jnp.zeros_like(ddw_ref)
            dvec_ref[...] = jnp.zeros_like(dvec_ref)

        p_ref[pl.ds(0, HALO), :] = jnp.zeros((HALO, WC), f32)
        q_ref[pl.ds(SEQ, HALO), :] = jnp.zeros((HALO, WC), f32)

        def glu(c, carry):
            r0 = pl.multiple_of(c * CT, CT)
            p_ref[pl.ds(r0 + HALO, CT), :] = a_ref[pl.ds(r0, CT), :] * _sig(b_ref[pl.ds(r0, CT), :])
            return carry

        lax.fori_loop(0, nct, glu, 0)

        def step(c, carry):
            r0 = pl.multiple_of(c * CT, CT)
            rolled = _conv_rolls(p_ref[pl.ds(r0, CT + HALO), :])
            u1 = u1_ref[pl.ds(r0, CT), :]
            xc = u1 - jnp.mean(u1, axis=-1, keepdims=True)
            rs = lax.rsqrt(jnp.mean(xc * xc, axis=-1, keepdims=True) + EPS)
            nrm = xc * rs
            u2 = nrm * vec_ref[1:2, :] + vec_ref[2:3, :]
            s2 = _sig(u2)
            u3 = u2 * s2
            cg = cg_ref[pl.ds(r0, CT), :]
            scg = _sig(cg)
            dcv_ = dcv_ref[pl.ds(r0, CT), :]
            dz_ref[2, pl.ds(r0, CT), :] = (dcv_ * u3 * _dsilu(cg, scg)).astype(bf16)
            du2 = dcv_ * (cg * scg) * _dsilu(u2, s2)
            dvec_ref[1:2, :] += _colsum(du2 * nrm)
            dvec_ref[2:3, :] += _colsum(du2)
            dn = du2 * vec_ref[1:2, :]
            du1 = rs * (dn - jnp.mean(dn, axis=-1, keepdims=True)
                        - nrm * jnp.mean(dn * nrm, axis=-1, keepdims=True))
            dvec_ref[0:1, :] += _colsum(du1)
            q_ref[pl.ds(r0, CT), :] = du1
            for d, a, b in _conv_delays():
                prod = du1 * rolled[b][HALO - 8 * a:HALO - 8 * a + CT, :]
                taps_ref[CONV_K - 1 - d] += jnp.sum(prod.reshape(CT // 8, 8, WC), axis=0)
            return carry

        taps_ref[...] = jnp.zeros_like(taps_ref)
        lax.fori_loop(0, nct, step, 0)
        for row in range(CONV_K):
            ddw_ref[pl.ds(row, 1), :] += _colsum(taps_ref[row])

        def back(c, carry):
            r0 = pl.multiple_of(c * CT, CT)
            wq = q_ref[pl.ds(r0, CT + HALO), :]
            up = {}
            acc = None
            for d, a, b in _conv_delays():
                if b not in up:
                    up[b] = wq if b == 0 else pltpu.roll(wq, CT + HALO - b, axis=0)
                term = up[b][8 * a:8 * a + CT, :] * dw_ref[pl.ds(CONV_K - 1 - d, 1), :]
                acc = term if acc is None else acc + term
            a_ = a_ref[pl.ds(r0, CT), :]
            sb = _sig(b_ref[pl.ds(r0, CT), :])
            dz_ref[0, pl.ds(r0, CT), :] = (acc * sb).astype(bf16)
            dz_ref[1, pl.ds(r0, CT), :] = (acc * a_ * sb * (1.0 - sb)).astype(bf16)
            return carry

        lax.fori_loop(0, nct, back, 0)

    def zs(col):
        return pl.BlockSpec((SEQ, WC), lambda b: (b, col // WC), pipeline_mode=pl.Buffered(1))

    def const(r):
        return pl.BlockSpec((r, WC), lambda b: (0, 0))

    seq = pl.BlockSpec((SEQ, WC), lambda b: (b, 0), pipeline_mode=pl.Buffered(1))
    return pl.pallas_call(
        body, grid=(BL,),
        in_specs=[zs(COL_A), zs(COL_B), zs(COL_CG), seq, seq, ANY, const(32), const(8)],
        out_specs=[pl.BlockSpec((3, SEQ, WC), lambda b: (DZ_CONV // 3, b, 0)), const(32), const(8)],
        out_shape=[jax.ShapeDtypeStruct(dz.shape, bf16), jax.ShapeDtypeStruct((32, WC), f32),
                   jax.ShapeDtypeStruct((8, WC), f32)],
        scratch_shapes=[pltpu.VMEM((SEQ + HALO, WC), f32), pltpu.VMEM((SEQ + HALO, WC), f32),
                        pltpu.VMEM((CONV_K, 8, WC), f32)],
        input_output_aliases={5: 0}, name="conv_bwd",
        compiler_params=_params(("arbitrary",)))(z, z, z, u1, dcv, dz, dw32, cvec)


def _pool_counts(r0):
    t1 = r0 + 1 + lax.broadcasted_iota(jnp.int32, (CT, 1), 0)
    return [jnp.minimum(t1, w).astype(f32) for w in POOL_WINDOWS]


def _pool_sums(win, forward):
    n = CT + PHALO

    def sh(x, s):
        return pltpu.roll(x, (n - s) if forward else s, axis=0)

    s2 = win + sh(win, 1)
    s4 = s2[:, GD:] + sh(s2[:, GD:], 2)
    s8 = s4[:, GD:] + sh(s4[:, GD:], 4)
    s16 = s8[:, GD:] + sh(s8[:, GD:], 8)
    lo = 0 if forward else PHALO
    return [s[lo:lo + CT, :GD] for s in (s2, s4, s8, s16)]


def _pool_fwd(z, pw, pvec, BL, SEQ):
    T = BL * SEQ
    nct = SEQ // CT

    def body(pi_ref, pg_ref, pw_ref, vec_ref, o_ref, p_ref):
        p_ref[pl.ds(0, PHALO), :] = jnp.zeros((PHALO, WC), f32)

        def fill(c, carry):
            r0 = pl.multiple_of(c * CT, CT)
            p_ref[pl.ds(r0 + PHALO, CT), :] = pi_ref[pl.ds(r0, CT), :]
            return carry

        lax.fori_loop(0, nct, fill, 0)

        def step(c, carry):
            r0 = pl.multiple_of(c * CT, CT)
            sums = _pool_sums(p_ref[pl.ds(r0, CT + PHALO), :], False)
            cnt = _pool_counts(r0)
            pin = pi_ref[pl.ds(r0, CT), :]
            mixed = []
            for g in range(4):
                pooled = sums[g] / cnt[g] - pin[:, g * GD:(g + 1) * GD]
                mixed.append(jnp.dot(pooled.astype(bf16), pw_ref[g], preferred_element_type=f32))
            m0 = jnp.concatenate(mixed, axis=1) + vec_ref[0:1, :]
            pg = pg_ref[pl.ds(r0, CT), :]
            o_ref[pl.ds(r0, CT), :] = ((m0 * vec_ref[1:2, :]) * (pg * _sig(pg))).astype(bf16)
            return carry

        lax.fori_loop(0, nct, step, 0)

    def zs(col):
        return pl.BlockSpec((SEQ, WC), lambda b: (b, col // WC))

    return pl.pallas_call(
        body, grid=(BL,),
        in_specs=[zs(COL_PI), zs(COL_PG), pl.BlockSpec((4, GD, GD), lambda b: (0, 0, 0)),
                  pl.BlockSpec((8, WC), lambda b: (0, 0))],
        out_specs=pl.BlockSpec((SEQ, WC), lambda b: (b, 0)),
        out_shape=jax.ShapeDtypeStruct((T, WC), bf16),
        scratch_shapes=[pltpu.VMEM((SEQ + PHALO, WC), f32)], name="pool_fwd",
        compiler_params=_params(("parallel",)))(z, z, pw, pvec)


def _pool_bwd(z, dpl, dz, pw, pvec, BL, SEQ):
    nct = SEQ // CT

    def body(pi_ref, pg_ref, dpl_ref, dzin_ref, pw_ref, vec_ref, dz_ref, dpw_ref, dvec_ref, p_ref, e_ref, dp_ref):
        @pl.when(pl.program_id(0) == 0)
        def _():
            dpw_ref[...] = jnp.zeros_like(dpw_ref)
            dvec_ref[...] = jnp.zeros_like(dvec_ref)

        p_ref[pl.ds(0, PHALO), :] = jnp.zeros((PHALO, WC), f32)
        e_ref[pl.ds(SEQ, PHALO), :] = jnp.zeros((PHALO, WC), f32)

        def fill(c, carry):
            r0 = pl.multiple_of(c * CT, CT)
            p_ref[pl.ds(r0 + PHALO, CT), :] = pi_ref[pl.ds(r0, CT), :]
            return carry

        lax.fori_loop(0, nct, fill, 0)

        def step(c, carry):
            r0 = pl.multiple_of(c * CT, CT)
            sums = _pool_sums(p_ref[pl.ds(r0, CT + PHALO), :], False)
            cnt = _pool_counts(r0)
            pin = pi_ref[pl.ds(r0, CT), :]
            pooled = [(sums[g] / cnt[g] - pin[:, g * GD:(g + 1) * GD]).astype(bf16) for g in range(4)]
            m0 = jnp.concatenate(
                [jnp.dot(pooled[g], pw_ref[g], preferred_element_type=f32) for g in range(4)], axis=1) + vec_ref[0:1, :]
            scale = vec_ref[1:2, :]
            pg = pg_ref[pl.ds(r0, CT), :]
            spg = _sig(pg)
            dpl_ = dpl_ref[pl.ds(r0, CT), :]
            dmixed = dpl_ * (pg * spg)
            dz_ref[1, pl.ds(r0, CT), :] = (dpl_ * (m0 * scale) * _dsilu(pg, spg)).astype(bf16)
            dvec_ref[1:2, :] += _colsum(dmixed * m0)
            dm0 = dmixed * scale
            dvec_ref[0:1, :] += _colsum(dm0)
            dps, es = [], []
            for g in range(4):
                dm0g = dm0[:, g * GD:(g + 1) * GD].astype(bf16)
                dpw_ref[g] += lax.dot_general(pooled[g], dm0g, (((0,), (0,)), ((), ())), preferred_element_type=f32)
                dpg = lax.dot_general(dm0g, pw_ref[g], (((1,), (1,)), ((), ())), preferred_element_type=f32)
                dps.append(dpg)
                es.append(dpg / cnt[g])
            dp_ref[pl.ds(r0, CT), :] = jnp.concatenate(dps, axis=1)
            e_ref[pl.ds(r0, CT), :] = jnp.concatenate(es, axis=1)
            return carry

        lax.fori_loop(0, nct, step, 0)

        def back(c, carry):
            r0 = pl.multiple_of(c * CT, CT)
            fs = _pool_sums(e_ref[pl.ds(r0, CT + PHALO), :], True)
            dz_ref[0, pl.ds(r0, CT), :] = (jnp.concatenate(fs, axis=1) - dp_ref[pl.ds(r0, CT), :]).astype(bf16)
            return carry

        lax.fori_loop(0, nct, back, 0)

    def zs(col):
        return pl.BlockSpec((SEQ, WC), lambda b: (b, col // WC))

    return pl.pallas_call(
        body, grid=(BL,),
        in_specs=[zs(COL_PI), zs(COL_PG), pl.BlockSpec((SEQ, WC), lambda b: (b, 0)), ANY,
                  pl.BlockSpec((4, GD, GD), lambda b: (0, 0, 0)), pl.BlockSpec((8, WC), lambda b: (0, 0))],
        out_specs=[pl.BlockSpec((2, SEQ, WC), lambda b: (DZ_POOL // 2, b, 0)),
                   pl.BlockSpec((4, GD, GD), lambda b: (0, 0, 0)), pl.BlockSpec((8, WC), lambda b: (0, 0))],
        out_shape=[jax.ShapeDtypeStruct(dz.shape, bf16), jax.ShapeDtypeStruct((4, GD, GD), f32),
                   jax.ShapeDtypeStruct((8, WC), f32)],
        scratch_shapes=[pltpu.VMEM((SEQ + PHALO, WC), f32), pltpu.VMEM((SEQ + PHALO, WC), f32),
                        pltpu.VMEM((SEQ, WC), f32)],
        input_output_aliases={3: 0}, name="pool_bwd",
        compiler_params=_params(("arbitrary",)))(z, z, dpl, dz, pw, pvec)


def _attn_prologue(q_ref, k_ref, v_ref, qs0, qs1, kp, vp, SEQ):
    head0 = lax.broadcasted_iota(jnp.int32, (1, 2 * HEAD_DIM), 1) < HEAD_DIM
    kp[pl.ds(0, KEY_PAD), :] = jnp.zeros((KEY_PAD, 2 * HEAD_DIM), bf16)
    vp[pl.ds(0, KEY_PAD), :] = jnp.zeros((KEY_PAD, 2 * HEAD_DIM), bf16)

    def fill(g, carry):
        r0 = pl.multiple_of(g * QG, QG)
        q = q_ref[pl.ds(r0, QG), :] * (HEAD_DIM ** -0.5)
        qs0[pl.ds(r0, QG), :] = jnp.where(head0, q, 0.0).astype(bf16)
        qs1[pl.ds(r0, QG), :] = jnp.where(head0, 0.0, q).astype(bf16)
        kp[pl.ds(r0 + KEY_PAD, QG), :] = k_ref[pl.ds(r0, QG), :].astype(bf16)
        vp[pl.ds(r0 + KEY_PAD, QG), :] = v_ref[pl.ds(r0, QG), :].astype(bf16)
        return carry

    lax.fori_loop(0, SEQ // QG, fill, 0)
    return head0


def _attn_weights(qh, kw, bias):
    s = lax.dot_general(qh, kw, (((1,), (1,)), ((), ())), preferred_element_type=f32) + bias
    e = jnp.exp(s - jnp.max(s, axis=-1, keepdims=True))
    return e, 1.0 / jnp.sum(e, axis=-1, keepdims=True)


def _attn_fwd(z, bm, BL, SEQ):
    T = BL * SEQ
    W2 = 2 * HEAD_DIM

    def body(q_ref, k_ref, v_ref, ag_ref, bm_ref, o_ref, qs0, qs1, kp, vp):
        head0 = _attn_prologue(q_ref, k_ref, v_ref, qs0, qs1, kp, vp, SEQ)

        def group(g, carry):
            r0 = pl.multiple_of(g * QG, QG)
            kw = kp[pl.ds(r0, KW), :]
            vw = vp[pl.ds(r0, KW), :]
            variant = jnp.minimum(g, BIAS_VARIANTS - 1)
            outs = []
            for hh, qs in enumerate((qs0, qs1)):
                e, inv = _attn_weights(qs[pl.ds(r0, QG), :], kw, bm_ref[variant, hh])
                outs.append(jnp.dot(e.astype(bf16), vw, preferred_element_type=f32) * inv)
            o = jnp.where(head0, outs[0], outs[1])
            ag = ag_ref[pl.ds(r0, QG), :]
            o_ref[pl.ds(r0, QG), :] = (o * (ag * _sig(ag))).astype(bf16)
            return carry

        lax.fori_loop(0, SEQ // QG, group, 0, unroll=4)

    def zs(col):
        return pl.BlockSpec((SEQ, W2), lambda b, hp: (b, col // W2 + hp))

    return pl.pallas_call(
        body, grid=(BL, WC // W2),
        in_specs=[zs(COL_Q), zs(COL_K), zs(COL_V), zs(COL_AG),
                  pl.BlockSpec((BIAS_VARIANTS, 2, QG, KW), lambda b, hp: (0, hp, 0, 0))],
        out_specs=pl.BlockSpec((SEQ, W2), lambda b, hp: (b, hp)),
        out_shape=jax.ShapeDtypeStruct((T, WC), bf16),
        scratch_shapes=[pltpu.VMEM((SEQ, W2), bf16), pltpu.VMEM((SEQ, W2), bf16),
                        pltpu.VMEM((SEQ + KEY_PAD, W2), bf16), pltpu.VMEM((SEQ + KEY_PAD, W2), bf16)],
        name="attn_fwd", compiler_params=_params(("parallel", "parallel")))(z, z, z, z, bm)


def _attn_bwd(z, dat, dz, bm, BL, SEQ):
    W2 = 2 * HEAD_DIM

    def body(q_ref, k_ref, v_ref, ag_ref, dat_ref, dzin_ref, bm_ref, dz_ref, dbm_ref, qs0, qs1, kp, vp, dka, dva):
        @pl.when(pl.program_id(1) == 0)
        def _():
            dbm_ref[...] = jnp.zeros_like(dbm_ref)

        head0 = _attn_prologue(q_ref, k_ref, v_ref, qs0, qs1, kp, vp, SEQ)
        dka[...] = jnp.zeros_like(dka)
        dva[...] = jnp.zeros_like(dva)

        def group(g, carry):
            r0 = pl.multiple_of(g * QG, QG)
            kw = kp[pl.ds(r0, KW), :]
            vw = vp[pl.ds(r0, KW), :]
            variant = jnp.minimum(g, BIAS_VARIANTS - 1)
            ag = ag_ref[pl.ds(r0, QG), :]
            do = dat_ref[pl.ds(r0, QG), :] * (ag * _sig(ag))
            outs, dqs = [], []
            for hh, qs in enumerate((qs0, qs1)):
                qh = qs[pl.ds(r0, QG), :]
                e, inv = _attn_weights(qh, kw, bm_ref[variant, hh])
                eb = e.astype(bf16)
                outs.append(jnp.dot(eb, vw, preferred_element_type=f32) * inv)
                doh = (jnp.where(head0, do, 0.0) if hh == 0 else jnp.where(head0, 0.0, do)) * inv
                doh = doh.astype(bf16)
                dp = lax.dot_general(doh, vw, (((1,), (1,)), ((), ())), preferred_element_type=f32)
                ds_ = e * (dp - jnp.sum(e * dp, axis=-1, keepdims=True) * inv)
                dbm_ref[hh] += ds_
                dsb = ds_.astype(bf16)
                dqs.append(jnp.dot(dsb, kw, preferred_element_type=f32))
                dka[pl.ds(r0, KW), :] += lax.dot_general(dsb, qh, (((0,), (0,)), ((), ())), preferred_element_type=f32)
                dva[pl.ds(r0, KW), :] += lax.dot_general(eb, doh, (((0,), (0,)), ((), ())), preferred_element_type=f32)
            o = jnp.where(head0, outs[0], outs[1])
            dq = jnp.where(head0, dqs[0], dqs[1]) * (HEAD_DIM ** -0.5)
            dz_ref[0, pl.ds(r0, QG), :] = dq.astype(bf16)
            dz_ref[3, pl.ds(r0, QG), :] = (dat_ref[pl.ds(r0, QG), :] * o * _dsilu(ag, _sig(ag))).astype(bf16)
            return carry

        lax.fori_loop(0, SEQ // QG, group, 0, unroll=4)

        def flush(g, carry):
            r0 = pl.multiple_of(g * QG, QG)
            dz_ref[1, pl.ds(r0, QG), :] = dka[pl.ds(r0 + KEY_PAD, QG), :].astype(bf16)
            dz_ref[2, pl.ds(r0, QG), :] = dva[pl.ds(r0 + KEY_PAD, QG), :].astype(bf16)
            return carry

        lax.fori_loop(0, SEQ // QG, flush, 0)

    def zs(col):
        return pl.BlockSpec((SEQ, W2), lambda hp, b: (b, col // W2 + hp))

    return pl.pallas_call(
        body, grid=(WC // W2, BL),
        in_specs=[zs(COL_Q), zs(COL_K), zs(COL_V), zs(COL_AG), pl.BlockSpec((SEQ, W2), lambda hp, b: (b, hp)), ANY,
                  pl.BlockSpec((BIAS_VARIANTS, 2, QG, KW), lambda hp, b: (0, hp, 0, 0))],
        out_specs=[pl.BlockSpec((4, SEQ, W2), lambda hp, b: (DZ_ATTN // 4, b, hp)),
                   pl.BlockSpec((2, QG, KW), lambda hp, b: (hp, 0, 0))],
        out_shape=[jax.ShapeDtypeStruct(dz.shape, bf16), jax.ShapeDtypeStruct((8, QG, KW), f32)],
        scratch_shapes=[pltpu.VMEM((SEQ, W2), bf16), pltpu.VMEM((SEQ, W2), bf16),
                        pltpu.VMEM((SEQ + KEY_PAD, W2), bf16), pltpu.VMEM((SEQ + KEY_PAD, W2), bf16),
                        pltpu.VMEM((SEQ + KEY_PAD, W2), f32), pltpu.VMEM((SEQ + KEY_PAD, W2), f32)],
        input_output_aliases={5: 0}, name="attn_bwd",
        compiler_params=_params(("parallel", "arbitrary")))(z, z, z, z, dat, dz, bm)


BIAS_TOP = KEY_PAD + MAX_REL + QG - 1


def _bias_matrix(table):
    n = 2 * MAX_REL
    wd = QG + KW
    e = jnp.concatenate([jnp.broadcast_to(table[:, n:], (8, BIAS_TOP - n + 1)), table[:, n - 1:BIAS_TOP - wd + 1:-1],
                         jnp.zeros((8, 1), f32)], axis=1)
    flat = jnp.broadcast_to(e[:, None, :], (8, QG, wd)).reshape(8, QG * wd)
    skew = flat[:, :QG * (wd - 1)].reshape(8, QG, wd - 1)
    vals = skew[:, :, QG - 1:QG - 1 + KW]
    r = np.arange(QG)[:, None] // CHUNK
    j = np.arange(KW)[None, :]
    band = (j // CHUNK >= r) & (j // CHUNK <= r + LEFT_CHUNKS)
    keep = np.stack([band & (j >= KEY_PAD - v * QG) for v in range(BIAS_VARIANTS)])
    return jnp.where(jnp.asarray(keep)[:, None], vals[None], NEG_INF)


def _bias_fold(dbm):
    wd = QG + KW
    placed = jnp.pad(dbm, ((0, 0), (0, 0), (QG - 1, 0))).reshape(8, QG * (wd - 1))
    return jnp.pad(placed, ((0, 0), (0, QG))).reshape(8, QG, wd)


def _bias_colsum(folded):
    width = folded.shape[2]

    def body(x_ref, o_ref):
        for h in range(8):
            o_ref[pl.ds(h, 1), :] = _colsum(x_ref[h])

    return pl.pallas_call(body, out_shape=jax.ShapeDtypeStruct((8, width), f32), name="bias_colsum",
                          compiler_params=_params())(folded)


def _bias_table_grad(colsum):
    n = 2 * MAX_REL
    wd = QG + KW
    clipped = jnp.sum(colsum[:, :BIAS_TOP - n + 1], axis=1, keepdims=True)
    return jnp.concatenate([jnp.zeros((8, BIAS_TOP - wd + 2), f32), colsum[:, wd - 2:BIAS_TOP - n:-1], clipped], axis=1)


GATE_SPAN = 3 * WC


def _gate_specs(tm):
    return [pl.BlockSpec((tm, GATE_SPAN), lambda i: (i, COL_GM // GATE_SPAN)),
            pl.BlockSpec((tm, GATE_SPAN), lambda i: (i, COL_GM // GATE_SPAN + 1))]


def _gate_block(ga_ref, gb_ref, branch, half):
    k = 2 * branch + half
    ref, k = (ga_ref, k) if k < 3 else (gb_ref, k - 3)
    return _sig(ref[:, k * WC:(k + 1) * WC])


def _resident(shape):
    return pl.BlockSpec(shape, lambda i: (0,) * len(shape), pipeline_mode=pl.Buffered(1))


def _tail_fwd(z, acts, x2, lw, tgt=None):
    T = z.shape[0]
    tm = 256
    with_loss = tgt is not None

    def body(cv_ref, at_ref, pv_ref, ga_ref, gb_ref, x_ref, wc_ref, wa_ref, wp_ref, wo_ref, g_ref, *rest):
        out_ref, merged_ref, y_ref = rest[with_loss:with_loss + 3]
        ys = [jnp.dot(a[...], w[...], preferred_element_type=f32)
              for a, w in ((cv_ref, wc_ref), (at_ref, wa_ref), (pv_ref, wp_ref))]
        halves = []
        for half in range(2):
            cols = slice(half * WC, (half + 1) * WC)
            halves.append(sum(_gate_block(ga_ref, gb_ref, br, half) * ys[br][:, cols] for br in range(3)))
        merged = jnp.concatenate(halves, axis=1).astype(bf16)
        merged_ref[...] = merged
        y = jnp.dot(merged, wo_ref[...], preferred_element_type=f32)
        y_ref[...] = y
        r = lax.rsqrt(jnp.mean(y * y, axis=-1, keepdims=True) + EPS)
        out = x_ref[...] + (y * r) * g_ref[...]
        if not with_loss:
            out_ref[...] = out
        else:
            sq_ref = rest[4]
            e = out - rest[0][...]
            out_ref[...] = e / float(D)

            @pl.when(pl.program_id(0) == 0)
            def _():
                sq_ref[...] = jnp.zeros_like(sq_ref)

            sq_ref[...] += _colsum(e * e)

    act = pl.BlockSpec((tm, WC), lambda i: (i, 0))
    row = pl.BlockSpec((tm, D), lambda i: (i, 0))
    vec = pl.BlockSpec((1, D), lambda i: (0, 0))
    return pl.pallas_call(
        body, grid=(T // tm,),
        in_specs=[act, act, act] + _gate_specs(tm) + [row, _resident((WC, D)), _resident((WC, D)), _resident((WC, D)),
                                                      _resident((D, D)), _resident((1, D))] + [row] * with_loss,
        out_specs=[row, row, row] + [vec] * with_loss,
        out_shape=[jax.ShapeDtypeStruct((T, D), f32), jax.ShapeDtypeStruct((T, D), bf16), jax.ShapeDtypeStruct((T, D), f32)]
        + [jax.ShapeDtypeStruct((1, D), f32)] * with_loss,
        name="tail_fwd", compiler_params=_params(("arbitrary",)))(
            *acts, z, z, x2, lw["w_conv_out"], lw["w_attn_out"], lw["w_pool_out"], lw["w_out"], lw["post_g"],
            *([tgt] if with_loss else []))


def _tail_bwd(z, dout, y, acts, lw):
    T = z.shape[0]
    tm = 256
    nt = (((1,), (1,)), ((), ()))

    def body(d_ref, y_ref, cv_ref, at_ref, pv_ref, ga_ref, gb_ref, wc_ref, wa_ref, wp_ref, wo_ref, g_ref,
             dy_ref, dyc_ref, dya_ref, dyp_ref, dz_ref, dcv_ref, dat_ref, dpv_ref, dg_ref):
        y = y_ref[...]
        d = d_ref[...]
        r = lax.rsqrt(jnp.mean(y * y, axis=-1, keepdims=True) + EPS)
        yn = y * r
        dyn = d * g_ref[...]
        dy = (r * (dyn - yn * jnp.mean(dyn * yn, axis=-1, keepdims=True))).astype(bf16)
        dy_ref[...] = dy

        @pl.when(pl.program_id(0) == 0)
        def _():
            dg_ref[...] = jnp.zeros_like(dg_ref)

        dg_ref[...] += _colsum(d * yn)
        dmerged = lax.dot_general(dy, wo_ref[...], nt, preferred_element_type=f32)
        for br, (a_ref, w_ref, dyb_ref, da_ref) in enumerate(((cv_ref, wc_ref, dyc_ref, dcv_ref),
                                                               (at_ref, wa_ref, dya_ref, dat_ref),
                                                               (pv_ref, wp_ref, dyp_ref, dpv_ref))):
            yb = jnp.dot(a_ref[...], w_ref[...], preferred_element_type=f32)
            halves = []
            for half in range(2):
                cols = slice(half * WC, (half + 1) * WC)
                s = _gate_block(ga_ref, gb_ref, br, half)
                dm = dmerged[:, cols]
                halves.append((dm * s).astype(bf16))
                dz_ref[2 * br + half] = (dm * yb[:, cols] * s * (1.0 - s)).astype(bf16)
            dyb = jnp.concatenate(halves, axis=1)
            dyb_ref[...] = dyb
            da_ref[...] = lax.dot_general(dyb, w_ref[...], nt, preferred_element_type=f32)

    act = pl.BlockSpec((tm, WC), lambda i: (i, 0))
    row = pl.BlockSpec((tm, D), lambda i: (i, 0))
    vec = pl.BlockSpec((1, D), lambda i: (0, 0))
    return pl.pallas_call(
        body, grid=(T // tm,),
        in_specs=[row, row, act, act, act] + _gate_specs(tm) + [_resident((WC, D)), _resident((WC, D)), _resident((WC, D)),
                                                                _resident((D, D)), _resident((1, D))],
        out_specs=[row, row, row, row, pl.BlockSpec((6, tm, WC), lambda i: (DZ_GM // 6, i, 0)), act, act, act, vec],
        out_shape=[jax.ShapeDtypeStruct((T, D), bf16)] * 4 + [jax.ShapeDtypeStruct((DZ_BLOCKS, T, WC), bf16)]
        + [jax.ShapeDtypeStruct((T, WC), f32)] * 3 + [jax.ShapeDtypeStruct((1, D), f32)],
        name="tail_bwd", compiler_params=_params(("arbitrary",)))(
            dout, y, *acts, z, z, lw["w_conv_out"], lw["w_attn_out"], lw["w_pool_out"], lw["w_out"], lw["post_g"])


def _adamw(name, g, w, m, v):
    R, C = w.shape
    tr = R
    for cand in (512, 256, 248, 128, 64, 32, 16, 8):
        if R % cand == 0 and cand * C * 4 <= 2 * 1024 * 1024:
            tr = cand
            break
    c1 = 1.0 - ADAM_B1
    c2 = 1.0 - ADAM_B2
    bc1 = 1.0 - ADAM_B1 ** ADAM_STEP
    bc2 = 1.0 - ADAM_B2 ** ADAM_STEP

    def body(g_ref, w_ref, m_ref, v_ref, d_ref, nm_ref, nv_ref):
        g_ = g_ref[...]
        nm = ADAM_B1 * m_ref[...] + c1 * g_
        nv = ADAM_B2 * v_ref[...] + c2 * (g_ * g_)
        nm_ref[...] = nm
        nv_ref[...] = nv
        d_ref[...] = -ADAM_LR * ((nm / bc1) / (jnp.sqrt(nv / bc2) + ADAM_EPS) + ADAM_WD * w_ref[...])

    spec = pl.BlockSpec((tr, C), lambda i: (i, 0))
    return pl.pallas_call(
        body, grid=(R // tr,), in_specs=[spec] * 4, out_specs=[spec] * 3,
        out_shape=[jax.ShapeDtypeStruct((R, C), f32)] * 3, name=name,
        compiler_params=_params(("parallel",)))(g, w, m, v)


def _sum_slots(name, parts):
    _, R, C = parts.shape
    tr = R
    for cand in (256, 128, 64, 32, 16, 8):
        if R % cand == 0 and cand * C * 4 * N_DEV <= 8 * 1024 * 1024:
            tr = cand
            break

    def body(p_ref, o_ref):
        acc = p_ref[0].astype(f32)
        for s in range(1, N_DEV):
            acc = acc + p_ref[s].astype(f32)
        o_ref[...] = acc

    return pl.pallas_call(
        body, grid=(R // tr,), in_specs=[pl.BlockSpec((N_DEV, tr, C), lambda i: (0, i, 0))],
        out_specs=pl.BlockSpec((tr, C), lambda i: (i, 0)), out_shape=jax.ShapeDtypeStruct((R, C), f32),
        name=name, compiler_params=_params(("parallel",)))(parts)


def _row_tile(rows, row_bytes, budget):
    for cand in (512, 256, 128, 64, 32, 16):
        if rows % cand == 0 and cand * row_bytes <= budget:
            return cand
    return rows


def _pair_sum(core, g, theirs):
    R2, C4 = theirs.shape
    tr = _row_tile(R2, C4 * 2, 2 * 1024 * 1024)
    nb = R2 // tr

    def body(core_ref, g_ref, t_ref, o_ref):
        o_ref[...] = (g_ref[...].astype(f32) + t_ref[...].astype(f32)).astype(bf16)

    return pl.pallas_call(
        body,
        grid_spec=pltpu.PrefetchScalarGridSpec(
            num_scalar_prefetch=1, grid=(nb,),
            in_specs=[pl.BlockSpec((tr, C4), lambda i, core_ref: (core_ref[0] * nb + i, 0)),
                      pl.BlockSpec((tr, C4), lambda i, core_ref: (i, 0))],
            out_specs=pl.BlockSpec((tr, C4), lambda i, core_ref: (i, 0))),
        out_shape=jax.ShapeDtypeStruct((R2, C4), bf16), name="pair_sum",
        compiler_params=_params(("parallel",)))(core, g, theirs)


def _chip_sum(chip, mine, others):
    _, R2, C = others.shape
    tr = _row_tile(R2, C * 4, 1024 * 1024)

    def body(chip_ref, m_ref, o_ref, out_ref):
        acc = m_ref[...].astype(f32)
        for s in range(N_CHIPS - 1):
            acc = acc + o_ref[s].astype(f32)
        out_ref[...] = acc

    return pl.pallas_call(
        body,
        grid_spec=pltpu.PrefetchScalarGridSpec(
            num_scalar_prefetch=1, grid=(R2 // tr,),
            in_specs=[pl.BlockSpec((tr, C), lambda i, chip_ref: (i, chip_ref[0])),
                      pl.BlockSpec((N_CHIPS - 1, tr, C), lambda i, chip_ref: (0, i, 0))],
            out_specs=pl.BlockSpec((tr, C), lambda i, chip_ref: (i, 0))),
        out_shape=jax.ShapeDtypeStruct((R2, C), f32), name="chip_sum",
        compiler_params=_params(("parallel",)))(chip, mine, others)


def _place():
    x, y, c = lax.axis_index("x"), lax.axis_index("y"), lax.axis_index("c")
    return x, y, c


def _flip(v, bit):
    return 1 - v if bit else v


CHIP_FLIPS = ((1, 0), (0, 1), (1, 1))


class _Sems:
    def __init__(self, send, recv):
        self.send, self.recv = send, recv
        self.pairs = 0

    def pair(self):
        k = self.pairs
        self.pairs += 1
        return self.send.at[k], self.recv.at[k]


def _remote(src, dst, lands, sems, to):
    s, r = sems.pair()
    copy = pltpu.make_async_remote_copy(src_ref=src, dst_ref=dst, send_sem=s, recv_sem=r, device_id=to, device_id_type=MESH)
    wait = pltpu.make_async_remote_copy(src_ref=lands, dst_ref=lands, send_sem=s, recv_sem=r, device_id=to, device_id_type=MESH)
    return copy, wait


def _exchange(name, build, srcs, lands, n_remote):
    n_s, n_l = len(srcs), len(lands)

    def body(*refs):
        send, recv = refs[n_s + 2 * n_l:]
        remotes, recvs = build(refs[:n_s], refs[n_s + n_l:n_s + 2 * n_l], _Sems(send, recv))
        for cp in remotes:
            cp.start()
        for rv in recvs:
            rv.wait_recv()
        for cp in remotes:
            cp.wait_send()

    return pl.pallas_call(
        body, in_specs=[ANY] * (n_s + n_l), out_specs=[ANY] * n_l,
        out_shape=[jax.ShapeDtypeStruct(t.shape, t.dtype) for t in lands],
        scratch_shapes=[pltpu.SemaphoreType.DMA((n_remote,)), pltpu.SemaphoreType.DMA((n_remote,))],
        input_output_aliases={n_s + i: i for i in range(n_l)}, name=name)(*srcs, *lands)


HBM = pl.BlockSpec(memory_space=pltpu.HBM)
SEMS = pl.BlockSpec(memory_space=pltpu.SEMAPHORE)
DATAFLOW = pltpu.SideEffectType.DATAFLOW_SIDE_EFFECTING


def _start(name, build, srcs, lands, n_remote, after):
    n_s, n_l = len(srcs), len(lands)

    def body(*refs):
        send, recv = refs[n_s + n_l + 1], refs[n_s + n_l + 2]
        remotes, _ = build(refs[:n_s], refs[n_s:n_s + n_l], _Sems(send, recv))
        for cp in remotes:
            cp.start()
        refs[-1][...] = jnp.zeros((8, 128), f32)

    arrays = [pltpu.with_memory_space_constraint(a, pltpu.HBM) for a in (*srcs, *lands)]
    out = pl.pallas_call(
        body, name=name, in_specs=[HBM] * (n_s + n_l) + [ANY],
        out_specs=(SEMS, SEMS, *[HBM] * (n_s + n_l), pl.BlockSpec(memory_space=pltpu.VMEM)),
        out_shape=(pltpu.SemaphoreType.DMA((n_remote,)), pltpu.SemaphoreType.DMA((n_remote,)),
                   *[pltpu.HBM(a.shape, a.dtype) for a in arrays], jax.ShapeDtypeStruct((8, 128), f32)),
        input_output_aliases={i: 2 + i for i in range(n_s + n_l)},
        compiler_params=pltpu.CompilerParams(has_side_effects=DATAFLOW))(*arrays, after)
    return dict(name=name, build=build, sems=out[:2], srcs=out[2:2 + n_s], lands=out[2 + n_s:2 + n_s + n_l], token=out[-1])


def _wait(started, after):
    srcs, lands, build = started["srcs"], started["lands"], started["build"]
    n_s, n_l = len(srcs), len(lands)
    after = list(after) if isinstance(after, (list, tuple)) else [after]

    def body(*refs):
        send, recv = refs[n_s + n_l], refs[n_s + n_l + 1]
        remotes, recvs = build(refs[:n_s], refs[n_s:n_s + n_l], _Sems(send, recv))
        for rv in recvs:
            rv.wait_recv()
        for cp in remotes:
            cp.wait_send()

    out = pl.pallas_call(
        body, name=started["name"] + "_wait", in_specs=[HBM] * (n_s + n_l) + [SEMS, SEMS] + [ANY] * len(after),
        out_specs=[HBM] * (n_s + n_l), out_shape=[pltpu.HBM(a.shape, a.dtype) for a in (*srcs, *lands)],
        input_output_aliases={i: i for i in range(n_s + n_l)},
        compiler_params=pltpu.CompilerParams(has_side_effects=DATAFLOW))(*srcs, *lands, *started["sems"], *after)
    return out[:n_s], out[n_s:]


def _gather_plans(n_split, n_all):
    def over_ici(src, land, sems):
        x, y, c = _place()
        chip = 2 * x + y
        remotes, recvs = [], []
        for a in range(n_all):
            for fx, fy in CHIP_FLIPS:
                px, py = _flip(x, fx), _flip(y, fy)
                if a < n_split:
                    r2 = src[a].shape[0] // 2
                    rows = pl.ds(c * r2, r2)
                    cp, rv = _remote(src[a].at[rows], land[a].at[chip, rows], land[a].at[2 * px + py, rows], sems, (px, py, c))
                else:
                    cp, rv = _remote(src[a], land[a].at[chip], land[a].at[2 * px + py], sems, (px, py, c))
                remotes.append(cp)
                recvs.append(rv)
        return remotes, recvs

    def over_d2d(src, land, sems):
        x, y, c = _place()
        remotes, recvs = [], []
        for a in range(n_split):
            r2 = land[a].shape[1] // 2
            for fx, fy in CHIP_FLIPS:
                owner = 2 * _flip(x, fx) + _flip(y, fy)
                mine = land[a].at[owner, pl.ds(c * r2, r2)]
                cp, rv = _remote(mine, mine, land[a].at[owner, pl.ds((1 - c) * r2, r2)], sems, (x, y, 1 - c))
                remotes.append(cp)
                recvs.append(rv)
        return remotes, recvs

    return over_ici, over_d2d


def _gather_begin(tag, shards, n_split, after):
    over_ici, _ = _gather_plans(n_split, len(shards))
    lands = [lax.empty((N_CHIPS,) + s.shape, s.dtype) for s in shards]
    return _start("gather_ici_" + tag, over_ici, shards, lands, 3 * len(shards), after)


def _gather_end(started, shards, n_split, after):
    _, over_d2d = _gather_plans(n_split, len(shards))
    lands = _exchange("gather_d2d", over_d2d, [], _wait(started, after)[1], 3 * n_split)
    chip = 2 * lax.axis_index("x") + lax.axis_index("y")
    return [lax.dynamic_update_slice_in_dim(g, s[None], chip, axis=0) for g, s in zip(lands, shards)]


def _reduce_plans(n):
    def to_sibling(src, land, sems):
        x, y, c = _place()
        remotes, recvs = [], []
        for a in range(n):
            r2 = src[a].shape[0] // 2
            cp, rv = _remote(src[a].at[pl.ds((1 - c) * r2, r2), :], land[a], land[a], sems, (x, y, 1 - c))
            remotes.append(cp)
            recvs.append(rv)
        return remotes, recvs

    def across_chips(src, land, sems):
        x, y, c = _place()
        remotes, recvs = [], []
        for a in range(n):
            cw = src[a].shape[1] // N_CHIPS
            for k, (fx, fy) in enumerate(CHIP_FLIPS):
                px, py = _flip(x, fx), _flip(y, fy)
                cp, rv = _remote(src[a].at[:, pl.ds((2 * px + py) * cw, cw)], land[a].at[k], land[a].at[k], sems, (px, py, c))
                remotes.append(cp)
                recvs.append(rv)
        return remotes, recvs

    def share(src, land, sems):
        x, y, c = _place()
        remotes, recvs = [], []
        for a in range(n):
            cp, rv = _remote(src[a], land[a], land[a], sems, (x, y, 1 - c))
            remotes.append(cp)
            recvs.append(rv)
        return remotes, recvs

    return to_sibling, across_chips, share


def _reduce_begin(grads):
    n = len(grads)
    to_sibling, across_chips, _ = _reduce_plans(n)
    core = lax.axis_index("c").reshape(1).astype(jnp.int32)
    theirs = _exchange("reduce_pair", to_sibling, grads,
                       [lax.empty((g.shape[0] // 2, g.shape[1]), bf16) for g in grads], n)
    pair = [_pair_sum(core, g, t) for g, t in zip(grads, theirs)]
    lands = [lax.empty((N_CHIPS - 1, g.shape[0] // 2, g.shape[1] // N_CHIPS), bf16) for g in grads]
    return _start("reduce_chips", across_chips, pair, lands, 3 * n, pair[0])


def _reduce_end(started, after):
    x, y, c = _place()
    chip = (2 * x + y).reshape(1).astype(jnp.int32)
    pair, others = _wait(started, after)
    _, _, share = _reduce_plans(len(pair))
    mine = [_chip_sum(chip, p, o) for p, o in zip(pair, others)]
    sibs = _exchange("reduce_share", share, mine, [lax.empty(h.shape, f32) for h in mine], len(mine))
    return [jnp.where(c == 0, jnp.concatenate([h, s], axis=0), jnp.concatenate([s, h], axis=0))
            for h, s in zip(mine, sibs)]


def _to_all(src, land, sems):
    x, y, c = _place()
    me = 4 * x + 2 * y + c
    remotes, recvs = [], []
    for k in range(1, N_DEV):
        px, py, pc = _flip(x, (k >> 2) & 1), _flip(y, (k >> 1) & 1), _flip(c, k & 1)
        cp, rv = _remote(src[0], land[0].at[me], land[0].at[4 * px + 2 * py + pc], sems, (px, py, pc))
        remotes.append(cp)
        recvs.append(rv)
    return remotes, recvs


def _gather_small_begin(packed):
    return _start("gather_small", _to_all, [packed], [lax.empty((N_DEV,) + packed.shape, f32)], N_DEV - 1, packed)


def _gather_small_end(started, after):
    (packed,), (others,) = _wait(started, after)
    x, y, c = _place()
    return lax.dynamic_update_slice_in_dim(others, packed[None], 4 * x + 2 * y + c, axis=0)


def _gather_all(packed):
    others = _exchange("gather_all", _to_all, [packed], [lax.empty((N_DEV,) + packed.shape, f32)], N_DEV - 1)[0]
    x, y, c = _place()
    return lax.dynamic_update_slice_in_dim(others, packed[None], 4 * x + 2 * y + c, axis=0)


def _rows8(v):
    return jnp.pad(v[None, :], ((0, 7), (0, 0)))


def _vec_rows(vs):
    return jnp.pad(jnp.stack(vs), ((0, 8 - len(vs)), (0, 0)))


SMALL_ROWS = 224


def _pack_small(conv_vec, conv_dw, pool_vec, pool_w, pre_g, post_g, rel):
    return jnp.concatenate([
        conv_vec, conv_dw, pool_vec, pool_w.reshape(GD, WC),
        _rows8(pre_g).reshape(16, WC), _rows8(post_g).reshape(16, WC),
        jnp.pad(rel, ((0, 0), (0, D - rel.shape[1]))).reshape(16, WC)], axis=0)


def _unpack_small(p):
    conv_vec, pool_vec = p[0:8], p[40:48]
    return dict(
        conv_dw_b=conv_vec[0], conv_ln_g=conv_vec[1], conv_ln_b=conv_vec[2], conv_dw=p[8:8 + CONV_K],
        pool_b=pool_vec[0].reshape(4, GD), pool_scale=pool_vec[1], pool_w=p[48:176].reshape(4, GD, GD),
        pre_norm_g=p[176:192].reshape(8, D)[0], post_norm_g=p[192:208].reshape(8, D)[0],
        rel_bias=p[208:224].reshape(8, D)[:, :2 * MAX_REL + 1])


def _in_proj(h, w_in, after=None):
    return _mm("mm_in", h, w_in, "nn", h.shape[0], NCOL, D, 1024, 1536, D, f32, after=after)


def _layer_fwd(x2, ht, z, lw, BL, SEQ, tgt=None):
    cv, u1 = _conv_fwd(z, lw["dw32"], lw["cvec"], BL, SEQ)
    at = _attn_fwd(z, lw["bm"], BL, SEQ)
    pv = _pool_fwd(z, lw["pw"], lw["pvec"], BL, SEQ)
    out, merged, y, *sq = _tail_fwd(z, (cv, at, pv), x2, lw, tgt)
    saved = dict(x=x2, ht=ht, z=z, u1=u1, acts=(cv, at, pv), merged=merged, y=y)
    return (out if tgt is None else (out, sq[0])), saved


def _layer_bwd(dout, sv, lw, BL, SEQ, meanwhile=None):
    T = BL * SEQ
    tail = _tail_bwd(sv["z"], dout, sv["y"], sv["acts"], lw)
    dy, dys, dz, dacts, dpost = tail[0], tail[1:4], tail[4], tail[5:8], tail[8]
    dw_out_t = _mm("mm_dw_out", dy, sv["merged"], "tn", D, D, T, D, D, 1024, bf16)
    dws = [_mm("mm_dw_branch", act, dyb, "tn", WC, D, T, WC, D, 1024, bf16) for act, dyb in zip(sv["acts"], dys)]
    if meanwhile is not None:
        meanwhile(dws[2])
    dz, ddw, dcvec = _conv_bwd(sv["z"], sv["u1"], dacts[0], dz, lw["dw32"], lw["cvec"], BL, SEQ)
    dz, dbm = _attn_bwd(sv["z"], dacts[1], dz, lw["bm"], BL, SEQ)
    dz, dpw, dpvec = _pool_bwd(sv["z"], dacts[2], dz, lw["pw"], lw["pvec"], BL, SEQ)
    drel = _bias_table_grad(_bias_colsum(_bias_fold(dbm)))
    small_gather = _gather_small_begin(_pack_small(dcvec, ddw, dpvec, dpw, jnp.zeros((D,), f32), dpost[0], drel))
    dw_in = _mm_dw_in(sv["ht"], dz, small_gather["token"])
    reduction = _reduce_begin([dw_in, dws[0], dws[1], dws[2], dw_out_t])
    dh = _mm_dh(dz, lw["w_in"], reduction["token"])
    dx, dpre = _pre_bwd(dh, sv["x"], lw["pre_g"], dout)
    return dx, reduction, small_gather, dpre


BIG = ("w_in", "w_conv_out", "w_attn_out", "w_pool_out", "w_out")
PRE_ROWS = slice(176, 192)


def _layer_shards(w, l):
    return [w[k][l].astype(bf16) for k in BIG] + [w["conv_dw"][l]]


def _side_by_side(g):
    return jnp.transpose(g, (1, 0, 2)).reshape(g.shape[1], N_CHIPS * g.shape[2])


def _layer_weights(w_in, gathered, w, l, bm):
    lw = {k: _side_by_side(g) for k, g in zip(BIG[1:4], gathered[:3])}
    lw["w_in"] = w_in
    lw["w_out"] = gathered[3].reshape(D, D)
    lw["pre_g"] = w["pre_norm_g"][l][None]
    lw["post_g"] = w["post_norm_g"][l][None]
    lw["dw32"] = jnp.pad(_side_by_side(gathered[4]), ((0, 32 - CONV_K), (0, 0)))
    lw["cvec"] = _vec_rows([w["conv_dw_b"][l], w["conv_ln_g"][l], w["conv_ln_b"][l]])
    lw["bm"] = bm
    lw["pw"] = w["pool_w"][l].astype(bf16)
    lw["pvec"] = _vec_rows([w["pool_b"][l].reshape(WC), w["pool_scale"][l]])
    return lw


SMALL = ("pre_norm_g", "post_norm_g", "conv_dw_b", "conv_ln_g", "conv_ln_b", "rel_bias", "pool_w", "pool_b", "pool_scale")
ORDER = ("pre_norm_g", "post_norm_g", "w_in", "conv_dw", "conv_dw_b", "conv_ln_g", "conv_ln_b", "w_conv_out",
         "rel_bias", "w_attn_out", "pool_w", "pool_b", "pool_scale", "w_pool_out", "w_out")


def _pack_small_params(p):
    return jnp.concatenate([
        _pack_small(_vec_rows([p["conv_dw_b"][l], p["conv_ln_g"][l], p["conv_ln_b"][l]]), jnp.zeros((32, WC), f32),
                    _vec_rows([p["pool_b"][l].reshape(WC), p["pool_scale"][l]]), p["pool_w"][l],
                    p["pre_norm_g"][l], p["post_norm_g"][l], p["rel_bias"][l])
        for l in range(DEPTH)], axis=0)


def _unpack_small_params(packed):
    layers = [_unpack_small(packed[l * SMALL_ROWS:(l + 1) * SMALL_ROWS]) for l in range(DEPTH)]
    return {k: jnp.stack([layers[l][k] for l in range(DEPTH)]) for k in layers[0]}


def kernel(x, pre_norm_g, post_norm_g, w_in, conv_dw, conv_dw_b, conv_ln_g, conv_ln_b, w_conv_out, rel_bias, w_attn_out, pool_w, pool_b, pool_scale, w_pool_out, w_out, loss_target, m_pre_norm_g, m_post_norm_g, m_w_in, m_conv_dw, m_conv_dw_b, m_conv_ln_g, m_conv_ln_b, m_w_conv_out, m_rel_bias, m_w_attn_out, m_pool_w, m_pool_b, m_pool_scale, m_w_pool_out, m_w_out, v_pre_norm_g, v_post_norm_g, v_w_in, v_conv_dw, v_conv_dw_b, v_conv_ln_g, v_conv_ln_b, v_w_conv_out, v_rel_bias, v_w_attn_out, v_pool_w, v_pool_b, v_pool_scale, v_w_pool_out, v_w_out):
    BL, SEQ, _ = x.shape
    T = BL * SEQ
    w = dict(pre_norm_g=pre_norm_g, post_norm_g=post_norm_g, w_in=w_in, conv_dw=conv_dw, conv_dw_b=conv_dw_b,
             conv_ln_g=conv_ln_g, conv_ln_b=conv_ln_b, w_conv_out=w_conv_out, rel_bias=rel_bias, w_attn_out=w_attn_out,
             pool_w=pool_w, pool_b=pool_b, pool_scale=pool_scale, w_pool_out=w_pool_out, w_out=w_out)
    m = dict(pre_norm_g=m_pre_norm_g, post_norm_g=m_post_norm_g, w_in=m_w_in, conv_dw=m_conv_dw, conv_dw_b=m_conv_dw_b,
             conv_ln_g=m_conv_ln_g, conv_ln_b=m_conv_ln_b, w_conv_out=m_w_conv_out, rel_bias=m_rel_bias,
             w_attn_out=m_w_attn_out, pool_w=m_pool_w, pool_b=m_pool_b, pool_scale=m_pool_scale,
             w_pool_out=m_w_pool_out, w_out=m_w_out)
    v = dict(pre_norm_g=v_pre_norm_g, post_norm_g=v_post_norm_g, w_in=v_w_in, conv_dw=v_conv_dw, conv_dw_b=v_conv_dw_b,
             conv_ln_g=v_conv_ln_g, conv_ln_b=v_conv_ln_b, w_conv_out=v_w_conv_out, rel_bias=v_rel_bias,
             w_attn_out=v_w_attn_out, pool_w=v_pool_w, pool_b=v_pool_b, pool_scale=v_pool_scale,
             w_pool_out=v_w_pool_out, w_out=v_w_out)

    shards = [_layer_shards(w, l) for l in range(DEPTH)]
    x2 = x.reshape(T, D)
    h0, ht0 = _rms_pre(x2, pre_norm_g[0][None])
    first = _gather_begin("w_in0", shards[0][:1], 1, x2)
    bms = [_bias_matrix(rel_bias[l]) for l in range(DEPTH)]
    packs = [_pack_small_params(p) for p in (w, m, v)]
    w_in0 = _side_by_side(_gather_end(first, shards[0][:1], 1, [ht0, *bms, *packs])[0])
    rest0 = _gather_begin("rest0", shards[0][1:], 4, w_in0)
    all1 = _gather_begin("layer1", shards[1], 5, rest0["token"])
    z0 = _in_proj(h0, w_in0, after=all1["token"])
    lw0 = _layer_weights(w_in0, _gather_end(rest0, shards[0][1:], 4, z0), w, 0, bms[0])
    out0, saved0 = _layer_fwd(x2, ht0, z0, lw0, BL, SEQ)
    gathered1 = _gather_end(all1, shards[1], 5, out0)
    lw1 = _layer_weights(_side_by_side(gathered1[0]), gathered1[1:], w, 1, bms[1])
    h1, ht1 = _rms_pre(out0, lw1["pre_g"])
    (dout, sq), saved1 = _layer_fwd(out0, ht1, _in_proj(h1, lw1["w_in"]), lw1, BL, SEQ, loss_target.reshape(T, D))
    loss = lax.psum(0.5 * jnp.sum(sq) / float(D), ("x", "y", "c"))

    summed = [None] * DEPTH
    dx1, reduction1, small_gather1, dpre1 = _layer_bwd(dout, saved1, lw1, BL, SEQ)

    def finish_layer1(after):
        summed[1] = _reduce_end(reduction1, after)

    grad_x, reduction0, small_gather0, dpre0 = _layer_bwd(dx1, saved0, lw0, BL, SEQ, meanwhile=finish_layer1)
    summed[0] = _reduce_end(reduction0, grad_x)
    dpre = _sum_slots("sum_small", _gather_all(jnp.concatenate([_rows8(dpre0[0]), _rows8(dpre1[0])], axis=0)))
    gsmall = []
    for l, started in enumerate((small_gather0, small_gather1)):
        g = _sum_slots("sum_small", _gather_small_end(started, dpre))
        gsmall += [g[:PRE_ROWS.start], dpre[8 * l:8 * l + 8].reshape(16, WC), g[PRE_ROWS.stop:]]
    gsmall = jnp.concatenate(gsmall, axis=0)

    grads, deltas, new_m, new_v = {}, {}, {}, {}
    for i, k in enumerate(BIG):
        g = jnp.stack([summed[l][i] for l in range(DEPTH)])
        if k == "w_out":
            g = jnp.transpose(g, (0, 2, 1))
        grads[k] = g
        shape = w[k].shape
        flat2 = lambda a: a.reshape(shape[0] * shape[1], shape[2])
        d_, nm_, nv_ = _adamw("adamw_big", flat2(g), flat2(w[k]), flat2(m[k]), flat2(v[k]))
        deltas[k], new_m[k], new_v[k] = d_.reshape(shape), nm_.reshape(shape), nv_.reshape(shape)

    d_, nm_, nv_ = _adamw("adamw_small", gsmall, *packs)
    gs, ds, ms, vs = (_unpack_small_params(a) for a in (gsmall, d_, nm_, nv_))
    for k in SMALL:
        grads[k], deltas[k], new_m[k], new_v[k] = gs[k], ds[k], ms[k], vs[k]
    chip = 2 * lax.axis_index("x") + lax.axis_index("y")
    g_dw = lax.dynamic_slice_in_dim(gs["conv_dw"], chip * GD, GD, axis=2)
    flat2 = lambda a: a.reshape(DEPTH * CONV_K, GD)
    d_, nm_, nv_ = _adamw("adamw_conv_dw", flat2(g_dw), flat2(conv_dw), flat2(m["conv_dw"]), flat2(v["conv_dw"]))
    grads["conv_dw"] = g_dw
    deltas["conv_dw"], new_m["conv_dw"], new_v["conv_dw"] = (a.reshape(conv_dw.shape) for a in (d_, nm_, nv_))

    return (loss, grad_x.reshape(x.shape), *[grads[k] for k in ORDER], *[deltas[k] for k in ORDER],
            *[new_m[k] for k in ORDER], *[new_v[k] for k in ORDER])
```

```python
import numpy as np
import jax
import jax.numpy as jnp
from jax import lax
from jax.experimental import pallas as pl
from jax.experimental.pallas import tpu as pltpu

f32 = jnp.float32
bf16 = jnp.bfloat16

D = 1024
DEPTH = 2
WC = 512
HEAD_DIM = 64
CHUNK = 64
LEFT_CHUNKS = 8
KEY_PAD = LEFT_CHUNKS * CHUNK
MAX_REL = 256
CONV_K = 31
POOL_WINDOWS = (2, 4, 8, 16)
GD = 128
NCOL = 7680
EPS = 1e-6
NEG_INF = -1e30
COL_A, COL_B, COL_CG, COL_Q, COL_K, COL_V, COL_AG, COL_PI, COL_PG, COL_GM = (
    0, 512, 1024, 1536, 2048, 2560, 3072, 3584, 4096, 4608)

ADAM_LR = 0.001
ADAM_B1 = 0.9
ADAM_B2 = 0.999
ADAM_EPS = 1e-08
ADAM_WD = 0.01
ADAM_STEP = 10

QG = 256
KW = KEY_PAD + QG
BIAS_VARIANTS = KEY_PAD // QG + 1
CT = 128
HALO = 32
PHALO = 16
N_CHIPS = 4
N_DEV = 8
VMEM_LIMIT = 56 * 1024 * 1024
MESH = pl.DeviceIdType.MESH
ANY = pl.BlockSpec(memory_space=pl.ANY)

DZ_BLOCKS = 18
DZ_CONV, DZ_ATTN, DZ_POOL, DZ_GM = 0, 4, 8, 12


def _dz_block(c):
    return c + (c >= 3).astype(jnp.int32) + 2 * (c >= 9).astype(jnp.int32)


def _params(sem=None):
    return pltpu.CompilerParams(dimension_semantics=sem, vmem_limit_bytes=VMEM_LIMIT)


def _sig(x):
    return 1.0 / (1.0 + jnp.exp(-x))


def _dsilu(x, s):
    return s * (1.0 + x * (1.0 - s))


def _colsum(x):
    return jnp.sum(x, axis=0, keepdims=True)


def _rms_pre(x2, g):
    T = x2.shape[0]
    tm = 512

    def body(x_ref, g_ref, h_ref, ht_ref):
        x = x_ref[...]
        r = lax.rsqrt(jnp.mean(x * x, axis=-1, keepdims=True) + EPS)
        h = (x * r) * g_ref[...]
        h_ref[...] = h.astype(bf16)
        ht_ref[...] = h.T.astype(bf16)

    row = pl.BlockSpec((tm, D), lambda i: (i, 0))
    vec = pl.BlockSpec((1, D), lambda i: (0, 0))
    return pl.pallas_call(
        body, grid=(T // tm,), in_specs=[row, vec], out_specs=[row, pl.BlockSpec((D, tm), lambda i: (0, i))],
        out_shape=[jax.ShapeDtypeStruct((T, D), bf16), jax.ShapeDtypeStruct((D, T), bf16)], name="rms_pre",
        compiler_params=_params(("parallel",)))(x2, g)


def _mm(name, a, b, mode, m, n, k, tm, tn, tk, out_dtype, after=None):
    nk = k // tk
    assert m % tm == 0 and n % tn == 0 and k % tk == 0
    if mode == "nn":
        a_spec = pl.BlockSpec((tm, tk), lambda i, j, kk: (i, kk))
        b_spec = pl.BlockSpec((tk, tn), lambda i, j, kk: (kk, j))
        dn = (((1,), (0,)), ((), ()))
    elif mode == "nt":
        a_spec = pl.BlockSpec((tm, tk), lambda i, j, kk: (i, kk))
        b_spec = pl.BlockSpec((tn, tk), lambda i, j, kk: (j, kk))
        dn = (((1,), (1,)), ((), ()))
    else:
        a_spec = pl.BlockSpec((tk, tm), lambda i, j, kk: (kk, i))
        b_spec = pl.BlockSpec((tk, tn), lambda i, j, kk: (kk, j))
        dn = (((0,), (0,)), ((), ()))
    extra = [] if after is None else [after]

    def body(a_ref, b_ref, *rest):
        o_ref, acc_ref = rest[len(extra):]
        p = lax.dot_general(a_ref[...].astype(bf16), b_ref[...].astype(bf16), dn, preferred_element_type=f32)
        if nk == 1:
            o_ref[...] = p.astype(o_ref.dtype)
        else:
            kk = pl.program_id(2)

            @pl.when(kk == 0)
            def _():
                acc_ref[...] = p

            @pl.when(kk > 0)
            def _():
                acc_ref[...] += p

            @pl.when(kk == nk - 1)
            def _():
                o_ref[...] = acc_ref[...].astype(o_ref.dtype)

    acc_shape = (tm, tn) if nk > 1 else (8, 128)
    return pl.pallas_call(
        body, grid=(m // tm, n // tn, nk), in_specs=[a_spec, b_spec] + [ANY] * len(extra),
        out_specs=pl.BlockSpec((tm, tn), lambda i, j, kk: (i, j)),
        out_shape=jax.ShapeDtypeStruct((m, n), out_dtype),
        scratch_shapes=[pltpu.VMEM(acc_shape, f32)], name=name,
        compiler_params=_params(("parallel", "parallel", "arbitrary")))(a, b, *extra)


DZ_SPANS = ((DZ_CONV, 3), (DZ_ATTN, 4), (DZ_POOL, 2), (DZ_GM, 6))


def _mm_dx(dz, w_in, x2, g, dout, after):
    T = dz.shape[1]
    tm = 512

    def body(conv_ref, attn_ref, pool_ref, gm_ref, w_ref, x_ref, g_ref, d_ref, after_ref, dx_ref, dg_ref):
        dh = None
        col = 0
        for ref, (_, blocks) in zip((conv_ref, attn_ref, pool_ref, gm_ref), DZ_SPANS):
            for b in range(blocks):
                p = lax.dot_general(ref[b], w_ref[:, col * WC:(col + 1) * WC], (((1,), (1,)), ((), ())),
                                    preferred_element_type=f32)
                dh = p if dh is None else dh + p
                col += 1
        x = x_ref[...]
        r = lax.rsqrt(jnp.mean(x * x, axis=-1, keepdims=True) + EPS)
        xn = x * r
        dxn = dh * g_ref[...]
        dx_ref[...] = r * (dxn - xn * jnp.mean(dxn * xn, axis=-1, keepdims=True)) + d_ref[...]

        @pl.when(pl.program_id(0) == 0)
        def _():
            dg_ref[...] = jnp.zeros_like(dg_ref)

        dg_ref[...] += _colsum(dh * xn)

    spans = [pl.BlockSpec((blocks, tm, WC), lambda i, first=first, blocks=blocks: (first // blocks, i, 0))
             for first, blocks in DZ_SPANS]
    row = pl.BlockSpec((tm, D), lambda i: (i, 0))
    vec = pl.BlockSpec((1, D), lambda i: (0, 0))
    return pl.pallas_call(
        body, grid=(T // tm,),
        in_specs=spans + [pl.BlockSpec((D, NCOL), lambda i: (0, 0), pipeline_mode=pl.Buffered(1)), row, vec, row, ANY],
        out_specs=[row, vec], out_shape=[jax.ShapeDtypeStruct((T, D), f32), jax.ShapeDtypeStruct((1, D), f32)],
        name="mm_dx", compiler_params=_params(("arbitrary",)))(dz, dz, dz, dz, w_in, x2, g, dout, after)


def _mm_dw_in(ht, dz, after):
    T = dz.shape[1]

    def body(ht_ref, dz_ref, after_ref, o_ref):
        o_ref[...] = jnp.dot(ht_ref[...], dz_ref[...], preferred_element_type=f32).astype(bf16)

    return pl.pallas_call(
        body, grid=(NCOL // WC,),
        in_specs=[pl.BlockSpec((D, T), lambda j: (0, 0), pipeline_mode=pl.Buffered(1)),
                  pl.BlockSpec((None, T, WC), lambda j: (_dz_block(j), 0, 0)), ANY],
        out_specs=pl.BlockSpec((D, WC), lambda j: (0, j)), out_shape=jax.ShapeDtypeStruct((D, NCOL), bf16),
        name="mm_dw_in", compiler_params=_params(("parallel",)))(ht, dz, after)


def _conv_delays():
    return [(8 * a + b, a, b) for b in range(8) for a in range(4) if 8 * a + b < CONV_K]


def _conv_rolls(win):
    return [win if b == 0 else pltpu.roll(win, b, axis=0) for b in range(8)]


def _conv_taps(rolled, dw_ref):
    acc = None
    for d, a, b in _conv_delays():
        term = rolled[b][HALO - 8 * a:HALO - 8 * a + CT, :] * dw_ref[pl.ds(CONV_K - 1 - d, 1), :]
        acc = term if acc is None else acc + term
    return acc


def _conv_fwd(z, dw32, cvec, BL, SEQ):
    T = BL * SEQ
    nct = SEQ // CT

    def body(a_ref, b_ref, cg_ref, dw_ref, vec_ref, o_ref, u1_ref, p_ref):
        p_ref[pl.ds(0, HALO), :] = jnp.zeros((HALO, WC), f32)

        def glu(c, carry):
            r0 = pl.multiple_of(c * CT, CT)
            p_ref[pl.ds(r0 + HALO, CT), :] = a_ref[pl.ds(r0, CT), :] * _sig(b_ref[pl.ds(r0, CT), :])
            return carry

        lax.fori_loop(0, nct, glu, 0)

        def step(c, carry):
            r0 = pl.multiple_of(c * CT, CT)
            u1 = _conv_taps(_conv_rolls(p_ref[pl.ds(r0, CT + HALO), :]), dw_ref) + vec_ref[0:1, :]
            u1_ref[pl.ds(r0, CT), :] = u1
            xc = u1 - jnp.mean(u1, axis=-1, keepdims=True)
            rs = lax.rsqrt(jnp.mean(xc * xc, axis=-1, keepdims=True) + EPS)
            u2 = (xc * rs) * vec_ref[1:2, :] + vec_ref[2:3, :]
            cg = cg_ref[pl.ds(r0, CT), :]
            o_ref[pl.ds(r0, CT), :] = ((u2 * _sig(u2)) * (cg * _sig(cg))).astype(bf16)
            return carry

        lax.fori_loop(0, nct, step, 0)

    def zs(col):
        return pl.BlockSpec((SEQ, WC), lambda b: (b, col // WC))

    seq = pl.BlockSpec((SEQ, WC), lambda b: (b, 0))
    return pl.pallas_call(
        body, grid=(BL,),
        in_specs=[zs(COL_A), zs(COL_B), zs(COL_CG), pl.BlockSpec((32, WC), lambda b: (0, 0)),
                  pl.BlockSpec((8, WC), lambda b: (0, 0))],
        out_specs=[seq, seq],
        out_shape=[jax.ShapeDtypeStruct((T, WC), bf16), jax.ShapeDtypeStruct((T, WC), f32)],
        scratch_shapes=[pltpu.VMEM((SEQ + HALO, WC), f32)], name="conv_fwd",
        compiler_params=_params(("parallel",)))(z, z, z, dw32, cvec)


def _conv_bwd(z, u1, dcv, dz, dw32, cvec, BL, SEQ):
    nct = SEQ // CT

    def body(a_ref, b_ref, cg_ref, u1_ref, dcv_ref, dzin_ref, dw_ref, vec_ref, dz_ref, ddw_ref, dvec_ref,
             p_ref, q_ref, taps_ref):
        @pl.when(pl.program_id(0) == 0)
        def _():
            ddw_ref[...] = jnp.zeros_like(ddw_ref)
            dvec_ref[...] = jnp.zeros_like(dvec_ref)

        p_ref[pl.ds(0, HALO), :] = jnp.zeros((HALO, WC), f32)
        q_ref[pl.ds(SEQ, HALO), :] = jnp.zeros((HALO, WC), f32)

        def glu(c, carry):
            r0 = pl.multiple_of(c * CT, CT)
            p_ref[pl.ds(r0 + HALO, CT), :] = a_ref[pl.ds(r0, CT), :] * _sig(b_ref[pl.ds(r0, CT), :])
            return carry

        lax.fori_loop(0, nct, glu, 0)

        def step(c, carry):
            r0 = pl.multiple_of(c * CT, CT)
            rolled = _conv_rolls(p_ref[pl.ds(r0, CT + HALO), :])
            u1 = u1_ref[pl.ds(r0, CT), :]
            xc = u1 - jnp.mean(u1, axis=-1, keepdims=True)
            rs = lax.rsqrt(jnp.mean(xc * xc, axis=-1, keepdims=True) + EPS)
            nrm = xc * rs
            u2 = nrm * vec_ref[1:2, :] + vec_ref[2:3, :]
            s2 = _sig(u2)
            u3 = u2 * s2
            cg = cg_ref[pl.ds(r0, CT), :]
            scg = _sig(cg)
            dcv_ = dcv_ref[pl.ds(r0, CT), :]
            dz_ref[2, pl.ds(r0, CT), :] = (dcv_ * u3 * _dsilu(cg, scg)).astype(bf16)
            du2 = dcv_ * (cg * scg) * _dsilu(u2, s2)
            dvec_ref[1:2, :] += _colsum(du2 * nrm)
            dvec_ref[2:3, :] += _colsum(du2)
            dn = du2 * vec_ref[1:2, :]
            du1 = rs * (dn - jnp.mean(dn, axis=-1, keepdims=True)
                        - nrm * jnp.mean(dn * nrm, axis=-1, keepdims=True))
            dvec_ref[0:1, :] += _colsum(du1)
            q_ref[pl.ds(r0, CT), :] = du1
            for d, a, b in _conv_delays():
                prod = du1 * rolled[b][HALO - 8 * a:HALO - 8 * a + CT, :]
                taps_ref[CONV_K - 1 - d] += jnp.sum(prod.reshape(CT // 8, 8, WC), axis=0)
            return carry

        taps_ref[...] = jnp.zeros_like(taps_ref)
        lax.fori_loop(0, nct, step, 0)
        for row in range(CONV_K):
            ddw_ref[pl.ds(row, 1), :] += _colsum(taps_ref[row])

        def back(c, carry):
            r0 = pl.multiple_of(c * CT, CT)
            wq = q_ref[pl.ds(r0, CT + HALO), :]
            up = {}
            acc = None
            for d, a, b in _conv_delays():
                if b not in up:
                    up[b] = wq if b == 0 else pltpu.roll(wq, CT + HALO - b, axis=0)
                term = up[b][8 * a:8 * a + CT, :] * dw_ref[pl.ds(CONV_K - 1 - d, 1), :]
                acc = term if acc is None else acc + term
            a_ = a_ref[pl.ds(r0, CT), :]
            sb = _sig(b_ref[pl.ds(r0, CT), :])
            dz_ref[0, pl.ds(r0, CT), :] = (acc * sb).astype(bf16)
            dz_ref[1, pl.ds(r0, CT), :] = (acc * a_ * sb * (1.0 - sb)).astype(bf16)
            return carry

        lax.fori_loop(0, nct, back, 0)

    def zs(col):
        return pl.BlockSpec((SEQ, WC), lambda b: (b, col // WC), pipeline_mode=pl.Buffered(1))

    def const(r):
        return pl.BlockSpec((r, WC), lambda b: (0, 0))

    seq = pl.BlockSpec((SEQ, WC), lambda b: (b, 0), pipeline_mode=pl.Buffered(1))
    return pl.pallas_call(
        body, grid=(BL,),
        in_specs=[zs(COL_A), zs(COL_B), zs(COL_CG), seq, seq, ANY, const(32), const(8)],
        out_specs=[pl.BlockSpec((3, SEQ, WC), lambda b: (DZ_CONV // 3, b, 0)), const(32), const(8)],
        out_shape=[jax.ShapeDtypeStruct(dz.shape, bf16), jax.ShapeDtypeStruct((32, WC), f32),
                   jax.ShapeDtypeStruct((8, WC), f32)],
        scratch_shapes=[pltpu.VMEM((SEQ + HALO, WC), f32), pltpu.VMEM((SEQ + HALO, WC), f32),
                        pltpu.VMEM((CONV_K, 8, WC), f32)],
        input_output_aliases={5: 0}, name="conv_bwd",
        compiler_params=_params(("arbitrary",)))(z, z, z, u1, dcv, dz, dw32, cvec)


def _pool_counts(r0):
    t1 = r0 + 1 + lax.broadcasted_iota(jnp.int32, (CT, 1), 0)
    return [jnp.minimum(t1, w).astype(f32) for w in POOL_WINDOWS]


def _pool_sums(win, forward):
    n = CT + PHALO

    def sh(x, s):
        return pltpu.roll(x, (n - s) if forward else s, axis=0)

    s2 = win + sh(win, 1)
    s4 = s2[:, GD:] + sh(s2[:, GD:], 2)
    s8 = s4[:, GD:] + sh(s4[:, GD:], 4)
    s16 = s8[:, GD:] + sh(s8[:, GD:], 8)
    lo = 0 if forward else PHALO
    return [s[lo:lo + CT, :GD] for s in (s2, s4, s8, s16)]


def _pool_fwd(z, pw, pvec, BL, SEQ):
    T = BL * SEQ
    nct = SEQ // CT

    def body(pi_ref, pg_ref, pw_ref, vec_ref, o_ref, p_ref):
        p_ref[pl.ds(0, PHALO), :] = jnp.zeros((PHALO, WC), f32)

        def fill(c, carry):
            r0 = pl.multiple_of(c * CT, CT)
            p_ref[pl.ds(r0 + PHALO, CT), :] = pi_ref[pl.ds(r0, CT), :]
            return carry

        lax.fori_loop(0, nct, fill, 0)

        def step(c, carry):
            r0 = pl.multiple_of(c * CT, CT)
            sums = _pool_sums(p_ref[pl.ds(r0, CT + PHALO), :], False)
            cnt = _pool_counts(r0)
            pin = pi_ref[pl.ds(r0, CT), :]
            mixed = []
            for g in range(4):
                pooled = sums[g] / cnt[g] - pin[:, g * GD:(g + 1) * GD]
                mixed.append(jnp.dot(pooled.astype(bf16), pw_ref[g], preferred_element_type=f32))
            m0 = jnp.concatenate(mixed, axis=1) + vec_ref[0:1, :]
            pg = pg_ref[pl.ds(r0, CT), :]
            o_ref[pl.ds(r0, CT), :] = ((m0 * vec_ref[1:2, :]) * (pg * _sig(pg))).astype(bf16)
            return carry

        lax.fori_loop(0, nct, step, 0)

    def zs(col):
        return pl.BlockSpec((SEQ, WC), lambda b: (b, col // WC))

    return pl.pallas_call(
        body, grid=(BL,),
        in_specs=[zs(COL_PI), zs(COL_PG), pl.BlockSpec((4, GD, GD), lambda b: (0, 0, 0)),
                  pl.BlockSpec((8, WC), lambda b: (0, 0))],
        out_specs=pl.BlockSpec((SEQ, WC), lambda b: (b, 0)),
        out_shape=jax.ShapeDtypeStruct((T, WC), bf16),
        scratch_shapes=[pltpu.VMEM((SEQ + PHALO, WC), f32)], name="pool_fwd",
        compiler_params=_params(("parallel",)))(z, z, pw, pvec)


def _pool_bwd(z, dpl, dz, pw, pvec, BL, SEQ):
    nct = SEQ // CT

    def body(pi_ref, pg_ref, dpl_ref, dzin_ref, pw_ref, vec_ref, dz_ref, dpw_ref, dvec_ref, p_ref, e_ref, dp_ref):
        @pl.when(pl.program_id(0) == 0)
        def _():
            dpw_ref[...] = jnp.zeros_like(dpw_ref)
            dvec_ref[...] = jnp.zeros_like(dvec_ref)

        p_ref[pl.ds(0, PHALO), :] = jnp.zeros((PHALO, WC), f32)
        e_ref[pl.ds(SEQ, PHALO), :] = jnp.zeros((PHALO, WC), f32)

        def fill(c, carry):
            r0 = pl.multiple_of(c * CT, CT)
            p_ref[pl.ds(r0 + PHALO, CT), :] = pi_ref[pl.ds(r0, CT), :]
            return carry

        lax.fori_loop(0, nct, fill, 0)

        def step(c, carry):
            r0 = pl.multiple_of(c * CT, CT)
            sums = _pool_sums(p_ref[pl.ds(r0, CT + PHALO), :], False)
            cnt = _pool_counts(r0)
            pin = pi_ref[pl.ds(r0, CT), :]
            pooled = [(sums[g] / cnt[g] - pin[:, g * GD:(g + 1) * GD]).astype(bf16) for g in range(4)]
            m0 = jnp.concatenate(
                [jnp.dot(pooled[g], pw_ref[g], preferred_element_type=f32) for g in range(4)], axis=1) + vec_ref[0:1, :]
            scale = vec_ref[1:2, :]
            pg = pg_ref[pl.ds(r0, CT), :]
            spg = _sig(pg)
            dpl_ = dpl_ref[pl.ds(r0, CT), :]
            dmixed = dpl_ * (pg * spg)
            dz_ref[1, pl.ds(r0, CT), :] = (dpl_ * (m0 * scale) * _dsilu(pg, spg)).astype(bf16)
            dvec_ref[1:2, :] += _colsum(dmixed * m0)
            dm0 = dmixed * scale
            dvec_ref[0:1, :] += _colsum(dm0)
            dps, es = [], []
            for g in range(4):
                dm0g = dm0[:, g * GD:(g + 1) * GD].astype(bf16)
                dpw_ref[g] += lax.dot_general(pooled[g], dm0g, (((0,), (0,)), ((), ())), preferred_element_type=f32)
                dpg = lax.dot_general(dm0g, pw_ref[g], (((1,), (1,)), ((), ())), preferred_element_type=f32)
                dps.append(dpg)
                es.append(dpg / cnt[g])
            dp_ref[pl.ds(r0, CT), :] = jnp.concatenate(dps, axis=1)
            e_ref[pl.ds(r0, CT), :] = jnp.concatenate(es, axis=1)
            return carry

        lax.fori_loop(0, nct, step, 0)

        def back(c, carry):
            r0 = pl.multiple_of(c * CT, CT)
            fs = _pool_sums(e_ref[pl.ds(r0, CT + PHALO), :], True)
            dz_ref[0, pl.ds(r0, CT), :] = (jnp.concatenate(fs, axis=1) - dp_ref[pl.ds(r0, CT), :]).astype(bf16)
            return carry

        lax.fori_loop(0, nct, back, 0)

    def zs(col):
        return pl.BlockSpec((SEQ, WC), lambda b: (b, col // WC))

    return pl.pallas_call(
        body, grid=(BL,),
        in_specs=[zs(COL_PI), zs(COL_PG), pl.BlockSpec((SEQ, WC), lambda b: (b, 0)), ANY,
                  pl.BlockSpec((4, GD, GD), lambda b: (0, 0, 0)), pl.BlockSpec((8, WC), lambda b: (0, 0))],
        out_specs=[pl.BlockSpec((2, SEQ, WC), lambda b: (DZ_POOL // 2, b, 0)),
                   pl.BlockSpec((4, GD, GD), lambda b: (0, 0, 0)), pl.BlockSpec((8, WC), lambda b: (0, 0))],
        out_shape=[jax.ShapeDtypeStruct(dz.shape, bf16), jax.ShapeDtypeStruct((4, GD, GD), f32),
                   jax.ShapeDtypeStruct((8, WC), f32)],
        scratch_shapes=[pltpu.VMEM((SEQ + PHALO, WC), f32), pltpu.VMEM((SEQ + PHALO, WC), f32),
                        pltpu.VMEM((SEQ, WC), f32)],
        input_output_aliases={3: 0}, name="pool_bwd",
        compiler_params=_params(("arbitrary",)))(z, z, dpl, dz, pw, pvec)


def _attn_prologue(q_ref, k_ref, v_ref, qs0, qs1, kp, vp, SEQ):
    head0 = lax.broadcasted_iota(jnp.int32, (1, 2 * HEAD_DIM), 1) < HEAD_DIM
    kp[pl.ds(0, KEY_PAD), :] = jnp.zeros((KEY_PAD, 2 * HEAD_DIM), bf16)
    vp[pl.ds(0, KEY_PAD), :] = jnp.zeros((KEY_PAD, 2 * HEAD_DIM), bf16)

    def fill(g, carry):
        r0 = pl.multiple_of(g * QG, QG)
        q = q_ref[pl.ds(r0, QG), :] * (HEAD_DIM ** -0.5)
        qs0[pl.ds(r0, QG), :] = jnp.where(head0, q, 0.0).astype(bf16)
        qs1[pl.ds(r0, QG), :] = jnp.where(head0, 0.0, q).astype(bf16)
        kp[pl.ds(r0 + KEY_PAD, QG), :] = k_ref[pl.ds(r0, QG), :].astype(bf16)
        vp[pl.ds(r0 + KEY_PAD, QG), :] = v_ref[pl.ds(r0, QG), :].astype(bf16)
        return carry

    lax.fori_loop(0, SEQ // QG, fill, 0)
    return head0


def _attn_weights(qh, kw, bias):
    s = lax.dot_general(qh, kw, (((1,), (1,)), ((), ())), preferred_element_type=f32) + bias
    e = jnp.exp(s - jnp.max(s, axis=-1, keepdims=True))
    return e, 1.0 / jnp.sum(e, axis=-1, keepdims=True)


def _attn_fwd(z, bm, BL, SEQ):
    T = BL * SEQ
    W2 = 2 * HEAD_DIM

    def body(q_ref, k_ref, v_ref, ag_ref, bm_ref, o_ref, qs0, qs1, kp, vp):
        head0 = _attn_prologue(q_ref, k_ref, v_ref, qs0, qs1, kp, vp, SEQ)

        def group(g, carry):
            r0 = pl.multiple_of(g * QG, QG)
            kw = kp[pl.ds(r0, KW), :]
            vw = vp[pl.ds(r0, KW), :]
            variant = jnp.minimum(g, BIAS_VARIANTS - 1)
            outs = []
            for hh, qs in enumerate((qs0, qs1)):
                e, inv = _attn_weights(qs[pl.ds(r0, QG), :], kw, bm_ref[variant, hh])
                outs.append(jnp.dot(e.astype(bf16), vw, preferred_element_type=f32) * inv)
            o = jnp.where(head0, outs[0], outs[1])
            ag = ag_ref[pl.ds(r0, QG), :]
            o_ref[pl.ds(r0, QG), :] = (o * (ag * _sig(ag))).astype(bf16)
            return carry

        lax.fori_loop(0, SEQ // QG, group, 0, unroll=4)

    def zs(col):
        return pl.BlockSpec((SEQ, W2), lambda b, hp: (b, col // W2 + hp))

    return pl.pallas_call(
        body, grid=(BL, WC // W2),
        in_specs=[zs(COL_Q), zs(COL_K), zs(COL_V), zs(COL_AG),
                  pl.BlockSpec((BIAS_VARIANTS, 2, QG, KW), lambda b, hp: (0, hp, 0, 0))],
        out_specs=pl.BlockSpec((SEQ, W2), lambda b, hp: (b, hp)),
        out_shape=jax.ShapeDtypeStruct((T, WC), bf16),
        scratch_shapes=[pltpu.VMEM((SEQ, W2), bf16), pltpu.VMEM((SEQ, W2), bf16),
                        pltpu.VMEM((SEQ + KEY_PAD, W2), bf16), pltpu.VMEM((SEQ + KEY_PAD, W2), bf16)],
        name="attn_fwd", compiler_params=_params(("parallel", "parallel")))(z, z, z, z, bm)


def _attn_bwd(z, dat, dz, bm, BL, SEQ):
    W2 = 2 * HEAD_DIM

    def body(q_ref, k_ref, v_ref, ag_ref, dat_ref, dzin_ref, bm_ref, dz_ref, dbm_ref, qs0, qs1, kp, vp, dka, dva):
        @pl.when(pl.program_id(1) == 0)
        def _():
            dbm_ref[...] = jnp.zeros_like(dbm_ref)

        head0 = _attn_prologue(q_ref, k_ref, v_ref, qs0, qs1, kp, vp, SEQ)
        dka[...] = jnp.zeros_like(dka)
        dva[...] = jnp.zeros_like(dva)

        def group(g, carry):
            r0 = pl.multiple_of(g * QG, QG)
            kw = kp[pl.ds(r0, KW), :]
            vw = vp[pl.ds(r0, KW), :]
            variant = jnp.minimum(g, BIAS_VARIANTS - 1)
            ag = ag_ref[pl.ds(r0, QG), :]
            do = dat_ref[pl.ds(r0, QG), :] * (ag * _sig(ag))
            outs, dqs = [], []
            for hh, qs in enumerate((qs0, qs1)):
                qh = qs[pl.ds(r0, QG), :]
                e, inv = _attn_weights(qh, kw, bm_ref[variant, hh])
                eb = e.astype(bf16)
                outs.append(jnp.dot(eb, vw, preferred_element_type=f32) * inv)
                doh = (jnp.where(head0, do, 0.0) if hh == 0 else jnp.where(head0, 0.0, do)) * inv
                doh = doh.astype(bf16)
                dp = lax.dot_general(doh, vw, (((1,), (1,)), ((), ())), preferred_element_type=f32)
                ds_ = e * (dp - jnp.sum(e * dp, axis=-1, keepdims=True) * inv)
                dbm_ref[hh] += ds_
                dsb = ds_.astype(bf16)
                dqs.append(jnp.dot(dsb, kw, preferred_element_type=f32))
                dka[pl.ds(r0, KW), :] += lax.dot_general(dsb, qh, (((0,), (0,)), ((), ())), preferred_element_type=f32)
                dva[pl.ds(r0, KW), :] += lax.dot_general(eb, doh, (((0,), (0,)), ((), ())), preferred_element_type=f32)
            o = jnp.where(head0, outs[0], outs[1])
            dq = jnp.where(head0, dqs[0], dqs[1]) * (HEAD_DIM ** -0.5)
            dz_ref[0, pl.ds(r0, QG), :] = dq.astype(bf16)
            dz_ref[3, pl.ds(r0, QG), :] = (dat_ref[pl.ds(r0, QG), :] * o * _dsilu(ag, _sig(ag))).astype(bf16)
            return carry

        lax.fori_loop(0, SEQ // QG, group, 0, unroll=4)

        def flush(g, carry):
            r0 = pl.multiple_of(g * QG, QG)
            dz_ref[1, pl.ds(r0, QG), :] = dka[pl.ds(r0 + KEY_PAD, QG), :].astype(bf16)
            dz_ref[2, pl.ds(r0, QG), :] = dva[pl.ds(r0 + KEY_PAD, QG), :].astype(bf16)
            return carry

        lax.fori_loop(0, SEQ // QG, flush, 0)

    def zs(col):
        return pl.BlockSpec((SEQ, W2), lambda hp, b: (b, col // W2 + hp))

    return pl.pallas_call(
        body, grid=(WC // W2, BL),
        in_specs=[zs(COL_Q), zs(COL_K), zs(COL_V), zs(COL_AG), pl.BlockSpec((SEQ, W2), lambda hp, b: (b, hp)), ANY,
                  pl.BlockSpec((BIAS_VARIANTS, 2, QG, KW), lambda hp, b: (0, hp, 0, 0))],
        out_specs=[pl.BlockSpec((4, SEQ, W2), lambda hp, b: (DZ_ATTN // 4, b, hp)),
                   pl.BlockSpec((2, QG, KW), lambda hp, b: (hp, 0, 0))],
        out_shape=[jax.ShapeDtypeStruct(dz.shape, bf16), jax.ShapeDtypeStruct((8, QG, KW), f32)],
        scratch_shapes=[pltpu.VMEM((SEQ, W2), bf16), pltpu.VMEM((SEQ, W2), bf16),
                        pltpu.VMEM((SEQ + KEY_PAD, W2), bf16), pltpu.VMEM((SEQ + KEY_PAD, W2), bf16),
                        pltpu.VMEM((SEQ + KEY_PAD, W2), f32), pltpu.VMEM((SEQ + KEY_PAD, W2), f32)],
        input_output_aliases={5: 0}, name="attn_bwd",
        compiler_params=_params(("parallel", "arbitrary")))(z, z, z, z, dat, dz, bm)


BIAS_TOP = KEY_PAD + MAX_REL + QG - 1


def _bias_matrix(table):
    n = 2 * MAX_REL
    wd = QG + KW
    e = jnp.concatenate([jnp.broadcast_to(table[:, n:], (8, BIAS_TOP - n + 1)), table[:, n - 1:BIAS_TOP - wd + 1:-1],
                         jnp.zeros((8, 1), f32)], axis=1)
    flat = jnp.broadcast_to(e[:, None, :], (8, QG, wd)).reshape(8, QG * wd)
    skew = flat[:, :QG * (wd - 1)].reshape(8, QG, wd - 1)
    vals = skew[:, :, QG - 1:QG - 1 + KW]
    r = np.arange(QG)[:, None] // CHUNK
    j = np.arange(KW)[None, :]
    band = (j // CHUNK >= r) & (j // CHUNK <= r + LEFT_CHUNKS)
    keep = np.stack([band & (j >= KEY_PAD - v * QG) for v in range(BIAS_VARIANTS)])
    return jnp.where(jnp.asarray(keep)[:, None], vals[None], NEG_INF)


def _bias_fold(dbm):
    wd = QG + KW
    placed = jnp.pad(dbm, ((0, 0), (0, 0), (QG - 1, 0))).reshape(8, QG * (wd - 1))
    return jnp.pad(placed, ((0, 0), (0, QG))).reshape(8, QG, wd)


def _bias_colsum(folded):
    width = folded.shape[2]

    def body(x_ref, o_ref):
        for h in range(8):
            o_ref[pl.ds(h, 1), :] = _colsum(x_ref[h])

    return pl.pallas_call(body, out_shape=jax.ShapeDtypeStruct((8, width), f32), name="bias_colsum",
                          compiler_params=_params())(folded)


def _bias_table_grad(colsum):
    n = 2 * MAX_REL
    wd = QG + KW
    clipped = jnp.sum(colsum[:, :BIAS_TOP - n + 1], axis=1, keepdims=True)
    return jnp.concatenate([jnp.zeros((8, BIAS_TOP - wd + 2), f32), colsum[:, wd - 2:BIAS_TOP - n:-1], clipped], axis=1)


GATE_SPAN = 3 * WC


def _gate_specs(tm):
    return [pl.BlockSpec((tm, GATE_SPAN), lambda i: (i, COL_GM // GATE_SPAN)),
            pl.BlockSpec((tm, GATE_SPAN), lambda i: (i, COL_GM // GATE_SPAN + 1))]


def _gate_block(ga_ref, gb_ref, branch, half):
    k = 2 * branch + half
    ref, k = (ga_ref, k) if k < 3 else (gb_ref, k - 3)
    return _sig(ref[:, k * WC:(k + 1) * WC])


def _resident(shape):
    return pl.BlockSpec(shape, lambda i: (0,) * len(shape), pipeline_mode=pl.Buffered(1))


def _tail_fwd(z, acts, x2, lw, next_g=None, tgt=None):
    T = z.shape[0]
    tm = 256
    with_loss = tgt is not None
    assert with_loss != (next_g is not None)

    def body(cv_ref, at_ref, pv_ref, ga_ref, gb_ref, x_ref, wc_ref, wa_ref, wp_ref, wo_ref, g_ref, *rest):
        out_ref, merged_ref, y_ref = rest[1:4]
        ys = [jnp.dot(a[...], w[...], preferred_element_type=f32)
              for a, w in ((cv_ref, wc_ref), (at_ref, wa_ref), (pv_ref, wp_ref))]
        halves = []
        for half in range(2):
            cols = slice(half * WC, (half + 1) * WC)
            halves.append(sum(_gate_block(ga_ref, gb_ref, br, half) * ys[br][:, cols] for br in range(3)))
        merged = jnp.concatenate(halves, axis=1).astype(bf16)
        merged_ref[...] = merged
        y = jnp.dot(merged, wo_ref[...], preferred_element_type=f32)
        y_ref[...] = y
        r = lax.rsqrt(jnp.mean(y * y, axis=-1, keepdims=True) + EPS)
        out = x_ref[...] + (y * r) * g_ref[...]
        if with_loss:
            sq_ref = rest[4]
            e = out - rest[0][...]
            out_ref[...] = e / float(D)

            @pl.when(pl.program_id(0) == 0)
            def _():
                sq_ref[...] = jnp.zeros_like(sq_ref)

            sq_ref[...] += _colsum(e * e)
        else:
            out_ref[...] = out
            rn = lax.rsqrt(jnp.mean(out * out, axis=-1, keepdims=True) + EPS)
            h = (out * rn) * rest[0][...]
            rest[4][...] = h.astype(bf16)
            rest[5][...] = h.T.astype(bf16)

    act = pl.BlockSpec((tm, WC), lambda i: (i, 0))
    row = pl.BlockSpec((tm, D), lambda i: (i, 0))
    vec = pl.BlockSpec((1, D), lambda i: (0, 0))
    if with_loss:
        last_in, last_specs, last_shapes = tgt, [row, [vec]], [jax.ShapeDtypeStruct((1, D), f32)]
    else:
        last_in, last_specs = next_g, [vec, [row, pl.BlockSpec((D, tm), lambda i: (0, i))]]
        last_shapes = [jax.ShapeDtypeStruct((T, D), bf16), jax.ShapeDtypeStruct((D, T), bf16)]
    return pl.pallas_call(
        body, grid=(T // tm,),
        in_specs=[act, act, act] + _gate_specs(tm) + [row, _resident((WC, D)), _resident((WC, D)), _resident((WC, D)),
                                                      _resident((D, D)), _resident((1, D)), last_specs[0]],
        out_specs=[row, row, row] + last_specs[1],
        out_shape=[jax.ShapeDtypeStruct((T, D), f32), jax.ShapeDtypeStruct((T, D), bf16), jax.ShapeDtypeStruct((T, D), f32)]
        + last_shapes,
        name="tail_fwd", compiler_params=_params(("arbitrary",)))(
            *acts, z, z, x2, lw["w_conv_out"], lw["w_attn_out"], lw["w_pool_out"], lw["w_out"], lw["post_g"], last_in)


def _tail_bwd(z, dout, y, merged, acts, lw):
    T = z.shape[0]
    tm = 256
    nt = (((1,), (1,)), ((), ()))
    tn = (((0,), (0,)), ((), ()))

    def body(d_ref, y_ref, m_ref, cv_ref, at_ref, pv_ref, ga_ref, gb_ref, wc_ref, wa_ref, wp_ref, wo_ref, g_ref,
             dz_ref, dcv_ref, dat_ref, dpv_ref, dg_ref, dwc_ref, dwa_ref, dwp_ref, dwo_ref):
        @pl.when(pl.program_id(0) == 0)
        def _():
            for ref in (dg_ref, dwc_ref, dwa_ref, dwp_ref, dwo_ref):
                ref[...] = jnp.zeros_like(ref)

        y = y_ref[...]
        d = d_ref[...]
        r = lax.rsqrt(jnp.mean(y * y, axis=-1, keepdims=True) + EPS)
        yn = y * r
        dyn = d * g_ref[...]
        dy = (r * (dyn - yn * jnp.mean(dyn * yn, axis=-1, keepdims=True))).astype(bf16)
        dg_ref[...] += _colsum(d * yn)
        dwo_ref[...] += lax.dot_general(dy, m_ref[...], tn, preferred_element_type=f32)
        dmerged = lax.dot_general(dy, wo_ref[...], nt, preferred_element_type=f32)
        for br, (a_ref, w_ref, da_ref, dw_ref) in enumerate(((cv_ref, wc_ref, dcv_ref, dwc_ref),
                                                              (at_ref, wa_ref, dat_ref, dwa_ref),
                                                              (pv_ref, wp_ref, dpv_ref, dwp_ref))):
            yb = jnp.dot(a_ref[...], w_ref[...], preferred_element_type=f32)
            halves = []
            for half in range(2):
                cols = slice(half * WC, (half + 1) * WC)
                s = _gate_block(ga_ref, gb_ref, br, half)
                dm = dmerged[:, cols]
                halves.append((dm * s).astype(bf16))
                dz_ref[2 * br + half] = (dm * yb[:, cols] * s * (1.0 - s)).astype(bf16)
            dyb = jnp.concatenate(halves, axis=1)
            da_ref[...] = lax.dot_general(dyb, w_ref[...], nt, preferred_element_type=f32)
            dw_ref[...] += lax.dot_general(a_ref[...], dyb, tn, preferred_element_type=f32)

    act = pl.BlockSpec((tm, WC), lambda i: (i, 0))
    row = pl.BlockSpec((tm, D), lambda i: (i, 0))

    def whole(shape):
        return pl.BlockSpec(shape, lambda i: (0, 0))

    return pl.pallas_call(
        body, grid=(T // tm,),
        in_specs=[row, row, row, act, act, act] + _gate_specs(tm) + [_resident((WC, D)), _resident((WC, D)), _resident((WC, D)),
                                                                     _resident((D, D)), _resident((1, D))],
        out_specs=[pl.BlockSpec((6, tm, WC), lambda i: (DZ_GM // 6, i, 0)), act, act, act, whole((1, D)),
                   whole((WC, D)), whole((WC, D)), whole((WC, D)), whole((D, D))],
        out_shape=[jax.ShapeDtypeStruct((DZ_BLOCKS, T, WC), bf16)] + [jax.ShapeDtypeStruct((T, WC), f32)] * 3
        + [jax.ShapeDtypeStruct((1, D), f32)] + [jax.ShapeDtypeStruct((WC, D), f32)] * 3 + [jax.ShapeDtypeStruct((D, D), f32)],
        name="tail_bwd", compiler_params=_params(("arbitrary",)))(
            dout, y, merged, *acts, z, z, lw["w_conv_out"], lw["w_attn_out"], lw["w_pool_out"], lw["w_out"], lw["post_g"])


def _adamw(name, g, w, m, v):
    R, C = w.shape
    tr = R
    for cand in (512, 256, 248, 128, 64, 32, 16, 8):
        if R % cand == 0 and cand * C * 4 <= 2 * 1024 * 1024:
            tr = cand
            break
    c1 = 1.0 - ADAM_B1
    c2 = 1.0 - ADAM_B2
    bc1 = 1.0 - ADAM_B1 ** ADAM_STEP
    bc2 = 1.0 - ADAM_B2 ** ADAM_STEP

    def body(g_ref, w_ref, m_ref, v_ref, d_ref, nm_ref, nv_ref):
        g_ = g_ref[...]
        nm = ADAM_B1 * m_ref[...] + c1 * g_
        nv = ADAM_B2 * v_ref[...] + c2 * (g_ * g_)
        nm_ref[...] = nm
        nv_ref[...] = nv
        d_ref[...] = -ADAM_LR * ((nm / bc1) / (jnp.sqrt(nv / bc2) + ADAM_EPS) + ADAM_WD * w_ref[...])

    spec = pl.BlockSpec((tr, C), lambda i: (i, 0))
    return pl.pallas_call(
        body, grid=(R // tr,), in_specs=[spec] * 4, out_specs=[spec] * 3,
        out_shape=[jax.ShapeDtypeStruct((R, C), f32)] * 3, name=name,
        compiler_params=_params(("parallel",)))(g, w, m, v)


def _sum_slots(name, parts):
    _, R, C = parts.shape
    tr = R
    for cand in (256, 128, 64, 32, 16, 8):
        if R % cand == 0 and cand * C * 4 * N_DEV <= 8 * 1024 * 1024:
            tr = cand
            break

    def body(p_ref, o_ref):
        acc = p_ref[0].astype(f32)
        for s in range(1, N_DEV):
            acc = acc + p_ref[s].astype(f32)
        o_ref[...] = acc

    return pl.pallas_call(
        body, grid=(R // tr,), in_specs=[pl.BlockSpec((N_DEV, tr, C), lambda i: (0, i, 0))],
        out_specs=pl.BlockSpec((tr, C), lambda i: (i, 0)), out_shape=jax.ShapeDtypeStruct((R, C), f32),
        name=name, compiler_params=_params(("parallel",)))(parts)


def _row_tile(rows, row_bytes, budget):
    for cand in (512, 256, 128, 64, 32, 16):
        if rows % cand == 0 and cand * row_bytes <= budget:
            return cand
    return rows


def _pair_sum(core, g, theirs):
    R2, C4 = theirs.shape
    tr = _row_tile(R2, C4 * 2, 2 * 1024 * 1024)
    nb = R2 // tr

    def body(core_ref, g_ref, t_ref, o_ref):
        o_ref[...] = (g_ref[...].astype(f32) + t_ref[...].astype(f32)).astype(bf16)

    return pl.pallas_call(
        body,
        grid_spec=pltpu.PrefetchScalarGridSpec(
            num_scalar_prefetch=1, grid=(nb,),
            in_specs=[pl.BlockSpec((tr, C4), lambda i, core_ref: (core_ref[0] * nb + i, 0)),
                      pl.BlockSpec((tr, C4), lambda i, core_ref: (i, 0))],
            out_specs=pl.BlockSpec((tr, C4), lambda i, core_ref: (i, 0))),
        out_shape=jax.ShapeDtypeStruct((R2, C4), bf16), name="pair_sum",
        compiler_params=_params(("parallel",)))(core, g, theirs)


def _chip_sum(chip, mine, others):
    _, R2, C = others.shape
    tr = _row_tile(R2, C * 4, 1024 * 1024)

    def body(chip_ref, m_ref, o_ref, out_ref):
        acc = m_ref[...].astype(f32)
        for s in range(N_CHIPS - 1):
            acc = acc + o_ref[s].astype(f32)
        out_ref[...] = acc

    return pl.pallas_call(
        body,
        grid_spec=pltpu.PrefetchScalarGridSpec(
            num_scalar_prefetch=1, grid=(R2 // tr,),
            in_specs=[pl.BlockSpec((tr, C), lambda i, chip_ref: (i, chip_ref[0])),
                      pl.BlockSpec((N_CHIPS - 1, tr, C), lambda i, chip_ref: (0, i, 0))],
            out_specs=pl.BlockSpec((tr, C), lambda i, chip_ref: (i, 0))),
        out_shape=jax.ShapeDtypeStruct((R2, C), f32), name="chip_sum",
        compiler_params=_params(("parallel",)))(chip, mine, others)


def _place():
    x, y, c = lax.axis_index("x"), lax.axis_index("y"), lax.axis_index("c")
    return x, y, c


def _flip(v, bit):
    return 1 - v if bit else v


CHIP_FLIPS = ((1, 0), (0, 1), (1, 1))


class _Sems:
    def __init__(self, send, recv):
        self.send, self.recv = send, recv
        self.pairs = 0

    def pair(self):
        k = self.pairs
        self.pairs += 1
        return self.send.at[k], self.recv.at[k]


def _remote(src, dst, lands, sems, to):
    s, r = sems.pair()
    copy = pltpu.make_async_remote_copy(src_ref=src, dst_ref=dst, send_sem=s, recv_sem=r, device_id=to, device_id_type=MESH)
    wait = pltpu.make_async_remote_copy(src_ref=lands, dst_ref=lands, send_sem=s, recv_sem=r, device_id=to, device_id_type=MESH)
    return copy, wait


def _exchange(name, build, srcs, lands, n_remote):
    n_s, n_l = len(srcs), len(lands)

    def body(*refs):
        send, recv = refs[n_s + 2 * n_l:]
        remotes, recvs = build(refs[:n_s], refs[n_s + n_l:n_s + 2 * n_l], _Sems(send, recv))
        for cp in remotes:
            cp.start()
        for rv in recvs:
            rv.wait_recv()
        for cp in remotes:
            cp.wait_send()

    return pl.pallas_call(
        body, in_specs=[ANY] * (n_s + n_l), out_specs=[ANY] * n_l,
        out_shape=[jax.ShapeDtypeStruct(t.shape, t.dtype) for t in lands],
        scratch_shapes=[pltpu.SemaphoreType.DMA((n_remote,)), pltpu.SemaphoreType.DMA((n_remote,))],
        input_output_aliases={n_s + i: i for i in range(n_l)}, name=name)(*srcs, *lands)


HBM = pl.BlockSpec(memory_space=pltpu.HBM)
SEMS = pl.BlockSpec(memory_space=pltpu.SEMAPHORE)
DATAFLOW = pltpu.SideEffectType.DATAFLOW_SIDE_EFFECTING


def _start(name, build, srcs, lands, n_remote, after):
    n_s, n_l = len(srcs), len(lands)

    def body(*refs):
        send, recv = refs[n_s + n_l + 1], refs[n_s + n_l + 2]
        remotes, _ = build(refs[:n_s], refs[n_s:n_s + n_l], _Sems(send, recv))
        for cp in remotes:
            cp.start()
        refs[-1][...] = jnp.zeros((8, 128), f32)

    arrays = [pltpu.with_memory_space_constraint(a, pltpu.HBM) for a in (*srcs, *lands)]
    out = pl.pallas_call(
        body, name=name, in_specs=[HBM] * (n_s + n_l) + [ANY],
        out_specs=(SEMS, SEMS, *[HBM] * (n_s + n_l), pl.BlockSpec(memory_space=pltpu.VMEM)),
        out_shape=(pltpu.SemaphoreType.DMA((n_remote,)), pltpu.SemaphoreType.DMA((n_remote,)),
                   *[pltpu.HBM(a.shape, a.dtype) for a in arrays], jax.ShapeDtypeStruct((8, 128), f32)),
        input_output_aliases={i: 2 + i for i in range(n_s + n_l)},
        compiler_params=pltpu.CompilerParams(has_side_effects=DATAFLOW))(*arrays, after)
    return dict(name=name, build=build, sems=out[:2], srcs=out[2:2 + n_s], lands=out[2 + n_s:2 + n_s + n_l], token=out[-1])


def _wait(started, after):
    srcs, lands, build = started["srcs"], started["lands"], started["build"]
    n_s, n_l = len(srcs), len(lands)
    after = list(after) if isinstance(after, (list, tuple)) else [after]

    def body(*refs):
        send, recv = refs[n_s + n_l], refs[n_s + n_l + 1]
        remotes, recvs = build(refs[:n_s], refs[n_s:n_s + n_l], _Sems(send, recv))
        for rv in recvs:
            rv.wait_recv()
        for cp in remotes:
            cp.wait_send()

    out = pl.pallas_call(
        body, name=started["name"] + "_wait", in_specs=[HBM] * (n_s + n_l) + [SEMS, SEMS] + [ANY] * len(after),
        out_specs=[HBM] * (n_s + n_l), out_shape=[pltpu.HBM(a.shape, a.dtype) for a in (*srcs, *lands)],
        input_output_aliases={i: i for i in range(n_s + n_l)},
        compiler_params=pltpu.CompilerParams(has_side_effects=DATAFLOW))(*srcs, *lands, *started["sems"], *after)
    return out[:n_s], out[n_s:]


def _gather_plans(n_split, n_all):
    def over_ici(src, land, sems):
        x, y, c = _place()
        chip = 2 * x + y
        remotes, recvs = [], []
        for a in range(n_all):
            for fx, fy in CHIP_FLIPS:
                px, py = _flip(x, fx), _flip(y, fy)
                if a < n_split:
                    r2 = src[a].shape[0] // 2
                    rows = pl.ds(c * r2, r2)
                    cp, rv = _remote(src[a].at[rows], land[a].at[chip, rows], land[a].at[2 * px + py, rows], sems, (px, py, c))
                else:
                    cp, rv = _remote(src[a], land[a].at[chip], land[a].at[2 * px + py], sems, (px, py, c))
                remotes.append(cp)
                recvs.append(rv)
        return remotes, recvs

    def over_d2d(src, land, sems):
        x, y, c = _place()
        remotes, recvs = [], []
        for a in range(n_split):
            r2 = land[a].shape[1] // 2
            for fx, fy in CHIP_FLIPS:
                owner = 2 * _flip(x, fx) + _flip(y, fy)
                mine = land[a].at[owner, pl.ds(c * r2, r2)]
                cp, rv = _remote(mine, mine, land[a].at[owner, pl.ds((1 - c) * r2, r2)], sems, (x, y, 1 - c))
                remotes.append(cp)
                recvs.append(rv)
        return remotes, recvs

    return over_ici, over_d2d


def _gather_begin(tag, shards, n_split, after):
    over_ici, _ = _gather_plans(n_split, len(shards))
    lands = [lax.empty((N_CHIPS,) + s.shape, s.dtype) for s in shards]
    return _start("gather_ici_" + tag, over_ici, shards, lands, 3 * len(shards), after)


def _gather_end(started, shards, n_split, after):
    _, over_d2d = _gather_plans(n_split, len(shards))
    lands = _exchange("gather_d2d", over_d2d, [], _wait(started, after)[1], 3 * n_split)
    chip = 2 * lax.axis_index("x") + lax.axis_index("y")
    return [lax.dynamic_update_slice_in_dim(g, s[None], chip, axis=0) for g, s in zip(lands, shards)]


def _reduce_plans(n):
    def to_sibling(src, land, sems):
        x, y, c = _place()
        remotes, recvs = [], []
        for a in range(n):
            r2 = src[a].shape[0] // 2
            cp, rv = _remote(src[a].at[pl.ds((1 - c) * r2, r2), :], land[a], land[a], sems, (x, y, 1 - c))
            remotes.append(cp)
            recvs.append(rv)
        return remotes, recvs

    def across_chips(src, land, sems):
        x, y, c = _place()
        remotes, recvs = [], []
        for a in range(n):
            cw = src[a].shape[1] // N_CHIPS
            for k, (fx, fy) in enumerate(CHIP_FLIPS):
                px, py = _flip(x, fx), _flip(y, fy)
                cp, rv = _remote(src[a].at[:, pl.ds((2 * px + py) * cw, cw)], land[a].at[k], land[a].at[k], sems, (px, py, c))
                remotes.append(cp)
                recvs.append(rv)
        return remotes, recvs

    def share(src, land, sems):
        x, y, c = _place()
        remotes, recvs = [], []
        for a in range(n):
            cp, rv = _remote(src[a], land[a], land[a], sems, (x, y, 1 - c))
            remotes.append(cp)
            recvs.append(rv)
        return remotes, recvs

    return to_sibling, across_chips, share


def _reduce_begin(grads):
    n = len(grads)
    to_sibling, across_chips, _ = _reduce_plans(n)
    core = lax.axis_index("c").reshape(1).astype(jnp.int32)
    theirs = _exchange("reduce_pair", to_sibling, grads,
                       [lax.empty((g.shape[0] // 2, g.shape[1]), bf16) for g in grads], n)
    pair = [_pair_sum(core, g, t) for g, t in zip(grads, theirs)]
    lands = [lax.empty((N_CHIPS - 1, g.shape[0] // 2, g.shape[1] // N_CHIPS), bf16) for g in grads]
    return _start("reduce_chips", across_chips, pair, lands, 3 * n, pair[0])


def _reduce_end(started, after):
    x, y, c = _place()
    chip = (2 * x + y).reshape(1).astype(jnp.int32)
    pair, others = _wait(started, after)
    _, _, share = _reduce_plans(len(pair))
    mine = [_chip_sum(chip, p, o) for p, o in zip(pair, others)]
    sibs = _exchange("reduce_share", share, mine, [lax.empty(h.shape, f32) for h in mine], len(mine))
    return [jnp.where(c == 0, jnp.concatenate([h, s], axis=0), jnp.concatenate([s, h], axis=0))
            for h, s in zip(mine, sibs)]


def _to_all(src, land, sems):
    x, y, c = _place()
    me = 4 * x + 2 * y + c
    remotes, recvs = [], []
    for k in range(1, N_DEV):
        px, py, pc = _flip(x, (k >> 2) & 1), _flip(y, (k >> 1) & 1), _flip(c, k & 1)
        cp, rv = _remote(src[0], land[0].at[me], land[0].at[4 * px + 2 * py + pc], sems, (px, py, pc))
        remotes.append(cp)
        recvs.append(rv)
    return remotes, recvs


def _gather_small_begin(packed):
    return _start("gather_small", _to_all, [packed], [lax.empty((N_DEV,) + packed.shape, f32)], N_DEV - 1, packed)


def _gather_small_end(started, after):
    (packed,), (others,) = _wait(started, after)
    x, y, c = _place()
    return lax.dynamic_update_slice_in_dim(others, packed[None], 4 * x + 2 * y + c, axis=0)


def _gather_all(packed):
    others = _exchange("gather_all", _to_all, [packed], [lax.empty((N_DEV,) + packed.shape, f32)], N_DEV - 1)[0]
    x, y, c = _place()
    return lax.dynamic_update_slice_in_dim(others, packed[None], 4 * x + 2 * y + c, axis=0)


def _rows8(v):
    return jnp.pad(v[None, :], ((0, 7), (0, 0)))


def _vec_rows(vs):
    return jnp.pad(jnp.stack(vs), ((0, 8 - len(vs)), (0, 0)))


SMALL_ROWS = 224


def _pack_small(conv_vec, conv_dw, pool_vec, pool_w, pre_g, post_g, rel):
    return jnp.concatenate([
        conv_vec, conv_dw, pool_vec, pool_w.reshape(GD, WC),
        _rows8(pre_g).reshape(16, WC), _rows8(post_g).reshape(16, WC),
        jnp.pad(rel, ((0, 0), (0, D - rel.shape[1]))).reshape(16, WC)], axis=0)


def _unpack_small(p):
    conv_vec, pool_vec = p[0:8], p[40:48]
    return dict(
        conv_dw_b=conv_vec[0], conv_ln_g=conv_vec[1], conv_ln_b=conv_vec[2], conv_dw=p[8:8 + CONV_K],
        pool_b=pool_vec[0].reshape(4, GD), pool_scale=pool_vec[1], pool_w=p[48:176].reshape(4, GD, GD),
        pre_norm_g=p[176:192].reshape(8, D)[0], post_norm_g=p[192:208].reshape(8, D)[0],
        rel_bias=p[208:224].reshape(8, D)[:, :2 * MAX_REL + 1])


def _in_proj(h, w_in, after=None):
    return _mm("mm_in", h, w_in, "nn", h.shape[0], NCOL, D, 1024, 1536, D, f32, after=after)


def _layer_fwd(x2, ht, z, lw, BL, SEQ, next_g=None, tgt=None):
    cv, u1 = _conv_fwd(z, lw["dw32"], lw["cvec"], BL, SEQ)
    at = _attn_fwd(z, lw["bm"], BL, SEQ)
    pv = _pool_fwd(z, lw["pw"], lw["pvec"], BL, SEQ)
    out, merged, y, *last = _tail_fwd(z, (cv, at, pv), x2, lw, next_g, tgt)
    return (out, *last), dict(x=x2, ht=ht, z=z, u1=u1, acts=(cv, at, pv), merged=merged, y=y)


def _layer_bwd(dout, sv, lw, BL, SEQ, meanwhile=None):
    T = BL * SEQ
    tail = _tail_bwd(sv["z"], dout, sv["y"], sv["merged"], sv["acts"], lw)
    dz, dacts, dpost = tail[0], tail[1:4], tail[4]
    dws = [g.astype(bf16) for g in tail[5:8]]
    dw_out_t = tail[8].astype(bf16)
    if meanwhile is not None:
        meanwhile(dw_out_t)
    dz, ddw, dcvec = _conv_bwd(sv["z"], sv["u1"], dacts[0], dz, lw["dw32"], lw["cvec"], BL, SEQ)
    dz, dbm = _attn_bwd(sv["z"], dacts[1], dz, lw["bm"], BL, SEQ)
    dz, dpw, dpvec = _pool_bwd(sv["z"], dacts[2], dz, lw["pw"], lw["pvec"], BL, SEQ)
    drel = _bias_table_grad(_bias_colsum(_bias_fold(dbm)))
    small_gather = _gather_small_begin(_pack_small(dcvec, ddw, dpvec, dpw, jnp.zeros((D,), f32), dpost[0], drel))
    dw_in = _mm_dw_in(sv["ht"], dz, small_gather["token"])
    reduction = _reduce_begin([dw_in, dws[0], dws[1], dws[2], dw_out_t])
    dx, dpre = _mm_dx(dz, lw["w_in"], sv["x"], lw["pre_g"], dout, reduction["token"])
    return dx, reduction, small_gather, dpre


BIG = ("w_in", "w_conv_out", "w_attn_out", "w_pool_out", "w_out")
PRE_ROWS = slice(176, 192)


def _layer_shards(w, l):
    return [w[k][l].astype(bf16) for k in BIG] + [w["conv_dw"][l]]


def _side_by_side(g):
    return jnp.transpose(g, (1, 0, 2)).reshape(g.shape[1], N_CHIPS * g.shape[2])


def _layer_weights(w_in, gathered, w, l, bm):
    lw = {k: _side_by_side(g) for k, g in zip(BIG[1:4], gathered[:3])}
    lw["w_in"] = w_in
    lw["w_out"] = gathered[3].reshape(D, D)
    lw["pre_g"] = w["pre_norm_g"][l][None]
    lw["post_g"] = w["post_norm_g"][l][None]
    lw["dw32"] = jnp.pad(_side_by_side(gathered[4]), ((0, 32 - CONV_K), (0, 0)))
    lw["cvec"] = _vec_rows([w["conv_dw_b"][l], w["conv_ln_g"][l], w["conv_ln_b"][l]])
    lw["bm"] = bm
    lw["pw"] = w["pool_w"][l].astype(bf16)
    lw["pvec"] = _vec_rows([w["pool_b"][l].reshape(WC), w["pool_scale"][l]])
    return lw


SMALL = ("pre_norm_g", "post_norm_g", "conv_dw_b", "conv_ln_g", "conv_ln_b", "rel_bias", "pool_w", "pool_b", "pool_scale")
ORDER = ("pre_norm_g", "post_norm_g", "w_in", "conv_dw", "conv_dw_b", "conv_ln_g", "conv_ln_b", "w_conv_out",
         "rel_bias", "w_attn_out", "pool_w", "pool_b", "pool_scale", "w_pool_out", "w_out")


def _pack_small_params(p):
    return jnp.concatenate([
        _pack_small(_vec_rows([p["conv_dw_b"][l], p["conv_ln_g"][l], p["conv_ln_b"][l]]), jnp.zeros((32, WC), f32),
                    _vec_rows([p["pool_b"][l].reshape(WC), p["pool_scale"][l]]), p["pool_w"][l],
                    p["pre_norm_g"][l], p["post_norm_g"][l], p["rel_bias"][l])
        for l in range(DEPTH)], axis=0)


def _unpack_small_params(packed):
    layers = [_unpack_small(packed[l * SMALL_ROWS:(l + 1) * SMALL_ROWS]) for l in range(DEPTH)]
    return {k: jnp.stack([layers[l][k] for l in range(DEPTH)]) for k in layers[0]}


def kernel(x, pre_norm_g, post_norm_g, w_in, conv_dw, conv_dw_b, conv_ln_g, conv_ln_b, w_conv_out, rel_bias, w_attn_out, pool_w, pool_b, pool_scale, w_pool_out, w_out, loss_target, m_pre_norm_g, m_post_norm_g, m_w_in, m_conv_dw, m_conv_dw_b, m_conv_ln_g, m_conv_ln_b, m_w_conv_out, m_rel_bias, m_w_attn_out, m_pool_w, m_pool_b, m_pool_scale, m_w_pool_out, m_w_out, v_pre_norm_g, v_post_norm_g, v_w_in, v_conv_dw, v_conv_dw_b, v_conv_ln_g, v_conv_ln_b, v_w_conv_out, v_rel_bias, v_w_attn_out, v_pool_w, v_pool_b, v_pool_scale, v_w_pool_out, v_w_out):
    BL, SEQ, _ = x.shape
    T = BL * SEQ
    w = dict(pre_norm_g=pre_norm_g, post_norm_g=post_norm_g, w_in=w_in, conv_dw=conv_dw, conv_dw_b=conv_dw_b,
             conv_ln_g=conv_ln_g, conv_ln_b=conv_ln_b, w_conv_out=w_conv_out, rel_bias=rel_bias, w_attn_out=w_attn_out,
             pool_w=pool_w, pool_b=pool_b, pool_scale=pool_scale, w_pool_out=w_pool_out, w_out=w_out)
    m = dict(pre_norm_g=m_pre_norm_g, post_norm_g=m_post_norm_g, w_in=m_w_in, conv_dw=m_conv_dw, conv_dw_b=m_conv_dw_b,
             conv_ln_g=m_conv_ln_g, conv_ln_b=m_conv_ln_b, w_conv_out=m_w_conv_out, rel_bias=m_rel_bias,
             w_attn_out=m_w_attn_out, pool_w=m_pool_w, pool_b=m_pool_b, pool_scale=m_pool_scale,
             w_pool_out=m_w_pool_out, w_out=m_w_out)
    v = dict(pre_norm_g=v_pre_norm_g, post_norm_g=v_post_norm_g, w_in=v_w_in, conv_dw=v_conv_dw, conv_dw_b=v_conv_dw_b,
             conv_ln_g=v_conv_ln_g, conv_ln_b=v_conv_ln_b, w_conv_out=v_w_conv_out, rel_bias=v_rel_bias,
             w_attn_out=v_w_attn_out, pool_w=v_pool_w, pool_b=v_pool_b, pool_scale=v_pool_scale,
             w_pool_out=v_w_pool_out, w_out=v_w_out)

    shards = [_layer_shards(w, l) for l in range(DEPTH)]
    x2 = x.reshape(T, D)
    h0, ht0 = _rms_pre(x2, pre_norm_g[0][None])
    first = _gather_begin("w_in0", shards[0][:1], 1, x2)
    bms = [_bias_matrix(rel_bias[l]) for l in range(DEPTH)]
    packs = [_pack_small_params(p) for p in (w, m, v)]
    w_in0 = _side_by_side(_gather_end(first, shards[0][:1], 1, [ht0, *bms, *packs])[0])
    rest0 = _gather_begin("rest0", shards[0][1:], 4, w_in0)
    all1 = _gather_begin("layer1", shards[1], 5, rest0["token"])
    z0 = _in_proj(h0, w_in0, after=all1["token"])
    lw0 = _layer_weights(w_in0, _gather_end(rest0, shards[0][1:], 4, z0), w, 0, bms[0])
    (out0, h1, ht1), saved0 = _layer_fwd(x2, ht0, z0, lw0, BL, SEQ, next_g=pre_norm_g[1][None])
    gathered1 = _gather_end(all1, shards[1], 5, out0)
    lw1 = _layer_weights(_side_by_side(gathered1[0]), gathered1[1:], w, 1, bms[1])
    (dout, sq), saved1 = _layer_fwd(out0, ht1, _in_proj(h1, lw1["w_in"]), lw1, BL, SEQ, tgt=loss_target.reshape(T, D))
    loss = lax.psum(0.5 * jnp.sum(sq) / float(D), ("x", "y", "c"))

    summed = [None] * DEPTH
    dx1, reduction1, small_gather1, dpre1 = _layer_bwd(dout, saved1, lw1, BL, SEQ)

    def finish_layer1(after):
        summed[1] = _reduce_end(reduction1, after)

    grad_x, reduction0, small_gather0, dpre0 = _layer_bwd(dx1, saved0, lw0, BL, SEQ, meanwhile=finish_layer1)
    summed[0] = _reduce_end(reduction0, grad_x)
    dpre = _sum_slots("sum_small", _gather_all(jnp.concatenate([_rows8(dpre0[0]), _rows8(dpre1[0])], axis=0)))
    gsmall = []
    for l, started in enumerate((small_gather0, small_gather1)):
        g = _sum_slots("sum_small", _gather_small_end(started, dpre))
        gsmall += [g[:PRE_ROWS.start], dpre[8 * l:8 * l + 8].reshape(16, WC), g[PRE_ROWS.stop:]]
    gsmall = jnp.concatenate(gsmall, axis=0)

    grads, deltas, new_m, new_v = {}, {}, {}, {}
    for i, k in enumerate(BIG):
        g = jnp.stack([summed[l][i] for l in range(DEPTH)])
        if k == "w_out":
            g = jnp.transpose(g, (0, 2, 1))
        grads[k] = g
        shape = w[k].shape
        flat2 = lambda a: a.reshape(shape[0] * shape[1], shape[2])
        d_, nm_, nv_ = _adamw("adamw_big", flat2(g), flat2(w[k]), flat2(m[k]), flat2(v[k]))
        deltas[k], new_m[k], new_v[k] = d_.reshape(shape), nm_.reshape(shape), nv_.reshape(shape)

    d_, nm_, nv_ = _adamw("adamw_small", gsmall, *packs)
    gs, ds, ms, vs = (_unpack_small_params(a) for a in (gsmall, d_, nm_, nv_))
    for k in SMALL:
        grads[k], deltas[k], new_m[k], new_v[k] = gs[k], ds[k], ms[k], vs[k]
    chip = 2 * lax.axis_index("x") + lax.axis_index("y")
    g_dw = lax.dynamic_slice_in_dim(gs["conv_dw"], chip * GD, GD, axis=2)
    flat2 = lambda a: a.reshape(DEPTH * CONV_K, GD)
    d_, nm_, nv_ = _adamw("adamw_conv_dw", flat2(g_dw), flat2(conv_dw), flat2(m["conv_dw"]), flat2(v["conv_dw"]))
    grads["conv_dw"] = g_dw
    deltas["conv_dw"], new_m["conv_dw"], new_v["conv_dw"] = (a.reshape(conv_dw.shape) for a in (d_, nm_, nv_))

    return (loss, grad_x.reshape(x.shape), *[grads[k] for k in ORDER], *[deltas[k] for k in ORDER],
            *[new_m[k] for k in ORDER], *[new_v[k] for k in ORDER])
```

```python
import numpy as np
import jax
import jax.numpy as jnp
from jax import lax
from jax.experimental import pallas as pl
from jax.experimental.pallas import tpu as pltpu

f32 = jnp.float32
bf16 = jnp.bfloat16

D = 1024
DEPTH = 2
WC = 512
HEAD_DIM = 64
CHUNK = 64
LEFT_CHUNKS = 8
KEY_PAD = LEFT_CHUNKS * CHUNK
MAX_REL = 256
CONV_K = 31
POOL_WINDOWS = (2, 4, 8, 16)
GD = 128
NCOL = 7680
EPS = 1e-6
NEG_INF = -1e30
COL_A, COL_B, COL_CG, COL_Q, COL_K, COL_V, COL_AG, COL_PI, COL_PG, COL_GM = (
    0, 512, 1024, 1536, 2048, 2560, 3072, 3584, 4096, 4608)

ADAM_LR = 0.001
ADAM_B1 = 0.9
ADAM_B2 = 0.999
ADAM_EPS = 1e-08
ADAM_WD = 0.01
ADAM_STEP = 10

QG = 256
KW = KEY_PAD + QG
BIAS_VARIANTS = KEY_PAD // QG + 1
CT = 128
HALO = 32
PHALO = 16
N_CHIPS = 4
N_DEV = 8
VMEM_LIMIT = 56 * 1024 * 1024
MESH = pl.DeviceIdType.MESH
ANY = pl.BlockSpec(memory_space=pl.ANY)

DZ_BLOCKS = 18
DZ_CONV, DZ_ATTN, DZ_POOL, DZ_GM = 0, 4, 8, 12


def _dz_block(c):
    return c + (c >= 3).astype(jnp.int32) + 2 * (c >= 9).astype(jnp.int32)


def _params(sem=None):
    return pltpu.CompilerParams(dimension_semantics=sem, vmem_limit_bytes=VMEM_LIMIT)


def _sig(x):
    return 1.0 / (1.0 + jnp.exp(-x))


def _dsilu(x, s):
    return s * (1.0 + x * (1.0 - s))


def _colsum(x):
    return jnp.sum(x, axis=0, keepdims=True)


def _rms_pre(x2, g):
    T = x2.shape[0]
    tm = 512

    def body(x_ref, g_ref, h_ref, ht_ref):
        x = x_ref[...]
        r = lax.rsqrt(jnp.mean(x * x, axis=-1, keepdims=True) + EPS)
        h = (x * r) * g_ref[...]
        h_ref[...] = h.astype(bf16)
        ht_ref[...] = h.T.astype(bf16)

    row = pl.BlockSpec((tm, D), lambda i: (i, 0))
    vec = pl.BlockSpec((1, D), lambda i: (0, 0))
    return pl.pallas_call(
        body, grid=(T // tm,), in_specs=[row, vec], out_specs=[row, pl.BlockSpec((D, tm), lambda i: (0, i))],
        out_shape=[jax.ShapeDtypeStruct((T, D), bf16), jax.ShapeDtypeStruct((D, T), bf16)], name="rms_pre",
        compiler_params=_params(("parallel",)))(x2, g)


def _mm(name, a, b, mode, m, n, k, tm, tn, tk, out_dtype, after=None):
    nk = k // tk
    assert m % tm == 0 and n % tn == 0 and k % tk == 0
    if mode == "nn":
        a_spec = pl.BlockSpec((tm, tk), lambda i, j, kk: (i, kk))
        b_spec = pl.BlockSpec((tk, tn), lambda i, j, kk: (kk, j))
        dn = (((1,), (0,)), ((), ()))
    elif mode == "nt":
        a_spec = pl.BlockSpec((tm, tk), lambda i, j, kk: (i, kk))
        b_spec = pl.BlockSpec((tn, tk), lambda i, j, kk: (j, kk))
        dn = (((1,), (1,)), ((), ()))
    else:
        a_spec = pl.BlockSpec((tk, tm), lambda i, j, kk: (kk, i))
        b_spec = pl.BlockSpec((tk, tn), lambda i, j, kk: (kk, j))
        dn = (((0,), (0,)), ((), ()))
    extra = [] if after is None else [after]

    def body(a_ref, b_ref, *rest):
        o_ref, acc_ref = rest[len(extra):]
        p = lax.dot_general(a_ref[...].astype(bf16), b_ref[...].astype(bf16), dn, preferred_element_type=f32)
        if nk == 1:
            o_ref[...] = p.astype(o_ref.dtype)
        else:
            kk = pl.program_id(2)

            @pl.when(kk == 0)
            def _():
                acc_ref[...] = p

            @pl.when(kk > 0)
            def _():
                acc_ref[...] += p

            @pl.when(kk == nk - 1)
            def _():
                o_ref[...] = acc_ref[...].astype(o_ref.dtype)

    acc_shape = (tm, tn) if nk > 1 else (8, 128)
    return pl.pallas_call(
        body, grid=(m // tm, n // tn, nk), in_specs=[a_spec, b_spec] + [ANY] * len(extra),
        out_specs=pl.BlockSpec((tm, tn), lambda i, j, kk: (i, j)),
        out_shape=jax.ShapeDtypeStruct((m, n), out_dtype),
        scratch_shapes=[pltpu.VMEM(acc_shape, f32)], name=name,
        compiler_params=_params(("parallel", "parallel", "arbitrary")))(a, b, *extra)


DZ_SPANS = ((DZ_CONV, 3), (DZ_ATTN, 4), (DZ_POOL, 2), (DZ_GM, 6))


def _mm_dx(dz, w_in, x2, g, dout, after):
    T = dz.shape[1]
    tm = 512

    def body(conv_ref, attn_ref, pool_ref, gm_ref, w_ref, x_ref, g_ref, d_ref, after_ref, dx_ref, dg_ref):
        dh = None
        col = 0
        for ref, (_, blocks) in zip((conv_ref, attn_ref, pool_ref, gm_ref), DZ_SPANS):
            for b in range(blocks):
                p = lax.dot_general(ref[b], w_ref[:, col * WC:(col + 1) * WC], (((1,), (1,)), ((), ())),
                                    preferred_element_type=f32)
                dh = p if dh is None else dh + p
                col += 1
        x = x_ref[...]
        r = lax.rsqrt(jnp.mean(x * x, axis=-1, keepdims=True) + EPS)
        xn = x * r
        dxn = dh * g_ref[...]
        dx_ref[...] = r * (dxn - xn * jnp.mean(dxn * xn, axis=-1, keepdims=True)) + d_ref[...]

        @pl.when(pl.program_id(0) == 0)
        def _():
            dg_ref[...] = jnp.zeros_like(dg_ref)

        dg_ref[...] += _colsum(dh * xn)

    spans = [pl.BlockSpec((blocks, tm, WC), lambda i, first=first, blocks=blocks: (first // blocks, i, 0))
             for first, blocks in DZ_SPANS]
    row = pl.BlockSpec((tm, D), lambda i: (i, 0))
    vec = pl.BlockSpec((1, D), lambda i: (0, 0))
    return pl.pallas_call(
        body, grid=(T // tm,),
        in_specs=spans + [pl.BlockSpec((D, NCOL), lambda i: (0, 0), pipeline_mode=pl.Buffered(1)), row, vec, row, ANY],
        out_specs=[row, vec], out_shape=[jax.ShapeDtypeStruct((T, D), f32), jax.ShapeDtypeStruct((1, D), f32)],
        name="mm_dx", compiler_params=_params(("arbitrary",)))(dz, dz, dz, dz, w_in, x2, g, dout, after)


def _mm_dw_in(ht, dz, after):
    T = dz.shape[1]

    def body(ht_ref, dz_ref, after_ref, o_ref):
        o_ref[...] = jnp.dot(ht_ref[...], dz_ref[...], preferred_element_type=f32).astype(bf16)

    return pl.pallas_call(
        body, grid=(NCOL // WC,),
        in_specs=[pl.BlockSpec((D, T), lambda j: (0, 0), pipeline_mode=pl.Buffered(1)),
                  pl.BlockSpec((None, T, WC), lambda j: (_dz_block(j), 0, 0)), ANY],
        out_specs=pl.BlockSpec((D, WC), lambda j: (0, j)), out_shape=jax.ShapeDtypeStruct((D, NCOL), bf16),
        name="mm_dw_in", compiler_params=_params(("parallel",)))(ht, dz, after)


def _conv_delays():
    return [(8 * a + b, a, b) for b in range(8) for a in range(4) if 8 * a + b < CONV_K]


def _conv_rolls(win):
    return [win if b == 0 else pltpu.roll(win, b, axis=0) for b in range(8)]


def _conv_taps(rolled, dw_ref):
    acc = None
    for d, a, b in _conv_delays():
        term = rolled[b][HALO - 8 * a:HALO - 8 * a + CT, :] * dw_ref[pl.ds(CONV_K - 1 - d, 1), :]
        acc = term if acc is None else acc + term
    return acc


def _conv_fwd(z, dw32, cvec, BL, SEQ):
    T = BL * SEQ
    nct = SEQ // CT

    def body(a_ref, b_ref, cg_ref, dw_ref, vec_ref, o_ref, u1_ref, p_ref):
        p_ref[pl.ds(0, HALO), :] = jnp.zeros((HALO, WC), f32)

        def glu(c, carry):
            r0 = pl.multiple_of(c * CT, CT)
            p_ref[pl.ds(r0 + HALO, CT), :] = a_ref[pl.ds(r0, CT), :] * _sig(b_ref[pl.ds(r0, CT), :])
            return carry

        lax.fori_loop(0, nct, glu, 0)

        def step(c, carry):
            r0 = pl.multiple_of(c * CT, CT)
            u1 = _conv_taps(_conv_rolls(p_ref[pl.ds(r0, CT + HALO), :]), dw_ref) + vec_ref[0:1, :]
            u1_ref[pl.ds(r0, CT), :] = u1
            xc = u1 - jnp.mean(u1, axis=-1, keepdims=True)
            rs = lax.rsqrt(jnp.mean(xc * xc, axis=-1, keepdims=True) + EPS)
            u2 = (xc * rs) * vec_ref[1:2, :] + vec_ref[2:3, :]
            cg = cg_ref[pl.ds(r0, CT), :]
            o_ref[pl.ds(r0, CT), :] = ((u2 * _sig(u2)) * (cg * _sig(cg))).astype(bf16)
            return carry

        lax.fori_loop(0, nct, step, 0)

    def zs(col):
        return pl.BlockSpec((SEQ, WC), lambda b: (b, col // WC))

    seq = pl.BlockSpec((SEQ, WC), lambda b: (b, 0))
    return pl.pallas_call(
        body, grid=(BL,),
        in_specs=[zs(COL_A), zs(COL_B), zs(COL_CG), pl.BlockSpec((32, WC), lambda b: (0, 0)),
                  pl.BlockSpec((8, WC), lambda b: (0, 0))],
        out_specs=[seq, seq],
        out_shape=[jax.ShapeDtypeStruct((T, WC), bf16), jax.ShapeDtypeStruct((T, WC), f32)],
        scratch_shapes=[pltpu.VMEM((SEQ + HALO, WC), f32)], name="conv_fwd",
        compiler_params=_params(("parallel",)))(z, z, z, dw32, cvec)


def _conv_bwd(z, u1, dcv, dz, dw32, cvec, BL, SEQ):
    nct = SEQ // CT

    def body(a_ref, b_ref, cg_ref, u1_ref, dcv_ref, dzin_ref, dw_ref, vec_ref, dz_ref, ddw_ref, dvec_ref,
             p_ref, q_ref, taps_ref):
        @pl.when(pl.program_id(0) == 0)
        def _():
            ddw_ref[...] = jnp.zeros_like(ddw_ref)
            dvec_ref[...] = jnp.zeros_like(dvec_ref)

        p_ref[pl.ds(0, HALO), :] = jnp.zeros((HALO, WC), f32)
        q_ref[pl.ds(SEQ, HALO), :] = jnp.zeros((HALO, WC), f32)

        def glu(c, carry):
            r0 = pl.multiple_of(c * CT, CT)
            p_ref[pl.ds(r0 + HALO, CT), :] = a_ref[pl.ds(r0, CT), :] * _sig(b_ref[pl.ds(r0, CT), :])
            return carry

        lax.fori_loop(0, nct, glu, 0)

        def step(c, carry):
            r0 = pl.multiple_of(c * CT, CT)
            rolled = _conv_rolls(p_ref[pl.ds(r0, CT + HALO), :])
            u1 = u1_ref[pl.ds(r0, CT), :]
            xc = u1 - jnp.mean(u1, axis=-1, keepdims=True)
            rs = lax.rsqrt(jnp.mean(xc * xc, axis=-1, keepdims=True) + EPS)
            nrm = xc * rs
            u2 = nrm * vec_ref[1:2, :] + vec_ref[2:3, :]
            s2 = _sig(u2)
            u3 = u2 * s2
            cg = cg_ref[pl.ds(r0, CT), :]
            scg = _sig(cg)
            dcv_ = dcv_ref[pl.ds(r0, CT), :]
            dz_ref[2, pl.ds(r0, CT), :] = (dcv_ * u3 * _dsilu(cg, scg)).astype(bf16)
            du2 = dcv_ * (cg * scg) * _dsilu(u2, s2)
            dvec_ref[1:2, :] += _colsum(du2 * nrm)
            dvec_ref[2:3, :] += _colsum(du2)
            dn = du2 * vec_ref[1:2, :]
            du1 = rs * (dn - jnp.mean(dn, axis=-1, keepdims=True)
                        - nrm * jnp.mean(dn * nrm, axis=-1, keepdims=True))
            dvec_ref[0:1, :] += _colsum(du1)
            q_ref[pl.ds(r0, CT), :] = du1
            for d, a, b in _conv_delays():
                prod = du1 * rolled[b][HALO - 8 * a:HALO - 8 * a + CT, :]
                taps_ref[CONV_K - 1 - d] += jnp.sum(prod.reshape(CT // 8, 8, WC), axis=0)
            return carry

        taps_ref[...] = jnp.zeros_like(taps_ref)
        lax.fori_loop(0, nct, step, 0)
        for row in range(CONV_K):
            ddw_ref[pl.ds(row, 1), :] += _colsum(taps_ref[row])

        def back(c, carry):
            r0 = pl.multiple_of(c * CT, CT)
            wq = q_ref[pl.ds(r0, CT + HALO), :]
            up = {}
            acc = None
            for d, a, b in _conv_delays():
                if b not in up:
                    up[b] = wq if b == 0 else pltpu.roll(wq, CT + HALO - b, axis=0)
                term = up[b][8 * a:8 * a + CT, :] * dw_ref[pl.ds(CONV_K - 1 - d, 1), :]
                acc = term if acc is None else acc + term
            a_ = a_ref[pl.ds(r0, CT), :]
            sb = _sig(b_ref[pl.ds(r0, CT), :])
            dz_ref[0, pl.ds(r0, CT), :] = (acc * sb).astype(bf16)
            dz_ref[1, pl.ds(r0, CT), :] = (acc * a_ * sb * (1.0 - sb)).astype(bf16)
            return carry

        lax.fori_loop(0, nct, back, 0)

    def zs(col):
        return pl.BlockSpec((SEQ, WC), lambda b: (b, col // WC), pipeline_mode=pl.Buffered(1))

    def const(r):
        return pl.BlockSpec((r, WC), lambda b: (0, 0))

    seq = pl.BlockSpec((SEQ, WC), lambda b: (b, 0), pipeline_mode=pl.Buffered(1))
    return pl.pallas_call(
        body, grid=(BL,),
        in_specs=[zs(COL_A), zs(COL_B), zs(COL_CG), seq, seq, ANY, const(32), const(8)],
        out_specs=[pl.BlockSpec((3, SEQ, WC), lambda b: (DZ_CONV // 3, b, 0)), const(32), const(8)],
        out_shape=[jax.ShapeDtypeStruct(dz.shape, bf16), jax.ShapeDtypeStruct((32, WC), f32),
                   jax.ShapeDtypeStruct((8, WC), f32)],
        scratch_shapes=[pltpu.VMEM((SEQ + HALO, WC), f32), pltpu.VMEM((SEQ + HALO, WC), f32),
                        pltpu.VMEM((CONV_K, 8, WC), f32)],
        input_output_aliases={5: 0}, name="conv_bwd",
        compiler_params=_params(("arbitrary",)))(z, z, z, u1, dcv, dz, dw32, cvec)


def _pool_counts(r0):
    t1 = r0 + 1 + lax.broadcasted_iota(jnp.int32, (CT, 1), 0)
    return [jnp.minimum(t1, w).astype(f32) for w in POOL_WINDOWS]


def _pool_sums(win, forward):
    n = CT + PHALO

    def sh(x, s):
        return pltpu.roll(x, (n - s) if forward else s, axis=0)

    s2 = win + sh(win, 1)
    s4 = s2[:, GD:] + sh(s2[:, GD:], 2)
    s8 = s4[:, GD:] + sh(s4[:, GD:], 4)
    s16 = s8[:, GD:] + sh(s8[:, GD:], 8)
    lo = 0 if forward else PHALO
    return [s[lo:lo + CT, :GD] for s in (s2, s4, s8, s16)]


def _pool_fwd(z, pw, pvec, BL, SEQ):
    T = BL * SEQ
    nct = SEQ // CT

    def body(pi_ref, pg_ref, pw_ref, vec_ref, o_ref, p_ref):
        p_ref[pl.ds(0, PHALO), :] = jnp.zeros((PHALO, WC), f32)

        def fill(c, carry):
            r0 = pl.multiple_of(c * CT, CT)
            p_ref[pl.ds(r0 + PHALO, CT), :] = pi_ref[pl.ds(r0, CT), :]
            return carry

        lax.fori_loop(0, nct, fill, 0)

        def step(c, carry):
            r0 = pl.multiple_of(c * CT, CT)
            sums = _pool_sums(p_ref[pl.ds(r0, CT + PHALO), :], False)
            cnt = _pool_counts(r0)
            pin = pi_ref[pl.ds(r0, CT), :]
            mixed = []
            for g in range(4):
                pooled = sums[g] / cnt[g] - pin[:, g * GD:(g + 1) * GD]
                mixed.append(jnp.dot(pooled.astype(bf16), pw_ref[g], preferred_element_type=f32))
            m0 = jnp.concatenate(mixed, axis=1) + vec_ref[0:1, :]
            pg = pg_ref[pl.ds(r0, CT), :]
            o_ref[pl.ds(r0, CT), :] = ((m0 * vec_ref[1:2, :]) * (pg * _sig(pg))).astype(bf16)
            return carry

        lax.fori_loop(0, nct, step, 0)

    def zs(col):
        return pl.BlockSpec((SEQ, WC), lambda b: (b, col // WC))

    return pl.pallas_call(
        body, grid=(BL,),
        in_specs=[zs(COL_PI), zs(COL_PG), pl.BlockSpec((4, GD, GD), lambda b: (0, 0, 0)),
                  pl.BlockSpec((8, WC), lambda b: (0, 0))],
        out_specs=pl.BlockSpec((SEQ, WC), lambda b: (b, 0)),
        out_shape=jax.ShapeDtypeStruct((T, WC), bf16),
        scratch_shapes=[pltpu.VMEM((SEQ + PHALO, WC), f32)], name="pool_fwd",
        compiler_params=_params(("parallel",)))(z, z, pw, pvec)


def _pool_bwd(z, dpl, dz, pw, pvec, BL, SEQ):
    nct = SEQ // CT

    def body(pi_ref, pg_ref, dpl_ref, dzin_ref, pw_ref, vec_ref, dz_ref, dpw_ref, dvec_ref, p_ref, e_ref, dp_ref):
        @pl.when(pl.program_id(0) == 0)
        def _():
            dpw_ref[...] = jnp.zeros_like(dpw_ref)
            dvec_ref[...] = jnp.zeros_like(dvec_ref)

        p_ref[pl.ds(0, PHALO), :] = jnp.zeros((PHALO, WC), f32)
        e_ref[pl.ds(SEQ, PHALO), :] = jnp.zeros((PHALO, WC), f32)

        def fill(c, carry):
            r0 = pl.multiple_of(c * CT, CT)
            p_ref[pl.ds(r0 + PHALO, CT), :] = pi_ref[pl.ds(r0, CT), :]
            return carry

        lax.fori_loop(0, nct, fill, 0)

        def step(c, carry):
            r0 = pl.multiple_of(c * CT, CT)
            sums = _pool_sums(p_ref[pl.ds(r0, CT + PHALO), :], False)
            cnt = _pool_counts(r0)
            pin = pi_ref[pl.ds(r0, CT), :]
            pooled = [(sums[g] / cnt[g] - pin[:, g * GD:(g + 1) * GD]).astype(bf16) for g in range(4)]
            m0 = jnp.concatenate(
                [jnp.dot(pooled[g], pw_ref[g], preferred_element_type=f32) for g in range(4)], axis=1) + vec_ref[0:1, :]
            scale = vec_ref[1:2, :]
            pg = pg_ref[pl.ds(r0, CT), :]
            spg = _sig(pg)
            dpl_ = dpl_ref[pl.ds(r0, CT), :]
            dmixed = dpl_ * (pg * spg)
            dz_ref[1, pl.ds(r0, CT), :] = (dpl_ * (m0 * scale) * _dsilu(pg, spg)).astype(bf16)
            dvec_ref[1:2, :] += _colsum(dmixed * m0)
            dm0 = dmixed * scale
            dvec_ref[0:1, :] += _colsum(dm0)
            dps, es = [], []
            for g in range(4):
                dm0g = dm0[:, g * GD:(g + 1) * GD].astype(bf16)
                dpw_ref[g] += lax.dot_general(pooled[g], dm0g, (((0,), (0,)), ((), ())), preferred_element_type=f32)
                dpg = lax.dot_general(dm0g, pw_ref[g], (((1,), (1,)), ((), ())), preferred_element_type=f32)
                dps.append(dpg)
                es.append(dpg / cnt[g])
            dp_ref[pl.ds(r0, CT), :] = jnp.concatenate(dps, axis=1)
            e_ref[pl.ds(r0, CT), :] = jnp.concatenate(es, axis=1)
            return carry

        lax.fori_loop(0, nct, step, 0)

        def back(c, carry):
            r0 = pl.multiple_of(c * CT, CT)
            fs = _pool_sums(e_ref[pl.ds(r0, CT + PHALO), :], True)
            dz_ref[0, pl.ds(r0, CT), :] = (jnp.concatenate(fs, axis=1) - dp_ref[pl.ds(r0, CT), :]).astype(bf16)
            return carry

        lax.fori_loop(0, nct, back, 0)

    def zs(col):
        return pl.BlockSpec((SEQ, WC), lambda b: (b, col // WC))

    return pl.pallas_call(
        body, grid=(BL,),
        in_specs=[zs(COL_PI), zs(COL_PG), pl.BlockSpec((SEQ, WC), lambda b: (b, 0)), ANY,
                  pl.BlockSpec((4, GD, GD), lambda b: (0, 0, 0)), pl.BlockSpec((8, WC), lambda b: (0, 0))],
        out_specs=[pl.BlockSpec((2, SEQ, WC), lambda b: (DZ_POOL // 2, b, 0)),
                   pl.BlockSpec((4, GD, GD), lambda b: (0, 0, 0)), pl.BlockSpec((8, WC), lambda b: (0, 0))],
        out_shape=[jax.ShapeDtypeStruct(dz.shape, bf16), jax.ShapeDtypeStruct((4, GD, GD), f32),
                   jax.ShapeDtypeStruct((8, WC), f32)],
        scratch_shapes=[pltpu.VMEM((SEQ + PHALO, WC), f32), pltpu.VMEM((SEQ + PHALO, WC), f32),
                        pltpu.VMEM((SEQ, WC), f32)],
        input_output_aliases={3: 0}, name="pool_bwd",
        compiler_params=_params(("arbitrary",)))(z, z, dpl, dz, pw, pvec)


def _attn_prologue(q_ref, k_ref, v_ref, qs0, qs1, kp, vp, SEQ):
    head0 = lax.broadcasted_iota(jnp.int32, (1, 2 * HEAD_DIM), 1) < HEAD_DIM
    kp[pl.ds(0, KEY_PAD), :] = jnp.zeros((KEY_PAD, 2 * HEAD_DIM), bf16)
    vp[pl.ds(0, KEY_PAD), :] = jnp.zeros((KEY_PAD, 2 * HEAD_DIM), bf16)

    def fill(g, carry):
        r0 = pl.multiple_of(g * QG, QG)
        q = q_ref[pl.ds(r0, QG), :] * (HEAD_DIM ** -0.5)
        qs0[pl.ds(r0, QG), :] = jnp.where(head0, q, 0.0).astype(bf16)
        qs1[pl.ds(r0, QG), :] = jnp.where(head0, 0.0, q).astype(bf16)
        kp[pl.ds(r0 + KEY_PAD, QG), :] = k_ref[pl.ds(r0, QG), :].astype(bf16)
        vp[pl.ds(r0 + KEY_PAD, QG), :] = v_ref[pl.ds(r0, QG), :].astype(bf16)
        return carry

    lax.fori_loop(0, SEQ // QG, fill, 0)
    return head0


def _attn_weights(qh, kw, bias):
    s = lax.dot_general(qh, kw, (((1,), (1,)), ((), ())), preferred_element_type=f32) + bias
    e = jnp.exp(s - jnp.max(s, axis=-1, keepdims=True))
    return e, 1.0 / jnp.sum(e, axis=-1, keepdims=True)


def _attn_fwd(z, bm, BL, SEQ):
    T = BL * SEQ
    W2 = 2 * HEAD_DIM

    def body(q_ref, k_ref, v_ref, ag_ref, bm_ref, o_ref, qs0, qs1, kp, vp):
        head0 = _attn_prologue(q_ref, k_ref, v_ref, qs0, qs1, kp, vp, SEQ)

        def group(g, carry):
            r0 = pl.multiple_of(g * QG, QG)
            kw = kp[pl.ds(r0, KW), :]
            vw = vp[pl.ds(r0, KW), :]
            variant = jnp.minimum(g, BIAS_VARIANTS - 1)
            outs = []
            for hh, qs in enumerate((qs0, qs1)):
                e, inv = _attn_weights(qs[pl.ds(r0, QG), :], kw, bm_ref[variant, hh])
                outs.append(jnp.dot(e.astype(bf16), vw, preferred_element_type=f32) * inv)
            o = jnp.where(head0, outs[0], outs[1])
            ag = ag_ref[pl.ds(r0, QG), :]
            o_ref[pl.ds(r0, QG), :] = (o * (ag * _sig(ag))).astype(bf16)
            return carry

        lax.fori_loop(0, SEQ // QG, group, 0, unroll=8)

    def zs(col):
        return pl.BlockSpec((SEQ, W2), lambda b, hp: (b, col // W2 + hp))

    return pl.pallas_call(
        body, grid=(BL, WC // W2),
        in_specs=[zs(COL_Q), zs(COL_K), zs(COL_V), zs(COL_AG),
                  pl.BlockSpec((BIAS_VARIANTS, 2, QG, KW), lambda b, hp: (0, hp, 0, 0))],
        out_specs=pl.BlockSpec((SEQ, W2), lambda b, hp: (b, hp)),
        out_shape=jax.ShapeDtypeStruct((T, WC), bf16),
        scratch_shapes=[pltpu.VMEM((SEQ, W2), bf16), pltpu.VMEM((SEQ, W2), bf16),
                        pltpu.VMEM((SEQ + KEY_PAD, W2), bf16), pltpu.VMEM((SEQ + KEY_PAD, W2), bf16)],
        name="attn_fwd", compiler_params=_params(("parallel", "parallel")))(z, z, z, z, bm)


def _attn_bwd(z, dat, dz, bm, BL, SEQ):
    W2 = 2 * HEAD_DIM

    def body(q_ref, k_ref, v_ref, ag_ref, dat_ref, dzin_ref, bm_ref, dz_ref, dbm_ref, qs0, qs1, kp, vp, dka, dva):
        @pl.when(pl.program_id(1) == 0)
        def _():
            dbm_ref[...] = jnp.zeros_like(dbm_ref)

        head0 = _attn_prologue(q_ref, k_ref, v_ref, qs0, qs1, kp, vp, SEQ)
        dka[...] = jnp.zeros_like(dka)
        dva[...] = jnp.zeros_like(dva)

        def group(g, carry):
            r0 = pl.multiple_of(g * QG, QG)
            kw = kp[pl.ds(r0, KW), :]
            vw = vp[pl.ds(r0, KW), :]
            variant = jnp.minimum(g, BIAS_VARIANTS - 1)
            ag = ag_ref[pl.ds(r0, QG), :]
            do = dat_ref[pl.ds(r0, QG), :] * (ag * _sig(ag))
            outs, dqs = [], []
            for hh, qs in enumerate((qs0, qs1)):
                qh = qs[pl.ds(r0, QG), :]
                e, inv = _attn_weights(qh, kw, bm_ref[variant, hh])
                eb = e.astype(bf16)
                outs.append(jnp.dot(eb, vw, preferred_element_type=f32) * inv)
                doh = (jnp.where(head0, do, 0.0) if hh == 0 else jnp.where(head0, 0.0, do)) * inv
                doh = doh.astype(bf16)
                dp = lax.dot_general(doh, vw, (((1,), (1,)), ((), ())), preferred_element_type=f32)
                ds_ = e * (dp - jnp.sum(e * dp, axis=-1, keepdims=True) * inv)
                dbm_ref[hh] += ds_
                dsb = ds_.astype(bf16)
                dqs.append(jnp.dot(dsb, kw, preferred_element_type=f32))
                dka[pl.ds(r0, KW), :] += lax.dot_general(dsb, qh, (((0,), (0,)), ((), ())), preferred_element_type=f32)
                dva[pl.ds(r0, KW), :] += lax.dot_general(eb, doh, (((0,), (0,)), ((), ())), preferred_element_type=f32)
            o = jnp.where(head0, outs[0], outs[1])
            dq = jnp.where(head0, dqs[0], dqs[1]) * (HEAD_DIM ** -0.5)
            dz_ref[0, pl.ds(r0, QG), :] = dq.astype(bf16)
            dz_ref[3, pl.ds(r0, QG), :] = (dat_ref[pl.ds(r0, QG), :] * o * _dsilu(ag, _sig(ag))).astype(bf16)
            return carry

        lax.fori_loop(0, SEQ // QG, group, 0, unroll=8)

        def flush(g, carry):
            r0 = pl.multiple_of(g * QG, QG)
            dz_ref[1, pl.ds(r0, QG), :] = dka[pl.ds(r0 + KEY_PAD, QG), :].astype(bf16)
            dz_ref[2, pl.ds(r0, QG), :] = dva[pl.ds(r0 + KEY_PAD, QG), :].astype(bf16)
            return carry

        lax.fori_loop(0, SEQ // QG, flush, 0)

    def zs(col):
        return pl.BlockSpec((SEQ, W2), lambda hp, b: (b, col // W2 + hp))

    return pl.pallas_call(
        body, grid=(WC // W2, BL),
        in_specs=[zs(COL_Q), zs(COL_K), zs(COL_V), zs(COL_AG), pl.BlockSpec((SEQ, W2), lambda hp, b: (b, hp)), ANY,
                  pl.BlockSpec((BIAS_VARIANTS, 2, QG, KW), lambda hp, b: (0, hp, 0, 0))],
        out_specs=[pl.BlockSpec((4, SEQ, W2), lambda hp, b: (DZ_ATTN // 4, b, hp)),
                   pl.BlockSpec((2, QG, KW), lambda hp, b: (hp, 0, 0))],
        out_shape=[jax.ShapeDtypeStruct(dz.shape, bf16), jax.ShapeDtypeStruct((8, QG, KW), f32)],
        scratch_shapes=[pltpu.VMEM((SEQ, W2), bf16), pltpu.VMEM((SEQ, W2), bf16),
                        pltpu.VMEM((SEQ + KEY_PAD, W2), bf16), pltpu.VMEM((SEQ + KEY_PAD, W2), bf16),
                        pltpu.VMEM((SEQ + KEY_PAD, W2), f32), pltpu.VMEM((SEQ + KEY_PAD, W2), f32)],
        input_output_aliases={5: 0}, name="attn_bwd",
        compiler_params=_params(("parallel", "arbitrary")))(z, z, z, z, dat, dz, bm)


BIAS_TOP = KEY_PAD + MAX_REL + QG - 1


def _bias_matrix(table):
    n = 2 * MAX_REL
    wd = QG + KW
    e = jnp.concatenate([jnp.broadcast_to(table[:, n:], (8, BIAS_TOP - n + 1)), table[:, n - 1:BIAS_TOP - wd + 1:-1],
                         jnp.zeros((8, 1), f32)], axis=1)
    flat = jnp.broadcast_to(e[:, None, :], (8, QG, wd)).reshape(8, QG * wd)
    skew = flat[:, :QG * (wd - 1)].reshape(8, QG, wd - 1)
    vals = skew[:, :, QG - 1:QG - 1 + KW]
    r = np.arange(QG)[:, None] // CHUNK
    j = np.arange(KW)[None, :]
    band = (j // CHUNK >= r) & (j // CHUNK <= r + LEFT_CHUNKS)
    keep = np.stack([band & (j >= KEY_PAD - v * QG) for v in range(BIAS_VARIANTS)])
    return jnp.where(jnp.asarray(keep)[:, None], vals[None], NEG_INF)


def _bias_fold(dbm):
    wd = QG + KW
    placed = jnp.pad(dbm, ((0, 0), (0, 0), (QG - 1, 0))).reshape(8, QG * (wd - 1))
    return jnp.pad(placed, ((0, 0), (0, QG))).reshape(8, QG, wd)


def _bias_colsum(folded):
    width = folded.shape[2]

    def body(x_ref, o_ref):
        for h in range(8):
            o_ref[pl.ds(h, 1), :] = _colsum(x_ref[h])

    return pl.pallas_call(body, out_shape=jax.ShapeDtypeStruct((8, width), f32), name="bias_colsum",
                          compiler_params=_params())(folded)


def _bias_table_grad(colsum):
    n = 2 * MAX_REL
    wd = QG + KW
    clipped = jnp.sum(colsum[:, :BIAS_TOP - n + 1], axis=1, keepdims=True)
    return jnp.concatenate([jnp.zeros((8, BIAS_TOP - wd + 2), f32), colsum[:, wd - 2:BIAS_TOP - n:-1], clipped], axis=1)


GATE_SPAN = 3 * WC
TAIL_ROWS = 3 * WC + D


def _gate_specs(tm):
    return [pl.BlockSpec((tm, GATE_SPAN), lambda i: (i, COL_GM // GATE_SPAN)),
            pl.BlockSpec((tm, GATE_SPAN), lambda i: (i, COL_GM // GATE_SPAN + 1))]


def _gate_block(ga_ref, gb_ref, branch, half):
    k = 2 * branch + half
    ref, k = (ga_ref, k) if k < 3 else (gb_ref, k - 3)
    return _sig(ref[:, k * WC:(k + 1) * WC])


def _resident(shape):
    return pl.BlockSpec(shape, lambda i: (0,) * len(shape), pipeline_mode=pl.Buffered(1))


def _tail_fwd(z, acts, x2, lw, next_g=None, tgt=None):
    T = z.shape[0]
    tm = 256
    with_loss = tgt is not None
    assert with_loss != (next_g is not None)

    def body(cv_ref, at_ref, pv_ref, ga_ref, gb_ref, x_ref, wc_ref, wa_ref, wp_ref, wo_ref, g_ref, *rest):
        out_ref, merged_ref, y_ref = rest[1:4]
        ys = [jnp.dot(a[...], w[...], preferred_element_type=f32)
              for a, w in ((cv_ref, wc_ref), (at_ref, wa_ref), (pv_ref, wp_ref))]
        halves = []
        for half in range(2):
            cols = slice(half * WC, (half + 1) * WC)
            halves.append(sum(_gate_block(ga_ref, gb_ref, br, half) * ys[br][:, cols] for br in range(3)))
        merged = jnp.concatenate(halves, axis=1).astype(bf16)
        merged_ref[...] = merged
        y = jnp.dot(merged, wo_ref[...], preferred_element_type=f32)
        y_ref[...] = y
        r = lax.rsqrt(jnp.mean(y * y, axis=-1, keepdims=True) + EPS)
        out = x_ref[...] + (y * r) * g_ref[...]
        if with_loss:
            sq_ref = rest[4]
            e = out - rest[0][...]
            out_ref[...] = e / float(D)

            @pl.when(pl.program_id(0) == 0)
            def _():
                sq_ref[...] = jnp.zeros_like(sq_ref)

            sq_ref[...] += _colsum(e * e)
        else:
            out_ref[...] = out
            rn = lax.rsqrt(jnp.mean(out * out, axis=-1, keepdims=True) + EPS)
            h = (out * rn) * rest[0][...]
            rest[4][...] = h.astype(bf16)
            rest[5][...] = h.T.astype(bf16)

    act = pl.BlockSpec((tm, WC), lambda i: (i, 0))
    row = pl.BlockSpec((tm, D), lambda i: (i, 0))
    vec = pl.BlockSpec((1, D), lambda i: (0, 0))
    if with_loss:
        last_in, last_specs, last_shapes = tgt, [row, [vec]], [jax.ShapeDtypeStruct((1, D), f32)]
    else:
        last_in, last_specs = next_g, [vec, [row, pl.BlockSpec((D, tm), lambda i: (0, i))]]
        last_shapes = [jax.ShapeDtypeStruct((T, D), bf16), jax.ShapeDtypeStruct((D, T), bf16)]
    return pl.pallas_call(
        body, grid=(T // tm,),
        in_specs=[act, act, act] + _gate_specs(tm) + [row, _resident((WC, D)), _resident((WC, D)), _resident((WC, D)),
                                                      _resident((D, D)), _resident((1, D)), last_specs[0]],
        out_specs=[row, row, row] + last_specs[1],
        out_shape=[jax.ShapeDtypeStruct((T, D), f32), jax.ShapeDtypeStruct((T, D), bf16), jax.ShapeDtypeStruct((T, D), f32)]
        + last_shapes,
        name="tail_fwd", compiler_params=_params(("arbitrary",)))(
            *acts, z, z, x2, lw["w_conv_out"], lw["w_attn_out"], lw["w_pool_out"], lw["w_out"], lw["post_g"], last_in)


def _tail_bwd(z, dout, y, merged, acts, lw):
    T = z.shape[0]
    tm = 256
    nt = (((1,), (1,)), ((), ()))
    tn = (((0,), (0,)), ((), ()))

    def body(d_ref, y_ref, m_ref, cv_ref, at_ref, pv_ref, ga_ref, gb_ref, wc_ref, wa_ref, wp_ref, wo_ref, g_ref,
             dz_ref, dcv_ref, dat_ref, dpv_ref, dg_ref, dw_ref):
        @pl.when(pl.program_id(0) == 0)
        def _():
            dg_ref[...] = jnp.zeros_like(dg_ref)
            dw_ref[...] = jnp.zeros_like(dw_ref)

        y = y_ref[...]
        d = d_ref[...]
        r = lax.rsqrt(jnp.mean(y * y, axis=-1, keepdims=True) + EPS)
        yn = y * r
        dyn = d * g_ref[...]
        dy = (r * (dyn - yn * jnp.mean(dyn * yn, axis=-1, keepdims=True))).astype(bf16)
        dg_ref[...] += _colsum(d * yn)
        dw_ref[pl.ds(3 * WC, D), :] += lax.dot_general(dy, m_ref[...], tn, preferred_element_type=f32)
        dmerged = lax.dot_general(dy, wo_ref[...], nt, preferred_element_type=f32)
        for br, (a_ref, w_ref, da_ref) in enumerate(((cv_ref, wc_ref, dcv_ref), (at_ref, wa_ref, dat_ref),
                                                     (pv_ref, wp_ref, dpv_ref))):
            yb = jnp.dot(a_ref[...], w_ref[...], preferred_element_type=f32)
            halves = []
            for half in range(2):
                cols = slice(half * WC, (half + 1) * WC)
                s = _gate_block(ga_ref, gb_ref, br, half)
                dm = dmerged[:, cols]
                halves.append((dm * s).astype(bf16))
                dz_ref[2 * br + half] = (dm * yb[:, cols] * s * (1.0 - s)).astype(bf16)
            dyb = jnp.concatenate(halves, axis=1)
            da_ref[...] = lax.dot_general(dyb, w_ref[...], nt, preferred_element_type=f32)
            dw_ref[pl.ds(br * WC, WC), :] += lax.dot_general(a_ref[...], dyb, tn, preferred_element_type=f32)

    act = pl.BlockSpec((tm, WC), lambda i: (i, 0))
    row = pl.BlockSpec((tm, D), lambda i: (i, 0))

    def whole(shape):
        return pl.BlockSpec(shape, lambda i: (0, 0))

    return pl.pallas_call(
        body, grid=(T // tm,),
        in_specs=[row, row, row, act, act, act] + _gate_specs(tm) + [_resident((WC, D)), _resident((WC, D)), _resident((WC, D)),
                                                                     _resident((D, D)), _resident((1, D))],
        out_specs=[pl.BlockSpec((6, tm, WC), lambda i: (DZ_GM // 6, i, 0)), act, act, act, whole((1, D)),
                   whole((TAIL_ROWS, D))],
        out_shape=[jax.ShapeDtypeStruct((DZ_BLOCKS, T, WC), bf16)] + [jax.ShapeDtypeStruct((T, WC), f32)] * 3
        + [jax.ShapeDtypeStruct((1, D), f32), jax.ShapeDtypeStruct((TAIL_ROWS, D), f32)],
        name="tail_bwd", compiler_params=_params(("arbitrary",)))(
            dout, y, merged, *acts, z, z, lw["w_conv_out"], lw["w_attn_out"], lw["w_pool_out"], lw["w_out"], lw["post_g"])


def _adamw(name, g, w, m, v):
    R, C = w.shape
    tr = R
    for cand in (512, 256, 248, 128, 64, 32, 16, 8):
        if R % cand == 0 and cand * C * 4 <= 2 * 1024 * 1024:
            tr = cand
            break
    c1 = 1.0 - ADAM_B1
    c2 = 1.0 - ADAM_B2
    bc1 = 1.0 - ADAM_B1 ** ADAM_STEP
    bc2 = 1.0 - ADAM_B2 ** ADAM_STEP

    def body(g_ref, w_ref, m_ref, v_ref, d_ref, nm_ref, nv_ref):
        g_ = g_ref[...]
        nm = ADAM_B1 * m_ref[...] + c1 * g_
        nv = ADAM_B2 * v_ref[...] + c2 * (g_ * g_)
        nm_ref[...] = nm
        nv_ref[...] = nv
        d_ref[...] = -ADAM_LR * ((nm / bc1) / (jnp.sqrt(nv / bc2) + ADAM_EPS) + ADAM_WD * w_ref[...])

    spec = pl.BlockSpec((tr, C), lambda i: (i, 0))
    return pl.pallas_call(
        body, grid=(R // tr,), in_specs=[spec] * 4, out_specs=[spec] * 3,
        out_shape=[jax.ShapeDtypeStruct((R, C), f32)] * 3, name=name,
        compiler_params=_params(("parallel",)))(g, w, m, v)


def _sum_slots(name, parts):
    _, R, C = parts.shape
    tr = R
    for cand in (256, 128, 64, 32, 16, 8):
        if R % cand == 0 and cand * C * 4 * N_DEV <= 8 * 1024 * 1024:
            tr = cand
            break

    def body(p_ref, o_ref):
        acc = p_ref[0].astype(f32)
        for s in range(1, N_DEV):
            acc = acc + p_ref[s].astype(f32)
        o_ref[...] = acc

    return pl.pallas_call(
        body, grid=(R // tr,), in_specs=[pl.BlockSpec((N_DEV, tr, C), lambda i: (0, i, 0))],
        out_specs=pl.BlockSpec((tr, C), lambda i: (i, 0)), out_shape=jax.ShapeDtypeStruct((R, C), f32),
        name=name, compiler_params=_params(("parallel",)))(parts)


def _row_tile(rows, row_bytes, budget):
    for cand in (512, 256, 128, 64, 32, 16):
        if rows % cand == 0 and cand * row_bytes <= budget:
            return cand
    return rows


def _pair_sum(core, g, theirs):
    R2, C4 = theirs.shape
    tr = _row_tile(R2, C4 * 2, 2 * 1024 * 1024)
    nb = R2 // tr

    def body(core_ref, g_ref, t_ref, o_ref):
        o_ref[...] = (g_ref[...].astype(f32) + t_ref[...].astype(f32)).astype(bf16)

    return pl.pallas_call(
        body,
        grid_spec=pltpu.PrefetchScalarGridSpec(
            num_scalar_prefetch=1, grid=(nb,),
            in_specs=[pl.BlockSpec((tr, C4), lambda i, core_ref: (core_ref[0] * nb + i, 0)),
                      pl.BlockSpec((tr, C4), lambda i, core_ref: (i, 0))],
            out_specs=pl.BlockSpec((tr, C4), lambda i, core_ref: (i, 0))),
        out_shape=jax.ShapeDtypeStruct((R2, C4), bf16), name="pair_sum",
        compiler_params=_params(("parallel",)))(core, g, theirs)


def _chip_sum(chip, mine, others):
    _, R2, C = others.shape
    tr = _row_tile(R2, C * 4, 1024 * 1024)

    def body(chip_ref, m_ref, o_ref, out_ref):
        acc = m_ref[...].astype(f32)
        for s in range(N_CHIPS - 1):
            acc = acc + o_ref[s].astype(f32)
        out_ref[...] = acc

    return pl.pallas_call(
        body,
        grid_spec=pltpu.PrefetchScalarGridSpec(
            num_scalar_prefetch=1, grid=(R2 // tr,),
            in_specs=[pl.BlockSpec((tr, C), lambda i, chip_ref: (i, chip_ref[0])),
                      pl.BlockSpec((N_CHIPS - 1, tr, C), lambda i, chip_ref: (0, i, 0))],
            out_specs=pl.BlockSpec((tr, C), lambda i, chip_ref: (i, 0))),
        out_shape=jax.ShapeDtypeStruct((R2, C), f32), name="chip_sum",
        compiler_params=_params(("parallel",)))(chip, mine, others)


def _place():
    x, y, c = lax.axis_index("x"), lax.axis_index("y"), lax.axis_index("c")
    return x, y, c


def _flip(v, bit):
    return 1 - v if bit else v


CHIP_FLIPS = ((1, 0), (0, 1), (1, 1))


class _Sems:
    def __init__(self, send, recv):
        self.send, self.recv = send, recv
        self.pairs = 0

    def pair(self):
        k = self.pairs
        self.pairs += 1
        return self.send.at[k], self.recv.at[k]


def _remote(src, dst, lands, sems, to):
    s, r = sems.pair()
    copy = pltpu.make_async_remote_copy(src_ref=src, dst_ref=dst, send_sem=s, recv_sem=r, device_id=to, device_id_type=MESH)
    wait = pltpu.make_async_remote_copy(src_ref=lands, dst_ref=lands, send_sem=s, recv_sem=r, device_id=to, device_id_type=MESH)
    return copy, wait


def _exchange(name, build, srcs, lands, n_remote):
    n_s, n_l = len(srcs), len(lands)

    def body(*refs):
        send, recv = refs[n_s + 2 * n_l:]
        remotes, recvs = build(refs[:n_s], refs[n_s + n_l:n_s + 2 * n_l], _Sems(send, recv))
        for cp in remotes:
            cp.start()
        for rv in recvs:
            rv.wait_recv()
        for cp in remotes:
            cp.wait_send()

    return pl.pallas_call(
        body, in_specs=[ANY] * (n_s + n_l), out_specs=[ANY] * n_l,
        out_shape=[jax.ShapeDtypeStruct(t.shape, t.dtype) for t in lands],
        scratch_shapes=[pltpu.SemaphoreType.DMA((n_remote,)), pltpu.SemaphoreType.DMA((n_remote,))],
        input_output_aliases={n_s + i: i for i in range(n_l)}, name=name)(*srcs, *lands)


HBM = pl.BlockSpec(memory_space=pltpu.HBM)
SEMS = pl.BlockSpec(memory_space=pltpu.SEMAPHORE)
DATAFLOW = pltpu.SideEffectType.DATAFLOW_SIDE_EFFECTING


def _start(name, build, srcs, lands, n_remote, after):
    n_s, n_l = len(srcs), len(lands)

    def body(*refs):
        send, recv = refs[n_s + n_l + 1], refs[n_s + n_l + 2]
        remotes, _ = build(refs[:n_s], refs[n_s:n_s + n_l], _Sems(send, recv))
        for cp in remotes:
            cp.start()
        refs[-1][...] = jnp.zeros((8, 128), f32)

    arrays = [pltpu.with_memory_space_constraint(a, pltpu.HBM) for a in (*srcs, *lands)]
    out = pl.pallas_call(
        body, name=name, in_specs=[HBM] * (n_s + n_l) + [ANY],
        out_specs=(SEMS, SEMS, *[HBM] * (n_s + n_l), pl.BlockSpec(memory_space=pltpu.VMEM)),
        out_shape=(pltpu.SemaphoreType.DMA((n_remote,)), pltpu.SemaphoreType.DMA((n_remote,)),
                   *[pltpu.HBM(a.shape, a.dtype) for a in arrays], jax.ShapeDtypeStruct((8, 128), f32)),
        input_output_aliases={i: 2 + i for i in range(n_s + n_l)},
        compiler_params=pltpu.CompilerParams(has_side_effects=DATAFLOW))(*arrays, after)
    return dict(name=name, build=build, sems=out[:2], srcs=out[2:2 + n_s], lands=out[2 + n_s:2 + n_s + n_l], token=out[-1])


def _wait(started, after):
    srcs, lands, build = started["srcs"], started["lands"], started["build"]
    n_s, n_l = len(srcs), len(lands)
    after = list(after) if isinstance(after, (list, tuple)) else [after]

    def body(*refs):
        send, recv = refs[n_s + n_l], refs[n_s + n_l + 1]
        remotes, recvs = build(refs[:n_s], refs[n_s:n_s + n_l], _Sems(send, recv))
        for rv in recvs:
            rv.wait_recv()
        for cp in remotes:
            cp.wait_send()

    out = pl.pallas_call(
        body, name=started["name"] + "_wait", in_specs=[HBM] * (n_s + n_l) + [SEMS, SEMS] + [ANY] * len(after),
        out_specs=[HBM] * (n_s + n_l), out_shape=[pltpu.HBM(a.shape, a.dtype) for a in (*srcs, *lands)],
        input_output_aliases={i: i for i in range(n_s + n_l)},
        compiler_params=pltpu.CompilerParams(has_side_effects=DATAFLOW))(*srcs, *lands, *started["sems"], *after)
    return out[:n_s], out[n_s:]


def _gather_plans(n_split, n_all):
    def over_ici(src, land, sems):
        x, y, c = _place()
        chip = 2 * x + y
        remotes, recvs = [], []
        for a in range(n_all):
            for fx, fy in CHIP_FLIPS:
                px, py = _flip(x, fx), _flip(y, fy)
                if a < n_split:
                    r2 = src[a].shape[0] // 2
                    rows = pl.ds(c * r2, r2)
                    cp, rv = _remote(src[a].at[rows], land[a].at[chip, rows], land[a].at[2 * px + py, rows], sems, (px, py, c))
                else:
                    cp, rv = _remote(src[a], land[a].at[chip], land[a].at[2 * px + py], sems, (px, py, c))
                remotes.append(cp)
                recvs.append(rv)
        return remotes, recvs

    def over_d2d(src, land, sems):
        x, y, c = _place()
        remotes, recvs = [], []
        for a in range(n_split):
            r2 = land[a].shape[1] // 2
            for fx, fy in CHIP_FLIPS:
                owner = 2 * _flip(x, fx) + _flip(y, fy)
                mine = land[a].at[owner, pl.ds(c * r2, r2)]
                cp, rv = _remote(mine, mine, land[a].at[owner, pl.ds((1 - c) * r2, r2)], sems, (x, y, 1 - c))
                remotes.append(cp)
                recvs.append(rv)
        return remotes, recvs

    return over_ici, over_d2d


def _gather_begin(tag, shards, n_split, after):
    over_ici, _ = _gather_plans(n_split, len(shards))
    lands = [lax.empty((N_CHIPS,) + s.shape, s.dtype) for s in shards]
    return _start("gather_ici_" + tag, over_ici, shards, lands, 3 * len(shards), after)


def _gather_end(started, shards, n_split, after):
    _, over_d2d = _gather_plans(n_split, len(shards))
    lands = _exchange("gather_d2d", over_d2d, [], _wait(started, after)[1], 3 * n_split)
    chip = 2 * lax.axis_index("x") + lax.axis_index("y")
    return [lax.dynamic_update_slice_in_dim(g, s[None], chip, axis=0) for g, s in zip(lands, shards)]


def _reduce_plans(n):
    def to_sibling(src, land, sems):
        x, y, c = _place()
        remotes, recvs = [], []
        for a in range(n):
            r2 = src[a].shape[0] // 2
            cp, rv = _remote(src[a].at[pl.ds((1 - c) * r2, r2), :], land[a], land[a], sems, (x, y, 1 - c))
            remotes.append(cp)
            recvs.append(rv)
        return remotes, recvs

    def across_chips(src, land, sems):
        x, y, c = _place()
        remotes, recvs = [], []
        for a in range(n):
            cw = src[a].shape[1] // N_CHIPS
            for k, (fx, fy) in enumerate(CHIP_FLIPS):
                px, py = _flip(x, fx), _flip(y, fy)
                cp, rv = _remote(src[a].at[:, pl.ds((2 * px + py) * cw, cw)], land[a].at[k], land[a].at[k], sems, (px, py, c))
                remotes.append(cp)
                recvs.append(rv)
        return remotes, recvs

    def share(src, land, sems):
        x, y, c = _place()
        remotes, recvs = [], []
        for a in range(n):
            cp, rv = _remote(src[a], land[a], land[a], sems, (x, y, 1 - c))
            remotes.append(cp)
            recvs.append(rv)
        return remotes, recvs

    return to_sibling, across_chips, share


def _reduce_begin(grads):
    n = len(grads)
    to_sibling, across_chips, _ = _reduce_plans(n)
    core = lax.axis_index("c").reshape(1).astype(jnp.int32)
    theirs = _exchange("reduce_pair", to_sibling, grads,
                       [lax.empty((g.shape[0] // 2, g.shape[1]), bf16) for g in grads], n)
    pair = [_pair_sum(core, g, t) for g, t in zip(grads, theirs)]
    lands = [lax.empty((N_CHIPS - 1, g.shape[0] // 2, g.shape[1] // N_CHIPS), bf16) for g in grads]
    return _start("reduce_chips", across_chips, pair, lands, 3 * n, pair[0])


def _reduce_end(started, after):
    x, y, c = _place()
    chip = (2 * x + y).reshape(1).astype(jnp.int32)
    pair, others = _wait(started, after)
    _, _, share = _reduce_plans(len(pair))
    mine = [_chip_sum(chip, p, o) for p, o in zip(pair, others)]
    sibs = _exchange("reduce_share", share, mine, [lax.empty(h.shape, f32) for h in mine], len(mine))
    return [jnp.where(c == 0, jnp.concatenate([h, s], axis=0), jnp.concatenate([s, h], axis=0))
            for h, s in zip(mine, sibs)]


def _to_all(src, land, sems):
    x, y, c = _place()
    me = 4 * x + 2 * y + c
    remotes, recvs = [], []
    for k in range(1, N_DEV):
        px, py, pc = _flip(x, (k >> 2) & 1), _flip(y, (k >> 1) & 1), _flip(c, k & 1)
        cp, rv = _remote(src[0], land[0].at[me], land[0].at[4 * px + 2 * py + pc], sems, (px, py, pc))
        remotes.append(cp)
        recvs.append(rv)
    return remotes, recvs


def _gather_small_begin(packed):
    return _start("gather_small", _to_all, [packed], [lax.empty((N_DEV,) + packed.shape, f32)], N_DEV - 1, packed)


def _gather_small_end(started, after):
    (packed,), (others,) = _wait(started, after)
    x, y, c = _place()
    return lax.dynamic_update_slice_in_dim(others, packed[None], 4 * x + 2 * y + c, axis=0)


def _gather_all(packed):
    others = _exchange("gather_all", _to_all, [packed], [lax.empty((N_DEV,) + packed.shape, f32)], N_DEV - 1)[0]
    x, y, c = _place()
    return lax.dynamic_update_slice_in_dim(others, packed[None], 4 * x + 2 * y + c, axis=0)


def _rows8(v):
    return jnp.pad(v[None, :], ((0, 7), (0, 0)))


def _vec_rows(vs):
    return jnp.pad(jnp.stack(vs), ((0, 8 - len(vs)), (0, 0)))


SMALL_ROWS = 224


def _pack_small(conv_vec, conv_dw, pool_vec, pool_w, pre_g, post_g, rel):
    return jnp.concatenate([
        conv_vec, conv_dw, pool_vec, pool_w.reshape(GD, WC),
        _rows8(pre_g).reshape(16, WC), _rows8(post_g).reshape(16, WC),
        jnp.pad(rel, ((0, 0), (0, D - rel.shape[1]))).reshape(16, WC)], axis=0)


def _unpack_small(p):
    conv_vec, pool_vec = p[0:8], p[40:48]
    return dict(
        conv_dw_b=conv_vec[0], conv_ln_g=conv_vec[1], conv_ln_b=conv_vec[2], conv_dw=p[8:8 + CONV_K],
        pool_b=pool_vec[0].reshape(4, GD), pool_scale=pool_vec[1], pool_w=p[48:176].reshape(4, GD, GD),
        pre_norm_g=p[176:192].reshape(8, D)[0], post_norm_g=p[192:208].reshape(8, D)[0],
        rel_bias=p[208:224].reshape(8, D)[:, :2 * MAX_REL + 1])


def _in_proj(h, w_in, after=None):
    return _mm("mm_in", h, w_in, "nn", h.shape[0], NCOL, D, 1024, 1536, D, f32, after=after)


def _layer_fwd(x2, ht, z, lw, BL, SEQ, next_g=None, tgt=None):
    cv, u1 = _conv_fwd(z, lw["dw32"], lw["cvec"], BL, SEQ)
    at = _attn_fwd(z, lw["bm"], BL, SEQ)
    pv = _pool_fwd(z, lw["pw"], lw["pvec"], BL, SEQ)
    out, merged, y, *last = _tail_fwd(z, (cv, at, pv), x2, lw, next_g, tgt)
    return (out, *last), dict(x=x2, ht=ht, z=z, u1=u1, acts=(cv, at, pv), merged=merged, y=y)


def _layer_bwd(dout, sv, lw, BL, SEQ, meanwhile=None):
    tail = _tail_bwd(sv["z"], dout, sv["y"], sv["merged"], sv["acts"], lw)
    dz, dacts, dpost = tail[0], tail[1:4], tail[4]
    dw_tail = tail[5].astype(bf16)
    if meanwhile is not None:
        meanwhile(dw_tail)
    dz, ddw, dcvec = _conv_bwd(sv["z"], sv["u1"], dacts[0], dz, lw["dw32"], lw["cvec"], BL, SEQ)
    dz, dbm = _attn_bwd(sv["z"], dacts[1], dz, lw["bm"], BL, SEQ)
    dz, dpw, dpvec = _pool_bwd(sv["z"], dacts[2], dz, lw["pw"], lw["pvec"], BL, SEQ)
    drel = _bias_table_grad(_bias_colsum(_bias_fold(dbm)))
    small_gather = _gather_small_begin(_pack_small(dcvec, ddw, dpvec, dpw, jnp.zeros((D,), f32), dpost[0], drel))
    dw_in = _mm_dw_in(sv["ht"], dz, small_gather["token"])
    reduction = _reduce_begin([dw_in, dw_tail])
    dx, dpre = _mm_dx(dz, lw["w_in"], sv["x"], lw["pre_g"], dout, reduction["token"])
    return dx, reduction, small_gather, dpre


BIG = ("w_in", "w_conv_out", "w_attn_out", "w_pool_out", "w_out")
PRE_ROWS = slice(176, 192)


def _layer_shards(w, l):
    return [w[k][l].astype(bf16) for k in BIG] + [w["conv_dw"][l]]


def _side_by_side(g):
    return jnp.transpose(g, (1, 0, 2)).reshape(g.shape[1], N_CHIPS * g.shape[2])


def _layer_weights(w_in, gathered, w, l, bm):
    lw = {k: _side_by_side(g) for k, g in zip(BIG[1:4], gathered[:3])}
    lw["w_in"] = w_in
    lw["w_out"] = gathered[3].reshape(D, D)
    lw["pre_g"] = w["pre_norm_g"][l][None]
    lw["post_g"] = w["post_norm_g"][l][None]
    lw["dw32"] = jnp.pad(_side_by_side(gathered[4]), ((0, 32 - CONV_K), (0, 0)))
    lw["cvec"] = _vec_rows([w["conv_dw_b"][l], w["conv_ln_g"][l], w["conv_ln_b"][l]])
    lw["bm"] = bm
    lw["pw"] = w["pool_w"][l].astype(bf16)
    lw["pvec"] = _vec_rows([w["pool_b"][l].reshape(WC), w["pool_scale"][l]])
    return lw


SMALL = ("pre_norm_g", "post_norm_g", "conv_dw_b", "conv_ln_g", "conv_ln_b", "rel_bias", "pool_w", "pool_b", "pool_scale")
ORDER = ("pre_norm_g", "post_norm_g", "w_in", "conv_dw", "conv_dw_b", "conv_ln_g", "conv_ln_b", "w_conv_out",
         "rel_bias", "w_attn_out", "pool_w", "pool_b", "pool_scale", "w_pool_out", "w_out")


def _pack_small_params(p):
    return jnp.concatenate([
        _pack_small(_vec_rows([p["conv_dw_b"][l], p["conv_ln_g"][l], p["conv_ln_b"][l]]), jnp.zeros((32, WC), f32),
                    _vec_rows([p["pool_b"][l].reshape(WC), p["pool_scale"][l]]), p["pool_w"][l],
                    p["pre_norm_g"][l], p["post_norm_g"][l], p["rel_bias"][l])
        for l in range(DEPTH)], axis=0)


def _unpack_small_params(packed):
    layers = [_unpack_small(packed[l * SMALL_ROWS:(l + 1) * SMALL_ROWS]) for l in range(DEPTH)]
    return {k: jnp.stack([layers[l][k] for l in range(DEPTH)]) for k in layers[0]}


def kernel(x, pre_norm_g, post_norm_g, w_in, conv_dw, conv_dw_b, conv_ln_g, conv_ln_b, w_conv_out, rel_bias, w_attn_out, pool_w, pool_b, pool_scale, w_pool_out, w_out, loss_target, m_pre_norm_g, m_post_norm_g, m_w_in, m_conv_dw, m_conv_dw_b, m_conv_ln_g, m_conv_ln_b, m_w_conv_out, m_rel_bias, m_w_attn_out, m_pool_w, m_pool_b, m_pool_scale, m_w_pool_out, m_w_out, v_pre_norm_g, v_post_norm_g, v_w_in, v_conv_dw, v_conv_dw_b, v_conv_ln_g, v_conv_ln_b, v_w_conv_out, v_rel_bias, v_w_attn_out, v_pool_w, v_pool_b, v_pool_scale, v_w_pool_out, v_w_out):
    BL, SEQ, _ = x.shape
    T = BL * SEQ
    w = dict(pre_norm_g=pre_norm_g, post_norm_g=post_norm_g, w_in=w_in, conv_dw=conv_dw, conv_dw_b=conv_dw_b,
             conv_ln_g=conv_ln_g, conv_ln_b=conv_ln_b, w_conv_out=w_conv_out, rel_bias=rel_bias, w_attn_out=w_attn_out,
             pool_w=pool_w, pool_b=pool_b, pool_scale=pool_scale, w_pool_out=w_pool_out, w_out=w_out)
    m = dict(pre_norm_g=m_pre_norm_g, post_norm_g=m_post_norm_g, w_in=m_w_in, conv_dw=m_conv_dw, conv_dw_b=m_conv_dw_b,
             conv_ln_g=m_conv_ln_g, conv_ln_b=m_conv_ln_b, w_conv_out=m_w_conv_out, rel_bias=m_rel_bias,
             w_attn_out=m_w_attn_out, pool_w=m_pool_w, pool_b=m_pool_b, pool_scale=m_pool_scale,
             w_pool_out=m_w_pool_out, w_out=m_w_out)
    v = dict(pre_norm_g=v_pre_norm_g, post_norm_g=v_post_norm_g, w_in=v_w_in, conv_dw=v_conv_dw, conv_dw_b=v_conv_dw_b,
             conv_ln_g=v_conv_ln_g, conv_ln_b=v_conv_ln_b, w_conv_out=v_w_conv_out, rel_bias=v_rel_bias,
             w_attn_out=v_w_attn_out, pool_w=v_pool_w, pool_b=v_pool_b, pool_scale=v_pool_scale,
             w_pool_out=v_w_pool_out, w_out=v_w_out)

    shards = [_layer_shards(w, l) for l in range(DEPTH)]
    x2 = x.reshape(T, D)
    h0, ht0 = _rms_pre(x2, pre_norm_g[0][None])
    first = _gather_begin("w_in0", shards[0][:1], 1, x2)
    bms = [_bias_matrix(rel_bias[l]) for l in range(DEPTH)]
    packs = [_pack_small_params(p) for p in (w, m, v)]
    w_in0 = _side_by_side(_gather_end(first, shards[0][:1], 1, [ht0, *bms, *packs])[0])
    rest0 = _gather_begin("rest0", shards[0][1:], 4, w_in0)
    all1 = _gather_begin("layer1", shards[1], 5, rest0["token"])
    z0 = _in_proj(h0, w_in0, after=all1["token"])
    lw0 = _layer_weights(w_in0, _gather_end(rest0, shards[0][1:], 4, z0), w, 0, bms[0])
    (out0, h1, ht1), saved0 = _layer_fwd(x2, ht0, z0, lw0, BL, SEQ, next_g=pre_norm_g[1][None])
    gathered1 = _gather_end(all1, shards[1], 5, out0)
    lw1 = _layer_weights(_side_by_side(gathered1[0]), gathered1[1:], w, 1, bms[1])
    (dout, sq), saved1 = _layer_fwd(out0, ht1, _in_proj(h1, lw1["w_in"]), lw1, BL, SEQ, tgt=loss_target.reshape(T, D))
    loss = lax.psum(0.5 * jnp.sum(sq) / float(D), ("x", "y", "c"))

    summed = [None] * DEPTH
    dx1, reduction1, small_gather1, dpre1 = _layer_bwd(dout, saved1, lw1, BL, SEQ)

    def finish_layer1(after):
        summed[1] = _reduce_end(reduction1, after)

    grad_x, reduction0, small_gather0, dpre0 = _layer_bwd(dx1, saved0, lw0, BL, SEQ, meanwhile=finish_layer1)
    summed[0] = _reduce_end(reduction0, grad_x)
    dpre = _sum_slots("sum_small", _gather_all(jnp.concatenate([_rows8(dpre0[0]), _rows8(dpre1[0])], axis=0)))
    gsmall = []
    for l, started in enumerate((small_gather0, small_gather1)):
        g = _sum_slots("sum_small", _gather_small_end(started, dpre))
        gsmall += [g[:PRE_ROWS.start], dpre[8 * l:8 * l + 8].reshape(16, WC), g[PRE_ROWS.stop:]]
    gsmall = jnp.concatenate(gsmall, axis=0)

    grads, deltas, new_m, new_v = {}, {}, {}, {}
    for i, k in enumerate(BIG):
        if k == "w_in":
            g = jnp.stack([summed[l][0] for l in range(DEPTH)])
        elif k == "w_out":
            g = jnp.stack([summed[l][1][3 * WC:].T for l in range(DEPTH)])
        else:
            g = jnp.stack([summed[l][1][(i - 1) * WC:i * WC] for l in range(DEPTH)])
        grads[k] = g
        shape = w[k].shape
        flat2 = lambda a: a.reshape(shape[0] * shape[1], shape[2])
        d_, nm_, nv_ = _adamw("adamw_big", flat2(g), flat2(w[k]), flat2(m[k]), flat2(v[k]))
        deltas[k], new_m[k], new_v[k] = d_.reshape(shape), nm_.reshape(shape), nv_.reshape(shape)

    d_, nm_, nv_ = _adamw("adamw_small", gsmall, *packs)
    gs, ds, ms, vs = (_unpack_small_params(a) for a in (gsmall, d_, nm_, nv_))
    for k in SMALL:
        grads[k], deltas[k], new_m[k], new_v[k] = gs[k], ds[k], ms[k], vs[k]
    chip = 2 * lax.axis_index("x") + lax.axis_index("y")
    g_dw = lax.dynamic_slice_in_dim(gs["conv_dw"], chip * GD, GD, axis=2)
    flat2 = lambda a: a.reshape(DEPTH * CONV_K, GD)
    d_, nm_, nv_ = _adamw("adamw_conv_dw", flat2(g_dw), flat2(conv_dw), flat2(m["conv_dw"]), flat2(v["conv_dw"]))
    grads["conv_dw"] = g_dw
    deltas["conv_dw"], new_m["conv_dw"], new_v["conv_dw"] = (a.reshape(conv_dw.shape) for a in (d_, nm_, nv_))

    return (loss, grad_x.reshape(x.shape), *[grads[k] for k in ORDER], *[deltas[k] for k in ORDER],
            *[new_m[k] for k in ORDER], *[new_v[k] for k in ORDER])
```

```python
import numpy as np
import jax
import jax.numpy as jnp
from jax import lax
from jax.experimental import pallas as pl
from jax.experimental.pallas import tpu as pltpu

f32 = jnp.float32
bf16 = jnp.bfloat16

D = 1024
DEPTH = 2
WC = 512
HEAD_DIM = 64
CHUNK = 64
LEFT_CHUNKS = 8
KEY_PAD = LEFT_CHUNKS * CHUNK
MAX_REL = 256
CONV_K = 31
POOL_WINDOWS = (2, 4, 8, 16)
GD = 128
NCOL = 7680
EPS = 1e-6
NEG_INF = -1e30
COL_A, COL_B, COL_CG, COL_Q, COL_K, COL_V, COL_AG, COL_PI, COL_PG, COL_GM = (
    0, 512, 1024, 1536, 2048, 2560, 3072, 3584, 4096, 4608)

ADAM_LR = 0.001
ADAM_B1 = 0.9
ADAM_B2 = 0.999
ADAM_EPS = 1e-08
ADAM_WD = 0.01
ADAM_STEP = 10

QG = 256
KW = KEY_PAD + QG
BIAS_VARIANTS = KEY_PAD // QG + 1
CT = 256
HALO = 32
PHALO = 16
N_CHIPS = 4
N_DEV = 8
VMEM_LIMIT = 56 * 1024 * 1024
MESH = pl.DeviceIdType.MESH
ANY = pl.BlockSpec(memory_space=pl.ANY)

DZ_BLOCKS = 18
DZ_CONV, DZ_ATTN, DZ_POOL, DZ_GM = 0, 4, 8, 12


def _dz_block(c):
    return c + (c >= 3).astype(jnp.int32) + 2 * (c >= 9).astype(jnp.int32)


def _params(sem=None):
    return pltpu.CompilerParams(dimension_semantics=sem, vmem_limit_bytes=VMEM_LIMIT)


def _sig(x):
    return 1.0 / (1.0 + jnp.exp(-x))


def _dsilu(x, s):
    return s * (1.0 + x * (1.0 - s))


def _colsum(x):
    return jnp.sum(x, axis=0, keepdims=True)


def _rms_pre(x2, g):
    T = x2.shape[0]
    tm = 512

    def body(x_ref, g_ref, h_ref, ht_ref):
        x = x_ref[...]
        r = lax.rsqrt(jnp.mean(x * x, axis=-1, keepdims=True) + EPS)
        h = (x * r) * g_ref[...]
        h_ref[...] = h.astype(bf16)
        ht_ref[...] = h.T.astype(bf16)

    row = pl.BlockSpec((tm, D), lambda i: (i, 0))
    vec = pl.BlockSpec((1, D), lambda i: (0, 0))
    return pl.pallas_call(
        body, grid=(T // tm,), in_specs=[row, vec], out_specs=[row, pl.BlockSpec((D, tm), lambda i: (0, i))],
        out_shape=[jax.ShapeDtypeStruct((T, D), bf16), jax.ShapeDtypeStruct((D, T), bf16)], name="rms_pre",
        compiler_params=_params(("parallel",)))(x2, g)


def _in_proj(h, w_in, after):
    T = h.shape[0]
    tm, tn = 512, 1536

    def body(h_ref, w_ref, after_ref, z_ref):
        for n0 in range(0, NCOL, tn):
            z_ref[:, n0:n0 + tn] = jnp.dot(h_ref[...], w_ref[:, n0:n0 + tn], preferred_element_type=f32)

    return pl.pallas_call(
        body, grid=(T // tm,),
        in_specs=[pl.BlockSpec((tm, D), lambda i: (i, 0)),
                  pl.BlockSpec((D, NCOL), lambda i: (0, 0), pipeline_mode=pl.Buffered(1)), ANY],
        out_specs=pl.BlockSpec((tm, NCOL), lambda i: (i, 0)), out_shape=jax.ShapeDtypeStruct((T, NCOL), f32),
        name="mm_in", compiler_params=_params(("parallel",)))(h, w_in, after)


DZ_SPANS = ((DZ_CONV, 3), (DZ_ATTN, 4), (DZ_POOL, 2), (DZ_GM, 6))


def _mm_dx(dz, w_in, x2, g, dout, after):
    T = dz.shape[1]
    tm = 512

    def body(conv_ref, attn_ref, pool_ref, gm_ref, w_ref, x_ref, g_ref, d_ref, after_ref, dx_ref, dg_ref):
        dh = None
        col = 0
        for ref, (_, blocks) in zip((conv_ref, attn_ref, pool_ref, gm_ref), DZ_SPANS):
            for b in range(blocks):
                p = lax.dot_general(ref[b], w_ref[:, col * WC:(col + 1) * WC], (((1,), (1,)), ((), ())),
                                    preferred_element_type=f32)
                dh = p if dh is None else dh + p
                col += 1
        x = x_ref[...]
        r = lax.rsqrt(jnp.mean(x * x, axis=-1, keepdims=True) + EPS)
        xn = x * r
        dxn = dh * g_ref[...]
        dx_ref[...] = r * (dxn - xn * jnp.mean(dxn * xn, axis=-1, keepdims=True)) + d_ref[...]

        @pl.when(pl.program_id(0) == 0)
        def _():
            dg_ref[...] = jnp.zeros_like(dg_ref)

        dg_ref[...] += _colsum(dh * xn)

    spans = [pl.BlockSpec((blocks, tm, WC), lambda i, first=first, blocks=blocks: (first // blocks, i, 0))
             for first, blocks in DZ_SPANS]
    row = pl.BlockSpec((tm, D), lambda i: (i, 0))
    vec = pl.BlockSpec((1, D), lambda i: (0, 0))
    return pl.pallas_call(
        body, grid=(T // tm,),
        in_specs=spans + [pl.BlockSpec((D, NCOL), lambda i: (0, 0), pipeline_mode=pl.Buffered(1)), row, vec, row, ANY],
        out_specs=[row, vec], out_shape=[jax.ShapeDtypeStruct((T, D), f32), jax.ShapeDtypeStruct((1, D), f32)],
        name="mm_dx", compiler_params=_params(("arbitrary",)))(dz, dz, dz, dz, w_in, x2, g, dout, after)


def _mm_dw_in(ht, dz, after):
    T = dz.shape[1]

    def body(ht_ref, dz_ref, after_ref, o_ref):
        o_ref[...] = jnp.dot(ht_ref[...], dz_ref[...], preferred_element_type=f32).astype(bf16)

    return pl.pallas_call(
        body, grid=(NCOL // WC,),
        in_specs=[pl.BlockSpec((D, T), lambda j: (0, 0), pipeline_mode=pl.Buffered(1)),
                  pl.BlockSpec((None, T, WC), lambda j: (_dz_block(j), 0, 0)), ANY],
        out_specs=pl.BlockSpec((D, WC), lambda j: (0, j)), out_shape=jax.ShapeDtypeStruct((D, NCOL), bf16),
        name="mm_dw_in", compiler_params=_params(("parallel",)))(ht, dz, after)


def _conv_delays():
    return [(8 * a + b, a, b) for b in range(8) for a in range(4) if 8 * a + b < CONV_K]


def _conv_rolls(win):
    return [win if b == 0 else pltpu.roll(win, b, axis=0) for b in range(8)]


def _conv_taps(rolled, dw_ref):
    acc = None
    for d, a, b in _conv_delays():
        term = rolled[b][HALO - 8 * a:HALO - 8 * a + CT, :] * dw_ref[pl.ds(CONV_K - 1 - d, 1), :]
        acc = term if acc is None else acc + term
    return acc


def _conv_fwd(z, dw32, cvec, BL, SEQ):
    T = BL * SEQ
    nct = SEQ // CT

    def body(a_ref, b_ref, cg_ref, dw_ref, vec_ref, o_ref, u1_ref, p_ref):
        p_ref[pl.ds(0, HALO), :] = jnp.zeros((HALO, WC), f32)

        def glu(c, carry):
            r0 = pl.multiple_of(c * CT, CT)
            p_ref[pl.ds(r0 + HALO, CT), :] = a_ref[pl.ds(r0, CT), :] * _sig(b_ref[pl.ds(r0, CT), :])
            return carry

        lax.fori_loop(0, nct, glu, 0)

        def step(c, carry):
            r0 = pl.multiple_of(c * CT, CT)
            u1 = _conv_taps(_conv_rolls(p_ref[pl.ds(r0, CT + HALO), :]), dw_ref) + vec_ref[0:1, :]
            u1_ref[pl.ds(r0, CT), :] = u1
            xc = u1 - jnp.mean(u1, axis=-1, keepdims=True)
            rs = lax.rsqrt(jnp.mean(xc * xc, axis=-1, keepdims=True) + EPS)
            u2 = (xc * rs) * vec_ref[1:2, :] + vec_ref[2:3, :]
            cg = cg_ref[pl.ds(r0, CT), :]
            o_ref[pl.ds(r0, CT), :] = ((u2 * _sig(u2)) * (cg * _sig(cg))).astype(bf16)
            return carry

        lax.fori_loop(0, nct, step, 0)

    def zs(col):
        return pl.BlockSpec((SEQ, WC), lambda b: (b, col // WC))

    seq = pl.BlockSpec((SEQ, WC), lambda b: (b, 0))
    return pl.pallas_call(
        body, grid=(BL,),
        in_specs=[zs(COL_A), zs(COL_B), zs(COL_CG), pl.BlockSpec((32, WC), lambda b: (0, 0)),
                  pl.BlockSpec((8, WC), lambda b: (0, 0))],
        out_specs=[seq, seq],
        out_shape=[jax.ShapeDtypeStruct((T, WC), bf16), jax.ShapeDtypeStruct((T, WC), f32)],
        scratch_shapes=[pltpu.VMEM((SEQ + HALO, WC), f32)], name="conv_fwd",
        compiler_params=_params(("parallel",)))(z, z, z, dw32, cvec)


def _conv_bwd(z, u1, dcv, dz, dw32, cvec, BL, SEQ):
    nct = SEQ // CT

    def body(a_ref, b_ref, cg_ref, u1_ref, dcv_ref, dzin_ref, dw_ref, vec_ref, dz_ref, ddw_ref, dvec_ref,
             p_ref, q_ref, taps_ref):
        @pl.when(pl.program_id(0) == 0)
        def _():
            ddw_ref[...] = jnp.zeros_like(ddw_ref)
            dvec_ref[...] = jnp.zeros_like(dvec_ref)

        p_ref[pl.ds(0, HALO), :] = jnp.zeros((HALO, WC), f32)
        q_ref[pl.ds(SEQ, HALO), :] = jnp.zeros((HALO, WC), f32)

        def glu(c, carry):
            r0 = pl.multiple_of(c * CT, CT)
            p_ref[pl.ds(r0 + HALO, CT), :] = a_ref[pl.ds(r0, CT), :] * _sig(b_ref[pl.ds(r0, CT), :])
            return carry

        lax.fori_loop(0, nct, glu, 0)

        def step(c, carry):
            r0 = pl.multiple_of(c * CT, CT)
            rolled = _conv_rolls(p_ref[pl.ds(r0, CT + HALO), :])
            u1 = u1_ref[pl.ds(r0, CT), :]
            xc = u1 - jnp.mean(u1, axis=-1, keepdims=True)
            rs = lax.rsqrt(jnp.mean(xc * xc, axis=-1, keepdims=True) + EPS)
            nrm = xc * rs
            u2 = nrm * vec_ref[1:2, :] + vec_ref[2:3, :]
            s2 = _sig(u2)
            u3 = u2 * s2
            cg = cg_ref[pl.ds(r0, CT), :]
            scg = _sig(cg)
            dcv_ = dcv_ref[pl.ds(r0, CT), :]
            dz_ref[2, pl.ds(r0, CT), :] = (dcv_ * u3 * _dsilu(cg, scg)).astype(bf16)
            du2 = dcv_ * (cg * scg) * _dsilu(u2, s2)
            dvec_ref[1:2, :] += _colsum(du2 * nrm)
            dvec_ref[2:3, :] += _colsum(du2)
            dn = du2 * vec_ref[1:2, :]
            du1 = rs * (dn - jnp.mean(dn, axis=-1, keepdims=True)
                        - nrm * jnp.mean(dn * nrm, axis=-1, keepdims=True))
            dvec_ref[0:1, :] += _colsum(du1)
            q_ref[pl.ds(r0, CT), :] = du1
            for d, a, b in _conv_delays():
                prod = du1 * rolled[b][HALO - 8 * a:HALO - 8 * a + CT, :]
                taps_ref[CONV_K - 1 - d] += jnp.sum(prod.reshape(CT // 8, 8, WC), axis=0)
            return carry

        taps_ref[...] = jnp.zeros_like(taps_ref)
        lax.fori_loop(0, nct, step, 0)
        for row in range(CONV_K):
            ddw_ref[pl.ds(row, 1), :] += _colsum(taps_ref[row])

        def back(c, carry):
            r0 = pl.multiple_of(c * CT, CT)
            wq = q_ref[pl.ds(r0, CT + HALO), :]
            up = {}
            acc = None
            for d, a, b in _conv_delays():
                if b not in up:
                    up[b] = wq if b == 0 else pltpu.roll(wq, CT + HALO - b, axis=0)
                term = up[b][8 * a:8 * a + CT, :] * dw_ref[pl.ds(CONV_K - 1 - d, 1), :]
                acc = term if acc is None else acc + term
            a_ = a_ref[pl.ds(r0, CT), :]
            sb = _sig(b_ref[pl.ds(r0, CT), :])
            dz_ref[0, pl.ds(r0, CT), :] = (acc * sb).astype(bf16)
            dz_ref[1, pl.ds(r0, CT), :] = (acc * a_ * sb * (1.0 - sb)).astype(bf16)
            return carry

        lax.fori_loop(0, nct, back, 0)

    def zs(col):
        return pl.BlockSpec((SEQ, WC), lambda b: (b, col // WC), pipeline_mode=pl.Buffered(1))

    def const(r):
        return pl.BlockSpec((r, WC), lambda b: (0, 0))

    seq = pl.BlockSpec((SEQ, WC), lambda b: (b, 0), pipeline_mode=pl.Buffered(1))
    return pl.pallas_call(
        body, grid=(BL,),
        in_specs=[zs(COL_A), zs(COL_B), zs(COL_CG), seq, seq, ANY, const(32), const(8)],
        out_specs=[pl.BlockSpec((3, SEQ, WC), lambda b: (DZ_CONV // 3, b, 0)), const(32), const(8)],
        out_shape=[jax.ShapeDtypeStruct(dz.shape, bf16), jax.ShapeDtypeStruct((32, WC), f32),
                   jax.ShapeDtypeStruct((8, WC), f32)],
        scratch_shapes=[pltpu.VMEM((SEQ + HALO, WC), f32), pltpu.VMEM((SEQ + HALO, WC), f32),
                        pltpu.VMEM((CONV_K, 8, WC), f32)],
        input_output_aliases={5: 0}, name="conv_bwd",
        compiler_params=_params(("arbitrary",)))(z, z, z, u1, dcv, dz, dw32, cvec)


def _pool_counts(r0):
    t1 = r0 + 1 + lax.broadcasted_iota(jnp.int32, (CT, 1), 0)
    return [jnp.minimum(t1, w).astype(f32) for w in POOL_WINDOWS]


def _pool_sums(win, forward):
    n = CT + PHALO

    def sh(x, s):
        return pltpu.roll(x, (n - s) if forward else s, axis=0)

    s2 = win + sh(win, 1)
    s4 = s2[:, GD:] + sh(s2[:, GD:], 2)
    s8 = s4[:, GD:] + sh(s4[:, GD:], 4)
    s16 = s8[:, GD:] + sh(s8[:, GD:], 8)
    lo = 0 if forward else PHALO
    return [s[lo:lo + CT, :GD] for s in (s2, s4, s8, s16)]


def _pool_fwd(z, pw, pvec, BL, SEQ):
    T = BL * SEQ
    nct = SEQ // CT

    def body(pi_ref, pg_ref, pw_ref, vec_ref, o_ref, p_ref):
        p_ref[pl.ds(0, PHALO), :] = jnp.zeros((PHALO, WC), f32)

        def fill(c, carry):
            r0 = pl.multiple_of(c * CT, CT)
            p_ref[pl.ds(r0 + PHALO, CT), :] = pi_ref[pl.ds(r0, CT), :]
            return carry

        lax.fori_loop(0, nct, fill, 0)

        def step(c, carry):
            r0 = pl.multiple_of(c * CT, CT)
            sums = _pool_sums(p_ref[pl.ds(r0, CT + PHALO), :], False)
            cnt = _pool_counts(r0)
            pin = pi_ref[pl.ds(r0, CT), :]
            mixed = []
            for g in range(4):
                pooled = sums[g] / cnt[g] - pin[:, g * GD:(g + 1) * GD]
                mixed.append(jnp.dot(pooled.astype(bf16), pw_ref[g], preferred_element_type=f32))
            m0 = jnp.concatenate(mixed, axis=1) + vec_ref[0:1, :]
            pg = pg_ref[pl.ds(r0, CT), :]
            o_ref[pl.ds(r0, CT), :] = ((m0 * vec_ref[1:2, :]) * (pg * _sig(pg))).astype(bf16)
            return carry

        lax.fori_loop(0, nct, step, 0)

    def zs(col):
        return pl.BlockSpec((SEQ, WC), lambda b: (b, col // WC))

    return pl.pallas_call(
        body, grid=(BL,),
        in_specs=[zs(COL_PI), zs(COL_PG), pl.BlockSpec((4, GD, GD), lambda b: (0, 0, 0)),
                  pl.BlockSpec((8, WC), lambda b: (0, 0))],
        out_specs=pl.BlockSpec((SEQ, WC), lambda b: (b, 0)),
        out_shape=jax.ShapeDtypeStruct((T, WC), bf16),
        scratch_shapes=[pltpu.VMEM((SEQ + PHALO, WC), f32)], name="pool_fwd",
        compiler_params=_params(("parallel",)))(z, z, pw, pvec)


def _pool_bwd(z, dpl, dz, pw, pvec, BL, SEQ):
    nct = SEQ // CT

    def body(pi_ref, pg_ref, dpl_ref, dzin_ref, pw_ref, vec_ref, dz_ref, dpw_ref, dvec_ref, p_ref, e_ref, dp_ref):
        @pl.when(pl.program_id(0) == 0)
        def _():
            dpw_ref[...] = jnp.zeros_like(dpw_ref)
            dvec_ref[...] = jnp.zeros_like(dvec_ref)

        p_ref[pl.ds(0, PHALO), :] = jnp.zeros((PHALO, WC), f32)
        e_ref[pl.ds(SEQ, PHALO), :] = jnp.zeros((PHALO, WC), f32)

        def fill(c, carry):
            r0 = pl.multiple_of(c * CT, CT)
            p_ref[pl.ds(r0 + PHALO, CT), :] = pi_ref[pl.ds(r0, CT), :]
            return carry

        lax.fori_loop(0, nct, fill, 0)

        def step(c, carry):
            r0 = pl.multiple_of(c * CT, CT)
            sums = _pool_sums(p_ref[pl.ds(r0, CT + PHALO), :], False)
            cnt = _pool_counts(r0)
            pin = pi_ref[pl.ds(r0, CT), :]
            pooled = [(sums[g] / cnt[g] - pin[:, g * GD:(g + 1) * GD]).astype(bf16) for g in range(4)]
            m0 = jnp.concatenate(
                [jnp.dot(pooled[g], pw_ref[g], preferred_element_type=f32) for g in range(4)], axis=1) + vec_ref[0:1, :]
            scale = vec_ref[1:2, :]
            pg = pg_ref[pl.ds(r0, CT), :]
            spg = _sig(pg)
            dpl_ = dpl_ref[pl.ds(r0, CT), :]
            dmixed = dpl_ * (pg * spg)
            dz_ref[1, pl.ds(r0, CT), :] = (dpl_ * (m0 * scale) * _dsilu(pg, spg)).astype(bf16)
            dvec_ref[1:2, :] += _colsum(dmixed * m0)
            dm0 = dmixed * scale
            dvec_ref[0:1, :] += _colsum(dm0)
            dps, es = [], []
            for g in range(4):
                dm0g = dm0[:, g * GD:(g + 1) * GD].astype(bf16)
                dpw_ref[g] += lax.dot_general(pooled[g], dm0g, (((0,), (0,)), ((), ())), preferred_element_type=f32)
                dpg = lax.dot_general(dm0g, pw_ref[g], (((1,), (1,)), ((), ())), preferred_element_type=f32)
                dps.append(dpg)
                es.append(dpg / cnt[g])
            dp_ref[pl.ds(r0, CT), :] = jnp.concatenate(dps, axis=1)
            e_ref[pl.ds(r0, CT), :] = jnp.concatenate(es, axis=1)
            return carry

        lax.fori_loop(0, nct, step, 0)

        def back(c, carry):
            r0 = pl.multiple_of(c * CT, CT)
            fs = _pool_sums(e_ref[pl.ds(r0, CT + PHALO), :], True)
            dz_ref[0, pl.ds(r0, CT), :] = (jnp.concatenate(fs, axis=1) - dp_ref[pl.ds(r0, CT), :]).astype(bf16)
            return carry

        lax.fori_loop(0, nct, back, 0)

    def zs(col):
        return pl.BlockSpec((SEQ, WC), lambda b: (b, col // WC))

    return pl.pallas_call(
        body, grid=(BL,),
        in_specs=[zs(COL_PI), zs(COL_PG), pl.BlockSpec((SEQ, WC), lambda b: (b, 0)), ANY,
                  pl.BlockSpec((4, GD, GD), lambda b: (0, 0, 0)), pl.BlockSpec((8, WC), lambda b: (0, 0))],
        out_specs=[pl.BlockSpec((2, SEQ, WC), lambda b: (DZ_POOL // 2, b, 0)),
                   pl.BlockSpec((4, GD, GD), lambda b: (0, 0, 0)), pl.BlockSpec((8, WC), lambda b: (0, 0))],
        out_shape=[jax.ShapeDtypeStruct(dz.shape, bf16), jax.ShapeDtypeStruct((4, GD, GD), f32),
                   jax.ShapeDtypeStruct((8, WC), f32)],
        scratch_shapes=[pltpu.VMEM((SEQ + PHALO, WC), f32), pltpu.VMEM((SEQ + PHALO, WC), f32),
                        pltpu.VMEM((SEQ, WC), f32)],
        input_output_aliases={3: 0}, name="pool_bwd",
        compiler_params=_params(("arbitrary",)))(z, z, dpl, dz, pw, pvec)


def _attn_prologue(q_ref, k_ref, v_ref, qs0, qs1, kp, vp, SEQ):
    head0 = lax.broadcasted_iota(jnp.int32, (1, 2 * HEAD_DIM), 1) < HEAD_DIM
    kp[pl.ds(0, KEY_PAD), :] = jnp.zeros((KEY_PAD, 2 * HEAD_DIM), bf16)
    vp[pl.ds(0, KEY_PAD), :] = jnp.zeros((KEY_PAD, 2 * HEAD_DIM), bf16)

    def fill(g, carry):
        r0 = pl.multiple_of(g * QG, QG)
        q = q_ref[pl.ds(r0, QG), :] * (HEAD_DIM ** -0.5)
        qs0[pl.ds(r0, QG), :] = jnp.where(head0, q, 0.0).astype(bf16)
        qs1[pl.ds(r0, QG), :] = jnp.where(head0, 0.0, q).astype(bf16)
        kp[pl.ds(r0 + KEY_PAD, QG), :] = k_ref[pl.ds(r0, QG), :].astype(bf16)
        vp[pl.ds(r0 + KEY_PAD, QG), :] = v_ref[pl.ds(r0, QG), :].astype(bf16)
        return carry

    lax.fori_loop(0, SEQ // QG, fill, 0)
    return head0


def _attn_weights(qh, kw, bias):
    s = lax.dot_general(qh, kw, (((1,), (1,)), ((), ())), preferred_element_type=f32) + bias
    e = jnp.exp(s - jnp.max(s, axis=-1, keepdims=True))
    return e, 1.0 / jnp.sum(e, axis=-1, keepdims=True)


def _attn_fwd(z, bm, BL, SEQ):
    T = BL * SEQ
    W2 = 2 * HEAD_DIM

    def body(q_ref, k_ref, v_ref, ag_ref, bm_ref, o_ref, qs0, qs1, kp, vp):
        head0 = _attn_prologue(q_ref, k_ref, v_ref, qs0, qs1, kp, vp, SEQ)

        def group(g, carry):
            r0 = pl.multiple_of(g * QG, QG)
            kw = kp[pl.ds(r0, KW), :]
            vw = vp[pl.ds(r0, KW), :]
            variant = jnp.minimum(g, BIAS_VARIANTS - 1)
            outs = []
            for hh, qs in enumerate((qs0, qs1)):
                e, inv = _attn_weights(qs[pl.ds(r0, QG), :], kw, bm_ref[variant, hh])
                outs.append(jnp.dot(e.astype(bf16), vw, preferred_element_type=f32) * inv)
            o = jnp.where(head0, outs[0], outs[1])
            ag = ag_ref[pl.ds(r0, QG), :]
            o_ref[pl.ds(r0, QG), :] = (o * (ag * _sig(ag))).astype(bf16)
            return carry

        lax.fori_loop(0, SEQ // QG, group, 0, unroll=8)

    def zs(col):
        return pl.BlockSpec((SEQ, W2), lambda b, hp: (b, col // W2 + hp))

    return pl.pallas_call(
        body, grid=(BL, WC // W2),
        in_specs=[zs(COL_Q), zs(COL_K), zs(COL_V), zs(COL_AG),
                  pl.BlockSpec((BIAS_VARIANTS, 2, QG, KW), lambda b, hp: (0, hp, 0, 0))],
        out_specs=pl.BlockSpec((SEQ, W2), lambda b, hp: (b, hp)),
        out_shape=jax.ShapeDtypeStruct((T, WC), bf16),
        scratch_shapes=[pltpu.VMEM((SEQ, W2), bf16), pltpu.VMEM((SEQ, W2), bf16),
                        pltpu.VMEM((SEQ + KEY_PAD, W2), bf16), pltpu.VMEM((SEQ + KEY_PAD, W2), bf16)],
        name="attn_fwd", compiler_params=_params(("parallel", "parallel")))(z, z, z, z, bm)


def _attn_bwd(z, dat, dz, bm, BL, SEQ):
    W2 = 2 * HEAD_DIM

    def body(q_ref, k_ref, v_ref, ag_ref, dat_ref, dzin_ref, bm_ref, dz_ref, dbm_ref, qs0, qs1, kp, vp, dka, dva):
        @pl.when(pl.program_id(1) == 0)
        def _():
            dbm_ref[...] = jnp.zeros_like(dbm_ref)

        head0 = _attn_prologue(q_ref, k_ref, v_ref, qs0, qs1, kp, vp, SEQ)
        dka[...] = jnp.zeros_like(dka)
        dva[...] = jnp.zeros_like(dva)

        def group(g, carry):
            r0 = pl.multiple_of(g * QG, QG)
            kw = kp[pl.ds(r0, KW), :]
            vw = vp[pl.ds(r0, KW), :]
            variant = jnp.minimum(g, BIAS_VARIANTS - 1)
            ag = ag_ref[pl.ds(r0, QG), :]
            do = dat_ref[pl.ds(r0, QG), :] * (ag * _sig(ag))
            outs, dqs = [], []
            for hh, qs in enumerate((qs0, qs1)):
                qh = qs[pl.ds(r0, QG), :]
                e, inv = _attn_weights(qh, kw, bm_ref[variant, hh])
                eb = e.astype(bf16)
                outs.append(jnp.dot(eb, vw, preferred_element_type=f32) * inv)
                doh = (jnp.where(head0, do, 0.0) if hh == 0 else jnp.where(head0, 0.0, do)) * inv
                doh = doh.astype(bf16)
                dp = lax.dot_general(doh, vw, (((1,), (1,)), ((), ())), preferred_element_type=f32)
                ds_ = e * (dp - jnp.sum(e * dp, axis=-1, keepdims=True) * inv)
                dbm_ref[hh] += ds_
                dsb = ds_.astype(bf16)
                dqs.append(jnp.dot(dsb, kw, preferred_element_type=f32))
                dka[pl.ds(r0, KW), :] += lax.dot_general(dsb, qh, (((0,), (0,)), ((), ())), preferred_element_type=f32)
                dva[pl.ds(r0, KW), :] += lax.dot_general(eb, doh, (((0,), (0,)), ((), ())), preferred_element_type=f32)
            o = jnp.where(head0, outs[0], outs[1])
            dq = jnp.where(head0, dqs[0], dqs[1]) * (HEAD_DIM ** -0.5)
            dz_ref[0, pl.ds(r0, QG), :] = dq.astype(bf16)
            dz_ref[3, pl.ds(r0, QG), :] = (dat_ref[pl.ds(r0, QG), :] * o * _dsilu(ag, _sig(ag))).astype(bf16)
            return carry

        lax.fori_loop(0, SEQ // QG, group, 0, unroll=8)

        def flush(g, carry):
            r0 = pl.multiple_of(g * QG, QG)
            dz_ref[1, pl.ds(r0, QG), :] = dka[pl.ds(r0 + KEY_PAD, QG), :].astype(bf16)
            dz_ref[2, pl.ds(r0, QG), :] = dva[pl.ds(r0 + KEY_PAD, QG), :].astype(bf16)
            return carry

        lax.fori_loop(0, SEQ // QG, flush, 0)

    def zs(col):
        return pl.BlockSpec((SEQ, W2), lambda hp, b: (b, col // W2 + hp))

    return pl.pallas_call(
        body, grid=(WC // W2, BL),
        in_specs=[zs(COL_Q), zs(COL_K), zs(COL_V), zs(COL_AG), pl.BlockSpec((SEQ, W2), lambda hp, b: (b, hp)), ANY,
                  pl.BlockSpec((BIAS_VARIANTS, 2, QG, KW), lambda hp, b: (0, hp, 0, 0))],
        out_specs=[pl.BlockSpec((4, SEQ, W2), lambda hp, b: (DZ_ATTN // 4, b, hp)),
                   pl.BlockSpec((2, QG, KW), lambda hp, b: (hp, 0, 0))],
        out_shape=[jax.ShapeDtypeStruct(dz.shape, bf16), jax.ShapeDtypeStruct((8, QG, KW), f32)],
        scratch_shapes=[pltpu.VMEM((SEQ, W2), bf16), pltpu.VMEM((SEQ, W2), bf16),
                        pltpu.VMEM((SEQ + KEY_PAD, W2), bf16), pltpu.VMEM((SEQ + KEY_PAD, W2), bf16),
                        pltpu.VMEM((SEQ + KEY_PAD, W2), f32), pltpu.VMEM((SEQ + KEY_PAD, W2), f32)],
        input_output_aliases={5: 0}, name="attn_bwd",
        compiler_params=_params(("parallel", "arbitrary")))(z, z, z, z, dat, dz, bm)


BIAS_TOP = KEY_PAD + MAX_REL + QG - 1


def _bias_matrix(table):
    n = 2 * MAX_REL
    wd = QG + KW
    e = jnp.concatenate([jnp.broadcast_to(table[:, n:], (8, BIAS_TOP - n + 1)), table[:, n - 1:BIAS_TOP - wd + 1:-1],
                         jnp.zeros((8, 1), f32)], axis=1)
    flat = jnp.broadcast_to(e[:, None, :], (8, QG, wd)).reshape(8, QG * wd)
    skew = flat[:, :QG * (wd - 1)].reshape(8, QG, wd - 1)
    vals = skew[:, :, QG - 1:QG - 1 + KW]
    r = np.arange(QG)[:, None] // CHUNK
    j = np.arange(KW)[None, :]
    band = (j // CHUNK >= r) & (j // CHUNK <= r + LEFT_CHUNKS)
    keep = np.stack([band & (j >= KEY_PAD - v * QG) for v in range(BIAS_VARIANTS)])
    return jnp.where(jnp.asarray(keep)[:, None], vals[None], NEG_INF)


def _bias_fold(dbm):
    wd = QG + KW
    placed = jnp.pad(dbm, ((0, 0), (0, 0), (QG - 1, 0))).reshape(8, QG * (wd - 1))
    return jnp.pad(placed, ((0, 0), (0, QG))).reshape(8, QG, wd)


def _bias_colsum(folded):
    width = folded.shape[2]

    def body(x_ref, o_ref):
        for h in range(8):
            o_ref[pl.ds(h, 1), :] = _colsum(x_ref[h])

    return pl.pallas_call(body, out_shape=jax.ShapeDtypeStruct((8, width), f32), name="bias_colsum",
                          compiler_params=_params())(folded)


def _bias_table_grad(colsum):
    n = 2 * MAX_REL
    wd = QG + KW
    clipped = jnp.sum(colsum[:, :BIAS_TOP - n + 1], axis=1, keepdims=True)
    return jnp.concatenate([jnp.zeros((8, BIAS_TOP - wd + 2), f32), colsum[:, wd - 2:BIAS_TOP - n:-1], clipped], axis=1)


GATE_SPAN = 3 * WC
TAIL_ROWS = 3 * WC + D


def _gate_specs(tm):
    return [pl.BlockSpec((tm, GATE_SPAN), lambda i: (i, COL_GM // GATE_SPAN)),
            pl.BlockSpec((tm, GATE_SPAN), lambda i: (i, COL_GM // GATE_SPAN + 1))]


def _gate_block(ga_ref, gb_ref, branch, half):
    k = 2 * branch + half
    ref, k = (ga_ref, k) if k < 3 else (gb_ref, k - 3)
    return _sig(ref[:, k * WC:(k + 1) * WC])


def _resident(shape):
    return pl.BlockSpec(shape, lambda i: (0,) * len(shape), pipeline_mode=pl.Buffered(1))


def _tail_fwd(z, acts, x2, lw, next_g=None, tgt=None):
    T = z.shape[0]
    tm = 256
    with_loss = tgt is not None
    assert with_loss != (next_g is not None)

    def body(cv_ref, at_ref, pv_ref, ga_ref, gb_ref, x_ref, wc_ref, wa_ref, wp_ref, wo_ref, g_ref, *rest):
        out_ref, merged_ref, y_ref = rest[1:4]
        ys = [jnp.dot(a[...], w[...], preferred_element_type=f32)
              for a, w in ((cv_ref, wc_ref), (at_ref, wa_ref), (pv_ref, wp_ref))]
        halves = []
        for half in range(2):
            cols = slice(half * WC, (half + 1) * WC)
            halves.append(sum(_gate_block(ga_ref, gb_ref, br, half) * ys[br][:, cols] for br in range(3)))
        merged = jnp.concatenate(halves, axis=1).astype(bf16)
        merged_ref[...] = merged
        y = jnp.dot(merged, wo_ref[...], preferred_element_type=f32)
        y_ref[...] = y
        r = lax.rsqrt(jnp.mean(y * y, axis=-1, keepdims=True) + EPS)
        out = x_ref[...] + (y * r) * g_ref[...]
        if with_loss:
            sq_ref = rest[4]
            e = out - rest[0][...]
            out_ref[...] = e / float(D)

            @pl.when(pl.program_id(0) == 0)
            def _():
                sq_ref[...] = jnp.zeros_like(sq_ref)

            sq_ref[...] += _colsum(e * e)
        else:
            out_ref[...] = out
            rn = lax.rsqrt(jnp.mean(out * out, axis=-1, keepdims=True) + EPS)
            h = (out * rn) * rest[0][...]
            rest[4][...] = h.astype(bf16)
            rest[5][...] = h.T.astype(bf16)

    act = pl.BlockSpec((tm, WC), lambda i: (i, 0))
    row = pl.BlockSpec((tm, D), lambda i: (i, 0))
    vec = pl.BlockSpec((1, D), lambda i: (0, 0))
    if with_loss:
        last_in, last_specs, last_shapes = tgt, [row, [vec]], [jax.ShapeDtypeStruct((1, D), f32)]
    else:
        last_in, last_specs = next_g, [vec, [row, pl.BlockSpec((D, tm), lambda i: (0, i))]]
        last_shapes = [jax.ShapeDtypeStruct((T, D), bf16), jax.ShapeDtypeStruct((D, T), bf16)]
    return pl.pallas_call(
        body, grid=(T // tm,),
        in_specs=[act, act, act] + _gate_specs(tm) + [row, _resident((WC, D)), _resident((WC, D)), _resident((WC, D)),
                                                      _resident((D, D)), _resident((1, D)), last_specs[0]],
        out_specs=[row, row, row] + last_specs[1],
        out_shape=[jax.ShapeDtypeStruct((T, D), f32), jax.ShapeDtypeStruct((T, D), bf16), jax.ShapeDtypeStruct((T, D), f32)]
        + last_shapes,
        name="tail_fwd", compiler_params=_params(("arbitrary",)))(
            *acts, z, z, x2, lw["w_conv_out"], lw["w_attn_out"], lw["w_pool_out"], lw["w_out"], lw["post_g"], last_in)


def _tail_bwd(z, dout, y, merged, acts, lw):
    T = z.shape[0]
    tm = 256
    nt = (((1,), (1,)), ((), ()))
    tn = (((0,), (0,)), ((), ()))

    def body(d_ref, y_ref, m_ref, cv_ref, at_ref, pv_ref, ga_ref, gb_ref, wc_ref, wa_ref, wp_ref, wo_ref, g_ref,
             dz_ref, dcv_ref, dat_ref, dpv_ref, dg_ref, dw_ref):
        @pl.when(pl.program_id(0) == 0)
        def _():
            dg_ref[...] = jnp.zeros_like(dg_ref)
            dw_ref[...] = jnp.zeros_like(dw_ref)

        y = y_ref[...]
        d = d_ref[...]
        r = lax.rsqrt(jnp.mean(y * y, axis=-1, keepdims=True) + EPS)
        yn = y * r
        dyn = d * g_ref[...]
        dy = (r * (dyn - yn * jnp.mean(dyn * yn, axis=-1, keepdims=True))).astype(bf16)
        dg_ref[...] += _colsum(d * yn)
        dw_ref[pl.ds(3 * WC, D), :] += lax.dot_general(dy, m_ref[...], tn, preferred_element_type=f32)
        dmerged = lax.dot_general(dy, wo_ref[...], nt, preferred_element_type=f32)
        for br, (a_ref, w_ref, da_ref) in enumerate(((cv_ref, wc_ref, dcv_ref), (at_ref, wa_ref, dat_ref),
                                                     (pv_ref, wp_ref, dpv_ref))):
            yb = jnp.dot(a_ref[...], w_ref[...], preferred_element_type=f32)
            halves = []
            for half in range(2):
                cols = slice(half * WC, (half + 1) * WC)
                s = _gate_block(ga_ref, gb_ref, br, half)
                dm = dmerged[:, cols]
                halves.append((dm * s).astype(bf16))
                dz_ref[2 * br + half] = (dm * yb[:, cols] * s * (1.0 - s)).astype(bf16)
            dyb = jnp.concatenate(halves, axis=1)
            da_ref[...] = lax.dot_general(dyb, w_ref[...], nt, preferred_element_type=f32)
            dw_ref[pl.ds(br * WC, WC), :] += lax.dot_general(a_ref[...], dyb, tn, preferred_element_type=f32)

    act = pl.BlockSpec((tm, WC), lambda i: (i, 0))
    row = pl.BlockSpec((tm, D), lambda i: (i, 0))

    def whole(shape):
        return pl.BlockSpec(shape, lambda i: (0, 0))

    return pl.pallas_call(
        body, grid=(T // tm,),
        in_specs=[row, row, row, act, act, act] + _gate_specs(tm) + [_resident((WC, D)), _resident((WC, D)), _resident((WC, D)),
                                                                     _resident((D, D)), _resident((1, D))],
        out_specs=[pl.BlockSpec((6, tm, WC), lambda i: (DZ_GM // 6, i, 0)), act, act, act, whole((1, D)),
                   whole((TAIL_ROWS, D))],
        out_shape=[jax.ShapeDtypeStruct((DZ_BLOCKS, T, WC), bf16)] + [jax.ShapeDtypeStruct((T, WC), f32)] * 3
        + [jax.ShapeDtypeStruct((1, D), f32), jax.ShapeDtypeStruct((TAIL_ROWS, D), f32)],
        name="tail_bwd", compiler_params=_params(("arbitrary",)))(
            dout, y, merged, *acts, z, z, lw["w_conv_out"], lw["w_attn_out"], lw["w_pool_out"], lw["w_out"], lw["post_g"])


def _adamw(name, g, w, m, v):
    R, C = w.shape
    tr = R
    for cand in (512, 256, 248, 128, 64, 32, 16, 8):
        if R % cand == 0 and cand * C * 4 <= 2 * 1024 * 1024:
            tr = cand
            break
    c1 = 1.0 - ADAM_B1
    c2 = 1.0 - ADAM_B2
    bc1 = 1.0 - ADAM_B1 ** ADAM_STEP
    bc2 = 1.0 - ADAM_B2 ** ADAM_STEP

    def body(g_ref, w_ref, m_ref, v_ref, d_ref, nm_ref, nv_ref):
        g_ = g_ref[...]
        nm = ADAM_B1 * m_ref[...] + c1 * g_
        nv = ADAM_B2 * v_ref[...] + c2 * (g_ * g_)
        nm_ref[...] = nm
        nv_ref[...] = nv
        d_ref[...] = -ADAM_LR * ((nm / bc1) / (jnp.sqrt(nv / bc2) + ADAM_EPS) + ADAM_WD * w_ref[...])

    spec = pl.BlockSpec((tr, C), lambda i: (i, 0))
    return pl.pallas_call(
        body, grid=(R // tr,), in_specs=[spec] * 4, out_specs=[spec] * 3,
        out_shape=[jax.ShapeDtypeStruct((R, C), f32)] * 3, name=name,
        compiler_params=_params(("parallel",)))(g, w, m, v)


def _sum_slots(name, parts):
    _, R, C = parts.shape
    tr = R
    for cand in (256, 128, 64, 32, 16, 8):
        if R % cand == 0 and cand * C * 4 * N_DEV <= 8 * 1024 * 1024:
            tr = cand
            break

    def body(p_ref, o_ref):
        acc = p_ref[0].astype(f32)
        for s in range(1, N_DEV):
            acc = acc + p_ref[s].astype(f32)
        o_ref[...] = acc

    return pl.pallas_call(
        body, grid=(R // tr,), in_specs=[pl.BlockSpec((N_DEV, tr, C), lambda i: (0, i, 0))],
        out_specs=pl.BlockSpec((tr, C), lambda i: (i, 0)), out_shape=jax.ShapeDtypeStruct((R, C), f32),
        name=name, compiler_params=_params(("parallel",)))(parts)


def _row_tile(rows, row_bytes, budget):
    for cand in (512, 256, 128, 64, 32, 16):
        if rows % cand == 0 and cand * row_bytes <= budget:
            return cand
    return rows


def _pair_sum(core, g, theirs):
    R2, C4 = theirs.shape
    tr = _row_tile(R2, C4 * 2, 2 * 1024 * 1024)
    nb = R2 // tr

    def body(core_ref, g_ref, t_ref, o_ref):
        o_ref[...] = (g_ref[...].astype(f32) + t_ref[...].astype(f32)).astype(bf16)

    return pl.pallas_call(
        body,
        grid_spec=pltpu.PrefetchScalarGridSpec(
            num_scalar_prefetch=1, grid=(nb,),
            in_specs=[pl.BlockSpec((tr, C4), lambda i, core_ref: (core_ref[0] * nb + i, 0)),
                      pl.BlockSpec((tr, C4), lambda i, core_ref: (i, 0))],
            out_specs=pl.BlockSpec((tr, C4), lambda i, core_ref: (i, 0))),
        out_shape=jax.ShapeDtypeStruct((R2, C4), bf16), name="pair_sum",
        compiler_params=_params(("parallel",)))(core, g, theirs)


def _chip_sum(chip, mine, others):
    _, R2, C = others.shape
    tr = _row_tile(R2, C * 4, 1024 * 1024)

    def body(chip_ref, m_ref, o_ref, out_ref):
        acc = m_ref[...].astype(f32)
        for s in range(N_CHIPS - 1):
            acc = acc + o_ref[s].astype(f32)
        out_ref[...] = acc

    return pl.pallas_call(
        body,
        grid_spec=pltpu.PrefetchScalarGridSpec(
            num_scalar_prefetch=1, grid=(R2 // tr,),
            in_specs=[pl.BlockSpec((tr, C), lambda i, chip_ref: (i, chip_ref[0])),
                      pl.BlockSpec((N_CHIPS - 1, tr, C), lambda i, chip_ref: (0, i, 0))],
            out_specs=pl.BlockSpec((tr, C), lambda i, chip_ref: (i, 0))),
        out_shape=jax.ShapeDtypeStruct((R2, C), f32), name="chip_sum",
        compiler_params=_params(("parallel",)))(chip, mine, others)


def _place():
    x, y, c = lax.axis_index("x"), lax.axis_index("y"), lax.axis_index("c")
    return x, y, c


def _flip(v, bit):
    return 1 - v if bit else v


CHIP_FLIPS = ((1, 0), (0, 1), (1, 1))


class _Sems:
    def __init__(self, send, recv):
        self.send, self.recv = send, recv
        self.pairs = 0

    def pair(self):
        k = self.pairs
        self.pairs += 1
        return self.send.at[k], self.recv.at[k]


def _remote(src, dst, lands, sems, to):
    s, r = sems.pair()
    copy = pltpu.make_async_remote_copy(src_ref=src, dst_ref=dst, send_sem=s, recv_sem=r, device_id=to, device_id_type=MESH)
    wait = pltpu.make_async_remote_copy(src_ref=lands, dst_ref=lands, send_sem=s, recv_sem=r, device_id=to, device_id_type=MESH)
    return copy, wait


def _exchange(name, build, srcs, lands, n_remote):
    n_s, n_l = len(srcs), len(lands)

    def body(*refs):
        send, recv = refs[n_s + 2 * n_l:]
        remotes, recvs = build(refs[:n_s], refs[n_s + n_l:n_s + 2 * n_l], _Sems(send, recv))
        for cp in remotes:
            cp.start()
        for rv in recvs:
            rv.wait_recv()
        for cp in remotes:
            cp.wait_send()

    return pl.pallas_call(
        body, in_specs=[ANY] * (n_s + n_l), out_specs=[ANY] * n_l,
        out_shape=[jax.ShapeDtypeStruct(t.shape, t.dtype) for t in lands],
        scratch_shapes=[pltpu.SemaphoreType.DMA((n_remote,)), pltpu.SemaphoreType.DMA((n_remote,))],
        input_output_aliases={n_s + i: i for i in range(n_l)}, name=name)(*srcs, *lands)


HBM = pl.BlockSpec(memory_space=pltpu.HBM)
SEMS = pl.BlockSpec(memory_space=pltpu.SEMAPHORE)
DATAFLOW = pltpu.SideEffectType.DATAFLOW_SIDE_EFFECTING


def _start(name, build, srcs, lands, n_remote, after):
    n_s, n_l = len(srcs), len(lands)

    def body(*refs):
        send, recv = refs[n_s + n_l + 1], refs[n_s + n_l + 2]
        remotes, _ = build(refs[:n_s], refs[n_s:n_s + n_l], _Sems(send, recv))
        for cp in remotes:
            cp.start()
        refs[-1][...] = jnp.zeros((8, 128), f32)

    arrays = [pltpu.with_memory_space_constraint(a, pltpu.HBM) for a in (*srcs, *lands)]
    out = pl.pallas_call(
        body, name=name, in_specs=[HBM] * (n_s + n_l) + [ANY],
        out_specs=(SEMS, SEMS, *[HBM] * (n_s + n_l), pl.BlockSpec(memory_space=pltpu.VMEM)),
        out_shape=(pltpu.SemaphoreType.DMA((n_remote,)), pltpu.SemaphoreType.DMA((n_remote,)),
                   *[pltpu.HBM(a.shape, a.dtype) for a in arrays], jax.ShapeDtypeStruct((8, 128), f32)),
        input_output_aliases={i: 2 + i for i in range(n_s + n_l)},
        compiler_params=pltpu.CompilerParams(has_side_effects=DATAFLOW))(*arrays, after)
    return dict(name=name, build=build, sems=out[:2], srcs=out[2:2 + n_s], lands=out[2 + n_s:2 + n_s + n_l], token=out[-1])


def _wait(started, after):
    srcs, lands, build = started["srcs"], started["lands"], started["build"]
    n_s, n_l = len(srcs), len(lands)
    after = list(after) if isinstance(after, (list, tuple)) else [after]

    def body(*refs):
        send, recv = refs[n_s + n_l], refs[n_s + n_l + 1]
        remotes, recvs = build(refs[:n_s], refs[n_s:n_s + n_l], _Sems(send, recv))
        for rv in recvs:
            rv.wait_recv()
        for cp in remotes:
            cp.wait_send()

    out = pl.pallas_call(
        body, name=started["name"] + "_wait", in_specs=[HBM] * (n_s + n_l) + [SEMS, SEMS] + [ANY] * len(after),
        out_specs=[HBM] * (n_s + n_l), out_shape=[pltpu.HBM(a.shape, a.dtype) for a in (*srcs, *lands)],
        input_output_aliases={i: i for i in range(n_s + n_l)},
        compiler_params=pltpu.CompilerParams(has_side_effects=DATAFLOW))(*srcs, *lands, *started["sems"], *after)
    return out[:n_s], out[n_s:]


def _gather_plans(n_split, n_all):
    def over_ici(src, land, sems):
        x, y, c = _place()
        chip = 2 * x + y
        remotes, recvs = [], []
        for a in range(n_all):
            for fx, fy in CHIP_FLIPS:
                px, py = _flip(x, fx), _flip(y, fy)
                if a < n_split:
                    r2 = src[a].shape[0] // 2
                    rows = pl.ds(c * r2, r2)
                    cp, rv = _remote(src[a].at[rows], land[a].at[chip, rows], land[a].at[2 * px + py, rows], sems, (px, py, c))
                else:
                    cp, rv = _remote(src[a], land[a].at[chip], land[a].at[2 * px + py], sems, (px, py, c))
                remotes.append(cp)
                recvs.append(rv)
        return remotes, recvs

    def over_d2d(src, land, sems):
        x, y, c = _place()
        remotes, recvs = [], []
        for a in range(n_split):
            r2 = land[a].shape[1] // 2
            for fx, fy in CHIP_FLIPS:
                owner = 2 * _flip(x, fx) + _flip(y, fy)
                mine = land[a].at[owner, pl.ds(c * r2, r2)]
                cp, rv = _remote(mine, mine, land[a].at[owner, pl.ds((1 - c) * r2, r2)], sems, (x, y, 1 - c))
                remotes.append(cp)
                recvs.append(rv)
        return remotes, recvs

    return over_ici, over_d2d


def _gather_begin(tag, shards, n_split, after):
    over_ici, _ = _gather_plans(n_split, len(shards))
    lands = [lax.empty((N_CHIPS,) + s.shape, s.dtype) for s in shards]
    return _start("gather_ici_" + tag, over_ici, shards, lands, 3 * len(shards), after)


def _gather_end(started, shards, n_split, after):
    _, over_d2d = _gather_plans(n_split, len(shards))
    lands = _exchange("gather_d2d", over_d2d, [], _wait(started, after)[1], 3 * n_split)
    chip = 2 * lax.axis_index("x") + lax.axis_index("y")
    return [lax.dynamic_update_slice_in_dim(g, s[None], chip, axis=0) for g, s in zip(lands, shards)]


def _reduce_plans(n):
    def to_sibling(src, land, sems):
        x, y, c = _place()
        remotes, recvs = [], []
        for a in range(n):
            r2 = src[a].shape[0] // 2
            cp, rv = _remote(src[a].at[pl.ds((1 - c) * r2, r2), :], land[a], land[a], sems, (x, y, 1 - c))
            remotes.append(cp)
            recvs.append(rv)
        return remotes, recvs

    def across_chips(src, land, sems):
        x, y, c = _place()
        remotes, recvs = [], []
        for a in range(n):
            cw = src[a].shape[1] // N_CHIPS
            for k, (fx, fy) in enumerate(CHIP_FLIPS):
                px, py = _flip(x, fx), _flip(y, fy)
                cp, rv = _remote(src[a].at[:, pl.ds((2 * px + py) * cw, cw)], land[a].at[k], land[a].at[k], sems, (px, py, c))
                remotes.append(cp)
                recvs.append(rv)
        return remotes, recvs

    def share(src, land, sems):
        x, y, c = _place()
        remotes, recvs = [], []
        for a in range(n):
            cp, rv = _remote(src[a], land[a], land[a], sems, (x, y, 1 - c))
            remotes.append(cp)
            recvs.append(rv)
        return remotes, recvs

    return to_sibling, across_chips, share


def _reduce_begin(grads):
    n = len(grads)
    to_sibling, across_chips, _ = _reduce_plans(n)
    core = lax.axis_index("c").reshape(1).astype(jnp.int32)
    theirs = _exchange("reduce_pair", to_sibling, grads,
                       [lax.empty((g.shape[0] // 2, g.shape[1]), bf16) for g in grads], n)
    pair = [_pair_sum(core, g, t) for g, t in zip(grads, theirs)]
    lands = [lax.empty((N_CHIPS - 1, g.shape[0] // 2, g.shape[1] // N_CHIPS), bf16) for g in grads]
    return _start("reduce_chips", across_chips, pair, lands, 3 * n, pair[0])


def _reduce_end(started, after):
    x, y, c = _place()
    chip = (2 * x + y).reshape(1).astype(jnp.int32)
    pair, others = _wait(started, after)
    _, _, share = _reduce_plans(len(pair))
    mine = [_chip_sum(chip, p, o) for p, o in zip(pair, others)]
    sibs = _exchange("reduce_share", share, mine, [lax.empty(h.shape, f32) for h in mine], len(mine))
    return [jnp.where(c == 0, jnp.concatenate([h, s], axis=0), jnp.concatenate([s, h], axis=0))
            for h, s in zip(mine, sibs)]


def _to_all(src, land, sems):
    x, y, c = _place()
    me = 4 * x + 2 * y + c
    remotes, recvs = [], []
    for k in range(1, N_DEV):
        px, py, pc = _flip(x, (k >> 2) & 1), _flip(y, (k >> 1) & 1), _flip(c, k & 1)
        cp, rv = _remote(src[0], land[0].at[me], land[0].at[4 * px + 2 * py + pc], sems, (px, py, pc))
        remotes.append(cp)
        recvs.append(rv)
    return remotes, recvs


def _gather_small_begin(packed):
    return _start("gather_small", _to_all, [packed], [lax.empty((N_DEV,) + packed.shape, f32)], N_DEV - 1, packed)


def _gather_small_end(started, after):
    (packed,), (others,) = _wait(started, after)
    x, y, c = _place()
    return lax.dynamic_update_slice_in_dim(others, packed[None], 4 * x + 2 * y + c, axis=0)


def _gather_all(packed):
    others = _exchange("gather_all", _to_all, [packed], [lax.empty((N_DEV,) + packed.shape, f32)], N_DEV - 1)[0]
    x, y, c = _place()
    return lax.dynamic_update_slice_in_dim(others, packed[None], 4 * x + 2 * y + c, axis=0)


def _rows8(v):
    return jnp.pad(v[None, :], ((0, 7), (0, 0)))


def _vec_rows(vs):
    return jnp.pad(jnp.stack(vs), ((0, 8 - len(vs)), (0, 0)))


SMALL_ROWS = 224


def _pack_small(conv_vec, conv_dw, pool_vec, pool_w, pre_g, post_g, rel):
    return jnp.concatenate([
        conv_vec, conv_dw, pool_vec, pool_w.reshape(GD, WC),
        _rows8(pre_g).reshape(16, WC), _rows8(post_g).reshape(16, WC),
        jnp.pad(rel, ((0, 0), (0, D - rel.shape[1]))).reshape(16, WC)], axis=0)


def _unpack_small(p):
    conv_vec, pool_vec = p[0:8], p[40:48]
    return dict(
        conv_dw_b=conv_vec[0], conv_ln_g=conv_vec[1], conv_ln_b=conv_vec[2], conv_dw=p[8:8 + CONV_K],
        pool_b=pool_vec[0].reshape(4, GD), pool_scale=pool_vec[1], pool_w=p[48:176].reshape(4, GD, GD),
        pre_norm_g=p[176:192].reshape(8, D)[0], post_norm_g=p[192:208].reshape(8, D)[0],
        rel_bias=p[208:224].reshape(8, D)[:, :2 * MAX_REL + 1])


def _layer_fwd(x2, ht, z, lw, BL, SEQ, next_g=None, tgt=None):
    cv, u1 = _conv_fwd(z, lw["dw32"], lw["cvec"], BL, SEQ)
    at = _attn_fwd(z, lw["bm"], BL, SEQ)
    pv = _pool_fwd(z, lw["pw"], lw["pvec"], BL, SEQ)
    out, merged, y, *last = _tail_fwd(z, (cv, at, pv), x2, lw, next_g, tgt)
    return (out, *last), dict(x=x2, ht=ht, z=z, u1=u1, acts=(cv, at, pv), merged=merged, y=y)


def _layer_bwd(dout, sv, lw, BL, SEQ, meanwhile=None):
    tail = _tail_bwd(sv["z"], dout, sv["y"], sv["merged"], sv["acts"], lw)
    dz, dacts, dpost = tail[0], tail[1:4], tail[4]
    dw_tail = tail[5].astype(bf16)
    if meanwhile is not None:
        meanwhile(dw_tail)
    dz, ddw, dcvec = _conv_bwd(sv["z"], sv["u1"], dacts[0], dz, lw["dw32"], lw["cvec"], BL, SEQ)
    dz, dbm = _attn_bwd(sv["z"], dacts[1], dz, lw["bm"], BL, SEQ)
    dz, dpw, dpvec = _pool_bwd(sv["z"], dacts[2], dz, lw["pw"], lw["pvec"], BL, SEQ)
    drel = _bias_table_grad(_bias_colsum(_bias_fold(dbm)))
    small_gather = _gather_small_begin(_pack_small(dcvec, ddw, dpvec, dpw, jnp.zeros((D,), f32), dpost[0], drel))
    dw_in = _mm_dw_in(sv["ht"], dz, small_gather["token"])
    reduction = _reduce_begin([dw_in, dw_tail])
    dx, dpre = _mm_dx(dz, lw["w_in"], sv["x"], lw["pre_g"], dout, reduction["token"])
    return dx, reduction, small_gather, dpre


BIG = ("w_in", "w_conv_out", "w_attn_out", "w_pool_out", "w_out")
PRE_ROWS = slice(176, 192)


def _layer_shards(w, l):
    return [w[k][l].astype(bf16) for k in BIG] + [w["conv_dw"][l]]


def _side_by_side(g):
    return jnp.transpose(g, (1, 0, 2)).reshape(g.shape[1], N_CHIPS * g.shape[2])


def _layer_weights(w_in, gathered, w, l, bm):
    lw = {k: _side_by_side(g) for k, g in zip(BIG[1:4], gathered[:3])}
    lw["w_in"] = w_in
    lw["w_out"] = gathered[3].reshape(D, D)
    lw["pre_g"] = w["pre_norm_g"][l][None]
    lw["post_g"] = w["post_norm_g"][l][None]
    lw["dw32"] = jnp.pad(_side_by_side(gathered[4]), ((0, 32 - CONV_K), (0, 0)))
    lw["cvec"] = _vec_rows([w["conv_dw_b"][l], w["conv_ln_g"][l], w["conv_ln_b"][l]])
    lw["bm"] = bm
    lw["pw"] = w["pool_w"][l].astype(bf16)
    lw["pvec"] = _vec_rows([w["pool_b"][l].reshape(WC), w["pool_scale"][l]])
    return lw


SMALL = ("pre_norm_g", "post_norm_g", "conv_dw_b", "conv_ln_g", "conv_ln_b", "rel_bias", "pool_w", "pool_b", "pool_scale")
ORDER = ("pre_norm_g", "post_norm_g", "w_in", "conv_dw", "conv_dw_b", "conv_ln_g", "conv_ln_b", "w_conv_out",
         "rel_bias", "w_attn_out", "pool_w", "pool_b", "pool_scale", "w_pool_out", "w_out")


def _pack_small_params(p):
    return jnp.concatenate([
        _pack_small(_vec_rows([p["conv_dw_b"][l], p["conv_ln_g"][l], p["conv_ln_b"][l]]), jnp.zeros((32, WC), f32),
                    _vec_rows([p["pool_b"][l].reshape(WC), p["pool_scale"][l]]), p["pool_w"][l],
                    p["pre_norm_g"][l], p["post_norm_g"][l], p["rel_bias"][l])
        for l in range(DEPTH)], axis=0)


def _unpack_small_params(packed):
    layers = [_unpack_small(packed[l * SMALL_ROWS:(l + 1) * SMALL_ROWS]) for l in range(DEPTH)]
    return {k: jnp.stack([layers[l][k] for l in range(DEPTH)]) for k in layers[0]}


def kernel(x, pre_norm_g, post_norm_g, w_in, conv_dw, conv_dw_b, conv_ln_g, conv_ln_b, w_conv_out, rel_bias, w_attn_out, pool_w, pool_b, pool_scale, w_pool_out, w_out, loss_target, m_pre_norm_g, m_post_norm_g, m_w_in, m_conv_dw, m_conv_dw_b, m_conv_ln_g, m_conv_ln_b, m_w_conv_out, m_rel_bias, m_w_attn_out, m_pool_w, m_pool_b, m_pool_scale, m_w_pool_out, m_w_out, v_pre_norm_g, v_post_norm_g, v_w_in, v_conv_dw, v_conv_dw_b, v_conv_ln_g, v_conv_ln_b, v_w_conv_out, v_rel_bias, v_w_attn_out, v_pool_w, v_pool_b, v_pool_scale, v_w_pool_out, v_w_out):
    BL, SEQ, _ = x.shape
    T = BL * SEQ
    w = dict(pre_norm_g=pre_norm_g, post_norm_g=post_norm_g, w_in=w_in, conv_dw=conv_dw, conv_dw_b=conv_dw_b,
             conv_ln_g=conv_ln_g, conv_ln_b=conv_ln_b, w_conv_out=w_conv_out, rel_bias=rel_bias, w_attn_out=w_attn_out,
             pool_w=pool_w, pool_b=pool_b, pool_scale=pool_scale, w_pool_out=w_pool_out, w_out=w_out)
    m = dict(pre_norm_g=m_pre_norm_g, post_norm_g=m_post_norm_g, w_in=m_w_in, conv_dw=m_conv_dw, conv_dw_b=m_conv_dw_b,
             conv_ln_g=m_conv_ln_g, conv_ln_b=m_conv_ln_b, w_conv_out=m_w_conv_out, rel_bias=m_rel_bias,
             w_attn_out=m_w_attn_out, pool_w=m_pool_w, pool_b=m_pool_b, pool_scale=m_pool_scale,
             w_pool_out=m_w_pool_out, w_out=m_w_out)
    v = dict(pre_norm_g=v_pre_norm_g, post_norm_g=v_post_norm_g, w_in=v_w_in, conv_dw=v_conv_dw, conv_dw_b=v_conv_dw_b,
             conv_ln_g=v_conv_ln_g, conv_ln_b=v_conv_ln_b, w_conv_out=v_w_conv_out, rel_bias=v_rel_bias,
             w_attn_out=v_w_attn_out, pool_w=v_pool_w, pool_b=v_pool_b, pool_scale=v_pool_scale,
             w_pool_out=v_w_pool_out, w_out=v_w_out)

    shards = [_layer_shards(w, l) for l in range(DEPTH)]
    x2 = x.reshape(T, D)
    h0, ht0 = _rms_pre(x2, pre_norm_g[0][None])
    first = _gather_begin("w_in0", shards[0][:1], 1, x2)
    bms = [_bias_matrix(rel_bias[l]) for l in range(DEPTH)]
    packs = [_pack_small_params(p) for p in (w, m, v)]
    w_in0 = _side_by_side(_gather_end(first, shards[0][:1], 1, [ht0, *bms, *packs])[0])
    rest0 = _gather_begin("rest0", shards[0][1:], 4, w_in0)
    all1 = _gather_begin("layer1", shards[1], 5, rest0["token"])
    z0 = _in_proj(h0, w_in0, after=all1["token"])
    lw0 = _layer_weights(w_in0, _gather_end(rest0, shards[0][1:], 4, z0), w, 0, bms[0])
    (out0, h1, ht1), saved0 = _layer_fwd(x2, ht0, z0, lw0, BL, SEQ, next_g=pre_norm_g[1][None])
    gathered1 = _gather_end(all1, shards[1], 5, out0)
    lw1 = _layer_weights(_side_by_side(gathered1[0]), gathered1[1:], w, 1, bms[1])
    z1 = _in_proj(h1, lw1["w_in"], after=h1)
    (dout, sq), saved1 = _layer_fwd(out0, ht1, z1, lw1, BL, SEQ, tgt=loss_target.reshape(T, D))
    loss = lax.psum(0.5 * jnp.sum(sq) / float(D), ("x", "y", "c"))

    summed = [None] * DEPTH
    dx1, reduction1, small_gather1, dpre1 = _layer_bwd(dout, saved1, lw1, BL, SEQ)

    def finish_layer1(after):
        summed[1] = _reduce_end(reduction1, after)

    grad_x, reduction0, small_gather0, dpre0 = _layer_bwd(dx1, saved0, lw0, BL, SEQ, meanwhile=finish_layer1)
    summed[0] = _reduce_end(reduction0, grad_x)
    dpre = _sum_slots("sum_small", _gather_all(jnp.concatenate([_rows8(dpre0[0]), _rows8(dpre1[0])], axis=0)))
    gsmall = []
    for l, started in enumerate((small_gather0, small_gather1)):
        g = _sum_slots("sum_small", _gather_small_end(started, dpre))
        gsmall += [g[:PRE_ROWS.start], dpre[8 * l:8 * l + 8].reshape(16, WC), g[PRE_ROWS.stop:]]
    gsmall = jnp.concatenate(gsmall, axis=0)

    grads, deltas, new_m, new_v = {}, {}, {}, {}
    for i, k in enumerate(BIG):
        if k == "w_in":
            g = jnp.stack([summed[l][0] for l in range(DEPTH)])
        elif k == "w_out":
            g = jnp.stack([summed[l][1][3 * WC:].T for l in range(DEPTH)])
        else:
            g = jnp.stack([summed[l][1][(i - 1) * WC:i * WC] for l in range(DEPTH)])
        grads[k] = g
        shape = w[k].shape
        flat2 = lambda a: a.reshape(shape[0] * shape[1], shape[2])
        d_, nm_, nv_ = _adamw("adamw_big", flat2(g), flat2(w[k]), flat2(m[k]), flat2(v[k]))
        deltas[k], new_m[k], new_v[k] = d_.reshape(shape), nm_.reshape(shape), nv_.reshape(shape)

    d_, nm_, nv_ = _adamw("adamw_small", gsmall, *packs)
    gs, ds, ms, vs = (_unpack_small_params(a) for a in (gsmall, d_, nm_, nv_))
    for k in SMALL:
        grads[k], deltas[k], new_m[k], new_v[k] = gs[k], ds[k], ms[k], vs[k]
    chip = 2 * lax.axis_index("x") + lax.axis_index("y")
    g_dw = lax.dynamic_slice_in_dim(gs["conv_dw"], chip * GD, GD, axis=2)
    flat2 = lambda a: a.reshape(DEPTH * CONV_K, GD)
    d_, nm_, nv_ = _adamw("adamw_conv_dw", flat2(g_dw), flat2(conv_dw), flat2(m["conv_dw"]), flat2(v["conv_dw"]))
    grads["conv_dw"] = g_dw
    deltas["conv_dw"], new_m["conv_dw"], new_v["conv_dw"] = (a.reshape(conv_dw.shape) for a in (d_, nm_, nv_))

    return (loss, grad_x.reshape(x.shape), *[grads[k] for k in ORDER], *[deltas[k] for k in ORDER],
            *[new_m[k] for k in ORDER], *[new_v[k] for k in ORDER])
```

```python
import numpy as np
import jax
import jax.numpy as jnp
from jax import lax
from jax.experimental import pallas as pl
from jax.experimental.pallas import tpu as pltpu

f32 = jnp.float32
bf16 = jnp.bfloat16

D = 1024
DEPTH = 2
WC = 512
HEAD_DIM = 64
CHUNK = 64
LEFT_CHUNKS = 8
KEY_PAD = LEFT_CHUNKS * CHUNK
MAX_REL = 256
CONV_K = 31
POOL_WINDOWS = (2, 4, 8, 16)
GD = 128
NCOL = 7680
EPS = 1e-6
NEG_INF = -1e30
COL_A, COL_B, COL_CG, COL_Q, COL_K, COL_V, COL_AG, COL_PI, COL_PG, COL_GM = (
    0, 512, 1024, 1536, 2048, 2560, 3072, 3584, 4096, 4608)

ADAM_LR = 0.001
ADAM_B1 = 0.9
ADAM_B2 = 0.999
ADAM_EPS = 1e-08
ADAM_WD = 0.01
ADAM_STEP = 10

QG = 256
KW = KEY_PAD + QG
BIAS_VARIANTS = KEY_PAD // QG + 1
CT = 256
HALO = 32
PHALO = 16
N_CHIPS = 4
N_DEV = 8
VMEM_LIMIT = 56 * 1024 * 1024
MESH = pl.DeviceIdType.MESH
ANY = pl.BlockSpec(memory_space=pl.ANY)

DZ_BLOCKS = 18
DZ_CONV, DZ_ATTN, DZ_POOL, DZ_GM = 0, 4, 8, 12


def _dz_block(c):
    return c + (c >= 3).astype(jnp.int32) + 2 * (c >= 9).astype(jnp.int32)


def _params(sem=None):
    return pltpu.CompilerParams(dimension_semantics=sem, vmem_limit_bytes=VMEM_LIMIT)


def _sig(x):
    return 1.0 / (1.0 + jnp.exp(-x))


def _dsilu(x, s):
    return s * (1.0 + x * (1.0 - s))


def _colsum(x):
    return jnp.sum(x, axis=0, keepdims=True)


def _rms_pre(x2, g):
    T = x2.shape[0]
    tm = 512

    def body(x_ref, g_ref, h_ref, ht_ref):
        x = x_ref[...]
        r = lax.rsqrt(jnp.mean(x * x, axis=-1, keepdims=True) + EPS)
        h = (x * r) * g_ref[...]
        h_ref[...] = h.astype(bf16)
        ht_ref[...] = h.T.astype(bf16)

    row = pl.BlockSpec((tm, D), lambda i: (i, 0))
    vec = pl.BlockSpec((1, D), lambda i: (0, 0))
    return pl.pallas_call(
        body, grid=(T // tm,), in_specs=[row, vec], out_specs=[row, pl.BlockSpec((D, tm), lambda i: (0, i))],
        out_shape=[jax.ShapeDtypeStruct((T, D), bf16), jax.ShapeDtypeStruct((D, T), bf16)], name="rms_pre",
        compiler_params=_params(("parallel",)))(x2, g)


def _in_proj(h, w_in, after):
    T = h.shape[0]
    tm, tn = 512, 1536

    def body(h_ref, w_ref, after_ref, z_ref):
        for n0 in range(0, NCOL, tn):
            z_ref[:, n0:n0 + tn] = jnp.dot(h_ref[...], w_ref[:, n0:n0 + tn], preferred_element_type=f32)

    return pl.pallas_call(
        body, grid=(T // tm,),
        in_specs=[pl.BlockSpec((tm, D), lambda i: (i, 0)),
                  pl.BlockSpec((D, NCOL), lambda i: (0, 0), pipeline_mode=pl.Buffered(1)), ANY],
        out_specs=pl.BlockSpec((tm, NCOL), lambda i: (i, 0)), out_shape=jax.ShapeDtypeStruct((T, NCOL), f32),
        name="mm_in", compiler_params=_params(("parallel",)))(h, w_in, after)


DZ_SPANS = ((DZ_CONV, 3), (DZ_ATTN, 4), (DZ_POOL, 2), (DZ_GM, 6))


def _mm_dx(dz, w_in, x2, g, dout, after):
    T = dz.shape[1]
    tm = 512

    def body(conv_ref, attn_ref, pool_ref, gm_ref, w_ref, x_ref, g_ref, d_ref, after_ref, dx_ref, dg_ref):
        dh = None
        col = 0
        for ref, (_, blocks) in zip((conv_ref, attn_ref, pool_ref, gm_ref), DZ_SPANS):
            for b in range(blocks):
                p = lax.dot_general(ref[b], w_ref[:, col * WC:(col + 1) * WC], (((1,), (1,)), ((), ())),
                                    preferred_element_type=f32)
                dh = p if dh is None else dh + p
                col += 1
        x = x_ref[...]
        r = lax.rsqrt(jnp.mean(x * x, axis=-1, keepdims=True) + EPS)
        xn = x * r
        dxn = dh * g_ref[...]
        dx_ref[...] = r * (dxn - xn * jnp.mean(dxn * xn, axis=-1, keepdims=True)) + d_ref[...]

        @pl.when(pl.program_id(0) == 0)
        def _():
            dg_ref[...] = jnp.zeros_like(dg_ref)

        dg_ref[...] += _colsum(dh * xn)

    spans = [pl.BlockSpec((blocks, tm, WC), lambda i, first=first, blocks=blocks: (first // blocks, i, 0))
             for first, blocks in DZ_SPANS]
    row = pl.BlockSpec((tm, D), lambda i: (i, 0))
    vec = pl.BlockSpec((1, D), lambda i: (0, 0))
    return pl.pallas_call(
        body, grid=(T // tm,),
        in_specs=spans + [pl.BlockSpec((D, NCOL), lambda i: (0, 0), pipeline_mode=pl.Buffered(1)), row, vec, row, ANY],
        out_specs=[row, vec], out_shape=[jax.ShapeDtypeStruct((T, D), f32), jax.ShapeDtypeStruct((1, D), f32)],
        name="mm_dx", compiler_params=_params(("arbitrary",)))(dz, dz, dz, dz, w_in, x2, g, dout, after)


def _mm_dw_in(ht, dz, after):
    T = dz.shape[1]

    def body(ht_ref, dz_ref, after_ref, o_ref):
        o_ref[...] = jnp.dot(ht_ref[...], dz_ref[...], preferred_element_type=f32).astype(bf16)

    return pl.pallas_call(
        body, grid=(NCOL // WC,),
        in_specs=[pl.BlockSpec((D, T), lambda j: (0, 0), pipeline_mode=pl.Buffered(1)),
                  pl.BlockSpec((None, T, WC), lambda j: (_dz_block(j), 0, 0)), ANY],
        out_specs=pl.BlockSpec((D, WC), lambda j: (0, j)), out_shape=jax.ShapeDtypeStruct((D, NCOL), bf16),
        name="mm_dw_in", compiler_params=_params(("parallel",)))(ht, dz, after)


def _conv_delays():
    return [(8 * a + b, a, b) for b in range(8) for a in range(4) if 8 * a + b < CONV_K]


def _conv_rolls(win):
    return [win if b == 0 else pltpu.roll(win, b, axis=0) for b in range(8)]


def _conv_taps(rolled, dw_ref):
    acc = None
    for d, a, b in _conv_delays():
        term = rolled[b][HALO - 8 * a:HALO - 8 * a + CT, :] * dw_ref[pl.ds(CONV_K - 1 - d, 1), :]
        acc = term if acc is None else acc + term
    return acc


def _conv_fwd(z, dw32, cvec, BL, SEQ):
    T = BL * SEQ
    nct = SEQ // CT

    def body(a_ref, b_ref, cg_ref, dw_ref, vec_ref, o_ref, u1_ref, p_ref):
        p_ref[pl.ds(0, HALO), :] = jnp.zeros((HALO, WC), f32)

        def glu(c, carry):
            r0 = pl.multiple_of(c * CT, CT)
            p_ref[pl.ds(r0 + HALO, CT), :] = a_ref[pl.ds(r0, CT), :] * _sig(b_ref[pl.ds(r0, CT), :])
            return carry

        lax.fori_loop(0, nct, glu, 0)

        def step(c, carry):
            r0 = pl.multiple_of(c * CT, CT)
            u1 = _conv_taps(_conv_rolls(p_ref[pl.ds(r0, CT + HALO), :]), dw_ref) + vec_ref[0:1, :]
            u1_ref[pl.ds(r0, CT), :] = u1
            xc = u1 - jnp.mean(u1, axis=-1, keepdims=True)
            rs = lax.rsqrt(jnp.mean(xc * xc, axis=-1, keepdims=True) + EPS)
            u2 = (xc * rs) * vec_ref[1:2, :] + vec_ref[2:3, :]
            cg = cg_ref[pl.ds(r0, CT), :]
            o_ref[pl.ds(r0, CT), :] = ((u2 * _sig(u2)) * (cg * _sig(cg))).astype(bf16)
            return carry

        lax.fori_loop(0, nct, step, 0)

    def zs(col):
        return pl.BlockSpec((SEQ, WC), lambda b: (b, col // WC))

    seq = pl.BlockSpec((SEQ, WC), lambda b: (b, 0))
    return pl.pallas_call(
        body, grid=(BL,),
        in_specs=[zs(COL_A), zs(COL_B), zs(COL_CG), pl.BlockSpec((32, WC), lambda b: (0, 0)),
                  pl.BlockSpec((8, WC), lambda b: (0, 0))],
        out_specs=[seq, seq],
        out_shape=[jax.ShapeDtypeStruct((T, WC), bf16), jax.ShapeDtypeStruct((T, WC), f32)],
        scratch_shapes=[pltpu.VMEM((SEQ + HALO, WC), f32)], name="conv_fwd",
        compiler_params=_params(("parallel",)))(z, z, z, dw32, cvec)


def _conv_bwd(z, u1, dcv, dz, dw32, cvec, BL, SEQ):
    nct = SEQ // CT

    def body(a_ref, b_ref, cg_ref, u1_ref, dcv_ref, dzin_ref, dw_ref, vec_ref, dz_ref, ddw_ref, dvec_ref,
             p_ref, q_ref, taps_ref):
        @pl.when(pl.program_id(0) == 0)
        def _():
            ddw_ref[...] = jnp.zeros_like(ddw_ref)
            dvec_ref[...] = jnp.zeros_like(dvec_ref)

        p_ref[pl.ds(0, HALO), :] = jnp.zeros((HALO, WC), f32)
        q_ref[pl.ds(SEQ, HALO), :] = jnp.zeros((HALO, WC), f32)

        def glu(c, carry):
            r0 = pl.multiple_of(c * CT, CT)
            p_ref[pl.ds(r0 + HALO, CT), :] = a_ref[pl.ds(r0, CT), :] * _sig(b_ref[pl.ds(r0, CT), :])
            return carry

        lax.fori_loop(0, nct, glu, 0)

        def step(c, carry):
            r0 = pl.multiple_of(c * CT, CT)
            rolled = _conv_rolls(p_ref[pl.ds(r0, CT + HALO), :])
            u1 = u1_ref[pl.ds(r0, CT), :]
            xc = u1 - jnp.mean(u1, axis=-1, keepdims=True)
            rs = lax.rsqrt(jnp.mean(xc * xc, axis=-1, keepdims=True) + EPS)
            nrm = xc * rs
            u2 = nrm * vec_ref[1:2, :] + vec_ref[2:3, :]
            s2 = _sig(u2)
            u3 = u2 * s2
            cg = cg_ref[pl.ds(r0, CT), :]
            scg = _sig(cg)
            dcv_ = dcv_ref[pl.ds(r0, CT), :]
            dz_ref[2, pl.ds(r0, CT), :] = (dcv_ * u3 * _dsilu(cg, scg)).astype(bf16)
            du2 = dcv_ * (cg * scg) * _dsilu(u2, s2)
            dvec_ref[1:2, :] += _colsum(du2 * nrm)
            dvec_ref[2:3, :] += _colsum(du2)
            dn = du2 * vec_ref[1:2, :]
            du1 = rs * (dn - jnp.mean(dn, axis=-1, keepdims=True)
                        - nrm * jnp.mean(dn * nrm, axis=-1, keepdims=True))
            dvec_ref[0:1, :] += _colsum(du1)
            q_ref[pl.ds(r0, CT), :] = du1
            for d, a, b in _conv_delays():
                prod = du1 * rolled[b][HALO - 8 * a:HALO - 8 * a + CT, :]
                taps_ref[CONV_K - 1 - d] += jnp.sum(prod.reshape(CT // 8, 8, WC), axis=0)
            return carry

        taps_ref[...] = jnp.zeros_like(taps_ref)
        lax.fori_loop(0, nct, step, 0)
        for row in range(CONV_K):
            ddw_ref[pl.ds(row, 1), :] += _colsum(taps_ref[row])

        def back(c, carry):
            r0 = pl.multiple_of(c * CT, CT)
            wq = q_ref[pl.ds(r0, CT + HALO), :]
            up = {}
            acc = None
            for d, a, b in _conv_delays():
                if b not in up:
                    up[b] = wq if b == 0 else pltpu.roll(wq, CT + HALO - b, axis=0)
                term = up[b][8 * a:8 * a + CT, :] * dw_ref[pl.ds(CONV_K - 1 - d, 1), :]
                acc = term if acc is None else acc + term
            a_ = a_ref[pl.ds(r0, CT), :]
            sb = _sig(b_ref[pl.ds(r0, CT), :])
            dz_ref[0, pl.ds(r0, CT), :] = (acc * sb).astype(bf16)
            dz_ref[1, pl.ds(r0, CT), :] = (acc * a_ * sb * (1.0 - sb)).astype(bf16)
            return carry

        lax.fori_loop(0, nct, back, 0)

    def zs(col):
        return pl.BlockSpec((SEQ, WC), lambda b: (b, col // WC), pipeline_mode=pl.Buffered(1))

    def const(r):
        return pl.BlockSpec((r, WC), lambda b: (0, 0))

    seq = pl.BlockSpec((SEQ, WC), lambda b: (b, 0), pipeline_mode=pl.Buffered(1))
    return pl.pallas_call(
        body, grid=(BL,),
        in_specs=[zs(COL_A), zs(COL_B), zs(COL_CG), seq, seq, ANY, const(32), const(8)],
        out_specs=[pl.BlockSpec((3, SEQ, WC), lambda b: (DZ_CONV // 3, b, 0)), const(32), const(8)],
        out_shape=[jax.ShapeDtypeStruct(dz.shape, bf16), jax.ShapeDtypeStruct((32, WC), f32),
                   jax.ShapeDtypeStruct((8, WC), f32)],
        scratch_shapes=[pltpu.VMEM((SEQ + HALO, WC), f32), pltpu.VMEM((SEQ + HALO, WC), f32),
                        pltpu.VMEM((CONV_K, 8, WC), f32)],
        input_output_aliases={5: 0}, name="conv_bwd",
        compiler_params=_params(("arbitrary",)))(z, z, z, u1, dcv, dz, dw32, cvec)


def _pool_counts(r0):
    t1 = r0 + 1 + lax.broadcasted_iota(jnp.int32, (CT, 1), 0)
    return [jnp.minimum(t1, w).astype(f32) for w in POOL_WINDOWS]


def _pool_sums(win, forward):
    n = CT + PHALO

    def sh(x, s):
        return pltpu.roll(x, (n - s) if forward else s, axis=0)

    s2 = win + sh(win, 1)
    s4 = s2[:, GD:] + sh(s2[:, GD:], 2)
    s8 = s4[:, GD:] + sh(s4[:, GD:], 4)
    s16 = s8[:, GD:] + sh(s8[:, GD:], 8)
    lo = 0 if forward else PHALO
    return [s[lo:lo + CT, :GD] for s in (s2, s4, s8, s16)]


def _pool_fwd(z, pw, pvec, BL, SEQ):
    T = BL * SEQ
    nct = SEQ // CT

    def body(pi_ref, pg_ref, pw_ref, vec_ref, o_ref, p_ref):
        p_ref[pl.ds(0, PHALO), :] = jnp.zeros((PHALO, WC), f32)

        def fill(c, carry):
            r0 = pl.multiple_of(c * CT, CT)
            p_ref[pl.ds(r0 + PHALO, CT), :] = pi_ref[pl.ds(r0, CT), :]
            return carry

        lax.fori_loop(0, nct, fill, 0)

        def step(c, carry):
            r0 = pl.multiple_of(c * CT, CT)
            sums = _pool_sums(p_ref[pl.ds(r0, CT + PHALO), :], False)
            cnt = _pool_counts(r0)
            pin = pi_ref[pl.ds(r0, CT), :]
            mixed = []
            for g in range(4):
                pooled = sums[g] / cnt[g] - pin[:, g * GD:(g + 1) * GD]
                mixed.append(jnp.dot(pooled.astype(bf16), pw_ref[g], preferred_element_type=f32))
            m0 = jnp.concatenate(mixed, axis=1) + vec_ref[0:1, :]
            pg = pg_ref[pl.ds(r0, CT), :]
            o_ref[pl.ds(r0, CT), :] = ((m0 * vec_ref[1:2, :]) * (pg * _sig(pg))).astype(bf16)
            return carry

        lax.fori_loop(0, nct, step, 0)

    def zs(col):
        return pl.BlockSpec((SEQ, WC), lambda b: (b, col // WC))

    return pl.pallas_call(
        body, grid=(BL,),
        in_specs=[zs(COL_PI), zs(COL_PG), pl.BlockSpec((4, GD, GD), lambda b: (0, 0, 0)),
                  pl.BlockSpec((8, WC), lambda b: (0, 0))],
        out_specs=pl.BlockSpec((SEQ, WC), lambda b: (b, 0)),
        out_shape=jax.ShapeDtypeStruct((T, WC), bf16),
        scratch_shapes=[pltpu.VMEM((SEQ + PHALO, WC), f32)], name="pool_fwd",
        compiler_params=_params(("parallel",)))(z, z, pw, pvec)


def _pool_bwd(z, dpl, dz, pw, pvec, BL, SEQ):
    nct = SEQ // CT

    def body(pi_ref, pg_ref, dpl_ref, dzin_ref, pw_ref, vec_ref, dz_ref, dpw_ref, dvec_ref, p_ref, e_ref, dp_ref):
        @pl.when(pl.program_id(0) == 0)
        def _():
            dpw_ref[...] = jnp.zeros_like(dpw_ref)
            dvec_ref[...] = jnp.zeros_like(dvec_ref)

        p_ref[pl.ds(0, PHALO), :] = jnp.zeros((PHALO, WC), f32)
        e_ref[pl.ds(SEQ, PHALO), :] = jnp.zeros((PHALO, WC), f32)

        def fill(c, carry):
            r0 = pl.multiple_of(c * CT, CT)
            p_ref[pl.ds(r0 + PHALO, CT), :] = pi_ref[pl.ds(r0, CT), :]
            return carry

        lax.fori_loop(0, nct, fill, 0)

        def step(c, carry):
            r0 = pl.multiple_of(c * CT, CT)
            sums = _pool_sums(p_ref[pl.ds(r0, CT + PHALO), :], False)
            cnt = _pool_counts(r0)
            pin = pi_ref[pl.ds(r0, CT), :]
            pooled = [(sums[g] / cnt[g] - pin[:, g * GD:(g + 1) * GD]).astype(bf16) for g in range(4)]
            m0 = jnp.concatenate(
                [jnp.dot(pooled[g], pw_ref[g], preferred_element_type=f32) for g in range(4)], axis=1) + vec_ref[0:1, :]
            scale = vec_ref[1:2, :]
            pg = pg_ref[pl.ds(r0, CT), :]
            spg = _sig(pg)
            dpl_ = dpl_ref[pl.ds(r0, CT), :]
            dmixed = dpl_ * (pg * spg)
            dz_ref[1, pl.ds(r0, CT), :] = (dpl_ * (m0 * scale) * _dsilu(pg, spg)).astype(bf16)
            dvec_ref[1:2, :] += _colsum(dmixed * m0)
            dm0 = dmixed * scale
            dvec_ref[0:1, :] += _colsum(dm0)
            dps, es = [], []
            for g in range(4):
                dm0g = dm0[:, g * GD:(g + 1) * GD].astype(bf16)
                dpw_ref[g] += lax.dot_general(pooled[g], dm0g, (((0,), (0,)), ((), ())), preferred_element_type=f32)
                dpg = lax.dot_general(dm0g, pw_ref[g], (((1,), (1,)), ((), ())), preferred_element_type=f32)
                dps.append(dpg)
                es.append(dpg / cnt[g])
            dp_ref[pl.ds(r0, CT), :] = jnp.concatenate(dps, axis=1)
            e_ref[pl.ds(r0, CT), :] = jnp.concatenate(es, axis=1)
            return carry

        lax.fori_loop(0, nct, step, 0)

        def back(c, carry):
            r0 = pl.multiple_of(c * CT, CT)
            fs = _pool_sums(e_ref[pl.ds(r0, CT + PHALO), :], True)
            dz_ref[0, pl.ds(r0, CT), :] = (jnp.concatenate(fs, axis=1) - dp_ref[pl.ds(r0, CT), :]).astype(bf16)
            return carry

        lax.fori_loop(0, nct, back, 0)

    def zs(col):
        return pl.BlockSpec((SEQ, WC), lambda b: (b, col // WC))

    return pl.pallas_call(
        body, grid=(BL,),
        in_specs=[zs(COL_PI), zs(COL_PG), pl.BlockSpec((SEQ, WC), lambda b: (b, 0)), ANY,
                  pl.BlockSpec((4, GD, GD), lambda b: (0, 0, 0)), pl.BlockSpec((8, WC), lambda b: (0, 0))],
        out_specs=[pl.BlockSpec((2, SEQ, WC), lambda b: (DZ_POOL // 2, b, 0)),
                   pl.BlockSpec((4, GD, GD), lambda b: (0, 0, 0)), pl.BlockSpec((8, WC), lambda b: (0, 0))],
        out_shape=[jax.ShapeDtypeStruct(dz.shape, bf16), jax.ShapeDtypeStruct((4, GD, GD), f32),
                   jax.ShapeDtypeStruct((8, WC), f32)],
        scratch_shapes=[pltpu.VMEM((SEQ + PHALO, WC), f32), pltpu.VMEM((SEQ + PHALO, WC), f32),
                        pltpu.VMEM((SEQ, WC), f32)],
        input_output_aliases={3: 0}, name="pool_bwd",
        compiler_params=_params(("arbitrary",)))(z, z, dpl, dz, pw, pvec)


def _attn_prologue(q_ref, k_ref, v_ref, qs0, qs1, kp, vp, SEQ):
    head0 = lax.broadcasted_iota(jnp.int32, (1, 2 * HEAD_DIM), 1) < HEAD_DIM
    kp[pl.ds(0, KEY_PAD), :] = jnp.zeros((KEY_PAD, 2 * HEAD_DIM), bf16)
    vp[pl.ds(0, KEY_PAD), :] = jnp.zeros((KEY_PAD, 2 * HEAD_DIM), bf16)

    def fill(g, carry):
        r0 = pl.multiple_of(g * QG, QG)
        q = q_ref[pl.ds(r0, QG), :] * (HEAD_DIM ** -0.5)
        qs0[pl.ds(r0, QG), :] = jnp.where(head0, q, 0.0).astype(bf16)
        qs1[pl.ds(r0, QG), :] = jnp.where(head0, 0.0, q).astype(bf16)
        kp[pl.ds(r0 + KEY_PAD, QG), :] = k_ref[pl.ds(r0, QG), :].astype(bf16)
        vp[pl.ds(r0 + KEY_PAD, QG), :] = v_ref[pl.ds(r0, QG), :].astype(bf16)
        return carry

    lax.fori_loop(0, SEQ // QG, fill, 0)
    return head0


def _attn_weights(qh, kw, bias):
    s = lax.dot_general(qh, kw, (((1,), (1,)), ((), ())), preferred_element_type=f32) + bias
    e = jnp.exp(s - jnp.max(s, axis=-1, keepdims=True))
    return e, 1.0 / jnp.sum(e, axis=-1, keepdims=True)


def _attn_fwd(z, bm, BL, SEQ):
    T = BL * SEQ
    W2 = 2 * HEAD_DIM

    def body(q_ref, k_ref, v_ref, ag_ref, bm_ref, at_ref, o_ref, e_ref, qs0, qs1, kp, vp):
        head0 = _attn_prologue(q_ref, k_ref, v_ref, qs0, qs1, kp, vp, SEQ)

        def group(g, carry):
            r0 = pl.multiple_of(g * QG, QG)
            kw = kp[pl.ds(r0, KW), :]
            vw = vp[pl.ds(r0, KW), :]
            variant = jnp.minimum(g, BIAS_VARIANTS - 1)
            outs = []
            for hh, qs in enumerate((qs0, qs1)):
                e, inv = _attn_weights(qs[pl.ds(r0, QG), :], kw, bm_ref[variant, hh])
                eb = e.astype(bf16)
                e_ref[pl.ds(r0, QG), hh * KW:(hh + 1) * KW] = eb
                outs.append(jnp.dot(eb, vw, preferred_element_type=f32) * inv)
            o = jnp.where(head0, outs[0], outs[1])
            o_ref[pl.ds(r0, QG), :] = o
            ag = ag_ref[pl.ds(r0, QG), :]
            at_ref[pl.ds(r0, QG), :] = (o * (ag * _sig(ag))).astype(bf16)
            return carry

        lax.fori_loop(0, SEQ // QG, group, 0, unroll=8)

    def zs(col):
        return pl.BlockSpec((SEQ, W2), lambda b, hp: (b, col // W2 + hp))

    pair = pl.BlockSpec((SEQ, W2), lambda b, hp: (b, hp))
    return pl.pallas_call(
        body, grid=(BL, WC // W2),
        in_specs=[zs(COL_Q), zs(COL_K), zs(COL_V), zs(COL_AG),
                  pl.BlockSpec((BIAS_VARIANTS, 2, QG, KW), lambda b, hp: (0, hp, 0, 0))],
        out_specs=[pair, pair, pl.BlockSpec((None, SEQ, 2 * KW), lambda b, hp: (hp, b, 0))],
        out_shape=[jax.ShapeDtypeStruct((T, WC), bf16), jax.ShapeDtypeStruct((T, WC), f32),
                   jax.ShapeDtypeStruct((WC // W2, T, 2 * KW), bf16)],
        scratch_shapes=[pltpu.VMEM((SEQ, W2), bf16), pltpu.VMEM((SEQ, W2), bf16),
                        pltpu.VMEM((SEQ + KEY_PAD, W2), bf16), pltpu.VMEM((SEQ + KEY_PAD, W2), bf16)],
        name="attn_fwd", compiler_params=_params(("parallel", "parallel")))(z, z, z, z, bm)


def _attn_bwd(z, dat, o, ew, dz, BL, SEQ):
    W2 = 2 * HEAD_DIM

    def body(q_ref, k_ref, v_ref, ag_ref, dat_ref, o_ref, e_ref, dzin_ref, dz_ref, dbm_ref, qs0, qs1, kp, vp, dka, dva):
        @pl.when(pl.program_id(1) == 0)
        def _():
            dbm_ref[...] = jnp.zeros_like(dbm_ref)

        head0 = _attn_prologue(q_ref, k_ref, v_ref, qs0, qs1, kp, vp, SEQ)
        dka[...] = jnp.zeros_like(dka)
        dva[...] = jnp.zeros_like(dva)

        def group(g, carry):
            r0 = pl.multiple_of(g * QG, QG)
            kw = kp[pl.ds(r0, KW), :]
            vw = vp[pl.ds(r0, KW), :]
            ag = ag_ref[pl.ds(r0, QG), :]
            do = dat_ref[pl.ds(r0, QG), :] * (ag * _sig(ag))
            dqs = []
            for hh, qs in enumerate((qs0, qs1)):
                qh = qs[pl.ds(r0, QG), :]
                eb = e_ref[pl.ds(r0, QG), hh * KW:(hh + 1) * KW]
                e = eb.astype(f32)
                inv = 1.0 / jnp.sum(e, axis=-1, keepdims=True)
                doh = (jnp.where(head0, do, 0.0) if hh == 0 else jnp.where(head0, 0.0, do)) * inv
                doh = doh.astype(bf16)
                dp = lax.dot_general(doh, vw, (((1,), (1,)), ((), ())), preferred_element_type=f32)
                ds_ = e * (dp - jnp.sum(e * dp, axis=-1, keepdims=True) * inv)
                dbm_ref[hh] += ds_
                dsb = ds_.astype(bf16)
                dqs.append(jnp.dot(dsb, kw, preferred_element_type=f32))
                dka[pl.ds(r0, KW), :] += lax.dot_general(dsb, qh, (((0,), (0,)), ((), ())), preferred_element_type=f32)
                dva[pl.ds(r0, KW), :] += lax.dot_general(eb, doh, (((0,), (0,)), ((), ())), preferred_element_type=f32)
            dq = jnp.where(head0, dqs[0], dqs[1]) * (HEAD_DIM ** -0.5)
            dz_ref[0, pl.ds(r0, QG), :] = dq.astype(bf16)
            dz_ref[3, pl.ds(r0, QG), :] = (dat_ref[pl.ds(r0, QG), :] * o_ref[pl.ds(r0, QG), :]
                                           * _dsilu(ag, _sig(ag))).astype(bf16)
            return carry

        lax.fori_loop(0, SEQ // QG, group, 0, unroll=8)

        def flush(g, carry):
            r0 = pl.multiple_of(g * QG, QG)
            dz_ref[1, pl.ds(r0, QG), :] = dka[pl.ds(r0 + KEY_PAD, QG), :].astype(bf16)
            dz_ref[2, pl.ds(r0, QG), :] = dva[pl.ds(r0 + KEY_PAD, QG), :].astype(bf16)
            return carry

        lax.fori_loop(0, SEQ // QG, flush, 0)

    def zs(col):
        return pl.BlockSpec((SEQ, W2), lambda hp, b: (b, col // W2 + hp))

    pair = pl.BlockSpec((SEQ, W2), lambda hp, b: (b, hp))
    return pl.pallas_call(
        body, grid=(WC // W2, BL),
        in_specs=[zs(COL_Q), zs(COL_K), zs(COL_V), zs(COL_AG), pair, pair,
                  pl.BlockSpec((None, SEQ, 2 * KW), lambda hp, b: (hp, b, 0)), ANY],
        out_specs=[pl.BlockSpec((4, SEQ, W2), lambda hp, b: (DZ_ATTN // 4, b, hp)),
                   pl.BlockSpec((2, QG, KW), lambda hp, b: (hp, 0, 0))],
        out_shape=[jax.ShapeDtypeStruct(dz.shape, bf16), jax.ShapeDtypeStruct((8, QG, KW), f32)],
        scratch_shapes=[pltpu.VMEM((SEQ, W2), bf16), pltpu.VMEM((SEQ, W2), bf16),
                        pltpu.VMEM((SEQ + KEY_PAD, W2), bf16), pltpu.VMEM((SEQ + KEY_PAD, W2), bf16),
                        pltpu.VMEM((SEQ + KEY_PAD, W2), f32), pltpu.VMEM((SEQ + KEY_PAD, W2), f32)],
        input_output_aliases={7: 0}, name="attn_bwd",
        compiler_params=_params(("parallel", "arbitrary")))(z, z, z, z, dat, o, ew, dz)


BIAS_TOP = KEY_PAD + MAX_REL + QG - 1


def _bias_matrix(table):
    n = 2 * MAX_REL
    wd = QG + KW
    e = jnp.concatenate([jnp.broadcast_to(table[:, n:], (8, BIAS_TOP - n + 1)), table[:, n - 1:BIAS_TOP - wd + 1:-1],
                         jnp.zeros((8, 1), f32)], axis=1)
    flat = jnp.broadcast_to(e[:, None, :], (8, QG, wd)).reshape(8, QG * wd)
    skew = flat[:, :QG * (wd - 1)].reshape(8, QG, wd - 1)
    vals = skew[:, :, QG - 1:QG - 1 + KW]
    r = np.arange(QG)[:, None] // CHUNK
    j = np.arange(KW)[None, :]
    band = (j // CHUNK >= r) & (j // CHUNK <= r + LEFT_CHUNKS)
    keep = np.stack([band & (j >= KEY_PAD - v * QG) for v in range(BIAS_VARIANTS)])
    return jnp.where(jnp.asarray(keep)[:, None], vals[None], NEG_INF)


def _bias_fold(dbm):
    wd = QG + KW
    placed = jnp.pad(dbm, ((0, 0), (0, 0), (QG - 1, 0))).reshape(8, QG * (wd - 1))
    return jnp.pad(placed, ((0, 0), (0, QG))).reshape(8, QG, wd)


def _bias_colsum(folded):
    width = folded.shape[2]

    def body(x_ref, o_ref):
        for h in range(8):
            o_ref[pl.ds(h, 1), :] = _colsum(x_ref[h])

    return pl.pallas_call(body, out_shape=jax.ShapeDtypeStruct((8, width), f32), name="bias_colsum",
                          compiler_params=_params())(folded)


def _bias_table_grad(colsum):
    n = 2 * MAX_REL
    wd = QG + KW
    clipped = jnp.sum(colsum[:, :BIAS_TOP - n + 1], axis=1, keepdims=True)
    return jnp.concatenate([jnp.zeros((8, BIAS_TOP - wd + 2), f32), colsum[:, wd - 2:BIAS_TOP - n:-1], clipped], axis=1)


GATE_SPAN = 3 * WC
TAIL_ROWS = 3 * WC + D


def _gate_specs(tm):
    return [pl.BlockSpec((tm, GATE_SPAN), lambda i: (i, COL_GM // GATE_SPAN)),
            pl.BlockSpec((tm, GATE_SPAN), lambda i: (i, COL_GM // GATE_SPAN + 1))]


def _gate_block(ga_ref, gb_ref, branch, half):
    k = 2 * branch + half
    ref, k = (ga_ref, k) if k < 3 else (gb_ref, k - 3)
    return _sig(ref[:, k * WC:(k + 1) * WC])


def _resident(shape):
    return pl.BlockSpec(shape, lambda i: (0,) * len(shape), pipeline_mode=pl.Buffered(1))


def _tail_fwd(z, acts, x2, lw, next_g=None, tgt=None):
    T = z.shape[0]
    tm = 256
    with_loss = tgt is not None
    assert with_loss != (next_g is not None)

    def body(cv_ref, at_ref, pv_ref, ga_ref, gb_ref, x_ref, wc_ref, wa_ref, wp_ref, wo_ref, g_ref, *rest):
        out_ref, merged_ref, y_ref = rest[1:4]
        ys = [jnp.dot(a[...], w[...], preferred_element_type=f32)
              for a, w in ((cv_ref, wc_ref), (at_ref, wa_ref), (pv_ref, wp_ref))]
        halves = []
        for half in range(2):
            cols = slice(half * WC, (half + 1) * WC)
            halves.append(sum(_gate_block(ga_ref, gb_ref, br, half) * ys[br][:, cols] for br in range(3)))
        merged = jnp.concatenate(halves, axis=1).astype(bf16)
        merged_ref[...] = merged
        y = jnp.dot(merged, wo_ref[...], preferred_element_type=f32)
        y_ref[...] = y
        r = lax.rsqrt(jnp.mean(y * y, axis=-1, keepdims=True) + EPS)
        out = x_ref[...] + (y * r) * g_ref[...]
        if with_loss:
            sq_ref = rest[4]
            e = out - rest[0][...]
            out_ref[...] = e / float(D)

            @pl.when(pl.program_id(0) == 0)
            def _():
                sq_ref[...] = jnp.zeros_like(sq_ref)

            sq_ref[...] += _colsum(e * e)
        else:
            out_ref[...] = out
            rn = lax.rsqrt(jnp.mean(out * out, axis=-1, keepdims=True) + EPS)
            h = (out * rn) * rest[0][...]
            rest[4][...] = h.astype(bf16)
            rest[5][...] = h.T.astype(bf16)

    act = pl.BlockSpec((tm, WC), lambda i: (i, 0))
    row = pl.BlockSpec((tm, D), lambda i: (i, 0))
    vec = pl.BlockSpec((1, D), lambda i: (0, 0))
    if with_loss:
        last_in, last_specs, last_shapes = tgt, [row, [vec]], [jax.ShapeDtypeStruct((1, D), f32)]
    else:
        last_in, last_specs = next_g, [vec, [row, pl.BlockSpec((D, tm), lambda i: (0, i))]]
        last_shapes = [jax.ShapeDtypeStruct((T, D), bf16), jax.ShapeDtypeStruct((D, T), bf16)]
    return pl.pallas_call(
        body, grid=(T // tm,),
        in_specs=[act, act, act] + _gate_specs(tm) + [row, _resident((WC, D)), _resident((WC, D)), _resident((WC, D)),
                                                      _resident((D, D)), _resident((1, D)), last_specs[0]],
        out_specs=[row, row, row] + last_specs[1],
        out_shape=[jax.ShapeDtypeStruct((T, D), f32), jax.ShapeDtypeStruct((T, D), bf16), jax.ShapeDtypeStruct((T, D), f32)]
        + last_shapes,
        name="tail_fwd", compiler_params=_params(("arbitrary",)))(
            *acts, z, z, x2, lw["w_conv_out"], lw["w_attn_out"], lw["w_pool_out"], lw["w_out"], lw["post_g"], last_in)


def _tail_bwd(z, dout, y, merged, acts, lw):
    T = z.shape[0]
    tm = 256
    nt = (((1,), (1,)), ((), ()))
    tn = (((0,), (0,)), ((), ()))

    def body(d_ref, y_ref, m_ref, cv_ref, at_ref, pv_ref, ga_ref, gb_ref, wc_ref, wa_ref, wp_ref, wo_ref, g_ref,
             dz_ref, dcv_ref, dat_ref, dpv_ref, dg_ref, dw_ref):
        @pl.when(pl.program_id(0) == 0)
        def _():
            dg_ref[...] = jnp.zeros_like(dg_ref)
            dw_ref[...] = jnp.zeros_like(dw_ref)

        y = y_ref[...]
        d = d_ref[...]
        r = lax.rsqrt(jnp.mean(y * y, axis=-1, keepdims=True) + EPS)
        yn = y * r
        dyn = d * g_ref[...]
        dy = (r * (dyn - yn * jnp.mean(dyn * yn, axis=-1, keepdims=True))).astype(bf16)
        dg_ref[...] += _colsum(d * yn)
        dw_ref[pl.ds(3 * WC, D), :] += lax.dot_general(dy, m_ref[...], tn, preferred_element_type=f32)
        dmerged = lax.dot_general(dy, wo_ref[...], nt, preferred_element_type=f32)
        for br, (a_ref, w_ref, da_ref) in enumerate(((cv_ref, wc_ref, dcv_ref), (at_ref, wa_ref, dat_ref),
                                                     (pv_ref, wp_ref, dpv_ref))):
            yb = jnp.dot(a_ref[...], w_ref[...], preferred_element_type=f32)
            halves = []
            for half in range(2):
                cols = slice(half * WC, (half + 1) * WC)
                s = _gate_block(ga_ref, gb_ref, br, half)
                dm = dmerged[:, cols]
                halves.append((dm * s).astype(bf16))
                dz_ref[2 * br + half] = (dm * yb[:, cols] * s * (1.0 - s)).astype(bf16)
            dyb = jnp.concatenate(halves, axis=1)
            da_ref[...] = lax.dot_general(dyb, w_ref[...], nt, preferred_element_type=f32)
            dw_ref[pl.ds(br * WC, WC), :] += lax.dot_general(a_ref[...], dyb, tn, preferred_element_type=f32)

    act = pl.BlockSpec((tm, WC), lambda i: (i, 0))
    row = pl.BlockSpec((tm, D), lambda i: (i, 0))

    def whole(shape):
        return pl.BlockSpec(shape, lambda i: (0, 0))

    return pl.pallas_call(
        body, grid=(T // tm,),
        in_specs=[row, row, row, act, act, act] + _gate_specs(tm) + [_resident((WC, D)), _resident((WC, D)), _resident((WC, D)),
                                                                     _resident((D, D)), _resident((1, D))],
        out_specs=[pl.BlockSpec((6, tm, WC), lambda i: (DZ_GM // 6, i, 0)), act, act, act, whole((1, D)),
                   whole((TAIL_ROWS, D))],
        out_shape=[jax.ShapeDtypeStruct((DZ_BLOCKS, T, WC), bf16)] + [jax.ShapeDtypeStruct((T, WC), f32)] * 3
        + [jax.ShapeDtypeStruct((1, D), f32), jax.ShapeDtypeStruct((TAIL_ROWS, D), f32)],
        name="tail_bwd", compiler_params=_params(("arbitrary",)))(
            dout, y, merged, *acts, z, z, lw["w_conv_out"], lw["w_attn_out"], lw["w_pool_out"], lw["w_out"], lw["post_g"])


def _adamw(name, g, w, m, v):
    R, C = w.shape
    tr = R
    for cand in (512, 256, 248, 128, 64, 32, 16, 8):
        if R % cand == 0 and cand * C * 4 <= 2 * 1024 * 1024:
            tr = cand
            break
    c1 = 1.0 - ADAM_B1
    c2 = 1.0 - ADAM_B2
    bc1 = 1.0 - ADAM_B1 ** ADAM_STEP
    bc2 = 1.0 - ADAM_B2 ** ADAM_STEP

    def body(g_ref, w_ref, m_ref, v_ref, d_ref, nm_ref, nv_ref):
        g_ = g_ref[...]
        nm = ADAM_B1 * m_ref[...] + c1 * g_
        nv = ADAM_B2 * v_ref[...] + c2 * (g_ * g_)
        nm_ref[...] = nm
        nv_ref[...] = nv
        d_ref[...] = -ADAM_LR * ((nm / bc1) / (jnp.sqrt(nv / bc2) + ADAM_EPS) + ADAM_WD * w_ref[...])

    spec = pl.BlockSpec((tr, C), lambda i: (i, 0))
    return pl.pallas_call(
        body, grid=(R // tr,), in_specs=[spec] * 4, out_specs=[spec] * 3,
        out_shape=[jax.ShapeDtypeStruct((R, C), f32)] * 3, name=name,
        compiler_params=_params(("parallel",)))(g, w, m, v)


def _sum_slots(name, parts):
    _, R, C = parts.shape
    tr = R
    for cand in (256, 128, 64, 32, 16, 8):
        if R % cand == 0 and cand * C * 4 * N_DEV <= 8 * 1024 * 1024:
            tr = cand
            break

    def body(p_ref, o_ref):
        acc = p_ref[0].astype(f32)
        for s in range(1, N_DEV):
            acc = acc + p_ref[s].astype(f32)
        o_ref[...] = acc

    return pl.pallas_call(
        body, grid=(R // tr,), in_specs=[pl.BlockSpec((N_DEV, tr, C), lambda i: (0, i, 0))],
        out_specs=pl.BlockSpec((tr, C), lambda i: (i, 0)), out_shape=jax.ShapeDtypeStruct((R, C), f32),
        name=name, compiler_params=_params(("parallel",)))(parts)


def _row_tile(rows, row_bytes, budget):
    for cand in (512, 256, 128, 64, 32, 16):
        if rows % cand == 0 and cand * row_bytes <= budget:
            return cand
    return rows


def _pair_sum(core, g, theirs):
    R2, C4 = theirs.shape
    tr = _row_tile(R2, C4 * 2, 2 * 1024 * 1024)
    nb = R2 // tr

    def body(core_ref, g_ref, t_ref, o_ref):
        o_ref[...] = (g_ref[...].astype(f32) + t_ref[...].astype(f32)).astype(bf16)

    return pl.pallas_call(
        body,
        grid_spec=pltpu.PrefetchScalarGridSpec(
            num_scalar_prefetch=1, grid=(nb,),
            in_specs=[pl.BlockSpec((tr, C4), lambda i, core_ref: (core_ref[0] * nb + i, 0)),
                      pl.BlockSpec((tr, C4), lambda i, core_ref: (i, 0))],
            out_specs=pl.BlockSpec((tr, C4), lambda i, core_ref: (i, 0))),
        out_shape=jax.ShapeDtypeStruct((R2, C4), bf16), name="pair_sum",
        compiler_params=_params(("parallel",)))(core, g, theirs)


def _chip_sum(chip, mine, others):
    _, R2, C = others.shape
    tr = _row_tile(R2, C * 4, 1024 * 1024)

    def body(chip_ref, m_ref, o_ref, out_ref):
        acc = m_ref[...].astype(f32)
        for s in range(N_CHIPS - 1):
            acc = acc + o_ref[s].astype(f32)
        out_ref[...] = acc

    return pl.pallas_call(
        body,
        grid_spec=pltpu.PrefetchScalarGridSpec(
            num_scalar_prefetch=1, grid=(R2 // tr,),
            in_specs=[pl.BlockSpec((tr, C), lambda i, chip_ref: (i, chip_ref[0])),
                      pl.BlockSpec((N_CHIPS - 1, tr, C), lambda i, chip_ref: (0, i, 0))],
            out_specs=pl.BlockSpec((tr, C), lambda i, chip_ref: (i, 0))),
        out_shape=jax.ShapeDtypeStruct((R2, C), f32), name="chip_sum",
        compiler_params=_params(("parallel",)))(chip, mine, others)


def _place():
    x, y, c = lax.axis_index("x"), lax.axis_index("y"), lax.axis_index("c")
    return x, y, c


def _flip(v, bit):
    return 1 - v if bit else v


CHIP_FLIPS = ((1, 0), (0, 1), (1, 1))


class _Sems:
    def __init__(self, send, recv):
        self.send, self.recv = send, recv
        self.pairs = 0

    def pair(self):
        k = self.pairs
        self.pairs += 1
        return self.send.at[k], self.recv.at[k]


def _remote(src, dst, lands, sems, to):
    s, r = sems.pair()
    copy = pltpu.make_async_remote_copy(src_ref=src, dst_ref=dst, send_sem=s, recv_sem=r, device_id=to, device_id_type=MESH)
    wait = pltpu.make_async_remote_copy(src_ref=lands, dst_ref=lands, send_sem=s, recv_sem=r, device_id=to, device_id_type=MESH)
    return copy, wait


def _exchange(name, build, srcs, lands, n_remote):
    n_s, n_l = len(srcs), len(lands)

    def body(*refs):
        send, recv = refs[n_s + 2 * n_l:]
        remotes, recvs = build(refs[:n_s], refs[n_s + n_l:n_s + 2 * n_l], _Sems(send, recv))
        for cp in remotes:
            cp.start()
        for rv in recvs:
            rv.wait_recv()
        for cp in remotes:
            cp.wait_send()

    return pl.pallas_call(
        body, in_specs=[ANY] * (n_s + n_l), out_specs=[ANY] * n_l,
        out_shape=[jax.ShapeDtypeStruct(t.shape, t.dtype) for t in lands],
        scratch_shapes=[pltpu.SemaphoreType.DMA((n_remote,)), pltpu.SemaphoreType.DMA((n_remote,))],
        input_output_aliases={n_s + i: i for i in range(n_l)}, name=name)(*srcs, *lands)


HBM = pl.BlockSpec(memory_space=pltpu.HBM)
SEMS = pl.BlockSpec(memory_space=pltpu.SEMAPHORE)
DATAFLOW = pltpu.SideEffectType.DATAFLOW_SIDE_EFFECTING


def _start(name, build, srcs, lands, n_remote, after):
    n_s, n_l = len(srcs), len(lands)

    def body(*refs):
        send, recv = refs[n_s + n_l + 1], refs[n_s + n_l + 2]
        remotes, _ = build(refs[:n_s], refs[n_s:n_s + n_l], _Sems(send, recv))
        for cp in remotes:
            cp.start()
        refs[-1][...] = jnp.zeros((8, 128), f32)

    arrays = [pltpu.with_memory_space_constraint(a, pltpu.HBM) for a in (*srcs, *lands)]
    out = pl.pallas_call(
        body, name=name, in_specs=[HBM] * (n_s + n_l) + [ANY],
        out_specs=(SEMS, SEMS, *[HBM] * (n_s + n_l), pl.BlockSpec(memory_space=pltpu.VMEM)),
        out_shape=(pltpu.SemaphoreType.DMA((n_remote,)), pltpu.SemaphoreType.DMA((n_remote,)),
                   *[pltpu.HBM(a.shape, a.dtype) for a in arrays], jax.ShapeDtypeStruct((8, 128), f32)),
        input_output_aliases={i: 2 + i for i in range(n_s + n_l)},
        compiler_params=pltpu.CompilerParams(has_side_effects=DATAFLOW))(*arrays, after)
    return dict(name=name, build=build, sems=out[:2], srcs=out[2:2 + n_s], lands=out[2 + n_s:2 + n_s + n_l], token=out[-1])


def _wait(started, after):
    srcs, lands, build = started["srcs"], started["lands"], started["build"]
    n_s, n_l = len(srcs), len(lands)
    after = list(after) if isinstance(after, (list, tuple)) else [after]

    def body(*refs):
        send, recv = refs[n_s + n_l], refs[n_s + n_l + 1]
        remotes, recvs = build(refs[:n_s], refs[n_s:n_s + n_l], _Sems(send, recv))
        for rv in recvs:
            rv.wait_recv()
        for cp in remotes:
            cp.wait_send()

    out = pl.pallas_call(
        body, name=started["name"] + "_wait", in_specs=[HBM] * (n_s + n_l) + [SEMS, SEMS] + [ANY] * len(after),
        out_specs=[HBM] * (n_s + n_l), out_shape=[pltpu.HBM(a.shape, a.dtype) for a in (*srcs, *lands)],
        input_output_aliases={i: i for i in range(n_s + n_l)},
        compiler_params=pltpu.CompilerParams(has_side_effects=DATAFLOW))(*srcs, *lands, *started["sems"], *after)
    return out[:n_s], out[n_s:]


def _gather_plans(n_split, n_all):
    def over_ici(src, land, sems):
        x, y, c = _place()
        chip = 2 * x + y
        remotes, recvs = [], []
        for a in range(n_all):
            for fx, fy in CHIP_FLIPS:
                px, py = _flip(x, fx), _flip(y, fy)
                if a < n_split:
                    r2 = src[a].shape[0] // 2
                    rows = pl.ds(c * r2, r2)
                    cp, rv = _remote(src[a].at[rows], land[a].at[chip, rows], land[a].at[2 * px + py, rows], sems, (px, py, c))
                else:
                    cp, rv = _remote(src[a], land[a].at[chip], land[a].at[2 * px + py], sems, (px, py, c))
                remotes.append(cp)
                recvs.append(rv)
        return remotes, recvs

    def over_d2d(src, land, sems):
        x, y, c = _place()
        remotes, recvs = [], []
        for a in range(n_split):
            r2 = land[a].shape[1] // 2
            for fx, fy in CHIP_FLIPS:
                owner = 2 * _flip(x, fx) + _flip(y, fy)
                mine = land[a].at[owner, pl.ds(c * r2, r2)]
                cp, rv = _remote(mine, mine, land[a].at[owner, pl.ds((1 - c) * r2, r2)], sems, (x, y, 1 - c))
                remotes.append(cp)
                recvs.append(rv)
        return remotes, recvs

    return over_ici, over_d2d


def _gather_begin(tag, shards, n_split, after):
    over_ici, _ = _gather_plans(n_split, len(shards))
    lands = [lax.empty((N_CHIPS,) + s.shape, s.dtype) for s in shards]
    return _start("gather_ici_" + tag, over_ici, shards, lands, 3 * len(shards), after)


def _gather_end(started, shards, n_split, after):
    _, over_d2d = _gather_plans(n_split, len(shards))
    lands = _exchange("gather_d2d", over_d2d, [], _wait(started, after)[1], 3 * n_split)
    chip = 2 * lax.axis_index("x") + lax.axis_index("y")
    return [lax.dynamic_update_slice_in_dim(g, s[None], chip, axis=0) for g, s in zip(lands, shards)]


def _reduce_plans(n):
    def to_sibling(src, land, sems):
        x, y, c = _place()
        remotes, recvs = [], []
        for a in range(n):
            r2 = src[a].shape[0] // 2
            cp, rv = _remote(src[a].at[pl.ds((1 - c) * r2, r2), :], land[a], land[a], sems, (x, y, 1 - c))
            remotes.append(cp)
            recvs.append(rv)
        return remotes, recvs

    def across_chips(src, land, sems):
        x, y, c = _place()
        remotes, recvs = [], []
        for a in range(n):
            cw = src[a].shape[1] // N_CHIPS
            for k, (fx, fy) in enumerate(CHIP_FLIPS):
                px, py = _flip(x, fx), _flip(y, fy)
                cp, rv = _remote(src[a].at[:, pl.ds((2 * px + py) * cw, cw)], land[a].at[k], land[a].at[k], sems, (px, py, c))
                remotes.append(cp)
                recvs.append(rv)
        return remotes, recvs

    def share(src, land, sems):
        x, y, c = _place()
        remotes, recvs = [], []
        for a in range(n):
            cp, rv = _remote(src[a], land[a], land[a], sems, (x, y, 1 - c))
            remotes.append(cp)
            recvs.append(rv)
        return remotes, recvs

    return to_sibling, across_chips, share


def _reduce_begin(grads):
    n = len(grads)
    to_sibling, across_chips, _ = _reduce_plans(n)
    core = lax.axis_index("c").reshape(1).astype(jnp.int32)
    theirs = _exchange("reduce_pair", to_sibling, grads,
                       [lax.empty((g.shape[0] // 2, g.shape[1]), bf16) for g in grads], n)
    pair = [_pair_sum(core, g, t) for g, t in zip(grads, theirs)]
    lands = [lax.empty((N_CHIPS - 1, g.shape[0] // 2, g.shape[1] // N_CHIPS), bf16) for g in grads]
    return _start("reduce_chips", across_chips, pair, lands, 3 * n, pair[0])


def _reduce_end(started, after):
    x, y, c = _place()
    chip = (2 * x + y).reshape(1).astype(jnp.int32)
    pair, others = _wait(started, after)
    _, _, share = _reduce_plans(len(pair))
    mine = [_chip_sum(chip, p, o) for p, o in zip(pair, others)]
    sibs = _exchange("reduce_share", share, mine, [lax.empty(h.shape, f32) for h in mine], len(mine))
    return [jnp.where(c == 0, jnp.concatenate([h, s], axis=0), jnp.concatenate([s, h], axis=0))
            for h, s in zip(mine, sibs)]


def _to_all(src, land, sems):
    x, y, c = _place()
    me = 4 * x + 2 * y + c
    remotes, recvs = [], []
    for k in range(1, N_DEV):
        px, py, pc = _flip(x, (k >> 2) & 1), _flip(y, (k >> 1) & 1), _flip(c, k & 1)
        cp, rv = _remote(src[0], land[0].at[me], land[0].at[4 * px + 2 * py + pc], sems, (px, py, pc))
        remotes.append(cp)
        recvs.append(rv)
    return remotes, recvs


def _gather_small_begin(packed):
    return _start("gather_small", _to_all, [packed], [lax.empty((N_DEV,) + packed.shape, f32)], N_DEV - 1, packed)


def _gather_small_end(started, after):
    (packed,), (others,) = _wait(started, after)
    x, y, c = _place()
    return lax.dynamic_update_slice_in_dim(others, packed[None], 4 * x + 2 * y + c, axis=0)


def _gather_all(packed):
    others = _exchange("gather_all", _to_all, [packed], [lax.empty((N_DEV,) + packed.shape, f32)], N_DEV - 1)[0]
    x, y, c = _place()
    return lax.dynamic_update_slice_in_dim(others, packed[None], 4 * x + 2 * y + c, axis=0)


def _rows8(v):
    return jnp.pad(v[None, :], ((0, 7), (0, 0)))


def _vec_rows(vs):
    return jnp.pad(jnp.stack(vs), ((0, 8 - len(vs)), (0, 0)))


SMALL_ROWS = 224


def _pack_small(conv_vec, conv_dw, pool_vec, pool_w, pre_g, post_g, rel):
    return jnp.concatenate([
        conv_vec, conv_dw, pool_vec, pool_w.reshape(GD, WC),
        _rows8(pre_g).reshape(16, WC), _rows8(post_g).reshape(16, WC),
        jnp.pad(rel, ((0, 0), (0, D - rel.shape[1]))).reshape(16, WC)], axis=0)


def _unpack_small(p):
    conv_vec, pool_vec = p[0:8], p[40:48]
    return dict(
        conv_dw_b=conv_vec[0], conv_ln_g=conv_vec[1], conv_ln_b=conv_vec[2], conv_dw=p[8:8 + CONV_K],
        pool_b=pool_vec[0].reshape(4, GD), pool_scale=pool_vec[1], pool_w=p[48:176].reshape(4, GD, GD),
        pre_norm_g=p[176:192].reshape(8, D)[0], post_norm_g=p[192:208].reshape(8, D)[0],
        rel_bias=p[208:224].reshape(8, D)[:, :2 * MAX_REL + 1])


def _layer_fwd(x2, ht, z, lw, BL, SEQ, next_g=None, tgt=None):
    cv, u1 = _conv_fwd(z, lw["dw32"], lw["cvec"], BL, SEQ)
    at, attn_o, attn_e = _attn_fwd(z, lw["bm"], BL, SEQ)
    pv = _pool_fwd(z, lw["pw"], lw["pvec"], BL, SEQ)
    out, merged, y, *last = _tail_fwd(z, (cv, at, pv), x2, lw, next_g, tgt)
    saved = dict(x=x2, ht=ht, z=z, u1=u1, attn_o=attn_o, attn_e=attn_e, acts=(cv, at, pv), merged=merged, y=y)
    return (out, *last), saved


def _layer_bwd(dout, sv, lw, BL, SEQ, meanwhile=None):
    tail = _tail_bwd(sv["z"], dout, sv["y"], sv["merged"], sv["acts"], lw)
    dz, dacts, dpost = tail[0], tail[1:4], tail[4]
    dw_tail = tail[5].astype(bf16)
    if meanwhile is not None:
        meanwhile(dw_tail)
    dz, ddw, dcvec = _conv_bwd(sv["z"], sv["u1"], dacts[0], dz, lw["dw32"], lw["cvec"], BL, SEQ)
    dz, dbm = _attn_bwd(sv["z"], dacts[1], sv["attn_o"], sv["attn_e"], dz, BL, SEQ)
    dz, dpw, dpvec = _pool_bwd(sv["z"], dacts[2], dz, lw["pw"], lw["pvec"], BL, SEQ)
    drel = _bias_table_grad(_bias_colsum(_bias_fold(dbm)))
    small_gather = _gather_small_begin(_pack_small(dcvec, ddw, dpvec, dpw, jnp.zeros((D,), f32), dpost[0], drel))
    dw_in = _mm_dw_in(sv["ht"], dz, small_gather["token"])
    reduction = _reduce_begin([dw_in, dw_tail])
    dx, dpre = _mm_dx(dz, lw["w_in"], sv["x"], lw["pre_g"], dout, reduction["token"])
    return dx, reduction, small_gather, dpre


BIG = ("w_in", "w_conv_out", "w_attn_out", "w_pool_out", "w_out")
PRE_ROWS = slice(176, 192)


def _layer_shards(w, l):
    return [w[k][l].astype(bf16) for k in BIG] + [w["conv_dw"][l]]


def _side_by_side(g):
    return jnp.transpose(g, (1, 0, 2)).reshape(g.shape[1], N_CHIPS * g.shape[2])


def _layer_weights(w_in, gathered, w, l, bm):
    lw = {k: _side_by_side(g) for k, g in zip(BIG[1:4], gathered[:3])}
    lw["w_in"] = w_in
    lw["w_out"] = gathered[3].reshape(D, D)
    lw["pre_g"] = w["pre_norm_g"][l][None]
    lw["post_g"] = w["post_norm_g"][l][None]
    lw["dw32"] = jnp.pad(_side_by_side(gathered[4]), ((0, 32 - CONV_K), (0, 0)))
    lw["cvec"] = _vec_rows([w["conv_dw_b"][l], w["conv_ln_g"][l], w["conv_ln_b"][l]])
    lw["bm"] = bm
    lw["pw"] = w["pool_w"][l].astype(bf16)
    lw["pvec"] = _vec_rows([w["pool_b"][l].reshape(WC), w["pool_scale"][l]])
    return lw


SMALL = ("pre_norm_g", "post_norm_g", "conv_dw_b", "conv_ln_g", "conv_ln_b", "rel_bias", "pool_w", "pool_b", "pool_scale")
ORDER = ("pre_norm_g", "post_norm_g", "w_in", "conv_dw", "conv_dw_b", "conv_ln_g", "conv_ln_b", "w_conv_out",
         "rel_bias", "w_attn_out", "pool_w", "pool_b", "pool_scale", "w_pool_out", "w_out")


def _pack_small_params(p):
    return jnp.concatenate([
        _pack_small(_vec_rows([p["conv_dw_b"][l], p["conv_ln_g"][l], p["conv_ln_b"][l]]), jnp.zeros((32, WC), f32),
                    _vec_rows([p["pool_b"][l].reshape(WC), p["pool_scale"][l]]), p["pool_w"][l],
                    p["pre_norm_g"][l], p["post_norm_g"][l], p["rel_bias"][l])
        for l in range(DEPTH)], axis=0)


def _unpack_small_params(packed):
    layers = [_unpack_small(packed[l * SMALL_ROWS:(l + 1) * SMALL_ROWS]) for l in range(DEPTH)]
    return {k: jnp.stack([layers[l][k] for l in range(DEPTH)]) for k in layers[0]}


def kernel(x, pre_norm_g, post_norm_g, w_in, conv_dw, conv_dw_b, conv_ln_g, conv_ln_b, w_conv_out, rel_bias, w_attn_out, pool_w, pool_b, pool_scale, w_pool_out, w_out, loss_target, m_pre_norm_g, m_post_norm_g, m_w_in, m_conv_dw, m_conv_dw_b, m_conv_ln_g, m_conv_ln_b, m_w_conv_out, m_rel_bias, m_w_attn_out, m_pool_w, m_pool_b, m_pool_scale, m_w_pool_out, m_w_out, v_pre_norm_g, v_post_norm_g, v_w_in, v_conv_dw, v_conv_dw_b, v_conv_ln_g, v_conv_ln_b, v_w_conv_out, v_rel_bias, v_w_attn_out, v_pool_w, v_pool_b, v_pool_scale, v_w_pool_out, v_w_out):
    BL, SEQ, _ = x.shape
    T = BL * SEQ
    w = dict(pre_norm_g=pre_norm_g, post_norm_g=post_norm_g, w_in=w_in, conv_dw=conv_dw, conv_dw_b=conv_dw_b,
             conv_ln_g=conv_ln_g, conv_ln_b=conv_ln_b, w_conv_out=w_conv_out, rel_bias=rel_bias, w_attn_out=w_attn_out,
             pool_w=pool_w, pool_b=pool_b, pool_scale=pool_scale, w_pool_out=w_pool_out, w_out=w_out)
    m = dict(pre_norm_g=m_pre_norm_g, post_norm_g=m_post_norm_g, w_in=m_w_in, conv_dw=m_conv_dw, conv_dw_b=m_conv_dw_b,
             conv_ln_g=m_conv_ln_g, conv_ln_b=m_conv_ln_b, w_conv_out=m_w_conv_out, rel_bias=m_rel_bias,
             w_attn_out=m_w_attn_out, pool_w=m_pool_w, pool_b=m_pool_b, pool_scale=m_pool_scale,
             w_pool_out=m_w_pool_out, w_out=m_w_out)
    v = dict(pre_norm_g=v_pre_norm_g, post_norm_g=v_post_norm_g, w_in=v_w_in, conv_dw=v_conv_dw, conv_dw_b=v_conv_dw_b,
             conv_ln_g=v_conv_ln_g, conv_ln_b=v_conv_ln_b, w_conv_out=v_w_conv_out, rel_bias=v_rel_bias,
             w_attn_out=v_w_attn_out, pool_w=v_pool_w, pool_b=v_pool_b, pool_scale=v_pool_scale,
             w_pool_out=v_w_pool_out, w_out=v_w_out)

    shards = [_layer_shards(w, l) for l in range(DEPTH)]
    x2 = x.reshape(T, D)
    h0, ht0 = _rms_pre(x2, pre_norm_g[0][None])
    first = _gather_begin("w_in0", shards[0][:1], 1, x2)
    bms = [_bias_matrix(rel_bias[l]) for l in range(DEPTH)]
    packs = [_pack_small_params(p) for p in (w, m, v)]
    w_in0 = _side_by_side(_gather_end(first, shards[0][:1], 1, [ht0, *bms, *packs])[0])
    rest0 = _gather_begin("rest0", shards[0][1:], 4, w_in0)
    all1 = _gather_begin("layer1", shards[1], 5, rest0["token"])
    z0 = _in_proj(h0, w_in0, after=all1["token"])
    lw0 = _layer_weights(w_in0, _gather_end(rest0, shards[0][1:], 4, z0), w, 0, bms[0])
    (out0, h1, ht1), saved0 = _layer_fwd(x2, ht0, z0, lw0, BL, SEQ, next_g=pre_norm_g[1][None])
    gathered1 = _gather_end(all1, shards[1], 5, out0)
    lw1 = _layer_weights(_side_by_side(gathered1[0]), gathered1[1:], w, 1, bms[1])
    z1 = _in_proj(h1, lw1["w_in"], after=h1)
    (dout, sq), saved1 = _layer_fwd(out0, ht1, z1, lw1, BL, SEQ, tgt=loss_target.reshape(T, D))
    loss = lax.psum(0.5 * jnp.sum(sq) / float(D), ("x", "y", "c"))

    summed = [None] * DEPTH
    dx1, reduction1, small_gather1, dpre1 = _layer_bwd(dout, saved1, lw1, BL, SEQ)

    def finish_layer1(after):
        summed[1] = _reduce_end(reduction1, after)

    grad_x, reduction0, small_gather0, dpre0 = _layer_bwd(dx1, saved0, lw0, BL, SEQ, meanwhile=finish_layer1)
    summed[0] = _reduce_end(reduction0, grad_x)
    dpre = _sum_slots("sum_small", _gather_all(jnp.concatenate([_rows8(dpre0[0]), _rows8(dpre1[0])], axis=0)))
    gsmall = []
    for l, started in enumerate((small_gather0, small_gather1)):
        g = _sum_slots("sum_small", _gather_small_end(started, dpre))
        gsmall += [g[:PRE_ROWS.start], dpre[8 * l:8 * l + 8].reshape(16, WC), g[PRE_ROWS.stop:]]
    gsmall = jnp.concatenate(gsmall, axis=0)

    grads, deltas, new_m, new_v = {}, {}, {}, {}
    for i, k in enumerate(BIG):
        if k == "w_in":
            g = jnp.stack([summed[l][0] for l in range(DEPTH)])
        elif k == "w_out":
            g = jnp.stack([summed[l][1][3 * WC:].T for l in range(DEPTH)])
        else:
            g = jnp.stack([summed[l][1][(i - 1) * WC:i * WC] for l in range(DEPTH)])
        grads[k] = g
        shape = w[k].shape
        flat2 = lambda a: a.reshape(shape[0] * shape[1], shape[2])
        d_, nm_, nv_ = _adamw("adamw_big", flat2(g), flat2(w[k]), flat2(m[k]), flat2(v[k]))
        deltas[k], new_m[k], new_v[k] = d_.reshape(shape), nm_.reshape(shape), nv_.reshape(shape)

    d_, nm_, nv_ = _adamw("adamw_small", gsmall, *packs)
    gs, ds, ms, vs = (_unpack_small_params(a) for a in (gsmall, d_, nm_, nv_))
    for k in SMALL:
        grads[k], deltas[k], new_m[k], new_v[k] = gs[k], ds[k], ms[k], vs[k]
    chip = 2 * lax.axis_index("x") + lax.axis_index("y")
    g_dw = lax.dynamic_slice_in_dim(gs["conv_dw"], chip * GD, GD, axis=2)
    flat2 = lambda a: a.reshape(DEPTH * CONV_K, GD)
    d_, nm_, nv_ = _adamw("adamw_conv_dw", flat2(g_dw), flat2(conv_dw), flat2(m["conv_dw"]), flat2(v["conv_dw"]))
    grads["conv_dw"] = g_dw
    deltas["conv_dw"], new_m["conv_dw"], new_v["conv_dw"] = (a.reshape(conv_dw.shape) for a in (d_, nm_, nv_))

    return (loss, grad_x.reshape(x.shape), *[grads[k] for k in ORDER], *[deltas[k] for k in ORDER],
            *[new_m[k] for k in ORDER], *[new_v[k] for k in ORDER])
```

```python
import numpy as np
import jax
import jax.numpy as jnp
from jax import lax
from jax.experimental import pallas as pl
from jax.experimental.pallas import tpu as pltpu

f32 = jnp.float32
bf16 = jnp.bfloat16

D = 1024
DEPTH = 2
WC = 512
HEAD_DIM = 64
CHUNK = 64
LEFT_CHUNKS = 8
KEY_PAD = LEFT_CHUNKS * CHUNK
MAX_REL = 256
CONV_K = 31
POOL_WINDOWS = (2, 4, 8, 16)
GD = 128
NCOL = 7680
EPS = 1e-6
NEG_INF = -1e30
COL_A, COL_B, COL_CG, COL_Q, COL_K, COL_V, COL_AG, COL_PI, COL_PG, COL_GM = (
    0, 512, 1024, 1536, 2048, 2560, 3072, 3584, 4096, 4608)

ADAM_LR = 0.001
ADAM_B1 = 0.9
ADAM_B2 = 0.999
ADAM_EPS = 1e-08
ADAM_WD = 0.01
ADAM_STEP = 10

QG = 256
KW = KEY_PAD + QG
BIAS_VARIANTS = KEY_PAD // QG + 1
CT = 256
HALO = 32
PHALO = 16
N_CHIPS = 4
N_DEV = 8
VMEM_LIMIT = 58 * 1024 * 1024
MESH = pl.DeviceIdType.MESH
ANY = pl.BlockSpec(memory_space=pl.ANY)

DZ_BLOCKS = 18
DZ_CONV, DZ_ATTN, DZ_POOL, DZ_GM = 0, 4, 8, 12


def _dz_block(c):
    return c + (c >= 3).astype(jnp.int32) + 2 * (c >= 9).astype(jnp.int32)


def _params(sem=None):
    return pltpu.CompilerParams(dimension_semantics=sem, vmem_limit_bytes=VMEM_LIMIT)


def _sig(x):
    return 1.0 / (1.0 + jnp.exp(-x))


def _dsilu(x, s):
    return s * (1.0 + x * (1.0 - s))


def _colsum(x):
    return jnp.sum(x, axis=0, keepdims=True)


def _rms_pre(x2, g):
    T = x2.shape[0]
    tm = 512

    def body(x_ref, g_ref, h_ref, ht_ref):
        x = x_ref[...]
        r = lax.rsqrt(jnp.mean(x * x, axis=-1, keepdims=True) + EPS)
        h = (x * r) * g_ref[...]
        h_ref[...] = h.astype(bf16)
        ht_ref[...] = h.T.astype(bf16)

    row = pl.BlockSpec((tm, D), lambda i: (i, 0))
    vec = pl.BlockSpec((1, D), lambda i: (0, 0))
    return pl.pallas_call(
        body, grid=(T // tm,), in_specs=[row, vec], out_specs=[row, pl.BlockSpec((D, tm), lambda i: (0, i))],
        out_shape=[jax.ShapeDtypeStruct((T, D), bf16), jax.ShapeDtypeStruct((D, T), bf16)], name="rms_pre",
        compiler_params=_params(("parallel",)))(x2, g)


def _in_proj(h, w_in, after):
    T = h.shape[0]
    tm, tn = 512, 1536

    def body(h_ref, w_ref, after_ref, z_ref):
        for n0 in range(0, NCOL, tn):
            z_ref[:, n0:n0 + tn] = jnp.dot(h_ref[...], w_ref[:, n0:n0 + tn], preferred_element_type=f32)

    return pl.pallas_call(
        body, grid=(T // tm,),
        in_specs=[pl.BlockSpec((tm, D), lambda i: (i, 0)),
                  pl.BlockSpec((D, NCOL), lambda i: (0, 0), pipeline_mode=pl.Buffered(1)), ANY],
        out_specs=pl.BlockSpec((tm, NCOL), lambda i: (i, 0)), out_shape=jax.ShapeDtypeStruct((T, NCOL), f32),
        name="mm_in", compiler_params=_params(("parallel",)))(h, w_in, after)


DZ_SPANS = ((DZ_CONV, 3), (DZ_ATTN, 4), (DZ_POOL, 2), (DZ_GM, 6))


def _mm_dx(dz, w_in, x2, g, dout, after):
    T = dz.shape[1]
    tm = 512

    def body(conv_ref, attn_ref, pool_ref, gm_ref, w_ref, x_ref, g_ref, d_ref, after_ref, dx_ref, dg_ref):
        dh = None
        col = 0
        for ref, (_, blocks) in zip((conv_ref, attn_ref, pool_ref, gm_ref), DZ_SPANS):
            for b in range(blocks):
                p = lax.dot_general(ref[b], w_ref[:, col * WC:(col + 1) * WC], (((1,), (1,)), ((), ())),
                                    preferred_element_type=f32)
                dh = p if dh is None else dh + p
                col += 1
        x = x_ref[...]
        r = lax.rsqrt(jnp.mean(x * x, axis=-1, keepdims=True) + EPS)
        xn = x * r
        dxn = dh * g_ref[...]
        dx_ref[...] = r * (dxn - xn * jnp.mean(dxn * xn, axis=-1, keepdims=True)) + d_ref[...]

        @pl.when(pl.program_id(0) == 0)
        def _():
            dg_ref[...] = jnp.zeros_like(dg_ref)

        dg_ref[...] += _colsum(dh * xn)

    spans = [pl.BlockSpec((blocks, tm, WC), lambda i, first=first, blocks=blocks: (first // blocks, i, 0))
             for first, blocks in DZ_SPANS]
    row = pl.BlockSpec((tm, D), lambda i: (i, 0))
    vec = pl.BlockSpec((1, D), lambda i: (0, 0))
    return pl.pallas_call(
        body, grid=(T // tm,),
        in_specs=spans + [pl.BlockSpec((D, NCOL), lambda i: (0, 0), pipeline_mode=pl.Buffered(1)), row, vec, row, ANY],
        out_specs=[row, vec], out_shape=[jax.ShapeDtypeStruct((T, D), f32), jax.ShapeDtypeStruct((1, D), f32)],
        name="mm_dx", compiler_params=_params(("arbitrary",)))(dz, dz, dz, dz, w_in, x2, g, dout, after)


def _mm_dw_in(ht, dz, after):
    T = dz.shape[1]

    def body(ht_ref, dz_ref, after_ref, o_ref):
        o_ref[...] = jnp.dot(ht_ref[...], dz_ref[...], preferred_element_type=f32).astype(bf16)

    return pl.pallas_call(
        body, grid=(NCOL // WC,),
        in_specs=[pl.BlockSpec((D, T), lambda j: (0, 0), pipeline_mode=pl.Buffered(1)),
                  pl.BlockSpec((None, T, WC), lambda j: (_dz_block(j), 0, 0)), ANY],
        out_specs=pl.BlockSpec((D, WC), lambda j: (0, j)), out_shape=jax.ShapeDtypeStruct((D, NCOL), bf16),
        name="mm_dw_in", compiler_params=_params(("parallel",)))(ht, dz, after)


def _conv_delays():
    return [(8 * a + b, a, b) for b in range(8) for a in range(4) if 8 * a + b < CONV_K]


def _conv_rolls(win):
    return [win if b == 0 else pltpu.roll(win, b, axis=0) for b in range(8)]


def _conv_taps(rolled, dw_ref):
    acc = None
    for d, a, b in _conv_delays():
        term = rolled[b][HALO - 8 * a:HALO - 8 * a + CT, :] * dw_ref[pl.ds(CONV_K - 1 - d, 1), :]
        acc = term if acc is None else acc + term
    return acc


def _conv_fwd(z, dw32, cvec, BL, SEQ):
    T = BL * SEQ
    nct = SEQ // CT

    def body(a_ref, b_ref, cg_ref, dw_ref, vec_ref, o_ref, u1_ref, p_ref):
        p_ref[pl.ds(0, HALO), :] = jnp.zeros((HALO, WC), f32)

        def glu(c, carry):
            r0 = pl.multiple_of(c * CT, CT)
            p_ref[pl.ds(r0 + HALO, CT), :] = a_ref[pl.ds(r0, CT), :] * _sig(b_ref[pl.ds(r0, CT), :])
            return carry

        lax.fori_loop(0, nct, glu, 0)

        def step(c, carry):
            r0 = pl.multiple_of(c * CT, CT)
            u1 = _conv_taps(_conv_rolls(p_ref[pl.ds(r0, CT + HALO), :]), dw_ref) + vec_ref[0:1, :]
            u1_ref[pl.ds(r0, CT), :] = u1
            xc = u1 - jnp.mean(u1, axis=-1, keepdims=True)
            rs = lax.rsqrt(jnp.mean(xc * xc, axis=-1, keepdims=True) + EPS)
            u2 = (xc * rs) * vec_ref[1:2, :] + vec_ref[2:3, :]
            cg = cg_ref[pl.ds(r0, CT), :]
            o_ref[pl.ds(r0, CT), :] = ((u2 * _sig(u2)) * (cg * _sig(cg))).astype(bf16)
            return carry

        lax.fori_loop(0, nct, step, 0)

    def zs(col):
        return pl.BlockSpec((SEQ, WC), lambda b: (b, col // WC))

    seq = pl.BlockSpec((SEQ, WC), lambda b: (b, 0))
    return pl.pallas_call(
        body, grid=(BL,),
        in_specs=[zs(COL_A), zs(COL_B), zs(COL_CG), pl.BlockSpec((32, WC), lambda b: (0, 0)),
                  pl.BlockSpec((8, WC), lambda b: (0, 0))],
        out_specs=[seq, seq],
        out_shape=[jax.ShapeDtypeStruct((T, WC), bf16), jax.ShapeDtypeStruct((T, WC), f32)],
        scratch_shapes=[pltpu.VMEM((SEQ + HALO, WC), f32)], name="conv_fwd",
        compiler_params=_params(("parallel",)))(z, z, z, dw32, cvec)


def _conv_bwd(z, u1, dcv, dz, dw32, cvec, BL, SEQ):
    nct = SEQ // CT

    def body(a_ref, b_ref, cg_ref, u1_ref, dcv_ref, dzin_ref, dw_ref, vec_ref, dz_ref, ddw_ref, dvec_ref,
             p_ref, q_ref, taps_ref):
        @pl.when(pl.program_id(0) == 0)
        def _():
            ddw_ref[...] = jnp.zeros_like(ddw_ref)
            dvec_ref[...] = jnp.zeros_like(dvec_ref)

        p_ref[pl.ds(0, HALO), :] = jnp.zeros((HALO, WC), f32)
        q_ref[pl.ds(SEQ, HALO), :] = jnp.zeros((HALO, WC), f32)

        def glu(c, carry):
            r0 = pl.multiple_of(c * CT, CT)
            p_ref[pl.ds(r0 + HALO, CT), :] = a_ref[pl.ds(r0, CT), :] * _sig(b_ref[pl.ds(r0, CT), :])
            return carry

        lax.fori_loop(0, nct, glu, 0)

        def step(c, carry):
            r0 = pl.multiple_of(c * CT, CT)
            rolled = _conv_rolls(p_ref[pl.ds(r0, CT + HALO), :])
            u1 = u1_ref[pl.ds(r0, CT), :]
            xc = u1 - jnp.mean(u1, axis=-1, keepdims=True)
            rs = lax.rsqrt(jnp.mean(xc * xc, axis=-1, keepdims=True) + EPS)
            nrm = xc * rs
            u2 = nrm * vec_ref[1:2, :] + vec_ref[2:3, :]
            s2 = _sig(u2)
            u3 = u2 * s2
            cg = cg_ref[pl.ds(r0, CT), :]
            scg = _sig(cg)
            dcv_ = dcv_ref[pl.ds(r0, CT), :]
            dz_ref[2, pl.ds(r0, CT), :] = (dcv_ * u3 * _dsilu(cg, scg)).astype(bf16)
            du2 = dcv_ * (cg * scg) * _dsilu(u2, s2)
            dvec_ref[1:2, :] += _colsum(du2 * nrm)
            dvec_ref[2:3, :] += _colsum(du2)
            dn = du2 * vec_ref[1:2, :]
            du1 = rs * (dn - jnp.mean(dn, axis=-1, keepdims=True)
                        - nrm * jnp.mean(dn * nrm, axis=-1, keepdims=True))
            dvec_ref[0:1, :] += _colsum(du1)
            q_ref[pl.ds(r0, CT), :] = du1
            for d, a, b in _conv_delays():
                prod = du1 * rolled[b][HALO - 8 * a:HALO - 8 * a + CT, :]
                taps_ref[CONV_K - 1 - d] += jnp.sum(prod.reshape(CT // 8, 8, WC), axis=0)
            return carry

        taps_ref[...] = jnp.zeros_like(taps_ref)
        lax.fori_loop(0, nct, step, 0)
        for row in range(CONV_K):
            ddw_ref[pl.ds(row, 1), :] += _colsum(taps_ref[row])

        def back(c, carry):
            r0 = pl.multiple_of(c * CT, CT)
            wq = q_ref[pl.ds(r0, CT + HALO), :]
            up = {}
            acc = None
            for d, a, b in _conv_delays():
                if b not in up:
                    up[b] = wq if b == 0 else pltpu.roll(wq, CT + HALO - b, axis=0)
                term = up[b][8 * a:8 * a + CT, :] * dw_ref[pl.ds(CONV_K - 1 - d, 1), :]
                acc = term if acc is None else acc + term
            a_ = a_ref[pl.ds(r0, CT), :]
            sb = _sig(b_ref[pl.ds(r0, CT), :])
            dz_ref[0, pl.ds(r0, CT), :] = (acc * sb).astype(bf16)
            dz_ref[1, pl.ds(r0, CT), :] = (acc * a_ * sb * (1.0 - sb)).astype(bf16)
            return carry

        lax.fori_loop(0, nct, back, 0)

    def const(r):
        return pl.BlockSpec((r, WC), lambda b: (0, 0))

    first = [pl.BlockSpec((SEQ, WC), lambda b, col=col: (b, col // WC)) for col in (COL_A, COL_B)]
    seq = pl.BlockSpec((SEQ, WC), lambda b: (b, 0), pipeline_mode=pl.Buffered(1))
    return pl.pallas_call(
        body, grid=(BL,),
        in_specs=first + [pl.BlockSpec((SEQ, WC), lambda b: (b, COL_CG // WC), pipeline_mode=pl.Buffered(1)),
                          seq, seq, ANY, const(32), const(8)],
        out_specs=[pl.BlockSpec((3, SEQ, WC), lambda b: (DZ_CONV // 3, b, 0)), const(32), const(8)],
        out_shape=[jax.ShapeDtypeStruct(dz.shape, bf16), jax.ShapeDtypeStruct((32, WC), f32),
                   jax.ShapeDtypeStruct((8, WC), f32)],
        scratch_shapes=[pltpu.VMEM((SEQ + HALO, WC), f32), pltpu.VMEM((SEQ + HALO, WC), f32),
                        pltpu.VMEM((CONV_K, 8, WC), f32)],
        input_output_aliases={5: 0}, name="conv_bwd",
        compiler_params=_params(("arbitrary",)))(z, z, z, u1, dcv, dz, dw32, cvec)


def _pool_counts(r0):
    t1 = r0 + 1 + lax.broadcasted_iota(jnp.int32, (CT, 1), 0)
    return [jnp.minimum(t1, w).astype(f32) for w in POOL_WINDOWS]


def _pool_sums(win, forward):
    n = CT + PHALO

    def sh(x, s):
        return pltpu.roll(x, (n - s) if forward else s, axis=0)

    s2 = win + sh(win, 1)
    s4 = s2[:, GD:] + sh(s2[:, GD:], 2)
    s8 = s4[:, GD:] + sh(s4[:, GD:], 4)
    s16 = s8[:, GD:] + sh(s8[:, GD:], 8)
    lo = 0 if forward else PHALO
    return [s[lo:lo + CT, :GD] for s in (s2, s4, s8, s16)]


def _pool_fwd(z, pw, pvec, BL, SEQ):
    T = BL * SEQ
    nct = SEQ // CT

    def body(pi_ref, pg_ref, pw_ref, vec_ref, o_ref, p_ref):
        p_ref[pl.ds(0, PHALO), :] = jnp.zeros((PHALO, WC), f32)

        def fill(c, carry):
            r0 = pl.multiple_of(c * CT, CT)
            p_ref[pl.ds(r0 + PHALO, CT), :] = pi_ref[pl.ds(r0, CT), :]
            return carry

        lax.fori_loop(0, nct, fill, 0)

        def step(c, carry):
            r0 = pl.multiple_of(c * CT, CT)
            sums = _pool_sums(p_ref[pl.ds(r0, CT + PHALO), :], False)
            cnt = _pool_counts(r0)
            pin = pi_ref[pl.ds(r0, CT), :]
            mixed = []
            for g in range(4):
                pooled = sums[g] / cnt[g] - pin[:, g * GD:(g + 1) * GD]
                mixed.append(jnp.dot(pooled.astype(bf16), pw_ref[g], preferred_element_type=f32))
            m0 = jnp.concatenate(mixed, axis=1) + vec_ref[0:1, :]
            pg = pg_ref[pl.ds(r0, CT), :]
            o_ref[pl.ds(r0, CT), :] = ((m0 * vec_ref[1:2, :]) * (pg * _sig(pg))).astype(bf16)
            return carry

        lax.fori_loop(0, nct, step, 0)

    def zs(col):
        return pl.BlockSpec((SEQ, WC), lambda b: (b, col // WC))

    return pl.pallas_call(
        body, grid=(BL,),
        in_specs=[zs(COL_PI), zs(COL_PG), pl.BlockSpec((4, GD, GD), lambda b: (0, 0, 0)),
                  pl.BlockSpec((8, WC), lambda b: (0, 0))],
        out_specs=pl.BlockSpec((SEQ, WC), lambda b: (b, 0)),
        out_shape=jax.ShapeDtypeStruct((T, WC), bf16),
        scratch_shapes=[pltpu.VMEM((SEQ + PHALO, WC), f32)], name="pool_fwd",
        compiler_params=_params(("parallel",)))(z, z, pw, pvec)


def _pool_bwd(z, dpl, dz, pw, pvec, BL, SEQ):
    nct = SEQ // CT

    def body(pi_ref, pg_ref, dpl_ref, dzin_ref, pw_ref, vec_ref, dz_ref, dpw_ref, dvec_ref, p_ref, e_ref, dp_ref):
        @pl.when(pl.program_id(0) == 0)
        def _():
            dpw_ref[...] = jnp.zeros_like(dpw_ref)
            dvec_ref[...] = jnp.zeros_like(dvec_ref)

        p_ref[pl.ds(0, PHALO), :] = jnp.zeros((PHALO, WC), f32)
        e_ref[pl.ds(SEQ, PHALO), :] = jnp.zeros((PHALO, WC), f32)

        def fill(c, carry):
            r0 = pl.multiple_of(c * CT, CT)
            p_ref[pl.ds(r0 + PHALO, CT), :] = pi_ref[pl.ds(r0, CT), :]
            return carry

        lax.fori_loop(0, nct, fill, 0)

        def step(c, carry):
            r0 = pl.multiple_of(c * CT, CT)
            sums = _pool_sums(p_ref[pl.ds(r0, CT + PHALO), :], False)
            cnt = _pool_counts(r0)
            pin = pi_ref[pl.ds(r0, CT), :]
            pooled = [(sums[g] / cnt[g] - pin[:, g * GD:(g + 1) * GD]).astype(bf16) for g in range(4)]
            m0 = jnp.concatenate(
                [jnp.dot(pooled[g], pw_ref[g], preferred_element_type=f32) for g in range(4)], axis=1) + vec_ref[0:1, :]
            scale = vec_ref[1:2, :]
            pg = pg_ref[pl.ds(r0, CT), :]
            spg = _sig(pg)
            dpl_ = dpl_ref[pl.ds(r0, CT), :]
            dmixed = dpl_ * (pg * spg)
            dz_ref[1, pl.ds(r0, CT), :] = (dpl_ * (m0 * scale) * _dsilu(pg, spg)).astype(bf16)
            dvec_ref[1:2, :] += _colsum(dmixed * m0)
            dm0 = dmixed * scale
            dvec_ref[0:1, :] += _colsum(dm0)
            dps, es = [], []
            for g in range(4):
                dm0g = dm0[:, g * GD:(g + 1) * GD].astype(bf16)
                dpw_ref[g] += lax.dot_general(pooled[g], dm0g, (((0,), (0,)), ((), ())), preferred_element_type=f32)
                dpg = lax.dot_general(dm0g, pw_ref[g], (((1,), (1,)), ((), ())), preferred_element_type=f32)
                dps.append(dpg)
                es.append(dpg / cnt[g])
            dp_ref[pl.ds(r0, CT), :] = jnp.concatenate(dps, axis=1)
            e_ref[pl.ds(r0, CT), :] = jnp.concatenate(es, axis=1)
            return carry

        lax.fori_loop(0, nct, step, 0)

        def back(c, carry):
            r0 = pl.multiple_of(c * CT, CT)
            fs = _pool_sums(e_ref[pl.ds(r0, CT + PHALO), :], True)
            dz_ref[0, pl.ds(r0, CT), :] = (jnp.concatenate(fs, axis=1) - dp_ref[pl.ds(r0, CT), :]).astype(bf16)
            return carry

        lax.fori_loop(0, nct, back, 0)

    def zs(col):
        return pl.BlockSpec((SEQ, WC), lambda b: (b, col // WC))

    return pl.pallas_call(
        body, grid=(BL,),
        in_specs=[zs(COL_PI), zs(COL_PG), pl.BlockSpec((SEQ, WC), lambda b: (b, 0)), ANY,
                  pl.BlockSpec((4, GD, GD), lambda b: (0, 0, 0)), pl.BlockSpec((8, WC), lambda b: (0, 0))],
        out_specs=[pl.BlockSpec((2, SEQ, WC), lambda b: (DZ_POOL // 2, b, 0)),
                   pl.BlockSpec((4, GD, GD), lambda b: (0, 0, 0)), pl.BlockSpec((8, WC), lambda b: (0, 0))],
        out_shape=[jax.ShapeDtypeStruct(dz.shape, bf16), jax.ShapeDtypeStruct((4, GD, GD), f32),
                   jax.ShapeDtypeStruct((8, WC), f32)],
        scratch_shapes=[pltpu.VMEM((SEQ + PHALO, WC), f32), pltpu.VMEM((SEQ + PHALO, WC), f32),
                        pltpu.VMEM((SEQ, WC), f32)],
        input_output_aliases={3: 0}, name="pool_bwd",
        compiler_params=_params(("arbitrary",)))(z, z, dpl, dz, pw, pvec)


def _attn_prologue(q_ref, k_ref, v_ref, qs0, qs1, kp, vp, SEQ):
    head0 = lax.broadcasted_iota(jnp.int32, (1, 2 * HEAD_DIM), 1) < HEAD_DIM
    kp[pl.ds(0, KEY_PAD), :] = jnp.zeros((KEY_PAD, 2 * HEAD_DIM), bf16)
    vp[pl.ds(0, KEY_PAD), :] = jnp.zeros((KEY_PAD, 2 * HEAD_DIM), bf16)

    def fill(g, carry):
        r0 = pl.multiple_of(g * QG, QG)
        q = q_ref[pl.ds(r0, QG), :] * (HEAD_DIM ** -0.5)
        qs0[pl.ds(r0, QG), :] = jnp.where(head0, q, 0.0).astype(bf16)
        qs1[pl.ds(r0, QG), :] = jnp.where(head0, 0.0, q).astype(bf16)
        kp[pl.ds(r0 + KEY_PAD, QG), :] = k_ref[pl.ds(r0, QG), :].astype(bf16)
        vp[pl.ds(r0 + KEY_PAD, QG), :] = v_ref[pl.ds(r0, QG), :].astype(bf16)
        return carry

    lax.fori_loop(0, SEQ // QG, fill, 0)
    return head0


def _attn_weights(qh, kw, bias):
    s = lax.dot_general(qh, kw, (((1,), (1,)), ((), ())), preferred_element_type=f32) + bias
    e = jnp.exp(s - jnp.max(s, axis=-1, keepdims=True))
    return e, 1.0 / jnp.sum(e, axis=-1, keepdims=True)


def _attn_fwd(z, bm, BL, SEQ):
    T = BL * SEQ
    W2 = 2 * HEAD_DIM

    def body(q_ref, k_ref, v_ref, ag_ref, bm_ref, at_ref, o_ref, e_ref, qs0, qs1, kp, vp):
        head0 = _attn_prologue(q_ref, k_ref, v_ref, qs0, qs1, kp, vp, SEQ)

        def group(g, carry):
            r0 = pl.multiple_of(g * QG, QG)
            kw = kp[pl.ds(r0, KW), :]
            vw = vp[pl.ds(r0, KW), :]
            variant = jnp.minimum(g, BIAS_VARIANTS - 1)
            outs = []
            for hh, qs in enumerate((qs0, qs1)):
                e, inv = _attn_weights(qs[pl.ds(r0, QG), :], kw, bm_ref[variant, hh])
                eb = e.astype(bf16)
                e_ref[pl.ds(r0, QG), hh * KW:(hh + 1) * KW] = eb
                outs.append(jnp.dot(eb, vw, preferred_element_type=f32) * inv)
            o = jnp.where(head0, outs[0], outs[1])
            o_ref[pl.ds(r0, QG), :] = o
            ag = ag_ref[pl.ds(r0, QG), :]
            at_ref[pl.ds(r0, QG), :] = (o * (ag * _sig(ag))).astype(bf16)
            return carry

        lax.fori_loop(0, SEQ // QG, group, 0, unroll=8)

    def zs(col):
        return pl.BlockSpec((SEQ, W2), lambda b, hp: (b, col // W2 + hp))

    pair = pl.BlockSpec((SEQ, W2), lambda b, hp: (b, hp))
    return pl.pallas_call(
        body, grid=(BL, WC // W2),
        in_specs=[zs(COL_Q), zs(COL_K), zs(COL_V), zs(COL_AG),
                  pl.BlockSpec((BIAS_VARIANTS, 2, QG, KW), lambda b, hp: (0, hp, 0, 0))],
        out_specs=[pair, pair, pl.BlockSpec((None, SEQ, 2 * KW), lambda b, hp: (hp, b, 0))],
        out_shape=[jax.ShapeDtypeStruct((T, WC), bf16), jax.ShapeDtypeStruct((T, WC), f32),
                   jax.ShapeDtypeStruct((WC // W2, T, 2 * KW), bf16)],
        scratch_shapes=[pltpu.VMEM((SEQ, W2), bf16), pltpu.VMEM((SEQ, W2), bf16),
                        pltpu.VMEM((SEQ + KEY_PAD, W2), bf16), pltpu.VMEM((SEQ + KEY_PAD, W2), bf16)],
        name="attn_fwd", compiler_params=_params(("parallel", "parallel")))(z, z, z, z, bm)


def _attn_bwd(z, dat, o, ew, dz, BL, SEQ):
    W2 = 2 * HEAD_DIM

    def body(q_ref, k_ref, v_ref, ag_ref, dat_ref, o_ref, e_ref, dzin_ref, dz_ref, dbm_ref, qs0, qs1, kp, vp, dka, dva):
        @pl.when(pl.program_id(1) == 0)
        def _():
            dbm_ref[...] = jnp.zeros_like(dbm_ref)

        head0 = _attn_prologue(q_ref, k_ref, v_ref, qs0, qs1, kp, vp, SEQ)
        dka[...] = jnp.zeros_like(dka)
        dva[...] = jnp.zeros_like(dva)

        def group(g, carry):
            r0 = pl.multiple_of(g * QG, QG)
            kw = kp[pl.ds(r0, KW), :]
            vw = vp[pl.ds(r0, KW), :]
            ag = ag_ref[pl.ds(r0, QG), :]
            do = dat_ref[pl.ds(r0, QG), :] * (ag * _sig(ag))
            dqs = []
            for hh, qs in enumerate((qs0, qs1)):
                qh = qs[pl.ds(r0, QG), :]
                eb = e_ref[pl.ds(r0, QG), hh * KW:(hh + 1) * KW]
                e = eb.astype(f32)
                inv = 1.0 / jnp.sum(e, axis=-1, keepdims=True)
                doh = (jnp.where(head0, do, 0.0) if hh == 0 else jnp.where(head0, 0.0, do)) * inv
                doh = doh.astype(bf16)
                dp = lax.dot_general(doh, vw, (((1,), (1,)), ((), ())), preferred_element_type=f32)
                ds_ = e * (dp - jnp.sum(e * dp, axis=-1, keepdims=True) * inv)
                dbm_ref[hh] += ds_
                dsb = ds_.astype(bf16)
                dqs.append(jnp.dot(dsb, kw, preferred_element_type=f32))
                dka[pl.ds(r0, KW), :] += lax.dot_general(dsb, qh, (((0,), (0,)), ((), ())), preferred_element_type=f32)
                dva[pl.ds(r0, KW), :] += lax.dot_general(eb, doh, (((0,), (0,)), ((), ())), preferred_element_type=f32)
            dq = jnp.where(head0, dqs[0], dqs[1]) * (HEAD_DIM ** -0.5)
            dz_ref[0, pl.ds(r0, QG), :] = dq.astype(bf16)
            dz_ref[3, pl.ds(r0, QG), :] = (dat_ref[pl.ds(r0, QG), :] * o_ref[pl.ds(r0, QG), :]
                                           * _dsilu(ag, _sig(ag))).astype(bf16)
            return carry

        lax.fori_loop(0, SEQ // QG, group, 0, unroll=8)

        def flush(g, carry):
            r0 = pl.multiple_of(g * QG, QG)
            dz_ref[1, pl.ds(r0, QG), :] = dka[pl.ds(r0 + KEY_PAD, QG), :].astype(bf16)
            dz_ref[2, pl.ds(r0, QG), :] = dva[pl.ds(r0 + KEY_PAD, QG), :].astype(bf16)
            return carry

        lax.fori_loop(0, SEQ // QG, flush, 0)

    def zs(col):
        return pl.BlockSpec((SEQ, W2), lambda hp, b: (b, col // W2 + hp))

    pair = pl.BlockSpec((SEQ, W2), lambda hp, b: (b, hp))
    return pl.pallas_call(
        body, grid=(WC // W2, BL),
        in_specs=[zs(COL_Q), zs(COL_K), zs(COL_V), zs(COL_AG), pair, pair,
                  pl.BlockSpec((None, SEQ, 2 * KW), lambda hp, b: (hp, b, 0)), ANY],
        out_specs=[pl.BlockSpec((4, SEQ, W2), lambda hp, b: (DZ_ATTN // 4, b, hp)),
                   pl.BlockSpec((2, QG, KW), lambda hp, b: (hp, 0, 0))],
        out_shape=[jax.ShapeDtypeStruct(dz.shape, bf16), jax.ShapeDtypeStruct((8, QG, KW), f32)],
        scratch_shapes=[pltpu.VMEM((SEQ, W2), bf16), pltpu.VMEM((SEQ, W2), bf16),
                        pltpu.VMEM((SEQ + KEY_PAD, W2), bf16), pltpu.VMEM((SEQ + KEY_PAD, W2), bf16),
                        pltpu.VMEM((SEQ + KEY_PAD, W2), f32), pltpu.VMEM((SEQ + KEY_PAD, W2), f32)],
        input_output_aliases={7: 0}, name="attn_bwd",
        compiler_params=_params(("parallel", "arbitrary")))(z, z, z, z, dat, o, ew, dz)


BIAS_TOP = KEY_PAD + MAX_REL + QG - 1


def _bias_matrix(table):
    n = 2 * MAX_REL
    wd = QG + KW
    e = jnp.concatenate([jnp.broadcast_to(table[:, n:], (8, BIAS_TOP - n + 1)), table[:, n - 1:BIAS_TOP - wd + 1:-1],
                         jnp.zeros((8, 1), f32)], axis=1)
    flat = jnp.broadcast_to(e[:, None, :], (8, QG, wd)).reshape(8, QG * wd)
    skew = flat[:, :QG * (wd - 1)].reshape(8, QG, wd - 1)
    vals = skew[:, :, QG - 1:QG - 1 + KW]
    r = np.arange(QG)[:, None] // CHUNK
    j = np.arange(KW)[None, :]
    band = (j // CHUNK >= r) & (j // CHUNK <= r + LEFT_CHUNKS)
    keep = np.stack([band & (j >= KEY_PAD - v * QG) for v in range(BIAS_VARIANTS)])
    return jnp.where(jnp.asarray(keep)[:, None], vals[None], NEG_INF)


def _bias_fold(dbm):
    wd = QG + KW
    placed = jnp.pad(dbm, ((0, 0), (0, 0), (QG - 1, 0))).reshape(8, QG * (wd - 1))
    return jnp.pad(placed, ((0, 0), (0, QG))).reshape(8, QG, wd)


def _bias_colsum(folded):
    width = folded.shape[2]

    def body(x_ref, o_ref):
        for h in range(8):
            o_ref[pl.ds(h, 1), :] = _colsum(x_ref[h])

    return pl.pallas_call(body, out_shape=jax.ShapeDtypeStruct((8, width), f32), name="bias_colsum",
                          compiler_params=_params())(folded)


def _bias_table_grad(colsum):
    n = 2 * MAX_REL
    wd = QG + KW
    clipped = jnp.sum(colsum[:, :BIAS_TOP - n + 1], axis=1, keepdims=True)
    return jnp.concatenate([jnp.zeros((8, BIAS_TOP - wd + 2), f32), colsum[:, wd - 2:BIAS_TOP - n:-1], clipped], axis=1)


GATE_SPAN = 3 * WC
TAIL_ROWS = 3 * WC + D


def _gate_specs(tm):
    return [pl.BlockSpec((tm, GATE_SPAN), lambda i: (i, COL_GM // GATE_SPAN)),
            pl.BlockSpec((tm, GATE_SPAN), lambda i: (i, COL_GM // GATE_SPAN + 1))]


def _gate_block(ga_ref, gb_ref, branch, half):
    k = 2 * branch + half
    ref, k = (ga_ref, k) if k < 3 else (gb_ref, k - 3)
    return _sig(ref[:, k * WC:(k + 1) * WC])


def _resident(shape):
    return pl.BlockSpec(shape, lambda i: (0,) * len(shape), pipeline_mode=pl.Buffered(1))


def _tail_fwd(z, acts, x2, lw, next_g=None, tgt=None):
    T = z.shape[0]
    tm = 256
    with_loss = tgt is not None
    assert with_loss != (next_g is not None)

    def body(cv_ref, at_ref, pv_ref, ga_ref, gb_ref, x_ref, wc_ref, wa_ref, wp_ref, wo_ref, g_ref, *rest):
        out_ref, merged_ref, y_ref = rest[1:4]
        ys = [jnp.dot(a[...], w[...], preferred_element_type=f32)
              for a, w in ((cv_ref, wc_ref), (at_ref, wa_ref), (pv_ref, wp_ref))]
        halves = []
        for half in range(2):
            cols = slice(half * WC, (half + 1) * WC)
            halves.append(sum(_gate_block(ga_ref, gb_ref, br, half) * ys[br][:, cols] for br in range(3)))
        merged = jnp.concatenate(halves, axis=1).astype(bf16)
        merged_ref[...] = merged
        y = jnp.dot(merged, wo_ref[...], preferred_element_type=f32)
        y_ref[...] = y
        r = lax.rsqrt(jnp.mean(y * y, axis=-1, keepdims=True) + EPS)
        out = x_ref[...] + (y * r) * g_ref[...]
        if with_loss:
            sq_ref = rest[4]
            e = out - rest[0][...]
            out_ref[...] = e / float(D)

            @pl.when(pl.program_id(0) == 0)
            def _():
                sq_ref[...] = jnp.zeros_like(sq_ref)

            sq_ref[...] += _colsum(e * e)
        else:
            out_ref[...] = out
            rn = lax.rsqrt(jnp.mean(out * out, axis=-1, keepdims=True) + EPS)
            h = (out * rn) * rest[0][...]
            rest[4][...] = h.astype(bf16)
            rest[5][...] = h.T.astype(bf16)

    act = pl.BlockSpec((tm, WC), lambda i: (i, 0))
    row = pl.BlockSpec((tm, D), lambda i: (i, 0))
    vec = pl.BlockSpec((1, D), lambda i: (0, 0))
    if with_loss:
        last_in, last_specs, last_shapes = tgt, [row, [vec]], [jax.ShapeDtypeStruct((1, D), f32)]
    else:
        last_in, last_specs = next_g, [vec, [row, pl.BlockSpec((D, tm), lambda i: (0, i))]]
        last_shapes = [jax.ShapeDtypeStruct((T, D), bf16), jax.ShapeDtypeStruct((D, T), bf16)]
    return pl.pallas_call(
        body, grid=(T // tm,),
        in_specs=[act, act, act] + _gate_specs(tm) + [row, _resident((WC, D)), _resident((WC, D)), _resident((WC, D)),
                                                      _resident((D, D)), _resident((1, D)), last_specs[0]],
        out_specs=[row, row, row] + last_specs[1],
        out_shape=[jax.ShapeDtypeStruct((T, D), f32), jax.ShapeDtypeStruct((T, D), bf16), jax.ShapeDtypeStruct((T, D), f32)]
        + last_shapes,
        name="tail_fwd", compiler_params=_params(("arbitrary",)))(
            *acts, z, z, x2, lw["w_conv_out"], lw["w_attn_out"], lw["w_pool_out"], lw["w_out"], lw["post_g"], last_in)


def _tail_bwd(z, dout, y, merged, acts, lw):
    T = z.shape[0]
    tm = 256
    nt = (((1,), (1,)), ((), ()))
    tn = (((0,), (0,)), ((), ()))

    def body(d_ref, y_ref, m_ref, cv_ref, at_ref, pv_ref, ga_ref, gb_ref, wc_ref, wa_ref, wp_ref, wo_ref, g_ref,
             dz_ref, dcv_ref, dat_ref, dpv_ref, dg_ref, dw_ref):
        @pl.when(pl.program_id(0) == 0)
        def _():
            dg_ref[...] = jnp.zeros_like(dg_ref)
            dw_ref[...] = jnp.zeros_like(dw_ref)

        y = y_ref[...]
        d = d_ref[...]
        r = lax.rsqrt(jnp.mean(y * y, axis=-1, keepdims=True) + EPS)
        yn = y * r
        dyn = d * g_ref[...]
        dy = (r * (dyn - yn * jnp.mean(dyn * yn, axis=-1, keepdims=True))).astype(bf16)
        dg_ref[...] += _colsum(d * yn)
        dw_ref[pl.ds(3 * WC, D), :] += lax.dot_general(dy, m_ref[...], tn, preferred_element_type=f32)
        dmerged = lax.dot_general(dy, wo_ref[...], nt, preferred_element_type=f32)
        for br, (a_ref, w_ref, da_ref) in enumerate(((cv_ref, wc_ref, dcv_ref), (at_ref, wa_ref, dat_ref),
                                                     (pv_ref, wp_ref, dpv_ref))):
            yb = jnp.dot(a_ref[...], w_ref[...], preferred_element_type=f32)
            halves = []
            for half in range(2):
                cols = slice(half * WC, (half + 1) * WC)
                s = _gate_block(ga_ref, gb_ref, br, half)
                dm = dmerged[:, cols]
                halves.append((dm * s).astype(bf16))
                dz_ref[2 * br + half] = (dm * yb[:, cols] * s * (1.0 - s)).astype(bf16)
            dyb = jnp.concatenate(halves, axis=1)
            da_ref[...] = lax.dot_general(dyb, w_ref[...], nt, preferred_element_type=f32)
            dw_ref[pl.ds(br * WC, WC), :] += lax.dot_general(a_ref[...], dyb, tn, preferred_element_type=f32)

    act = pl.BlockSpec((tm, WC), lambda i: (i, 0))
    row = pl.BlockSpec((tm, D), lambda i: (i, 0))

    def whole(shape):
        return pl.BlockSpec(shape, lambda i: (0, 0))

    return pl.pallas_call(
        body, grid=(T // tm,),
        in_specs=[row, row, row, act, act, act] + _gate_specs(tm) + [_resident((WC, D)), _resident((WC, D)), _resident((WC, D)),
                                                                     _resident((D, D)), _resident((1, D))],
        out_specs=[pl.BlockSpec((6, tm, WC), lambda i: (DZ_GM // 6, i, 0)), act, act, act, whole((1, D)),
                   whole((TAIL_ROWS, D))],
        out_shape=[jax.ShapeDtypeStruct((DZ_BLOCKS, T, WC), bf16)] + [jax.ShapeDtypeStruct((T, WC), f32)] * 3
        + [jax.ShapeDtypeStruct((1, D), f32), jax.ShapeDtypeStruct((TAIL_ROWS, D), f32)],
        name="tail_bwd", compiler_params=_params(("arbitrary",)))(
            dout, y, merged, *acts, z, z, lw["w_conv_out"], lw["w_attn_out"], lw["w_pool_out"], lw["w_out"], lw["post_g"])


def _adamw(name, g, w, m, v):
    R, C = w.shape
    tr = R
    for cand in (512, 256, 248, 128, 64, 32, 16, 8):
        if R % cand == 0 and cand * C * 4 <= 2 * 1024 * 1024:
            tr = cand
            break
    c1 = 1.0 - ADAM_B1
    c2 = 1.0 - ADAM_B2
    bc1 = 1.0 - ADAM_B1 ** ADAM_STEP
    bc2 = 1.0 - ADAM_B2 ** ADAM_STEP

    def body(g_ref, w_ref, m_ref, v_ref, d_ref, nm_ref, nv_ref):
        g_ = g_ref[...]
        nm = ADAM_B1 * m_ref[...] + c1 * g_
        nv = ADAM_B2 * v_ref[...] + c2 * (g_ * g_)
        nm_ref[...] = nm
        nv_ref[...] = nv
        d_ref[...] = -ADAM_LR * ((nm / bc1) / (jnp.sqrt(nv / bc2) + ADAM_EPS) + ADAM_WD * w_ref[...])

    spec = pl.BlockSpec((tr, C), lambda i: (i, 0))
    return pl.pallas_call(
        body, grid=(R // tr,), in_specs=[spec] * 4, out_specs=[spec] * 3,
        out_shape=[jax.ShapeDtypeStruct((R, C), f32)] * 3, name=name,
        compiler_params=_params(("parallel",)))(g, w, m, v)


def _sum_slots(name, parts):
    _, R, C = parts.shape
    tr = R
    for cand in (256, 128, 64, 32, 16, 8):
        if R % cand == 0 and cand * C * 4 * N_DEV <= 8 * 1024 * 1024:
            tr = cand
            break

    def body(p_ref, o_ref):
        acc = p_ref[0].astype(f32)
        for s in range(1, N_DEV):
            acc = acc + p_ref[s].astype(f32)
        o_ref[...] = acc

    return pl.pallas_call(
        body, grid=(R // tr,), in_specs=[pl.BlockSpec((N_DEV, tr, C), lambda i: (0, i, 0))],
        out_specs=pl.BlockSpec((tr, C), lambda i: (i, 0)), out_shape=jax.ShapeDtypeStruct((R, C), f32),
        name=name, compiler_params=_params(("parallel",)))(parts)


def _row_tile(rows, row_bytes, budget):
    for cand in (512, 256, 128, 64, 32, 16):
        if rows % cand == 0 and cand * row_bytes <= budget:
            return cand
    return rows


def _pair_sum(core, g, theirs):
    R2, C4 = theirs.shape
    tr = _row_tile(R2, C4 * 2, 2 * 1024 * 1024)
    nb = R2 // tr

    def body(core_ref, g_ref, t_ref, o_ref):
        o_ref[...] = (g_ref[...].astype(f32) + t_ref[...].astype(f32)).astype(bf16)

    return pl.pallas_call(
        body,
        grid_spec=pltpu.PrefetchScalarGridSpec(
            num_scalar_prefetch=1, grid=(nb,),
            in_specs=[pl.BlockSpec((tr, C4), lambda i, core_ref: (core_ref[0] * nb + i, 0)),
                      pl.BlockSpec((tr, C4), lambda i, core_ref: (i, 0))],
            out_specs=pl.BlockSpec((tr, C4), lambda i, core_ref: (i, 0))),
        out_shape=jax.ShapeDtypeStruct((R2, C4), bf16), name="pair_sum",
        compiler_params=_params(("parallel",)))(core, g, theirs)


def _chip_sum(chip, mine, others):
    _, R2, C = others.shape
    tr = _row_tile(R2, C * 4, 1024 * 1024)

    def body(chip_ref, m_ref, o_ref, out_ref):
        acc = m_ref[...].astype(f32)
        for s in range(N_CHIPS - 1):
            acc = acc + o_ref[s].astype(f32)
        out_ref[...] = acc

    return pl.pallas_call(
        body,
        grid_spec=pltpu.PrefetchScalarGridSpec(
            num_scalar_prefetch=1, grid=(R2 // tr,),
            in_specs=[pl.BlockSpec((tr, C), lambda i, chip_ref: (i, chip_ref[0])),
                      pl.BlockSpec((N_CHIPS - 1, tr, C), lambda i, chip_ref: (0, i, 0))],
            out_specs=pl.BlockSpec((tr, C), lambda i, chip_ref: (i, 0))),
        out_shape=jax.ShapeDtypeStruct((R2, C), f32), name="chip_sum",
        compiler_params=_params(("parallel",)))(chip, mine, others)


def _place():
    x, y, c = lax.axis_index("x"), lax.axis_index("y"), lax.axis_index("c")
    return x, y, c


def _flip(v, bit):
    return 1 - v if bit else v


CHIP_FLIPS = ((1, 0), (0, 1), (1, 1))


class _Sems:
    def __init__(self, send, recv):
        self.send, self.recv = send, recv
        self.pairs = 0

    def pair(self):
        k = self.pairs
        self.pairs += 1
        return self.send.at[k], self.recv.at[k]


def _remote(src, dst, lands, sems, to):
    s, r = sems.pair()
    copy = pltpu.make_async_remote_copy(src_ref=src, dst_ref=dst, send_sem=s, recv_sem=r, device_id=to, device_id_type=MESH)
    wait = pltpu.make_async_remote_copy(src_ref=lands, dst_ref=lands, send_sem=s, recv_sem=r, device_id=to, device_id_type=MESH)
    return copy, wait


def _exchange(name, build, srcs, lands, n_remote):
    n_s, n_l = len(srcs), len(lands)

    def body(*refs):
        send, recv = refs[n_s + 2 * n_l:]
        remotes, recvs = build(refs[:n_s], refs[n_s + n_l:n_s + 2 * n_l], _Sems(send, recv))
        for cp in remotes:
            cp.start()
        for rv in recvs:
            rv.wait_recv()
        for cp in remotes:
            cp.wait_send()

    return pl.pallas_call(
        body, in_specs=[ANY] * (n_s + n_l), out_specs=[ANY] * n_l,
        out_shape=[jax.ShapeDtypeStruct(t.shape, t.dtype) for t in lands],
        scratch_shapes=[pltpu.SemaphoreType.DMA((n_remote,)), pltpu.SemaphoreType.DMA((n_remote,))],
        input_output_aliases={n_s + i: i for i in range(n_l)}, name=name)(*srcs, *lands)


HBM = pl.BlockSpec(memory_space=pltpu.HBM)
SEMS = pl.BlockSpec(memory_space=pltpu.SEMAPHORE)
DATAFLOW = pltpu.SideEffectType.DATAFLOW_SIDE_EFFECTING


def _start(name, build, srcs, lands, n_remote, after):
    n_s, n_l = len(srcs), len(lands)

    def body(*refs):
        send, recv = refs[n_s + n_l + 1], refs[n_s + n_l + 2]
        remotes, _ = build(refs[:n_s], refs[n_s:n_s + n_l], _Sems(send, recv))
        for cp in remotes:
            cp.start()
        refs[-1][...] = jnp.zeros((8, 128), f32)

    arrays = [pltpu.with_memory_space_constraint(a, pltpu.HBM) for a in (*srcs, *lands)]
    out = pl.pallas_call(
        body, name=name, in_specs=[HBM] * (n_s + n_l) + [ANY],
        out_specs=(SEMS, SEMS, *[HBM] * (n_s + n_l), pl.BlockSpec(memory_space=pltpu.VMEM)),
        out_shape=(pltpu.SemaphoreType.DMA((n_remote,)), pltpu.SemaphoreType.DMA((n_remote,)),
                   *[pltpu.HBM(a.shape, a.dtype) for a in arrays], jax.ShapeDtypeStruct((8, 128), f32)),
        input_output_aliases={i: 2 + i for i in range(n_s + n_l)},
        compiler_params=pltpu.CompilerParams(has_side_effects=DATAFLOW))(*arrays, after)
    return dict(name=name, build=build, sems=out[:2], srcs=out[2:2 + n_s], lands=out[2 + n_s:2 + n_s + n_l], token=out[-1])


def _wait(started, after):
    srcs, lands, build = started["srcs"], started["lands"], started["build"]
    n_s, n_l = len(srcs), len(lands)
    after = list(after) if isinstance(after, (list, tuple)) else [after]

    def body(*refs):
        send, recv = refs[n_s + n_l], refs[n_s + n_l + 1]
        remotes, recvs = build(refs[:n_s], refs[n_s:n_s + n_l], _Sems(send, recv))
        for rv in recvs:
            rv.wait_recv()
        for cp in remotes:
            cp.wait_send()

    out = pl.pallas_call(
        body, name=started["name"] + "_wait", in_specs=[HBM] * (n_s + n_l) + [SEMS, SEMS] + [ANY] * len(after),
        out_specs=[HBM] * (n_s + n_l), out_shape=[pltpu.HBM(a.shape, a.dtype) for a in (*srcs, *lands)],
        input_output_aliases={i: i for i in range(n_s + n_l)},
        compiler_params=pltpu.CompilerParams(has_side_effects=DATAFLOW))(*srcs, *lands, *started["sems"], *after)
    return out[:n_s], out[n_s:]


def _gather_plans(n_split, n_all):
    def over_ici(src, land, sems):
        x, y, c = _place()
        chip = 2 * x + y
        remotes, recvs = [], []
        for a in range(n_all):
            for fx, fy in CHIP_FLIPS:
                px, py = _flip(x, fx), _flip(y, fy)
                if a < n_split:
                    r2 = src[a].shape[0] // 2
                    rows = pl.ds(c * r2, r2)
                    cp, rv = _remote(src[a].at[rows], land[a].at[chip, rows], land[a].at[2 * px + py, rows], sems, (px, py, c))
                else:
                    cp, rv = _remote(src[a], land[a].at[chip], land[a].at[2 * px + py], sems, (px, py, c))
                remotes.append(cp)
                recvs.append(rv)
        return remotes, recvs

    def over_d2d(src, land, sems):
        x, y, c = _place()
        remotes, recvs = [], []
        for a in range(n_split):
            r2 = land[a].shape[1] // 2
            for fx, fy in CHIP_FLIPS:
                owner = 2 * _flip(x, fx) + _flip(y, fy)
                mine = land[a].at[owner, pl.ds(c * r2, r2)]
                cp, rv = _remote(mine, mine, land[a].at[owner, pl.ds((1 - c) * r2, r2)], sems, (x, y, 1 - c))
                remotes.append(cp)
                recvs.append(rv)
        return remotes, recvs

    return over_ici, over_d2d


def _gather_begin(tag, shards, n_split, after):
    over_ici, _ = _gather_plans(n_split, len(shards))
    lands = [lax.empty((N_CHIPS,) + s.shape, s.dtype) for s in shards]
    return _start("gather_ici_" + tag, over_ici, shards, lands, 3 * len(shards), after)


def _gather_end(started, shards, n_split, after):
    _, over_d2d = _gather_plans(n_split, len(shards))
    lands = _exchange("gather_d2d", over_d2d, [], _wait(started, after)[1], 3 * n_split)
    chip = 2 * lax.axis_index("x") + lax.axis_index("y")
    return [lax.dynamic_update_slice_in_dim(g, s[None], chip, axis=0) for g, s in zip(lands, shards)]


def _reduce_plans(n):
    def to_sibling(src, land, sems):
        x, y, c = _place()
        remotes, recvs = [], []
        for a in range(n):
            r2 = src[a].shape[0] // 2
            cp, rv = _remote(src[a].at[pl.ds((1 - c) * r2, r2), :], land[a], land[a], sems, (x, y, 1 - c))
            remotes.append(cp)
            recvs.append(rv)
        return remotes, recvs

    def across_chips(src, land, sems):
        x, y, c = _place()
        remotes, recvs = [], []
        for a in range(n):
            cw = src[a].shape[1] // N_CHIPS
            for k, (fx, fy) in enumerate(CHIP_FLIPS):
                px, py = _flip(x, fx), _flip(y, fy)
                cp, rv = _remote(src[a].at[:, pl.ds((2 * px + py) * cw, cw)], land[a].at[k], land[a].at[k], sems, (px, py, c))
                remotes.append(cp)
                recvs.append(rv)
        return remotes, recvs

    def share(src, land, sems):
        x, y, c = _place()
        remotes, recvs = [], []
        for a in range(n):
            cp, rv = _remote(src[a], land[a], land[a], sems, (x, y, 1 - c))
            remotes.append(cp)
            recvs.append(rv)
        return remotes, recvs

    return to_sibling, across_chips, share


def _reduce_pair_begin(grads):
    to_sibling, _, _ = _reduce_plans(len(grads))
    lands = [lax.empty((g.shape[0] // 2, g.shape[1]), bf16) for g in grads]
    return _start("reduce_pair", to_sibling, grads, lands, len(grads), grads[0])


def _reduce_chips_begin(started, after):
    grads, theirs = _wait(started, after)
    n = len(grads)
    _, across_chips, _ = _reduce_plans(n)
    core = lax.axis_index("c").reshape(1).astype(jnp.int32)
    pair = [_pair_sum(core, g, t) for g, t in zip(grads, theirs)]
    lands = [lax.empty((N_CHIPS - 1, g.shape[0] // 2, g.shape[1] // N_CHIPS), bf16) for g in grads]
    return _start("reduce_chips", across_chips, pair, lands, 3 * n, pair[0])


def _reduce_end(started, after):
    x, y, c = _place()
    chip = (2 * x + y).reshape(1).astype(jnp.int32)
    pair, others = _wait(started, after)
    _, _, share = _reduce_plans(len(pair))
    mine = [_chip_sum(chip, p, o) for p, o in zip(pair, others)]
    sibs = _exchange("reduce_share", share, mine, [lax.empty(h.shape, f32) for h in mine], len(mine))
    return [jnp.where(c == 0, jnp.concatenate([h, s], axis=0), jnp.concatenate([s, h], axis=0))
            for h, s in zip(mine, sibs)]


def _to_all(src, land, sems):
    x, y, c = _place()
    me = 4 * x + 2 * y + c
    remotes, recvs = [], []
    for k in range(1, N_DEV):
        px, py, pc = _flip(x, (k >> 2) & 1), _flip(y, (k >> 1) & 1), _flip(c, k & 1)
        cp, rv = _remote(src[0], land[0].at[me], land[0].at[4 * px + 2 * py + pc], sems, (px, py, pc))
        remotes.append(cp)
        recvs.append(rv)
    return remotes, recvs


def _gather_small_begin(packed):
    return _start("gather_small", _to_all, [packed], [lax.empty((N_DEV,) + packed.shape, f32)], N_DEV - 1, packed)


def _gather_small_end(started, after):
    (packed,), (others,) = _wait(started, after)
    x, y, c = _place()
    return lax.dynamic_update_slice_in_dim(others, packed[None], 4 * x + 2 * y + c, axis=0)


def _gather_all(packed):
    others = _exchange("gather_all", _to_all, [packed], [lax.empty((N_DEV,) + packed.shape, f32)], N_DEV - 1)[0]
    x, y, c = _place()
    return lax.dynamic_update_slice_in_dim(others, packed[None], 4 * x + 2 * y + c, axis=0)


def _rows8(v):
    return jnp.pad(v[None, :], ((0, 7), (0, 0)))


def _vec_rows(vs):
    return jnp.pad(jnp.stack(vs), ((0, 8 - len(vs)), (0, 0)))


SMALL_ROWS = 224


def _pack_small(conv_vec, conv_dw, pool_vec, pool_w, pre_g, post_g, rel):
    return jnp.concatenate([
        conv_vec, conv_dw, pool_vec, pool_w.reshape(GD, WC),
        _rows8(pre_g).reshape(16, WC), _rows8(post_g).reshape(16, WC),
        jnp.pad(rel, ((0, 0), (0, D - rel.shape[1]))).reshape(16, WC)], axis=0)


def _unpack_small(p):
    conv_vec, pool_vec = p[0:8], p[40:48]
    return dict(
        conv_dw_b=conv_vec[0], conv_ln_g=conv_vec[1], conv_ln_b=conv_vec[2], conv_dw=p[8:8 + CONV_K],
        pool_b=pool_vec[0].reshape(4, GD), pool_scale=pool_vec[1], pool_w=p[48:176].reshape(4, GD, GD),
        pre_norm_g=p[176:192].reshape(8, D)[0], post_norm_g=p[192:208].reshape(8, D)[0],
        rel_bias=p[208:224].reshape(8, D)[:, :2 * MAX_REL + 1])


def _layer_fwd(x2, ht, z, lw, BL, SEQ, next_g=None, tgt=None):
    cv, u1 = _conv_fwd(z, lw["dw32"], lw["cvec"], BL, SEQ)
    at, attn_o, attn_e = _attn_fwd(z, lw["bm"], BL, SEQ)
    pv = _pool_fwd(z, lw["pw"], lw["pvec"], BL, SEQ)
    out, merged, y, *last = _tail_fwd(z, (cv, at, pv), x2, lw, next_g, tgt)
    saved = dict(x=x2, ht=ht, z=z, u1=u1, attn_o=attn_o, attn_e=attn_e, acts=(cv, at, pv), merged=merged, y=y)
    return (out, *last), saved


def _layer_bwd(dout, sv, lw, BL, SEQ, meanwhile=None):
    tail = _tail_bwd(sv["z"], dout, sv["y"], sv["merged"], sv["acts"], lw)
    dz, dacts, dpost = tail[0], tail[1:4], tail[4]
    dw_tail = tail[5].astype(bf16)
    if meanwhile is not None:
        meanwhile(dw_tail)
    dz, ddw, dcvec = _conv_bwd(sv["z"], sv["u1"], dacts[0], dz, lw["dw32"], lw["cvec"], BL, SEQ)
    dz, dbm = _attn_bwd(sv["z"], dacts[1], sv["attn_o"], sv["attn_e"], dz, BL, SEQ)
    dz, dpw, dpvec = _pool_bwd(sv["z"], dacts[2], dz, lw["pw"], lw["pvec"], BL, SEQ)
    drel = _bias_table_grad(_bias_colsum(_bias_fold(dbm)))
    small_gather = _gather_small_begin(_pack_small(dcvec, ddw, dpvec, dpw, jnp.zeros((D,), f32), dpost[0], drel))
    dw_in = _mm_dw_in(sv["ht"], dz, small_gather["token"])
    pair = _reduce_pair_begin([dw_in, dw_tail])
    if meanwhile is None:
        dx, dpre = _mm_dx(dz, lw["w_in"], sv["x"], lw["pre_g"], dout, pair["token"])
        reduction = _reduce_chips_begin(pair, dx)
    else:
        reduction = _reduce_chips_begin(pair, pair["token"])
        dx, dpre = _mm_dx(dz, lw["w_in"], sv["x"], lw["pre_g"], dout, reduction["token"])
    return dx, reduction, small_gather, dpre


BIG = ("w_in", "w_conv_out", "w_attn_out", "w_pool_out", "w_out")
PRE_ROWS = slice(176, 192)


def _layer_shards(w, l):
    return [w[k][l].astype(bf16) for k in BIG] + [w["conv_dw"][l]]


def _side_by_side(g):
    return jnp.transpose(g, (1, 0, 2)).reshape(g.shape[1], N_CHIPS * g.shape[2])


def _layer_weights(w_in, gathered, w, l, bm):
    lw = {k: _side_by_side(g) for k, g in zip(BIG[1:4], gathered[:3])}
    lw["w_in"] = w_in
    lw["w_out"] = gathered[3].reshape(D, D)
    lw["pre_g"] = w["pre_norm_g"][l][None]
    lw["post_g"] = w["post_norm_g"][l][None]
    lw["dw32"] = jnp.pad(_side_by_side(gathered[4]), ((0, 32 - CONV_K), (0, 0)))
    lw["cvec"] = _vec_rows([w["conv_dw_b"][l], w["conv_ln_g"][l], w["conv_ln_b"][l]])
    lw["bm"] = bm
    lw["pw"] = w["pool_w"][l].astype(bf16)
    lw["pvec"] = _vec_rows([w["pool_b"][l].reshape(WC), w["pool_scale"][l]])
    return lw


SMALL = ("pre_norm_g", "post_norm_g", "conv_dw_b", "conv_ln_g", "conv_ln_b", "rel_bias", "pool_w", "pool_b", "pool_scale")
ORDER = ("pre_norm_g", "post_norm_g", "w_in", "conv_dw", "conv_dw_b", "conv_ln_g", "conv_ln_b", "w_conv_out",
         "rel_bias", "w_attn_out", "pool_w", "pool_b", "pool_scale", "w_pool_out", "w_out")


def _pack_small_params(p):
    return jnp.concatenate([
        _pack_small(_vec_rows([p["conv_dw_b"][l], p["conv_ln_g"][l], p["conv_ln_b"][l]]), jnp.zeros((32, WC), f32),
                    _vec_rows([p["pool_b"][l].reshape(WC), p["pool_scale"][l]]), p["pool_w"][l],
                    p["pre_norm_g"][l], p["post_norm_g"][l], p["rel_bias"][l])
        for l in range(DEPTH)], axis=0)


def _unpack_small_params(packed):
    layers = [_unpack_small(packed[l * SMALL_ROWS:(l + 1) * SMALL_ROWS]) for l in range(DEPTH)]
    return {k: jnp.stack([layers[l][k] for l in range(DEPTH)]) for k in layers[0]}


def kernel(x, pre_norm_g, post_norm_g, w_in, conv_dw, conv_dw_b, conv_ln_g, conv_ln_b, w_conv_out, rel_bias, w_attn_out, pool_w, pool_b, pool_scale, w_pool_out, w_out, loss_target, m_pre_norm_g, m_post_norm_g, m_w_in, m_conv_dw, m_conv_dw_b, m_conv_ln_g, m_conv_ln_b, m_w_conv_out, m_rel_bias, m_w_attn_out, m_pool_w, m_pool_b, m_pool_scale, m_w_pool_out, m_w_out, v_pre_norm_g, v_post_norm_g, v_w_in, v_conv_dw, v_conv_dw_b, v_conv_ln_g, v_conv_ln_b, v_w_conv_out, v_rel_bias, v_w_attn_out, v_pool_w, v_pool_b, v_pool_scale, v_w_pool_out, v_w_out):
    BL, SEQ, _ = x.shape
    T = BL * SEQ
    w = dict(pre_norm_g=pre_norm_g, post_norm_g=post_norm_g, w_in=w_in, conv_dw=conv_dw, conv_dw_b=conv_dw_b,
             conv_ln_g=conv_ln_g, conv_ln_b=conv_ln_b, w_conv_out=w_conv_out, rel_bias=rel_bias, w_attn_out=w_attn_out,
             pool_w=pool_w, pool_b=pool_b, pool_scale=pool_scale, w_pool_out=w_pool_out, w_out=w_out)
    m = dict(pre_norm_g=m_pre_norm_g, post_norm_g=m_post_norm_g, w_in=m_w_in, conv_dw=m_conv_dw, conv_dw_b=m_conv_dw_b,
             conv_ln_g=m_conv_ln_g, conv_ln_b=m_conv_ln_b, w_conv_out=m_w_conv_out, rel_bias=m_rel_bias,
             w_attn_out=m_w_attn_out, pool_w=m_pool_w, pool_b=m_pool_b, pool_scale=m_pool_scale,
             w_pool_out=m_w_pool_out, w_out=m_w_out)
    v = dict(pre_norm_g=v_pre_norm_g, post_norm_g=v_post_norm_g, w_in=v_w_in, conv_dw=v_conv_dw, conv_dw_b=v_conv_dw_b,
             conv_ln_g=v_conv_ln_g, conv_ln_b=v_conv_ln_b, w_conv_out=v_w_conv_out, rel_bias=v_rel_bias,
             w_attn_out=v_w_attn_out, pool_w=v_pool_w, pool_b=v_pool_b, pool_scale=v_pool_scale,
             w_pool_out=v_w_pool_out, w_out=v_w_out)

    shards = [_layer_shards(w, l) for l in range(DEPTH)]
    x2 = x.reshape(T, D)
    h0, ht0 = _rms_pre(x2, pre_norm_g[0][None])
    first = _gather_begin("w_in0", shards[0][:1], 1, x2)
    bms = [_bias_matrix(rel_bias[l]) for l in range(DEPTH)]
    packs = [_pack_small_params(p) for p in (w, m, v)]
    w_in0 = _side_by_side(_gather_end(first, shards[0][:1], 1, [ht0, *bms, *packs])[0])
    rest0 = _gather_begin("rest0", shards[0][1:], 4, w_in0)
    all1 = _gather_begin("layer1", shards[1], 5, rest0["token"])
    z0 = _in_proj(h0, w_in0, after=all1["token"])
    lw0 = _layer_weights(w_in0, _gather_end(rest0, shards[0][1:], 4, z0), w, 0, bms[0])
    (out0, h1, ht1), saved0 = _layer_fwd(x2, ht0, z0, lw0, BL, SEQ, next_g=pre_norm_g[1][None])
    gathered1 = _gather_end(all1, shards[1], 5, out0)
    lw1 = _layer_weights(_side_by_side(gathered1[0]), gathered1[1:], w, 1, bms[1])
    z1 = _in_proj(h1, lw1["w_in"], after=h1)
    (dout, sq), saved1 = _layer_fwd(out0, ht1, z1, lw1, BL, SEQ, tgt=loss_target.reshape(T, D))
    loss = lax.psum(0.5 * jnp.sum(sq) / float(D), ("x", "y", "c"))

    summed = [None] * DEPTH
    dx1, reduction1, small_gather1, dpre1 = _layer_bwd(dout, saved1, lw1, BL, SEQ)

    def finish_layer1(after):
        summed[1] = _reduce_end(reduction1, after)

    grad_x, reduction0, small_gather0, dpre0 = _layer_bwd(dx1, saved0, lw0, BL, SEQ, meanwhile=finish_layer1)
    summed[0] = _reduce_end(reduction0, grad_x)
    dpre = _sum_slots("sum_small", _gather_all(jnp.concatenate([_rows8(dpre0[0]), _rows8(dpre1[0])], axis=0)))
    gsmall = []
    for l, started in enumerate((small_gather0, small_gather1)):
        g = _sum_slots("sum_small", _gather_small_end(started, dpre))
        gsmall += [g[:PRE_ROWS.start], dpre[8 * l:8 * l + 8].reshape(16, WC), g[PRE_ROWS.stop:]]
    gsmall = jnp.concatenate(gsmall, axis=0)

    grads, deltas, new_m, new_v = {}, {}, {}, {}
    for i, k in enumerate(BIG):
        if k == "w_in":
            g = jnp.stack([summed[l][0] for l in range(DEPTH)])
        elif k == "w_out":
            g = jnp.stack([summed[l][1][3 * WC:].T for l in range(DEPTH)])
        else:
            g = jnp.stack([summed[l][1][(i - 1) * WC:i * WC] for l in range(DEPTH)])
        grads[k] = g
        shape = w[k].shape
        flat2 = lambda a: a.reshape(shape[0] * shape[1], shape[2])
        d_, nm_, nv_ = _adamw("adamw_big", flat2(g), flat2(w[k]), flat2(m[k]), flat2(v[k]))
        deltas[k], new_m[k], new_v[k] = d_.reshape(shape), nm_.reshape(shape), nv_.reshape(shape)

    d_, nm_, nv_ = _adamw("adamw_small", gsmall, *packs)
    gs, ds, ms, vs = (_unpack_small_params(a) for a in (gsmall, d_, nm_, nv_))
    for k in SMALL:
        grads[k], deltas[k], new_m[k], new_v[k] = gs[k], ds[k], ms[k], vs[k]
    chip = 2 * lax.axis_index("x") + lax.axis_index("y")
    g_dw = lax.dynamic_slice_in_dim(gs["conv_dw"], chip * GD, GD, axis=2)
    flat2 = lambda a: a.reshape(DEPTH * CONV_K, GD)
    d_, nm_, nv_ = _adamw("adamw_conv_dw", flat2(g_dw), flat2(conv_dw), flat2(m["conv_dw"]), flat2(v["conv_dw"]))
    grads["conv_dw"] = g_dw
    deltas["conv_dw"], new_m["conv_dw"], new_v["conv_dw"] = (a.reshape(conv_dw.shape) for a in (d_, nm_, nv_))

    return (loss, grad_x.reshape(x.shape), *[grads[k] for k in ORDER], *[deltas[k] for k in ORDER],
            *[new_m[k] for k in ORDER], *[new_v[k] for k in ORDER])
```

```python
import numpy as np
import jax
import jax.numpy as jnp
from jax import lax
from jax.experimental import pallas as pl
from jax.experimental.pallas import tpu as pltpu

f32 = jnp.float32
bf16 = jnp.bfloat16

D = 1024
DEPTH = 2
WC = 512
HEAD_DIM = 64
CHUNK = 64
LEFT_CHUNKS = 8
KEY_PAD = LEFT_CHUNKS * CHUNK
MAX_REL = 256
CONV_K = 31
POOL_WINDOWS = (2, 4, 8, 16)
GD = 128
NCOL = 7680
EPS = 1e-6
NEG_INF = -1e30
COL_A, COL_B, COL_CG, COL_Q, COL_K, COL_V, COL_AG, COL_PI, COL_PG, COL_GM = (
    0, 512, 1024, 1536, 2048, 2560, 3072, 3584, 4096, 4608)

ADAM_LR = 0.001
ADAM_B1 = 0.9
ADAM_B2 = 0.999
ADAM_EPS = 1e-08
ADAM_WD = 0.01
ADAM_STEP = 10

QG = 256
KW = KEY_PAD + QG
BIAS_VARIANTS = KEY_PAD // QG + 1
CT = 256
HALO = 32
PHALO = 16
N_CHIPS = 4
N_DEV = 8
VMEM_LIMIT = 58 * 1024 * 1024
MESH = pl.DeviceIdType.MESH
ANY = pl.BlockSpec(memory_space=pl.ANY)

DZ_BLOCKS = 18
DZ_CONV, DZ_ATTN, DZ_POOL, DZ_GM = 0, 4, 8, 12


def _dz_block(c):
    return c + (c >= 3).astype(jnp.int32) + 2 * (c >= 9).astype(jnp.int32)


def _params(sem=None):
    return pltpu.CompilerParams(dimension_semantics=sem, vmem_limit_bytes=VMEM_LIMIT)


def _sig(x):
    return 1.0 / (1.0 + jnp.exp(-x))


def _dsilu(x, s):
    return s * (1.0 + x * (1.0 - s))


def _colsum(x):
    return jnp.sum(x, axis=0, keepdims=True)


def _rms_pre(x2, g):
    T = x2.shape[0]
    tm = 512

    def body(x_ref, g_ref, h_ref, ht_ref):
        x = x_ref[...]
        r = lax.rsqrt(jnp.mean(x * x, axis=-1, keepdims=True) + EPS)
        h = (x * r) * g_ref[...]
        h_ref[...] = h.astype(bf16)
        ht_ref[...] = h.T.astype(bf16)

    row = pl.BlockSpec((tm, D), lambda i: (i, 0))
    vec = pl.BlockSpec((1, D), lambda i: (0, 0))
    return pl.pallas_call(
        body, grid=(T // tm,), in_specs=[row, vec], out_specs=[row, pl.BlockSpec((D, tm), lambda i: (0, i))],
        out_shape=[jax.ShapeDtypeStruct((T, D), bf16), jax.ShapeDtypeStruct((D, T), bf16)], name="rms_pre",
        compiler_params=_params(("parallel",)))(x2, g)


def _in_proj(h, w_in, after):
    T = h.shape[0]
    tm, tn = 512, 1536

    def body(h_ref, w_ref, after_ref, z_ref):
        for n0 in range(0, NCOL, tn):
            z_ref[:, n0:n0 + tn] = jnp.dot(h_ref[...], w_ref[:, n0:n0 + tn], preferred_element_type=f32)

    return pl.pallas_call(
        body, grid=(T // tm,),
        in_specs=[pl.BlockSpec((tm, D), lambda i: (i, 0)),
                  pl.BlockSpec((D, NCOL), lambda i: (0, 0), pipeline_mode=pl.Buffered(1)), ANY],
        out_specs=pl.BlockSpec((tm, NCOL), lambda i: (i, 0)), out_shape=jax.ShapeDtypeStruct((T, NCOL), f32),
        name="mm_in", compiler_params=_params(("parallel",)))(h, w_in, after)


DZ_SPANS = ((DZ_CONV, 3), (DZ_ATTN, 4), (DZ_POOL, 2), (DZ_GM, 6))


def _mm_dx(dz, w_in, x2, g, dout, after):
    T = dz.shape[1]
    tm = 512

    def body(conv_ref, attn_ref, pool_ref, gm_ref, w_ref, x_ref, g_ref, d_ref, after_ref, dx_ref, dg_ref):
        dh = None
        col = 0
        for ref, (_, blocks) in zip((conv_ref, attn_ref, pool_ref, gm_ref), DZ_SPANS):
            for b in range(blocks):
                p = lax.dot_general(ref[b], w_ref[:, col * WC:(col + 1) * WC], (((1,), (1,)), ((), ())),
                                    preferred_element_type=f32)
                dh = p if dh is None else dh + p
                col += 1
        x = x_ref[...]
        r = lax.rsqrt(jnp.mean(x * x, axis=-1, keepdims=True) + EPS)
        xn = x * r
        dxn = dh * g_ref[...]
        dx_ref[...] = r * (dxn - xn * jnp.mean(dxn * xn, axis=-1, keepdims=True)) + d_ref[...]

        @pl.when(pl.program_id(0) == 0)
        def _():
            dg_ref[...] = jnp.zeros_like(dg_ref)

        dg_ref[...] += _colsum(dh * xn)

    spans = [pl.BlockSpec((blocks, tm, WC), lambda i, first=first, blocks=blocks: (first // blocks, i, 0))
             for first, blocks in DZ_SPANS]
    row = pl.BlockSpec((tm, D), lambda i: (i, 0))
    vec = pl.BlockSpec((1, D), lambda i: (0, 0))
    return pl.pallas_call(
        body, grid=(T // tm,),
        in_specs=spans + [pl.BlockSpec((D, NCOL), lambda i: (0, 0), pipeline_mode=pl.Buffered(1)), row, vec, row, ANY],
        out_specs=[row, vec], out_shape=[jax.ShapeDtypeStruct((T, D), f32), jax.ShapeDtypeStruct((1, D), f32)],
        name="mm_dx", compiler_params=_params(("arbitrary",)))(dz, dz, dz, dz, w_in, x2, g, dout, after)


def _mm_dw_in(ht, dz, after):
    T = dz.shape[1]

    def body(ht_ref, dz_ref, after_ref, o_ref):
        o_ref[...] = jnp.dot(ht_ref[...], dz_ref[...], preferred_element_type=f32).astype(bf16)

    return pl.pallas_call(
        body, grid=(NCOL // WC,),
        in_specs=[pl.BlockSpec((D, T), lambda j: (0, 0), pipeline_mode=pl.Buffered(1)),
                  pl.BlockSpec((None, T, WC), lambda j: (_dz_block(j), 0, 0)), ANY],
        out_specs=pl.BlockSpec((D, WC), lambda j: (0, j)), out_shape=jax.ShapeDtypeStruct((D, NCOL), bf16),
        name="mm_dw_in", compiler_params=_params(("parallel",)))(ht, dz, after)


def _conv_delays():
    return [(8 * a + b, a, b) for b in range(8) for a in range(4) if 8 * a + b < CONV_K]


def _conv_rolls(win):
    return [win if b == 0 else pltpu.roll(win, b, axis=0) for b in range(8)]


def _conv_taps(rolled, dw_ref):
    acc = None
    for d, a, b in _conv_delays():
        term = rolled[b][HALO - 8 * a:HALO - 8 * a + CT, :] * dw_ref[pl.ds(CONV_K - 1 - d, 1), :]
        acc = term if acc is None else acc + term
    return acc


def _conv_fwd(z, dw32, cvec, BL, SEQ):
    T = BL * SEQ
    nct = SEQ // CT

    def body(a_ref, b_ref, cg_ref, dw_ref, vec_ref, o_ref, u1_ref, p_ref):
        p_ref[pl.ds(0, HALO), :] = jnp.zeros((HALO, WC), f32)

        def glu(c, carry):
            r0 = pl.multiple_of(c * CT, CT)
            p_ref[pl.ds(r0 + HALO, CT), :] = a_ref[pl.ds(r0, CT), :] * _sig(b_ref[pl.ds(r0, CT), :])
            return carry

        lax.fori_loop(0, nct, glu, 0)

        def step(c, carry):
            r0 = pl.multiple_of(c * CT, CT)
            u1 = _conv_taps(_conv_rolls(p_ref[pl.ds(r0, CT + HALO), :]), dw_ref) + vec_ref[0:1, :]
            u1_ref[pl.ds(r0, CT), :] = u1
            xc = u1 - jnp.mean(u1, axis=-1, keepdims=True)
            rs = lax.rsqrt(jnp.mean(xc * xc, axis=-1, keepdims=True) + EPS)
            u2 = (xc * rs) * vec_ref[1:2, :] + vec_ref[2:3, :]
            cg = cg_ref[pl.ds(r0, CT), :]
            o_ref[pl.ds(r0, CT), :] = ((u2 * _sig(u2)) * (cg * _sig(cg))).astype(bf16)
            return carry

        lax.fori_loop(0, nct, step, 0)

    def zs(col):
        return pl.BlockSpec((SEQ, WC), lambda b: (b, col // WC))

    seq = pl.BlockSpec((SEQ, WC), lambda b: (b, 0))
    return pl.pallas_call(
        body, grid=(BL,),
        in_specs=[zs(COL_A), zs(COL_B), zs(COL_CG), pl.BlockSpec((32, WC), lambda b: (0, 0)),
                  pl.BlockSpec((8, WC), lambda b: (0, 0))],
        out_specs=[seq, seq],
        out_shape=[jax.ShapeDtypeStruct((T, WC), bf16), jax.ShapeDtypeStruct((T, WC), f32)],
        scratch_shapes=[pltpu.VMEM((SEQ + HALO, WC), f32)], name="conv_fwd",
        compiler_params=_params(("parallel",)))(z, z, z, dw32, cvec)


def _conv_bwd(z, u1, dcv, dz, dw32, cvec, BL, SEQ):
    nct = SEQ // CT

    def body(a_ref, b_ref, cg_ref, u1_ref, dcv_ref, dzin_ref, dw_ref, vec_ref, dz_ref, ddw_ref, dvec_ref,
             p_ref, q_ref, taps_ref):
        @pl.when(pl.program_id(0) == 0)
        def _():
            ddw_ref[...] = jnp.zeros_like(ddw_ref)
            dvec_ref[...] = jnp.zeros_like(dvec_ref)

        p_ref[pl.ds(0, HALO), :] = jnp.zeros((HALO, WC), f32)
        q_ref[pl.ds(SEQ, HALO), :] = jnp.zeros((HALO, WC), f32)

        def glu(c, carry):
            r0 = pl.multiple_of(c * CT, CT)
            p_ref[pl.ds(r0 + HALO, CT), :] = a_ref[pl.ds(r0, CT), :] * _sig(b_ref[pl.ds(r0, CT), :])
            return carry

        lax.fori_loop(0, nct, glu, 0)

        def step(c, carry):
            r0 = pl.multiple_of(c * CT, CT)
            rolled = _conv_rolls(p_ref[pl.ds(r0, CT + HALO), :])
            u1 = u1_ref[pl.ds(r0, CT), :]
            xc = u1 - jnp.mean(u1, axis=-1, keepdims=True)
            rs = lax.rsqrt(jnp.mean(xc * xc, axis=-1, keepdims=True) + EPS)
            nrm = xc * rs
            u2 = nrm * vec_ref[1:2, :] + vec_ref[2:3, :]
            s2 = _sig(u2)
            u3 = u2 * s2
            cg = cg_ref[pl.ds(r0, CT), :]
            scg = _sig(cg)
            dcv_ = dcv_ref[pl.ds(r0, CT), :]
            dz_ref[2, pl.ds(r0, CT), :] = (dcv_ * u3 * _dsilu(cg, scg)).astype(bf16)
            du2 = dcv_ * (cg * scg) * _dsilu(u2, s2)
            dvec_ref[1:2, :] += _colsum(du2 * nrm)
            dvec_ref[2:3, :] += _colsum(du2)
            dn = du2 * vec_ref[1:2, :]
            du1 = rs * (dn - jnp.mean(dn, axis=-1, keepdims=True)
                        - nrm * jnp.mean(dn * nrm, axis=-1, keepdims=True))
            dvec_ref[0:1, :] += _colsum(du1)
            q_ref[pl.ds(r0, CT), :] = du1
            for d, a, b in _conv_delays():
                prod = du1 * rolled[b][HALO - 8 * a:HALO - 8 * a + CT, :]
                taps_ref[CONV_K - 1 - d] += jnp.sum(prod.reshape(CT // 8, 8, WC), axis=0)
            return carry

        taps_ref[...] = jnp.zeros_like(taps_ref)
        lax.fori_loop(0, nct, step, 0)
        for row in range(CONV_K):
            ddw_ref[pl.ds(row, 1), :] += _colsum(taps_ref[row])

        def back(c, carry):
            r0 = pl.multiple_of(c * CT, CT)
            wq = q_ref[pl.ds(r0, CT + HALO), :]
            up = {}
            acc = None
            for d, a, b in _conv_delays():
                if b not in up:
                    up[b] = wq if b == 0 else pltpu.roll(wq, CT + HALO - b, axis=0)
                term = up[b][8 * a:8 * a + CT, :] * dw_ref[pl.ds(CONV_K - 1 - d, 1), :]
                acc = term if acc is None else acc + term
            a_ = a_ref[pl.ds(r0, CT), :]
            sb = _sig(b_ref[pl.ds(r0, CT), :])
            dz_ref[0, pl.ds(r0, CT), :] = (acc * sb).astype(bf16)
            dz_ref[1, pl.ds(r0, CT), :] = (acc * a_ * sb * (1.0 - sb)).astype(bf16)
            return carry

        lax.fori_loop(0, nct, back, 0)

    def const(r):
        return pl.BlockSpec((r, WC), lambda b: (0, 0))

    first = [pl.BlockSpec((SEQ, WC), lambda b, col=col: (b, col // WC)) for col in (COL_A, COL_B)]
    seq = pl.BlockSpec((SEQ, WC), lambda b: (b, 0), pipeline_mode=pl.Buffered(1))
    return pl.pallas_call(
        body, grid=(BL,),
        in_specs=first + [pl.BlockSpec((SEQ, WC), lambda b: (b, COL_CG // WC), pipeline_mode=pl.Buffered(1)),
                          seq, seq, ANY, const(32), const(8)],
        out_specs=[pl.BlockSpec((3, SEQ, WC), lambda b: (DZ_CONV // 3, b, 0)), const(32), const(8)],
        out_shape=[jax.ShapeDtypeStruct(dz.shape, bf16), jax.ShapeDtypeStruct((32, WC), f32),
                   jax.ShapeDtypeStruct((8, WC), f32)],
        scratch_shapes=[pltpu.VMEM((SEQ + HALO, WC), f32), pltpu.VMEM((SEQ + HALO, WC), f32),
                        pltpu.VMEM((CONV_K, 8, WC), f32)],
        input_output_aliases={5: 0}, name="conv_bwd",
        compiler_params=_params(("arbitrary",)))(z, z, z, u1, dcv, dz, dw32, cvec)


def _pool_counts(r0):
    t1 = r0 + 1 + lax.broadcasted_iota(jnp.int32, (CT, 1), 0)
    return [jnp.minimum(t1, w).astype(f32) for w in POOL_WINDOWS]


def _pool_sums(win, forward):
    n = CT + PHALO

    def sh(x, s):
        return pltpu.roll(x, (n - s) if forward else s, axis=0)

    s2 = win + sh(win, 1)
    s4 = s2[:, GD:] + sh(s2[:, GD:], 2)
    s8 = s4[:, GD:] + sh(s4[:, GD:], 4)
    s16 = s8[:, GD:] + sh(s8[:, GD:], 8)
    lo = 0 if forward else PHALO
    return [s[lo:lo + CT, :GD] for s in (s2, s4, s8, s16)]


def _pool_fwd(z, pw, pvec, BL, SEQ):
    T = BL * SEQ
    nct = SEQ // CT

    def body(pi_ref, pg_ref, pw_ref, vec_ref, o_ref, p_ref):
        p_ref[pl.ds(0, PHALO), :] = jnp.zeros((PHALO, WC), f32)

        def fill(c, carry):
            r0 = pl.multiple_of(c * CT, CT)
            p_ref[pl.ds(r0 + PHALO, CT), :] = pi_ref[pl.ds(r0, CT), :]
            return carry

        lax.fori_loop(0, nct, fill, 0)

        def step(c, carry):
            r0 = pl.multiple_of(c * CT, CT)
            sums = _pool_sums(p_ref[pl.ds(r0, CT + PHALO), :], False)
            cnt = _pool_counts(r0)
            pin = pi_ref[pl.ds(r0, CT), :]
            mixed = []
            for g in range(4):
                pooled = sums[g] / cnt[g] - pin[:, g * GD:(g + 1) * GD]
                mixed.append(jnp.dot(pooled.astype(bf16), pw_ref[g], preferred_element_type=f32))
            m0 = jnp.concatenate(mixed, axis=1) + vec_ref[0:1, :]
            pg = pg_ref[pl.ds(r0, CT), :]
            o_ref[pl.ds(r0, CT), :] = ((m0 * vec_ref[1:2, :]) * (pg * _sig(pg))).astype(bf16)
            return carry

        lax.fori_loop(0, nct, step, 0)

    def zs(col):
        return pl.BlockSpec((SEQ, WC), lambda b: (b, col // WC))

    return pl.pallas_call(
        body, grid=(BL,),
        in_specs=[zs(COL_PI), zs(COL_PG), pl.BlockSpec((4, GD, GD), lambda b: (0, 0, 0)),
                  pl.BlockSpec((8, WC), lambda b: (0, 0))],
        out_specs=pl.BlockSpec((SEQ, WC), lambda b: (b, 0)),
        out_shape=jax.ShapeDtypeStruct((T, WC), bf16),
        scratch_shapes=[pltpu.VMEM((SEQ + PHALO, WC), f32)], name="pool_fwd",
        compiler_params=_params(("parallel",)))(z, z, pw, pvec)


def _pool_bwd(z, dpl, dz, pw, pvec, BL, SEQ):
    nct = SEQ // CT

    def body(pi_ref, pg_ref, dpl_ref, dzin_ref, pw_ref, vec_ref, dz_ref, dpw_ref, dvec_ref, p_ref, e_ref, dp_ref):
        @pl.when(pl.program_id(0) == 0)
        def _():
            dpw_ref[...] = jnp.zeros_like(dpw_ref)
            dvec_ref[...] = jnp.zeros_like(dvec_ref)

        p_ref[pl.ds(0, PHALO), :] = jnp.zeros((PHALO, WC), f32)
        e_ref[pl.ds(SEQ, PHALO), :] = jnp.zeros((PHALO, WC), f32)

        def fill(c, carry):
            r0 = pl.multiple_of(c * CT, CT)
            p_ref[pl.ds(r0 + PHALO, CT), :] = pi_ref[pl.ds(r0, CT), :]
            return carry

        lax.fori_loop(0, nct, fill, 0)

        def step(c, carry):
            r0 = pl.multiple_of(c * CT, CT)
            sums = _pool_sums(p_ref[pl.ds(r0, CT + PHALO), :], False)
            cnt = _pool_counts(r0)
            pin = pi_ref[pl.ds(r0, CT), :]
            pooled = [(sums[g] / cnt[g] - pin[:, g * GD:(g + 1) * GD]).astype(bf16) for g in range(4)]
            m0 = jnp.concatenate(
                [jnp.dot(pooled[g], pw_ref[g], preferred_element_type=f32) for g in range(4)], axis=1) + vec_ref[0:1, :]
            scale = vec_ref[1:2, :]
            pg = pg_ref[pl.ds(r0, CT), :]
            spg = _sig(pg)
            dpl_ = dpl_ref[pl.ds(r0, CT), :]
            dmixed = dpl_ * (pg * spg)
            dz_ref[1, pl.ds(r0, CT), :] = (dpl_ * (m0 * scale) * _dsilu(pg, spg)).astype(bf16)
            dvec_ref[1:2, :] += _colsum(dmixed * m0)
            dm0 = dmixed * scale
            dvec_ref[0:1, :] += _colsum(dm0)
            dps, es = [], []
            for g in range(4):
                dm0g = dm0[:, g * GD:(g + 1) * GD].astype(bf16)
                dpw_ref[g] += lax.dot_general(pooled[g], dm0g, (((0,), (0,)), ((), ())), preferred_element_type=f32)
                dpg = lax.dot_general(dm0g, pw_ref[g], (((1,), (1,)), ((), ())), preferred_element_type=f32)
                dps.append(dpg)
                es.append(dpg / cnt[g])
            dp_ref[pl.ds(r0, CT), :] = jnp.concatenate(dps, axis=1)
            e_ref[pl.ds(r0, CT), :] = jnp.concatenate(es, axis=1)
            return carry

        lax.fori_loop(0, nct, step, 0)

        def back(c, carry):
            r0 = pl.multiple_of(c * CT, CT)
            fs = _pool_sums(e_ref[pl.ds(r0, CT + PHALO), :], True)
            dz_ref[0, pl.ds(r0, CT), :] = (jnp.concatenate(fs, axis=1) - dp_ref[pl.ds(r0, CT), :]).astype(bf16)
            return carry

        lax.fori_loop(0, nct, back, 0)

    def zs(col):
        return pl.BlockSpec((SEQ, WC), lambda b: (b, col // WC))

    return pl.pallas_call(
        body, grid=(BL,),
        in_specs=[zs(COL_PI), zs(COL_PG), pl.BlockSpec((SEQ, WC), lambda b: (b, 0)), ANY,
                  pl.BlockSpec((4, GD, GD), lambda b: (0, 0, 0)), pl.BlockSpec((8, WC), lambda b: (0, 0))],
        out_specs=[pl.BlockSpec((2, SEQ, WC), lambda b: (DZ_POOL // 2, b, 0)),
                   pl.BlockSpec((4, GD, GD), lambda b: (0, 0, 0)), pl.BlockSpec((8, WC), lambda b: (0, 0))],
        out_shape=[jax.ShapeDtypeStruct(dz.shape, bf16), jax.ShapeDtypeStruct((4, GD, GD), f32),
                   jax.ShapeDtypeStruct((8, WC), f32)],
        scratch_shapes=[pltpu.VMEM((SEQ + PHALO, WC), f32), pltpu.VMEM((SEQ + PHALO, WC), f32),
                        pltpu.VMEM((SEQ, WC), f32)],
        input_output_aliases={3: 0}, name="pool_bwd",
        compiler_params=_params(("arbitrary",)))(z, z, dpl, dz, pw, pvec)


def _attn_prologue(q_ref, k_ref, v_ref, qs0, qs1, kp, vp, SEQ):
    head0 = lax.broadcasted_iota(jnp.int32, (1, 2 * HEAD_DIM), 1) < HEAD_DIM
    kp[pl.ds(0, KEY_PAD), :] = jnp.zeros((KEY_PAD, 2 * HEAD_DIM), bf16)
    vp[pl.ds(0, KEY_PAD), :] = jnp.zeros((KEY_PAD, 2 * HEAD_DIM), bf16)

    def fill(g, carry):
        r0 = pl.multiple_of(g * QG, QG)
        q = q_ref[pl.ds(r0, QG), :] * (HEAD_DIM ** -0.5)
        qs0[pl.ds(r0, QG), :] = jnp.where(head0, q, 0.0).astype(bf16)
        qs1[pl.ds(r0, QG), :] = jnp.where(head0, 0.0, q).astype(bf16)
        kp[pl.ds(r0 + KEY_PAD, QG), :] = k_ref[pl.ds(r0, QG), :].astype(bf16)
        vp[pl.ds(r0 + KEY_PAD, QG), :] = v_ref[pl.ds(r0, QG), :].astype(bf16)
        return carry

    lax.fori_loop(0, SEQ // QG, fill, 0)
    return head0


def _attn_weights(qh, kw, bias):
    s = lax.dot_general(qh, kw, (((1,), (1,)), ((), ())), preferred_element_type=f32) + bias
    e = jnp.exp(s - jnp.max(s, axis=-1, keepdims=True))
    return e, 1.0 / jnp.sum(e, axis=-1, keepdims=True)


def _attn_fwd(z, bm, BL, SEQ):
    T = BL * SEQ
    W2 = 2 * HEAD_DIM

    def body(q_ref, k_ref, v_ref, ag_ref, bm_ref, at_ref, o_ref, e_ref, qs0, qs1, kp, vp):
        head0 = _attn_prologue(q_ref, k_ref, v_ref, qs0, qs1, kp, vp, SEQ)

        def group(g, carry):
            r0 = pl.multiple_of(g * QG, QG)
            kw = kp[pl.ds(r0, KW), :]
            vw = vp[pl.ds(r0, KW), :]
            variant = jnp.minimum(g, BIAS_VARIANTS - 1)
            outs = []
            for hh, qs in enumerate((qs0, qs1)):
                e, inv = _attn_weights(qs[pl.ds(r0, QG), :], kw, bm_ref[variant, hh])
                eb = e.astype(bf16)
                e_ref[pl.ds(r0, QG), hh * KW:(hh + 1) * KW] = eb
                outs.append(jnp.dot(eb, vw, preferred_element_type=f32) * inv)
            o = jnp.where(head0, outs[0], outs[1])
            o_ref[pl.ds(r0, QG), :] = o
            ag = ag_ref[pl.ds(r0, QG), :]
            at_ref[pl.ds(r0, QG), :] = (o * (ag * _sig(ag))).astype(bf16)
            return carry

        lax.fori_loop(0, SEQ // QG, group, 0, unroll=8)

    def zs(col):
        return pl.BlockSpec((SEQ, W2), lambda b, hp: (b, col // W2 + hp))

    pair = pl.BlockSpec((SEQ, W2), lambda b, hp: (b, hp))
    return pl.pallas_call(
        body, grid=(BL, WC // W2),
        in_specs=[zs(COL_Q), zs(COL_K), zs(COL_V), zs(COL_AG),
                  pl.BlockSpec((BIAS_VARIANTS, 2, QG, KW), lambda b, hp: (0, hp, 0, 0))],
        out_specs=[pair, pair, pl.BlockSpec((None, SEQ, 2 * KW), lambda b, hp: (hp, b, 0))],
        out_shape=[jax.ShapeDtypeStruct((T, WC), bf16), jax.ShapeDtypeStruct((T, WC), f32),
                   jax.ShapeDtypeStruct((WC // W2, T, 2 * KW), bf16)],
        scratch_shapes=[pltpu.VMEM((SEQ, W2), bf16), pltpu.VMEM((SEQ, W2), bf16),
                        pltpu.VMEM((SEQ + KEY_PAD, W2), bf16), pltpu.VMEM((SEQ + KEY_PAD, W2), bf16)],
        name="attn_fwd", compiler_params=_params(("parallel", "parallel")))(z, z, z, z, bm)


def _attn_bwd(z, dat, o, ew, dz, BL, SEQ):
    W2 = 2 * HEAD_DIM

    def body(q_ref, k_ref, v_ref, ag_ref, dat_ref, o_ref, e_ref, dzin_ref, dz_ref, dbm_ref, qs0, qs1, kp, vp, dka, dva):
        @pl.when(pl.program_id(1) == 0)
        def _():
            dbm_ref[...] = jnp.zeros_like(dbm_ref)

        head0 = _attn_prologue(q_ref, k_ref, v_ref, qs0, qs1, kp, vp, SEQ)
        dka[...] = jnp.zeros_like(dka)
        dva[...] = jnp.zeros_like(dva)

        def group(g, carry):
            r0 = pl.multiple_of(g * QG, QG)
            kw = kp[pl.ds(r0, KW), :]
            vw = vp[pl.ds(r0, KW), :]
            ag = ag_ref[pl.ds(r0, QG), :]
            do = dat_ref[pl.ds(r0, QG), :] * (ag * _sig(ag))
            dqs = []
            for hh, qs in enumerate((qs0, qs1)):
                qh = qs[pl.ds(r0, QG), :]
                eb = e_ref[pl.ds(r0, QG), hh * KW:(hh + 1) * KW]
                e = eb.astype(f32)
                inv = 1.0 / jnp.sum(e, axis=-1, keepdims=True)
                doh = (jnp.where(head0, do, 0.0) if hh == 0 else jnp.where(head0, 0.0, do)) * inv
                doh = doh.astype(bf16)
                dp = lax.dot_general(doh, vw, (((1,), (1,)), ((), ())), preferred_element_type=f32)
                ds_ = e * (dp - jnp.sum(e * dp, axis=-1, keepdims=True) * inv)
                dbm_ref[hh] += ds_
                dsb = ds_.astype(bf16)
                dqs.append(jnp.dot(dsb, kw, preferred_element_type=f32))
                dka[pl.ds(r0, KW), :] += lax.dot_general(dsb, qh, (((0,), (0,)), ((), ())), preferred_element_type=f32)
                dva[pl.ds(r0, KW), :] += lax.dot_general(eb, doh, (((0,), (0,)), ((), ())), preferred_element_type=f32)
            dq = jnp.where(head0, dqs[0], dqs[1]) * (HEAD_DIM ** -0.5)
            dz_ref[0, pl.ds(r0, QG), :] = dq.astype(bf16)
            dz_ref[3, pl.ds(r0, QG), :] = (dat_ref[pl.ds(r0, QG), :] * o_ref[pl.ds(r0, QG), :]
                                           * _dsilu(ag, _sig(ag))).astype(bf16)
            return carry

        lax.fori_loop(0, SEQ // QG, group, 0, unroll=8)

        def flush(g, carry):
            r0 = pl.multiple_of(g * QG, QG)
            dz_ref[1, pl.ds(r0, QG), :] = dka[pl.ds(r0 + KEY_PAD, QG), :].astype(bf16)
            dz_ref[2, pl.ds(r0, QG), :] = dva[pl.ds(r0 + KEY_PAD, QG), :].astype(bf16)
            return carry

        lax.fori_loop(0, SEQ // QG, flush, 0)

    def zs(col):
        return pl.BlockSpec((SEQ, W2), lambda hp, b: (b, col // W2 + hp))

    pair = pl.BlockSpec((SEQ, W2), lambda hp, b: (b, hp))
    return pl.pallas_call(
        body, grid=(WC // W2, BL),
        in_specs=[zs(COL_Q), zs(COL_K), zs(COL_V), zs(COL_AG), pair, pair,
                  pl.BlockSpec((None, SEQ, 2 * KW), lambda hp, b: (hp, b, 0)), ANY],
        out_specs=[pl.BlockSpec((4, SEQ, W2), lambda hp, b: (DZ_ATTN // 4, b, hp)),
                   pl.BlockSpec((2, QG, KW), lambda hp, b: (hp, 0, 0))],
        out_shape=[jax.ShapeDtypeStruct(dz.shape, bf16), jax.ShapeDtypeStruct((8, QG, KW), f32)],
        scratch_shapes=[pltpu.VMEM((SEQ, W2), bf16), pltpu.VMEM((SEQ, W2), bf16),
                        pltpu.VMEM((SEQ + KEY_PAD, W2), bf16), pltpu.VMEM((SEQ + KEY_PAD, W2), bf16),
                        pltpu.VMEM((SEQ + KEY_PAD, W2), f32), pltpu.VMEM((SEQ + KEY_PAD, W2), f32)],
        input_output_aliases={7: 0}, name="attn_bwd",
        compiler_params=_params(("parallel", "arbitrary")))(z, z, z, z, dat, o, ew, dz)


BIAS_TOP = KEY_PAD + MAX_REL + QG - 1


def _bias_matrix(table):
    n = 2 * MAX_REL
    wd = QG + KW
    e = jnp.concatenate([jnp.broadcast_to(table[:, n:], (8, BIAS_TOP - n + 1)), table[:, n - 1:BIAS_TOP - wd + 1:-1],
                         jnp.zeros((8, 1), f32)], axis=1)
    flat = jnp.broadcast_to(e[:, None, :], (8, QG, wd)).reshape(8, QG * wd)
    skew = flat[:, :QG * (wd - 1)].reshape(8, QG, wd - 1)
    vals = skew[:, :, QG - 1:QG - 1 + KW]
    r = np.arange(QG)[:, None] // CHUNK
    j = np.arange(KW)[None, :]
    band = (j // CHUNK >= r) & (j // CHUNK <= r + LEFT_CHUNKS)
    keep = np.stack([band & (j >= KEY_PAD - v * QG) for v in range(BIAS_VARIANTS)])
    return jnp.where(jnp.asarray(keep)[:, None], vals[None], NEG_INF)


def _bias_fold(dbm):
    wd = QG + KW
    placed = jnp.pad(dbm, ((0, 0), (0, 0), (QG - 1, 0))).reshape(8, QG * (wd - 1))
    return jnp.pad(placed, ((0, 0), (0, QG))).reshape(8, QG, wd)


def _bias_colsum(folded):
    width = folded.shape[2]

    def body(x_ref, o_ref):
        for h in range(8):
            o_ref[pl.ds(h, 1), :] = _colsum(x_ref[h])

    return pl.pallas_call(body, out_shape=jax.ShapeDtypeStruct((8, width), f32), name="bias_colsum",
                          compiler_params=_params())(folded)


def _bias_table_grad(colsum):
    n = 2 * MAX_REL
    wd = QG + KW
    clipped = jnp.sum(colsum[:, :BIAS_TOP - n + 1], axis=1, keepdims=True)
    return jnp.concatenate([jnp.zeros((8, BIAS_TOP - wd + 2), f32), colsum[:, wd - 2:BIAS_TOP - n:-1], clipped], axis=1)


GATE_SPAN = 3 * WC
TAIL_ROWS = 3 * WC + D


def _gate_specs(tm):
    return [pl.BlockSpec((tm, GATE_SPAN), lambda i: (i, COL_GM // GATE_SPAN)),
            pl.BlockSpec((tm, GATE_SPAN), lambda i: (i, COL_GM // GATE_SPAN + 1))]


def _gate_block(ga_ref, gb_ref, branch, half):
    k = 2 * branch + half
    ref, k = (ga_ref, k) if k < 3 else (gb_ref, k - 3)
    return _sig(ref[:, k * WC:(k + 1) * WC])


def _resident(shape):
    return pl.BlockSpec(shape, lambda i: (0,) * len(shape), pipeline_mode=pl.Buffered(1))


def _tail_fwd(z, acts, x2, lw, next_g=None, tgt=None):
    T = z.shape[0]
    tm = 256
    with_loss = tgt is not None
    assert with_loss != (next_g is not None)

    def body(cv_ref, at_ref, pv_ref, ga_ref, gb_ref, x_ref, wc_ref, wa_ref, wp_ref, wo_ref, g_ref, *rest):
        out_ref, merged_ref, y_ref = rest[1:4]
        ys = [jnp.dot(a[...], w[...], preferred_element_type=f32)
              for a, w in ((cv_ref, wc_ref), (at_ref, wa_ref), (pv_ref, wp_ref))]
        halves = []
        for half in range(2):
            cols = slice(half * WC, (half + 1) * WC)
            halves.append(sum(_gate_block(ga_ref, gb_ref, br, half) * ys[br][:, cols] for br in range(3)))
        merged = jnp.concatenate(halves, axis=1).astype(bf16)
        merged_ref[...] = merged
        y = jnp.dot(merged, wo_ref[...], preferred_element_type=f32)
        y_ref[...] = y
        r = lax.rsqrt(jnp.mean(y * y, axis=-1, keepdims=True) + EPS)
        out = x_ref[...] + (y * r) * g_ref[...]
        if with_loss:
            sq_ref = rest[4]
            e = out - rest[0][...]
            out_ref[...] = e / float(D)

            @pl.when(pl.program_id(0) == 0)
            def _():
                sq_ref[...] = jnp.zeros_like(sq_ref)

            sq_ref[...] += _colsum(e * e)
        else:
            out_ref[...] = out
            rn = lax.rsqrt(jnp.mean(out * out, axis=-1, keepdims=True) + EPS)
            h = (out * rn) * rest[0][...]
            rest[4][...] = h.astype(bf16)
            rest[5][...] = h.T.astype(bf16)

    act = pl.BlockSpec((tm, WC), lambda i: (i, 0))
    row = pl.BlockSpec((tm, D), lambda i: (i, 0))
    vec = pl.BlockSpec((1, D), lambda i: (0, 0))
    if with_loss:
        last_in, last_specs, last_shapes = tgt, [row, [vec]], [jax.ShapeDtypeStruct((1, D), f32)]
    else:
        last_in, last_specs = next_g, [vec, [row, pl.BlockSpec((D, tm), lambda i: (0, i))]]
        last_shapes = [jax.ShapeDtypeStruct((T, D), bf16), jax.ShapeDtypeStruct((D, T), bf16)]
    return pl.pallas_call(
        body, grid=(T // tm,),
        in_specs=[act, act, act] + _gate_specs(tm) + [row, _resident((WC, D)), _resident((WC, D)), _resident((WC, D)),
                                                      _resident((D, D)), _resident((1, D)), last_specs[0]],
        out_specs=[row, row, row] + last_specs[1],
        out_shape=[jax.ShapeDtypeStruct((T, D), f32), jax.ShapeDtypeStruct((T, D), bf16), jax.ShapeDtypeStruct((T, D), f32)]
        + last_shapes,
        name="tail_fwd", compiler_params=_params(("arbitrary",)))(
            *acts, z, z, x2, lw["w_conv_out"], lw["w_attn_out"], lw["w_pool_out"], lw["w_out"], lw["post_g"], last_in)


def _tail_bwd(z, dout, y, merged, acts, lw, after):
    T = z.shape[0]
    tm = 256
    nt = (((1,), (1,)), ((), ()))
    tn = (((0,), (0,)), ((), ()))

    def body(d_ref, y_ref, m_ref, cv_ref, at_ref, pv_ref, ga_ref, gb_ref, wc_ref, wa_ref, wp_ref, wo_ref, g_ref, after_ref,
             dz_ref, dcv_ref, dat_ref, dpv_ref, dg_ref, dw_ref):
        @pl.when(pl.program_id(0) == 0)
        def _():
            dg_ref[...] = jnp.zeros_like(dg_ref)
            dw_ref[...] = jnp.zeros_like(dw_ref)

        y = y_ref[...]
        d = d_ref[...]
        r = lax.rsqrt(jnp.mean(y * y, axis=-1, keepdims=True) + EPS)
        yn = y * r
        dyn = d * g_ref[...]
        dy = (r * (dyn - yn * jnp.mean(dyn * yn, axis=-1, keepdims=True))).astype(bf16)
        dg_ref[...] += _colsum(d * yn)
        dw_ref[pl.ds(3 * WC, D), :] += lax.dot_general(dy, m_ref[...], tn, preferred_element_type=f32)
        dmerged = lax.dot_general(dy, wo_ref[...], nt, preferred_element_type=f32)
        for br, (a_ref, w_ref, da_ref) in enumerate(((cv_ref, wc_ref, dcv_ref), (at_ref, wa_ref, dat_ref),
                                                     (pv_ref, wp_ref, dpv_ref))):
            yb = jnp.dot(a_ref[...], w_ref[...], preferred_element_type=f32)
            halves = []
            for half in range(2):
                cols = slice(half * WC, (half + 1) * WC)
                s = _gate_block(ga_ref, gb_ref, br, half)
                dm = dmerged[:, cols]
                halves.append((dm * s).astype(bf16))
                dz_ref[2 * br + half] = (dm * yb[:, cols] * s * (1.0 - s)).astype(bf16)
            dyb = jnp.concatenate(halves, axis=1)
            da_ref[...] = lax.dot_general(dyb, w_ref[...], nt, preferred_element_type=f32)
            dw_ref[pl.ds(br * WC, WC), :] += lax.dot_general(a_ref[...], dyb, tn, preferred_element_type=f32)

    act = pl.BlockSpec((tm, WC), lambda i: (i, 0))
    row = pl.BlockSpec((tm, D), lambda i: (i, 0))

    def whole(shape):
        return pl.BlockSpec(shape, lambda i: (0, 0))

    return pl.pallas_call(
        body, grid=(T // tm,),
        in_specs=[row, row, row, act, act, act] + _gate_specs(tm) + [_resident((WC, D)), _resident((WC, D)), _resident((WC, D)),
                                                                     _resident((D, D)), _resident((1, D)), ANY],
        out_specs=[pl.BlockSpec((6, tm, WC), lambda i: (DZ_GM // 6, i, 0)), act, act, act, whole((1, D)),
                   whole((TAIL_ROWS, D))],
        out_shape=[jax.ShapeDtypeStruct((DZ_BLOCKS, T, WC), bf16)] + [jax.ShapeDtypeStruct((T, WC), f32)] * 3
        + [jax.ShapeDtypeStruct((1, D), f32), jax.ShapeDtypeStruct((TAIL_ROWS, D), f32)],
        name="tail_bwd", compiler_params=_params(("arbitrary",)))(
            dout, y, merged, *acts, z, z, lw["w_conv_out"], lw["w_attn_out"], lw["w_pool_out"], lw["w_out"], lw["post_g"],
            after)


def _adamw(name, g, w, m, v):
    R, C = w.shape
    tr = R
    for cand in (512, 256, 248, 128, 64, 32, 16, 8):
        if R % cand == 0 and cand * C * 4 <= 2 * 1024 * 1024:
            tr = cand
            break
    c1 = 1.0 - ADAM_B1
    c2 = 1.0 - ADAM_B2
    bc1 = 1.0 - ADAM_B1 ** ADAM_STEP
    bc2 = 1.0 - ADAM_B2 ** ADAM_STEP

    def body(g_ref, w_ref, m_ref, v_ref, d_ref, nm_ref, nv_ref):
        g_ = g_ref[...]
        nm = ADAM_B1 * m_ref[...] + c1 * g_
        nv = ADAM_B2 * v_ref[...] + c2 * (g_ * g_)
        nm_ref[...] = nm
        nv_ref[...] = nv
        d_ref[...] = -ADAM_LR * ((nm / bc1) / (jnp.sqrt(nv / bc2) + ADAM_EPS) + ADAM_WD * w_ref[...])

    spec = pl.BlockSpec((tr, C), lambda i: (i, 0))
    return pl.pallas_call(
        body, grid=(R // tr,), in_specs=[spec] * 4, out_specs=[spec] * 3,
        out_shape=[jax.ShapeDtypeStruct((R, C), f32)] * 3, name=name,
        compiler_params=_params(("parallel",)))(g, w, m, v)


def _sum_slots(name, parts):
    _, R, C = parts.shape
    tr = R
    for cand in (256, 128, 64, 32, 16, 8):
        if R % cand == 0 and cand * C * 4 * N_DEV <= 8 * 1024 * 1024:
            tr = cand
            break

    def body(p_ref, o_ref):
        acc = p_ref[0].astype(f32)
        for s in range(1, N_DEV):
            acc = acc + p_ref[s].astype(f32)
        o_ref[...] = acc

    return pl.pallas_call(
        body, grid=(R // tr,), in_specs=[pl.BlockSpec((N_DEV, tr, C), lambda i: (0, i, 0))],
        out_specs=pl.BlockSpec((tr, C), lambda i: (i, 0)), out_shape=jax.ShapeDtypeStruct((R, C), f32),
        name=name, compiler_params=_params(("parallel",)))(parts)


def _row_tile(rows, row_bytes, budget):
    for cand in (512, 256, 128, 64, 32, 16):
        if rows % cand == 0 and cand * row_bytes <= budget:
            return cand
    return rows


def _pair_sum(core, g, theirs):
    R2, C4 = theirs.shape
    tr = _row_tile(R2, C4 * 2, 2 * 1024 * 1024)
    nb = R2 // tr

    def body(core_ref, g_ref, t_ref, o_ref):
        o_ref[...] = (g_ref[...].astype(f32) + t_ref[...].astype(f32)).astype(bf16)

    return pl.pallas_call(
        body,
        grid_spec=pltpu.PrefetchScalarGridSpec(
            num_scalar_prefetch=1, grid=(nb,),
            in_specs=[pl.BlockSpec((tr, C4), lambda i, core_ref: (core_ref[0] * nb + i, 0)),
                      pl.BlockSpec((tr, C4), lambda i, core_ref: (i, 0))],
            out_specs=pl.BlockSpec((tr, C4), lambda i, core_ref: (i, 0))),
        out_shape=jax.ShapeDtypeStruct((R2, C4), bf16), name="pair_sum",
        compiler_params=_params(("parallel",)))(core, g, theirs)


def _chip_sum(chip, mine, others):
    _, R2, C = others.shape
    tr = _row_tile(R2, C * 4, 1024 * 1024)

    def body(chip_ref, m_ref, o_ref, out_ref):
        acc = m_ref[...].astype(f32)
        for s in range(N_CHIPS - 1):
            acc = acc + o_ref[s].astype(f32)
        out_ref[...] = acc

    return pl.pallas_call(
        body,
        grid_spec=pltpu.PrefetchScalarGridSpec(
            num_scalar_prefetch=1, grid=(R2 // tr,),
            in_specs=[pl.BlockSpec((tr, C), lambda i, chip_ref: (i, chip_ref[0])),
                      pl.BlockSpec((N_CHIPS - 1, tr, C), lambda i, chip_ref: (0, i, 0))],
            out_specs=pl.BlockSpec((tr, C), lambda i, chip_ref: (i, 0))),
        out_shape=jax.ShapeDtypeStruct((R2, C), f32), name="chip_sum",
        compiler_params=_params(("parallel",)))(chip, mine, others)


def _place():
    x, y, c = lax.axis_index("x"), lax.axis_index("y"), lax.axis_index("c")
    return x, y, c


def _flip(v, bit):
    return 1 - v if bit else v


CHIP_FLIPS = ((1, 0), (0, 1), (1, 1))


class _Sems:
    def __init__(self, send, recv):
        self.send, self.recv = send, recv
        self.pairs = 0

    def pair(self):
        k = self.pairs
        self.pairs += 1
        return self.send.at[k], self.recv.at[k]


def _remote(src, dst, lands, sems, to):
    s, r = sems.pair()
    copy = pltpu.make_async_remote_copy(src_ref=src, dst_ref=dst, send_sem=s, recv_sem=r, device_id=to, device_id_type=MESH)
    wait = pltpu.make_async_remote_copy(src_ref=lands, dst_ref=lands, send_sem=s, recv_sem=r, device_id=to, device_id_type=MESH)
    return copy, wait


def _exchange(name, build, srcs, lands, n_remote):
    n_s, n_l = len(srcs), len(lands)

    def body(*refs):
        send, recv = refs[n_s + 2 * n_l:]
        remotes, recvs = build(refs[:n_s], refs[n_s + n_l:n_s + 2 * n_l], _Sems(send, recv))
        for cp in remotes:
            cp.start()
        for rv in recvs:
            rv.wait_recv()
        for cp in remotes:
            cp.wait_send()

    return pl.pallas_call(
        body, in_specs=[ANY] * (n_s + n_l), out_specs=[ANY] * n_l,
        out_shape=[jax.ShapeDtypeStruct(t.shape, t.dtype) for t in lands],
        scratch_shapes=[pltpu.SemaphoreType.DMA((n_remote,)), pltpu.SemaphoreType.DMA((n_remote,))],
        input_output_aliases={n_s + i: i for i in range(n_l)}, name=name)(*srcs, *lands)


HBM = pl.BlockSpec(memory_space=pltpu.HBM)
SEMS = pl.BlockSpec(memory_space=pltpu.SEMAPHORE)
DATAFLOW = pltpu.SideEffectType.DATAFLOW_SIDE_EFFECTING


def _start(name, build, srcs, lands, n_remote, after):
    n_s, n_l = len(srcs), len(lands)

    def body(*refs):
        send, recv = refs[n_s + n_l + 1], refs[n_s + n_l + 2]
        remotes, _ = build(refs[:n_s], refs[n_s:n_s + n_l], _Sems(send, recv))
        for cp in remotes:
            cp.start()
        refs[-1][...] = jnp.zeros((8, 128), f32)

    arrays = [pltpu.with_memory_space_constraint(a, pltpu.HBM) for a in (*srcs, *lands)]
    out = pl.pallas_call(
        body, name=name, in_specs=[HBM] * (n_s + n_l) + [ANY],
        out_specs=(SEMS, SEMS, *[HBM] * (n_s + n_l), pl.BlockSpec(memory_space=pltpu.VMEM)),
        out_shape=(pltpu.SemaphoreType.DMA((n_remote,)), pltpu.SemaphoreType.DMA((n_remote,)),
                   *[pltpu.HBM(a.shape, a.dtype) for a in arrays], jax.ShapeDtypeStruct((8, 128), f32)),
        input_output_aliases={i: 2 + i for i in range(n_s + n_l)},
        compiler_params=pltpu.CompilerParams(has_side_effects=DATAFLOW))(*arrays, after)
    return dict(name=name, build=build, sems=out[:2], srcs=out[2:2 + n_s], lands=out[2 + n_s:2 + n_s + n_l], token=out[-1])


def _wait(started, after):
    srcs, lands, build = started["srcs"], started["lands"], started["build"]
    n_s, n_l = len(srcs), len(lands)
    after = list(after) if isinstance(after, (list, tuple)) else [after]

    def body(*refs):
        send, recv = refs[n_s + n_l], refs[n_s + n_l + 1]
        remotes, recvs = build(refs[:n_s], refs[n_s:n_s + n_l], _Sems(send, recv))
        for rv in recvs:
            rv.wait_recv()
        for cp in remotes:
            cp.wait_send()

    out = pl.pallas_call(
        body, name=started["name"] + "_wait", in_specs=[HBM] * (n_s + n_l) + [SEMS, SEMS] + [ANY] * len(after),
        out_specs=[HBM] * (n_s + n_l), out_shape=[pltpu.HBM(a.shape, a.dtype) for a in (*srcs, *lands)],
        input_output_aliases={i: i for i in range(n_s + n_l)},
        compiler_params=pltpu.CompilerParams(has_side_effects=DATAFLOW))(*srcs, *lands, *started["sems"], *after)
    return out[:n_s], out[n_s:]


def _gather_plans(n_split, n_all):
    def over_ici(src, land, sems):
        x, y, c = _place()
        chip = 2 * x + y
        remotes, recvs = [], []
        for a in range(n_all):
            for fx, fy in CHIP_FLIPS:
                px, py = _flip(x, fx), _flip(y, fy)
                if a < n_split:
                    r2 = src[a].shape[0] // 2
                    rows = pl.ds(c * r2, r2)
                    cp, rv = _remote(src[a].at[rows], land[a].at[chip, rows], land[a].at[2 * px + py, rows], sems, (px, py, c))
                else:
                    cp, rv = _remote(src[a], land[a].at[chip], land[a].at[2 * px + py], sems, (px, py, c))
                remotes.append(cp)
                recvs.append(rv)
        return remotes, recvs

    def over_d2d(src, land, sems):
        x, y, c = _place()
        remotes, recvs = [], []
        for a in range(n_split):
            r2 = land[a].shape[1] // 2
            for fx, fy in CHIP_FLIPS:
                owner = 2 * _flip(x, fx) + _flip(y, fy)
                mine = land[a].at[owner, pl.ds(c * r2, r2)]
                cp, rv = _remote(mine, mine, land[a].at[owner, pl.ds((1 - c) * r2, r2)], sems, (x, y, 1 - c))
                remotes.append(cp)
                recvs.append(rv)
        return remotes, recvs

    return over_ici, over_d2d


def _gather_begin(tag, shards, n_split, after):
    over_ici, _ = _gather_plans(n_split, len(shards))
    lands = [lax.empty((N_CHIPS,) + s.shape, s.dtype) for s in shards]
    return _start("gather_ici_" + tag, over_ici, shards, lands, 3 * len(shards), after)


def _gather_end(started, shards, n_split, after):
    _, over_d2d = _gather_plans(n_split, len(shards))
    lands = _exchange("gather_d2d", over_d2d, [], _wait(started, after)[1], 3 * n_split)
    chip = 2 * lax.axis_index("x") + lax.axis_index("y")
    return [lax.dynamic_update_slice_in_dim(g, s[None], chip, axis=0) for g, s in zip(lands, shards)]


def _reduce_plans(n):
    def to_sibling(src, land, sems):
        x, y, c = _place()
        remotes, recvs = [], []
        for a in range(n):
            r2 = src[a].shape[0] // 2
            cp, rv = _remote(src[a].at[pl.ds((1 - c) * r2, r2), :], land[a], land[a], sems, (x, y, 1 - c))
            remotes.append(cp)
            recvs.append(rv)
        return remotes, recvs

    def across_chips(src, land, sems):
        x, y, c = _place()
        remotes, recvs = [], []
        for a in range(n):
            cw = src[a].shape[1] // N_CHIPS
            for k, (fx, fy) in enumerate(CHIP_FLIPS):
                px, py = _flip(x, fx), _flip(y, fy)
                cp, rv = _remote(src[a].at[:, pl.ds((2 * px + py) * cw, cw)], land[a].at[k], land[a].at[k], sems, (px, py, c))
                remotes.append(cp)
                recvs.append(rv)
        return remotes, recvs

    def share(src, land, sems):
        x, y, c = _place()
        remotes, recvs = [], []
        for a in range(n):
            cp, rv = _remote(src[a], land[a], land[a], sems, (x, y, 1 - c))
            remotes.append(cp)
            recvs.append(rv)
        return remotes, recvs

    return to_sibling, across_chips, share


def _reduce_pair_begin(grads):
    to_sibling, _, _ = _reduce_plans(len(grads))
    lands = [lax.empty((g.shape[0] // 2, g.shape[1]), bf16) for g in grads]
    return _start("reduce_pair", to_sibling, grads, lands, len(grads), grads[0])


def _reduce_chips_begin(started, after):
    grads, theirs = _wait(started, after)
    n = len(grads)
    _, across_chips, _ = _reduce_plans(n)
    core = lax.axis_index("c").reshape(1).astype(jnp.int32)
    pair = [_pair_sum(core, g, t) for g, t in zip(grads, theirs)]
    lands = [lax.empty((N_CHIPS - 1, g.shape[0] // 2, g.shape[1] // N_CHIPS), bf16) for g in grads]
    return _start("reduce_chips", across_chips, pair, lands, 3 * n, pair[0])


def _reduce_end(started, after):
    x, y, c = _place()
    chip = (2 * x + y).reshape(1).astype(jnp.int32)
    pair, others = _wait(started, after)
    _, _, share = _reduce_plans(len(pair))
    mine = [_chip_sum(chip, p, o) for p, o in zip(pair, others)]
    sibs = _exchange("reduce_share", share, mine, [lax.empty(h.shape, f32) for h in mine], len(mine))
    return [jnp.where(c == 0, jnp.concatenate([h, s], axis=0), jnp.concatenate([s, h], axis=0))
            for h, s in zip(mine, sibs)]


def _to_all(src, land, sems):
    x, y, c = _place()
    me = 4 * x + 2 * y + c
    remotes, recvs = [], []
    for k in range(1, N_DEV):
        px, py, pc = _flip(x, (k >> 2) & 1), _flip(y, (k >> 1) & 1), _flip(c, k & 1)
        cp, rv = _remote(src[0], land[0].at[me], land[0].at[4 * px + 2 * py + pc], sems, (px, py, pc))
        remotes.append(cp)
        recvs.append(rv)
    return remotes, recvs


def _gather_small_begin(packed):
    return _start("gather_small", _to_all, [packed], [lax.empty((N_DEV,) + packed.shape, f32)], N_DEV - 1, packed)


def _gather_small_end(started, after):
    (packed,), (others,) = _wait(started, after)
    x, y, c = _place()
    return lax.dynamic_update_slice_in_dim(others, packed[None], 4 * x + 2 * y + c, axis=0)


def _gather_all(packed):
    others = _exchange("gather_all", _to_all, [packed], [lax.empty((N_DEV,) + packed.shape, f32)], N_DEV - 1)[0]
    x, y, c = _place()
    return lax.dynamic_update_slice_in_dim(others, packed[None], 4 * x + 2 * y + c, axis=0)


def _rows8(v):
    return jnp.pad(v[None, :], ((0, 7), (0, 0)))


def _vec_rows(vs):
    return jnp.pad(jnp.stack(vs), ((0, 8 - len(vs)), (0, 0)))


SMALL_ROWS = 224


def _pack_small(conv_vec, conv_dw, pool_vec, pool_w, pre_g, post_g, rel):
    return jnp.concatenate([
        conv_vec, conv_dw, pool_vec, pool_w.reshape(GD, WC),
        _rows8(pre_g).reshape(16, WC), _rows8(post_g).reshape(16, WC),
        jnp.pad(rel, ((0, 0), (0, D - rel.shape[1]))).reshape(16, WC)], axis=0)


def _unpack_small(p):
    conv_vec, pool_vec = p[0:8], p[40:48]
    return dict(
        conv_dw_b=conv_vec[0], conv_ln_g=conv_vec[1], conv_ln_b=conv_vec[2], conv_dw=p[8:8 + CONV_K],
        pool_b=pool_vec[0].reshape(4, GD), pool_scale=pool_vec[1], pool_w=p[48:176].reshape(4, GD, GD),
        pre_norm_g=p[176:192].reshape(8, D)[0], post_norm_g=p[192:208].reshape(8, D)[0],
        rel_bias=p[208:224].reshape(8, D)[:, :2 * MAX_REL + 1])


def _layer_fwd(x2, ht, z, lw, BL, SEQ, next_g=None, tgt=None):
    cv, u1 = _conv_fwd(z, lw["dw32"], lw["cvec"], BL, SEQ)
    at, attn_o, attn_e = _attn_fwd(z, lw["bm"], BL, SEQ)
    pv = _pool_fwd(z, lw["pw"], lw["pvec"], BL, SEQ)
    out, merged, y, *last = _tail_fwd(z, (cv, at, pv), x2, lw, next_g, tgt)
    saved = dict(x=x2, ht=ht, z=z, u1=u1, attn_o=attn_o, attn_e=attn_e, acts=(cv, at, pv), merged=merged, y=y)
    return (out, *last), saved


def _layer_bwd(dout, sv, lw, BL, SEQ, after, meanwhile=None):
    tail = _tail_bwd(sv["z"], dout, sv["y"], sv["merged"], sv["acts"], lw, after)
    dz, dacts, dpost = tail[0], tail[1:4], tail[4]
    dw_tail = tail[5].astype(bf16)
    if meanwhile is not None:
        meanwhile(dw_tail)
    dz, ddw, dcvec = _conv_bwd(sv["z"], sv["u1"], dacts[0], dz, lw["dw32"], lw["cvec"], BL, SEQ)
    dz, dbm = _attn_bwd(sv["z"], dacts[1], sv["attn_o"], sv["attn_e"], dz, BL, SEQ)
    dz, dpw, dpvec = _pool_bwd(sv["z"], dacts[2], dz, lw["pw"], lw["pvec"], BL, SEQ)
    drel = _bias_table_grad(_bias_colsum(_bias_fold(dbm)))
    small_gather = _gather_small_begin(_pack_small(dcvec, ddw, dpvec, dpw, jnp.zeros((D,), f32), dpost[0], drel))
    dw_in = _mm_dw_in(sv["ht"], dz, small_gather["token"])
    pair = _reduce_pair_begin([dw_in, dw_tail])
    if meanwhile is None:
        dx, dpre = _mm_dx(dz, lw["w_in"], sv["x"], lw["pre_g"], dout, pair["token"])
        reduction = _reduce_chips_begin(pair, dx)
    else:
        reduction = _reduce_chips_begin(pair, pair["token"])
        dx, dpre = _mm_dx(dz, lw["w_in"], sv["x"], lw["pre_g"], dout, reduction["token"])
    return dx, reduction, small_gather, dpre


BIG = ("w_in", "w_conv_out", "w_attn_out", "w_pool_out", "w_out")
PRE_ROWS = slice(176, 192)


def _layer_shards(w, l):
    return [w[k][l].astype(bf16) for k in BIG] + [w["conv_dw"][l]]


def _side_by_side(g):
    return jnp.transpose(g, (1, 0, 2)).reshape(g.shape[1], N_CHIPS * g.shape[2])


def _layer_weights(w_in, gathered, w, l, bm):
    lw = {k: _side_by_side(g) for k, g in zip(BIG[1:4], gathered[:3])}
    lw["w_in"] = w_in
    lw["w_out"] = gathered[3].reshape(D, D)
    lw["pre_g"] = w["pre_norm_g"][l][None]
    lw["post_g"] = w["post_norm_g"][l][None]
    lw["dw32"] = jnp.pad(_side_by_side(gathered[4]), ((0, 32 - CONV_K), (0, 0)))
    lw["cvec"] = _vec_rows([w["conv_dw_b"][l], w["conv_ln_g"][l], w["conv_ln_b"][l]])
    lw["bm"] = bm
    lw["pw"] = w["pool_w"][l].astype(bf16)
    lw["pvec"] = _vec_rows([w["pool_b"][l].reshape(WC), w["pool_scale"][l]])
    return lw


SMALL = ("pre_norm_g", "post_norm_g", "conv_dw_b", "conv_ln_g", "conv_ln_b", "rel_bias", "pool_w", "pool_b", "pool_scale")
ORDER = ("pre_norm_g", "post_norm_g", "w_in", "conv_dw", "conv_dw_b", "conv_ln_g", "conv_ln_b", "w_conv_out",
         "rel_bias", "w_attn_out", "pool_w", "pool_b", "pool_scale", "w_pool_out", "w_out")


def _pack_small_params(p):
    return jnp.concatenate([
        _pack_small(_vec_rows([p["conv_dw_b"][l], p["conv_ln_g"][l], p["conv_ln_b"][l]]), jnp.zeros((32, WC), f32),
                    _vec_rows([p["pool_b"][l].reshape(WC), p["pool_scale"][l]]), p["pool_w"][l],
                    p["pre_norm_g"][l], p["post_norm_g"][l], p["rel_bias"][l])
        for l in range(DEPTH)], axis=0)


def _unpack_small_params(packed):
    layers = [_unpack_small(packed[l * SMALL_ROWS:(l + 1) * SMALL_ROWS]) for l in range(DEPTH)]
    return {k: jnp.stack([layers[l][k] for l in range(DEPTH)]) for k in layers[0]}


def kernel(x, pre_norm_g, post_norm_g, w_in, conv_dw, conv_dw_b, conv_ln_g, conv_ln_b, w_conv_out, rel_bias, w_attn_out, pool_w, pool_b, pool_scale, w_pool_out, w_out, loss_target, m_pre_norm_g, m_post_norm_g, m_w_in, m_conv_dw, m_conv_dw_b, m_conv_ln_g, m_conv_ln_b, m_w_conv_out, m_rel_bias, m_w_attn_out, m_pool_w, m_pool_b, m_pool_scale, m_w_pool_out, m_w_out, v_pre_norm_g, v_post_norm_g, v_w_in, v_conv_dw, v_conv_dw_b, v_conv_ln_g, v_conv_ln_b, v_w_conv_out, v_rel_bias, v_w_attn_out, v_pool_w, v_pool_b, v_pool_scale, v_w_pool_out, v_w_out):
    BL, SEQ, _ = x.shape
    T = BL * SEQ
    w = dict(pre_norm_g=pre_norm_g, post_norm_g=post_norm_g, w_in=w_in, conv_dw=conv_dw, conv_dw_b=conv_dw_b,
             conv_ln_g=conv_ln_g, conv_ln_b=conv_ln_b, w_conv_out=w_conv_out, rel_bias=rel_bias, w_attn_out=w_attn_out,
             pool_w=pool_w, pool_b=pool_b, pool_scale=pool_scale, w_pool_out=w_pool_out, w_out=w_out)
    m = dict(pre_norm_g=m_pre_norm_g, post_norm_g=m_post_norm_g, w_in=m_w_in, conv_dw=m_conv_dw, conv_dw_b=m_conv_dw_b,
             conv_ln_g=m_conv_ln_g, conv_ln_b=m_conv_ln_b, w_conv_out=m_w_conv_out, rel_bias=m_rel_bias,
             w_attn_out=m_w_attn_out, pool_w=m_pool_w, pool_b=m_pool_b, pool_scale=m_pool_scale,
             w_pool_out=m_w_pool_out, w_out=m_w_out)
    v = dict(pre_norm_g=v_pre_norm_g, post_norm_g=v_post_norm_g, w_in=v_w_in, conv_dw=v_conv_dw, conv_dw_b=v_conv_dw_b,
             conv_ln_g=v_conv_ln_g, conv_ln_b=v_conv_ln_b, w_conv_out=v_w_conv_out, rel_bias=v_rel_bias,
             w_attn_out=v_w_attn_out, pool_w=v_pool_w, pool_b=v_pool_b, pool_scale=v_pool_scale,
             w_pool_out=v_w_pool_out, w_out=v_w_out)

    shards = [_layer_shards(w, l) for l in range(DEPTH)]
    x2 = x.reshape(T, D)
    h0, ht0 = _rms_pre(x2, pre_norm_g[0][None])
    first = _gather_begin("w_in0", shards[0][:1], 1, x2)
    bms = [_bias_matrix(rel_bias[l]) for l in range(DEPTH)]
    packs = [_pack_small_params(p) for p in (w, m, v)]
    w_in0 = _side_by_side(_gather_end(first, shards[0][:1], 1, [ht0, *bms, *packs])[0])
    rest0 = _gather_begin("rest0", shards[0][1:], 4, w_in0)
    all1 = _gather_begin("layer1", shards[1], 5, rest0["token"])
    z0 = _in_proj(h0, w_in0, after=all1["token"])
    lw0 = _layer_weights(w_in0, _gather_end(rest0, shards[0][1:], 4, z0), w, 0, bms[0])
    (out0, h1, ht1), saved0 = _layer_fwd(x2, ht0, z0, lw0, BL, SEQ, next_g=pre_norm_g[1][None])
    gathered1 = _gather_end(all1, shards[1], 5, out0)
    lw1 = _layer_weights(_side_by_side(gathered1[0]), gathered1[1:], w, 1, bms[1])
    z1 = _in_proj(h1, lw1["w_in"], after=h1)
    (dout, sq), saved1 = _layer_fwd(out0, ht1, z1, lw1, BL, SEQ, tgt=loss_target.reshape(T, D))
    loss = lax.psum(0.5 * jnp.sum(sq) / float(D), ("x", "y", "c"))

    summed = [None] * DEPTH
    dx1, reduction1, small_gather1, dpre1 = _layer_bwd(dout, saved1, lw1, BL, SEQ, dout)

    def finish_layer1(after):
        summed[1] = _reduce_end(reduction1, after)

    grad_x, reduction0, small_gather0, dpre0 = _layer_bwd(dx1, saved0, lw0, BL, SEQ, reduction1["token"],
                                                          meanwhile=finish_layer1)
    summed[0] = _reduce_end(reduction0, grad_x)
    dpre = _sum_slots("sum_small", _gather_all(jnp.concatenate([_rows8(dpre0[0]), _rows8(dpre1[0])], axis=0)))
    gsmall = []
    for l, started in enumerate((small_gather0, small_gather1)):
        g = _sum_slots("sum_small", _gather_small_end(started, dpre))
        gsmall += [g[:PRE_ROWS.start], dpre[8 * l:8 * l + 8].reshape(16, WC), g[PRE_ROWS.stop:]]
    gsmall = jnp.concatenate(gsmall, axis=0)

    grads, deltas, new_m, new_v = {}, {}, {}, {}
    for i, k in enumerate(BIG):
        if k == "w_in":
            g = jnp.stack([summed[l][0] for l in range(DEPTH)])
        elif k == "w_out":
            g = jnp.stack([summed[l][1][3 * WC:].T for l in range(DEPTH)])
        else:
            g = jnp.stack([summed[l][1][(i - 1) * WC:i * WC] for l in range(DEPTH)])
        grads[k] = g
        shape = w[k].shape
        flat2 = lambda a: a.reshape(shape[0] * shape[1], shape[2])
        d_, nm_, nv_ = _adamw("adamw_big", flat2(g), flat2(w[k]), flat2(m[k]), flat2(v[k]))
        deltas[k], new_m[k], new_v[k] = d_.reshape(shape), nm_.reshape(shape), nv_.reshape(shape)

    d_, nm_, nv_ = _adamw("adamw_small", gsmall, *packs)
    gs, ds, ms, vs = (_unpack_small_params(a) for a in (gsmall, d_, nm_, nv_))
    for k in SMALL:
        grads[k], deltas[k], new_m[k], new_v[k] = gs[k], ds[k], ms[k], vs[k]
    chip = 2 * lax.axis_index("x") + lax.axis_index("y")
    g_dw = lax.dynamic_slice_in_dim(gs["conv_dw"], chip * GD, GD, axis=2)
    flat2 = lambda a: a.reshape(DEPTH * CONV_K, GD)
    d_, nm_, nv_ = _adamw("adamw_conv_dw", flat2(g_dw), flat2(conv_dw), flat2(m["conv_dw"]), flat2(v["conv_dw"]))
    grads["conv_dw"] = g_dw
    deltas["conv_dw"], new_m["conv_dw"], new_v["conv_dw"] = (a.reshape(conv_dw.shape) for a in (d_, nm_, nv_))

    return (loss, grad_x.reshape(x.shape), *[grads[k] for k in ORDER], *[deltas[k] for k in ORDER],
            *[new_m[k] for k in ORDER], *[new_v[k] for k in ORDER])
```

```python
import numpy as np
import jax
import jax.numpy as jnp
from jax import lax
from jax.experimental import pallas as pl
from jax.experimental.pallas import tpu as pltpu

f32 = jnp.float32
bf16 = jnp.bfloat16

D = 1024
DEPTH = 2
WC = 512
HEAD_DIM = 64
CHUNK = 64
LEFT_CHUNKS = 8
KEY_PAD = LEFT_CHUNKS * CHUNK
MAX_REL = 256
CONV_K = 31
POOL_WINDOWS = (2, 4, 8, 16)
GD = 128
NCOL = 7680
EPS = 1e-6
NEG_INF = -1e30
COL_A, COL_B, COL_CG, COL_Q, COL_K, COL_V, COL_AG, COL_PI, COL_PG, COL_GM = (
    0, 512, 1024, 1536, 2048, 2560, 3072, 3584, 4096, 4608)

ADAM_LR = 0.001
ADAM_B1 = 0.9
ADAM_B2 = 0.999
ADAM_EPS = 1e-08
ADAM_WD = 0.01
ADAM_STEP = 10

QG = 256
KW = KEY_PAD + QG
BIAS_VARIANTS = KEY_PAD // QG + 1
CT = 256
HALO = 32
PHALO = 16
N_CHIPS = 4
N_DEV = 8
VMEM_LIMIT = 58 * 1024 * 1024
MESH = pl.DeviceIdType.MESH
ANY = pl.BlockSpec(memory_space=pl.ANY)

DZ_BLOCKS = 18
DZ_CONV, DZ_ATTN, DZ_POOL, DZ_GM = 0, 4, 8, 12


def _dz_block(c):
    return c + (c >= 3).astype(jnp.int32) + 2 * (c >= 9).astype(jnp.int32)


def _params(sem=None):
    return pltpu.CompilerParams(dimension_semantics=sem, vmem_limit_bytes=VMEM_LIMIT)


def _sig(x):
    return 1.0 / (1.0 + jnp.exp(-x))


def _dsilu(x, s):
    return s * (1.0 + x * (1.0 - s))


def _colsum(x):
    return jnp.sum(x, axis=0, keepdims=True)


def _rms_pre(x2, g):
    T = x2.shape[0]
    tm = 512

    def body(x_ref, g_ref, h_ref, ht_ref):
        x = x_ref[...]
        r = lax.rsqrt(jnp.mean(x * x, axis=-1, keepdims=True) + EPS)
        h = (x * r) * g_ref[...]
        h_ref[...] = h.astype(bf16)
        ht_ref[...] = h.T.astype(bf16)

    row = pl.BlockSpec((tm, D), lambda i: (i, 0))
    vec = pl.BlockSpec((1, D), lambda i: (0, 0))
    return pl.pallas_call(
        body, grid=(T // tm,), in_specs=[row, vec], out_specs=[row, pl.BlockSpec((D, tm), lambda i: (0, i))],
        out_shape=[jax.ShapeDtypeStruct((T, D), bf16), jax.ShapeDtypeStruct((D, T), bf16)], name="rms_pre",
        compiler_params=_params(("parallel",)))(x2, g)


def _in_proj(h, w_in, after):
    T = h.shape[0]
    tm, tn = 512, 1536

    def body(h_ref, w_ref, after_ref, z_ref):
        for n0 in range(0, NCOL, tn):
            z_ref[:, n0:n0 + tn] = jnp.dot(h_ref[...], w_ref[:, n0:n0 + tn], preferred_element_type=f32)

    return pl.pallas_call(
        body, grid=(T // tm,),
        in_specs=[pl.BlockSpec((tm, D), lambda i: (i, 0)),
                  pl.BlockSpec((D, NCOL), lambda i: (0, 0), pipeline_mode=pl.Buffered(1)), ANY],
        out_specs=pl.BlockSpec((tm, NCOL), lambda i: (i, 0)), out_shape=jax.ShapeDtypeStruct((T, NCOL), f32),
        name="mm_in", compiler_params=_params(("parallel",)))(h, w_in, after)


DZ_SPANS = ((DZ_CONV, 3), (DZ_ATTN, 4), (DZ_POOL, 2), (DZ_GM, 6))


def _mm_dx(dz, w_in, x2, g, dout, after):
    T = dz.shape[1]
    tm = 512

    def body(conv_ref, attn_ref, pool_ref, gm_ref, w_ref, x_ref, g_ref, d_ref, after_ref, dx_ref, dg_ref):
        dh = None
        col = 0
        for ref, (_, blocks) in zip((conv_ref, attn_ref, pool_ref, gm_ref), DZ_SPANS):
            for b in range(blocks):
                p = lax.dot_general(ref[b], w_ref[:, col * WC:(col + 1) * WC], (((1,), (1,)), ((), ())),
                                    preferred_element_type=f32)
                dh = p if dh is None else dh + p
                col += 1
        x = x_ref[...]
        r = lax.rsqrt(jnp.mean(x * x, axis=-1, keepdims=True) + EPS)
        xn = x * r
        dxn = dh * g_ref[...]
        dx_ref[...] = r * (dxn - xn * jnp.mean(dxn * xn, axis=-1, keepdims=True)) + d_ref[...]

        @pl.when(pl.program_id(0) == 0)
        def _():
            dg_ref[...] = jnp.zeros_like(dg_ref)

        dg_ref[...] += _colsum(dh * xn)

    spans = [pl.BlockSpec((blocks, tm, WC), lambda i, first=first, blocks=blocks: (first // blocks, i, 0))
             for first, blocks in DZ_SPANS]
    row = pl.BlockSpec((tm, D), lambda i: (i, 0))
    vec = pl.BlockSpec((1, D), lambda i: (0, 0))
    return pl.pallas_call(
        body, grid=(T // tm,),
        in_specs=spans + [pl.BlockSpec((D, NCOL), lambda i: (0, 0), pipeline_mode=pl.Buffered(1)), row, vec, row, ANY],
        out_specs=[row, vec], out_shape=[jax.ShapeDtypeStruct((T, D), f32), jax.ShapeDtypeStruct((1, D), f32)],
        name="mm_dx", compiler_params=_params(("arbitrary",)))(dz, dz, dz, dz, w_in, x2, g, dout, after)


def _mm_dw_in(ht, dz, after):
    T = dz.shape[1]

    def body(ht_ref, dz_ref, after_ref, o_ref):
        o_ref[...] = jnp.dot(ht_ref[...], dz_ref[...], preferred_element_type=f32).astype(bf16)

    return pl.pallas_call(
        body, grid=(NCOL // WC,),
        in_specs=[pl.BlockSpec((D, T), lambda j: (0, 0), pipeline_mode=pl.Buffered(1)),
                  pl.BlockSpec((None, T, WC), lambda j: (_dz_block(j), 0, 0)), ANY],
        out_specs=pl.BlockSpec((D, WC), lambda j: (0, j)), out_shape=jax.ShapeDtypeStruct((D, NCOL), bf16),
        name="mm_dw_in", compiler_params=_params(("parallel",)))(ht, dz, after)


def _conv_delays():
    return [(8 * a + b, a, b) for b in range(8) for a in range(4) if 8 * a + b < CONV_K]


def _conv_rolls(win):
    return [win if b == 0 else pltpu.roll(win, b, axis=0) for b in range(8)]


def _conv_taps(rolled, dw_ref):
    acc = None
    for d, a, b in _conv_delays():
        term = rolled[b][HALO - 8 * a:HALO - 8 * a + CT, :] * dw_ref[pl.ds(CONV_K - 1 - d, 1), :]
        acc = term if acc is None else acc + term
    return acc


def _conv_fwd(z, dw32, cvec, BL, SEQ):
    T = BL * SEQ
    nct = SEQ // CT

    def body(a_ref, b_ref, cg_ref, dw_ref, vec_ref, o_ref, u1_ref, p_ref):
        p_ref[pl.ds(0, HALO), :] = jnp.zeros((HALO, WC), f32)

        def glu(c, carry):
            r0 = pl.multiple_of(c * CT, CT)
            p_ref[pl.ds(r0 + HALO, CT), :] = a_ref[pl.ds(r0, CT), :] * _sig(b_ref[pl.ds(r0, CT), :])
            return carry

        lax.fori_loop(0, nct, glu, 0)

        def step(c, carry):
            r0 = pl.multiple_of(c * CT, CT)
            u1 = _conv_taps(_conv_rolls(p_ref[pl.ds(r0, CT + HALO), :]), dw_ref) + vec_ref[0:1, :]
            u1_ref[pl.ds(r0, CT), :] = u1
            xc = u1 - jnp.mean(u1, axis=-1, keepdims=True)
            rs = lax.rsqrt(jnp.mean(xc * xc, axis=-1, keepdims=True) + EPS)
            u2 = (xc * rs) * vec_ref[1:2, :] + vec_ref[2:3, :]
            cg = cg_ref[pl.ds(r0, CT), :]
            o_ref[pl.ds(r0, CT), :] = ((u2 * _sig(u2)) * (cg * _sig(cg))).astype(bf16)
            return carry

        lax.fori_loop(0, nct, step, 0)

    def zs(col):
        return pl.BlockSpec((SEQ, WC), lambda b: (b, col // WC))

    seq = pl.BlockSpec((SEQ, WC), lambda b: (b, 0))
    return pl.pallas_call(
        body, grid=(BL,),
        in_specs=[zs(COL_A), zs(COL_B), zs(COL_CG), pl.BlockSpec((32, WC), lambda b: (0, 0)),
                  pl.BlockSpec((8, WC), lambda b: (0, 0))],
        out_specs=[seq, seq],
        out_shape=[jax.ShapeDtypeStruct((T, WC), bf16), jax.ShapeDtypeStruct((T, WC), f32)],
        scratch_shapes=[pltpu.VMEM((SEQ + HALO, WC), f32)], name="conv_fwd",
        compiler_params=_params(("parallel",)))(z, z, z, dw32, cvec)


def _conv_bwd(z, u1, dcv, dz, dw32, cvec, BL, SEQ):
    nct = SEQ // CT

    def body(a_ref, b_ref, cg_ref, u1_ref, dcv_ref, dzin_ref, dw_ref, vec_ref, dz_ref, ddw_ref, dvec_ref,
             p_ref, q_ref, taps_ref):
        @pl.when(pl.program_id(0) == 0)
        def _():
            ddw_ref[...] = jnp.zeros_like(ddw_ref)
            dvec_ref[...] = jnp.zeros_like(dvec_ref)

        p_ref[pl.ds(0, HALO), :] = jnp.zeros((HALO, WC), f32)
        q_ref[pl.ds(SEQ, HALO), :] = jnp.zeros((HALO, WC), f32)

        def glu(c, carry):
            r0 = pl.multiple_of(c * CT, CT)
            p_ref[pl.ds(r0 + HALO, CT), :] = a_ref[pl.ds(r0, CT), :] * _sig(b_ref[pl.ds(r0, CT), :])
            return carry

        lax.fori_loop(0, nct, glu, 0)

        def step(c, carry):
            r0 = pl.multiple_of(c * CT, CT)
            rolled = _conv_rolls(p_ref[pl.ds(r0, CT + HALO), :])
            u1 = u1_ref[pl.ds(r0, CT), :]
            xc = u1 - jnp.mean(u1, axis=-1, keepdims=True)
            rs = lax.rsqrt(jnp.mean(xc * xc, axis=-1, keepdims=True) + EPS)
            nrm = xc * rs
            u2 = nrm * vec_ref[1:2, :] + vec_ref[2:3, :]
            s2 = _sig(u2)
            u3 = u2 * s2
            cg = cg_ref[pl.ds(r0, CT), :]
            scg = _sig(cg)
            dcv_ = dcv_ref[pl.ds(r0, CT), :]
            dz_ref[2, pl.ds(r0, CT), :] = (dcv_ * u3 * _dsilu(cg, scg)).astype(bf16)
            du2 = dcv_ * (cg * scg) * _dsilu(u2, s2)
            dvec_ref[1:2, :] += _colsum(du2 * nrm)
            dvec_ref[2:3, :] += _colsum(du2)
            dn = du2 * vec_ref[1:2, :]
            du1 = rs * (dn - jnp.mean(dn, axis=-1, keepdims=True)
                        - nrm * jnp.mean(dn * nrm, axis=-1, keepdims=True))
            dvec_ref[0:1, :] += _colsum(du1)
            q_ref[pl.ds(r0, CT), :] = du1
            for d, a, b in _conv_delays():
                prod = du1 * rolled[b][HALO - 8 * a:HALO - 8 * a + CT, :]
                taps_ref[CONV_K - 1 - d] += jnp.sum(prod.reshape(CT // 8, 8, WC), axis=0)
            return carry

        taps_ref[...] = jnp.zeros_like(taps_ref)
        lax.fori_loop(0, nct, step, 0)
        for row in range(CONV_K):
            ddw_ref[pl.ds(row, 1), :] += _colsum(taps_ref[row])

        def back(c, carry):
            r0 = pl.multiple_of(c * CT, CT)
            wq = q_ref[pl.ds(r0, CT + HALO), :]
            up = {}
            acc = None
            for d, a, b in _conv_delays():
                if b not in up:
                    up[b] = wq if b == 0 else pltpu.roll(wq, CT + HALO - b, axis=0)
                term = up[b][8 * a:8 * a + CT, :] * dw_ref[pl.ds(CONV_K - 1 - d, 1), :]
                acc = term if acc is None else acc + term
            a_ = a_ref[pl.ds(r0, CT), :]
            sb = _sig(b_ref[pl.ds(r0, CT), :])
            dz_ref[0, pl.ds(r0, CT), :] = (acc * sb).astype(bf16)
            dz_ref[1, pl.ds(r0, CT), :] = (acc * a_ * sb * (1.0 - sb)).astype(bf16)
            return carry

        lax.fori_loop(0, nct, back, 0)

    def const(r):
        return pl.BlockSpec((r, WC), lambda b: (0, 0))

    first = [pl.BlockSpec((SEQ, WC), lambda b, col=col: (b, col // WC)) for col in (COL_A, COL_B)]
    seq = pl.BlockSpec((SEQ, WC), lambda b: (b, 0), pipeline_mode=pl.Buffered(1))
    return pl.pallas_call(
        body, grid=(BL,),
        in_specs=first + [pl.BlockSpec((SEQ, WC), lambda b: (b, COL_CG // WC), pipeline_mode=pl.Buffered(1)),
                          seq, seq, ANY, const(32), const(8)],
        out_specs=[pl.BlockSpec((3, SEQ, WC), lambda b: (DZ_CONV // 3, b, 0)), const(32), const(8)],
        out_shape=[jax.ShapeDtypeStruct(dz.shape, bf16), jax.ShapeDtypeStruct((32, WC), f32),
                   jax.ShapeDtypeStruct((8, WC), f32)],
        scratch_shapes=[pltpu.VMEM((SEQ + HALO, WC), f32), pltpu.VMEM((SEQ + HALO, WC), f32),
                        pltpu.VMEM((CONV_K, 8, WC), f32)],
        input_output_aliases={5: 0}, name="conv_bwd",
        compiler_params=_params(("arbitrary",)))(z, z, z, u1, dcv, dz, dw32, cvec)


def _pool_counts(r0):
    t1 = r0 + 1 + lax.broadcasted_iota(jnp.int32, (CT, 1), 0)
    return [jnp.minimum(t1, w).astype(f32) for w in POOL_WINDOWS]


def _pool_sums(win, forward):
    n = CT + PHALO

    def sh(x, s):
        return pltpu.roll(x, (n - s) if forward else s, axis=0)

    s2 = win + sh(win, 1)
    s4 = s2[:, GD:] + sh(s2[:, GD:], 2)
    s8 = s4[:, GD:] + sh(s4[:, GD:], 4)
    s16 = s8[:, GD:] + sh(s8[:, GD:], 8)
    lo = 0 if forward else PHALO
    return [s[lo:lo + CT, :GD] for s in (s2, s4, s8, s16)]


def _pool_fwd(z, pw, pvec, BL, SEQ):
    T = BL * SEQ
    nct = SEQ // CT

    def body(pi_ref, pg_ref, pw_ref, vec_ref, o_ref, p_ref):
        p_ref[pl.ds(0, PHALO), :] = jnp.zeros((PHALO, WC), f32)

        def fill(c, carry):
            r0 = pl.multiple_of(c * CT, CT)
            p_ref[pl.ds(r0 + PHALO, CT), :] = pi_ref[pl.ds(r0, CT), :]
            return carry

        lax.fori_loop(0, nct, fill, 0)

        def step(c, carry):
            r0 = pl.multiple_of(c * CT, CT)
            sums = _pool_sums(p_ref[pl.ds(r0, CT + PHALO), :], False)
            cnt = _pool_counts(r0)
            pin = pi_ref[pl.ds(r0, CT), :]
            mixed = []
            for g in range(4):
                pooled = sums[g] / cnt[g] - pin[:, g * GD:(g + 1) * GD]
                mixed.append(jnp.dot(pooled.astype(bf16), pw_ref[g], preferred_element_type=f32))
            m0 = jnp.concatenate(mixed, axis=1) + vec_ref[0:1, :]
            pg = pg_ref[pl.ds(r0, CT), :]
            o_ref[pl.ds(r0, CT), :] = ((m0 * vec_ref[1:2, :]) * (pg * _sig(pg))).astype(bf16)
            return carry

        lax.fori_loop(0, nct, step, 0)

    def zs(col):
        return pl.BlockSpec((SEQ, WC), lambda b: (b, col // WC))

    return pl.pallas_call(
        body, grid=(BL,),
        in_specs=[zs(COL_PI), zs(COL_PG), pl.BlockSpec((4, GD, GD), lambda b: (0, 0, 0)),
                  pl.BlockSpec((8, WC), lambda b: (0, 0))],
        out_specs=pl.BlockSpec((SEQ, WC), lambda b: (b, 0)),
        out_shape=jax.ShapeDtypeStruct((T, WC), bf16),
        scratch_shapes=[pltpu.VMEM((SEQ + PHALO, WC), f32)], name="pool_fwd",
        compiler_params=_params(("parallel",)))(z, z, pw, pvec)


def _pool_bwd(z, dpl, dz, pw, pvec, BL, SEQ):
    nct = SEQ // CT

    def body(pi_ref, pg_ref, dpl_ref, dzin_ref, pw_ref, vec_ref, dz_ref, dpw_ref, dvec_ref, p_ref, e_ref, dp_ref):
        @pl.when(pl.program_id(0) == 0)
        def _():
            dpw_ref[...] = jnp.zeros_like(dpw_ref)
            dvec_ref[...] = jnp.zeros_like(dvec_ref)

        p_ref[pl.ds(0, PHALO), :] = jnp.zeros((PHALO, WC), f32)
        e_ref[pl.ds(SEQ, PHALO), :] = jnp.zeros((PHALO, WC), f32)

        def fill(c, carry):
            r0 = pl.multiple_of(c * CT, CT)
            p_ref[pl.ds(r0 + PHALO, CT), :] = pi_ref[pl.ds(r0, CT), :]
            return carry

        lax.fori_loop(0, nct, fill, 0)

        def step(c, carry):
            r0 = pl.multiple_of(c * CT, CT)
            sums = _pool_sums(p_ref[pl.ds(r0, CT + PHALO), :], False)
            cnt = _pool_counts(r0)
            pin = pi_ref[pl.ds(r0, CT), :]
            pooled = [(sums[g] / cnt[g] - pin[:, g * GD:(g + 1) * GD]).astype(bf16) for g in range(4)]
            m0 = jnp.concatenate(
                [jnp.dot(pooled[g], pw_ref[g], preferred_element_type=f32) for g in range(4)], axis=1) + vec_ref[0:1, :]
            scale = vec_ref[1:2, :]
            pg = pg_ref[pl.ds(r0, CT), :]
            spg = _sig(pg)
            dpl_ = dpl_ref[pl.ds(r0, CT), :]
            dmixed = dpl_ * (pg * spg)
            dz_ref[1, pl.ds(r0, CT), :] = (dpl_ * (m0 * scale) * _dsilu(pg, spg)).astype(bf16)
            dvec_ref[1:2, :] += _colsum(dmixed * m0)
            dm0 = dmixed * scale
            dvec_ref[0:1, :] += _colsum(dm0)
            dps, es = [], []
            for g in range(4):
                dm0g = dm0[:, g * GD:(g + 1) * GD].astype(bf16)
                dpw_ref[g] += lax.dot_general(pooled[g], dm0g, (((0,), (0,)), ((), ())), preferred_element_type=f32)
                dpg = lax.dot_general(dm0g, pw_ref[g], (((1,), (1,)), ((), ())), preferred_element_type=f32)
                dps.append(dpg)
                es.append(dpg / cnt[g])
            dp_ref[pl.ds(r0, CT), :] = jnp.concatenate(dps, axis=1)
            e_ref[pl.ds(r0, CT), :] = jnp.concatenate(es, axis=1)
            return carry

        lax.fori_loop(0, nct, step, 0)

        def back(c, carry):
            r0 = pl.multiple_of(c * CT, CT)
            fs = _pool_sums(e_ref[pl.ds(r0, CT + PHALO), :], True)
            dz_ref[0, pl.ds(r0, CT), :] = (jnp.concatenate(fs, axis=1) - dp_ref[pl.ds(r0, CT), :]).astype(bf16)
            return carry

        lax.fori_loop(0, nct, back, 0)

    def zs(col):
        return pl.BlockSpec((SEQ, WC), lambda b: (b, col // WC))

    return pl.pallas_call(
        body, grid=(BL,),
        in_specs=[zs(COL_PI), zs(COL_PG), pl.BlockSpec((SEQ, WC), lambda b: (b, 0)), ANY,
                  pl.BlockSpec((4, GD, GD), lambda b: (0, 0, 0)), pl.BlockSpec((8, WC), lambda b: (0, 0))],
        out_specs=[pl.BlockSpec((2, SEQ, WC), lambda b: (DZ_POOL // 2, b, 0)),
                   pl.BlockSpec((4, GD, GD), lambda b: (0, 0, 0)), pl.BlockSpec((8, WC), lambda b: (0, 0))],
        out_shape=[jax.ShapeDtypeStruct(dz.shape, bf16), jax.ShapeDtypeStruct((4, GD, GD), f32),
                   jax.ShapeDtypeStruct((8, WC), f32)],
        scratch_shapes=[pltpu.VMEM((SEQ + PHALO, WC), f32), pltpu.VMEM((SEQ + PHALO, WC), f32),
                        pltpu.VMEM((SEQ, WC), f32)],
        input_output_aliases={3: 0}, name="pool_bwd",
        compiler_params=_params(("arbitrary",)))(z, z, dpl, dz, pw, pvec)


def _attn_prologue(q_ref, k_ref, v_ref, qs0, qs1, kp, vp, SEQ):
    head0 = lax.broadcasted_iota(jnp.int32, (1, 2 * HEAD_DIM), 1) < HEAD_DIM
    kp[pl.ds(0, KEY_PAD), :] = jnp.zeros((KEY_PAD, 2 * HEAD_DIM), bf16)
    vp[pl.ds(0, KEY_PAD), :] = jnp.zeros((KEY_PAD, 2 * HEAD_DIM), bf16)

    def fill(g, carry):
        r0 = pl.multiple_of(g * QG, QG)
        q = q_ref[pl.ds(r0, QG), :] * (HEAD_DIM ** -0.5)
        qs0[pl.ds(r0, QG), :] = jnp.where(head0, q, 0.0).astype(bf16)
        qs1[pl.ds(r0, QG), :] = jnp.where(head0, 0.0, q).astype(bf16)
        kp[pl.ds(r0 + KEY_PAD, QG), :] = k_ref[pl.ds(r0, QG), :].astype(bf16)
        vp[pl.ds(r0 + KEY_PAD, QG), :] = v_ref[pl.ds(r0, QG), :].astype(bf16)
        return carry

    lax.fori_loop(0, SEQ // QG, fill, 0)
    return head0


def _attn_weights(qh, kw, bias):
    s = lax.dot_general(qh, kw, (((1,), (1,)), ((), ())), preferred_element_type=f32) + bias
    e = jnp.exp(s - jnp.max(s, axis=-1, keepdims=True))
    return e, 1.0 / jnp.sum(e, axis=-1, keepdims=True)


def _attn_fwd(z, bm, BL, SEQ):
    T = BL * SEQ
    W2 = 2 * HEAD_DIM

    def body(q_ref, k_ref, v_ref, ag_ref, bm_ref, at_ref, o_ref, e_ref, qs0, qs1, kp, vp):
        head0 = _attn_prologue(q_ref, k_ref, v_ref, qs0, qs1, kp, vp, SEQ)

        def group(g, carry):
            r0 = pl.multiple_of(g * QG, QG)
            kw = kp[pl.ds(r0, KW), :]
            vw = vp[pl.ds(r0, KW), :]
            variant = jnp.minimum(g, BIAS_VARIANTS - 1)
            outs = []
            for hh, qs in enumerate((qs0, qs1)):
                e, inv = _attn_weights(qs[pl.ds(r0, QG), :], kw, bm_ref[variant, hh])
                eb = e.astype(bf16)
                e_ref[pl.ds(r0, QG), hh * KW:(hh + 1) * KW] = eb
                outs.append(jnp.dot(eb, vw, preferred_element_type=f32) * inv)
            o = jnp.where(head0, outs[0], outs[1])
            o_ref[pl.ds(r0, QG), :] = o
            ag = ag_ref[pl.ds(r0, QG), :]
            at_ref[pl.ds(r0, QG), :] = (o * (ag * _sig(ag))).astype(bf16)
            return carry

        lax.fori_loop(0, SEQ // QG, group, 0, unroll=8)

    def zs(col):
        return pl.BlockSpec((SEQ, W2), lambda b, hp: (b, col // W2 + hp))

    pair = pl.BlockSpec((SEQ, W2), lambda b, hp: (b, hp))
    return pl.pallas_call(
        body, grid=(BL, WC // W2),
        in_specs=[zs(COL_Q), zs(COL_K), zs(COL_V), zs(COL_AG),
                  pl.BlockSpec((BIAS_VARIANTS, 2, QG, KW), lambda b, hp: (0, hp, 0, 0))],
        out_specs=[pair, pair, pl.BlockSpec((None, SEQ, 2 * KW), lambda b, hp: (hp, b, 0))],
        out_shape=[jax.ShapeDtypeStruct((T, WC), bf16), jax.ShapeDtypeStruct((T, WC), f32),
                   jax.ShapeDtypeStruct((WC // W2, T, 2 * KW), bf16)],
        scratch_shapes=[pltpu.VMEM((SEQ, W2), bf16), pltpu.VMEM((SEQ, W2), bf16),
                        pltpu.VMEM((SEQ + KEY_PAD, W2), bf16), pltpu.VMEM((SEQ + KEY_PAD, W2), bf16)],
        name="attn_fwd", compiler_params=_params(("parallel", "parallel")))(z, z, z, z, bm)


def _attn_bwd(z, dat, o, ew, dz, BL, SEQ):
    W2 = 2 * HEAD_DIM

    def body(q_ref, k_ref, v_ref, ag_ref, dat_ref, o_ref, e_ref, dzin_ref, dz_ref, dbm_ref, qs0, qs1, kp, vp, dka, dva):
        @pl.when(pl.program_id(1) == 0)
        def _():
            dbm_ref[...] = jnp.zeros_like(dbm_ref)

        head0 = _attn_prologue(q_ref, k_ref, v_ref, qs0, qs1, kp, vp, SEQ)
        dka[...] = jnp.zeros_like(dka)
        dva[...] = jnp.zeros_like(dva)

        def group(g, carry):
            r0 = pl.multiple_of(g * QG, QG)
            kw = kp[pl.ds(r0, KW), :]
            vw = vp[pl.ds(r0, KW), :]
            ag = ag_ref[pl.ds(r0, QG), :]
            do = dat_ref[pl.ds(r0, QG), :] * (ag * _sig(ag))
            dqs = []
            for hh, qs in enumerate((qs0, qs1)):
                qh = qs[pl.ds(r0, QG), :]
                eb = e_ref[pl.ds(r0, QG), hh * KW:(hh + 1) * KW]
                e = eb.astype(f32)
                inv = 1.0 / jnp.sum(e, axis=-1, keepdims=True)
                doh = (jnp.where(head0, do, 0.0) if hh == 0 else jnp.where(head0, 0.0, do)) * inv
                doh = doh.astype(bf16)
                dp = lax.dot_general(doh, vw, (((1,), (1,)), ((), ())), preferred_element_type=f32)
                ds_ = e * (dp - jnp.sum(e * dp, axis=-1, keepdims=True) * inv)
                dbm_ref[hh] += ds_
                dsb = ds_.astype(bf16)
                dqs.append(jnp.dot(dsb, kw, preferred_element_type=f32))
                dka[pl.ds(r0, KW), :] += lax.dot_general(dsb, qh, (((0,), (0,)), ((), ())), preferred_element_type=f32)
                dva[pl.ds(r0, KW), :] += lax.dot_general(eb, doh, (((0,), (0,)), ((), ())), preferred_element_type=f32)
            dq = jnp.where(head0, dqs[0], dqs[1]) * (HEAD_DIM ** -0.5)
            dz_ref[0, pl.ds(r0, QG), :] = dq.astype(bf16)
            dz_ref[3, pl.ds(r0, QG), :] = (dat_ref[pl.ds(r0, QG), :] * o_ref[pl.ds(r0, QG), :]
                                           * _dsilu(ag, _sig(ag))).astype(bf16)
            return carry

        lax.fori_loop(0, SEQ // QG, group, 0, unroll=8)

        def flush(g, carry):
            r0 = pl.multiple_of(g * QG, QG)
            dz_ref[1, pl.ds(r0, QG), :] = dka[pl.ds(r0 + KEY_PAD, QG), :].astype(bf16)
            dz_ref[2, pl.ds(r0, QG), :] = dva[pl.ds(r0 + KEY_PAD, QG), :].astype(bf16)
            return carry

        lax.fori_loop(0, SEQ // QG, flush, 0)

    def zs(col):
        return pl.BlockSpec((SEQ, W2), lambda hp, b: (b, col // W2 + hp))

    pair = pl.BlockSpec((SEQ, W2), lambda hp, b: (b, hp))
    return pl.pallas_call(
        body, grid=(WC // W2, BL),
        in_specs=[zs(COL_Q), zs(COL_K), zs(COL_V), zs(COL_AG), pair, pair,
                  pl.BlockSpec((None, SEQ, 2 * KW), lambda hp, b: (hp, b, 0)), ANY],
        out_specs=[pl.BlockSpec((4, SEQ, W2), lambda hp, b: (DZ_ATTN // 4, b, hp)),
                   pl.BlockSpec((2, QG, KW), lambda hp, b: (hp, 0, 0))],
        out_shape=[jax.ShapeDtypeStruct(dz.shape, bf16), jax.ShapeDtypeStruct((8, QG, KW), f32)],
        scratch_shapes=[pltpu.VMEM((SEQ, W2), bf16), pltpu.VMEM((SEQ, W2), bf16),
                        pltpu.VMEM((SEQ + KEY_PAD, W2), bf16), pltpu.VMEM((SEQ + KEY_PAD, W2), bf16),
                        pltpu.VMEM((SEQ + KEY_PAD, W2), f32), pltpu.VMEM((SEQ + KEY_PAD, W2), f32)],
        input_output_aliases={7: 0}, name="attn_bwd",
        compiler_params=_params(("parallel", "arbitrary")))(z, z, z, z, dat, o, ew, dz)


BIAS_TOP = KEY_PAD + MAX_REL + QG - 1


def _bias_matrix(table):
    n = 2 * MAX_REL
    wd = QG + KW
    e = jnp.concatenate([jnp.broadcast_to(table[:, n:], (8, BIAS_TOP - n + 1)), table[:, n - 1:BIAS_TOP - wd + 1:-1],
                         jnp.zeros((8, 1), f32)], axis=1)
    flat = jnp.broadcast_to(e[:, None, :], (8, QG, wd)).reshape(8, QG * wd)
    skew = flat[:, :QG * (wd - 1)].reshape(8, QG, wd - 1)
    vals = skew[:, :, QG - 1:QG - 1 + KW]
    r = np.arange(QG)[:, None] // CHUNK
    j = np.arange(KW)[None, :]
    band = (j // CHUNK >= r) & (j // CHUNK <= r + LEFT_CHUNKS)
    keep = np.stack([band & (j >= KEY_PAD - v * QG) for v in range(BIAS_VARIANTS)])
    return jnp.where(jnp.asarray(keep)[:, None], vals[None], NEG_INF)


def _bias_fold(dbm):
    wd = QG + KW
    placed = jnp.pad(dbm, ((0, 0), (0, 0), (QG - 1, 0))).reshape(8, QG * (wd - 1))
    return jnp.pad(placed, ((0, 0), (0, QG))).reshape(8, QG, wd)


def _bias_colsum(folded):
    width = folded.shape[2]

    def body(x_ref, o_ref):
        for h in range(8):
            o_ref[pl.ds(h, 1), :] = _colsum(x_ref[h])

    return pl.pallas_call(body, out_shape=jax.ShapeDtypeStruct((8, width), f32), name="bias_colsum",
                          compiler_params=_params())(folded)


def _bias_table_grad(colsum):
    n = 2 * MAX_REL
    wd = QG + KW
    clipped = jnp.sum(colsum[:, :BIAS_TOP - n + 1], axis=1, keepdims=True)
    return jnp.concatenate([jnp.zeros((8, BIAS_TOP - wd + 2), f32), colsum[:, wd - 2:BIAS_TOP - n:-1], clipped], axis=1)


GATE_SPAN = 3 * WC
TAIL_ROWS = 3 * WC + D


def _gate_specs(tm):
    return [pl.BlockSpec((tm, GATE_SPAN), lambda i: (i, COL_GM // GATE_SPAN)),
            pl.BlockSpec((tm, GATE_SPAN), lambda i: (i, COL_GM // GATE_SPAN + 1))]


def _gate_block(ga_ref, gb_ref, branch, half):
    k = 2 * branch + half
    ref, k = (ga_ref, k) if k < 3 else (gb_ref, k - 3)
    return _sig(ref[:, k * WC:(k + 1) * WC])


def _resident(shape):
    return pl.BlockSpec(shape, lambda i: (0,) * len(shape), pipeline_mode=pl.Buffered(1))


def _tail_fwd(z, acts, x2, lw, next_g=None, tgt=None):
    T = z.shape[0]
    tm = 256
    with_loss = tgt is not None
    assert with_loss != (next_g is not None)

    def body(cv_ref, at_ref, pv_ref, ga_ref, gb_ref, x_ref, wc_ref, wa_ref, wp_ref, wo_ref, g_ref, *rest):
        out_ref, merged_ref, y_ref = rest[1:4]
        ys = [jnp.dot(a[...], w[...], preferred_element_type=f32)
              for a, w in ((cv_ref, wc_ref), (at_ref, wa_ref), (pv_ref, wp_ref))]
        halves = []
        for half in range(2):
            cols = slice(half * WC, (half + 1) * WC)
            halves.append(sum(_gate_block(ga_ref, gb_ref, br, half) * ys[br][:, cols] for br in range(3)))
        merged = jnp.concatenate(halves, axis=1).astype(bf16)
        merged_ref[...] = merged
        y = jnp.dot(merged, wo_ref[...], preferred_element_type=f32)
        y_ref[...] = y
        r = lax.rsqrt(jnp.mean(y * y, axis=-1, keepdims=True) + EPS)
        out = x_ref[...] + (y * r) * g_ref[...]
        if with_loss:
            sq_ref = rest[4]
            e = out - rest[0][...]
            out_ref[...] = e / float(D)

            @pl.when(pl.program_id(0) == 0)
            def _():
                sq_ref[...] = jnp.zeros_like(sq_ref)

            sq_ref[...] += _colsum(e * e)
        else:
            out_ref[...] = out
            rn = lax.rsqrt(jnp.mean(out * out, axis=-1, keepdims=True) + EPS)
            h = (out * rn) * rest[0][...]
            rest[4][...] = h.astype(bf16)
            rest[5][...] = h.T.astype(bf16)

    act = pl.BlockSpec((tm, WC), lambda i: (i, 0))
    row = pl.BlockSpec((tm, D), lambda i: (i, 0))
    vec = pl.BlockSpec((1, D), lambda i: (0, 0))
    if with_loss:
        last_in, last_specs, last_shapes = tgt, [row, [vec]], [jax.ShapeDtypeStruct((1, D), f32)]
    else:
        last_in, last_specs = next_g, [vec, [row, pl.BlockSpec((D, tm), lambda i: (0, i))]]
        last_shapes = [jax.ShapeDtypeStruct((T, D), bf16), jax.ShapeDtypeStruct((D, T), bf16)]
    return pl.pallas_call(
        body, grid=(T // tm,),
        in_specs=[act, act, act] + _gate_specs(tm) + [row, _resident((WC, D)), _resident((WC, D)), _resident((WC, D)),
                                                      _resident((D, D)), _resident((1, D)), last_specs[0]],
        out_specs=[row, row, row] + last_specs[1],
        out_shape=[jax.ShapeDtypeStruct((T, D), f32), jax.ShapeDtypeStruct((T, D), bf16), jax.ShapeDtypeStruct((T, D), f32)]
        + last_shapes,
        name="tail_fwd", compiler_params=_params(("arbitrary",)))(
            *acts, z, z, x2, lw["w_conv_out"], lw["w_attn_out"], lw["w_pool_out"], lw["w_out"], lw["post_g"], last_in)


def _tail_bwd(z, dout, y, merged, acts, lw):
    T = z.shape[0]
    tm = 256
    nt = (((1,), (1,)), ((), ()))
    tn = (((0,), (0,)), ((), ()))

    def body(d_ref, y_ref, m_ref, cv_ref, at_ref, pv_ref, ga_ref, gb_ref, wc_ref, wa_ref, wp_ref, wo_ref, g_ref,
             dz_ref, dcv_ref, dat_ref, dpv_ref, dg_ref, dw_ref):
        @pl.when(pl.program_id(0) == 0)
        def _():
            dg_ref[...] = jnp.zeros_like(dg_ref)
            dw_ref[...] = jnp.zeros_like(dw_ref)

        y = y_ref[...]
        d = d_ref[...]
        r = lax.rsqrt(jnp.mean(y * y, axis=-1, keepdims=True) + EPS)
        yn = y * r
        dyn = d * g_ref[...]
        dy = (r * (dyn - yn * jnp.mean(dyn * yn, axis=-1, keepdims=True))).astype(bf16)
        dg_ref[...] += _colsum(d * yn)
        dw_ref[pl.ds(3 * WC, D), :] += lax.dot_general(dy, m_ref[...], tn, preferred_element_type=f32)
        dmerged = lax.dot_general(dy, wo_ref[...], nt, preferred_element_type=f32)
        for br, (a_ref, w_ref, da_ref) in enumerate(((cv_ref, wc_ref, dcv_ref), (at_ref, wa_ref, dat_ref),
                                                     (pv_ref, wp_ref, dpv_ref))):
            yb = jnp.dot(a_ref[...], w_ref[...], preferred_element_type=f32)
            halves = []
            for half in range(2):
                cols = slice(half * WC, (half + 1) * WC)
                s = _gate_block(ga_ref, gb_ref, br, half)
                dm = dmerged[:, cols]
                halves.append((dm * s).astype(bf16))
                dz_ref[2 * br + half] = (dm * yb[:, cols] * s * (1.0 - s)).astype(bf16)
            dyb = jnp.concatenate(halves, axis=1)
            da_ref[...] = lax.dot_general(dyb, w_ref[...], nt, preferred_element_type=f32)
            dw_ref[pl.ds(br * WC, WC), :] += lax.dot_general(a_ref[...], dyb, tn, preferred_element_type=f32)

    act = pl.BlockSpec((tm, WC), lambda i: (i, 0))
    row = pl.BlockSpec((tm, D), lambda i: (i, 0))

    def whole(shape):
        return pl.BlockSpec(shape, lambda i: (0, 0))

    return pl.pallas_call(
        body, grid=(T // tm,),
        in_specs=[row, row, row, act, act, act] + _gate_specs(tm) + [_resident((WC, D)), _resident((WC, D)), _resident((WC, D)),
                                                                     _resident((D, D)), _resident((1, D))],
        out_specs=[pl.BlockSpec((6, tm, WC), lambda i: (DZ_GM // 6, i, 0)), act, act, act, whole((1, D)),
                   whole((TAIL_ROWS, D))],
        out_shape=[jax.ShapeDtypeStruct((DZ_BLOCKS, T, WC), bf16)] + [jax.ShapeDtypeStruct((T, WC), f32)] * 3
        + [jax.ShapeDtypeStruct((1, D), f32), jax.ShapeDtypeStruct((TAIL_ROWS, D), f32)],
        name="tail_bwd", compiler_params=_params(("arbitrary",)))(
            dout, y, merged, *acts, z, z, lw["w_conv_out"], lw["w_attn_out"], lw["w_pool_out"], lw["w_out"], lw["post_g"])


def _adamw(name, g, w, m, v):
    R, C = w.shape
    tr = R
    for cand in (512, 256, 248, 128, 64, 32, 16, 8):
        if R % cand == 0 and cand * C * 4 <= 2 * 1024 * 1024:
            tr = cand
            break
    c1 = 1.0 - ADAM_B1
    c2 = 1.0 - ADAM_B2
    bc1 = 1.0 - ADAM_B1 ** ADAM_STEP
    bc2 = 1.0 - ADAM_B2 ** ADAM_STEP

    def body(g_ref, w_ref, m_ref, v_ref, d_ref, nm_ref, nv_ref):
        g_ = g_ref[...]
        nm = ADAM_B1 * m_ref[...] + c1 * g_
        nv = ADAM_B2 * v_ref[...] + c2 * (g_ * g_)
        nm_ref[...] = nm
        nv_ref[...] = nv
        d_ref[...] = -ADAM_LR * ((nm / bc1) / (jnp.sqrt(nv / bc2) + ADAM_EPS) + ADAM_WD * w_ref[...])

    spec = pl.BlockSpec((tr, C), lambda i: (i, 0))
    return pl.pallas_call(
        body, grid=(R // tr,), in_specs=[spec] * 4, out_specs=[spec] * 3,
        out_shape=[jax.ShapeDtypeStruct((R, C), f32)] * 3, name=name,
        compiler_params=_params(("parallel",)))(g, w, m, v)


def _sum_slots(name, parts):
    _, R, C = parts.shape
    tr = R
    for cand in (256, 128, 64, 32, 16, 8):
        if R % cand == 0 and cand * C * 4 * N_DEV <= 8 * 1024 * 1024:
            tr = cand
            break

    def body(p_ref, o_ref):
        acc = p_ref[0].astype(f32)
        for s in range(1, N_DEV):
            acc = acc + p_ref[s].astype(f32)
        o_ref[...] = acc

    return pl.pallas_call(
        body, grid=(R // tr,), in_specs=[pl.BlockSpec((N_DEV, tr, C), lambda i: (0, i, 0))],
        out_specs=pl.BlockSpec((tr, C), lambda i: (i, 0)), out_shape=jax.ShapeDtypeStruct((R, C), f32),
        name=name, compiler_params=_params(("parallel",)))(parts)


def _row_tile(rows, row_bytes, budget):
    for cand in (512, 256, 128, 64, 32, 16):
        if rows % cand == 0 and cand * row_bytes <= budget:
            return cand
    return rows


def _pair_sum(core, g, theirs):
    R2, C4 = theirs.shape
    tr = _row_tile(R2, C4 * 2, 2 * 1024 * 1024)
    nb = R2 // tr

    def body(core_ref, g_ref, t_ref, o_ref):
        o_ref[...] = (g_ref[...].astype(f32) + t_ref[...].astype(f32)).astype(bf16)

    return pl.pallas_call(
        body,
        grid_spec=pltpu.PrefetchScalarGridSpec(
            num_scalar_prefetch=1, grid=(nb,),
            in_specs=[pl.BlockSpec((tr, C4), lambda i, core_ref: (core_ref[0] * nb + i, 0)),
                      pl.BlockSpec((tr, C4), lambda i, core_ref: (i, 0))],
            out_specs=pl.BlockSpec((tr, C4), lambda i, core_ref: (i, 0))),
        out_shape=jax.ShapeDtypeStruct((R2, C4), bf16), name="pair_sum",
        compiler_params=_params(("parallel",)))(core, g, theirs)


def _chip_sum(chip, mine, others):
    _, R2, C = others.shape
    tr = _row_tile(R2, C * 4, 1024 * 1024)

    def body(chip_ref, m_ref, o_ref, out_ref):
        acc = m_ref[...].astype(f32)
        for s in range(N_CHIPS - 1):
            acc = acc + o_ref[s].astype(f32)
        out_ref[...] = acc

    return pl.pallas_call(
        body,
        grid_spec=pltpu.PrefetchScalarGridSpec(
            num_scalar_prefetch=1, grid=(R2 // tr,),
            in_specs=[pl.BlockSpec((tr, C), lambda i, chip_ref: (i, chip_ref[0])),
                      pl.BlockSpec((N_CHIPS - 1, tr, C), lambda i, chip_ref: (0, i, 0))],
            out_specs=pl.BlockSpec((tr, C), lambda i, chip_ref: (i, 0))),
        out_shape=jax.ShapeDtypeStruct((R2, C), f32), name="chip_sum",
        compiler_params=_params(("parallel",)))(chip, mine, others)


def _place():
    x, y, c = lax.axis_index("x"), lax.axis_index("y"), lax.axis_index("c")
    return x, y, c


def _flip(v, bit):
    return 1 - v if bit else v


CHIP_FLIPS = ((1, 0), (0, 1), (1, 1))


class _Sems:
    def __init__(self, send, recv):
        self.send, self.recv = send, recv
        self.pairs = 0

    def pair(self):
        k = self.pairs
        self.pairs += 1
        return self.send.at[k], self.recv.at[k]


def _remote(src, dst, lands, sems, to):
    s, r = sems.pair()
    copy = pltpu.make_async_remote_copy(src_ref=src, dst_ref=dst, send_sem=s, recv_sem=r, device_id=to, device_id_type=MESH)
    wait = pltpu.make_async_remote_copy(src_ref=lands, dst_ref=lands, send_sem=s, recv_sem=r, device_id=to, device_id_type=MESH)
    return copy, wait


def _exchange(name, build, srcs, lands, n_remote):
    n_s, n_l = len(srcs), len(lands)

    def body(*refs):
        send, recv = refs[n_s + 2 * n_l:]
        remotes, recvs = build(refs[:n_s], refs[n_s + n_l:n_s + 2 * n_l], _Sems(send, recv))
        for cp in remotes:
            cp.start()
        for rv in recvs:
            rv.wait_recv()
        for cp in remotes:
            cp.wait_send()

    return pl.pallas_call(
        body, in_specs=[ANY] * (n_s + n_l), out_specs=[ANY] * n_l,
        out_shape=[jax.ShapeDtypeStruct(t.shape, t.dtype) for t in lands],
        scratch_shapes=[pltpu.SemaphoreType.DMA((n_remote,)), pltpu.SemaphoreType.DMA((n_remote,))],
        input_output_aliases={n_s + i: i for i in range(n_l)}, name=name)(*srcs, *lands)


HBM = pl.BlockSpec(memory_space=pltpu.HBM)
SEMS = pl.BlockSpec(memory_space=pltpu.SEMAPHORE)
DATAFLOW = pltpu.SideEffectType.DATAFLOW_SIDE_EFFECTING


def _start(name, build, srcs, lands, n_remote, after):
    n_s, n_l = len(srcs), len(lands)

    def body(*refs):
        send, recv = refs[n_s + n_l + 1], refs[n_s + n_l + 2]
        remotes, _ = build(refs[:n_s], refs[n_s:n_s + n_l], _Sems(send, recv))
        for cp in remotes:
            cp.start()
        refs[-1][...] = jnp.zeros((8, 128), f32)

    arrays = [pltpu.with_memory_space_constraint(a, pltpu.HBM) for a in (*srcs, *lands)]
    out = pl.pallas_call(
        body, name=name, in_specs=[HBM] * (n_s + n_l) + [ANY],
        out_specs=(SEMS, SEMS, *[HBM] * (n_s + n_l), pl.BlockSpec(memory_space=pltpu.VMEM)),
        out_shape=(pltpu.SemaphoreType.DMA((n_remote,)), pltpu.SemaphoreType.DMA((n_remote,)),
                   *[pltpu.HBM(a.shape, a.dtype) for a in arrays], jax.ShapeDtypeStruct((8, 128), f32)),
        input_output_aliases={i: 2 + i for i in range(n_s + n_l)},
        compiler_params=pltpu.CompilerParams(has_side_effects=DATAFLOW))(*arrays, after)
    return dict(name=name, build=build, sems=out[:2], srcs=out[2:2 + n_s], lands=out[2 + n_s:2 + n_s + n_l], token=out[-1])


def _wait(started, after):
    srcs, lands, build = started["srcs"], started["lands"], started["build"]
    n_s, n_l = len(srcs), len(lands)
    after = list(after) if isinstance(after, (list, tuple)) else [after]

    def body(*refs):
        send, recv = refs[n_s + n_l], refs[n_s + n_l + 1]
        remotes, recvs = build(refs[:n_s], refs[n_s:n_s + n_l], _Sems(send, recv))
        for rv in recvs:
            rv.wait_recv()
        for cp in remotes:
            cp.wait_send()

    out = pl.pallas_call(
        body, name=started["name"] + "_wait", in_specs=[HBM] * (n_s + n_l) + [SEMS, SEMS] + [ANY] * len(after),
        out_specs=[HBM] * (n_s + n_l), out_shape=[pltpu.HBM(a.shape, a.dtype) for a in (*srcs, *lands)],
        input_output_aliases={i: i for i in range(n_s + n_l)},
        compiler_params=pltpu.CompilerParams(has_side_effects=DATAFLOW))(*srcs, *lands, *started["sems"], *after)
    return out[:n_s], out[n_s:]


def _gather_plans(n_split, n_all):
    def over_ici(src, land, sems):
        x, y, c = _place()
        chip = 2 * x + y
        remotes, recvs = [], []
        for a in range(n_all):
            for fx, fy in CHIP_FLIPS:
                px, py = _flip(x, fx), _flip(y, fy)
                if a < n_split:
                    r2 = src[a].shape[0] // 2
                    rows = pl.ds(c * r2, r2)
                    cp, rv = _remote(src[a].at[rows], land[a].at[chip, rows], land[a].at[2 * px + py, rows], sems, (px, py, c))
                else:
                    cp, rv = _remote(src[a], land[a].at[chip], land[a].at[2 * px + py], sems, (px, py, c))
                remotes.append(cp)
                recvs.append(rv)
        return remotes, recvs

    def over_d2d(src, land, sems):
        x, y, c = _place()
        remotes, recvs = [], []
        for a in range(n_split):
            r2 = land[a].shape[1] // 2
            for fx, fy in CHIP_FLIPS:
                owner = 2 * _flip(x, fx) + _flip(y, fy)
                mine = land[a].at[owner, pl.ds(c * r2, r2)]
                cp, rv = _remote(mine, mine, land[a].at[owner, pl.ds((1 - c) * r2, r2)], sems, (x, y, 1 - c))
                remotes.append(cp)
                recvs.append(rv)
        return remotes, recvs

    return over_ici, over_d2d


def _gather_begin(tag, shards, n_split, after):
    over_ici, _ = _gather_plans(n_split, len(shards))
    lands = [lax.empty((N_CHIPS,) + s.shape, s.dtype) for s in shards]
    return _start("gather_ici_" + tag, over_ici, shards, lands, 3 * len(shards), after)


def _gather_end(started, shards, n_split, after):
    _, over_d2d = _gather_plans(n_split, len(shards))
    lands = _exchange("gather_d2d", over_d2d, [], _wait(started, after)[1], 3 * n_split)
    chip = 2 * lax.axis_index("x") + lax.axis_index("y")
    return [lax.dynamic_update_slice_in_dim(g, s[None], chip, axis=0) for g, s in zip(lands, shards)]


def _reduce_plans(n):
    def to_sibling(src, land, sems):
        x, y, c = _place()
        remotes, recvs = [], []
        for a in range(n):
            r2 = src[a].shape[0] // 2
            cp, rv = _remote(src[a].at[pl.ds((1 - c) * r2, r2), :], land[a], land[a], sems, (x, y, 1 - c))
            remotes.append(cp)
            recvs.append(rv)
        return remotes, recvs

    def across_chips(src, land, sems):
        x, y, c = _place()
        remotes, recvs = [], []
        for a in range(n):
            cw = src[a].shape[1] // N_CHIPS
            for k, (fx, fy) in enumerate(CHIP_FLIPS):
                px, py = _flip(x, fx), _flip(y, fy)
                cp, rv = _remote(src[a].at[:, pl.ds((2 * px + py) * cw, cw)], land[a].at[k], land[a].at[k], sems, (px, py, c))
                remotes.append(cp)
                recvs.append(rv)
        return remotes, recvs

    def share(src, land, sems):
        x, y, c = _place()
        remotes, recvs = [], []
        for a in range(n):
            cp, rv = _remote(src[a], land[a], land[a], sems, (x, y, 1 - c))
            remotes.append(cp)
            recvs.append(rv)
        return remotes, recvs

    return to_sibling, across_chips, share


def _reduce_begin(grads):
    n = len(grads)
    to_sibling, across_chips, _ = _reduce_plans(n)
    core = lax.axis_index("c").reshape(1).astype(jnp.int32)
    theirs = _exchange("reduce_pair", to_sibling, grads,
                       [lax.empty((g.shape[0] // 2, g.shape[1]), bf16) for g in grads], n)
    pair = [_pair_sum(core, g, t) for g, t in zip(grads, theirs)]
    lands = [lax.empty((N_CHIPS - 1, g.shape[0] // 2, g.shape[1] // N_CHIPS), bf16) for g in grads]
    return _start("reduce_chips", across_chips, pair, lands, 3 * n, pair[0])


def _reduce_end(started, after):
    x, y, c = _place()
    chip = (2 * x + y).reshape(1).astype(jnp.int32)
    pair, others = _wait(started, after)
    _, _, share = _reduce_plans(len(pair))
    mine = [_chip_sum(chip, p, o) for p, o in zip(pair, others)]
    sibs = _exchange("reduce_share", share, mine, [lax.empty(h.shape, f32) for h in mine], len(mine))
    return [jnp.where(c == 0, jnp.concatenate([h, s], axis=0), jnp.concatenate([s, h], axis=0))
            for h, s in zip(mine, sibs)]


def _to_all(src, land, sems):
    x, y, c = _place()
    me = 4 * x + 2 * y + c
    remotes, recvs = [], []
    for k in range(1, N_DEV):
        px, py, pc = _flip(x, (k >> 2) & 1), _flip(y, (k >> 1) & 1), _flip(c, k & 1)
        cp, rv = _remote(src[0], land[0].at[me], land[0].at[4 * px + 2 * py + pc], sems, (px, py, pc))
        remotes.append(cp)
        recvs.append(rv)
    return remotes, recvs


def _gather_small_begin(packed):
    return _start("gather_small", _to_all, [packed], [lax.empty((N_DEV,) + packed.shape, f32)], N_DEV - 1, packed)


def _gather_small_end(started, after):
    (packed,), (others,) = _wait(started, after)
    x, y, c = _place()
    return lax.dynamic_update_slice_in_dim(others, packed[None], 4 * x + 2 * y + c, axis=0)


def _gather_all(packed):
    others = _exchange("gather_all", _to_all, [packed], [lax.empty((N_DEV,) + packed.shape, f32)], N_DEV - 1)[0]
    x, y, c = _place()
    return lax.dynamic_update_slice_in_dim(others, packed[None], 4 * x + 2 * y + c, axis=0)


def _rows8(v):
    return jnp.pad(v[None, :], ((0, 7), (0, 0)))


def _vec_rows(vs):
    return jnp.pad(jnp.stack(vs), ((0, 8 - len(vs)), (0, 0)))


SMALL_ROWS = 224


def _pack_small(conv_vec, conv_dw, pool_vec, pool_w, pre_g, post_g, rel):
    return jnp.concatenate([
        conv_vec, conv_dw, pool_vec, pool_w.reshape(GD, WC),
        _rows8(pre_g).reshape(16, WC), _rows8(post_g).reshape(16, WC),
        jnp.pad(rel, ((0, 0), (0, D - rel.shape[1]))).reshape(16, WC)], axis=0)


def _unpack_small(p):
    conv_vec, pool_vec = p[0:8], p[40:48]
    return dict(
        conv_dw_b=conv_vec[0], conv_ln_g=conv_vec[1], conv_ln_b=conv_vec[2], conv_dw=p[8:8 + CONV_K],
        pool_b=pool_vec[0].reshape(4, GD), pool_scale=pool_vec[1], pool_w=p[48:176].reshape(4, GD, GD),
        pre_norm_g=p[176:192].reshape(8, D)[0], post_norm_g=p[192:208].reshape(8, D)[0],
        rel_bias=p[208:224].reshape(8, D)[:, :2 * MAX_REL + 1])


def _layer_fwd(x2, ht, z, lw, BL, SEQ, next_g=None, tgt=None):
    cv, u1 = _conv_fwd(z, lw["dw32"], lw["cvec"], BL, SEQ)
    at, attn_o, attn_e = _attn_fwd(z, lw["bm"], BL, SEQ)
    pv = _pool_fwd(z, lw["pw"], lw["pvec"], BL, SEQ)
    out, merged, y, *last = _tail_fwd(z, (cv, at, pv), x2, lw, next_g, tgt)
    saved = dict(x=x2, ht=ht, z=z, u1=u1, attn_o=attn_o, attn_e=attn_e, acts=(cv, at, pv), merged=merged, y=y)
    return (out, *last), saved


def _layer_bwd(dout, sv, lw, BL, SEQ, meanwhile=None):
    tail = _tail_bwd(sv["z"], dout, sv["y"], sv["merged"], sv["acts"], lw)
    dz, dacts, dpost = tail[0], tail[1:4], tail[4]
    dw_tail = tail[5].astype(bf16)
    if meanwhile is not None:
        meanwhile(dw_tail)
    dz, ddw, dcvec = _conv_bwd(sv["z"], sv["u1"], dacts[0], dz, lw["dw32"], lw["cvec"], BL, SEQ)
    dz, dbm = _attn_bwd(sv["z"], dacts[1], sv["attn_o"], sv["attn_e"], dz, BL, SEQ)
    dz, dpw, dpvec = _pool_bwd(sv["z"], dacts[2], dz, lw["pw"], lw["pvec"], BL, SEQ)
    drel = _bias_table_grad(_bias_colsum(_bias_fold(dbm)))
    small_gather = _gather_small_begin(_pack_small(dcvec, ddw, dpvec, dpw, jnp.zeros((D,), f32), dpost[0], drel))
    dw_in = _mm_dw_in(sv["ht"], dz, small_gather["token"])
    reduction = _reduce_begin([dw_in, dw_tail])
    dx, dpre = _mm_dx(dz, lw["w_in"], sv["x"], lw["pre_g"], dout, reduction["token"])
    return dx, reduction, small_gather, dpre


BIG = ("w_in", "w_conv_out", "w_attn_out", "w_pool_out", "w_out")
PRE_ROWS = slice(176, 192)


def _layer_shards(w, l):
    return [w[k][l].astype(bf16) for k in BIG] + [w["conv_dw"][l]]


def _side_by_side(g):
    return jnp.transpose(g, (1, 0, 2)).reshape(g.shape[1], N_CHIPS * g.shape[2])


def _layer_weights(w_in, gathered, w, l, bm):
    lw = {k: _side_by_side(g) for k, g in zip(BIG[1:4], gathered[:3])}
    lw["w_in"] = w_in
    lw["w_out"] = gathered[3].reshape(D, D)
    lw["pre_g"] = w["pre_norm_g"][l][None]
    lw["post_g"] = w["post_norm_g"][l][None]
    lw["dw32"] = jnp.pad(_side_by_side(gathered[4]), ((0, 32 - CONV_K), (0, 0)))
    lw["cvec"] = _vec_rows([w["conv_dw_b"][l], w["conv_ln_g"][l], w["conv_ln_b"][l]])
    lw["bm"] = bm
    lw["pw"] = w["pool_w"][l].astype(bf16)
    lw["pvec"] = _vec_rows([w["pool_b"][l].reshape(WC), w["pool_scale"][l]])
    return lw


SMALL = ("pre_norm_g", "post_norm_g", "conv_dw_b", "conv_ln_g", "conv_ln_b", "rel_bias", "pool_w", "pool_b", "pool_scale")
ORDER = ("pre_norm_g", "post_norm_g", "w_in", "conv_dw", "conv_dw_b", "conv_ln_g", "conv_ln_b", "w_conv_out",
         "rel_bias", "w_attn_out", "pool_w", "pool_b", "pool_scale", "w_pool_out", "w_out")


def _pack_small_params(p):
    return jnp.concatenate([
        _pack_small(_vec_rows([p["conv_dw_b"][l], p["conv_ln_g"][l], p["conv_ln_b"][l]]), jnp.zeros((32, WC), f32),
                    _vec_rows([p["pool_b"][l].reshape(WC), p["pool_scale"][l]]), p["pool_w"][l],
                    p["pre_norm_g"][l], p["post_norm_g"][l], p["rel_bias"][l])
        for l in range(DEPTH)], axis=0)


def _unpack_small_params(packed):
    layers = [_unpack_small(packed[l * SMALL_ROWS:(l + 1) * SMALL_ROWS]) for l in range(DEPTH)]
    return {k: jnp.stack([layers[l][k] for l in range(DEPTH)]) for k in layers[0]}


def kernel(x, pre_norm_g, post_norm_g, w_in, conv_dw, conv_dw_b, conv_ln_g, conv_ln_b, w_conv_out, rel_bias, w_attn_out, pool_w, pool_b, pool_scale, w_pool_out, w_out, loss_target, m_pre_norm_g, m_post_norm_g, m_w_in, m_conv_dw, m_conv_dw_b, m_conv_ln_g, m_conv_ln_b, m_w_conv_out, m_rel_bias, m_w_attn_out, m_pool_w, m_pool_b, m_pool_scale, m_w_pool_out, m_w_out, v_pre_norm_g, v_post_norm_g, v_w_in, v_conv_dw, v_conv_dw_b, v_conv_ln_g, v_conv_ln_b, v_w_conv_out, v_rel_bias, v_w_attn_out, v_pool_w, v_pool_b, v_pool_scale, v_w_pool_out, v_w_out):
    BL, SEQ, _ = x.shape
    T = BL * SEQ
    w = dict(pre_norm_g=pre_norm_g, post_norm_g=post_norm_g, w_in=w_in, conv_dw=conv_dw, conv_dw_b=conv_dw_b,
             conv_ln_g=conv_ln_g, conv_ln_b=conv_ln_b, w_conv_out=w_conv_out, rel_bias=rel_bias, w_attn_out=w_attn_out,
             pool_w=pool_w, pool_b=pool_b, pool_scale=pool_scale, w_pool_out=w_pool_out, w_out=w_out)
    m = dict(pre_norm_g=m_pre_norm_g, post_norm_g=m_post_norm_g, w_in=m_w_in, conv_dw=m_conv_dw, conv_dw_b=m_conv_dw_b,
             conv_ln_g=m_conv_ln_g, conv_ln_b=m_conv_ln_b, w_conv_out=m_w_conv_out, rel_bias=m_rel_bias,
             w_attn_out=m_w_attn_out, pool_w=m_pool_w, pool_b=m_pool_b, pool_scale=m_pool_scale,
             w_pool_out=m_w_pool_out, w_out=m_w_out)
    v = dict(pre_norm_g=v_pre_norm_g, post_norm_g=v_post_norm_g, w_in=v_w_in, conv_dw=v_conv_dw, conv_dw_b=v_conv_dw_b,
             conv_ln_g=v_conv_ln_g, conv_ln_b=v_conv_ln_b, w_conv_out=v_w_conv_out, rel_bias=v_rel_bias,
             w_attn_out=v_w_attn_out, pool_w=v_pool_w, pool_b=v_pool_b, pool_scale=v_pool_scale,
             w_pool_out=v_w_pool_out, w_out=v_w_out)

    shards = [_layer_shards(w, l) for l in range(DEPTH)]
    x2 = x.reshape(T, D)
    h0, ht0 = _rms_pre(x2, pre_norm_g[0][None])
    first = _gather_begin("w_in0", shards[0][:1], 1, x2)
    bms = [_bias_matrix(rel_bias[l]) for l in range(DEPTH)]
    packs = [_pack_small_params(p) for p in (w, m, v)]
    w_in0 = _side_by_side(_gather_end(first, shards[0][:1], 1, [ht0, *bms, *packs])[0])
    rest0 = _gather_begin("rest0", shards[0][1:], 4, w_in0)
    all1 = _gather_begin("layer1", shards[1], 5, rest0["token"])
    z0 = _in_proj(h0, w_in0, after=all1["token"])
    lw0 = _layer_weights(w_in0, _gather_end(rest0, shards[0][1:], 4, z0), w, 0, bms[0])
    (out0, h1, ht1), saved0 = _layer_fwd(x2, ht0, z0, lw0, BL, SEQ, next_g=pre_norm_g[1][None])
    gathered1 = _gather_end(all1, shards[1], 5, out0)
    lw1 = _layer_weights(_side_by_side(gathered1[0]), gathered1[1:], w, 1, bms[1])
    z1 = _in_proj(h1, lw1["w_in"], after=h1)
    (dout, sq), saved1 = _layer_fwd(out0, ht1, z1, lw1, BL, SEQ, tgt=loss_target.reshape(T, D))
    loss = lax.psum(0.5 * jnp.sum(sq) / float(D), ("x", "y", "c"))

    summed = [None] * DEPTH
    dx1, reduction1, small_gather1, dpre1 = _layer_bwd(dout, saved1, lw1, BL, SEQ)

    def finish_layer1(after):
        summed[1] = _reduce_end(reduction1, after)

    grad_x, reduction0, small_gather0, dpre0 = _layer_bwd(dx1, saved0, lw0, BL, SEQ, meanwhile=finish_layer1)
    summed[0] = _reduce_end(reduction0, grad_x)
    dpre = _sum_slots("sum_small", _gather_all(jnp.concatenate([_rows8(dpre0[0]), _rows8(dpre1[0])], axis=0)))
    gsmall = []
    for l, started in enumerate((small_gather0, small_gather1)):
        g = _sum_slots("sum_small", _gather_small_end(started, dpre))
        gsmall += [g[:PRE_ROWS.start], dpre[8 * l:8 * l + 8].reshape(16, WC), g[PRE_ROWS.stop:]]
    gsmall = jnp.concatenate(gsmall, axis=0)

    grads, deltas, new_m, new_v = {}, {}, {}, {}
    for i, k in enumerate(BIG):
        if k == "w_in":
            g = jnp.stack([summed[l][0] for l in range(DEPTH)])
        elif k == "w_out":
            g = jnp.stack([summed[l][1][3 * WC:].T for l in range(DEPTH)])
        else:
            g = jnp.stack([summed[l][1][(i - 1) * WC:i * WC] for l in range(DEPTH)])
        grads[k] = g
        shape = w[k].shape
        flat2 = lambda a: a.reshape(shape[0] * shape[1], shape[2])
        d_, nm_, nv_ = _adamw("adamw_big", flat2(g), flat2(w[k]), flat2(m[k]), flat2(v[k]))
        deltas[k], new_m[k], new_v[k] = d_.reshape(shape), nm_.reshape(shape), nv_.reshape(shape)

    d_, nm_, nv_ = _adamw("adamw_small", gsmall, *packs)
    gs, ds, ms, vs = (_unpack_small_params(a) for a in (gsmall, d_, nm_, nv_))
    for k in SMALL:
        grads[k], deltas[k], new_m[k], new_v[k] = gs[k], ds[k], ms[k], vs[k]
    chip = 2 * lax.axis_index("x") + lax.axis_index("y")
    g_dw = lax.dynamic_slice_in_dim(gs["conv_dw"], chip * GD, GD, axis=2)
    flat2 = lambda a: a.reshape(DEPTH * CONV_K, GD)
    d_, nm_, nv_ = _adamw("adamw_conv_dw", flat2(g_dw), flat2(conv_dw), flat2(m["conv_dw"]), flat2(v["conv_dw"]))
    grads["conv_dw"] = g_dw
    deltas["conv_dw"], new_m["conv_dw"], new_v["conv_dw"] = (a.reshape(conv_dw.shape) for a in (d_, nm_, nv_))

    return (loss, grad_x.reshape(x.shape), *[grads[k] for k in ORDER], *[deltas[k] for k in ORDER],
            *[new_m[k] for k in ORDER], *[new_v[k] for k in ORDER])
```

```python
import numpy as np
import jax
import jax.numpy as jnp
from jax import lax
from jax.experimental import pallas as pl
from jax.experimental.pallas import tpu as pltpu

f32 = jnp.float32
bf16 = jnp.bfloat16

D = 1024
DEPTH = 2
WC = 512
HEAD_DIM = 64
CHUNK = 64
LEFT_CHUNKS = 8
KEY_PAD = LEFT_CHUNKS * CHUNK
MAX_REL = 256
CONV_K = 31
POOL_WINDOWS = (2, 4, 8, 16)
GD = 128
NCOL = 7680
EPS = 1e-6
NEG_INF = -1e30
COL_A, COL_B, COL_CG, COL_Q, COL_K, COL_V, COL_AG, COL_PI, COL_PG, COL_GM = (
    0, 512, 1024, 1536, 2048, 2560, 3072, 3584, 4096, 4608)

ADAM_LR = 0.001
ADAM_B1 = 0.9
ADAM_B2 = 0.999
ADAM_EPS = 1e-08
ADAM_WD = 0.01
ADAM_STEP = 10

QG = 256
KW = KEY_PAD + QG
BIAS_VARIANTS = KEY_PAD // QG + 1
CT = 256
HALO = 32
PHALO = 16
N_CHIPS = 4
N_DEV = 8
VMEM_LIMIT = 58 * 1024 * 1024
MESH = pl.DeviceIdType.MESH
ANY = pl.BlockSpec(memory_space=pl.ANY)

DZ_BLOCKS = 18
DZ_CONV, DZ_ATTN, DZ_POOL, DZ_GM = 0, 4, 8, 12


def _dz_block(c):
    return c + (c >= 3).astype(jnp.int32) + 2 * (c >= 9).astype(jnp.int32)


def _params(sem=None):
    return pltpu.CompilerParams(dimension_semantics=sem, vmem_limit_bytes=VMEM_LIMIT)


def _sig(x):
    return 1.0 / (1.0 + jnp.exp(-x))


def _dsilu(x, s):
    return s * (1.0 + x * (1.0 - s))


def _colsum(x):
    return jnp.sum(x, axis=0, keepdims=True)


def _rms_pre(x2, g):
    T = x2.shape[0]
    tm = 512

    def body(x_ref, g_ref, h_ref, ht_ref):
        x = x_ref[...]
        r = lax.rsqrt(jnp.mean(x * x, axis=-1, keepdims=True) + EPS)
        h = (x * r) * g_ref[...]
        h_ref[...] = h.astype(bf16)
        ht_ref[...] = h.T.astype(bf16)

    row = pl.BlockSpec((tm, D), lambda i: (i, 0))
    vec = pl.BlockSpec((1, D), lambda i: (0, 0))
    return pl.pallas_call(
        body, grid=(T // tm,), in_specs=[row, vec], out_specs=[row, pl.BlockSpec((D, tm), lambda i: (0, i))],
        out_shape=[jax.ShapeDtypeStruct((T, D), bf16), jax.ShapeDtypeStruct((D, T), bf16)], name="rms_pre",
        compiler_params=_params(("parallel",)))(x2, g)


def _in_proj(h, w_in, after):
    T = h.shape[0]
    tm, tn = 512, 1536

    def body(h_ref, w_ref, after_ref, z_ref):
        for n0 in range(0, NCOL, tn):
            z_ref[:, n0:n0 + tn] = jnp.dot(h_ref[...], w_ref[:, n0:n0 + tn], preferred_element_type=f32)

    return pl.pallas_call(
        body, grid=(T // tm,),
        in_specs=[pl.BlockSpec((tm, D), lambda i: (i, 0)),
                  pl.BlockSpec((D, NCOL), lambda i: (0, 0), pipeline_mode=pl.Buffered(1)), ANY],
        out_specs=pl.BlockSpec((tm, NCOL), lambda i: (i, 0)), out_shape=jax.ShapeDtypeStruct((T, NCOL), f32),
        name="mm_in", compiler_params=_params(("parallel",)))(h, w_in, after)


DZ_SPANS = ((DZ_CONV, 3), (DZ_ATTN, 4), (DZ_POOL, 2), (DZ_GM, 6))


def _mm_dx(dz, w_in, x2, g, dout, after):
    T = dz.shape[1]
    tm = 512

    def body(conv_ref, attn_ref, pool_ref, gm_ref, w_ref, x_ref, g_ref, d_ref, after_ref, dx_ref, dg_ref):
        dh = None
        col = 0
        for ref, (_, blocks) in zip((conv_ref, attn_ref, pool_ref, gm_ref), DZ_SPANS):
            for b in range(blocks):
                p = lax.dot_general(ref[b], w_ref[:, col * WC:(col + 1) * WC], (((1,), (1,)), ((), ())),
                                    preferred_element_type=f32)
                dh = p if dh is None else dh + p
                col += 1
        x = x_ref[...]
        r = lax.rsqrt(jnp.mean(x * x, axis=-1, keepdims=True) + EPS)
        xn = x * r
        dxn = dh * g_ref[...]
        dx_ref[...] = r * (dxn - xn * jnp.mean(dxn * xn, axis=-1, keepdims=True)) + d_ref[...]

        @pl.when(pl.program_id(0) == 0)
        def _():
            dg_ref[...] = jnp.zeros_like(dg_ref)

        dg_ref[...] += _colsum(dh * xn)

    spans = [pl.BlockSpec((blocks, tm, WC), lambda i, first=first, blocks=blocks: (first // blocks, i, 0))
             for first, blocks in DZ_SPANS]
    row = pl.BlockSpec((tm, D), lambda i: (i, 0))
    vec = pl.BlockSpec((1, D), lambda i: (0, 0))
    return pl.pallas_call(
        body, grid=(T // tm,),
        in_specs=spans + [pl.BlockSpec((D, NCOL), lambda i: (0, 0), pipeline_mode=pl.Buffered(1)), row, vec, row, ANY],
        out_specs=[row, vec], out_shape=[jax.ShapeDtypeStruct((T, D), f32), jax.ShapeDtypeStruct((1, D), f32)],
        name="mm_dx", compiler_params=_params(("arbitrary",)))(dz, dz, dz, dz, w_in, x2, g, dout, after)


def _mm_dw_in(ht, dz, after):
    T = dz.shape[1]

    def body(ht_ref, dz_ref, after_ref, o_ref):
        o_ref[...] = jnp.dot(ht_ref[...], dz_ref[...], preferred_element_type=f32).astype(bf16)

    return pl.pallas_call(
        body, grid=(NCOL // WC,),
        in_specs=[pl.BlockSpec((D, T), lambda j: (0, 0), pipeline_mode=pl.Buffered(1)),
                  pl.BlockSpec((None, T, WC), lambda j: (_dz_block(j), 0, 0)), ANY],
        out_specs=pl.BlockSpec((D, WC), lambda j: (0, j)), out_shape=jax.ShapeDtypeStruct((D, NCOL), bf16),
        name="mm_dw_in", compiler_params=_params(("parallel",)))(ht, dz, after)


def _conv_delays():
    return [(8 * a + b, a, b) for b in range(8) for a in range(4) if 8 * a + b < CONV_K]


CONV_LANES = 128


def _conv_rolls(win):
    return [win if b == 0 else pltpu.roll(win, b, axis=0) for b in range(8)]


def _conv_taps(rolled, dw_ref):
    acc = None
    for d, a, b in _conv_delays():
        term = rolled[b][HALO - 8 * a:HALO - 8 * a + CT, :] * dw_ref[pl.ds(CONV_K - 1 - d, 1), :]
        acc = term if acc is None else acc + term
    return acc


def _conv_fwd(z, dw32, cvec, BL, SEQ):
    T = BL * SEQ
    nct = SEQ // CT

    def body(a_ref, b_ref, cg_ref, dw_ref, vec_ref, o_ref, u1_ref, p_ref):
        p_ref[pl.ds(0, HALO), :] = jnp.zeros((HALO, WC), f32)

        def glu(c, carry):
            r0 = pl.multiple_of(c * CT, CT)
            p_ref[pl.ds(r0 + HALO, CT), :] = a_ref[pl.ds(r0, CT), :] * _sig(b_ref[pl.ds(r0, CT), :])
            return carry

        lax.fori_loop(0, nct, glu, 0)

        def step(c, carry):
            r0 = pl.multiple_of(c * CT, CT)
            u1 = _conv_taps(_conv_rolls(p_ref[pl.ds(r0, CT + HALO), :]), dw_ref) + vec_ref[0:1, :]
            u1_ref[pl.ds(r0, CT), :] = u1
            xc = u1 - jnp.mean(u1, axis=-1, keepdims=True)
            rs = lax.rsqrt(jnp.mean(xc * xc, axis=-1, keepdims=True) + EPS)
            u2 = (xc * rs) * vec_ref[1:2, :] + vec_ref[2:3, :]
            cg = cg_ref[pl.ds(r0, CT), :]
            o_ref[pl.ds(r0, CT), :] = ((u2 * _sig(u2)) * (cg * _sig(cg))).astype(bf16)
            return carry

        lax.fori_loop(0, nct, step, 0)

    def zs(col):
        return pl.BlockSpec((SEQ, WC), lambda b: (b, col // WC))

    seq = pl.BlockSpec((SEQ, WC), lambda b: (b, 0))
    return pl.pallas_call(
        body, grid=(BL,),
        in_specs=[zs(COL_A), zs(COL_B), zs(COL_CG), pl.BlockSpec((32, WC), lambda b: (0, 0)),
                  pl.BlockSpec((8, WC), lambda b: (0, 0))],
        out_specs=[seq, seq],
        out_shape=[jax.ShapeDtypeStruct((T, WC), bf16), jax.ShapeDtypeStruct((T, WC), f32)],
        scratch_shapes=[pltpu.VMEM((SEQ + HALO, WC), f32)], name="conv_fwd",
        compiler_params=_params(("parallel",)))(z, z, z, dw32, cvec)


def _conv_bwd(z, u1, dcv, dz, dw32, cvec, BL, SEQ):
    nct = SEQ // CT

    def body(a_ref, b_ref, cg_ref, u1_ref, dcv_ref, dzin_ref, dw_ref, vec_ref, dz_ref, ddw_ref, dvec_ref,
             p_ref, q_ref, taps_ref):
        @pl.when(pl.program_id(0) == 0)
        def _():
            ddw_ref[...] = jnp.zeros_like(ddw_ref)
            dvec_ref[...] = jnp.zeros_like(dvec_ref)

        p_ref[pl.ds(0, HALO), :] = jnp.zeros((HALO, WC), f32)
        q_ref[pl.ds(SEQ, HALO), :] = jnp.zeros((HALO, WC), f32)

        def glu(c, carry):
            r0 = pl.multiple_of(c * CT, CT)
            p_ref[pl.ds(r0 + HALO, CT), :] = a_ref[pl.ds(r0, CT), :] * _sig(b_ref[pl.ds(r0, CT), :])
            return carry

        lax.fori_loop(0, nct, glu, 0)

        def step(c, carry):
            r0 = pl.multiple_of(c * CT, CT)
            u1 = u1_ref[pl.ds(r0, CT), :]
            xc = u1 - jnp.mean(u1, axis=-1, keepdims=True)
            rs = lax.rsqrt(jnp.mean(xc * xc, axis=-1, keepdims=True) + EPS)
            nrm = xc * rs
            u2 = nrm * vec_ref[1:2, :] + vec_ref[2:3, :]
            s2 = _sig(u2)
            u3 = u2 * s2
            cg = cg_ref[pl.ds(r0, CT), :]
            scg = _sig(cg)
            dcv_ = dcv_ref[pl.ds(r0, CT), :]
            dz_ref[2, pl.ds(r0, CT), :] = (dcv_ * u3 * _dsilu(cg, scg)).astype(bf16)
            du2 = dcv_ * (cg * scg) * _dsilu(u2, s2)
            dvec_ref[1:2, :] += _colsum(du2 * nrm)
            dvec_ref[2:3, :] += _colsum(du2)
            dn = du2 * vec_ref[1:2, :]
            du1 = rs * (dn - jnp.mean(dn, axis=-1, keepdims=True)
                        - nrm * jnp.mean(dn * nrm, axis=-1, keepdims=True))
            dvec_ref[0:1, :] += _colsum(du1)
            q_ref[pl.ds(r0, CT), :] = du1
            for c0 in range(0, WC, CONV_LANES):
                rolled = _conv_rolls(p_ref[pl.ds(r0, CT + HALO), c0:c0 + CONV_LANES])
                du1_part = du1[:, c0:c0 + CONV_LANES]
                for d, a, b in _conv_delays():
                    prod = du1_part * rolled[b][HALO - 8 * a:HALO - 8 * a + CT, :]
                    taps_ref[CONV_K - 1 - d, :, c0:c0 + CONV_LANES] += jnp.sum(
                        prod.reshape(CT // 8, 8, CONV_LANES), axis=0)
            return carry

        taps_ref[...] = jnp.zeros_like(taps_ref)
        lax.fori_loop(0, nct, step, 0)
        for row in range(CONV_K):
            ddw_ref[pl.ds(row, 1), :] += _colsum(taps_ref[row])

        def back(c, carry):
            r0 = pl.multiple_of(c * CT, CT)
            parts = []
            for c0 in range(0, WC, CONV_LANES):
                wq = q_ref[pl.ds(r0, CT + HALO), c0:c0 + CONV_LANES]
                up = {}
                part = None
                for d, a, b in _conv_delays():
                    if b not in up:
                        up[b] = wq if b == 0 else pltpu.roll(wq, CT + HALO - b, axis=0)
                    term = up[b][8 * a:8 * a + CT, :] * dw_ref[pl.ds(CONV_K - 1 - d, 1), c0:c0 + CONV_LANES]
                    part = term if part is None else part + term
                parts.append(part)
            acc = jnp.concatenate(parts, axis=1)
            a_ = a_ref[pl.ds(r0, CT), :]
            sb = _sig(b_ref[pl.ds(r0, CT), :])
            dz_ref[0, pl.ds(r0, CT), :] = (acc * sb).astype(bf16)
            dz_ref[1, pl.ds(r0, CT), :] = (acc * a_ * sb * (1.0 - sb)).astype(bf16)
            return carry

        lax.fori_loop(0, nct, back, 0)

    def const(r):
        return pl.BlockSpec((r, WC), lambda b: (0, 0))

    first = [pl.BlockSpec((SEQ, WC), lambda b, col=col: (b, col // WC)) for col in (COL_A, COL_B)]
    seq = pl.BlockSpec((SEQ, WC), lambda b: (b, 0), pipeline_mode=pl.Buffered(1))
    return pl.pallas_call(
        body, grid=(BL,),
        in_specs=first + [pl.BlockSpec((SEQ, WC), lambda b: (b, COL_CG // WC), pipeline_mode=pl.Buffered(1)),
                          seq, seq, ANY, const(32), const(8)],
        out_specs=[pl.BlockSpec((3, SEQ, WC), lambda b: (DZ_CONV // 3, b, 0)), const(32), const(8)],
        out_shape=[jax.ShapeDtypeStruct(dz.shape, bf16), jax.ShapeDtypeStruct((32, WC), f32),
                   jax.ShapeDtypeStruct((8, WC), f32)],
        scratch_shapes=[pltpu.VMEM((SEQ + HALO, WC), f32), pltpu.VMEM((SEQ + HALO, WC), f32),
                        pltpu.VMEM((CONV_K, 8, WC), f32)],
        input_output_aliases={5: 0}, name="conv_bwd",
        compiler_params=_params(("arbitrary",)))(z, z, z, u1, dcv, dz, dw32, cvec)


def _pool_counts(r0):
    t1 = r0 + 1 + lax.broadcasted_iota(jnp.int32, (CT, 1), 0)
    return [jnp.minimum(t1, w).astype(f32) for w in POOL_WINDOWS]


def _pool_sums(win, forward):
    n = CT + PHALO

    def sh(x, s):
        return pltpu.roll(x, (n - s) if forward else s, axis=0)

    s2 = win + sh(win, 1)
    s4 = s2[:, GD:] + sh(s2[:, GD:], 2)
    s8 = s4[:, GD:] + sh(s4[:, GD:], 4)
    s16 = s8[:, GD:] + sh(s8[:, GD:], 8)
    lo = 0 if forward else PHALO
    return [s[lo:lo + CT, :GD] for s in (s2, s4, s8, s16)]


def _pool_fwd(z, pw, pvec, BL, SEQ):
    T = BL * SEQ
    nct = SEQ // CT

    def body(pi_ref, pg_ref, pw_ref, vec_ref, o_ref, p_ref):
        p_ref[pl.ds(0, PHALO), :] = jnp.zeros((PHALO, WC), f32)

        def fill(c, carry):
            r0 = pl.multiple_of(c * CT, CT)
            p_ref[pl.ds(r0 + PHALO, CT), :] = pi_ref[pl.ds(r0, CT), :]
            return carry

        lax.fori_loop(0, nct, fill, 0)

        def step(c, carry):
            r0 = pl.multiple_of(c * CT, CT)
            sums = _pool_sums(p_ref[pl.ds(r0, CT + PHALO), :], False)
            cnt = _pool_counts(r0)
            pin = pi_ref[pl.ds(r0, CT), :]
            mixed = []
            for g in range(4):
                pooled = sums[g] / cnt[g] - pin[:, g * GD:(g + 1) * GD]
                mixed.append(jnp.dot(pooled.astype(bf16), pw_ref[g], preferred_element_type=f32))
            m0 = jnp.concatenate(mixed, axis=1) + vec_ref[0:1, :]
            pg = pg_ref[pl.ds(r0, CT), :]
            o_ref[pl.ds(r0, CT), :] = ((m0 * vec_ref[1:2, :]) * (pg * _sig(pg))).astype(bf16)
            return carry

        lax.fori_loop(0, nct, step, 0)

    def zs(col):
        return pl.BlockSpec((SEQ, WC), lambda b: (b, col // WC))

    return pl.pallas_call(
        body, grid=(BL,),
        in_specs=[zs(COL_PI), zs(COL_PG), pl.BlockSpec((4, GD, GD), lambda b: (0, 0, 0)),
                  pl.BlockSpec((8, WC), lambda b: (0, 0))],
        out_specs=pl.BlockSpec((SEQ, WC), lambda b: (b, 0)),
        out_shape=jax.ShapeDtypeStruct((T, WC), bf16),
        scratch_shapes=[pltpu.VMEM((SEQ + PHALO, WC), f32)], name="pool_fwd",
        compiler_params=_params(("parallel",)))(z, z, pw, pvec)


def _pool_bwd(z, dpl, dz, pw, pvec, BL, SEQ):
    nct = SEQ // CT

    def body(pi_ref, pg_ref, dpl_ref, dzin_ref, pw_ref, vec_ref, dz_ref, dpw_ref, dvec_ref, p_ref, e_ref, dp_ref):
        @pl.when(pl.program_id(0) == 0)
        def _():
            dpw_ref[...] = jnp.zeros_like(dpw_ref)
            dvec_ref[...] = jnp.zeros_like(dvec_ref)

        p_ref[pl.ds(0, PHALO), :] = jnp.zeros((PHALO, WC), f32)
        e_ref[pl.ds(SEQ, PHALO), :] = jnp.zeros((PHALO, WC), f32)

        def fill(c, carry):
            r0 = pl.multiple_of(c * CT, CT)
            p_ref[pl.ds(r0 + PHALO, CT), :] = pi_ref[pl.ds(r0, CT), :]
            return carry

        lax.fori_loop(0, nct, fill, 0)

        def step(c, carry):
            r0 = pl.multiple_of(c * CT, CT)
            sums = _pool_sums(p_ref[pl.ds(r0, CT + PHALO), :], False)
            cnt = _pool_counts(r0)
            pin = pi_ref[pl.ds(r0, CT), :]
            pooled = [(sums[g] / cnt[g] - pin[:, g * GD:(g + 1) * GD]).astype(bf16) for g in range(4)]
            m0 = jnp.concatenate(
                [jnp.dot(pooled[g], pw_ref[g], preferred_element_type=f32) for g in range(4)], axis=1) + vec_ref[0:1, :]
            scale = vec_ref[1:2, :]
            pg = pg_ref[pl.ds(r0, CT), :]
            spg = _sig(pg)
            dpl_ = dpl_ref[pl.ds(r0, CT), :]
            dmixed = dpl_ * (pg * spg)
            dz_ref[1, pl.ds(r0, CT), :] = (dpl_ * (m0 * scale) * _dsilu(pg, spg)).astype(bf16)
            dvec_ref[1:2, :] += _colsum(dmixed * m0)
            dm0 = dmixed * scale
            dvec_ref[0:1, :] += _colsum(dm0)
            dps, es = [], []
            for g in range(4):
                dm0g = dm0[:, g * GD:(g + 1) * GD].astype(bf16)
                dpw_ref[g] += lax.dot_general(pooled[g], dm0g, (((0,), (0,)), ((), ())), preferred_element_type=f32)
                dpg = lax.dot_general(dm0g, pw_ref[g], (((1,), (1,)), ((), ())), preferred_element_type=f32)
                dps.append(dpg)
                es.append(dpg / cnt[g])
            dp_ref[pl.ds(r0, CT), :] = jnp.concatenate(dps, axis=1)
            e_ref[pl.ds(r0, CT), :] = jnp.concatenate(es, axis=1)
            return carry

        lax.fori_loop(0, nct, step, 0)

        def back(c, carry):
            r0 = pl.multiple_of(c * CT, CT)
            fs = _pool_sums(e_ref[pl.ds(r0, CT + PHALO), :], True)
            dz_ref[0, pl.ds(r0, CT), :] = (jnp.concatenate(fs, axis=1) - dp_ref[pl.ds(r0, CT), :]).astype(bf16)
            return carry

        lax.fori_loop(0, nct, back, 0)

    def zs(col):
        return pl.BlockSpec((SEQ, WC), lambda b: (b, col // WC))

    return pl.pallas_call(
        body, grid=(BL,),
        in_specs=[zs(COL_PI), zs(COL_PG), pl.BlockSpec((SEQ, WC), lambda b: (b, 0)), ANY,
                  pl.BlockSpec((4, GD, GD), lambda b: (0, 0, 0)), pl.BlockSpec((8, WC), lambda b: (0, 0))],
        out_specs=[pl.BlockSpec((2, SEQ, WC), lambda b: (DZ_POOL // 2, b, 0)),
                   pl.BlockSpec((4, GD, GD), lambda b: (0, 0, 0)), pl.BlockSpec((8, WC), lambda b: (0, 0))],
        out_shape=[jax.ShapeDtypeStruct(dz.shape, bf16), jax.ShapeDtypeStruct((4, GD, GD), f32),
                   jax.ShapeDtypeStruct((8, WC), f32)],
        scratch_shapes=[pltpu.VMEM((SEQ + PHALO, WC), f32), pltpu.VMEM((SEQ + PHALO, WC), f32),
                        pltpu.VMEM((SEQ, WC), f32)],
        input_output_aliases={3: 0}, name="pool_bwd",
        compiler_params=_params(("arbitrary",)))(z, z, dpl, dz, pw, pvec)


def _attn_prologue(q_ref, k_ref, v_ref, qs0, qs1, kp, vp, SEQ):
    head0 = lax.broadcasted_iota(jnp.int32, (1, 2 * HEAD_DIM), 1) < HEAD_DIM
    kp[pl.ds(0, KEY_PAD), :] = jnp.zeros((KEY_PAD, 2 * HEAD_DIM), bf16)
    vp[pl.ds(0, KEY_PAD), :] = jnp.zeros((KEY_PAD, 2 * HEAD_DIM), bf16)

    def fill(g, carry):
        r0 = pl.multiple_of(g * QG, QG)
        q = q_ref[pl.ds(r0, QG), :] * (HEAD_DIM ** -0.5)
        qs0[pl.ds(r0, QG), :] = jnp.where(head0, q, 0.0).astype(bf16)
        qs1[pl.ds(r0, QG), :] = jnp.where(head0, 0.0, q).astype(bf16)
        kp[pl.ds(r0 + KEY_PAD, QG), :] = k_ref[pl.ds(r0, QG), :].astype(bf16)
        vp[pl.ds(r0 + KEY_PAD, QG), :] = v_ref[pl.ds(r0, QG), :].astype(bf16)
        return carry

    lax.fori_loop(0, SEQ // QG, fill, 0)
    return head0


def _attn_weights(qh, kw, bias):
    s = lax.dot_general(qh, kw, (((1,), (1,)), ((), ())), preferred_element_type=f32) + bias
    e = jnp.exp(s - jnp.max(s, axis=-1, keepdims=True))
    return e, 1.0 / jnp.sum(e, axis=-1, keepdims=True)


def _attn_fwd(z, bm, BL, SEQ):
    T = BL * SEQ
    W2 = 2 * HEAD_DIM

    def body(q_ref, k_ref, v_ref, ag_ref, bm_ref, at_ref, o_ref, e_ref, qs0, qs1, kp, vp):
        head0 = _attn_prologue(q_ref, k_ref, v_ref, qs0, qs1, kp, vp, SEQ)

        def group(g, carry):
            r0 = pl.multiple_of(g * QG, QG)
            kw = kp[pl.ds(r0, KW), :]
            vw = vp[pl.ds(r0, KW), :]
            variant = jnp.minimum(g, BIAS_VARIANTS - 1)
            outs = []
            for hh, qs in enumerate((qs0, qs1)):
                e, inv = _attn_weights(qs[pl.ds(r0, QG), :], kw, bm_ref[variant, hh])
                eb = e.astype(bf16)
                e_ref[pl.ds(r0, QG), hh * KW:(hh + 1) * KW] = eb
                outs.append(jnp.dot(eb, vw, preferred_element_type=f32) * inv)
            o = jnp.where(head0, outs[0], outs[1])
            o_ref[pl.ds(r0, QG), :] = o
            ag = ag_ref[pl.ds(r0, QG), :]
            at_ref[pl.ds(r0, QG), :] = (o * (ag * _sig(ag))).astype(bf16)
            return carry

        lax.fori_loop(0, SEQ // QG, group, 0, unroll=8)

    def zs(col):
        return pl.BlockSpec((SEQ, W2), lambda b, hp: (b, col // W2 + hp))

    pair = pl.BlockSpec((SEQ, W2), lambda b, hp: (b, hp))
    return pl.pallas_call(
        body, grid=(BL, WC // W2),
        in_specs=[zs(COL_Q), zs(COL_K), zs(COL_V), zs(COL_AG),
                  pl.BlockSpec((BIAS_VARIANTS, 2, QG, KW), lambda b, hp: (0, hp, 0, 0))],
        out_specs=[pair, pair, pl.BlockSpec((None, SEQ, 2 * KW), lambda b, hp: (hp, b, 0))],
        out_shape=[jax.ShapeDtypeStruct((T, WC), bf16), jax.ShapeDtypeStruct((T, WC), f32),
                   jax.ShapeDtypeStruct((WC // W2, T, 2 * KW), bf16)],
        scratch_shapes=[pltpu.VMEM((SEQ, W2), bf16), pltpu.VMEM((SEQ, W2), bf16),
                        pltpu.VMEM((SEQ + KEY_PAD, W2), bf16), pltpu.VMEM((SEQ + KEY_PAD, W2), bf16)],
        name="attn_fwd", compiler_params=_params(("parallel", "parallel")))(z, z, z, z, bm)


def _attn_bwd(z, dat, o, ew, dz, BL, SEQ):
    W2 = 2 * HEAD_DIM

    def body(q_ref, k_ref, v_ref, ag_ref, dat_ref, o_ref, e_ref, dzin_ref, dz_ref, dbm_ref, qs0, qs1, kp, vp, dka, dva):
        @pl.when(pl.program_id(1) == 0)
        def _():
            dbm_ref[...] = jnp.zeros_like(dbm_ref)

        head0 = _attn_prologue(q_ref, k_ref, v_ref, qs0, qs1, kp, vp, SEQ)
        dka[...] = jnp.zeros_like(dka)
        dva[...] = jnp.zeros_like(dva)

        def group(g, carry):
            r0 = pl.multiple_of(g * QG, QG)
            kw = kp[pl.ds(r0, KW), :]
            vw = vp[pl.ds(r0, KW), :]
            ag = ag_ref[pl.ds(r0, QG), :]
            do = dat_ref[pl.ds(r0, QG), :] * (ag * _sig(ag))
            dqs = []
            for hh, qs in enumerate((qs0, qs1)):
                qh = qs[pl.ds(r0, QG), :]
                eb = e_ref[pl.ds(r0, QG), hh * KW:(hh + 1) * KW]
                e = eb.astype(f32)
                inv = 1.0 / jnp.sum(e, axis=-1, keepdims=True)
                doh = (jnp.where(head0, do, 0.0) if hh == 0 else jnp.where(head0, 0.0, do)) * inv
                doh = doh.astype(bf16)
                dp = lax.dot_general(doh, vw, (((1,), (1,)), ((), ())), preferred_element_type=f32)
                ds_ = e * (dp - jnp.sum(e * dp, axis=-1, keepdims=True) * inv)
                dbm_ref[hh] += ds_
                dsb = ds_.astype(bf16)
                dqs.append(jnp.dot(dsb, kw, preferred_element_type=f32))
                dka[pl.ds(r0, KW), :] += lax.dot_general(dsb, qh, (((0,), (0,)), ((), ())), preferred_element_type=f32)
                dva[pl.ds(r0, KW), :] += lax.dot_general(eb, doh, (((0,), (0,)), ((), ())), preferred_element_type=f32)
            dq = jnp.where(head0, dqs[0], dqs[1]) * (HEAD_DIM ** -0.5)
            dz_ref[0, pl.ds(r0, QG), :] = dq.astype(bf16)
            dz_ref[3, pl.ds(r0, QG), :] = (dat_ref[pl.ds(r0, QG), :] * o_ref[pl.ds(r0, QG), :]
                                           * _dsilu(ag, _sig(ag))).astype(bf16)
            return carry

        lax.fori_loop(0, SEQ // QG, group, 0, unroll=8)

        def flush(g, carry):
            r0 = pl.multiple_of(g * QG, QG)
            dz_ref[1, pl.ds(r0, QG), :] = dka[pl.ds(r0 + KEY_PAD, QG), :].astype(bf16)
            dz_ref[2, pl.ds(r0, QG), :] = dva[pl.ds(r0 + KEY_PAD, QG), :].astype(bf16)
            return carry

        lax.fori_loop(0, SEQ // QG, flush, 0)

    def zs(col):
        return pl.BlockSpec((SEQ, W2), lambda hp, b: (b, col // W2 + hp))

    pair = pl.BlockSpec((SEQ, W2), lambda hp, b: (b, hp))
    return pl.pallas_call(
        body, grid=(WC // W2, BL),
        in_specs=[zs(COL_Q), zs(COL_K), zs(COL_V), zs(COL_AG), pair, pair,
                  pl.BlockSpec((None, SEQ, 2 * KW), lambda hp, b: (hp, b, 0)), ANY],
        out_specs=[pl.BlockSpec((4, SEQ, W2), lambda hp, b: (DZ_ATTN // 4, b, hp)),
                   pl.BlockSpec((2, QG, KW), lambda hp, b: (hp, 0, 0))],
        out_shape=[jax.ShapeDtypeStruct(dz.shape, bf16), jax.ShapeDtypeStruct((8, QG, KW), f32)],
        scratch_shapes=[pltpu.VMEM((SEQ, W2), bf16), pltpu.VMEM((SEQ, W2), bf16),
                        pltpu.VMEM((SEQ + KEY_PAD, W2), bf16), pltpu.VMEM((SEQ + KEY_PAD, W2), bf16),
                        pltpu.VMEM((SEQ + KEY_PAD, W2), f32), pltpu.VMEM((SEQ + KEY_PAD, W2), f32)],
        input_output_aliases={7: 0}, name="attn_bwd",
        compiler_params=_params(("parallel", "arbitrary")))(z, z, z, z, dat, o, ew, dz)


BIAS_TOP = KEY_PAD + MAX_REL + QG - 1


def _bias_matrix(table):
    n = 2 * MAX_REL
    wd = QG + KW
    e = jnp.concatenate([jnp.broadcast_to(table[:, n:], (8, BIAS_TOP - n + 1)), table[:, n - 1:BIAS_TOP - wd + 1:-1],
                         jnp.zeros((8, 1), f32)], axis=1)
    flat = jnp.broadcast_to(e[:, None, :], (8, QG, wd)).reshape(8, QG * wd)
    skew = flat[:, :QG * (wd - 1)].reshape(8, QG, wd - 1)
    vals = skew[:, :, QG - 1:QG - 1 + KW]
    r = np.arange(QG)[:, None] // CHUNK
    j = np.arange(KW)[None, :]
    band = (j // CHUNK >= r) & (j // CHUNK <= r + LEFT_CHUNKS)
    keep = np.stack([band & (j >= KEY_PAD - v * QG) for v in range(BIAS_VARIANTS)])
    return jnp.where(jnp.asarray(keep)[:, None], vals[None], NEG_INF)


def _bias_fold(dbm):
    wd = QG + KW
    placed = jnp.pad(dbm, ((0, 0), (0, 0), (QG - 1, 0))).reshape(8, QG * (wd - 1))
    return jnp.pad(placed, ((0, 0), (0, QG))).reshape(8, QG, wd)


def _bias_colsum(folded):
    width = folded.shape[2]

    def body(x_ref, o_ref):
        for h in range(8):
            o_ref[pl.ds(h, 1), :] = _colsum(x_ref[h])

    return pl.pallas_call(body, out_shape=jax.ShapeDtypeStruct((8, width), f32), name="bias_colsum",
                          compiler_params=_params())(folded)


def _bias_table_grad(colsum):
    n = 2 * MAX_REL
    wd = QG + KW
    clipped = jnp.sum(colsum[:, :BIAS_TOP - n + 1], axis=1, keepdims=True)
    return jnp.concatenate([jnp.zeros((8, BIAS_TOP - wd + 2), f32), colsum[:, wd - 2:BIAS_TOP - n:-1], clipped], axis=1)


GATE_SPAN = 3 * WC
TAIL_ROWS = 3 * WC + D


def _gate_specs(tm):
    return [pl.BlockSpec((tm, GATE_SPAN), lambda i: (i, COL_GM // GATE_SPAN)),
            pl.BlockSpec((tm, GATE_SPAN), lambda i: (i, COL_GM // GATE_SPAN + 1))]


def _gate_block(ga_ref, gb_ref, branch, half):
    k = 2 * branch + half
    ref, k = (ga_ref, k) if k < 3 else (gb_ref, k - 3)
    return _sig(ref[:, k * WC:(k + 1) * WC])


def _resident(shape):
    return pl.BlockSpec(shape, lambda i: (0,) * len(shape), pipeline_mode=pl.Buffered(1))


def _tail_fwd(z, acts, x2, lw, next_g=None, tgt=None):
    T = z.shape[0]
    tm = 256
    with_loss = tgt is not None
    assert with_loss != (next_g is not None)

    def body(cv_ref, at_ref, pv_ref, ga_ref, gb_ref, x_ref, wc_ref, wa_ref, wp_ref, wo_ref, g_ref, *rest):
        out_ref, merged_ref, y_ref = rest[1:4]
        ys = [jnp.dot(a[...], w[...], preferred_element_type=f32)
              for a, w in ((cv_ref, wc_ref), (at_ref, wa_ref), (pv_ref, wp_ref))]
        halves = []
        for half in range(2):
            cols = slice(half * WC, (half + 1) * WC)
            halves.append(sum(_gate_block(ga_ref, gb_ref, br, half) * ys[br][:, cols] for br in range(3)))
        merged = jnp.concatenate(halves, axis=1).astype(bf16)
        merged_ref[...] = merged
        y = jnp.dot(merged, wo_ref[...], preferred_element_type=f32)
        y_ref[...] = y
        r = lax.rsqrt(jnp.mean(y * y, axis=-1, keepdims=True) + EPS)
        out = x_ref[...] + (y * r) * g_ref[...]
        if with_loss:
            sq_ref = rest[4]
            e = out - rest[0][...]
            out_ref[...] = e / float(D)

            @pl.when(pl.program_id(0) == 0)
            def _():
                sq_ref[...] = jnp.zeros_like(sq_ref)

            sq_ref[...] += _colsum(e * e)
        else:
            out_ref[...] = out
            rn = lax.rsqrt(jnp.mean(out * out, axis=-1, keepdims=True) + EPS)
            h = (out * rn) * rest[0][...]
            rest[4][...] = h.astype(bf16)
            rest[5][...] = h.T.astype(bf16)

    act = pl.BlockSpec((tm, WC), lambda i: (i, 0))
    row = pl.BlockSpec((tm, D), lambda i: (i, 0))
    vec = pl.BlockSpec((1, D), lambda i: (0, 0))
    if with_loss:
        last_in, last_specs, last_shapes = tgt, [row, [vec]], [jax.ShapeDtypeStruct((1, D), f32)]
    else:
        last_in, last_specs = next_g, [vec, [row, pl.BlockSpec((D, tm), lambda i: (0, i))]]
        last_shapes = [jax.ShapeDtypeStruct((T, D), bf16), jax.ShapeDtypeStruct((D, T), bf16)]
    return pl.pallas_call(
        body, grid=(T // tm,),
        in_specs=[act, act, act] + _gate_specs(tm) + [row, _resident((WC, D)), _resident((WC, D)), _resident((WC, D)),
                                                      _resident((D, D)), _resident((1, D)), last_specs[0]],
        out_specs=[row, row, row] + last_specs[1],
        out_shape=[jax.ShapeDtypeStruct((T, D), f32), jax.ShapeDtypeStruct((T, D), bf16), jax.ShapeDtypeStruct((T, D), f32)]
        + last_shapes,
        name="tail_fwd", compiler_params=_params(("arbitrary",)))(
            *acts, z, z, x2, lw["w_conv_out"], lw["w_attn_out"], lw["w_pool_out"], lw["w_out"], lw["post_g"], last_in)


def _tail_bwd(z, dout, y, merged, acts, lw):
    T = z.shape[0]
    tm = 256
    nt = (((1,), (1,)), ((), ()))
    tn = (((0,), (0,)), ((), ()))

    def body(d_ref, y_ref, m_ref, cv_ref, at_ref, pv_ref, ga_ref, gb_ref, wc_ref, wa_ref, wp_ref, wo_ref, g_ref,
             dz_ref, dcv_ref, dat_ref, dpv_ref, dg_ref, dw_ref):
        @pl.when(pl.program_id(0) == 0)
        def _():
            dg_ref[...] = jnp.zeros_like(dg_ref)
            dw_ref[...] = jnp.zeros_like(dw_ref)

        y = y_ref[...]
        d = d_ref[...]
        r = lax.rsqrt(jnp.mean(y * y, axis=-1, keepdims=True) + EPS)
        yn = y * r
        dyn = d * g_ref[...]
        dy = (r * (dyn - yn * jnp.mean(dyn * yn, axis=-1, keepdims=True))).astype(bf16)
        dg_ref[...] += _colsum(d * yn)
        dw_ref[pl.ds(3 * WC, D), :] += lax.dot_general(dy, m_ref[...], tn, preferred_element_type=f32)
        dmerged = lax.dot_general(dy, wo_ref[...], nt, preferred_element_type=f32)
        for br, (a_ref, w_ref, da_ref) in enumerate(((cv_ref, wc_ref, dcv_ref), (at_ref, wa_ref, dat_ref),
                                                     (pv_ref, wp_ref, dpv_ref))):
            yb = jnp.dot(a_ref[...], w_ref[...], preferred_element_type=f32)
            halves = []
            for half in range(2):
                cols = slice(half * WC, (half + 1) * WC)
                s = _gate_block(ga_ref, gb_ref, br, half)
                dm = dmerged[:, cols]
                halves.append((dm * s).astype(bf16))
                dz_ref[2 * br + half] = (dm * yb[:, cols] * s * (1.0 - s)).astype(bf16)
            dyb = jnp.concatenate(halves, axis=1)
            da_ref[...] = lax.dot_general(dyb, w_ref[...], nt, preferred_element_type=f32)
            dw_ref[pl.ds(br * WC, WC), :] += lax.dot_general(a_ref[...], dyb, tn, preferred_element_type=f32)

    act = pl.BlockSpec((tm, WC), lambda i: (i, 0))
    row = pl.BlockSpec((tm, D), lambda i: (i, 0))

    def whole(shape):
        return pl.BlockSpec(shape, lambda i: (0, 0))

    return pl.pallas_call(
        body, grid=(T // tm,),
        in_specs=[row, row, row, act, act, act] + _gate_specs(tm) + [_resident((WC, D)), _resident((WC, D)), _resident((WC, D)),
                                                                     _resident((D, D)), _resident((1, D))],
        out_specs=[pl.BlockSpec((6, tm, WC), lambda i: (DZ_GM // 6, i, 0)), act, act, act, whole((1, D)),
                   whole((TAIL_ROWS, D))],
        out_shape=[jax.ShapeDtypeStruct((DZ_BLOCKS, T, WC), bf16)] + [jax.ShapeDtypeStruct((T, WC), f32)] * 3
        + [jax.ShapeDtypeStruct((1, D), f32), jax.ShapeDtypeStruct((TAIL_ROWS, D), f32)],
        name="tail_bwd", compiler_params=_params(("arbitrary",)))(
            dout, y, merged, *acts, z, z, lw["w_conv_out"], lw["w_attn_out"], lw["w_pool_out"], lw["w_out"], lw["post_g"])


def _adamw(name, g, w, m, v):
    R, C = w.shape
    tr = R
    for cand in (512, 256, 248, 128, 64, 32, 16, 8):
        if R % cand == 0 and cand * C * 4 <= 2 * 1024 * 1024:
            tr = cand
            break
    c1 = 1.0 - ADAM_B1
    c2 = 1.0 - ADAM_B2
    bc1 = 1.0 - ADAM_B1 ** ADAM_STEP
    bc2 = 1.0 - ADAM_B2 ** ADAM_STEP

    def body(g_ref, w_ref, m_ref, v_ref, d_ref, nm_ref, nv_ref):
        g_ = g_ref[...]
        nm = ADAM_B1 * m_ref[...] + c1 * g_
        nv = ADAM_B2 * v_ref[...] + c2 * (g_ * g_)
        nm_ref[...] = nm
        nv_ref[...] = nv
        d_ref[...] = -ADAM_LR * ((nm / bc1) / (jnp.sqrt(nv / bc2) + ADAM_EPS) + ADAM_WD * w_ref[...])

    spec = pl.BlockSpec((tr, C), lambda i: (i, 0))
    return pl.pallas_call(
        body, grid=(R // tr,), in_specs=[spec] * 4, out_specs=[spec] * 3,
        out_shape=[jax.ShapeDtypeStruct((R, C), f32)] * 3, name=name,
        compiler_params=_params(("parallel",)))(g, w, m, v)


def _sum_slots(name, parts):
    _, R, C = parts.shape
    tr = R
    for cand in (256, 128, 64, 32, 16, 8):
        if R % cand == 0 and cand * C * 4 * N_DEV <= 8 * 1024 * 1024:
            tr = cand
            break

    def body(p_ref, o_ref):
        acc = p_ref[0].astype(f32)
        for s in range(1, N_DEV):
            acc = acc + p_ref[s].astype(f32)
        o_ref[...] = acc

    return pl.pallas_call(
        body, grid=(R // tr,), in_specs=[pl.BlockSpec((N_DEV, tr, C), lambda i: (0, i, 0))],
        out_specs=pl.BlockSpec((tr, C), lambda i: (i, 0)), out_shape=jax.ShapeDtypeStruct((R, C), f32),
        name=name, compiler_params=_params(("parallel",)))(parts)


def _row_tile(rows, row_bytes, budget):
    for cand in (512, 256, 128, 64, 32, 16):
        if rows % cand == 0 and cand * row_bytes <= budget:
            return cand
    return rows


def _pair_sum(core, g, theirs):
    R2, C4 = theirs.shape
    tr = _row_tile(R2, C4 * 2, 2 * 1024 * 1024)
    nb = R2 // tr

    def body(core_ref, g_ref, t_ref, o_ref):
        o_ref[...] = (g_ref[...].astype(f32) + t_ref[...].astype(f32)).astype(bf16)

    return pl.pallas_call(
        body,
        grid_spec=pltpu.PrefetchScalarGridSpec(
            num_scalar_prefetch=1, grid=(nb,),
            in_specs=[pl.BlockSpec((tr, C4), lambda i, core_ref: (core_ref[0] * nb + i, 0)),
                      pl.BlockSpec((tr, C4), lambda i, core_ref: (i, 0))],
            out_specs=pl.BlockSpec((tr, C4), lambda i, core_ref: (i, 0))),
        out_shape=jax.ShapeDtypeStruct((R2, C4), bf16), name="pair_sum",
        compiler_params=_params(("parallel",)))(core, g, theirs)


def _chip_sum(chip, mine, others):
    _, R2, C = others.shape
    tr = _row_tile(R2, C * 4, 1024 * 1024)

    def body(chip_ref, m_ref, o_ref, out_ref):
        acc = m_ref[...].astype(f32)
        for s in range(N_CHIPS - 1):
            acc = acc + o_ref[s].astype(f32)
        out_ref[...] = acc

    return pl.pallas_call(
        body,
        grid_spec=pltpu.PrefetchScalarGridSpec(
            num_scalar_prefetch=1, grid=(R2 // tr,),
            in_specs=[pl.BlockSpec((tr, C), lambda i, chip_ref: (i, chip_ref[0])),
                      pl.BlockSpec((N_CHIPS - 1, tr, C), lambda i, chip_ref: (0, i, 0))],
            out_specs=pl.BlockSpec((tr, C), lambda i, chip_ref: (i, 0))),
        out_shape=jax.ShapeDtypeStruct((R2, C), f32), name="chip_sum",
        compiler_params=_params(("parallel",)))(chip, mine, others)


def _place():
    x, y, c = lax.axis_index("x"), lax.axis_index("y"), lax.axis_index("c")
    return x, y, c


def _flip(v, bit):
    return 1 - v if bit else v


CHIP_FLIPS = ((1, 0), (0, 1), (1, 1))


class _Sems:
    def __init__(self, send, recv):
        self.send, self.recv = send, recv
        self.pairs = 0

    def pair(self):
        k = self.pairs
        self.pairs += 1
        return self.send.at[k], self.recv.at[k]


def _remote(src, dst, lands, sems, to):
    s, r = sems.pair()
    copy = pltpu.make_async_remote_copy(src_ref=src, dst_ref=dst, send_sem=s, recv_sem=r, device_id=to, device_id_type=MESH)
    wait = pltpu.make_async_remote_copy(src_ref=lands, dst_ref=lands, send_sem=s, recv_sem=r, device_id=to, device_id_type=MESH)
    return copy, wait


def _exchange(name, build, srcs, lands, n_remote):
    n_s, n_l = len(srcs), len(lands)

    def body(*refs):
        send, recv = refs[n_s + 2 * n_l:]
        remotes, recvs = build(refs[:n_s], refs[n_s + n_l:n_s + 2 * n_l], _Sems(send, recv))
        for cp in remotes:
            cp.start()
        for rv in recvs:
            rv.wait_recv()
        for cp in remotes:
            cp.wait_send()

    return pl.pallas_call(
        body, in_specs=[ANY] * (n_s + n_l), out_specs=[ANY] * n_l,
        out_shape=[jax.ShapeDtypeStruct(t.shape, t.dtype) for t in lands],
        scratch_shapes=[pltpu.SemaphoreType.DMA((n_remote,)), pltpu.SemaphoreType.DMA((n_remote,))],
        input_output_aliases={n_s + i: i for i in range(n_l)}, name=name)(*srcs, *lands)


HBM = pl.BlockSpec(memory_space=pltpu.HBM)
SEMS = pl.BlockSpec(memory_space=pltpu.SEMAPHORE)
DATAFLOW = pltpu.SideEffectType.DATAFLOW_SIDE_EFFECTING


def _start(name, build, srcs, lands, n_remote, after):
    n_s, n_l = len(srcs), len(lands)

    def body(*refs):
        send, recv = refs[n_s + n_l + 1], refs[n_s + n_l + 2]
        remotes, _ = build(refs[:n_s], refs[n_s:n_s + n_l], _Sems(send, recv))
        for cp in remotes:
            cp.start()
        refs[-1][...] = jnp.zeros((8, 128), f32)

    arrays = [pltpu.with_memory_space_constraint(a, pltpu.HBM) for a in (*srcs, *lands)]
    out = pl.pallas_call(
        body, name=name, in_specs=[HBM] * (n_s + n_l) + [ANY],
        out_specs=(SEMS, SEMS, *[HBM] * (n_s + n_l), pl.BlockSpec(memory_space=pltpu.VMEM)),
        out_shape=(pltpu.SemaphoreType.DMA((n_remote,)), pltpu.SemaphoreType.DMA((n_remote,)),
                   *[pltpu.HBM(a.shape, a.dtype) for a in arrays], jax.ShapeDtypeStruct((8, 128), f32)),
        input_output_aliases={i: 2 + i for i in range(n_s + n_l)},
        compiler_params=pltpu.CompilerParams(has_side_effects=DATAFLOW))(*arrays, after)
    return dict(name=name, build=build, sems=out[:2], srcs=out[2:2 + n_s], lands=out[2 + n_s:2 + n_s + n_l], token=out[-1])


def _wait(started, after):
    srcs, lands, build = started["srcs"], started["lands"], started["build"]
    n_s, n_l = len(srcs), len(lands)
    after = list(after) if isinstance(after, (list, tuple)) else [after]

    def body(*refs):
        send, recv = refs[n_s + n_l], refs[n_s + n_l + 1]
        remotes, recvs = build(refs[:n_s], refs[n_s:n_s + n_l], _Sems(send, recv))
        for rv in recvs:
            rv.wait_recv()
        for cp in remotes:
            cp.wait_send()

    out = pl.pallas_call(
        body, name=started["name"] + "_wait", in_specs=[HBM] * (n_s + n_l) + [SEMS, SEMS] + [ANY] * len(after),
        out_specs=[HBM] * (n_s + n_l), out_shape=[pltpu.HBM(a.shape, a.dtype) for a in (*srcs, *lands)],
        input_output_aliases={i: i for i in range(n_s + n_l)},
        compiler_params=pltpu.CompilerParams(has_side_effects=DATAFLOW))(*srcs, *lands, *started["sems"], *after)
    return out[:n_s], out[n_s:]


def _gather_plans(n_split, n_all):
    def over_ici(src, land, sems):
        x, y, c = _place()
        chip = 2 * x + y
        remotes, recvs = [], []
        for a in range(n_all):
            for fx, fy in CHIP_FLIPS:
                px, py = _flip(x, fx), _flip(y, fy)
                if a < n_split:
                    r2 = src[a].shape[0] // 2
                    rows = pl.ds(c * r2, r2)
                    cp, rv = _remote(src[a].at[rows], land[a].at[chip, rows], land[a].at[2 * px + py, rows], sems, (px, py, c))
                else:
                    cp, rv = _remote(src[a], land[a].at[chip], land[a].at[2 * px + py], sems, (px, py, c))
                remotes.append(cp)
                recvs.append(rv)
        return remotes, recvs

    def over_d2d(src, land, sems):
        x, y, c = _place()
        remotes, recvs = [], []
        for a in range(n_split):
            r2 = land[a].shape[1] // 2
            for fx, fy in CHIP_FLIPS:
                owner = 2 * _flip(x, fx) + _flip(y, fy)
                mine = land[a].at[owner, pl.ds(c * r2, r2)]
                cp, rv = _remote(mine, mine, land[a].at[owner, pl.ds((1 - c) * r2, r2)], sems, (x, y, 1 - c))
                remotes.append(cp)
                recvs.append(rv)
        return remotes, recvs

    return over_ici, over_d2d


def _gather_begin(tag, shards, n_split, after):
    over_ici, _ = _gather_plans(n_split, len(shards))
    lands = [lax.empty((N_CHIPS,) + s.shape, s.dtype) for s in shards]
    return _start("gather_ici_" + tag, over_ici, shards, lands, 3 * len(shards), after)


def _gather_end(started, shards, n_split, after):
    _, over_d2d = _gather_plans(n_split, len(shards))
    lands = _exchange("gather_d2d", over_d2d, [], _wait(started, after)[1], 3 * n_split)
    chip = 2 * lax.axis_index("x") + lax.axis_index("y")
    return [lax.dynamic_update_slice_in_dim(g, s[None], chip, axis=0) for g, s in zip(lands, shards)]


def _reduce_plans(n):
    def to_sibling(src, land, sems):
        x, y, c = _place()
        remotes, recvs = [], []
        for a in range(n):
            r2 = src[a].shape[0] // 2
            cp, rv = _remote(src[a].at[pl.ds((1 - c) * r2, r2), :], land[a], land[a], sems, (x, y, 1 - c))
            remotes.append(cp)
            recvs.append(rv)
        return remotes, recvs

    def across_chips(src, land, sems):
        x, y, c = _place()
        remotes, recvs = [], []
        for a in range(n):
            cw = src[a].shape[1] // N_CHIPS
            for k, (fx, fy) in enumerate(CHIP_FLIPS):
                px, py = _flip(x, fx), _flip(y, fy)
                cp, rv = _remote(src[a].at[:, pl.ds((2 * px + py) * cw, cw)], land[a].at[k], land[a].at[k], sems, (px, py, c))
                remotes.append(cp)
                recvs.append(rv)
        return remotes, recvs

    def share(src, land, sems):
        x, y, c = _place()
        remotes, recvs = [], []
        for a in range(n):
            cp, rv = _remote(src[a], land[a], land[a], sems, (x, y, 1 - c))
            remotes.append(cp)
            recvs.append(rv)
        return remotes, recvs

    return to_sibling, across_chips, share


def _reduce_begin(grads):
    n = len(grads)
    to_sibling, across_chips, _ = _reduce_plans(n)
    core = lax.axis_index("c").reshape(1).astype(jnp.int32)
    theirs = _exchange("reduce_pair", to_sibling, grads,
                       [lax.empty((g.shape[0] // 2, g.shape[1]), bf16) for g in grads], n)
    pair = [_pair_sum(core, g, t) for g, t in zip(grads, theirs)]
    lands = [lax.empty((N_CHIPS - 1, g.shape[0] // 2, g.shape[1] // N_CHIPS), bf16) for g in grads]
    return _start("reduce_chips", across_chips, pair, lands, 3 * n, pair[0])


def _reduce_end(started, after):
    x, y, c = _place()
    chip = (2 * x + y).reshape(1).astype(jnp.int32)
    pair, others = _wait(started, after)
    _, _, share = _reduce_plans(len(pair))
    mine = [_chip_sum(chip, p, o) for p, o in zip(pair, others)]
    sibs = _exchange("reduce_share", share, mine, [lax.empty(h.shape, f32) for h in mine], len(mine))
    return [jnp.where(c == 0, jnp.concatenate([h, s], axis=0), jnp.concatenate([s, h], axis=0))
            for h, s in zip(mine, sibs)]


def _to_all(src, land, sems):
    x, y, c = _place()
    me = 4 * x + 2 * y + c
    remotes, recvs = [], []
    for k in range(1, N_DEV):
        px, py, pc = _flip(x, (k >> 2) & 1), _flip(y, (k >> 1) & 1), _flip(c, k & 1)
        cp, rv = _remote(src[0], land[0].at[me], land[0].at[4 * px + 2 * py + pc], sems, (px, py, pc))
        remotes.append(cp)
        recvs.append(rv)
    return remotes, recvs


def _gather_small_begin(packed):
    return _start("gather_small", _to_all, [packed], [lax.empty((N_DEV,) + packed.shape, f32)], N_DEV - 1, packed)


def _gather_small_end(started, after):
    (packed,), (others,) = _wait(started, after)
    x, y, c = _place()
    return lax.dynamic_update_slice_in_dim(others, packed[None], 4 * x + 2 * y + c, axis=0)


def _gather_all(packed):
    others = _exchange("gather_all", _to_all, [packed], [lax.empty((N_DEV,) + packed.shape, f32)], N_DEV - 1)[0]
    x, y, c = _place()
    return lax.dynamic_update_slice_in_dim(others, packed[None], 4 * x + 2 * y + c, axis=0)


def _rows8(v):
    return jnp.pad(v[None, :], ((0, 7), (0, 0)))


def _vec_rows(vs):
    return jnp.pad(jnp.stack(vs), ((0, 8 - len(vs)), (0, 0)))


SMALL_ROWS = 224


def _pack_small(conv_vec, conv_dw, pool_vec, pool_w, pre_g, post_g, rel):
    return jnp.concatenate([
        conv_vec, conv_dw, pool_vec, pool_w.reshape(GD, WC),
        _rows8(pre_g).reshape(16, WC), _rows8(post_g).reshape(16, WC),
        jnp.pad(rel, ((0, 0), (0, D - rel.shape[1]))).reshape(16, WC)], axis=0)


def _unpack_small(p):
    conv_vec, pool_vec = p[0:8], p[40:48]
    return dict(
        conv_dw_b=conv_vec[0], conv_ln_g=conv_vec[1], conv_ln_b=conv_vec[2], conv_dw=p[8:8 + CONV_K],
        pool_b=pool_vec[0].reshape(4, GD), pool_scale=pool_vec[1], pool_w=p[48:176].reshape(4, GD, GD),
        pre_norm_g=p[176:192].reshape(8, D)[0], post_norm_g=p[192:208].reshape(8, D)[0],
        rel_bias=p[208:224].reshape(8, D)[:, :2 * MAX_REL + 1])


def _layer_fwd(x2, ht, z, lw, BL, SEQ, next_g=None, tgt=None):
    cv, u1 = _conv_fwd(z, lw["dw32"], lw["cvec"], BL, SEQ)
    at, attn_o, attn_e = _attn_fwd(z, lw["bm"], BL, SEQ)
    pv = _pool_fwd(z, lw["pw"], lw["pvec"], BL, SEQ)
    out, merged, y, *last = _tail_fwd(z, (cv, at, pv), x2, lw, next_g, tgt)
    saved = dict(x=x2, ht=ht, z=z, u1=u1, attn_o=attn_o, attn_e=attn_e, acts=(cv, at, pv), merged=merged, y=y)
    return (out, *last), saved


def _layer_bwd(dout, sv, lw, BL, SEQ, meanwhile=None):
    tail = _tail_bwd(sv["z"], dout, sv["y"], sv["merged"], sv["acts"], lw)
    dz, dacts, dpost = tail[0], tail[1:4], tail[4]
    dw_tail = tail[5].astype(bf16)
    if meanwhile is not None:
        meanwhile(dw_tail)
    dz, ddw, dcvec = _conv_bwd(sv["z"], sv["u1"], dacts[0], dz, lw["dw32"], lw["cvec"], BL, SEQ)
    dz, dbm = _attn_bwd(sv["z"], dacts[1], sv["attn_o"], sv["attn_e"], dz, BL, SEQ)
    dz, dpw, dpvec = _pool_bwd(sv["z"], dacts[2], dz, lw["pw"], lw["pvec"], BL, SEQ)
    drel = _bias_table_grad(_bias_colsum(_bias_fold(dbm)))
    small_gather = _gather_small_begin(_pack_small(dcvec, ddw, dpvec, dpw, jnp.zeros((D,), f32), dpost[0], drel))
    dw_in = _mm_dw_in(sv["ht"], dz, small_gather["token"])
    reduction = _reduce_begin([dw_in, dw_tail])
    dx, dpre = _mm_dx(dz, lw["w_in"], sv["x"], lw["pre_g"], dout, reduction["token"])
    return dx, reduction, small_gather, dpre


BIG = ("w_in", "w_conv_out", "w_attn_out", "w_pool_out", "w_out")
PRE_ROWS = slice(176, 192)


def _layer_shards(w, l):
    return [w[k][l].astype(bf16) for k in BIG] + [w["conv_dw"][l]]


def _side_by_side(g):
    return jnp.transpose(g, (1, 0, 2)).reshape(g.shape[1], N_CHIPS * g.shape[2])


def _layer_weights(w_in, gathered, w, l, bm):
    lw = {k: _side_by_side(g) for k, g in zip(BIG[1:4], gathered[:3])}
    lw["w_in"] = w_in
    lw["w_out"] = gathered[3].reshape(D, D)
    lw["pre_g"] = w["pre_norm_g"][l][None]
    lw["post_g"] = w["post_norm_g"][l][None]
    lw["dw32"] = jnp.pad(_side_by_side(gathered[4]), ((0, 32 - CONV_K), (0, 0)))
    lw["cvec"] = _vec_rows([w["conv_dw_b"][l], w["conv_ln_g"][l], w["conv_ln_b"][l]])
    lw["bm"] = bm
    lw["pw"] = w["pool_w"][l].astype(bf16)
    lw["pvec"] = _vec_rows([w["pool_b"][l].reshape(WC), w["pool_scale"][l]])
    return lw


SMALL = ("pre_norm_g", "post_norm_g", "conv_dw_b", "conv_ln_g", "conv_ln_b", "rel_bias", "pool_w", "pool_b", "pool_scale")
ORDER = ("pre_norm_g", "post_norm_g", "w_in", "conv_dw", "conv_dw_b", "conv_ln_g", "conv_ln_b", "w_conv_out",
         "rel_bias", "w_attn_out", "pool_w", "pool_b", "pool_scale", "w_pool_out", "w_out")


def _pack_small_params(p):
    return jnp.concatenate([
        _pack_small(_vec_rows([p["conv_dw_b"][l], p["conv_ln_g"][l], p["conv_ln_b"][l]]), jnp.zeros((32, WC), f32),
                    _vec_rows([p["pool_b"][l].reshape(WC), p["pool_scale"][l]]), p["pool_w"][l],
                    p["pre_norm_g"][l], p["post_norm_g"][l], p["rel_bias"][l])
        for l in range(DEPTH)], axis=0)


def _unpack_small_params(packed):
    layers = [_unpack_small(packed[l * SMALL_ROWS:(l + 1) * SMALL_ROWS]) for l in range(DEPTH)]
    return {k: jnp.stack([layers[l][k] for l in range(DEPTH)]) for k in layers[0]}


def kernel(x, pre_norm_g, post_norm_g, w_in, conv_dw, conv_dw_b, conv_ln_g, conv_ln_b, w_conv_out, rel_bias, w_attn_out, pool_w, pool_b, pool_scale, w_pool_out, w_out, loss_target, m_pre_norm_g, m_post_norm_g, m_w_in, m_conv_dw, m_conv_dw_b, m_conv_ln_g, m_conv_ln_b, m_w_conv_out, m_rel_bias, m_w_attn_out, m_pool_w, m_pool_b, m_pool_scale, m_w_pool_out, m_w_out, v_pre_norm_g, v_post_norm_g, v_w_in, v_conv_dw, v_conv_dw_b, v_conv_ln_g, v_conv_ln_b, v_w_conv_out, v_rel_bias, v_w_attn_out, v_pool_w, v_pool_b, v_pool_scale, v_w_pool_out, v_w_out):
    BL, SEQ, _ = x.shape
    T = BL * SEQ
    w = dict(pre_norm_g=pre_norm_g, post_norm_g=post_norm_g, w_in=w_in, conv_dw=conv_dw, conv_dw_b=conv_dw_b,
             conv_ln_g=conv_ln_g, conv_ln_b=conv_ln_b, w_conv_out=w_conv_out, rel_bias=rel_bias, w_attn_out=w_attn_out,
             pool_w=pool_w, pool_b=pool_b, pool_scale=pool_scale, w_pool_out=w_pool_out, w_out=w_out)
    m = dict(pre_norm_g=m_pre_norm_g, post_norm_g=m_post_norm_g, w_in=m_w_in, conv_dw=m_conv_dw, conv_dw_b=m_conv_dw_b,
             conv_ln_g=m_conv_ln_g, conv_ln_b=m_conv_ln_b, w_conv_out=m_w_conv_out, rel_bias=m_rel_bias,
             w_attn_out=m_w_attn_out, pool_w=m_pool_w, pool_b=m_pool_b, pool_scale=m_pool_scale,
             w_pool_out=m_w_pool_out, w_out=m_w_out)
    v = dict(pre_norm_g=v_pre_norm_g, post_norm_g=v_post_norm_g, w_in=v_w_in, conv_dw=v_conv_dw, conv_dw_b=v_conv_dw_b,
             conv_ln_g=v_conv_ln_g, conv_ln_b=v_conv_ln_b, w_conv_out=v_w_conv_out, rel_bias=v_rel_bias,
             w_attn_out=v_w_attn_out, pool_w=v_pool_w, pool_b=v_pool_b, pool_scale=v_pool_scale,
             w_pool_out=v_w_pool_out, w_out=v_w_out)

    shards = [_layer_shards(w, l) for l in range(DEPTH)]
    x2 = x.reshape(T, D)
    h0, ht0 = _rms_pre(x2, pre_norm_g[0][None])
    first = _gather_begin("w_in0", shards[0][:1], 1, x2)
    bms = [_bias_matrix(rel_bias[l]) for l in range(DEPTH)]
    packs = [_pack_small_params(p) for p in (w, m, v)]
    w_in0 = _side_by_side(_gather_end(first, shards[0][:1], 1, [ht0, *bms, *packs])[0])
    rest0 = _gather_begin("rest0", shards[0][1:], 4, w_in0)
    all1 = _gather_begin("layer1", shards[1], 5, rest0["token"])
    z0 = _in_proj(h0, w_in0, after=all1["token"])
    lw0 = _layer_weights(w_in0, _gather_end(rest0, shards[0][1:], 4, z0), w, 0, bms[0])
    (out0, h1, ht1), saved0 = _layer_fwd(x2, ht0, z0, lw0, BL, SEQ, next_g=pre_norm_g[1][None])
    gathered1 = _gather_end(all1, shards[1], 5, out0)
    lw1 = _layer_weights(_side_by_side(gathered1[0]), gathered1[1:], w, 1, bms[1])
    z1 = _in_proj(h1, lw1["w_in"], after=h1)
    (dout, sq), saved1 = _layer_fwd(out0, ht1, z1, lw1, BL, SEQ, tgt=loss_target.reshape(T, D))
    loss = lax.psum(0.5 * jnp.sum(sq) / float(D), ("x", "y", "c"))

    summed = [None] * DEPTH
    dx1, reduction1, small_gather1, dpre1 = _layer_bwd(dout, saved1, lw1, BL, SEQ)

    def finish_layer1(after):
        summed[1] = _reduce_end(reduction1, after)

    grad_x, reduction0, small_gather0, dpre0 = _layer_bwd(dx1, saved0, lw0, BL, SEQ, meanwhile=finish_layer1)
    summed[0] = _reduce_end(reduction0, grad_x)
    dpre = _sum_slots("sum_small", _gather_all(jnp.concatenate([_rows8(dpre0[0]), _rows8(dpre1[0])], axis=0)))
    gsmall = []
    for l, started in enumerate((small_gather0, small_gather1)):
        g = _sum_slots("sum_small", _gather_small_end(started, dpre))
        gsmall += [g[:PRE_ROWS.start], dpre[8 * l:8 * l + 8].reshape(16, WC), g[PRE_ROWS.stop:]]
    gsmall = jnp.concatenate(gsmall, axis=0)

    grads, deltas, new_m, new_v = {}, {}, {}, {}
    for i, k in enumerate(BIG):
        if k == "w_in":
            g = jnp.stack([summed[l][0] for l in range(DEPTH)])
        elif k == "w_out":
            g = jnp.stack([summed[l][1][3 * WC:].T for l in range(DEPTH)])
        else:
            g = jnp.stack([summed[l][1][(i - 1) * WC:i * WC] for l in range(DEPTH)])
        grads[k] = g
        shape = w[k].shape
        flat2 = lambda a: a.reshape(shape[0] * shape[1], shape[2])
        d_, nm_, nv_ = _adamw("adamw_big", flat2(g), flat2(w[k]), flat2(m[k]), flat2(v[k]))
        deltas[k], new_m[k], new_v[k] = d_.reshape(shape), nm_.reshape(shape), nv_.reshape(shape)

    d_, nm_, nv_ = _adamw("adamw_small", gsmall, *packs)
    gs, ds, ms, vs = (_unpack_small_params(a) for a in (gsmall, d_, nm_, nv_))
    for k in SMALL:
        grads[k], deltas[k], new_m[k], new_v[k] = gs[k], ds[k], ms[k], vs[k]
    chip = 2 * lax.axis_index("x") + lax.axis_index("y")
    g_dw = lax.dynamic_slice_in_dim(gs["conv_dw"], chip * GD, GD, axis=2)
    flat2 = lambda a: a.reshape(DEPTH * CONV_K, GD)
    d_, nm_, nv_ = _adamw("adamw_conv_dw", flat2(g_dw), flat2(conv_dw), flat2(m["conv_dw"]), flat2(v["conv_dw"]))
    grads["conv_dw"] = g_dw
    deltas["conv_dw"], new_m["conv_dw"], new_v["conv_dw"] = (a.reshape(conv_dw.shape) for a in (d_, nm_, nv_))

    return (loss, grad_x.reshape(x.shape), *[grads[k] for k in ORDER], *[deltas[k] for k in ORDER],
            *[new_m[k] for k in ORDER], *[new_v[k] for k in ORDER])
```
